```python
import jax
import jax.numpy as jnp
from jax import lax
import numpy as np

D_MODEL = 1024
BATCH = 32
SEQ = 2048
DEPTH = 4

CTX_LEN = 256
GRID_W = 64
ROPE_THETA = 10000.0
EPS = 1e-6
NEG_INF = -1e30
F32 = jnp.float32
ATTN_BLOCK = 128

MLA_HEADS = 4
MLA_Q_LORA = 256
MLA_KV_LORA = 256
MLA_NOPE = 128
MLA_ROPE = 64
MLA_V = 128
CMLP_GROUPS = 4
CMLP_GROUP_DIM = 128
CMLP_CHUNK = 128
CMLP_WIDTH = CMLP_GROUPS * CMLP_GROUP_DIM
RET_HEADS = 4
RET_QK = 64
RET_V = 128
RET_CHUNK = 128
RET_QK_W = RET_HEADS * RET_QK
RET_V_W = RET_HEADS * RET_V
SWA_Q_HEADS = 8
SWA_KV_HEADS = 2
SWA_HEAD_DIM = 64
SWA_WINDOW = 128
SWA_BLOCK = 128
SWA_Q_W = SWA_Q_HEADS * SWA_HEAD_DIM
SWA_KV_W = SWA_KV_HEADS * SWA_HEAD_DIM
FFN_HIDDEN = -(-8 * D_MODEL // (3 * 256)) * 256

AB_KV_WIDTH = MLA_KV_LORA + MLA_ROPE
AB_IN = AB_KV_WIDTH + MLA_Q_LORA + 2 * CMLP_WIDTH
AB_OUT = MLA_HEADS * MLA_V + CMLP_WIDTH
CD_KV_WIDTH = RET_QK_W + RET_V_W + 2 * SWA_KV_W
CD_IN = CD_KV_WIDTH + RET_QK_W + RET_V_W + SWA_Q_W
CD_OUT = RET_V_W + SWA_Q_W

N_EVEN = (DEPTH + 1) // 2
N_ODD = DEPTH // 2

kernel_name = 'hybrid_mla_gmlp_retention_swa_dit_trunk'


def _rms(x, g):
    xf = x.astype(F32)
    y = xf * lax.rsqrt(jnp.mean(xf * xf, axis=-1, keepdims=True) + EPS)
    return (y * g.astype(F32)).astype(x.dtype)


def _group_rms(y, g, groups):
    shp = y.shape
    yf = y.astype(F32).reshape(shp[:-1] + (groups, shp[-1] // groups))
    yf = yf * lax.rsqrt(jnp.mean(yf * yf, axis=-1, keepdims=True) + EPS)
    return (yf.reshape(shp) * g.astype(F32)).astype(y.dtype)


def _modulate(x, shift, scale):
    return x * (1.0 + scale) + shift


def _split_cols(z, widths):
    return jnp.split(z, np.cumsum(widths)[:-1].tolist(), axis=-1)


def _heads(a, h):
    return a.reshape(a.shape[:2] + (h, a.shape[-1] // h))


def _flip(a):
    return jnp.flip(a, axis=1)


def _axial_rope(n, rot_dim):
    t = jnp.arange(n)
    row = (t // GRID_W).astype(F32)
    col = (t % GRID_W).astype(F32)
    n_freq = rot_dim // 4
    freqs = ROPE_THETA ** (-jnp.arange(n_freq, dtype=F32) / n_freq)
    ang = jnp.concatenate([row[:, None] * freqs, col[:, None] * freqs], axis=-1)
    return jnp.cos(ang), jnp.sin(ang)


def _apply_rope(x, rope):
    cos, sin = rope
    cos = cos[None, :, None, :].astype(x.dtype)
    sin = sin[None, :, None, :].astype(x.dtype)
    x1, x2 = jnp.split(x, 2, axis=-1)
    return jnp.concatenate([x1 * cos - x2 * sin, x1 * sin + x2 * cos], axis=-1)


def _to_blocks(a, blk):
    B, n = a.shape[:2]
    return a.reshape((B, n // blk, blk) + a.shape[2:]).swapaxes(0, 1)


def _from_blocks(a):
    nb, B, blk = a.shape[:3]
    return a.swapaxes(0, 1).reshape((B, nb * blk) + a.shape[3:])


def _attend_block(q, k, v, scale, mask=None, sink=None):
    s = jnp.einsum('bqhgd,bkhd->bhgqk', q, k, preferred_element_type=F32) * scale
    if mask is not None:
        s = jnp.where(mask, s, NEG_INF)
    if sink is not None:
        sink_col = jnp.broadcast_to(sink.astype(F32)[None, :, :, None, None], s.shape[:-1] + (1,))
        p = jax.nn.softmax(jnp.concatenate([s, sink_col], axis=-1), axis=-1)[..., :-1]
    else:
        p = jax.nn.softmax(s, axis=-1)
    return jnp.einsum('bhgqk,bkhe->bqhge', p.astype(v.dtype), v)


def _dense_attention(q, k, v, scale, sink=None):
    out = lax.map(lambda qb: _attend_block(qb, k, v, scale, sink=sink), _to_blocks(q, ATTN_BLOCK))
    return _from_blocks(out)


def _window_attention(q, k, v, k_ctx, v_ctx, scale, sink):
    n = q.shape[1]
    blk = SWA_BLOCK
    nb = n // blk
    pad = ((0, 0), (blk, blk), (0, 0), (0, 0))
    k_pad = jnp.pad(k, pad)
    v_pad = jnp.pad(v, pad)
    rel_k = jnp.arange(3 * blk) - blk
    band = jnp.abs(rel_k[None, :] - jnp.arange(blk)[:, None]) <= SWA_WINDOW
    ctx_mask = jnp.ones((blk, k_ctx.shape[1]), dtype=bool)

    def one_block(args):
        i, qb = args
        kpos = i * blk + rel_k
        local = band & ((kpos >= 0) & (kpos < n))[None, :]
        kb = lax.dynamic_slice_in_dim(k_pad, i * blk, 3 * blk, axis=1)
        vb = lax.dynamic_slice_in_dim(v_pad, i * blk, 3 * blk, axis=1)
        return _attend_block(qb, jnp.concatenate([kb, k_ctx], axis=1), jnp.concatenate([vb, v_ctx], axis=1),
                             scale, jnp.concatenate([local, ctx_mask], axis=1), sink)

    out = lax.map(one_block, (jnp.arange(nb), _to_blocks(q, blk)))
    return _from_blocks(out)


def _retention_chunkwise(q, k, v, log_gamma, s0):
    C = RET_CHUNK
    idx = jnp.arange(C, dtype=F32)
    diff = idx[:, None] - idx[None, :]
    intra = jnp.where(diff[None] >= 0, jnp.exp(log_gamma[:, None, None] * jnp.maximum(diff, 0.0)[None]), 0.0)
    q_dec = jnp.exp((idx[:, None] + 1.0) * log_gamma[None, :])[None, :, :, None]
    k_dec = jnp.exp((C - 1.0 - idx[:, None]) * log_gamma[None, :])[None, :, :, None]
    c_dec = jnp.exp(C * log_gamma)[None, :, None, None]

    def step(s, inp):
        qc, kc, vc = inp
        vc = vc.astype(F32)
        att = jnp.einsum('bqhd,bkhd->bhqk', qc, kc, preferred_element_type=F32) * intra[None]
        y = jnp.einsum('bhqk,bkhe->bqhe', att, vc) + jnp.einsum('bqhd,bhde->bqhe', qc.astype(F32), s) * q_dec
        s = s * c_dec + jnp.einsum('bkhd,bkhe->bhde', kc.astype(F32) * k_dec, vc)
        return s, y

    s, ys = lax.scan(step, s0, (_to_blocks(q, C), _to_blocks(k, C), _to_blocks(v, C)))
    return _from_blocks(ys), s


def _retention_state(k, v, log_gamma):
    n = k.shape[1]
    w = jnp.exp((n - 1.0 - jnp.arange(n, dtype=F32))[:, None] * log_gamma[None, :])
    return jnp.einsum('bjhd,bjhe,jh->bhde', k.astype(F32), v.astype(F32), w)


def _mixer_ab(xn, hn, w_in, w_out, q_norm, kv_norm, wq_b, wkv_b, v_norm, w_s, b_s, with_ctx_out):
    n = xn.shape[1]
    rope = _axial_rope(n, MLA_ROPE)
    scale = (MLA_NOPE + MLA_ROPE) ** -0.5

    def kv_side(z, rope_):
        kv_lat, k_pe = _split_cols(z[..., :AB_KV_WIDTH], [MLA_KV_LORA, MLA_ROPE])
        kv = _heads(_rms(kv_lat, kv_norm) @ wkv_b, MLA_HEADS)
        k_nope, v = jnp.split(kv, [MLA_NOPE], axis=-1)
        k_pe = k_pe[:, :, None, :]
        if rope_ is not None:
            k_pe = _apply_rope(k_pe, rope_)
        k = jnp.concatenate([k_nope, jnp.broadcast_to(k_pe, k_nope.shape[:3] + (MLA_ROPE,))], axis=-1)
        return k, v

    def q_side(z, rope_):
        q_lat = z[..., AB_KV_WIDTH:AB_KV_WIDTH + MLA_Q_LORA]
        q = _heads(_rms(q_lat, q_norm) @ wq_b, MLA_HEADS)
        if rope_ is not None:
            q = jnp.concatenate([q[..., :MLA_NOPE], _apply_rope(q[..., MLA_NOPE:], rope_)], axis=-1)
        return q[:, :, :, None, :]

    def chunk_mlp(z):
        u, v = jnp.split(jax.nn.gelu(z[..., AB_KV_WIDTH + MLA_Q_LORA:]), 2, axis=-1)
        v = _group_rms(v, v_norm, CMLP_GROUPS)
        B, m = v.shape[:2]
        v = v.reshape(B, m // CMLP_CHUNK, CMLP_CHUNK, CMLP_GROUPS, CMLP_GROUP_DIM)
        v = jnp.einsum('gpq,bcqgd->bcpgd', w_s, v) + b_s.T[None, None, :, :, None]
        return u * v.reshape(B, m, CMLP_WIDTH)

    def merge(o, z):
        o = o.reshape(o.shape[:2] + (MLA_HEADS * MLA_V,))
        return jnp.concatenate([o, chunk_mlp(z)], axis=-1) @ w_out

    zx = xn @ w_in
    zh = hn @ (w_in if with_ctx_out else w_in[:, :AB_KV_WIDTH])
    kx, vx = kv_side(zx, rope)
    kh, vh = kv_side(zh, None)
    ox = _dense_attention(q_side(zx, rope), jnp.concatenate([kx, kh], axis=1),
                          jnp.concatenate([vx, vh], axis=1), scale)
    yx = merge(ox, zx)
    yh = None
    if with_ctx_out:
        yh = merge(_dense_attention(q_side(zh, None), kh, vh, scale), zh)
    return yx, yh


def _mixer_cd(xn, hn, w_in, w_out, dec_f, dec_b, ret_norm, sink, with_ctx_out):
    B, n = xn.shape[:2]
    rope_ret = _axial_rope(n, RET_QK)
    rope_swa = _axial_rope(n, SWA_HEAD_DIM)
    lg_f = jax.nn.log_sigmoid(dec_f.astype(F32))
    lg_b = jax.nn.log_sigmoid(dec_b.astype(F32))
    swa_scale = SWA_HEAD_DIM ** -0.5
    groups = SWA_Q_HEADS // SWA_KV_HEADS
    sink_g = sink.reshape(SWA_KV_HEADS, groups)

    def kv_side(z, use_rope):
        rk, rv, sk, sv = _split_cols(z[..., :CD_KV_WIDTH], [RET_QK_W, RET_V_W, SWA_KV_W, SWA_KV_W])
        rk = _heads(rk, RET_HEADS) * (RET_QK ** -0.5)
        sk = _heads(sk, SWA_KV_HEADS)
        if use_rope:
            rk = _apply_rope(rk, rope_ret)
            sk = _apply_rope(sk, rope_swa)
        return rk, _heads(rv, RET_HEADS), sk, _heads(sv, SWA_KV_HEADS)

    def q_side(z, use_rope):
        rq, rg, sq = _split_cols(z[..., CD_KV_WIDTH:], [RET_QK_W, RET_V_W, SWA_Q_W])
        rq = _heads(rq, RET_HEADS)
        sq = _heads(sq, SWA_Q_HEADS)
        if use_rope:
            rq = _apply_rope(rq, rope_ret)
            sq = _apply_rope(sq, rope_swa)
        return rq, rg, sq.reshape(sq.shape[:2] + (SWA_KV_HEADS, groups, SWA_HEAD_DIM))

    def merge(y_ret, gate, o_swa, dtype):
        y_ret = _group_rms(y_ret.reshape(y_ret.shape[:2] + (RET_V_W,)), ret_norm, RET_HEADS).astype(dtype)
        y_ret = y_ret * jax.nn.silu(gate)
        return jnp.concatenate([y_ret, o_swa.reshape(o_swa.shape[:2] + (SWA_Q_W,))], axis=-1) @ w_out

    zx = xn @ w_in
    zh = hn @ (w_in if with_ctx_out else w_in[:, :CD_KV_WIDTH])
    rkh, rvh, skh, svh = kv_side(zh, False)
    yh = None
    if with_ctx_out:
        rqh, rgh, sqh = q_side(zh, False)
        s0 = jnp.zeros((B, RET_HEADS, RET_QK, RET_V), F32)
        yh_f, s_f = _retention_chunkwise(rqh, rkh, rvh, lg_f, s0)
        yh_b, s_b = _retention_chunkwise(_flip(rqh), _flip(rkh), _flip(rvh), lg_b, s0)
        yh = merge(yh_f + _flip(yh_b), rgh, _dense_attention(sqh, skh, svh, swa_scale, sink_g), hn.dtype)
    else:
        s_f = _retention_state(rkh, rvh, lg_f)
        s_b = _retention_state(_flip(rkh), _flip(rvh), lg_b)
    rkx, rvx, skx, svx = kv_side(zx, True)
    rqx, rgx, sqx = q_side(zx, True)
    yx_f, _ = _retention_chunkwise(rqx, rkx, rvx, lg_f, s_f)
    yx_b, _ = _retention_chunkwise(_flip(rqx), _flip(rkx), _flip(rvx), lg_b, s_b)
    ox = _window_attention(sqx, skx, svx, skh, svh, swa_scale, sink_g)
    yx = merge(yx_f + _flip(yx_b), rgx, ox, xn.dtype)
    return yx, yh


def _swiglu(x, w_in, w_out):
    a, b = jnp.split(x @ w_in, 2, axis=-1)
    return (jax.nn.silu(a) * b) @ w_out


def _fwd_setup_inputs(seed: int = 0) -> dict:
    key = jax.random.key(seed)
    ks = iter(jax.random.split(key, 32))
    D = D_MODEL

    def nrm(shape, s):
        return jax.random.normal(next(ks), shape, F32) * s

    decay_logit = jnp.asarray(np.log(2.0 ** (5 + np.arange(RET_HEADS)) - 1.0), dtype=F32)
    return dict(
        x=nrm((BATCH, SEQ, D), 1.0),
        c=nrm((BATCH, D), 1.0),
        ctx=nrm((BATCH, CTX_LEN, D), 1.0),
        c_ctx=nrm((D,), 1.0),
        ada_w=nrm((DEPTH, D, 6 * D), 0.5 * D ** -0.5),
        ada_b=nrm((DEPTH, 6 * D), 0.02),
        norm_mix=1.0 + nrm((DEPTH, D), 0.02),
        norm_ffn=1.0 + nrm((DEPTH, D), 0.02),
        norm_final=1.0 + nrm((D,), 0.02),
        ffn_in=nrm((DEPTH, D, 2 * FFN_HIDDEN), D ** -0.5),
        ffn_out=nrm((DEPTH, FFN_HIDDEN, D), FFN_HIDDEN ** -0.5),
        ab_in=nrm((N_EVEN, D, AB_IN), D ** -0.5),
        ab_out=nrm((N_EVEN, AB_OUT, D), AB_OUT ** -0.5),
        mla_q_norm=1.0 + nrm((N_EVEN, MLA_Q_LORA), 0.02),
        mla_kv_norm=1.0 + nrm((N_EVEN, MLA_KV_LORA), 0.02),
        mla_wq_b=nrm((N_EVEN, MLA_Q_LORA, MLA_HEADS * (MLA_NOPE + MLA_ROPE)), MLA_Q_LORA ** -0.5),
        mla_wkv_b=nrm((N_EVEN, MLA_KV_LORA, MLA_HEADS * (MLA_NOPE + MLA_V)), MLA_KV_LORA ** -0.5),
        cmlp_v_norm=1.0 + nrm((N_EVEN, CMLP_WIDTH), 0.02),
        cmlp_ws=nrm((N_EVEN, CMLP_GROUPS, CMLP_CHUNK, CMLP_CHUNK), CMLP_CHUNK ** -0.5),
        cmlp_bs=1.0 + nrm((N_EVEN, CMLP_GROUPS, CMLP_CHUNK), 0.02),
        cd_in=nrm((N_ODD, D, CD_IN), D ** -0.5),
        cd_out=nrm((N_ODD, CD_OUT, D), CD_OUT ** -0.5),
        ret_decay_fwd=decay_logit + nrm((N_ODD, RET_HEADS), 0.05),
        ret_decay_bwd=decay_logit + nrm((N_ODD, RET_HEADS), 0.05),
        ret_norm=1.0 + nrm((N_ODD, RET_V_W), 0.02),
        swa_sink=nrm((N_ODD, SWA_Q_HEADS), 0.5),
    )


def _fwd_reference(x, c, ctx, c_ctx, ada_w, ada_b, norm_mix, norm_ffn, norm_final, ffn_in, ffn_out,
              ab_in, ab_out, mla_q_norm, mla_kv_norm, mla_wq_b, mla_wkv_b, cmlp_v_norm, cmlp_ws, cmlp_bs,
              cd_in, cd_out, ret_decay_fwd, ret_decay_bwd, ret_norm, swa_sink):
    h = ctx
    cond = jax.nn.silu(c)
    cond_ctx = jax.nn.silu(c_ctx)
    for layer in range(DEPTH):
        last = layer == DEPTH - 1
        mx = [m[:, None, :] for m in jnp.split(cond @ ada_w[layer] + ada_b[layer], 6, axis=-1)]
        mh = jnp.split(cond_ctx @ ada_w[layer] + ada_b[layer], 6, axis=-1)
        xn = _modulate(_rms(x, norm_mix[layer]), mx[0], mx[1])
        hn = _modulate(_rms(h, norm_mix[layer]), mh[0], mh[1])
        j = layer // 2
        if layer % 2 == 0:
            yx, yh = _mixer_ab(xn, hn, ab_in[j], ab_out[j], mla_q_norm[j], mla_kv_norm[j], mla_wq_b[j],
                               mla_wkv_b[j], cmlp_v_norm[j], cmlp_ws[j], cmlp_bs[j], not last)
        else:
            yx, yh = _mixer_cd(xn, hn, cd_in[j], cd_out[j], ret_decay_fwd[j], ret_decay_bwd[j],
                               ret_norm[j], swa_sink[j], not last)
        x = x + mx[2] * yx
        x = x + mx[5] * _swiglu(_modulate(_rms(x, norm_ffn[layer]), mx[3], mx[4]), ffn_in[layer], ffn_out[layer])
        if not last:
            h = h + mh[2] * yh
            h = h + mh[5] * _swiglu(_modulate(_rms(h, norm_ffn[layer]), mh[3], mh[4]), ffn_in[layer], ffn_out[layer])
    return _rms(x, norm_final)


import jax as _jax
import jax.numpy as _jnp

TWIN_FORMAT = 'train_step'
FWD_PARAMS = ['x', 'c', 'ctx', 'c_ctx', 'ada_w', 'ada_b', 'norm_mix', 'norm_ffn', 'norm_final', 'ffn_in', 'ffn_out', 'ab_in', 'ab_out', 'mla_q_norm', 'mla_kv_norm', 'mla_wq_b', 'mla_wkv_b', 'cmlp_v_norm', 'cmlp_ws', 'cmlp_bs', 'cd_in', 'cd_out', 'ret_decay_fwd', 'ret_decay_bwd', 'ret_norm', 'swa_sink']
TWIN_WEIGHTS = ['c_ctx', 'ada_w', 'ada_b', 'norm_mix', 'norm_ffn', 'norm_final', 'ffn_in', 'ffn_out', 'ab_in', 'ab_out', 'mla_q_norm', 'mla_kv_norm', 'mla_wq_b', 'mla_wkv_b', 'cmlp_v_norm', 'cmlp_ws', 'cmlp_bs', 'cd_in', 'cd_out', 'ret_decay_fwd', 'ret_decay_bwd', 'ret_norm', 'swa_sink']
TWIN_DIFF_INPUT = 'x'
TWIN_INPUTS = ['x', 'c', 'ctx', 'c_ctx', 'ada_w', 'ada_b', 'norm_mix', 'norm_ffn', 'norm_final', 'ffn_in', 'ffn_out', 'ab_in', 'ab_out', 'mla_q_norm', 'mla_kv_norm', 'mla_wq_b', 'mla_wkv_b', 'cmlp_v_norm', 'cmlp_ws', 'cmlp_bs', 'cd_in', 'cd_out', 'ret_decay_fwd', 'ret_decay_bwd', 'ret_norm', 'swa_sink', 'loss_target', 'm_c_ctx', 'm_ada_w', 'm_ada_b', 'm_norm_mix', 'm_norm_ffn', 'm_norm_final', 'm_ffn_in', 'm_ffn_out', 'm_ab_in', 'm_ab_out', 'm_mla_q_norm', 'm_mla_kv_norm', 'm_mla_wq_b', 'm_mla_wkv_b', 'm_cmlp_v_norm', 'm_cmlp_ws', 'm_cmlp_bs', 'm_cd_in', 'm_cd_out', 'm_ret_decay_fwd', 'm_ret_decay_bwd', 'm_ret_norm', 'm_swa_sink', 'v_c_ctx', 'v_ada_w', 'v_ada_b', 'v_norm_mix', 'v_norm_ffn', 'v_norm_final', 'v_ffn_in', 'v_ffn_out', 'v_ab_in', 'v_ab_out', 'v_mla_q_norm', 'v_mla_kv_norm', 'v_mla_wq_b', 'v_mla_wkv_b', 'v_cmlp_v_norm', 'v_cmlp_ws', 'v_cmlp_bs', 'v_cd_in', 'v_cd_out', 'v_ret_decay_fwd', 'v_ret_decay_bwd', 'v_ret_norm', 'v_swa_sink']
TWIN_OUTPUTS = ['loss', 'grad_x', 'grad_c_ctx', 'grad_ada_w', 'grad_ada_b', 'grad_norm_mix', 'grad_norm_ffn', 'grad_norm_final', 'grad_ffn_in', 'grad_ffn_out', 'grad_ab_in', 'grad_ab_out', 'grad_mla_q_norm', 'grad_mla_kv_norm', 'grad_mla_wq_b', 'grad_mla_wkv_b', 'grad_cmlp_v_norm', 'grad_cmlp_ws', 'grad_cmlp_bs', 'grad_cd_in', 'grad_cd_out', 'grad_ret_decay_fwd', 'grad_ret_decay_bwd', 'grad_ret_norm', 'grad_swa_sink', 'delta_c_ctx', 'delta_ada_w', 'delta_ada_b', 'delta_norm_mix', 'delta_norm_ffn', 'delta_norm_final', 'delta_ffn_in', 'delta_ffn_out', 'delta_ab_in', 'delta_ab_out', 'delta_mla_q_norm', 'delta_mla_kv_norm', 'delta_mla_wq_b', 'delta_mla_wkv_b', 'delta_cmlp_v_norm', 'delta_cmlp_ws', 'delta_cmlp_bs', 'delta_cd_in', 'delta_cd_out', 'delta_ret_decay_fwd', 'delta_ret_decay_bwd', 'delta_ret_norm', 'delta_swa_sink', 'new_m_c_ctx', 'new_m_ada_w', 'new_m_ada_b', 'new_m_norm_mix', 'new_m_norm_ffn', 'new_m_norm_final', 'new_m_ffn_in', 'new_m_ffn_out', 'new_m_ab_in', 'new_m_ab_out', 'new_m_mla_q_norm', 'new_m_mla_kv_norm', 'new_m_mla_wq_b', 'new_m_mla_wkv_b', 'new_m_cmlp_v_norm', 'new_m_cmlp_ws', 'new_m_cmlp_bs', 'new_m_cd_in', 'new_m_cd_out', 'new_m_ret_decay_fwd', 'new_m_ret_decay_bwd', 'new_m_ret_norm', 'new_m_swa_sink', 'new_v_c_ctx', 'new_v_ada_w', 'new_v_ada_b', 'new_v_norm_mix', 'new_v_norm_ffn', 'new_v_norm_final', 'new_v_ffn_in', 'new_v_ffn_out', 'new_v_ab_in', 'new_v_ab_out', 'new_v_mla_q_norm', 'new_v_mla_kv_norm', 'new_v_mla_wq_b', 'new_v_mla_wkv_b', 'new_v_cmlp_v_norm', 'new_v_cmlp_ws', 'new_v_cmlp_bs', 'new_v_cd_in', 'new_v_cd_out', 'new_v_ret_decay_fwd', 'new_v_ret_decay_bwd', 'new_v_ret_norm', 'new_v_swa_sink']
TWIN_LEAF_KINDS = {'loss': 'loss', 'grad_x': 'grad_x', 'grad_c_ctx': 'grad_w', 'grad_ada_w': 'grad_w', 'grad_ada_b': 'grad_w', 'grad_norm_mix': 'grad_w', 'grad_norm_ffn': 'grad_w', 'grad_norm_final': 'grad_w', 'grad_ffn_in': 'grad_w', 'grad_ffn_out': 'grad_w', 'grad_ab_in': 'grad_w', 'grad_ab_out': 'grad_w', 'grad_mla_q_norm': 'grad_w', 'grad_mla_kv_norm': 'grad_w', 'grad_mla_wq_b': 'grad_w', 'grad_mla_wkv_b': 'grad_w', 'grad_cmlp_v_norm': 'grad_w', 'grad_cmlp_ws': 'grad_w', 'grad_cmlp_bs': 'grad_w', 'grad_cd_in': 'grad_w', 'grad_cd_out': 'grad_w', 'grad_ret_decay_fwd': 'grad_w', 'grad_ret_decay_bwd': 'grad_w', 'grad_ret_norm': 'grad_w', 'grad_swa_sink': 'grad_w', 'delta_c_ctx': 'delta_w', 'delta_ada_w': 'delta_w', 'delta_ada_b': 'delta_w', 'delta_norm_mix': 'delta_w', 'delta_norm_ffn': 'delta_w', 'delta_norm_final': 'delta_w', 'delta_ffn_in': 'delta_w', 'delta_ffn_out': 'delta_w', 'delta_ab_in': 'delta_w', 'delta_ab_out': 'delta_w', 'delta_mla_q_norm': 'delta_w', 'delta_mla_kv_norm': 'delta_w', 'delta_mla_wq_b': 'delta_w', 'delta_mla_wkv_b': 'delta_w', 'delta_cmlp_v_norm': 'delta_w', 'delta_cmlp_ws': 'delta_w', 'delta_cmlp_bs': 'delta_w', 'delta_cd_in': 'delta_w', 'delta_cd_out': 'delta_w', 'delta_ret_decay_fwd': 'delta_w', 'delta_ret_decay_bwd': 'delta_w', 'delta_ret_norm': 'delta_w', 'delta_swa_sink': 'delta_w', 'new_m_c_ctx': 'new_m', 'new_m_ada_w': 'new_m', 'new_m_ada_b': 'new_m', 'new_m_norm_mix': 'new_m', 'new_m_norm_ffn': 'new_m', 'new_m_norm_final': 'new_m', 'new_m_ffn_in': 'new_m', 'new_m_ffn_out': 'new_m', 'new_m_ab_in': 'new_m', 'new_m_ab_out': 'new_m', 'new_m_mla_q_norm': 'new_m', 'new_m_mla_kv_norm': 'new_m', 'new_m_mla_wq_b': 'new_m', 'new_m_mla_wkv_b': 'new_m', 'new_m_cmlp_v_norm': 'new_m', 'new_m_cmlp_ws': 'new_m', 'new_m_cmlp_bs': 'new_m', 'new_m_cd_in': 'new_m', 'new_m_cd_out': 'new_m', 'new_m_ret_decay_fwd': 'new_m', 'new_m_ret_decay_bwd': 'new_m', 'new_m_ret_norm': 'new_m', 'new_m_swa_sink': 'new_m', 'new_v_c_ctx': 'new_v', 'new_v_ada_w': 'new_v', 'new_v_ada_b': 'new_v', 'new_v_norm_mix': 'new_v', 'new_v_norm_ffn': 'new_v', 'new_v_norm_final': 'new_v', 'new_v_ffn_in': 'new_v', 'new_v_ffn_out': 'new_v', 'new_v_ab_in': 'new_v', 'new_v_ab_out': 'new_v', 'new_v_mla_q_norm': 'new_v', 'new_v_mla_kv_norm': 'new_v', 'new_v_mla_wq_b': 'new_v', 'new_v_mla_wkv_b': 'new_v', 'new_v_cmlp_v_norm': 'new_v', 'new_v_cmlp_ws': 'new_v', 'new_v_cmlp_bs': 'new_v', 'new_v_cd_in': 'new_v', 'new_v_cd_out': 'new_v', 'new_v_ret_decay_fwd': 'new_v', 'new_v_ret_decay_bwd': 'new_v', 'new_v_ret_norm': 'new_v', 'new_v_swa_sink': 'new_v'}


def _forward(args):
    return _fwd_reference(*[args[k] for k in FWD_PARAMS])


def _output_shape():
    out = _jax.eval_shape(lambda: _forward(_fwd_setup_inputs(0)))
    return out.shape, out.dtype

N_MICROBATCH = 1
ADAM_LR = 0.001
ADAM_B1 = 0.9
ADAM_B2 = 0.999
ADAM_EPS = 1e-08
ADAM_WD = 0.01
ADAM_STEP = 10
PER_EXAMPLE_BATCH_AXIS = {'x': 0, 'c': 0, 'ctx': 0, 'loss_target': 0}
SHARED_INPUTS = []
_WEIGHT_DTYPES = {'c_ctx': _jnp.float32, 'ada_w': _jnp.float32, 'ada_b': _jnp.float32, 'norm_mix': _jnp.float32, 'norm_ffn': _jnp.float32, 'norm_final': _jnp.float32, 'ffn_in': _jnp.float32, 'ffn_out': _jnp.float32, 'ab_in': _jnp.float32, 'ab_out': _jnp.float32, 'mla_q_norm': _jnp.float32, 'mla_kv_norm': _jnp.float32, 'mla_wq_b': _jnp.float32, 'mla_wkv_b': _jnp.float32, 'cmlp_v_norm': _jnp.float32, 'cmlp_ws': _jnp.float32, 'cmlp_bs': _jnp.float32, 'cd_in': _jnp.float32, 'cd_out': _jnp.float32, 'ret_decay_fwd': _jnp.float32, 'ret_decay_bwd': _jnp.float32, 'ret_norm': _jnp.float32, 'swa_sink': _jnp.float32}
MOMENT_SCALE = {'c_ctx': 3.624454e-02, 'ada_w': 7.642674e-02, 'ada_b': 1.235927e-01, 'norm_mix': 6.787001e-02, 'norm_ffn': 7.365174e-02, 'norm_final': 6.410774e+01, 'ffn_in': 3.280478e-02, 'ffn_out': 5.350201e-02, 'ab_in': 5.997319e-02, 'ab_out': 6.177291e-02, 'mla_q_norm': 1.076169e-02, 'mla_kv_norm': 3.444380e-02, 'mla_wq_b': 6.331540e-03, 'mla_wkv_b': 1.861183e-02, 'cmlp_v_norm': 5.868921e-02, 'cmlp_ws': 5.759196e-02, 'cmlp_bs': 6.483866e-02, 'cd_in': 5.337776e-02, 'cd_out': 3.640481e-02, 'ret_decay_fwd': 2.090422e-01, 'ret_decay_bwd': 5.348397e-01, 'ret_norm': 4.815582e-02, 'swa_sink': 5.182539e-04}


def _to_microbatches(a, axis):
    t = _jnp.moveaxis(a, axis, 0)
    t = t.reshape((N_MICROBATCH, t.shape[0] // N_MICROBATCH) + t.shape[1:])
    return _jnp.moveaxis(t, 1, axis + 1)


def setup_inputs(seed: int = 0) -> dict:
    inp = _fwd_setup_inputs(seed)
    key = _jax.random.fold_in(_jax.random.key(seed), 7919)
    shape, _ = _output_shape()
    out = dict(inp)
    out["loss_target"] = _jax.random.normal(_jax.random.fold_in(key, 0), shape, _jnp.float32)
    for i, name in enumerate(TWIN_WEIGHTS):
        w = inp[name].astype(_jnp.float32)
        if MOMENT_SCALE is None:
            s = _jnp.sqrt(_jnp.mean(_jnp.square(w)) + 1e-30)
        else:
            s = MOMENT_SCALE[name]
        km, kv = _jax.random.split(_jax.random.fold_in(key, i + 1))
        out[name] = w
        out["m_" + name] = s * _jax.random.normal(km, w.shape, _jnp.float32)
        out["v_" + name] = (s * s) * _jax.random.uniform(kv, w.shape, _jnp.float32, 0.5, 1.5)
    if N_MICROBATCH > 1:
        for name, axis in PER_EXAMPLE_BATCH_AXIS.items():
            out[name] = _to_microbatches(out[name], axis)
    return {'x': out['x'], 'c': out['c'], 'ctx': out['ctx'], 'c_ctx': out['c_ctx'], 'ada_w': out['ada_w'], 'ada_b': out['ada_b'], 'norm_mix': out['norm_mix'], 'norm_ffn': out['norm_ffn'], 'norm_final': out['norm_final'], 'ffn_in': out['ffn_in'], 'ffn_out': out['ffn_out'], 'ab_in': out['ab_in'], 'ab_out': out['ab_out'], 'mla_q_norm': out['mla_q_norm'], 'mla_kv_norm': out['mla_kv_norm'], 'mla_wq_b': out['mla_wq_b'], 'mla_wkv_b': out['mla_wkv_b'], 'cmlp_v_norm': out['cmlp_v_norm'], 'cmlp_ws': out['cmlp_ws'], 'cmlp_bs': out['cmlp_bs'], 'cd_in': out['cd_in'], 'cd_out': out['cd_out'], 'ret_decay_fwd': out['ret_decay_fwd'], 'ret_decay_bwd': out['ret_decay_bwd'], 'ret_norm': out['ret_norm'], 'swa_sink': out['swa_sink'], 'loss_target': out['loss_target'], 'm_c_ctx': out['m_c_ctx'], 'm_ada_w': out['m_ada_w'], 'm_ada_b': out['m_ada_b'], 'm_norm_mix': out['m_norm_mix'], 'm_norm_ffn': out['m_norm_ffn'], 'm_norm_final': out['m_norm_final'], 'm_ffn_in': out['m_ffn_in'], 'm_ffn_out': out['m_ffn_out'], 'm_ab_in': out['m_ab_in'], 'm_ab_out': out['m_ab_out'], 'm_mla_q_norm': out['m_mla_q_norm'], 'm_mla_kv_norm': out['m_mla_kv_norm'], 'm_mla_wq_b': out['m_mla_wq_b'], 'm_mla_wkv_b': out['m_mla_wkv_b'], 'm_cmlp_v_norm': out['m_cmlp_v_norm'], 'm_cmlp_ws': out['m_cmlp_ws'], 'm_cmlp_bs': out['m_cmlp_bs'], 'm_cd_in': out['m_cd_in'], 'm_cd_out': out['m_cd_out'], 'm_ret_decay_fwd': out['m_ret_decay_fwd'], 'm_ret_decay_bwd': out['m_ret_decay_bwd'], 'm_ret_norm': out['m_ret_norm'], 'm_swa_sink': out['m_swa_sink'], 'v_c_ctx': out['v_c_ctx'], 'v_ada_w': out['v_ada_w'], 'v_ada_b': out['v_ada_b'], 'v_norm_mix': out['v_norm_mix'], 'v_norm_ffn': out['v_norm_ffn'], 'v_norm_final': out['v_norm_final'], 'v_ffn_in': out['v_ffn_in'], 'v_ffn_out': out['v_ffn_out'], 'v_ab_in': out['v_ab_in'], 'v_ab_out': out['v_ab_out'], 'v_mla_q_norm': out['v_mla_q_norm'], 'v_mla_kv_norm': out['v_mla_kv_norm'], 'v_mla_wq_b': out['v_mla_wq_b'], 'v_mla_wkv_b': out['v_mla_wkv_b'], 'v_cmlp_v_norm': out['v_cmlp_v_norm'], 'v_cmlp_ws': out['v_cmlp_ws'], 'v_cmlp_bs': out['v_cmlp_bs'], 'v_cd_in': out['v_cd_in'], 'v_cd_out': out['v_cd_out'], 'v_ret_decay_fwd': out['v_ret_decay_fwd'], 'v_ret_decay_bwd': out['v_ret_decay_bwd'], 'v_ret_norm': out['v_ret_norm'], 'v_swa_sink': out['v_swa_sink']}


def _loss(weights, diff, rest, loss_target):
    with _jax.named_scope("forward"):
        args = {**rest, TWIN_DIFF_INPUT: diff, **{k: w.astype(_WEIGHT_DTYPES[k]) for k, w in weights.items()}}
        y = _forward(args)
    with _jax.named_scope("loss_head"):
        err = _jnp.square(y.astype(_jnp.float32) - loss_target)
        return 0.5 * _jnp.sum(_jnp.mean(err, axis=-1)) if err.ndim else 0.5 * err


def _adamw(w, g, m, v):
    m = ADAM_B1 * m + (1.0 - ADAM_B1) * g
    v = ADAM_B2 * v + (1.0 - ADAM_B2) * _jnp.square(g)
    m_hat = m / (1.0 - ADAM_B1 ** ADAM_STEP)
    v_hat = v / (1.0 - ADAM_B2 ** ADAM_STEP)
    delta = -ADAM_LR * (m_hat / (_jnp.sqrt(v_hat) + ADAM_EPS) + ADAM_WD * w)
    return delta, m, v


def reference(x, c, ctx, c_ctx, ada_w, ada_b, norm_mix, norm_ffn, norm_final, ffn_in, ffn_out, ab_in, ab_out, mla_q_norm, mla_kv_norm, mla_wq_b, mla_wkv_b, cmlp_v_norm, cmlp_ws, cmlp_bs, cd_in, cd_out, ret_decay_fwd, ret_decay_bwd, ret_norm, swa_sink, loss_target, m_c_ctx, m_ada_w, m_ada_b, m_norm_mix, m_norm_ffn, m_norm_final, m_ffn_in, m_ffn_out, m_ab_in, m_ab_out, m_mla_q_norm, m_mla_kv_norm, m_mla_wq_b, m_mla_wkv_b, m_cmlp_v_norm, m_cmlp_ws, m_cmlp_bs, m_cd_in, m_cd_out, m_ret_decay_fwd, m_ret_decay_bwd, m_ret_norm, m_swa_sink, v_c_ctx, v_ada_w, v_ada_b, v_norm_mix, v_norm_ffn, v_norm_final, v_ffn_in, v_ffn_out, v_ab_in, v_ab_out, v_mla_q_norm, v_mla_kv_norm, v_mla_wq_b, v_mla_wkv_b, v_cmlp_v_norm, v_cmlp_ws, v_cmlp_bs, v_cd_in, v_cd_out, v_ret_decay_fwd, v_ret_decay_bwd, v_ret_norm, v_swa_sink):
    given = dict(x=x, c=c, ctx=ctx, c_ctx=c_ctx, ada_w=ada_w, ada_b=ada_b, norm_mix=norm_mix, norm_ffn=norm_ffn, norm_final=norm_final, ffn_in=ffn_in, ffn_out=ffn_out, ab_in=ab_in, ab_out=ab_out, mla_q_norm=mla_q_norm, mla_kv_norm=mla_kv_norm, mla_wq_b=mla_wq_b, mla_wkv_b=mla_wkv_b, cmlp_v_norm=cmlp_v_norm, cmlp_ws=cmlp_ws, cmlp_bs=cmlp_bs, cd_in=cd_in, cd_out=cd_out, ret_decay_fwd=ret_decay_fwd, ret_decay_bwd=ret_decay_bwd, ret_norm=ret_norm, swa_sink=swa_sink, loss_target=loss_target, m_c_ctx=m_c_ctx, m_ada_w=m_ada_w, m_ada_b=m_ada_b, m_norm_mix=m_norm_mix, m_norm_ffn=m_norm_ffn, m_norm_final=m_norm_final, m_ffn_in=m_ffn_in, m_ffn_out=m_ffn_out, m_ab_in=m_ab_in, m_ab_out=m_ab_out, m_mla_q_norm=m_mla_q_norm, m_mla_kv_norm=m_mla_kv_norm, m_mla_wq_b=m_mla_wq_b, m_mla_wkv_b=m_mla_wkv_b, m_cmlp_v_norm=m_cmlp_v_norm, m_cmlp_ws=m_cmlp_ws, m_cmlp_bs=m_cmlp_bs, m_cd_in=m_cd_in, m_cd_out=m_cd_out, m_ret_decay_fwd=m_ret_decay_fwd, m_ret_decay_bwd=m_ret_decay_bwd, m_ret_norm=m_ret_norm, m_swa_sink=m_swa_sink, v_c_ctx=v_c_ctx, v_ada_w=v_ada_w, v_ada_b=v_ada_b, v_norm_mix=v_norm_mix, v_norm_ffn=v_norm_ffn, v_norm_final=v_norm_final, v_ffn_in=v_ffn_in, v_ffn_out=v_ffn_out, v_ab_in=v_ab_in, v_ab_out=v_ab_out, v_mla_q_norm=v_mla_q_norm, v_mla_kv_norm=v_mla_kv_norm, v_mla_wq_b=v_mla_wq_b, v_mla_wkv_b=v_mla_wkv_b, v_cmlp_v_norm=v_cmlp_v_norm, v_cmlp_ws=v_cmlp_ws, v_cmlp_bs=v_cmlp_bs, v_cd_in=v_cd_in, v_cd_out=v_cd_out, v_ret_decay_fwd=v_ret_decay_fwd, v_ret_decay_bwd=v_ret_decay_bwd, v_ret_norm=v_ret_norm, v_swa_sink=v_swa_sink)
    weights = {n: given[n] for n in TWIN_WEIGHTS}
    shared = {n: given[n] for n in SHARED_INPUTS}
    per_example = {n: given[n] for n in ['x', 'c', 'ctx']}
    grad_fn = _jax.value_and_grad(_loss, argnums=(0, 1))

    def one_microbatch(ex, loss_target):
        ex = dict(ex)
        diff = ex.pop(TWIN_DIFF_INPUT)
        return grad_fn(weights, diff, {**shared, **ex}, loss_target)

    if N_MICROBATCH == 1:
        loss, (grad_w, grad_x) = one_microbatch(per_example, given["loss_target"])
    else:
        def body(carry, xs):
            loss_sum, grad_sum = carry
            l_k, (gw_k, gx_k) = one_microbatch(xs[0], xs[1])
            with _jax.named_scope("update"):
                return (loss_sum + l_k, _jax.tree.map(_jnp.add, grad_sum, gw_k)), gx_k

        init = (_jnp.zeros((), _jnp.float32), _jax.tree.map(_jnp.zeros_like, weights))
        (loss, grad_w), grad_x = _jax.lax.scan(body, init, (per_example, given["loss_target"]))
    with _jax.named_scope("update"):
        delta_w, new_m, new_v = {}, {}, {}
        for n in TWIN_WEIGHTS:
            delta_w[n], new_m[n], new_v[n] = _adamw(weights[n], grad_w[n], given["m_" + n], given["v_" + n])
    return (loss, grad_x, *[grad_w[n] for n in TWIN_WEIGHTS], *[delta_w[n] for n in TWIN_WEIGHTS],
            *[new_m[n] for n in TWIN_WEIGHTS], *[new_v[n] for n in TWIN_WEIGHTS])
```

```python
import functools
import math

import jax
import jax.numpy as jnp
import numpy as np
from jax import lax
from jax.experimental import pallas as pl
from jax.experimental.pallas import tpu as pltpu

F32 = jnp.float32
BF16 = jnp.bfloat16
EPS = 1e-6
NEG_INF = -1e30
GRID_W = 64
ROPE_THETA = 10000.0
DEPTH = 4
MLA_HEADS, MLA_Q_LORA, MLA_KV_LORA, MLA_NOPE, MLA_ROPE, MLA_V = 4, 256, 256, 128, 64, 128
CMLP_GROUPS, CMLP_CHUNK = 4, 128
CMLP_WIDTH = 512
RET_HEADS, RET_QK, RET_V = 4, 64, 128
SWA_Q_HEADS, SWA_KV_HEADS, SWA_HEAD_DIM, SWA_WINDOW = 8, 2, 64, 128
SWA_GROUPS = SWA_Q_HEADS // SWA_KV_HEADS
AB_IN_P = 1664
ADAM_LR, ADAM_B1, ADAM_B2, ADAM_EPS, ADAM_WD, ADAM_STEP = 0.001, 0.9, 0.999, 1e-08, 0.01, 10

T = 256
SWA_SPAN = 3 * T
VMEM_LIMIT = 48 * 1024 * 1024
MESH = pl.DeviceIdType.MESH
ANY = pl.BlockSpec(memory_space=pl.ANY)


def _cparams(sem):
    return pltpu.CompilerParams(dimension_semantics=sem, vmem_limit_bytes=VMEM_LIMIT)


@functools.cache
def _bdot_fn(ca, cb):
    fa, fb = 1 - ca, 1 - cb

    def dg(p, q, cp, cq):
        return lax.dot_general(p.astype(BF16), q.astype(BF16), (((cp,), (cq,)), ((), ())), preferred_element_type=F32)

    @jax.custom_vjp
    def bd(a, b):
        return dg(a, b, ca, cb)

    def fwd(a, b):
        return dg(a, b, ca, cb), (a, b)

    def bwd(res, g):
        a, b = res
        da = dg(g, b, 1, fb) if ca == 1 else dg(b, g, fb, 1)
        db = dg(a, g, fa, 0) if cb == 0 else dg(g, a, 0, fa)
        return da, db

    bd.defvjp(fwd, bwd)
    return bd


def bdot(a, b):
    return _bdot_fn(1, 0)(a, b)


def bdot_nt(a, b):
    return _bdot_fn(1, 1)(a, b)


def bdot_tn(a, b):
    return _bdot_fn(0, 0)(a, b)


def _swap32(x):
    w = x.shape[-1]
    lane = lax.broadcasted_iota(jnp.int32, x.shape, 1)
    return jnp.where((lane & 32) == 0, pltpu.roll(x, w - 32, 1), pltpu.roll(x, 32, 1))


@jax.custom_vjp
def rope(x, c, s):
    return x * c + _swap32(x) * s


def _rope_fwd(x, c, s):
    return rope(x, c, s), (c, s)


def _rope_bwd(res, g):
    c, s = res
    return g * c + _swap32(g * s), jnp.zeros_like(c), jnp.zeros_like(s)


rope.defvjp(_rope_fwd, _rope_bwd)


def rms(x, g):
    return x * lax.rsqrt(jnp.mean(x * x, axis=-1, keepdims=True) + EPS) * g


def normmod(x, g, sh, sc):
    return rms(x, g) * (1.0 + sc) + sh


def log_sigmoid(x):
    return jnp.minimum(x, 0.0) - jnp.log(1.0 + jnp.exp(-jnp.abs(x)))


def _head_mask(shape, h):
    lane = lax.broadcasted_iota(jnp.int32, shape, 1)
    return ((lane >> 6) == h).astype(F32)


def _fold_matrix():
    i = lax.broadcasted_iota(jnp.int32, (256, 128), 0)
    j = lax.broadcasted_iota(jnp.int32, (256, 128), 1)
    return ((i & 63) == j).astype(F32)


def _expand_matrix(g):
    i = lax.broadcasted_iota(jnp.int32, (128, 256), 0)
    j = lax.broadcasted_iota(jnp.int32, (128, 256), 1)
    return (i == (j & 63) + 64 * g).astype(F32)


def _acc(ref, val, first):
    @pl.when(first)
    def _():
        ref[...] = val

    @pl.when(jnp.logical_not(first))
    def _():
        ref[...] += val


def _pick(n, cap, mult):
    best = None
    for d in range(mult, min(n, cap) + 1, mult):
        if n % d == 0:
            best = d
    return best if best is not None else n


def mm(a, b, *, ta=False, tb=False, out_dtype=F32, name, m_rows=None):
    M, K = (a.shape[1], a.shape[0]) if ta else a.shape
    N = b.shape[0] if tb else b.shape[1]
    if m_rows is not None:
        assert not ta
        M = m_rows
    tn = _pick(N, 768, 128)
    if tn < 256 and N <= 2304:
        tn = N
    tm = _pick(M, min(1024, (1 << 20) // tn), 128)
    tk = _pick(K, 2048 if not ta else 1024, 128) if K > 2816 or ta else K
    nk = K // tk
    grid = (M // tm, N // tn, nk)
    a_spec = pl.BlockSpec((tk, tm), lambda i, j, k: (k, i)) if ta else pl.BlockSpec((tm, tk), lambda i, j, k: (i, k))
    b_spec = pl.BlockSpec((tn, tk), lambda i, j, k: (j, k)) if tb else pl.BlockSpec((tk, tn), lambda i, j, k: (k, j))
    dims = (((0 if ta else 1,), (1 if tb else 0,)), ((), ()))

    def body(a_ref, b_ref, o_ref, acc_ref):
        k = pl.program_id(2)
        part = lax.dot_general(a_ref[...].astype(BF16), b_ref[...].astype(BF16), dims, preferred_element_type=F32)
        if nk == 1:
            o_ref[...] = part.astype(out_dtype)
        else:
            _acc(acc_ref, part, k == 0)

            @pl.when(k == nk - 1)
            def _():
                o_ref[...] = acc_ref[...].astype(out_dtype)

    return pl.pallas_call(
        body, grid=grid, in_specs=[a_spec, b_spec], out_specs=pl.BlockSpec((tm, tn), lambda i, j, k: (i, j)),
        out_shape=jax.ShapeDtypeStruct((M, N), out_dtype), scratch_shapes=[pltpu.VMEM((tm, tn), F32)],
        compiler_params=_cparams(("parallel", "parallel", "arbitrary")), name=name)(a, b)


def mm_gated(a, b, res, mod, gate_row, lay, *, name, n_tiles):
    K, N = b.shape
    M = n_tiles * T
    tn = _pick(N, 1024, 128)

    def body(a_ref, b_ref, r_ref, g_ref, y_ref, o_ref):
        y = lax.dot_general(a_ref[...], b_ref[...], (((1,), (0,)), ((), ())), preferred_element_type=F32)
        y_ref[...] = y
        o_ref[...] = r_ref[...] + g_ref[gate_row:gate_row + 1, :] * y

    return pl.pallas_call(
        body, grid=(n_tiles, N // tn),
        in_specs=[pl.BlockSpec((T, K), lambda i, j: (i, 0)), pl.BlockSpec((K, tn), lambda i, j: (0, j)),
                  pl.BlockSpec((T, tn), lambda i, j: (i, j)),
                  pl.BlockSpec((None, 6, tn), lambda i, j: (lay.mod_idx(i), 0, j))],
        out_specs=[pl.BlockSpec((T, tn), lambda i, j: (i, j)), pl.BlockSpec((T, tn), lambda i, j: (i, j))],
        out_shape=[jax.ShapeDtypeStruct((M, N), F32), jax.ShapeDtypeStruct((M, N), F32)],
        compiler_params=_cparams(("parallel", "parallel")), name=name)(a, b, res, mod)


class Layout:
    def __init__(self, B, SEQ, CTX, D):
        assert CTX == T and SEQ % T == 0 and SEQ >= SWA_SPAN
        self.B, self.SEQ, self.CTX, self.D = B, SEQ, CTX, D
        self.tps = SEQ // T
        self.nxt = B * self.tps
        self.nt = self.nxt + B
        self.NX, self.R = B * SEQ, B * SEQ + B * CTX
        self.nq = self.tps + 1

    def mod_idx(self, i):
        return jnp.where(i < self.nxt, i // self.tps, self.B)

    def rope_idx(self, i):
        return jnp.where(i < self.nxt, i % self.tps, self.tps)

    def first_of_mod(self, i):
        return jnp.logical_or(jnp.logical_and(i < self.nxt, i % self.tps == 0), i == self.nxt)

    def qrow(self, b, qi):
        return jnp.where(qi < self.tps, b * self.tps + qi, self.nxt + b)


def _row(w, col=0):
    return pl.BlockSpec((T, w), lambda i: (i, col))


def _full(shape):
    nd = len(shape)
    return pl.BlockSpec(shape, lambda i: (0,) * nd)


def _modspec(lay, rows, D):
    return pl.BlockSpec((None, rows, D), lambda i: (lay.mod_idx(i), 0, 0))


def _ropespec(lay, w):
    return pl.BlockSpec((T, w), lambda i: (lay.rope_idx(i), 0))


def norm_mod_fwd(S, g, mod, k0, lay, n_tiles, name):
    D = S.shape[1]

    def body(s_ref, g_ref, mod_ref, o_ref):
        o_ref[...] = normmod(s_ref[...], g_ref[...], mod_ref[k0:k0 + 1, :], mod_ref[k0 + 1:k0 + 2, :]).astype(BF16)

    return pl.pallas_call(
        body, grid=(n_tiles,), in_specs=[_row(D), _full((1, D)), _modspec(lay, 6, D)], out_specs=_row(D),
        out_shape=jax.ShapeDtypeStruct((n_tiles * T, D), BF16), compiler_params=_cparams(("parallel",)), name=name)(S, g, mod)


def norm_mod_bwd(S, g, mod, k0, dxn, ds_in, lay, n_tiles, name):
    D = S.shape[1]

    def body(s_ref, g_ref, mod_ref, dxn_ref, dsin_ref, ds_ref, dss_ref, dg_ref):
        i = pl.program_id(0)
        _, vjp = jax.vjp(normmod, s_ref[...], g_ref[...], mod_ref[k0:k0 + 1, :], mod_ref[k0 + 1:k0 + 2, :])
        dx, dg, dsh, dsc = vjp(dxn_ref[...])
        ds_ref[...] = dsin_ref[...] + dx
        _acc(dg_ref, dg, i == 0)
        _acc(dss_ref, jnp.concatenate([dsh, dsc], axis=0), lay.first_of_mod(i))

    return pl.pallas_call(
        body, grid=(n_tiles,), in_specs=[_row(D), _full((1, D)), _modspec(lay, 6, D), _row(D), _row(D)],
        out_specs=[_row(D), _modspec(lay, 2, D), _full((1, D))],
        out_shape=[jax.ShapeDtypeStruct((n_tiles * T, D), F32), jax.ShapeDtypeStruct((lay.B + 1, 2, D), F32),
                   jax.ShapeDtypeStruct((1, D), F32)],
        compiler_params=_cparams(("arbitrary",)), name=name)(S, g, mod, dxn, ds_in)


def gate_bwd(dS, y, mod, gate_row, lay, n_tiles, name):
    D = dS.shape[1]

    def body(ds_ref, y_ref, mod_ref, dy_ref, dgate_ref):
        i = pl.program_id(0)
        ds = ds_ref[...]
        dy_ref[...] = (mod_ref[gate_row:gate_row + 1, :] * ds).astype(BF16)
        _acc(dgate_ref, jnp.sum(ds * y_ref[...], axis=0, keepdims=True), lay.first_of_mod(i))

    return pl.pallas_call(
        body, grid=(n_tiles,), in_specs=[_row(D), _row(D), _modspec(lay, 6, D)],
        out_specs=[_row(D), _modspec(lay, 1, D)],
        out_shape=[jax.ShapeDtypeStruct((n_tiles * T, D), BF16), jax.ShapeDtypeStruct((lay.B + 1, 1, D), F32)],
        compiler_params=_cparams(("arbitrary",)), name=name)(dS, y, mod)


def _swiglu(a, b):
    return a * jax.nn.sigmoid(a) * b


def swiglu_fwd(ab, n_tiles, name):
    F = ab.shape[1] // 2

    def body(a_ref, b_ref, o_ref):
        o_ref[...] = _swiglu(a_ref[...], b_ref[...]).astype(BF16)

    return pl.pallas_call(
        body, grid=(n_tiles,), in_specs=[_row(F, 0), _row(F, 1)], out_specs=_row(F),
        out_shape=jax.ShapeDtypeStruct((n_tiles * T, F), BF16), compiler_params=_cparams(("parallel",)), name=name)(ab, ab)


def swiglu_bwd(ab, dact, n_tiles, name):
    F = ab.shape[1] // 2

    def body(a_ref, b_ref, d_ref, o_ref):
        _, vjp = jax.vjp(_swiglu, a_ref[...], b_ref[...])
        da, db = vjp(d_ref[...])
        o_ref[:, 0:F] = da.astype(BF16)
        o_ref[:, F:2 * F] = db.astype(BF16)

    return pl.pallas_call(
        body, grid=(n_tiles,), in_specs=[_row(F, 0), _row(F, 1), _row(F)], out_specs=_row(2 * F),
        out_shape=jax.ShapeDtypeStruct((n_tiles * T, 2 * F), BF16), compiler_params=_cparams(("parallel",)),
        name=name)(ab, ab, dact)


def loss_head(S, g, target, lay, name):
    D = S.shape[1]
    nxt = lay.nxt

    def tile_loss(x, gg, t):
        err = rms(x, gg) - t
        return 0.5 * jnp.sum(jnp.mean(err * err, axis=-1))

    def body(s_ref, g_ref, t_ref, loss_ref, ds_ref, dg_ref):
        i = pl.program_id(0)

        @pl.when(i < nxt)
        def _():
            val, vjp = jax.vjp(tile_loss, s_ref[...], g_ref[...], t_ref[...])
            dx, dg, _ = vjp(jnp.ones((), F32))
            ds_ref[...] = dx
            _acc(dg_ref, dg, i == 0)
            _acc(loss_ref, jnp.full((8, 128), val, F32), i == 0)

        @pl.when(i >= nxt)
        def _():
            ds_ref[...] = jnp.zeros((T, D), F32)

    return pl.pallas_call(
        body, grid=(lay.nt,),
        in_specs=[_row(D), _full((1, D)), pl.BlockSpec((T, D), lambda i: (jnp.minimum(i, nxt - 1), 0))],
        out_specs=[_full((8, 128)), _row(D), _full((1, D))],
        out_shape=[jax.ShapeDtypeStruct((8, 128), F32), jax.ShapeDtypeStruct((lay.R, D), F32),
                   jax.ShapeDtypeStruct((1, D), F32)],
        compiler_params=_cparams(("arbitrary",)), name=name)(S, g, target)


def _ab_prep(zkv, zq, zpe, c256, s256, c128, s128, gkv, gq, wk, wv, wqn, wqp):
    kvn = rms(zkv, gkv)
    kn, v = bdot(kvn, wk), bdot(kvn, wv)
    qn = rms(zq, gq)
    qnope, qpe = bdot(qn, wqn), rope(bdot(qn, wqp), c256, s256)
    kpe = rope(zpe, c128, s128)
    fold = _fold_matrix()
    qparts, kparts = [], []
    for h in range(MLA_HEADS):
        qparts += [qnope[:, 128 * h:128 * (h + 1)], bdot(qpe * _head_mask(qpe.shape, h), fold)]
        kparts += [kn[:, 128 * h:128 * (h + 1)], kpe]
    return jnp.concatenate(qparts, axis=1), jnp.concatenate(kparts, axis=1), v


def _ab_prep_specs(lay):
    return [_row(256, 0), _row(256, 1), _row(128, 12), _ropespec(lay, 256), _ropespec(lay, 256), _ropespec(lay, 128),
            _ropespec(lay, 128), _full((1, 256)), _full((1, 256)), _full((256, 512)), _full((256, 512)),
            _full((256, 512)), _full((256, 256))]


def ab_prep_fwd(z, tabc, tabs, gkv, gq, wk, wv, wqn, wqp, lay, name):
    def body(*refs):
        ins, (q_ref, k_ref, v_ref) = refs[:13], refs[13:]
        q, k, v = _ab_prep(*[r[...].astype(F32) for r in ins])
        q_ref[...] = q.astype(BF16)
        k_ref[...] = k.astype(BF16)
        v_ref[...] = v.astype(BF16)

    R = lay.R
    return pl.pallas_call(
        body, grid=(lay.nt,), in_specs=_ab_prep_specs(lay), out_specs=[_row(1024), _row(1024), _row(512)],
        out_shape=[jax.ShapeDtypeStruct((R, 1024), BF16), jax.ShapeDtypeStruct((R, 1024), BF16),
                   jax.ShapeDtypeStruct((R, 512), BF16)],
        compiler_params=_cparams(("parallel",)), name=name)(z, z, z, tabc, tabs, tabc, tabs, gkv, gq, wk, wv, wqn, wqp)


_MLA_SCALE = (MLA_NOPE + MLA_ROPE) ** -0.5


def _mla_x(q, kx, vx, kh, vh):
    sx, sh = bdot_nt(q, kx) * _MLA_SCALE, bdot_nt(q, kh) * _MLA_SCALE
    m = lax.stop_gradient(jnp.maximum(jnp.max(sx, axis=-1, keepdims=True), jnp.max(sh, axis=-1, keepdims=True)))
    ex, eh = jnp.exp(sx - m), jnp.exp(sh - m)
    inv = 1.0 / (jnp.sum(ex, axis=-1, keepdims=True) + jnp.sum(eh, axis=-1, keepdims=True))
    return bdot(ex * inv, vx) + bdot(eh * inv, vh)


def _mla_h(q, kh, vh):
    sh = bdot_nt(q, kh) * _MLA_SCALE
    eh = jnp.exp(sh - lax.stop_gradient(jnp.max(sh, axis=-1, keepdims=True)))
    return bdot(eh * (1.0 / jnp.sum(eh, axis=-1, keepdims=True)), vh)


def _mla_specs(lay):
    nxt, SEQ = lay.nxt, lay.SEQ
    return [pl.BlockSpec((T, 256), lambda b, h, qi: (lay.qrow(b, qi), h)),
            pl.BlockSpec((SEQ, 256), lambda b, h, qi: (b, h)), pl.BlockSpec((SEQ, 128), lambda b, h, qi: (b, h)),
            pl.BlockSpec((T, 256), lambda b, h, qi: (nxt + b, h)), pl.BlockSpec((T, 128), lambda b, h, qi: (nxt + b, h))]


def mla_fwd(q, k, v, lay, name):
    tps = lay.tps

    def body(q_ref, kx_ref, vx_ref, kh_ref, vh_ref, o_ref):
        qi = pl.program_id(2)
        f = lambda r: r[...].astype(F32)

        @pl.when(qi < tps)
        def _():
            o_ref[...] = _mla_x(f(q_ref), f(kx_ref), f(vx_ref), f(kh_ref), f(vh_ref)).astype(BF16)

        @pl.when(qi == tps)
        def _():
            o_ref[...] = _mla_h(f(q_ref), f(kh_ref), f(vh_ref)).astype(BF16)

    return pl.pallas_call(
        body, grid=(lay.B, MLA_HEADS, lay.nq), in_specs=_mla_specs(lay),
        out_specs=pl.BlockSpec((T, 128), lambda b, h, qi: (lay.qrow(b, qi), h)),
        out_shape=jax.ShapeDtypeStruct((lay.R, 512), BF16),
        compiler_params=_cparams(("parallel", "parallel", "arbitrary")), name=name)(q, k, v, k, v)


def mla_bwd(q, k, v, dmerged, lay, name):
    tps, SEQ, B = lay.tps, lay.SEQ, lay.B

    def body(q_ref, kx_ref, vx_ref, kh_ref, vh_ref, do_ref, dq_ref, dkx_ref, dkh_ref, dvx_ref, dvh_ref):
        qi = pl.program_id(2)
        f = lambda r: r[...].astype(F32)

        @pl.when(qi < tps)
        def _():
            _, vjp = jax.vjp(_mla_x, f(q_ref), f(kx_ref), f(vx_ref), f(kh_ref), f(vh_ref))
            dq, dkx, dvx, dkh, dvh = vjp(do_ref[...])
            dq_ref[...] = dq
            _acc(dkx_ref, dkx, qi == 0)
            _acc(dvx_ref, dvx, qi == 0)
            _acc(dkh_ref, dkh, qi == 0)
            _acc(dvh_ref, dvh, qi == 0)

        @pl.when(qi == tps)
        def _():
            _, vjp = jax.vjp(_mla_h, f(q_ref), f(kh_ref), f(vh_ref))
            dq, dkh, dvh = vjp(do_ref[...])
            dq_ref[...] = dq
            dkh_ref[...] += dkh
            dvh_ref[...] += dvh

    return pl.pallas_call(
        body, grid=(B, MLA_HEADS, lay.nq),
        in_specs=_mla_specs(lay) + [pl.BlockSpec((T, 128), lambda b, h, qi: (lay.qrow(b, qi), h))],
        out_specs=[pl.BlockSpec((T, 256), lambda b, h, qi: (lay.qrow(b, qi), h)),
                   pl.BlockSpec((SEQ, 256), lambda b, h, qi: (b, h)), pl.BlockSpec((T, 256), lambda b, h, qi: (b, h)),
                   pl.BlockSpec((SEQ, 128), lambda b, h, qi: (b, h)), pl.BlockSpec((T, 128), lambda b, h, qi: (b, h))],
        out_shape=[jax.ShapeDtypeStruct((lay.R, 1024), F32), jax.ShapeDtypeStruct((lay.NX, 1024), F32),
                   jax.ShapeDtypeStruct((B * T, 1024), F32), jax.ShapeDtypeStruct((lay.NX, 512), F32),
                   jax.ShapeDtypeStruct((B * T, 512), F32)],
        compiler_params=_cparams(("parallel", "parallel", "arbitrary")), name=name)(q, k, v, k, v, dmerged)


def _cmlp_piece(zu, zv, g, ws, bs):
    u, v = jax.nn.gelu(zu), jax.nn.gelu(zv)
    v = v * lax.rsqrt(jnp.mean(v * v, axis=-1, keepdims=True) + EPS) * g
    return u * (bdot(ws, v) + bs)


def _pieces():
    return [(c, g) for c in range(T // CMLP_CHUNK) for g in range(CMLP_GROUPS)]


def cmlp_merge_fwd(z, o, gvn, ws, bs, lay, name):
    def body(zu_ref, zv_ref, o_ref, g_ref, ws_ref, bs_ref, m_ref):
        m_ref[:, 0:512] = o_ref[...]
        for c, g in _pieces():
            rows, cols = slice(128 * c, 128 * (c + 1)), slice(128 * g, 128 * (g + 1))
            piece = _cmlp_piece(zu_ref[rows, cols], zv_ref[rows, cols], g_ref[:, cols], ws_ref[g], bs_ref[g])
            m_ref[rows, 512 + 128 * g:512 + 128 * (g + 1)] = piece.astype(BF16)

    return pl.pallas_call(
        body, grid=(lay.nt,),
        in_specs=[_row(512, 1), _row(512, 2), _row(512), _full((1, 512)), _full((4, 128, 128)), _full((4, 128, 1))],
        out_specs=_row(1024), out_shape=jax.ShapeDtypeStruct((lay.R, 1024), BF16),
        compiler_params=_cparams(("parallel",)), name=name)(z, z, o, gvn, ws, bs)


def ab_rows_bwd(z, tabc, tabs, dq, dk, dv, dmerged, gkv, gq, wk, wv, wqn, wqp, gvn, ws, bs, lay, name):
    def body(*refs):
        prep_in = refs[:3] + refs[5:9] + refs[13:19]
        zu_ref, zv_ref = refs[3:5]
        dq_ref, dk_ref, dv_ref, dcm_ref = refs[9:13]
        gvn_ref, ws_ref, bs_ref = refs[19:22]
        dz_ref, dgkv_ref, dgq_ref, dwk_ref, dwv_ref, dwqn_ref, dwqp_ref, dgvn_ref, dws_ref, dbs_ref = refs[22:]
        first = pl.program_id(0) == 0
        _, vjp = jax.vjp(_ab_prep, *[r[...].astype(F32) for r in prep_in])
        d = vjp((dq_ref[...], dk_ref[...], dv_ref[...]))
        dz_ref[:, 0:256] = d[0].astype(BF16)
        dz_ref[:, 256:512] = d[1].astype(BF16)
        dz_ref[:, 1536:1664] = d[2].astype(BF16)
        for ref, val in zip((dgkv_ref, dgq_ref, dwk_ref, dwv_ref, dwqn_ref, dwqp_ref), d[7:]):
            _acc(ref, val, first)
        dws = [0.0] * CMLP_GROUPS
        dbs = [0.0] * CMLP_GROUPS
        dgv = [0.0] * CMLP_GROUPS
        for c, g in _pieces():
            rows, cols = slice(128 * c, 128 * (c + 1)), slice(128 * g, 128 * (g + 1))
            _, vjp = jax.vjp(_cmlp_piece, zu_ref[rows, cols], zv_ref[rows, cols], gvn_ref[:, cols], ws_ref[g], bs_ref[g])
            dzu, dzv, dg_, dws_, dbs_ = vjp(dcm_ref[rows, cols])
            dz_ref[rows, 512 + 128 * g:512 + 128 * (g + 1)] = dzu.astype(BF16)
            dz_ref[rows, 1024 + 128 * g:1024 + 128 * (g + 1)] = dzv.astype(BF16)
            dws[g], dbs[g], dgv[g] = dws[g] + dws_, dbs[g] + dbs_, dgv[g] + dg_
        _acc(dgvn_ref, jnp.concatenate(dgv, axis=1), first)
        _acc(dws_ref, jnp.stack(dws), first)
        _acc(dbs_ref, jnp.stack(dbs), first)

    acc_shapes = [(1, 256), (1, 256), (256, 512), (256, 512), (256, 512), (256, 256), (1, 512), (4, 128, 128), (4, 128, 1)]
    return pl.pallas_call(
        body, grid=(lay.nt,),
        in_specs=_ab_prep_specs(lay)[:3] + [_row(512, 1), _row(512, 2)] + _ab_prep_specs(lay)[3:7]
        + [_row(1024), _row(1024), _row(512), _row(512, 1)] + _ab_prep_specs(lay)[7:]
        + [_full((1, 512)), _full((4, 128, 128)), _full((4, 128, 1))],
        out_specs=[_row(AB_IN_P)] + [_full(s) for s in acc_shapes],
        out_shape=[jax.ShapeDtypeStruct((lay.R, AB_IN_P), BF16)] + [jax.ShapeDtypeStruct(s, F32) for s in acc_shapes],
        compiler_params=_cparams(("arbitrary",)), name=name)(
            z, z, z, z, z, tabc, tabs, tabc, tabs, dq, dk, dv, dmerged, gkv, gq, wk, wv, wqn, wqp, gvn, ws, bs)


def _cd_prep(zrk, zrq, zsk, zsq0, zsq1, zsv, c256, s256, c128, s128):
    rk = rope(zrk * (RET_QK ** -0.5), c256, s256)
    rq = rope(zrq, c256, s256)
    sk = rope(zsk, c128, s128)
    sq0, sq1 = rope(zsq0, c256, s256), rope(zsq1, c256, s256)
    e0, e1 = _expand_matrix(0), _expand_matrix(1)
    return rq, rk, sq0, sq1, bdot(sk, e0), bdot(sk, e1), bdot(zsv, e0), bdot(zsv, e1)


def _cd_prep_specs(lay):
    return [_row(256, 0), _row(256, 4), _row(128, 6), _row(256, 7), _row(256, 8), _row(128, 7),
            _ropespec(lay, 256), _ropespec(lay, 256), _ropespec(lay, 128), _ropespec(lay, 128)]


def cd_prep_fwd(z, tabc, tabs, lay, name):
    def body(*refs):
        ins, (rq_ref, rk_ref, sq_ref, ke_ref, ve_ref) = refs[:10], refs[10:]
        rq, rk, sq0, sq1, k0, k1, v0, v1 = _cd_prep(*[r[...] for r in ins])
        rq_ref[...] = rq.astype(BF16)
        rk_ref[...] = rk.astype(BF16)
        for ref, (a, b) in ((sq_ref, (sq0, sq1)), (ke_ref, (k0, k1)), (ve_ref, (v0, v1))):
            ref[:, 0:256] = a.astype(BF16)
            ref[:, 256:512] = b.astype(BF16)

    R = lay.R
    return pl.pallas_call(
        body, grid=(lay.nt,), in_specs=_cd_prep_specs(lay),
        out_specs=[_row(256), _row(256), _row(512), _row(512), _row(512)],
        out_shape=[jax.ShapeDtypeStruct((R, w), BF16) for w in (256, 256, 512, 512, 512)],
        compiler_params=_cparams(("parallel",)), name=name)(z, z, z, z, z, z, tabc, tabs, tabc, tabs)


def _decay_mask(lgf, lgb, t, j):
    diff = t - j
    return (jnp.where(diff >= 0, jnp.exp(lgf * jnp.maximum(diff, 0.0)), 0.0)
            + jnp.where(diff <= 0, jnp.exp(lgb * jnp.maximum(-diff, 0.0)), 0.0))


def _ret_x(h, t0, SEQ, CTX, rq, rkx, rkh, df, db, rvx, rvh):
    t = t0 + lax.broadcasted_iota(jnp.int32, (T, 1), 0).astype(F32)
    jx = lax.broadcasted_iota(jnp.int32, (1, SEQ), 1).astype(F32)
    jh = lax.broadcasted_iota(jnp.int32, (1, CTX), 1).astype(F32)
    lgf, lgb = log_sigmoid(df), log_sigmoid(db)
    qh = rq * _head_mask(rq.shape, h)
    ax = bdot_nt(qh, rkx) * _decay_mask(lgf, lgb, t, jx)
    ah = bdot_nt(qh, rkh) * (jnp.exp(lgf * (t + (CTX - jh))) + jnp.exp(lgb * ((SEQ - t) + jh)))
    return bdot(ax, rvx) + bdot(ah, rvh)


def _ret_h(h, CTX, rq, rkh, df, db, rvh):
    t = lax.broadcasted_iota(jnp.int32, (T, 1), 0).astype(F32)
    jh = lax.broadcasted_iota(jnp.int32, (1, CTX), 1).astype(F32)
    qh = rq * _head_mask(rq.shape, h)
    return bdot(bdot_nt(qh, rkh) * _decay_mask(log_sigmoid(df), log_sigmoid(db), t, jh), rvh)


def _ret_specs(lay):
    nxt, SEQ = lay.nxt, lay.SEQ
    specs = [pl.BlockSpec((T, 256), lambda b, qi: (lay.qrow(b, qi), 0)), pl.BlockSpec((SEQ, 256), lambda b, qi: (b, 0)),
             pl.BlockSpec((T, 256), lambda b, qi: (nxt + b, 0)),
             pl.BlockSpec((4, 8, 128), lambda b, qi: (0, 0, 0)), pl.BlockSpec((4, 8, 128), lambda b, qi: (0, 0, 0))]
    specs += [pl.BlockSpec((SEQ, 128), functools.partial(lambda b, qi, h: (b, 2 + h), h=h)) for h in range(RET_HEADS)]
    specs += [pl.BlockSpec((T, 128), functools.partial(lambda b, qi, h: (nxt + b, 2 + h), h=h)) for h in range(RET_HEADS)]
    return specs


def ret_fwd(rq, rk, z, decf, decb, lay, name):
    tps, SEQ, CTX = lay.tps, lay.SEQ, lay.CTX

    def body(*refs):
        rq_ref, rkx_ref, rkh_ref, df_ref, db_ref = refs[:5]
        rvx, rvh, y_ref = refs[5:9], refs[9:13], refs[13]
        qi = pl.program_id(1)
        f = lambda r: r[...].astype(F32)

        @pl.when(qi < tps)
        def _():
            for h in range(RET_HEADS):
                y_ref[:, 128 * h:128 * (h + 1)] = _ret_x(
                    h, (qi * T).astype(F32), SEQ, CTX, f(rq_ref), f(rkx_ref), f(rkh_ref), df_ref[h][0:1, 0:1],
                    db_ref[h][0:1, 0:1], rvx[h][...], rvh[h][...])

        @pl.when(qi == tps)
        def _():
            for h in range(RET_HEADS):
                y_ref[:, 128 * h:128 * (h + 1)] = _ret_h(
                    h, CTX, f(rq_ref), f(rkh_ref), df_ref[h][0:1, 0:1], db_ref[h][0:1, 0:1], rvh[h][...])

    return pl.pallas_call(
        body, grid=(lay.B, lay.nq), in_specs=_ret_specs(lay),
        out_specs=pl.BlockSpec((T, 512), lambda b, qi: (lay.qrow(b, qi), 0)),
        out_shape=jax.ShapeDtypeStruct((lay.R, 512), F32),
        compiler_params=_cparams(("parallel", "arbitrary")), name=name)(rq, rk, rk, decf, decb, *([z] * 8))


def ret_bwd(rq, rk, z, decf, decb, dy, lay, name):
    tps, SEQ, CTX, B = lay.tps, lay.SEQ, lay.CTX, lay.B

    def body(*refs):
        rq_ref, rkx_ref, rkh_ref, df_ref, db_ref = refs[:5]
        rvx, rvh, dy_ref = refs[5:9], refs[9:13], refs[13]
        drq_ref, dkx_ref, dkh_ref, dvx_ref, dvh_ref, ddf_ref, ddb_ref = refs[14:]
        b, qi = pl.program_id(0), pl.program_id(1)
        f = lambda r: r[...].astype(F32)
        very_first = jnp.logical_and(b == 0, qi == 0)

        @pl.when(qi == 0)
        def _():
            for ref in (dkx_ref, dkh_ref, dvx_ref, dvh_ref):
                ref[...] = jnp.zeros(ref.shape, F32)

        @pl.when(very_first)
        def _():
            ddf_ref[...] = jnp.zeros(ddf_ref.shape, F32)
            ddb_ref[...] = jnp.zeros(ddb_ref.shape, F32)

        @pl.when(qi < tps)
        def _():
            for h in range(RET_HEADS):
                cols = slice(128 * h, 128 * (h + 1))
                _, vjp = jax.vjp(functools.partial(_ret_x, h, (qi * T).astype(F32), SEQ, CTX), f(rq_ref), f(rkx_ref),
                                 f(rkh_ref), df_ref[h][0:1, 0:1], db_ref[h][0:1, 0:1], rvx[h][...], rvh[h][...])
                dq, dkx, dkh, ddf, ddb, dvx, dvh = vjp(dy_ref[:, cols])
                if h == 0:
                    drq_ref[...] = dq
                else:
                    drq_ref[...] += dq
                dkx_ref[...] += dkx
                dkh_ref[...] += dkh
                dvx_ref[:, cols] += dvx
                dvh_ref[:, cols] += dvh
                ddf_ref[h] += jnp.broadcast_to(ddf, (8, 128))
                ddb_ref[h] += jnp.broadcast_to(ddb, (8, 128))

        @pl.when(qi == tps)
        def _():
            for h in range(RET_HEADS):
                cols = slice(128 * h, 128 * (h + 1))
                _, vjp = jax.vjp(functools.partial(_ret_h, h, CTX), f(rq_ref), f(rkh_ref), df_ref[h][0:1, 0:1],
                                 db_ref[h][0:1, 0:1], rvh[h][...])
                dq, dkh, ddf, ddb, dvh = vjp(dy_ref[:, cols])
                if h == 0:
                    drq_ref[...] = dq
                else:
                    drq_ref[...] += dq
                dkh_ref[...] += dkh
                dvh_ref[:, cols] += dvh
                ddf_ref[h] += jnp.broadcast_to(ddf, (8, 128))
                ddb_ref[h] += jnp.broadcast_to(ddb, (8, 128))

    dec_spec = pl.BlockSpec((4, 8, 128), lambda b, qi: (0, 0, 0))
    return pl.pallas_call(
        body, grid=(B, lay.nq),
        in_specs=_ret_specs(lay) + [pl.BlockSpec((T, 512), lambda b, qi: (lay.qrow(b, qi), 0))],
        out_specs=[pl.BlockSpec((T, 256), lambda b, qi: (lay.qrow(b, qi), 0)),
                   pl.BlockSpec((SEQ, 256), lambda b, qi: (b, 0)), pl.BlockSpec((T, 256), lambda b, qi: (b, 0)),
                   pl.BlockSpec((SEQ, 512), lambda b, qi: (b, 0)), pl.BlockSpec((T, 512), lambda b, qi: (b, 0)),
                   dec_spec, dec_spec],
        out_shape=[jax.ShapeDtypeStruct((lay.R, 256), F32), jax.ShapeDtypeStruct((lay.NX, 256), F32),
                   jax.ShapeDtypeStruct((B * T, 256), F32), jax.ShapeDtypeStruct((lay.NX, 512), F32),
                   jax.ShapeDtypeStruct((B * T, 512), F32), jax.ShapeDtypeStruct((4, 8, 128), F32),
                   jax.ShapeDtypeStruct((4, 8, 128), F32)],
        compiler_params=_cparams(("arbitrary", "arbitrary")), name=name)(rq, rk, rk, decf, decb, *([z] * 8), dy)


_SWA_SCALE = SWA_HEAD_DIM ** -0.5


def _swa_head(qh, sw, kh, vw, vh, sink):
    sh = bdot_nt(qh, kh) * _SWA_SCALE
    m = jnp.maximum(jnp.max(sh, axis=-1, keepdims=True), sink)
    if sw is not None:
        m = jnp.maximum(m, jnp.max(sw, axis=-1, keepdims=True))
    m = lax.stop_gradient(m)
    eh, es = jnp.exp(sh - m), jnp.exp(sink - m)
    tot = jnp.sum(eh, axis=-1, keepdims=True) + es
    if sw is None:
        return bdot(eh * (1.0 / tot), vh)
    ew = jnp.exp(sw - m)
    inv = 1.0 / (tot + jnp.sum(ew, axis=-1, keepdims=True))
    return bdot(ew * inv, vw) + bdot(eh * inv, vh)


def _swa_x(t0, kpos0, sq, kw, vw, kh, vh, *sinks):
    t = t0 + lax.broadcasted_iota(jnp.int32, (T, 1), 0)
    pos = kpos0 + lax.broadcasted_iota(jnp.int32, (1, SWA_SPAN), 1)
    band = jnp.abs(t - pos) <= SWA_WINDOW
    out = 0.0
    for i in range(SWA_GROUPS):
        mi = _head_mask(sq.shape, i)
        qh = sq * mi
        sw = jnp.where(band, bdot_nt(qh, kw) * _SWA_SCALE, NEG_INF)
        out = out + _swa_head(qh, sw, kh, vw, vh, sinks[i]) * mi
    return out


def _swa_h(sq, kh, vh, *sinks):
    out = 0.0
    for i in range(SWA_GROUPS):
        mi = _head_mask(sq.shape, i)
        out = out + _swa_head(sq * mi, None, kh, None, vh, sinks[i]) * mi
    return out


def _swa_specs(lay):
    nxt, SEQ = lay.nxt, lay.SEQ
    return [pl.BlockSpec((T, 256), lambda g, b, qi: (lay.qrow(b, qi), g)),
            pl.BlockSpec((SEQ, 256), lambda g, b, qi: (b, g)), pl.BlockSpec((SEQ, 256), lambda g, b, qi: (b, g)),
            pl.BlockSpec((T, 256), lambda g, b, qi: (nxt + b, g)), pl.BlockSpec((T, 256), lambda g, b, qi: (nxt + b, g)),
            pl.BlockSpec((None, 4, 8, 128), lambda g, b, qi: (g, 0, 0, 0))]


def _swa_start(qi, SEQ):
    return pl.multiple_of(jnp.clip((qi - 1) * T, 0, SEQ - SWA_SPAN), T)


def swa_fwd(sq, kexp, vexp, sink, lay, name):
    tps, SEQ = lay.tps, lay.SEQ

    def body(sq_ref, kx_ref, vx_ref, kh_ref, vh_ref, sink_ref, o_ref):
        qi = pl.program_id(2)
        f = lambda r: r[...].astype(F32)
        sinks = [sink_ref[i][0:1, 0:1] for i in range(SWA_GROUPS)]

        @pl.when(qi < tps)
        def _():
            k0 = _swa_start(qi, SEQ)
            kw, vw = kx_ref[pl.ds(k0, SWA_SPAN), :].astype(F32), vx_ref[pl.ds(k0, SWA_SPAN), :].astype(F32)
            o_ref[...] = _swa_x(qi * T, k0, f(sq_ref), kw, vw, f(kh_ref), f(vh_ref), *sinks).astype(BF16)

        @pl.when(qi == tps)
        def _():
            o_ref[...] = _swa_h(f(sq_ref), f(kh_ref), f(vh_ref), *sinks).astype(BF16)

    return pl.pallas_call(
        body, grid=(SWA_KV_HEADS, lay.B, lay.nq), in_specs=_swa_specs(lay),
        out_specs=pl.BlockSpec((T, 256), lambda g, b, qi: (lay.qrow(b, qi), g)),
        out_shape=jax.ShapeDtypeStruct((lay.R, 512), BF16),
        compiler_params=_cparams(("parallel", "parallel", "arbitrary")), name=name)(sq, kexp, vexp, kexp, vexp, sink)


def swa_bwd(sq, kexp, vexp, sink, dmerged, lay, name):
    tps, SEQ, B = lay.tps, lay.SEQ, lay.B

    def body(sq_ref, kx_ref, vx_ref, kh_ref, vh_ref, sink_ref, do_ref, dsq_ref, dkx_ref, dkh_ref, dvx_ref, dvh_ref, dsink_ref):
        b, qi = pl.program_id(1), pl.program_id(2)
        f = lambda r: r[...].astype(F32)
        sinks = [sink_ref[i][0:1, 0:1] for i in range(SWA_GROUPS)]
        very_first = jnp.logical_and(b == 0, qi == 0)

        def acc_sink(ds):
            for i in range(SWA_GROUPS):
                _acc(dsink_ref.at[i], jnp.broadcast_to(ds[i], (8, 128)), very_first)

        @pl.when(qi == 0)
        def _():
            for ref in (dkx_ref, dkh_ref, dvx_ref, dvh_ref):
                ref[...] = jnp.zeros(ref.shape, F32)

        @pl.when(qi < tps)
        def _():
            k0 = _swa_start(qi, SEQ)
            win = pl.ds(k0, SWA_SPAN)
            kw, vw = kx_ref[win, :].astype(F32), vx_ref[win, :].astype(F32)
            _, vjp = jax.vjp(functools.partial(_swa_x, qi * T, k0), f(sq_ref), kw, vw, f(kh_ref), f(vh_ref), *sinks)
            d = vjp(do_ref[...])
            dsq_ref[...] = d[0]
            dkx_ref[win, :] += d[1]
            dvx_ref[win, :] += d[2]
            dkh_ref[...] += d[3]
            dvh_ref[...] += d[4]
            acc_sink(d[5:9])

        @pl.when(qi == tps)
        def _():
            _, vjp = jax.vjp(_swa_h, f(sq_ref), f(kh_ref), f(vh_ref), *sinks)
            d = vjp(do_ref[...])
            dsq_ref[...] = d[0]
            dkh_ref[...] += d[1]
            dvh_ref[...] += d[2]
            acc_sink(d[3:7])

    xs = pl.BlockSpec((SEQ, 256), lambda g, b, qi: (b, g))
    hs = pl.BlockSpec((T, 256), lambda g, b, qi: (b, g))
    return pl.pallas_call(
        body, grid=(SWA_KV_HEADS, B, lay.nq),
        in_specs=_swa_specs(lay) + [pl.BlockSpec((T, 256), lambda g, b, qi: (lay.qrow(b, qi), 2 + g))],
        out_specs=[pl.BlockSpec((T, 256), lambda g, b, qi: (lay.qrow(b, qi), g)), xs, hs, xs, hs,
                   pl.BlockSpec((None, 4, 8, 128), lambda g, b, qi: (g, 0, 0, 0))],
        out_shape=[jax.ShapeDtypeStruct((lay.R, 512), F32), jax.ShapeDtypeStruct((lay.NX, 512), F32),
                   jax.ShapeDtypeStruct((B * T, 512), F32), jax.ShapeDtypeStruct((lay.NX, 512), F32),
                   jax.ShapeDtypeStruct((B * T, 512), F32), jax.ShapeDtypeStruct((SWA_KV_HEADS, 4, 8, 128), F32)],
        compiler_params=_cparams(("arbitrary", "arbitrary", "arbitrary")), name=name)(
            sq, kexp, vexp, kexp, vexp, sink, dmerged)


def _cd_merge_piece(y, rg, g):
    return (y * lax.rsqrt(jnp.mean(y * y, axis=-1, keepdims=True) + EPS) * g) * (rg * jax.nn.sigmoid(rg))


def cd_merge_fwd(y, z, o, gn, lay, name):
    def body(y_ref, rga_ref, rgb_ref, o_ref, g_ref, m_ref):
        for h in range(RET_HEADS):
            cols = slice(128 * h, 128 * (h + 1))
            rg_ref, rcols = (rga_ref, cols) if h < 2 else (rgb_ref, slice(128 * (h - 2), 128 * (h - 1)))
            m_ref[:, cols] = _cd_merge_piece(y_ref[:, cols], rg_ref[:, rcols], g_ref[:, cols]).astype(BF16)
        m_ref[:, 512:1024] = o_ref[...]

    return pl.pallas_call(
        body, grid=(lay.nt,), in_specs=[_row(512), _row(256, 5), _row(256, 6), _row(512), _full((1, 512))],
        out_specs=_row(1024), out_shape=jax.ShapeDtypeStruct((lay.R, 1024), BF16),
        compiler_params=_cparams(("parallel",)), name=name)(y, z, z, o, gn)


def cd_merge_bwd(y, z, gn, dmerged, lay, name):
    def body(y_ref, rga_ref, rgb_ref, g_ref, dm_ref, dy_ref, drg_ref, dg_ref):
        first = pl.program_id(0) == 0
        dgs = []
        for h in range(RET_HEADS):
            cols = slice(128 * h, 128 * (h + 1))
            rg_ref, rcols = (rga_ref, cols) if h < 2 else (rgb_ref, slice(128 * (h - 2), 128 * (h - 1)))
            _, vjp = jax.vjp(_cd_merge_piece, y_ref[:, cols], rg_ref[:, rcols], g_ref[:, cols])
            dy, drg, dg = vjp(dm_ref[:, cols])
            dy_ref[:, cols] = dy
            drg_ref[:, cols] = drg
            dgs.append(dg)
        _acc(dg_ref, jnp.concatenate(dgs, axis=1), first)

    return pl.pallas_call(
        body, grid=(lay.nt,), in_specs=[_row(512), _row(256, 5), _row(256, 6), _full((1, 512)), _row(512, 0)],
        out_specs=[_row(512), _row(512), _full((1, 512))],
        out_shape=[jax.ShapeDtypeStruct((lay.R, 512), F32), jax.ShapeDtypeStruct((lay.R, 512), F32),
                   jax.ShapeDtypeStruct((1, 512), F32)],
        compiler_params=_cparams(("arbitrary",)), name=name)(y, z, z, gn, dmerged)


def cd_rows_bwd(z, tabc, tabs, drq, drk, dsq, dke, dve, drv, drg, lay, name):
    def body(*refs):
        ins = refs[:10]
        drq_ref, drk_ref, dsq_ref, dke_ref, dve_ref, drv_ref, drg_ref, dz_ref = refs[10:]
        _, vjp = jax.vjp(_cd_prep, *[r[...] for r in ins])
        cts = (drq_ref[...], drk_ref[...], dsq_ref[:, 0:256], dsq_ref[:, 256:512], dke_ref[:, 0:256], dke_ref[:, 256:512],
               dve_ref[:, 0:256], dve_ref[:, 256:512])
        dzrk, dzrq, dzsk, dzsq0, dzsq1, dzsv = vjp(cts)[:6]
        dz_ref[:, 0:256] = dzrk.astype(BF16)
        dz_ref[:, 256:768] = drv_ref[...].astype(BF16)
        dz_ref[:, 768:896] = dzsk.astype(BF16)
        dz_ref[:, 896:1024] = dzsv.astype(BF16)
        dz_ref[:, 1024:1280] = dzrq.astype(BF16)
        dz_ref[:, 1280:1792] = drg_ref[...].astype(BF16)
        dz_ref[:, 1792:2048] = dzsq0.astype(BF16)
        dz_ref[:, 2048:2304] = dzsq1.astype(BF16)

    return pl.pallas_call(
        body, grid=(lay.nt,),
        in_specs=_cd_prep_specs(lay) + [_row(256), _row(256), _row(512), _row(512), _row(512), _row(512), _row(512)],
        out_specs=_row(2304), out_shape=jax.ShapeDtypeStruct((lay.R, 2304), BF16),
        compiler_params=_cparams(("parallel",)), name=name)(
            z, z, z, z, z, z, tabc, tabs, tabc, tabs, drq, drk, dsq, dke, dve, drv, drg)


def _pos():
    return lax.axis_index("x"), lax.axis_index("y"), lax.axis_index("c")


def _flip(v, bit):
    return 1 - v if bit else v


def _comm_call(name, body, ins, out_shapes, n_remote, n_local):
    return pl.pallas_call(
        body, in_specs=[ANY] * len(ins), out_specs=[ANY] * len(out_shapes), out_shape=out_shapes,
        scratch_shapes=[pltpu.SemaphoreType.DMA((n_remote,)), pltpu.SemaphoreType.DMA((n_remote,)),
                        pltpu.SemaphoreType.DMA((n_local,))],
        name=name)(*ins)


def gather8(arr, name):
    def body(a_ref, o_ref, ssem, rsem, lsem):
        x, y, c = _pos()
        me = 4 * x + 2 * y + c
        loc = pltpu.make_async_copy(a_ref, o_ref.at[me], lsem.at[0])
        loc.start()
        cps = []
        for m in range(1, 8):
            peer = (_flip(x, m & 4), _flip(y, m & 2), _flip(c, m & 1))
            cps.append(pltpu.make_async_remote_copy(a_ref, o_ref.at[me], ssem.at[m - 1], rsem.at[m - 1],
                                                    device_id=peer, device_id_type=MESH))
            cps[-1].start()
        for cp in cps:
            cp.wait()
        loc.wait()

    return _comm_call(name, body, [arr], [jax.ShapeDtypeStruct((8,) + arr.shape, arr.dtype)], 7, 1)[0]


def gather_chips(arr, name):
    def body(a_ref, o_ref, ssem, rsem, lsem):
        x, y, c = _pos()
        k = 2 * x + y
        loc = pltpu.make_async_copy(a_ref, o_ref.at[k], lsem.at[0])
        loc.start()
        cps = []
        for m in range(1, 4):
            peer = (_flip(x, m & 2), _flip(y, m & 1), c)
            cps.append(pltpu.make_async_remote_copy(a_ref, o_ref.at[k], ssem.at[m - 1], rsem.at[m - 1],
                                                    device_id=peer, device_id_type=MESH))
            cps[-1].start()
        for cp in cps:
            cp.wait()
        loc.wait()

    return _comm_call(name, body, [arr], [jax.ShapeDtypeStruct((4,) + arr.shape, arr.dtype)], 3, 1)[0]


def gather_weights(arrs, name):
    n = len(arrs)

    def body(*refs):
        a_refs, o_refs, (isend, irecv, lsem) = refs[:n], refs[n:2 * n], refs[2 * n:]
        x, y, c = _pos()
        k = 2 * x + y
        sib = (x, y, 1 - c)
        chips = [(m, (_flip(x, m & 2), _flip(y, m & 1)), 2 * _flip(x, m & 2) + _flip(y, m & 1)) for m in range(1, 4)]
        waits = []
        for w, (a, o) in enumerate(zip(a_refs, o_refs)):
            H = a.shape[0] // 2
            own = pl.ds(c * H, H)
            loc = pltpu.make_async_copy(a, o.at[k], lsem.at[w])
            loc.start()
            first = [pltpu.make_async_remote_copy(a.at[own], o.at[k, own], isend.at[6 * w + m - 1], irecv.at[6 * w + m - 1],
                                                  device_id=(*chip, c), device_id_type=MESH) for m, chip, _ in chips]
            for cp in first:
                cp.start()
            waits.append((loc, first, H, own, a, o, w))
        for loc, first, H, own, a, o, w in waits:
            passed = []
            for (m, chip, kk), cp in zip(chips, first):
                pltpu.make_async_remote_copy(a.at[own], o.at[kk, own], isend.at[6 * w + m - 1], irecv.at[6 * w + m - 1],
                                             device_id=(*chip, c), device_id_type=MESH).wait_recv()
                fw = pltpu.make_async_remote_copy(o.at[kk, own], o.at[kk, own], isend.at[6 * w + 2 + m], irecv.at[6 * w + 2 + m],
                                                  device_id=sib, device_id_type=MESH)
                fw.start()
                passed.append(fw)
            for fw in passed:
                fw.wait_recv()
            for cp in first + passed:
                cp.wait_send()
            loc.wait()

    outs = [jax.ShapeDtypeStruct((4,) + a.shape, a.dtype) for a in arrs]
    return _comm_call(name, body, list(arrs), outs, 6 * n, n)


def swap_other_half(arrs, name):
    n = len(arrs)

    def body(*refs):
        a_refs, o_refs, (ssem, rsem, _) = refs[:n], refs[n:2 * n], refs[2 * n:]
        x, y, c = _pos()
        cps = []
        for w, (a, o) in enumerate(zip(a_refs, o_refs)):
            H = a.shape[1] // 2
            cps.append(pltpu.make_async_remote_copy(a.at[:, pl.ds((1 - c) * H, H)], o, ssem.at[w], rsem.at[w],
                                                    device_id=(x, y, 1 - c), device_id_type=MESH))
            cps[-1].start()
        for cp in cps:
            cp.wait()

    outs = [jax.ShapeDtypeStruct((4, a.shape[1] // 2) + a.shape[2:], a.dtype) for a in arrs]
    return _comm_call(name, body, list(arrs), outs, n, 1)


def exchange_chips(arrs, name):
    n = len(arrs)

    def body(*refs):
        a_refs, o_refs, (ssem, rsem, lsem) = refs[:n], refs[n:2 * n], refs[2 * n:]
        x, y, c = _pos()
        k = 2 * x + y
        cps = []
        for w, (a, o) in enumerate(zip(a_refs, o_refs)):
            cps.append(pltpu.make_async_copy(a.at[k], o.at[k], lsem.at[w]))
            cps[-1].start()
            for m in range(1, 4):
                px, py = _flip(x, m & 2), _flip(y, m & 1)
                cps.append(pltpu.make_async_remote_copy(a.at[2 * px + py], o.at[k], ssem.at[3 * w + m - 1], rsem.at[3 * w + m - 1],
                                                        device_id=(px, py, c), device_id_type=MESH))
                cps[-1].start()
        for cp in cps:
            cp.wait()

    return _comm_call(name, body, list(arrs), [jax.ShapeDtypeStruct(a.shape, a.dtype) for a in arrs], 3 * n, n)


def share_halves(arrs, name):
    n = len(arrs)

    def body(*refs):
        a_refs, o_refs, (ssem, rsem, lsem) = refs[:n], refs[n:2 * n], refs[2 * n:]
        x, y, c = _pos()
        cps = []
        for w, (a, o) in enumerate(zip(a_refs, o_refs)):
            H = a.shape[0]
            mine = o.at[pl.ds(c * H, H)]
            cps.append(pltpu.make_async_copy(a, mine, lsem.at[w]))
            cps[-1].start()
            cps.append(pltpu.make_async_remote_copy(a, mine, ssem.at[w], rsem.at[w], device_id=(x, y, 1 - c),
                                                    device_id_type=MESH))
            cps[-1].start()
        for cp in cps:
            cp.wait()

    outs = [jax.ShapeDtypeStruct((2 * a.shape[0],) + a.shape[1:], a.dtype) for a in arrs]
    return _comm_call(name, body, list(arrs), outs, n, n)


def _adamw(w, g, m, v):
    m = ADAM_B1 * m + (1.0 - ADAM_B1) * g
    v = ADAM_B2 * v + (1.0 - ADAM_B2) * (g * g)
    m_hat = m / (1.0 - ADAM_B1 ** ADAM_STEP)
    v_hat = v / (1.0 - ADAM_B2 ** ADAM_STEP)
    return -ADAM_LR * (m_hat / (jnp.sqrt(v_hat) + ADAM_EPS) + ADAM_WD * w), m, v


def _rows_tile(R, C):
    return _pick(R, max(8, (2 << 20) // (4 * C) // 8 * 8), 8)


def chip_partial(gs, buf, cidx, name):
    _, L, R, C = gs.shape
    H, tr = L // 2, _rows_tile(R, C)

    def body(c_ref, g_ref, b_ref, o_ref):
        o_ref[...] = (g_ref[...] + b_ref[...]).astype(BF16)

    return pl.pallas_call(
        body, grid_spec=pltpu.PrefetchScalarGridSpec(
            num_scalar_prefetch=1, grid=(4, H, R // tr),
            in_specs=[pl.BlockSpec((None, None, tr, C), lambda s, l, r, c: (s, c[0] * H + l, r, 0)),
                      pl.BlockSpec((None, None, tr, C), lambda s, l, r, c: (s, l, r, 0))],
            out_specs=pl.BlockSpec((None, None, tr, C), lambda s, l, r, c: (s, l, r, 0))),
        out_shape=jax.ShapeDtypeStruct((4, H, R, C), BF16),
        compiler_params=_cparams(("parallel", "parallel", "parallel")), name=name)(cidx, gs, buf)


def adam_sharded(parts, w, m, v, cidx, name):
    _, H, R, C = parts.shape
    tr = _rows_tile(R, C)

    def body(c_ref, p_ref, w_ref, m_ref, v_ref, g_out, d_out, m_out, v_out):
        g = p_ref[0].astype(F32)
        for s in range(1, 4):
            g = g + p_ref[s].astype(F32)
        g_out[...] = g
        d_out[...], m_out[...], v_out[...] = _adamw(w_ref[...], g, m_ref[...], v_ref[...])

    own = pl.BlockSpec((None, tr, C), lambda l, r, c: (c[0] * H + l, r, 0))
    out = pl.BlockSpec((None, tr, C), lambda l, r, c: (l, r, 0))
    return pl.pallas_call(
        body, grid_spec=pltpu.PrefetchScalarGridSpec(
            num_scalar_prefetch=1, grid=(H, R // tr),
            in_specs=[pl.BlockSpec((4, None, tr, C), lambda l, r, c: (0, l, r, 0)), own, own, own],
            out_specs=[out, out, out, out]),
        out_shape=[jax.ShapeDtypeStruct((H, R, C), F32)] * 4,
        compiler_params=_cparams(("parallel", "parallel")), name=name)(cidx, parts, w, m, v)


def sum8(arr, name):
    n = arr.shape[1]
    tr = _pick(n, 512, 8)

    def body(a_ref, o_ref):
        s = a_ref[0]
        for j in range(1, 8):
            s = s + a_ref[j]
        o_ref[...] = s

    return pl.pallas_call(
        body, grid=(n // tr,), in_specs=[pl.BlockSpec((8, tr, 128), lambda i: (0, i, 0))],
        out_specs=pl.BlockSpec((tr, 128), lambda i: (i, 0)), out_shape=jax.ShapeDtypeStruct((n, 128), F32),
        compiler_params=_cparams(("parallel",)), name=name)(arr)


def adam_rows(w, g, m, v, name):
    n, C = w.shape
    tr = _rows_tile(n, C)

    def body(w_ref, g_ref, m_ref, v_ref, d_out, m_out, v_out):
        d_out[...], m_out[...], v_out[...] = _adamw(w_ref[...], g_ref[...], m_ref[...], v_ref[...])

    spec = pl.BlockSpec((tr, C), lambda i: (i, 0))
    return pl.pallas_call(
        body, grid=(n // tr,), in_specs=[spec] * 4, out_specs=[spec] * 3,
        out_shape=[jax.ShapeDtypeStruct((n, C), F32)] * 3, compiler_params=_cparams(("parallel",)), name=name)(w, g, m, v)


def _silu(c):
    return c * jax.nn.sigmoid(c)


def ada_fwd(c_all, w, b, name):
    NC, D = c_all.shape
    L, _, Wc = w.shape
    tn = _pick(Wc, 512, 128)

    def body(c_ref, w_ref, b_ref, o_ref):
        o_ref[...] = bdot(_silu(c_ref[...]), w_ref[...]) + b_ref[...]

    return pl.pallas_call(
        body, grid=(L, Wc // tn),
        in_specs=[pl.BlockSpec((NC, D), lambda l, j: (0, 0)), pl.BlockSpec((None, D, tn), lambda l, j: (l, 0, j)),
                  pl.BlockSpec((None, 1, tn), lambda l, j: (l, 0, j))],
        out_specs=pl.BlockSpec((None, NC, tn), lambda l, j: (l, 0, j)), out_shape=jax.ShapeDtypeStruct((L, NC, Wc), F32),
        compiler_params=_cparams(("parallel", "parallel")), name=name)(c_all, w, b)


def ada_bwd(c_all, w, dmod, name):
    NC, D = c_all.shape
    L, _, Wc = w.shape
    tn = _pick(Wc, 512, 128)

    def body(c_ref, w_ref, d_ref, dw_ref, db_ref, dc_ref):
        first = jnp.logical_and(pl.program_id(0) == 0, pl.program_id(1) == 0)
        f = lambda cs, ww: bdot(cs, ww)
        _, vjp = jax.vjp(f, _silu(c_ref[...]), w_ref[...])
        dcs, dw = vjp(d_ref[...])
        dw_ref[...] = dw
        db_ref[...] = jnp.sum(d_ref[...], axis=0, keepdims=True)
        _acc(dc_ref, dcs, first)

    return pl.pallas_call(
        body, grid=(L, Wc // tn),
        in_specs=[pl.BlockSpec((NC, D), lambda l, j: (0, 0)), pl.BlockSpec((None, D, tn), lambda l, j: (l, 0, j)),
                  pl.BlockSpec((None, NC, tn), lambda l, j: (l, 0, j))],
        out_specs=[pl.BlockSpec((None, D, tn), lambda l, j: (l, 0, j)), pl.BlockSpec((None, 1, tn), lambda l, j: (l, 0, j)),
                   pl.BlockSpec((NC, D), lambda l, j: (0, 0))],
        out_shape=[jax.ShapeDtypeStruct((L, D, Wc), F32), jax.ShapeDtypeStruct((L, 1, Wc), F32),
                   jax.ShapeDtypeStruct((NC, D), F32)],
        compiler_params=_cparams(("arbitrary", "arbitrary")), name=name)(c_all, w, dmod)


def cctx_grad(dcs_twice, c_ctx, name):
    def body(d_ref, c_ref, o_ref):
        _, vjp = jax.vjp(_silu, c_ref[...])
        o_ref[...] = vjp(0.5 * d_ref[...])[0]

    D = c_ctx.shape[1]
    return pl.pallas_call(body, out_shape=jax.ShapeDtypeStruct((8, D), F32), name=name)(dcs_twice, c_ctx)


def _rope_tables(SEQ):
    t = jnp.arange(SEQ)
    row, col = (t // GRID_W).astype(F32), (t % GRID_W).astype(F32)
    n_freq = 16
    freqs = ROPE_THETA ** (-jnp.arange(n_freq, dtype=F32) / n_freq)
    ang = jnp.concatenate([row[:, None] * freqs, col[:, None] * freqs], axis=-1)
    cos, sin = jnp.cos(ang), jnp.sin(ang)
    c = jnp.tile(jnp.concatenate([cos, cos], axis=1), (1, 4))
    s = jnp.tile(jnp.concatenate([-sin, sin], axis=1), (1, 4))
    return (jnp.concatenate([c, jnp.ones((T, 256), F32)], axis=0), jnp.concatenate([s, jnp.zeros((T, 256), F32)], axis=0))


def _unshard_cols(g):
    return jnp.transpose(g, (1, 2, 0, 3)).reshape(g.shape[1], g.shape[2], 4 * g.shape[3])


def _unshard_rows(g):
    return jnp.transpose(g, (1, 0, 2, 3)).reshape(g.shape[1], 4 * g.shape[2], g.shape[3])


def _shard_cols(w):
    L, R, C = w.shape
    return jnp.transpose(w.reshape(L, R, 4, C // 4), (2, 0, 1, 3))


def _shard_rows(w):
    L, R, C = w.shape
    return jnp.transpose(w.reshape(L, 4, R // 4, C), (1, 0, 2, 3))


def _ab_in_permute(w):
    L, D, _ = w.shape
    return jnp.concatenate([w[..., 0:256], w[..., 320:1600], w[..., 256:320], jnp.zeros((L, D, 64), w.dtype)], axis=-1)


def _ab_in_unpermute(g):
    return jnp.concatenate([g[..., 0:256], g[..., 1536:1600], g[..., 256:1536]], axis=-1)


def _split_heads(w, a):
    L, K, N = w.shape
    w4 = w.reshape(L, K, 4, N // 4)
    return w4[..., :a].reshape(L, K, 4 * a), w4[..., a:].reshape(L, K, N - 4 * a)


def _join_heads(p, q):
    L, K = p.shape[:2]
    return jnp.concatenate([p.reshape(L, K, 4, -1), q.reshape(L, K, 4, -1)], axis=-1).reshape(L, K, -1)


def _pack(arrs):
    parts = []
    for a in arrs:
        f = a.reshape(-1).astype(F32)
        parts.append(jnp.pad(f, (0, (-f.shape[0]) % 1024)))
    return jnp.concatenate(parts).reshape(-1, 128)


def _unpack(buf, like):
    flat, out, off = buf.reshape(-1), [], 0
    for a in like:
        n = math.prod(a.shape)
        out.append(flat[off:off + n].reshape(a.shape))
        off += n + (-n) % 1024
    return out


_SMALL = ("c_ctx", "ada_b", "norm_mix", "norm_ffn", "norm_final", "mla_q_norm", "mla_kv_norm", "cmlp_v_norm", "cmlp_ws",
          "cmlp_bs", "ret_decay_fwd", "ret_decay_bwd", "ret_norm", "swa_sink")
_BIG = ("ffn_in", "ffn_out", "ab_in", "ab_out", "mla_wq_b", "mla_wkv_b", "cd_in", "cd_out")
_WEIGHTS = ("c_ctx", "ada_w", "ada_b", "norm_mix", "norm_ffn", "norm_final", "ffn_in", "ffn_out", "ab_in", "ab_out",
            "mla_q_norm", "mla_kv_norm", "mla_wq_b", "mla_wkv_b", "cmlp_v_norm", "cmlp_ws", "cmlp_bs", "cd_in", "cd_out",
            "ret_decay_fwd", "ret_decay_bwd", "ret_norm", "swa_sink")


def kernel(x, c, ctx, c_ctx, ada_w, ada_b, norm_mix, norm_ffn, norm_final, ffn_in, ffn_out, ab_in, ab_out, mla_q_norm, mla_kv_norm, mla_wq_b, mla_wkv_b, cmlp_v_norm, cmlp_ws, cmlp_bs, cd_in, cd_out, ret_decay_fwd, ret_decay_bwd, ret_norm, swa_sink, loss_target, m_c_ctx, m_ada_w, m_ada_b, m_norm_mix, m_norm_ffn, m_norm_final, m_ffn_in, m_ffn_out, m_ab_in, m_ab_out, m_mla_q_norm, m_mla_kv_norm, m_mla_wq_b, m_mla_wkv_b, m_cmlp_v_norm, m_cmlp_ws, m_cmlp_bs, m_cd_in, m_cd_out, m_ret_decay_fwd, m_ret_decay_bwd, m_ret_norm, m_swa_sink, v_c_ctx, v_ada_w, v_ada_b, v_norm_mix, v_norm_ffn, v_norm_final, v_ffn_in, v_ffn_out, v_ab_in, v_ab_out, v_mla_q_norm, v_mla_kv_norm, v_mla_wq_b, v_mla_wkv_b, v_cmlp_v_norm, v_cmlp_ws, v_cmlp_bs, v_cd_in, v_cd_out, v_ret_decay_fwd, v_ret_decay_bwd, v_ret_norm, v_swa_sink):
    W = dict(c_ctx=c_ctx, ada_w=ada_w, ada_b=ada_b, norm_mix=norm_mix, norm_ffn=norm_ffn, norm_final=norm_final, ffn_in=ffn_in, ffn_out=ffn_out, ab_in=ab_in, ab_out=ab_out, mla_q_norm=mla_q_norm, mla_kv_norm=mla_kv_norm, mla_wq_b=mla_wq_b, mla_wkv_b=mla_wkv_b, cmlp_v_norm=cmlp_v_norm, cmlp_ws=cmlp_ws, cmlp_bs=cmlp_bs, cd_in=cd_in, cd_out=cd_out, ret_decay_fwd=ret_decay_fwd, ret_decay_bwd=ret_decay_bwd, ret_norm=ret_norm, swa_sink=swa_sink)
    M1 = dict(c_ctx=m_c_ctx, ada_w=m_ada_w, ada_b=m_ada_b, norm_mix=m_norm_mix, norm_ffn=m_norm_ffn, norm_final=m_norm_final, ffn_in=m_ffn_in, ffn_out=m_ffn_out, ab_in=m_ab_in, ab_out=m_ab_out, mla_q_norm=m_mla_q_norm, mla_kv_norm=m_mla_kv_norm, mla_wq_b=m_mla_wq_b, mla_wkv_b=m_mla_wkv_b, cmlp_v_norm=m_cmlp_v_norm, cmlp_ws=m_cmlp_ws, cmlp_bs=m_cmlp_bs, cd_in=m_cd_in, cd_out=m_cd_out, ret_decay_fwd=m_ret_decay_fwd, ret_decay_bwd=m_ret_decay_bwd, ret_norm=m_ret_norm, swa_sink=m_swa_sink)
    M2 = dict(c_ctx=v_c_ctx, ada_w=v_ada_w, ada_b=v_ada_b, norm_mix=v_norm_mix, norm_ffn=v_norm_ffn, norm_final=v_norm_final, ffn_in=v_ffn_in, ffn_out=v_ffn_out, ab_in=v_ab_in, ab_out=v_ab_out, mla_q_norm=v_mla_q_norm, mla_kv_norm=v_mla_kv_norm, mla_wq_b=v_mla_wq_b, mla_wkv_b=v_mla_wkv_b, cmlp_v_norm=v_cmlp_v_norm, cmlp_ws=v_cmlp_ws, cmlp_bs=v_cmlp_bs, cd_in=v_cd_in, cd_out=v_cd_out, ret_decay_fwd=v_ret_decay_fwd, ret_decay_bwd=v_ret_decay_bwd, ret_norm=v_ret_norm, swa_sink=v_swa_sink)

    B, SEQ, D = x.shape
    CTX = ctx.shape[1]
    lay = Layout(B, SEQ, CTX, D)
    nt, NX = lay.nt, lay.NX
    ix, iy, ic = lax.axis_index("x"), lax.axis_index("y"), lax.axis_index("c")
    chip, me = 2 * ix + iy, 4 * ix + 2 * iy + ic
    cidx = jnp.reshape(ic, (1,)).astype(jnp.int32)
    Wc = ada_w.shape[2]
    n_even, n_odd = ab_in.shape[0], cd_in.shape[0]

    rn_row = jnp.pad(ret_norm.reshape(1, -1), ((0, 0), (0, D - ret_norm.size)))
    pack0 = jnp.concatenate([c, rn_row, jnp.zeros((8 - (B + 1) % 8, D), F32)], axis=0) if (B + 1) % 8 else jnp.concatenate([c, rn_row], axis=0)
    g0 = gather8(pack0, "gather_cond")
    NC = -(-(8 * B + 1) // 16) * 16
    c_all = jnp.concatenate([g0[:, :B].reshape(8 * B, D), c_ctx[None], jnp.zeros((NC - 8 * B - 1, D), F32)], axis=0)
    rn_sh = ret_norm.shape[1]
    ret_norm_full = jnp.transpose(g0[0::2, B, :ret_norm.size].reshape(4, n_odd, rn_sh), (1, 0, 2)).reshape(n_odd, 4 * rn_sh)

    gw = gather_weights([W[n].astype(BF16) for n in _BIG], "gather_weights")
    w_ffn_in, w_ab_in, w_wq, w_wkv, w_cd_in = (_unshard_cols(gw[i]) for i in (0, 2, 4, 5, 6))
    w_ffn_out, w_ab_out, w_cd_out = (_unshard_rows(gw[i]) for i in (1, 3, 7))
    w_ab_in = _ab_in_permute(w_ab_in)
    w_qn, w_qp = _split_heads(w_wq, MLA_NOPE)
    w_k, w_v = _split_heads(w_wkv, MLA_NOPE)

    ab_sh = lax.dynamic_slice_in_dim(ada_b, chip * Wc, Wc, axis=1)[:, None, :]
    mod_sh = ada_fwd(c_all, ada_w, ab_sh, "ada_fwd")
    mod_all = _unshard_cols(gather_chips(mod_sh, "gather_mod"))
    mod_mine = jnp.concatenate([lax.dynamic_slice_in_dim(mod_all, me * B, B, axis=1), mod_all[:, 8 * B:8 * B + 1]], axis=1)
    mod = mod_mine.reshape(DEPTH, B + 1, 6, D)

    tabc, tabs = _rope_tables(SEQ)
    bc8 = lambda a: jnp.broadcast_to(a.reshape(a.shape + (1, 1)), a.shape + (8, 128))
    row = lambda a: a.reshape(1, -1)

    S = jnp.concatenate([x.reshape(NX, D), ctx.reshape(B * CTX, D)], axis=0)
    saved = []
    for l in range(DEPTH):
        j, even = l // 2, l % 2 == 0
        xn = norm_mod_fwd(S, row(norm_mix[l]), mod[l], 0, lay, nt, f"norm_mix_fwd{l}")
        if even:
            z = mm(xn, w_ab_in[j], name=f"ab_in{l}")
            q, k, v = ab_prep_fwd(z, tabc, tabs, row(mla_kv_norm[j]), row(mla_q_norm[j]), w_k[j], w_v[j], w_qn[j], w_qp[j],
                                  lay, f"ab_prep{l}")
            o = mla_fwd(q, k, v, lay, f"mla{l}")
            merged = cmlp_merge_fwd(z, o, row(cmlp_v_norm[j]), cmlp_ws[j], cmlp_bs[j][:, :, None], lay, f"cmlp{l}")
            w_out, mix = w_ab_out[j], (q, k, v)
        else:
            z = mm(xn, w_cd_in[j], name=f"cd_in{l}")
            rq, rk, sq, ke, ve = cd_prep_fwd(z, tabc, tabs, lay, f"cd_prep{l}")
            decf, decb, sink = bc8(ret_decay_fwd[j]), bc8(ret_decay_bwd[j]), bc8(swa_sink[j].reshape(SWA_KV_HEADS, SWA_GROUPS))
            yret = ret_fwd(rq, rk, z, decf, decb, lay, f"ret{l}")
            osw = swa_fwd(sq, ke, ve, sink, lay, f"swa{l}")
            merged = cd_merge_fwd(yret, z, osw, row(ret_norm_full[j]), lay, f"cd_merge{l}")
            w_out, mix = w_cd_out[j], (rq, rk, sq, ke, ve, decf, decb, sink, yret)
        y, S_mid = mm_gated(merged, w_out, S, mod[l], 2, lay, name=f"mix_out{l}", n_tiles=nt)
        xn2 = norm_mod_fwd(S_mid, row(norm_ffn[l]), mod[l], 3, lay, nt, f"norm_ffn_fwd{l}")
        ab = mm(xn2, w_ffn_in[l], name=f"ffn_in{l}")
        act = swiglu_fwd(ab, nt, f"swiglu{l}")
        f, S_new = mm_gated(act, w_ffn_out[l], S_mid, mod[l], 5, lay, name=f"ffn_out{l}", n_tiles=nt)
        saved.append((S, xn, z, mix, merged, w_out, y, S_mid, xn2, ab, act, f))
        S = S_new

    loss_blk, dS, d_norm_final = loss_head(S, row(norm_final), loss_target.reshape(NX, D), lay, "loss_head")
    loss = lax.psum(loss_blk[0, 0], ("x", "y", "c"))

    G = {n: [None] * W[n].shape[0] for n in ("norm_mix", "norm_ffn", "mla_q_norm", "mla_kv_norm", "cmlp_v_norm", "cmlp_ws",
                                             "cmlp_bs", "ret_decay_fwd", "ret_decay_bwd", "ret_norm", "swa_sink")}
    GB = {n: [None] * cnt for n, cnt in (("ffn_in", DEPTH), ("ffn_out", DEPTH), ("ab_in", n_even), ("ab_out", n_even),
                                         ("wqn", n_even), ("wqp", n_even), ("wk", n_even), ("wv", n_even), ("cd_in", n_odd),
                                         ("cd_out", n_odd))}
    dmod = [None] * DEPTH
    for l in reversed(range(DEPTH)):
        j, even = l // 2, l % 2 == 0
        S_in, xn, z, mix, merged, w_out, y, S_mid, xn2, ab, act, f = saved[l]
        df, dgate2 = gate_bwd(dS, f, mod[l], 5, lay, nt, f"ffn_gate_bwd{l}")
        dact = mm(df, w_ffn_out[l], tb=True, name=f"ffn_out_dx{l}")
        dab = swiglu_bwd(ab, dact, nt, f"swiglu_bwd{l}")
        GB["ffn_out"][l] = mm(act, df, ta=True, name=f"ffn_out_dw{l}")
        GB["ffn_in"][l] = mm(xn2, dab, ta=True, name=f"ffn_in_dw{l}")
        dxn2 = mm(dab, w_ffn_in[l], tb=True, name=f"ffn_in_dx{l}")
        dS_mid, dss2, dg = norm_mod_bwd(S_mid, row(norm_ffn[l]), mod[l], 3, dxn2, dS, lay, nt, f"norm_ffn_bwd{l}")
        G["norm_ffn"][l] = dg
        dy, dgate1 = gate_bwd(dS_mid, y, mod[l], 2, lay, nt, f"mix_gate_bwd{l}")
        dmerged = mm(dy, w_out, tb=True, name=f"mix_out_dx{l}")
        d_w_out = mm(merged, dy, ta=True, name=f"mix_out_dw{l}")
        cat = lambda a, b: jnp.concatenate([a, b], axis=0)
        if even:
            q, k, v = mix
            dq, dkx, dkh, dvx, dvh = mla_bwd(q, k, v, dmerged, lay, f"mla_bwd{l}")
            (dz, dgkv, dgq, dwk, dwv, dwqn, dwqp, dgvn, dws, dbs) = ab_rows_bwd(
                z, tabc, tabs, dq, cat(dkx, dkh), cat(dvx, dvh), dmerged, row(mla_kv_norm[j]), row(mla_q_norm[j]), w_k[j], w_v[j],
                w_qn[j], w_qp[j], row(cmlp_v_norm[j]), cmlp_ws[j], cmlp_bs[j][:, :, None], lay, f"ab_rows_bwd{l}")
            G["mla_kv_norm"][j], G["mla_q_norm"][j], G["cmlp_v_norm"][j] = dgkv, dgq, dgvn
            G["cmlp_ws"][j], G["cmlp_bs"][j] = dws, dbs
            GB["wk"][j], GB["wv"][j], GB["wqn"][j], GB["wqp"][j], GB["ab_out"][j] = dwk, dwv, dwqn, dwqp, d_w_out
            w_in = w_ab_in[j]
        else:
            rq, rk, sq, ke, ve, decf, decb, sink, yret = mix
            dyret, drg, dgn = cd_merge_bwd(yret, z, row(ret_norm_full[j]), dmerged, lay, f"cd_merge_bwd{l}")
            drq, dkx, dkh, dvx, dvh, ddf, ddb = ret_bwd(rq, rk, z, decf, decb, dyret, lay, f"ret_bwd{l}")
            dsq, dkex, dkeh, dvex, dveh, dsink = swa_bwd(sq, ke, ve, sink, dmerged, lay, f"swa_bwd{l}")
            dz = cd_rows_bwd(z, tabc, tabs, drq, cat(dkx, dkh), dsq, cat(dkex, dkeh), cat(dvex, dveh), cat(dvx, dvh), drg, lay,
                             f"cd_rows_bwd{l}")
            G["ret_norm"][j], G["ret_decay_fwd"][j], G["ret_decay_bwd"][j] = dgn, ddf[:, 0, 0], ddb[:, 0, 0]
            G["swa_sink"][j] = dsink[:, :, 0, 0].reshape(-1)
            GB["cd_out"][j] = d_w_out
            w_in = w_cd_in[j]
        GB["ab_in" if even else "cd_in"][j] = mm(xn, dz, ta=True, name=f"mix_in_dw{l}")
        dxn = mm(dz, w_in, tb=True, name=f"mix_in_dx{l}")
        dS, dss1, dg = norm_mod_bwd(S_in, row(norm_mix[l]), mod[l], 0, dxn, dS_mid, lay, nt, f"norm_mix_bwd{l}")
        G["norm_mix"][l] = dg
        dmod[l] = jnp.concatenate([dss1, dgate1, dss2, dgate2], axis=1)
    grad_x = dS[:NX].reshape(B, SEQ, D)

    st = lambda n: jnp.stack([g.reshape((4 * rn_sh,) if n == "ret_norm" else W[n].shape[1:]) for g in G[n]])
    small_parts = {n: st(n) for n in G}
    small_parts["norm_final"] = d_norm_final.reshape(-1)
    dmod_local = jnp.stack(dmod).reshape(DEPTH, B + 1, 6 * D)
    names1 = ["norm_mix", "norm_ffn", "norm_final", "mla_q_norm", "mla_kv_norm", "cmlp_v_norm", "cmlp_ws", "cmlp_bs",
              "ret_decay_fwd", "ret_decay_bwd", "ret_norm", "swa_sink"]
    like1 = [dmod_local] + [small_parts[n] for n in names1]
    g1 = gather8(_pack(like1), "gather_small_grads")
    tot1 = _unpack(sum8(g1, "sum_small_grads"), like1)
    sg = dict(zip(names1, tot1[1:]))
    n_dm = math.prod(dmod_local.shape)
    dm_each = g1.reshape(8, -1)[:, :n_dm].reshape(8, DEPTH, B + 1, 6 * D)
    dmod_all = jnp.concatenate([jnp.transpose(dm_each[:, :, :B], (1, 0, 2, 3)).reshape(DEPTH, 8 * B, 6 * D),
                                tot1[0][:, B:B + 1], jnp.zeros((DEPTH, NC - 8 * B - 1, 6 * D), F32)], axis=1)
    dmod_sh = lax.dynamic_slice_in_dim(dmod_all, chip * Wc, Wc, axis=2)
    g_ada_w, g_ada_b_sh, dcs = ada_bwd(c_all, ada_w, dmod_sh, "ada_bwd")
    like2 = [dcs[8 * B], g_ada_b_sh]
    g2 = gather8(_pack(like2), "gather_ada_grads")
    tot2 = _unpack(sum8(g2, "sum_ada_grads"), like2)
    bc = lambda a: jnp.broadcast_to(a.reshape(1, D), (8, D))
    sg["c_ctx"] = cctx_grad(bc(tot2[0]), bc(c_ctx), "c_ctx_grad")[0]
    off = D + (-D) % 1024
    gab = g2.reshape(8, -1)[0::2, off:off + DEPTH * Wc].reshape(4, DEPTH, Wc)
    sg["ada_b"] = jnp.transpose(gab, (1, 0, 2)).reshape(DEPTH, 4 * Wc)
    sg["ret_norm"] = lax.dynamic_slice_in_dim(sg["ret_norm"], chip * rn_sh, rn_sh, axis=1)

    stack = lambda n: jnp.stack(GB[n])
    gs = {"ffn_in": _shard_cols(stack("ffn_in")), "ffn_out": _shard_rows(stack("ffn_out")),
          "ab_in": _shard_cols(_ab_in_unpermute(stack("ab_in"))), "ab_out": _shard_rows(stack("ab_out")),
          "mla_wq_b": _shard_cols(_join_heads(stack("wqn"), stack("wqp"))),
          "mla_wkv_b": _shard_cols(_join_heads(stack("wk"), stack("wv"))),
          "cd_in": _shard_cols(stack("cd_in")), "cd_out": _shard_rows(stack("cd_out"))}
    bufs = swap_other_half([gs[n] for n in _BIG], "swap_core_halves")
    parts = [chip_partial(gs[n], b, cidx, f"chip_partial_{n}") for n, b in zip(_BIG, bufs)]
    arrived = exchange_chips(parts, "exchange_chips")
    halves = []
    for n, p in zip(_BIG, arrived):
        halves += adam_sharded(p, W[n], M1[n], M2[n], cidx, f"adam_{n}")
    full = share_halves(halves, "share_core_halves")
    out = {n: tuple(full[4 * i:4 * i + 4]) for i, n in enumerate(_BIG)}

    like_s = [W[n] for n in _SMALL]
    dsm, msm, vsm = adam_rows(_pack(like_s), _pack([sg[n].reshape(W[n].shape) for n in _SMALL]), _pack([M1[n] for n in _SMALL]),
                                _pack([M2[n] for n in _SMALL]), "adam_small")
    for n, d_, m_, v_ in zip(_SMALL, _unpack(dsm, like_s), _unpack(msm, like_s), _unpack(vsm, like_s)):
        out[n] = (sg[n].reshape(W[n].shape), d_, m_, v_)
    flat2 = lambda a: a.reshape(-1, a.shape[-1])
    d_, m_, v_ = adam_rows(flat2(ada_w), flat2(g_ada_w), flat2(m_ada_w), flat2(v_ada_w), "adam_ada_w")
    out["ada_w"] = (g_ada_w, d_.reshape(ada_w.shape), m_.reshape(ada_w.shape), v_.reshape(ada_w.shape))

    return (loss, grad_x, *[out[n][0] for n in _WEIGHTS], *[out[n][1] for n in _WEIGHTS], *[out[n][2] for n in _WEIGHTS],
            *[out[n][3] for n in _WEIGHTS])
```

```python
import functools
import math

import jax
import jax.numpy as jnp
import numpy as np
from jax import lax
from jax.experimental import pallas as pl
from jax.experimental.pallas import tpu as pltpu

F32 = jnp.float32
BF16 = jnp.bfloat16
EPS = 1e-6
NEG_INF = -1e30
GRID_W = 64
ROPE_THETA = 10000.0
DEPTH = 4
MLA_HEADS, MLA_Q_LORA, MLA_KV_LORA, MLA_NOPE, MLA_ROPE, MLA_V = 4, 256, 256, 128, 64, 128
CMLP_GROUPS, CMLP_CHUNK = 4, 128
CMLP_WIDTH = 512
RET_HEADS, RET_QK, RET_V = 4, 64, 128
SWA_Q_HEADS, SWA_KV_HEADS, SWA_HEAD_DIM, SWA_WINDOW = 8, 2, 64, 128
SWA_GROUPS = SWA_Q_HEADS // SWA_KV_HEADS
AB_IN_P = 1664
ADAM_LR, ADAM_B1, ADAM_B2, ADAM_EPS, ADAM_WD, ADAM_STEP = 0.001, 0.9, 0.999, 1e-08, 0.01, 10

T = 256
SWA_SPAN = 3 * T
VMEM_LIMIT = 48 * 1024 * 1024
MESH = pl.DeviceIdType.MESH
ANY = pl.BlockSpec(memory_space=pl.ANY)


def _cparams(sem):
    return pltpu.CompilerParams(dimension_semantics=sem, vmem_limit_bytes=VMEM_LIMIT)


@functools.cache
def _bdot_fn(ca, cb):
    fa, fb = 1 - ca, 1 - cb

    def dg(p, q, cp, cq):
        return lax.dot_general(p.astype(BF16), q.astype(BF16), (((cp,), (cq,)), ((), ())), preferred_element_type=F32)

    @jax.custom_vjp
    def bd(a, b):
        return dg(a, b, ca, cb)

    def fwd(a, b):
        return dg(a, b, ca, cb), (a, b)

    def bwd(res, g):
        a, b = res
        da = dg(g, b, 1, fb) if ca == 1 else dg(b, g, fb, 1)
        db = dg(a, g, fa, 0) if cb == 0 else dg(g, a, 0, fa)
        return da, db

    bd.defvjp(fwd, bwd)
    return bd


def bdot(a, b):
    return _bdot_fn(1, 0)(a, b)


def bdot_nt(a, b):
    return _bdot_fn(1, 1)(a, b)


def bdot_tn(a, b):
    return _bdot_fn(0, 0)(a, b)


def _swap32(x):
    w = x.shape[-1]
    lane = lax.broadcasted_iota(jnp.int32, x.shape, 1)
    return jnp.where((lane & 32) == 0, pltpu.roll(x, w - 32, 1), pltpu.roll(x, 32, 1))


@jax.custom_vjp
def rope(x, c, s):
    return x * c + _swap32(x) * s


def _rope_fwd(x, c, s):
    return rope(x, c, s), (c, s)


def _rope_bwd(res, g):
    c, s = res
    return g * c + _swap32(g * s), jnp.zeros_like(c), jnp.zeros_like(s)


rope.defvjp(_rope_fwd, _rope_bwd)


def rms(x, g):
    return x * lax.rsqrt(jnp.mean(x * x, axis=-1, keepdims=True) + EPS) * g


def normmod(x, g, sh, sc):
    return rms(x, g) * (1.0 + sc) + sh


def log_sigmoid(x):
    return jnp.minimum(x, 0.0) - jnp.log(1.0 + jnp.exp(-jnp.abs(x)))


def _head_mask(shape, h):
    lane = lax.broadcasted_iota(jnp.int32, shape, 1)
    return ((lane >> 6) == h).astype(F32)


def _fold_matrix():
    i = lax.broadcasted_iota(jnp.int32, (256, 128), 0)
    j = lax.broadcasted_iota(jnp.int32, (256, 128), 1)
    return ((i & 63) == j).astype(F32)


def _expand_matrix(g):
    i = lax.broadcasted_iota(jnp.int32, (128, 256), 0)
    j = lax.broadcasted_iota(jnp.int32, (128, 256), 1)
    return (i == (j & 63) + 64 * g).astype(F32)


def _acc(ref, val, first):
    @pl.when(first)
    def _():
        ref[...] = val

    @pl.when(jnp.logical_not(first))
    def _():
        ref[...] += val


def _pick(n, cap, mult):
    best = None
    for d in range(mult, min(n, cap) + 1, mult):
        if n % d == 0:
            best = d
    return best if best is not None else n


def mm(a, b, *, ta=False, tb=False, out_dtype=F32, name, m_rows=None):
    M, K = (a.shape[1], a.shape[0]) if ta else a.shape
    N = b.shape[0] if tb else b.shape[1]
    if m_rows is not None:
        assert not ta
        M = m_rows
    tn = _pick(N, 768, 128)
    if tn < 256 and N <= 2304:
        tn = N
    tm = _pick(M, min(1536, (1 << 20) // tn), 128)
    tk = _pick(K, 2048 if not ta else 1024, 128) if K > 2816 or ta else K
    nk = K // tk
    grid = (M // tm, N // tn, nk)
    a_spec = pl.BlockSpec((tk, tm), lambda i, j, k: (k, i)) if ta else pl.BlockSpec((tm, tk), lambda i, j, k: (i, k))
    b_spec = pl.BlockSpec((tn, tk), lambda i, j, k: (j, k)) if tb else pl.BlockSpec((tk, tn), lambda i, j, k: (k, j))
    dims = (((0 if ta else 1,), (1 if tb else 0,)), ((), ()))

    def body(a_ref, b_ref, o_ref, acc_ref):
        k = pl.program_id(2)
        part = lax.dot_general(a_ref[...].astype(BF16), b_ref[...].astype(BF16), dims, preferred_element_type=F32)
        if nk == 1:
            o_ref[...] = part.astype(out_dtype)
        else:
            _acc(acc_ref, part, k == 0)

            @pl.when(k == nk - 1)
            def _():
                o_ref[...] = acc_ref[...].astype(out_dtype)

    return pl.pallas_call(
        body, grid=grid, in_specs=[a_spec, b_spec], out_specs=pl.BlockSpec((tm, tn), lambda i, j, k: (i, j)),
        out_shape=jax.ShapeDtypeStruct((M, N), out_dtype), scratch_shapes=[pltpu.VMEM((tm, tn), F32)],
        compiler_params=_cparams(("parallel", "parallel", "arbitrary")), name=name)(a, b)


def mm_gated(a, b, res, mod, gate_row, lay, *, name, n_tiles):
    K, N = b.shape
    M = n_tiles * T
    tn = _pick(N, 1024, 128)

    def body(a_ref, b_ref, r_ref, g_ref, y_ref, o_ref):
        y = lax.dot_general(a_ref[...], b_ref[...], (((1,), (0,)), ((), ())), preferred_element_type=F32)
        y_ref[...] = y
        o_ref[...] = r_ref[...] + g_ref[gate_row:gate_row + 1, :] * y

    return pl.pallas_call(
        body, grid=(n_tiles, N // tn),
        in_specs=[pl.BlockSpec((T, K), lambda i, j: (i, 0)), pl.BlockSpec((K, tn), lambda i, j: (0, j)),
                  pl.BlockSpec((T, tn), lambda i, j: (i, j)),
                  pl.BlockSpec((None, 6, tn), lambda i, j: (lay.mod_idx(i), 0, j))],
        out_specs=[pl.BlockSpec((T, tn), lambda i, j: (i, j)), pl.BlockSpec((T, tn), lambda i, j: (i, j))],
        out_shape=[jax.ShapeDtypeStruct((M, N), F32), jax.ShapeDtypeStruct((M, N), F32)],
        compiler_params=_cparams(("parallel", "parallel")), name=name)(a, b, res, mod)


class Layout:
    def __init__(self, B, SEQ, CTX, D):
        assert CTX == T and SEQ % T == 0 and SEQ >= SWA_SPAN
        self.B, self.SEQ, self.CTX, self.D = B, SEQ, CTX, D
        self.tps = SEQ // T
        self.nxt = B * self.tps
        self.nt = self.nxt + B
        self.NX, self.R = B * SEQ, B * SEQ + B * CTX
        self.nq = self.tps + 1

    def mod_idx(self, i):
        return jnp.where(i < self.nxt, i // self.tps, self.B)

    def rope_idx(self, i):
        return jnp.where(i < self.nxt, i % self.tps, self.tps)

    def first_of_mod(self, i):
        return jnp.logical_or(jnp.logical_and(i < self.nxt, i % self.tps == 0), i == self.nxt)

    def qrow(self, b, qi):
        return jnp.where(qi < self.tps, b * self.tps + qi, self.nxt + b)


def _row(w, col=0):
    return pl.BlockSpec((T, w), lambda i: (i, col))


def _full(shape):
    nd = len(shape)
    return pl.BlockSpec(shape, lambda i: (0,) * nd)


def _modspec(lay, rows, D):
    return pl.BlockSpec((None, rows, D), lambda i: (lay.mod_idx(i), 0, 0))


def _ropespec(lay, w):
    return pl.BlockSpec((T, w), lambda i: (lay.rope_idx(i), 0))


def _xh_specs(lay, w):
    nxt = lay.nxt
    return [pl.BlockSpec((T, w), lambda i: (jnp.minimum(i, nxt - 1), 0)), pl.BlockSpec((T, w), lambda i: (jnp.maximum(i - nxt, 0), 0))]


def _xh_pick(lay, x_ref, h_ref):
    return jnp.where(pl.program_id(0) < lay.nxt, x_ref[...], h_ref[...])


def norm_mod_fwd(S, g, mod, k0, lay, n_tiles, name):
    D = S.shape[1]

    def body(s_ref, g_ref, mod_ref, o_ref):
        o_ref[...] = normmod(s_ref[...], g_ref[...], mod_ref[k0:k0 + 1, :], mod_ref[k0 + 1:k0 + 2, :]).astype(BF16)

    return pl.pallas_call(
        body, grid=(n_tiles,), in_specs=[_row(D), _full((1, D)), _modspec(lay, 6, D)], out_specs=_row(D),
        out_shape=jax.ShapeDtypeStruct((n_tiles * T, D), BF16), compiler_params=_cparams(("parallel",)), name=name)(S, g, mod)


def norm_mod_bwd(S, g, mod, k0, dxn, ds_in, lay, n_tiles, name):
    D = S.shape[1]

    def body(s_ref, g_ref, mod_ref, dxn_ref, dsin_ref, ds_ref, dss_ref, dg_ref):
        i = pl.program_id(0)
        _, vjp = jax.vjp(normmod, s_ref[...], g_ref[...], mod_ref[k0:k0 + 1, :], mod_ref[k0 + 1:k0 + 2, :])
        dx, dg, dsh, dsc = vjp(dxn_ref[...])
        ds_ref[...] = dsin_ref[...] + dx
        _acc(dg_ref, dg, i == 0)
        _acc(dss_ref, jnp.concatenate([dsh, dsc], axis=0), lay.first_of_mod(i))

    return pl.pallas_call(
        body, grid=(n_tiles,), in_specs=[_row(D), _full((1, D)), _modspec(lay, 6, D), _row(D), _row(D)],
        out_specs=[_row(D), _modspec(lay, 2, D), _full((1, D))],
        out_shape=[jax.ShapeDtypeStruct((n_tiles * T, D), F32), jax.ShapeDtypeStruct((lay.B + 1, 2, D), F32),
                   jax.ShapeDtypeStruct((1, D), F32)],
        compiler_params=_cparams(("arbitrary",)), name=name)(S, g, mod, dxn, ds_in)


def gate_bwd(dS, y, mod, gate_row, lay, n_tiles, name):
    D = dS.shape[1]

    def body(ds_ref, y_ref, mod_ref, dy_ref, dgate_ref):
        i = pl.program_id(0)
        ds = ds_ref[...]
        dy_ref[...] = (mod_ref[gate_row:gate_row + 1, :] * ds).astype(BF16)
        _acc(dgate_ref, jnp.sum(ds * y_ref[...], axis=0, keepdims=True), lay.first_of_mod(i))

    return pl.pallas_call(
        body, grid=(n_tiles,), in_specs=[_row(D), _row(D), _modspec(lay, 6, D)],
        out_specs=[_row(D), _modspec(lay, 1, D)],
        out_shape=[jax.ShapeDtypeStruct((n_tiles * T, D), BF16), jax.ShapeDtypeStruct((lay.B + 1, 1, D), F32)],
        compiler_params=_cparams(("arbitrary",)), name=name)(dS, y, mod)


def _swiglu(a, b):
    return a * jax.nn.sigmoid(a) * b


def _ffn_tiles(M, F):
    tn = _pick(F, 1408, 128)
    return _pick(M, (3 << 18) // tn, 128), tn


def ffn_in_act(x, w, name):
    M, D = x.shape
    F = w.shape[1] // 2
    tm, tn = _ffn_tiles(M, F)
    nj = F // tn

    def body(x_ref, wa_ref, wb_ref, act_ref, a_ref, b_ref):
        dims = (((1,), (0,)), ((), ()))
        a = lax.dot_general(x_ref[...], wa_ref[...], dims, preferred_element_type=F32)
        b = lax.dot_general(x_ref[...], wb_ref[...], dims, preferred_element_type=F32)
        act_ref[...] = _swiglu(a, b).astype(BF16)
        a_ref[...] = a.astype(BF16)
        b_ref[...] = b.astype(BF16)

    out = pl.BlockSpec((tm, tn), lambda i, j: (i, j))
    return pl.pallas_call(
        body, grid=(M // tm, nj),
        in_specs=[pl.BlockSpec((tm, D), lambda i, j: (i, 0)), pl.BlockSpec((D, tn), lambda i, j: (0, j)),
                  pl.BlockSpec((D, tn), lambda i, j: (0, j + nj))],
        out_specs=[out, out, out], out_shape=[jax.ShapeDtypeStruct((M, F), BF16)] * 3,
        compiler_params=_cparams(("parallel", "parallel")), name=name)(x, w, w)


def ffn_out_dx_act(df, w, a, b, name):
    M, D = df.shape
    F = w.shape[0]
    tm, tn = _ffn_tiles(M, F)

    def body(df_ref, w_ref, a_ref, b_ref, da_ref, db_ref):
        dact = lax.dot_general(df_ref[...], w_ref[...], (((1,), (1,)), ((), ())), preferred_element_type=F32)
        _, vjp = jax.vjp(_swiglu, a_ref[...].astype(F32), b_ref[...].astype(F32))
        da, db = vjp(dact)
        da_ref[...] = da.astype(BF16)
        db_ref[...] = db.astype(BF16)

    blk = pl.BlockSpec((tm, tn), lambda i, j: (i, j))
    return pl.pallas_call(
        body, grid=(M // tm, F // tn),
        in_specs=[pl.BlockSpec((tm, D), lambda i, j: (i, 0)), pl.BlockSpec((tn, D), lambda i, j: (j, 0)), blk, blk],
        out_specs=[blk, blk], out_shape=[jax.ShapeDtypeStruct((M, F), BF16)] * 2,
        compiler_params=_cparams(("parallel", "parallel")), name=name)(df, w, a, b)


def ffn_in_dx(da, db, w, name):
    M, F = da.shape
    D = w.shape[0]
    tm, tn = _pick(M, 512, 128), _pick(D, 512, 128)

    def body(da_ref, db_ref, wa_ref, wb_ref, o_ref):
        dims = (((1,), (1,)), ((), ()))
        o_ref[...] = (lax.dot_general(da_ref[...], wa_ref[...], dims, preferred_element_type=F32)
                      + lax.dot_general(db_ref[...], wb_ref[...], dims, preferred_element_type=F32))

    return pl.pallas_call(
        body, grid=(M // tm, D // tn),
        in_specs=[pl.BlockSpec((tm, F), lambda i, j: (i, 0)), pl.BlockSpec((tm, F), lambda i, j: (i, 0)),
                  pl.BlockSpec((tn, F), lambda i, j: (j, 0)), pl.BlockSpec((tn, F), lambda i, j: (j, 1))],
        out_specs=pl.BlockSpec((tm, tn), lambda i, j: (i, j)), out_shape=jax.ShapeDtypeStruct((M, D), F32),
        compiler_params=_cparams(("parallel", "parallel")), name=name)(da, db, w, w)


def loss_head(S, g, target, lay, name):
    D = S.shape[1]
    nxt = lay.nxt

    def tile_loss(x, gg, t):
        err = rms(x, gg) - t
        return 0.5 * jnp.sum(jnp.mean(err * err, axis=-1))

    def body(s_ref, g_ref, t_ref, loss_ref, ds_ref, dg_ref):
        i = pl.program_id(0)

        @pl.when(i < nxt)
        def _():
            val, vjp = jax.vjp(tile_loss, s_ref[...], g_ref[...], t_ref[...])
            dx, dg, _ = vjp(jnp.ones((), F32))
            ds_ref[...] = dx
            _acc(dg_ref, dg, i == 0)
            _acc(loss_ref, jnp.full((8, 128), val, F32), i == 0)

        @pl.when(i >= nxt)
        def _():
            ds_ref[...] = jnp.zeros((T, D), F32)

    return pl.pallas_call(
        body, grid=(lay.nt,),
        in_specs=[_row(D), _full((1, D)), pl.BlockSpec((T, D), lambda i: (jnp.minimum(i, nxt - 1), 0))],
        out_specs=[_full((8, 128)), _row(D), _full((1, D))],
        out_shape=[jax.ShapeDtypeStruct((8, 128), F32), jax.ShapeDtypeStruct((lay.R, D), F32),
                   jax.ShapeDtypeStruct((1, D), F32)],
        compiler_params=_cparams(("arbitrary",)), name=name)(S, g, target)


def _ab_prep(zkv, zq, zpe, c256, s256, c128, s128, gkv, gq, wk, wv, wqn, wqp):
    kvn = rms(zkv, gkv)
    kn, v = bdot(kvn, wk), bdot(kvn, wv)
    qn = rms(zq, gq)
    qnope, qpe = bdot(qn, wqn), rope(bdot(qn, wqp), c256, s256)
    kpe = rope(zpe, c128, s128)
    fold = _fold_matrix()
    qparts, kparts = [], []
    for h in range(MLA_HEADS):
        qparts += [qnope[:, 128 * h:128 * (h + 1)], bdot(qpe * _head_mask(qpe.shape, h), fold)]
        kparts += [kn[:, 128 * h:128 * (h + 1)], kpe]
    return jnp.concatenate(qparts, axis=1), jnp.concatenate(kparts, axis=1), v


def _ab_prep_specs(lay):
    return [_row(256, 0), _row(256, 1), _row(128, 12), _ropespec(lay, 256), _ropespec(lay, 256), _ropespec(lay, 128),
            _ropespec(lay, 128), _full((1, 256)), _full((1, 256)), _full((256, 512)), _full((256, 512)),
            _full((256, 512)), _full((256, 256))]


def ab_prep_fwd(z, tabc, tabs, gkv, gq, wk, wv, wqn, wqp, lay, name):
    def body(*refs):
        ins, (q_ref, k_ref, v_ref) = refs[:13], refs[13:]
        q, k, v = _ab_prep(*[r[...].astype(F32) for r in ins])
        q_ref[...] = q.astype(BF16)
        k_ref[...] = k.astype(BF16)
        v_ref[...] = v.astype(BF16)

    R = lay.R
    return pl.pallas_call(
        body, grid=(lay.nt,), in_specs=_ab_prep_specs(lay), out_specs=[_row(1024), _row(1024), _row(512)],
        out_shape=[jax.ShapeDtypeStruct((R, 1024), BF16), jax.ShapeDtypeStruct((R, 1024), BF16),
                   jax.ShapeDtypeStruct((R, 512), BF16)],
        compiler_params=_cparams(("parallel",)), name=name)(z, z, z, tabc, tabs, tabc, tabs, gkv, gq, wk, wv, wqn, wqp)


_MLA_SCALE = (MLA_NOPE + MLA_ROPE) ** -0.5


def _mla_x(q, kx, vx, kh, vh):
    sx, sh = bdot_nt(q, kx) * _MLA_SCALE, bdot_nt(q, kh) * _MLA_SCALE
    m = lax.stop_gradient(jnp.maximum(jnp.max(sx, axis=-1, keepdims=True), jnp.max(sh, axis=-1, keepdims=True)))
    ex, eh = jnp.exp(sx - m), jnp.exp(sh - m)
    inv = 1.0 / (jnp.sum(ex, axis=-1, keepdims=True) + jnp.sum(eh, axis=-1, keepdims=True))
    return bdot(ex * inv, vx) + bdot(eh * inv, vh)


def _mla_h(q, kh, vh):
    sh = bdot_nt(q, kh) * _MLA_SCALE
    eh = jnp.exp(sh - lax.stop_gradient(jnp.max(sh, axis=-1, keepdims=True)))
    return bdot(eh * (1.0 / jnp.sum(eh, axis=-1, keepdims=True)), vh)


def _mla_specs(lay):
    nxt, SEQ = lay.nxt, lay.SEQ
    return [pl.BlockSpec((T, 256), lambda b, h, qi: (lay.qrow(b, qi), h)),
            pl.BlockSpec((SEQ, 256), lambda b, h, qi: (b, h)), pl.BlockSpec((SEQ, 128), lambda b, h, qi: (b, h)),
            pl.BlockSpec((T, 256), lambda b, h, qi: (nxt + b, h)), pl.BlockSpec((T, 128), lambda b, h, qi: (nxt + b, h))]


def mla_fwd(q, k, v, lay, name):
    tps = lay.tps

    def body(q_ref, kx_ref, vx_ref, kh_ref, vh_ref, o_ref):
        qi = pl.program_id(2)
        f = lambda r: r[...]

        @pl.when(qi < tps)
        def _():
            o_ref[...] = _mla_x(f(q_ref), f(kx_ref), f(vx_ref), f(kh_ref), f(vh_ref)).astype(BF16)

        @pl.when(qi == tps)
        def _():
            o_ref[...] = _mla_h(f(q_ref), f(kh_ref), f(vh_ref)).astype(BF16)

    return pl.pallas_call(
        body, grid=(lay.B, MLA_HEADS, lay.nq), in_specs=_mla_specs(lay),
        out_specs=pl.BlockSpec((T, 128), lambda b, h, qi: (lay.qrow(b, qi), h)),
        out_shape=jax.ShapeDtypeStruct((lay.R, 512), BF16),
        compiler_params=_cparams(("parallel", "parallel", "arbitrary")), name=name)(q, k, v, k, v)


def mla_bwd(q, k, v, dmerged, lay, name):
    tps, SEQ, B = lay.tps, lay.SEQ, lay.B

    def body(q_ref, kx_ref, vx_ref, kh_ref, vh_ref, do_ref, dq_ref, dkx_ref, dkh_ref, dvx_ref, dvh_ref):
        qi = pl.program_id(2)
        f = lambda r: r[...].astype(F32)

        @pl.when(qi < tps)
        def _():
            _, vjp = jax.vjp(_mla_x, f(q_ref), f(kx_ref), f(vx_ref), f(kh_ref), f(vh_ref))
            dq, dkx, dvx, dkh, dvh = vjp(do_ref[...])
            dq_ref[...] = dq
            _acc(dkx_ref, dkx, qi == 0)
            _acc(dvx_ref, dvx, qi == 0)
            _acc(dkh_ref, dkh, qi == 0)
            _acc(dvh_ref, dvh, qi == 0)

        @pl.when(qi == tps)
        def _():
            _, vjp = jax.vjp(_mla_h, f(q_ref), f(kh_ref), f(vh_ref))
            dq, dkh, dvh = vjp(do_ref[...])
            dq_ref[...] = dq
            dkh_ref[...] += dkh
            dvh_ref[...] += dvh

    return pl.pallas_call(
        body, grid=(B, MLA_HEADS, lay.nq),
        in_specs=_mla_specs(lay) + [pl.BlockSpec((T, 128), lambda b, h, qi: (lay.qrow(b, qi), h))],
        out_specs=[pl.BlockSpec((T, 256), lambda b, h, qi: (lay.qrow(b, qi), h)),
                   pl.BlockSpec((SEQ, 256), lambda b, h, qi: (b, h)), pl.BlockSpec((T, 256), lambda b, h, qi: (b, h)),
                   pl.BlockSpec((SEQ, 128), lambda b, h, qi: (b, h)), pl.BlockSpec((T, 128), lambda b, h, qi: (b, h))],
        out_shape=[jax.ShapeDtypeStruct((lay.R, 1024), F32), jax.ShapeDtypeStruct((lay.NX, 1024), F32),
                   jax.ShapeDtypeStruct((B * T, 1024), F32), jax.ShapeDtypeStruct((lay.NX, 512), F32),
                   jax.ShapeDtypeStruct((B * T, 512), F32)],
        compiler_params=_cparams(("parallel", "parallel", "arbitrary")), name=name)(q, k, v, k, v, dmerged)


def _cmlp_piece(zu, zv, g, ws, bs):
    u, v = jax.nn.gelu(zu), jax.nn.gelu(zv)
    v = v * lax.rsqrt(jnp.mean(v * v, axis=-1, keepdims=True) + EPS) * g
    return u * (bdot(ws, v) + bs)


def _pieces():
    return [(c, g) for c in range(T // CMLP_CHUNK) for g in range(CMLP_GROUPS)]


def cmlp_merge_fwd(z, o, gvn, ws, bs, lay, name):
    def body(zu_ref, zv_ref, o_ref, g_ref, ws_ref, bs_ref, m_ref):
        m_ref[:, 0:512] = o_ref[...]
        for c, g in _pieces():
            rows, cols = slice(128 * c, 128 * (c + 1)), slice(128 * g, 128 * (g + 1))
            piece = _cmlp_piece(zu_ref[rows, cols], zv_ref[rows, cols], g_ref[:, cols], ws_ref[g], bs_ref[g])
            m_ref[rows, 512 + 128 * g:512 + 128 * (g + 1)] = piece.astype(BF16)

    return pl.pallas_call(
        body, grid=(lay.nt,),
        in_specs=[_row(512, 1), _row(512, 2), _row(512), _full((1, 512)), _full((4, 128, 128)), _full((4, 128, 1))],
        out_specs=_row(1024), out_shape=jax.ShapeDtypeStruct((lay.R, 1024), BF16),
        compiler_params=_cparams(("parallel",)), name=name)(z, z, o, gvn, ws, bs)


def ab_rows_bwd(z, tabc, tabs, dq, dkx, dkh, dvx, dvh, dmerged, gkv, gq, wk, wv, wqn, wqp, gvn, ws, bs, lay, name):
    def body(*refs):
        prep_in = refs[:3] + refs[5:9] + refs[11:17]
        zu_ref, zv_ref = refs[3:5]
        dq_ref, dcm_ref = refs[9:11]
        gvn_ref, ws_ref, bs_ref = refs[17:20]
        dkx_ref, dkh_ref, dvx_ref, dvh_ref = refs[20:24]
        dz_ref, dgkv_ref, dgq_ref, dwk_ref, dwv_ref, dwqn_ref, dwqp_ref, dgvn_ref, dws_ref, dbs_ref = refs[24:]
        first = pl.program_id(0) == 0
        _, vjp = jax.vjp(_ab_prep, *[r[...].astype(F32) for r in prep_in])
        d = vjp((dq_ref[...], _xh_pick(lay, dkx_ref, dkh_ref), _xh_pick(lay, dvx_ref, dvh_ref)))
        dz_ref[:, 0:256] = d[0].astype(BF16)
        dz_ref[:, 256:512] = d[1].astype(BF16)
        dz_ref[:, 1536:1664] = d[2].astype(BF16)
        for ref, val in zip((dgkv_ref, dgq_ref, dwk_ref, dwv_ref, dwqn_ref, dwqp_ref), d[7:]):
            _acc(ref, val, first)
        dws = [0.0] * CMLP_GROUPS
        dbs = [0.0] * CMLP_GROUPS
        dgv = [0.0] * CMLP_GROUPS
        for c, g in _pieces():
            rows, cols = slice(128 * c, 128 * (c + 1)), slice(128 * g, 128 * (g + 1))
            _, vjp = jax.vjp(_cmlp_piece, zu_ref[rows, cols], zv_ref[rows, cols], gvn_ref[:, cols], ws_ref[g], bs_ref[g])
            dzu, dzv, dg_, dws_, dbs_ = vjp(dcm_ref[rows, cols])
            dz_ref[rows, 512 + 128 * g:512 + 128 * (g + 1)] = dzu.astype(BF16)
            dz_ref[rows, 1024 + 128 * g:1024 + 128 * (g + 1)] = dzv.astype(BF16)
            dws[g], dbs[g], dgv[g] = dws[g] + dws_, dbs[g] + dbs_, dgv[g] + dg_
        _acc(dgvn_ref, jnp.concatenate(dgv, axis=1), first)
        _acc(dws_ref, jnp.stack(dws), first)
        _acc(dbs_ref, jnp.stack(dbs), first)

    acc_shapes = [(1, 256), (1, 256), (256, 512), (256, 512), (256, 512), (256, 256), (1, 512), (4, 128, 128), (4, 128, 1)]
    return pl.pallas_call(
        body, grid=(lay.nt,),
        in_specs=_ab_prep_specs(lay)[:3] + [_row(512, 1), _row(512, 2)] + _ab_prep_specs(lay)[3:7]
        + [_row(1024), _row(512, 1)] + _ab_prep_specs(lay)[7:]
        + [_full((1, 512)), _full((4, 128, 128)), _full((4, 128, 1))] + _xh_specs(lay, 1024) + _xh_specs(lay, 512),
        out_specs=[_row(AB_IN_P)] + [_full(s) for s in acc_shapes],
        out_shape=[jax.ShapeDtypeStruct((lay.R, AB_IN_P), BF16)] + [jax.ShapeDtypeStruct(s, F32) for s in acc_shapes],
        compiler_params=_cparams(("arbitrary",)), name=name)(
            z, z, z, z, z, tabc, tabs, tabc, tabs, dq, dmerged, gkv, gq, wk, wv, wqn, wqp, gvn, ws, bs, dkx, dkh, dvx, dvh)


def _cd_prep(zrk, zrq, zsk, zsq0, zsq1, zsv, c256, s256, c128, s128):
    rk = rope(zrk * (RET_QK ** -0.5), c256, s256)
    rq = rope(zrq, c256, s256)
    sk = rope(zsk, c128, s128)
    sq0, sq1 = rope(zsq0, c256, s256), rope(zsq1, c256, s256)
    e0, e1 = _expand_matrix(0), _expand_matrix(1)
    return rq, rk, sq0, sq1, bdot(sk, e0), bdot(sk, e1), bdot(zsv, e0), bdot(zsv, e1)


def _cd_prep_specs(lay):
    return [_row(256, 0), _row(256, 4), _row(128, 6), _row(256, 7), _row(256, 8), _row(128, 7),
            _ropespec(lay, 256), _ropespec(lay, 256), _ropespec(lay, 128), _ropespec(lay, 128)]


def cd_prep_fwd(z, tabc, tabs, lay, name):
    def body(*refs):
        ins, (rq_ref, rk_ref, sq_ref, ke_ref, ve_ref) = refs[:10], refs[10:]
        rq, rk, sq0, sq1, k0, k1, v0, v1 = _cd_prep(*[r[...] for r in ins])
        rq_ref[...] = rq.astype(BF16)
        rk_ref[...] = rk.astype(BF16)
        for ref, (a, b) in ((sq_ref, (sq0, sq1)), (ke_ref, (k0, k1)), (ve_ref, (v0, v1))):
            ref[:, 0:256] = a.astype(BF16)
            ref[:, 256:512] = b.astype(BF16)

    R = lay.R
    return pl.pallas_call(
        body, grid=(lay.nt,), in_specs=_cd_prep_specs(lay),
        out_specs=[_row(256), _row(256), _row(512), _row(512), _row(512)],
        out_shape=[jax.ShapeDtypeStruct((R, w), BF16) for w in (256, 256, 512, 512, 512)],
        compiler_params=_cparams(("parallel",)), name=name)(z, z, z, z, z, z, tabc, tabs, tabc, tabs)


def _decay_mask(lgf, lgb, t, j):
    diff = t - j
    return (jnp.where(diff >= 0, jnp.exp(lgf * jnp.maximum(diff, 0.0)), 0.0)
            + jnp.where(diff <= 0, jnp.exp(lgb * jnp.maximum(-diff, 0.0)), 0.0))


def _ret_x(h, t0, SEQ, CTX, rq, rkx, rkh, df, db, rvx, rvh):
    t = t0 + lax.broadcasted_iota(jnp.int32, (T, 1), 0).astype(F32)
    jx = lax.broadcasted_iota(jnp.int32, (1, SEQ), 1).astype(F32)
    jh = lax.broadcasted_iota(jnp.int32, (1, CTX), 1).astype(F32)
    lgf, lgb = log_sigmoid(df), log_sigmoid(db)
    qh = rq * _head_mask(rq.shape, h)
    ax = bdot_nt(qh, rkx) * _decay_mask(lgf, lgb, t, jx)
    ah = bdot_nt(qh, rkh) * (jnp.exp(lgf * (t + (CTX - jh))) + jnp.exp(lgb * ((SEQ - t) + jh)))
    return bdot(ax, rvx) + bdot(ah, rvh)


def _ret_h(h, CTX, rq, rkh, df, db, rvh):
    t = lax.broadcasted_iota(jnp.int32, (T, 1), 0).astype(F32)
    jh = lax.broadcasted_iota(jnp.int32, (1, CTX), 1).astype(F32)
    qh = rq * _head_mask(rq.shape, h)
    return bdot(bdot_nt(qh, rkh) * _decay_mask(log_sigmoid(df), log_sigmoid(db), t, jh), rvh)


def _ret_specs(lay):
    nxt, SEQ = lay.nxt, lay.SEQ
    specs = [pl.BlockSpec((T, 256), lambda b, qi: (lay.qrow(b, qi), 0)), pl.BlockSpec((SEQ, 256), lambda b, qi: (b, 0)),
             pl.BlockSpec((T, 256), lambda b, qi: (nxt + b, 0)),
             pl.BlockSpec((4, 8, 128), lambda b, qi: (0, 0, 0)), pl.BlockSpec((4, 8, 128), lambda b, qi: (0, 0, 0))]
    specs += [pl.BlockSpec((SEQ, 128), functools.partial(lambda b, qi, h: (b, 2 + h), h=h)) for h in range(RET_HEADS)]
    specs += [pl.BlockSpec((T, 128), functools.partial(lambda b, qi, h: (nxt + b, 2 + h), h=h)) for h in range(RET_HEADS)]
    return specs


def ret_fwd(rq, rk, z, decf, decb, lay, name):
    tps, SEQ, CTX = lay.tps, lay.SEQ, lay.CTX

    def body(*refs):
        rq_ref, rkx_ref, rkh_ref, df_ref, db_ref = refs[:5]
        rvx, rvh, y_ref = refs[5:9], refs[9:13], refs[13]
        qi = pl.program_id(1)
        f = lambda r: r[...]

        @pl.when(qi < tps)
        def _():
            for h in range(RET_HEADS):
                y_ref[:, 128 * h:128 * (h + 1)] = _ret_x(
                    h, (qi * T).astype(F32), SEQ, CTX, f(rq_ref), f(rkx_ref), f(rkh_ref), df_ref[h][0:1, 0:1],
                    db_ref[h][0:1, 0:1], rvx[h][...], rvh[h][...])

        @pl.when(qi == tps)
        def _():
            for h in range(RET_HEADS):
                y_ref[:, 128 * h:128 * (h + 1)] = _ret_h(
                    h, CTX, f(rq_ref), f(rkh_ref), df_ref[h][0:1, 0:1], db_ref[h][0:1, 0:1], rvh[h][...])

    return pl.pallas_call(
        body, grid=(lay.B, lay.nq), in_specs=_ret_specs(lay),
        out_specs=pl.BlockSpec((T, 512), lambda b, qi: (lay.qrow(b, qi), 0)),
        out_shape=jax.ShapeDtypeStruct((lay.R, 512), F32),
        compiler_params=_cparams(("parallel", "arbitrary")), name=name)(rq, rk, rk, decf, decb, *([z] * 8))


def ret_bwd(rq, rk, z, decf, decb, dy, lay, name):
    tps, SEQ, CTX, B = lay.tps, lay.SEQ, lay.CTX, lay.B

    def body(*refs):
        rq_ref, rkx_ref, rkh_ref, df_ref, db_ref = refs[:5]
        rvx, rvh, dy_ref = refs[5:9], refs[9:13], refs[13]
        drq_ref, dkx_ref, dkh_ref, dvx_ref, dvh_ref, ddf_ref, ddb_ref = refs[14:]
        b, qi = pl.program_id(0), pl.program_id(1)
        f = lambda r: r[...].astype(F32)
        very_first = jnp.logical_and(b == 0, qi == 0)

        @pl.when(qi == 0)
        def _():
            for ref in (dkx_ref, dkh_ref, dvx_ref, dvh_ref):
                ref[...] = jnp.zeros(ref.shape, F32)

        @pl.when(very_first)
        def _():
            ddf_ref[...] = jnp.zeros(ddf_ref.shape, F32)
            ddb_ref[...] = jnp.zeros(ddb_ref.shape, F32)

        @pl.when(qi < tps)
        def _():
            for h in range(RET_HEADS):
                cols = slice(128 * h, 128 * (h + 1))
                _, vjp = jax.vjp(functools.partial(_ret_x, h, (qi * T).astype(F32), SEQ, CTX), f(rq_ref), f(rkx_ref),
                                 f(rkh_ref), df_ref[h][0:1, 0:1], db_ref[h][0:1, 0:1], rvx[h][...], rvh[h][...])
                dq, dkx, dkh, ddf, ddb, dvx, dvh = vjp(dy_ref[:, cols])
                if h == 0:
                    drq_ref[...] = dq
                else:
                    drq_ref[...] += dq
                dkx_ref[...] += dkx
                dkh_ref[...] += dkh
                dvx_ref[:, cols] += dvx
                dvh_ref[:, cols] += dvh
                ddf_ref[h] += jnp.broadcast_to(ddf, (8, 128))
                ddb_ref[h] += jnp.broadcast_to(ddb, (8, 128))

        @pl.when(qi == tps)
        def _():
            for h in range(RET_HEADS):
                cols = slice(128 * h, 128 * (h + 1))
                _, vjp = jax.vjp(functools.partial(_ret_h, h, CTX), f(rq_ref), f(rkh_ref), df_ref[h][0:1, 0:1],
                                 db_ref[h][0:1, 0:1], rvh[h][...])
                dq, dkh, ddf, ddb, dvh = vjp(dy_ref[:, cols])
                if h == 0:
                    drq_ref[...] = dq
                else:
                    drq_ref[...] += dq
                dkh_ref[...] += dkh
                dvh_ref[:, cols] += dvh
                ddf_ref[h] += jnp.broadcast_to(ddf, (8, 128))
                ddb_ref[h] += jnp.broadcast_to(ddb, (8, 128))

    dec_spec = pl.BlockSpec((4, 8, 128), lambda b, qi: (0, 0, 0))
    return pl.pallas_call(
        body, grid=(B, lay.nq),
        in_specs=_ret_specs(lay) + [pl.BlockSpec((T, 512), lambda b, qi: (lay.qrow(b, qi), 0))],
        out_specs=[pl.BlockSpec((T, 256), lambda b, qi: (lay.qrow(b, qi), 0)),
                   pl.BlockSpec((SEQ, 256), lambda b, qi: (b, 0)), pl.BlockSpec((T, 256), lambda b, qi: (b, 0)),
                   pl.BlockSpec((SEQ, 512), lambda b, qi: (b, 0)), pl.BlockSpec((T, 512), lambda b, qi: (b, 0)),
                   dec_spec, dec_spec],
        out_shape=[jax.ShapeDtypeStruct((lay.R, 256), F32), jax.ShapeDtypeStruct((lay.NX, 256), F32),
                   jax.ShapeDtypeStruct((B * T, 256), F32), jax.ShapeDtypeStruct((lay.NX, 512), F32),
                   jax.ShapeDtypeStruct((B * T, 512), F32), jax.ShapeDtypeStruct((4, 8, 128), F32),
                   jax.ShapeDtypeStruct((4, 8, 128), F32)],
        compiler_params=_cparams(("arbitrary", "arbitrary")), name=name)(rq, rk, rk, decf, decb, *([z] * 8), dy)


_SWA_SCALE = SWA_HEAD_DIM ** -0.5


def _swa_head(qh, sw, kh, vw, vh, sink):
    sh = bdot_nt(qh, kh) * _SWA_SCALE
    m = jnp.maximum(jnp.max(sh, axis=-1, keepdims=True), sink)
    if sw is not None:
        m = jnp.maximum(m, jnp.max(sw, axis=-1, keepdims=True))
    m = lax.stop_gradient(m)
    eh, es = jnp.exp(sh - m), jnp.exp(sink - m)
    tot = jnp.sum(eh, axis=-1, keepdims=True) + es
    if sw is None:
        return bdot(eh * (1.0 / tot), vh)
    ew = jnp.exp(sw - m)
    inv = 1.0 / (tot + jnp.sum(ew, axis=-1, keepdims=True))
    return bdot(ew * inv, vw) + bdot(eh * inv, vh)


def _swa_x(t0, kpos0, sq, kw, vw, kh, vh, *sinks):
    t = t0 + lax.broadcasted_iota(jnp.int32, (T, 1), 0)
    pos = kpos0 + lax.broadcasted_iota(jnp.int32, (1, SWA_SPAN), 1)
    band = jnp.abs(t - pos) <= SWA_WINDOW
    out = 0.0
    for i in range(SWA_GROUPS):
        mi = _head_mask(sq.shape, i)
        qh = sq * mi
        sw = jnp.where(band, bdot_nt(qh, kw) * _SWA_SCALE, NEG_INF)
        out = out + _swa_head(qh, sw, kh, vw, vh, sinks[i]) * mi
    return out


def _swa_h(sq, kh, vh, *sinks):
    out = 0.0
    for i in range(SWA_GROUPS):
        mi = _head_mask(sq.shape, i)
        out = out + _swa_head(sq * mi, None, kh, None, vh, sinks[i]) * mi
    return out


def _swa_specs(lay):
    nxt, SEQ = lay.nxt, lay.SEQ
    return [pl.BlockSpec((T, 256), lambda g, b, qi: (lay.qrow(b, qi), g)),
            pl.BlockSpec((SEQ, 256), lambda g, b, qi: (b, g)), pl.BlockSpec((SEQ, 256), lambda g, b, qi: (b, g)),
            pl.BlockSpec((T, 256), lambda g, b, qi: (nxt + b, g)), pl.BlockSpec((T, 256), lambda g, b, qi: (nxt + b, g)),
            pl.BlockSpec((None, 4, 8, 128), lambda g, b, qi: (g, 0, 0, 0))]


def _swa_start(qi, SEQ):
    return pl.multiple_of(jnp.clip((qi - 1) * T, 0, SEQ - SWA_SPAN), T)


def swa_fwd(sq, kexp, vexp, sink, lay, name):
    tps, SEQ = lay.tps, lay.SEQ

    def body(sq_ref, kx_ref, vx_ref, kh_ref, vh_ref, sink_ref, o_ref):
        qi = pl.program_id(2)
        f = lambda r: r[...]
        sinks = [sink_ref[i][0:1, 0:1] for i in range(SWA_GROUPS)]

        @pl.when(qi < tps)
        def _():
            k0 = _swa_start(qi, SEQ)
            kw, vw = kx_ref[pl.ds(k0, SWA_SPAN), :], vx_ref[pl.ds(k0, SWA_SPAN), :]
            o_ref[...] = _swa_x(qi * T, k0, f(sq_ref), kw, vw, f(kh_ref), f(vh_ref), *sinks).astype(BF16)

        @pl.when(qi == tps)
        def _():
            o_ref[...] = _swa_h(f(sq_ref), f(kh_ref), f(vh_ref), *sinks).astype(BF16)

    return pl.pallas_call(
        body, grid=(SWA_KV_HEADS, lay.B, lay.nq), in_specs=_swa_specs(lay),
        out_specs=pl.BlockSpec((T, 256), lambda g, b, qi: (lay.qrow(b, qi), g)),
        out_shape=jax.ShapeDtypeStruct((lay.R, 512), BF16),
        compiler_params=_cparams(("parallel", "parallel", "arbitrary")), name=name)(sq, kexp, vexp, kexp, vexp, sink)


def swa_bwd(sq, kexp, vexp, sink, dmerged, lay, name):
    tps, SEQ, B = lay.tps, lay.SEQ, lay.B

    def body(sq_ref, kx_ref, vx_ref, kh_ref, vh_ref, sink_ref, do_ref, dsq_ref, dkx_ref, dkh_ref, dvx_ref, dvh_ref, dsink_ref):
        b, qi = pl.program_id(1), pl.program_id(2)
        f = lambda r: r[...].astype(F32)
        sinks = [sink_ref[i][0:1, 0:1] for i in range(SWA_GROUPS)]
        very_first = jnp.logical_and(b == 0, qi == 0)

        def acc_sink(ds):
            for i in range(SWA_GROUPS):
                _acc(dsink_ref.at[i], jnp.broadcast_to(ds[i], (8, 128)), very_first)

        @pl.when(qi == 0)
        def _():
            for ref in (dkx_ref, dkh_ref, dvx_ref, dvh_ref):
                ref[...] = jnp.zeros(ref.shape, F32)

        @pl.when(qi < tps)
        def _():
            k0 = _swa_start(qi, SEQ)
            win = pl.ds(k0, SWA_SPAN)
            kw, vw = kx_ref[win, :].astype(F32), vx_ref[win, :].astype(F32)
            _, vjp = jax.vjp(functools.partial(_swa_x, qi * T, k0), f(sq_ref), kw, vw, f(kh_ref), f(vh_ref), *sinks)
            d = vjp(do_ref[...])
            dsq_ref[...] = d[0]
            dkx_ref[win, :] += d[1]
            dvx_ref[win, :] += d[2]
            dkh_ref[...] += d[3]
            dvh_ref[...] += d[4]
            acc_sink(d[5:9])

        @pl.when(qi == tps)
        def _():
            _, vjp = jax.vjp(_swa_h, f(sq_ref), f(kh_ref), f(vh_ref), *sinks)
            d = vjp(do_ref[...])
            dsq_ref[...] = d[0]
            dkh_ref[...] += d[1]
            dvh_ref[...] += d[2]
            acc_sink(d[3:7])

    xs = pl.BlockSpec((SEQ, 256), lambda g, b, qi: (b, g))
    hs = pl.BlockSpec((T, 256), lambda g, b, qi: (b, g))
    return pl.pallas_call(
        body, grid=(SWA_KV_HEADS, B, lay.nq),
        in_specs=_swa_specs(lay) + [pl.BlockSpec((T, 256), lambda g, b, qi: (lay.qrow(b, qi), 2 + g))],
        out_specs=[pl.BlockSpec((T, 256), lambda g, b, qi: (lay.qrow(b, qi), g)), xs, hs, xs, hs,
                   pl.BlockSpec((None, 4, 8, 128), lambda g, b, qi: (g, 0, 0, 0))],
        out_shape=[jax.ShapeDtypeStruct((lay.R, 512), F32), jax.ShapeDtypeStruct((lay.NX, 512), F32),
                   jax.ShapeDtypeStruct((B * T, 512), F32), jax.ShapeDtypeStruct((lay.NX, 512), F32),
                   jax.ShapeDtypeStruct((B * T, 512), F32), jax.ShapeDtypeStruct((SWA_KV_HEADS, 4, 8, 128), F32)],
        compiler_params=_cparams(("arbitrary", "arbitrary", "arbitrary")), name=name)(
            sq, kexp, vexp, kexp, vexp, sink, dmerged)


def _cd_merge_piece(y, rg, g):
    return (y * lax.rsqrt(jnp.mean(y * y, axis=-1, keepdims=True) + EPS) * g) * (rg * jax.nn.sigmoid(rg))


def cd_merge_fwd(y, z, o, gn, lay, name):
    def body(y_ref, rga_ref, rgb_ref, o_ref, g_ref, m_ref):
        for h in range(RET_HEADS):
            cols = slice(128 * h, 128 * (h + 1))
            rg_ref, rcols = (rga_ref, cols) if h < 2 else (rgb_ref, slice(128 * (h - 2), 128 * (h - 1)))
            m_ref[:, cols] = _cd_merge_piece(y_ref[:, cols], rg_ref[:, rcols], g_ref[:, cols]).astype(BF16)
        m_ref[:, 512:1024] = o_ref[...]

    return pl.pallas_call(
        body, grid=(lay.nt,), in_specs=[_row(512), _row(256, 5), _row(256, 6), _row(512), _full((1, 512))],
        out_specs=_row(1024), out_shape=jax.ShapeDtypeStruct((lay.R, 1024), BF16),
        compiler_params=_cparams(("parallel",)), name=name)(y, z, z, o, gn)


def cd_merge_bwd(y, z, gn, dmerged, lay, name):
    def body(y_ref, rga_ref, rgb_ref, g_ref, dm_ref, dy_ref, drg_ref, dg_ref):
        first = pl.program_id(0) == 0
        dgs = []
        for h in range(RET_HEADS):
            cols = slice(128 * h, 128 * (h + 1))
            rg_ref, rcols = (rga_ref, cols) if h < 2 else (rgb_ref, slice(128 * (h - 2), 128 * (h - 1)))
            _, vjp = jax.vjp(_cd_merge_piece, y_ref[:, cols], rg_ref[:, rcols], g_ref[:, cols])
            dy, drg, dg = vjp(dm_ref[:, cols])
            dy_ref[:, cols] = dy
            drg_ref[:, cols] = drg
            dgs.append(dg)
        _acc(dg_ref, jnp.concatenate(dgs, axis=1), first)

    return pl.pallas_call(
        body, grid=(lay.nt,), in_specs=[_row(512), _row(256, 5), _row(256, 6), _full((1, 512)), _row(512, 0)],
        out_specs=[_row(512), _row(512), _full((1, 512))],
        out_shape=[jax.ShapeDtypeStruct((lay.R, 512), F32), jax.ShapeDtypeStruct((lay.R, 512), F32),
                   jax.ShapeDtypeStruct((1, 512), F32)],
        compiler_params=_cparams(("arbitrary",)), name=name)(y, z, z, gn, dmerged)


def cd_rows_bwd(z, tabc, tabs, drq, dsq, drg, drk, dke, dve, drv, lay, name):
    def body(*refs):
        ins = refs[:10]
        drq_ref, dsq_ref, drg_ref = refs[10:13]
        drk, dke, dve, drv = (_xh_pick(lay, refs[13 + 2 * n], refs[14 + 2 * n]) for n in range(4))
        dz_ref = refs[21]
        _, vjp = jax.vjp(_cd_prep, *[r[...] for r in ins])
        cts = (drq_ref[...], drk, dsq_ref[:, 0:256], dsq_ref[:, 256:512], dke[:, 0:256], dke[:, 256:512],
               dve[:, 0:256], dve[:, 256:512])
        dzrk, dzrq, dzsk, dzsq0, dzsq1, dzsv = vjp(cts)[:6]
        dz_ref[:, 0:256] = dzrk.astype(BF16)
        dz_ref[:, 256:768] = drv.astype(BF16)
        dz_ref[:, 768:896] = dzsk.astype(BF16)
        dz_ref[:, 896:1024] = dzsv.astype(BF16)
        dz_ref[:, 1024:1280] = dzrq.astype(BF16)
        dz_ref[:, 1280:1792] = drg_ref[...].astype(BF16)
        dz_ref[:, 1792:2048] = dzsq0.astype(BF16)
        dz_ref[:, 2048:2304] = dzsq1.astype(BF16)

    return pl.pallas_call(
        body, grid=(lay.nt,),
        in_specs=_cd_prep_specs(lay) + [_row(256), _row(512), _row(512)] + _xh_specs(lay, 256) + _xh_specs(lay, 512)
        + _xh_specs(lay, 512) + _xh_specs(lay, 512),
        out_specs=_row(2304), out_shape=jax.ShapeDtypeStruct((lay.R, 2304), BF16),
        compiler_params=_cparams(("parallel",)), name=name)(
            z, z, z, z, z, z, tabc, tabs, tabc, tabs, drq, dsq, drg, *drk, *dke, *dve, *drv)


def _pos():
    return lax.axis_index("x"), lax.axis_index("y"), lax.axis_index("c")


def _flip(v, bit):
    return 1 - v if bit else v


def _comm_call(name, body, ins, out_shapes, n_remote, n_local, aliases=None):
    return pl.pallas_call(
        body, in_specs=[ANY] * len(ins), out_specs=[ANY] * len(out_shapes), out_shape=out_shapes,
        scratch_shapes=[pltpu.SemaphoreType.DMA((n_remote,)), pltpu.SemaphoreType.DMA((n_remote,)),
                        pltpu.SemaphoreType.DMA((n_local,))],
        input_output_aliases=aliases or {}, name=name)(*ins)


def gather8(arr, name):
    def body(a_ref, o_ref, ssem, rsem, lsem):
        x, y, c = _pos()
        me = 4 * x + 2 * y + c
        loc = pltpu.make_async_copy(a_ref, o_ref.at[me], lsem.at[0])
        loc.start()
        cps = []
        for m in range(1, 8):
            peer = (_flip(x, m & 4), _flip(y, m & 2), _flip(c, m & 1))
            cps.append(pltpu.make_async_remote_copy(a_ref, o_ref.at[me], ssem.at[m - 1], rsem.at[m - 1],
                                                    device_id=peer, device_id_type=MESH))
            cps[-1].start()
        for cp in cps:
            cp.wait()
        loc.wait()

    return _comm_call(name, body, [arr], [jax.ShapeDtypeStruct((8,) + arr.shape, arr.dtype)], 7, 1)[0]


def gather_chips(arr, name):
    def body(a_ref, o_ref, ssem, rsem, lsem):
        x, y, c = _pos()
        k = 2 * x + y
        loc = pltpu.make_async_copy(a_ref, o_ref.at[k], lsem.at[0])
        loc.start()
        cps = []
        for m in range(1, 4):
            peer = (_flip(x, m & 2), _flip(y, m & 1), c)
            cps.append(pltpu.make_async_remote_copy(a_ref, o_ref.at[k], ssem.at[m - 1], rsem.at[m - 1],
                                                    device_id=peer, device_id_type=MESH))
            cps[-1].start()
        for cp in cps:
            cp.wait()
        loc.wait()

    return _comm_call(name, body, [arr], [jax.ShapeDtypeStruct((4,) + arr.shape, arr.dtype)], 3, 1)[0]


def gather_weights(arrs, name):
    n = len(arrs)

    def body(*refs):
        a_refs, o_refs, (isend, irecv, lsem) = refs[:n], refs[n:2 * n], refs[2 * n:]
        x, y, c = _pos()
        k = 2 * x + y
        sib = (x, y, 1 - c)
        chips = [(m, (_flip(x, m & 2), _flip(y, m & 1)), 2 * _flip(x, m & 2) + _flip(y, m & 1)) for m in range(1, 4)]
        waits = []
        for w, (a, o) in enumerate(zip(a_refs, o_refs)):
            H = a.shape[0] // 2
            own = pl.ds(c * H, H)
            loc = pltpu.make_async_copy(a, o.at[k], lsem.at[w])
            loc.start()
            first = [pltpu.make_async_remote_copy(a.at[own], o.at[k, own], isend.at[6 * w + m - 1], irecv.at[6 * w + m - 1],
                                                  device_id=(*chip, c), device_id_type=MESH) for m, chip, _ in chips]
            for cp in first:
                cp.start()
            waits.append((loc, first, H, own, a, o, w))
        for loc, first, H, own, a, o, w in waits:
            passed = []
            for (m, chip, kk), cp in zip(chips, first):
                pltpu.make_async_remote_copy(a.at[own], o.at[kk, own], isend.at[6 * w + m - 1], irecv.at[6 * w + m - 1],
                                             device_id=(*chip, c), device_id_type=MESH).wait_recv()
                fw = pltpu.make_async_remote_copy(o.at[kk, own], o.at[kk, own], isend.at[6 * w + 2 + m], irecv.at[6 * w + 2 + m],
                                                  device_id=sib, device_id_type=MESH)
                fw.start()
                passed.append(fw)
            for fw in passed:
                fw.wait_recv()
            for cp in first + passed:
                cp.wait_send()
            loc.wait()

    outs = [jax.ShapeDtypeStruct((4,) + a.shape, a.dtype) for a in arrs]
    return _comm_call(name, body, list(arrs), outs, 6 * n, n)


def swap_other_half(arrs, name):
    n = len(arrs)

    def body(*refs):
        a_refs, o_refs, (ssem, rsem, _) = refs[:n], refs[n:2 * n], refs[2 * n:]
        x, y, c = _pos()
        cps = []
        for w, (a, o) in enumerate(zip(a_refs, o_refs)):
            H = a.shape[1] // 2
            cps.append(pltpu.make_async_remote_copy(a.at[:, pl.ds((1 - c) * H, H)], o, ssem.at[w], rsem.at[w],
                                                    device_id=(x, y, 1 - c), device_id_type=MESH))
            cps[-1].start()
        for cp in cps:
            cp.wait()

    outs = [jax.ShapeDtypeStruct((4, a.shape[1] // 2) + a.shape[2:], a.dtype) for a in arrs]
    return _comm_call(name, body, list(arrs), outs, n, 1)


def exchange_chips(arrs, name):
    n = len(arrs)

    def body(*refs):
        a_refs, o_refs, (ssem, rsem, lsem) = refs[:n], refs[n:2 * n], refs[2 * n:]
        x, y, c = _pos()
        k = 2 * x + y
        cps = []
        for w, (a, o) in enumerate(zip(a_refs, o_refs)):
            cps.append(pltpu.make_async_copy(a.at[k], o.at[k], lsem.at[w]))
            cps[-1].start()
            for m in range(1, 4):
                px, py = _flip(x, m & 2), _flip(y, m & 1)
                cps.append(pltpu.make_async_remote_copy(a.at[2 * px + py], o.at[k], ssem.at[3 * w + m - 1], rsem.at[3 * w + m - 1],
                                                        device_id=(px, py, c), device_id_type=MESH))
                cps[-1].start()
        for cp in cps:
            cp.wait()

    return _comm_call(name, body, list(arrs), [jax.ShapeDtypeStruct(a.shape, a.dtype) for a in arrs], 3 * n, n)


def share_halves(arrs, name):
    n = len(arrs)

    def body(*refs):
        o_refs, (ssem, rsem, _) = refs[n:2 * n], refs[2 * n:]
        x, y, c = _pos()
        cps = []
        for w, o in enumerate(o_refs):
            H = o.shape[0] // 2
            mine = o.at[pl.ds(c * H, H)]
            cps.append(pltpu.make_async_remote_copy(mine, mine, ssem.at[w], rsem.at[w], device_id=(x, y, 1 - c),
                                                    device_id_type=MESH))
            cps[-1].start()
        for cp in cps:
            cp.wait()

    outs = [jax.ShapeDtypeStruct(a.shape, a.dtype) for a in arrs]
    return _comm_call(name, body, list(arrs), outs, n, 1, aliases={i: i for i in range(n)})


def _adamw(w, g, m, v):
    m = ADAM_B1 * m + (1.0 - ADAM_B1) * g
    v = ADAM_B2 * v + (1.0 - ADAM_B2) * (g * g)
    m_hat = m / (1.0 - ADAM_B1 ** ADAM_STEP)
    v_hat = v / (1.0 - ADAM_B2 ** ADAM_STEP)
    return -ADAM_LR * (m_hat / (jnp.sqrt(v_hat) + ADAM_EPS) + ADAM_WD * w), m, v


def _rows_tile(R, C):
    return _pick(R, max(8, (2 << 20) // (4 * C) // 8 * 8), 8)


def chip_partial(gs, buf, cidx, name):
    _, L, R, C = gs.shape
    H, tr = L // 2, _rows_tile(R, C)

    def body(c_ref, g_ref, b_ref, o_ref):
        o_ref[...] = (g_ref[...] + b_ref[...]).astype(BF16)

    return pl.pallas_call(
        body, grid_spec=pltpu.PrefetchScalarGridSpec(
            num_scalar_prefetch=1, grid=(4, H, R // tr),
            in_specs=[pl.BlockSpec((None, None, tr, C), lambda s, l, r, c: (s, c[0] * H + l, r, 0)),
                      pl.BlockSpec((None, None, tr, C), lambda s, l, r, c: (s, l, r, 0))],
            out_specs=pl.BlockSpec((None, None, tr, C), lambda s, l, r, c: (s, l, r, 0))),
        out_shape=jax.ShapeDtypeStruct((4, H, R, C), BF16),
        compiler_params=_cparams(("parallel", "parallel", "parallel")), name=name)(cidx, gs, buf)


def adam_sharded(parts, w, m, v, cidx, name):
    _, H, R, C = parts.shape
    tr = _rows_tile(R, C)

    def body(c_ref, p_ref, w_ref, m_ref, v_ref, g_out, d_out, m_out, v_out):
        g = p_ref[0].astype(F32)
        for s in range(1, 4):
            g = g + p_ref[s].astype(F32)
        g_out[...] = g
        d_out[...], m_out[...], v_out[...] = _adamw(w_ref[...], g, m_ref[...], v_ref[...])

    own = pl.BlockSpec((None, tr, C), lambda l, r, c: (c[0] * H + l, r, 0))
    return pl.pallas_call(
        body, grid_spec=pltpu.PrefetchScalarGridSpec(
            num_scalar_prefetch=1, grid=(H, R // tr),
            in_specs=[pl.BlockSpec((4, None, tr, C), lambda l, r, c: (0, l, r, 0)), own, own, own],
            out_specs=[own, own, own, own]),
        out_shape=[jax.ShapeDtypeStruct((2 * H, R, C), F32)] * 4,
        compiler_params=_cparams(("parallel", "parallel")), name=name)(cidx, parts, w, m, v)


def sum8(arr, name):
    n = arr.shape[1]
    tr = _pick(n, 512, 8)

    def body(a_ref, o_ref):
        s = a_ref[0]
        for j in range(1, 8):
            s = s + a_ref[j]
        o_ref[...] = s

    return pl.pallas_call(
        body, grid=(n // tr,), in_specs=[pl.BlockSpec((8, tr, 128), lambda i: (0, i, 0))],
        out_specs=pl.BlockSpec((tr, 128), lambda i: (i, 0)), out_shape=jax.ShapeDtypeStruct((n, 128), F32),
        compiler_params=_cparams(("parallel",)), name=name)(arr)


def adam_rows(w, g, m, v, name):
    n, C = w.shape
    tr = _rows_tile(n, C)

    def body(w_ref, g_ref, m_ref, v_ref, d_out, m_out, v_out):
        d_out[...], m_out[...], v_out[...] = _adamw(w_ref[...], g_ref[...], m_ref[...], v_ref[...])

    spec = pl.BlockSpec((tr, C), lambda i: (i, 0))
    return pl.pallas_call(
        body, grid=(n // tr,), in_specs=[spec] * 4, out_specs=[spec] * 3,
        out_shape=[jax.ShapeDtypeStruct((n, C), F32)] * 3, compiler_params=_cparams(("parallel",)), name=name)(w, g, m, v)


def _silu(c):
    return c * jax.nn.sigmoid(c)


def ada_fwd(c_all, w, b, name):
    NC, D = c_all.shape
    L, _, Wc = w.shape
    tn = _pick(Wc, 512, 128)

    def body(c_ref, w_ref, b_ref, o_ref):
        o_ref[...] = bdot(_silu(c_ref[...]), w_ref[...]) + b_ref[...]

    return pl.pallas_call(
        body, grid=(L, Wc // tn),
        in_specs=[pl.BlockSpec((NC, D), lambda l, j: (0, 0)), pl.BlockSpec((None, D, tn), lambda l, j: (l, 0, j)),
                  pl.BlockSpec((None, 1, tn), lambda l, j: (l, 0, j))],
        out_specs=pl.BlockSpec((None, NC, tn), lambda l, j: (l, 0, j)), out_shape=jax.ShapeDtypeStruct((L, NC, Wc), F32),
        compiler_params=_cparams(("parallel", "parallel")), name=name)(c_all, w, b)


def ada_bwd(c_all, w, dmod, name):
    NC, D = c_all.shape
    L, _, Wc = w.shape
    tn = _pick(Wc, 512, 128)

    def body(c_ref, w_ref, d_ref, dw_ref, db_ref, dc_ref):
        first = jnp.logical_and(pl.program_id(0) == 0, pl.program_id(1) == 0)
        f = lambda cs, ww: bdot(cs, ww)
        _, vjp = jax.vjp(f, _silu(c_ref[...]), w_ref[...])
        dcs, dw = vjp(d_ref[...])
        dw_ref[...] = dw
        db_ref[...] = jnp.sum(d_ref[...], axis=0, keepdims=True)
        _acc(dc_ref, dcs, first)

    return pl.pallas_call(
        body, grid=(L, Wc // tn),
        in_specs=[pl.BlockSpec((NC, D), lambda l, j: (0, 0)), pl.BlockSpec((None, D, tn), lambda l, j: (l, 0, j)),
                  pl.BlockSpec((None, NC, tn), lambda l, j: (l, 0, j))],
        out_specs=[pl.BlockSpec((None, D, tn), lambda l, j: (l, 0, j)), pl.BlockSpec((None, 1, tn), lambda l, j: (l, 0, j)),
                   pl.BlockSpec((NC, D), lambda l, j: (0, 0))],
        out_shape=[jax.ShapeDtypeStruct((L, D, Wc), F32), jax.ShapeDtypeStruct((L, 1, Wc), F32),
                   jax.ShapeDtypeStruct((NC, D), F32)],
        compiler_params=_cparams(("arbitrary", "arbitrary")), name=name)(c_all, w, dmod)


def cctx_grad(dcs_twice, c_ctx, name):
    def body(d_ref, c_ref, o_ref):
        _, vjp = jax.vjp(_silu, c_ref[...])
        o_ref[...] = vjp(0.5 * d_ref[...])[0]

    D = c_ctx.shape[1]
    return pl.pallas_call(body, out_shape=jax.ShapeDtypeStruct((8, D), F32), name=name)(dcs_twice, c_ctx)


def _rope_tables(SEQ):
    t = jnp.arange(SEQ)
    row, col = (t // GRID_W).astype(F32), (t % GRID_W).astype(F32)
    n_freq = 16
    freqs = ROPE_THETA ** (-jnp.arange(n_freq, dtype=F32) / n_freq)
    ang = jnp.concatenate([row[:, None] * freqs, col[:, None] * freqs], axis=-1)
    cos, sin = jnp.cos(ang), jnp.sin(ang)
    c = jnp.tile(jnp.concatenate([cos, cos], axis=1), (1, 4))
    s = jnp.tile(jnp.concatenate([-sin, sin], axis=1), (1, 4))
    return (jnp.concatenate([c, jnp.ones((T, 256), F32)], axis=0), jnp.concatenate([s, jnp.zeros((T, 256), F32)], axis=0))


def _unshard_cols(g):
    return jnp.transpose(g, (1, 2, 0, 3)).reshape(g.shape[1], g.shape[2], 4 * g.shape[3])


def _unshard_rows(g):
    return jnp.transpose(g, (1, 0, 2, 3)).reshape(g.shape[1], 4 * g.shape[2], g.shape[3])


def _shard_cols(w, n=4):
    L, R, C = w.shape
    return jnp.transpose(w.reshape(L, R, n, C // n), (2, 0, 1, 3))


def _shard_rows(w):
    L, R, C = w.shape
    return jnp.transpose(w.reshape(L, 4, R // 4, C), (1, 0, 2, 3))


def _ab_in_permute(w):
    L, D, _ = w.shape
    return jnp.concatenate([w[..., 0:256], w[..., 320:1600], w[..., 256:320], jnp.zeros((L, D, 64), w.dtype)], axis=-1)


def _ab_in_unpermute(g):
    return jnp.concatenate([g[..., 0:256], g[..., 1536:1600], g[..., 256:1536]], axis=-1)


def _split_heads(w, a):
    L, K, N = w.shape
    w4 = w.reshape(L, K, 4, N // 4)
    return w4[..., :a].reshape(L, K, 4 * a), w4[..., a:].reshape(L, K, N - 4 * a)


def _join_heads(p, q):
    L, K = p.shape[:2]
    return jnp.concatenate([p.reshape(L, K, 4, -1), q.reshape(L, K, 4, -1)], axis=-1).reshape(L, K, -1)


def _pack(arrs):
    parts = []
    for a in arrs:
        f = a.reshape(-1).astype(F32)
        parts.append(jnp.pad(f, (0, (-f.shape[0]) % 1024)))
    return jnp.concatenate(parts).reshape(-1, 128)


def _unpack(buf, like):
    flat, out, off = buf.reshape(-1), [], 0
    for a in like:
        n = math.prod(a.shape)
        out.append(flat[off:off + n].reshape(a.shape))
        off += n + (-n) % 1024
    return out


_SMALL = ("c_ctx", "ada_b", "norm_mix", "norm_ffn", "norm_final", "mla_q_norm", "mla_kv_norm", "cmlp_v_norm", "cmlp_ws",
          "cmlp_bs", "ret_decay_fwd", "ret_decay_bwd", "ret_norm", "swa_sink")
_BIG = ("ffn_in", "ffn_out", "ab_in", "ab_out", "mla_wq_b", "mla_wkv_b", "cd_in", "cd_out")
_WEIGHTS = ("c_ctx", "ada_w", "ada_b", "norm_mix", "norm_ffn", "norm_final", "ffn_in", "ffn_out", "ab_in", "ab_out",
            "mla_q_norm", "mla_kv_norm", "mla_wq_b", "mla_wkv_b", "cmlp_v_norm", "cmlp_ws", "cmlp_bs", "cd_in", "cd_out",
            "ret_decay_fwd", "ret_decay_bwd", "ret_norm", "swa_sink")


def kernel(x, c, ctx, c_ctx, ada_w, ada_b, norm_mix, norm_ffn, norm_final, ffn_in, ffn_out, ab_in, ab_out, mla_q_norm, mla_kv_norm, mla_wq_b, mla_wkv_b, cmlp_v_norm, cmlp_ws, cmlp_bs, cd_in, cd_out, ret_decay_fwd, ret_decay_bwd, ret_norm, swa_sink, loss_target, m_c_ctx, m_ada_w, m_ada_b, m_norm_mix, m_norm_ffn, m_norm_final, m_ffn_in, m_ffn_out, m_ab_in, m_ab_out, m_mla_q_norm, m_mla_kv_norm, m_mla_wq_b, m_mla_wkv_b, m_cmlp_v_norm, m_cmlp_ws, m_cmlp_bs, m_cd_in, m_cd_out, m_ret_decay_fwd, m_ret_decay_bwd, m_ret_norm, m_swa_sink, v_c_ctx, v_ada_w, v_ada_b, v_norm_mix, v_norm_ffn, v_norm_final, v_ffn_in, v_ffn_out, v_ab_in, v_ab_out, v_mla_q_norm, v_mla_kv_norm, v_mla_wq_b, v_mla_wkv_b, v_cmlp_v_norm, v_cmlp_ws, v_cmlp_bs, v_cd_in, v_cd_out, v_ret_decay_fwd, v_ret_decay_bwd, v_ret_norm, v_swa_sink):
    W = dict(c_ctx=c_ctx, ada_w=ada_w, ada_b=ada_b, norm_mix=norm_mix, norm_ffn=norm_ffn, norm_final=norm_final, ffn_in=ffn_in, ffn_out=ffn_out, ab_in=ab_in, ab_out=ab_out, mla_q_norm=mla_q_norm, mla_kv_norm=mla_kv_norm, mla_wq_b=mla_wq_b, mla_wkv_b=mla_wkv_b, cmlp_v_norm=cmlp_v_norm, cmlp_ws=cmlp_ws, cmlp_bs=cmlp_bs, cd_in=cd_in, cd_out=cd_out, ret_decay_fwd=ret_decay_fwd, ret_decay_bwd=ret_decay_bwd, ret_norm=ret_norm, swa_sink=swa_sink)
    M1 = dict(c_ctx=m_c_ctx, ada_w=m_ada_w, ada_b=m_ada_b, norm_mix=m_norm_mix, norm_ffn=m_norm_ffn, norm_final=m_norm_final, ffn_in=m_ffn_in, ffn_out=m_ffn_out, ab_in=m_ab_in, ab_out=m_ab_out, mla_q_norm=m_mla_q_norm, mla_kv_norm=m_mla_kv_norm, mla_wq_b=m_mla_wq_b, mla_wkv_b=m_mla_wkv_b, cmlp_v_norm=m_cmlp_v_norm, cmlp_ws=m_cmlp_ws, cmlp_bs=m_cmlp_bs, cd_in=m_cd_in, cd_out=m_cd_out, ret_decay_fwd=m_ret_decay_fwd, ret_decay_bwd=m_ret_decay_bwd, ret_norm=m_ret_norm, swa_sink=m_swa_sink)
    M2 = dict(c_ctx=v_c_ctx, ada_w=v_ada_w, ada_b=v_ada_b, norm_mix=v_norm_mix, norm_ffn=v_norm_ffn, norm_final=v_norm_final, ffn_in=v_ffn_in, ffn_out=v_ffn_out, ab_in=v_ab_in, ab_out=v_ab_out, mla_q_norm=v_mla_q_norm, mla_kv_norm=v_mla_kv_norm, mla_wq_b=v_mla_wq_b, mla_wkv_b=v_mla_wkv_b, cmlp_v_norm=v_cmlp_v_norm, cmlp_ws=v_cmlp_ws, cmlp_bs=v_cmlp_bs, cd_in=v_cd_in, cd_out=v_cd_out, ret_decay_fwd=v_ret_decay_fwd, ret_decay_bwd=v_ret_decay_bwd, ret_norm=v_ret_norm, swa_sink=v_swa_sink)

    B, SEQ, D = x.shape
    CTX = ctx.shape[1]
    lay = Layout(B, SEQ, CTX, D)
    nt, NX = lay.nt, lay.NX
    ix, iy, ic = lax.axis_index("x"), lax.axis_index("y"), lax.axis_index("c")
    chip, me = 2 * ix + iy, 4 * ix + 2 * iy + ic
    cidx = jnp.reshape(ic, (1,)).astype(jnp.int32)
    Wc = ada_w.shape[2]
    n_even, n_odd = ab_in.shape[0], cd_in.shape[0]

    rn_row = jnp.pad(ret_norm.reshape(1, -1), ((0, 0), (0, D - ret_norm.size)))
    pack0 = jnp.concatenate([c, rn_row, jnp.zeros((8 - (B + 1) % 8, D), F32)], axis=0) if (B + 1) % 8 else jnp.concatenate([c, rn_row], axis=0)
    g0 = gather8(pack0, "gather_cond")
    NC = -(-(8 * B + 1) // 16) * 16
    c_all = jnp.concatenate([g0[:, :B].reshape(8 * B, D), c_ctx[None], jnp.zeros((NC - 8 * B - 1, D), F32)], axis=0)
    rn_sh = ret_norm.shape[1]
    ret_norm_full = jnp.transpose(g0[0::2, B, :ret_norm.size].reshape(4, n_odd, rn_sh), (1, 0, 2)).reshape(n_odd, 4 * rn_sh)

    gw = gather_weights([W[n].astype(BF16) for n in _BIG], "gather_weights")
    w_ffn_in, w_ab_in, w_wq, w_wkv, w_cd_in = (_unshard_cols(gw[i]) for i in (0, 2, 4, 5, 6))
    w_ffn_out, w_ab_out, w_cd_out = (_unshard_rows(gw[i]) for i in (1, 3, 7))
    w_ab_in = _ab_in_permute(w_ab_in)
    w_qn, w_qp = _split_heads(w_wq, MLA_NOPE)
    w_k, w_v = _split_heads(w_wkv, MLA_NOPE)

    ab_sh = lax.dynamic_slice_in_dim(ada_b, chip * Wc, Wc, axis=1)[:, None, :]
    mod_sh = ada_fwd(c_all, ada_w, ab_sh, "ada_fwd")
    mod_all = _unshard_cols(gather_chips(mod_sh, "gather_mod"))
    mod_mine = jnp.concatenate([lax.dynamic_slice_in_dim(mod_all, me * B, B, axis=1), mod_all[:, 8 * B:8 * B + 1]], axis=1)
    mod = mod_mine.reshape(DEPTH, B + 1, 6, D)

    tabc, tabs = _rope_tables(SEQ)
    bc8 = lambda a: jnp.broadcast_to(a.reshape(a.shape + (1, 1)), a.shape + (8, 128))
    row = lambda a: a.reshape(1, -1)

    S = jnp.concatenate([x.reshape(NX, D), ctx.reshape(B * CTX, D)], axis=0)
    saved = []
    for l in range(DEPTH):
        j, even = l // 2, l % 2 == 0
        xn = norm_mod_fwd(S, row(norm_mix[l]), mod[l], 0, lay, nt, f"norm_mix_fwd{l}")
        if even:
            z = mm(xn, w_ab_in[j], name=f"ab_in{l}")
            q, k, v = ab_prep_fwd(z, tabc, tabs, row(mla_kv_norm[j]), row(mla_q_norm[j]), w_k[j], w_v[j], w_qn[j], w_qp[j],
                                  lay, f"ab_prep{l}")
            o = mla_fwd(q, k, v, lay, f"mla{l}")
            merged = cmlp_merge_fwd(z, o, row(cmlp_v_norm[j]), cmlp_ws[j], cmlp_bs[j][:, :, None], lay, f"cmlp{l}")
            w_out, mix = w_ab_out[j], (q, k, v)
        else:
            z = mm(xn, w_cd_in[j], name=f"cd_in{l}")
            rq, rk, sq, ke, ve = cd_prep_fwd(z, tabc, tabs, lay, f"cd_prep{l}")
            decf, decb, sink = bc8(ret_decay_fwd[j]), bc8(ret_decay_bwd[j]), bc8(swa_sink[j].reshape(SWA_KV_HEADS, SWA_GROUPS))
            yret = ret_fwd(rq, rk, z, decf, decb, lay, f"ret{l}")
            osw = swa_fwd(sq, ke, ve, sink, lay, f"swa{l}")
            merged = cd_merge_fwd(yret, z, osw, row(ret_norm_full[j]), lay, f"cd_merge{l}")
            w_out, mix = w_cd_out[j], (rq, rk, sq, ke, ve, decf, decb, sink, yret)
        y, S_mid = mm_gated(merged, w_out, S, mod[l], 2, lay, name=f"mix_out{l}", n_tiles=nt)
        xn2 = norm_mod_fwd(S_mid, row(norm_ffn[l]), mod[l], 3, lay, nt, f"norm_ffn_fwd{l}")
        act, fa, fb = ffn_in_act(xn2, w_ffn_in[l], f"ffn_in{l}")
        f, S_new = mm_gated(act, w_ffn_out[l], S_mid, mod[l], 5, lay, name=f"ffn_out{l}", n_tiles=nt)
        saved.append((S, xn, z, mix, merged, w_out, y, S_mid, xn2, (fa, fb), act, f))
        S = S_new

    loss_blk, dS, d_norm_final = loss_head(S, row(norm_final), loss_target.reshape(NX, D), lay, "loss_head")
    loss = lax.psum(loss_blk[0, 0], ("x", "y", "c"))

    G = {n: [None] * W[n].shape[0] for n in ("norm_mix", "norm_ffn", "mla_q_norm", "mla_kv_norm", "cmlp_v_norm", "cmlp_ws",
                                             "cmlp_bs", "ret_decay_fwd", "ret_decay_bwd", "ret_norm", "swa_sink")}
    GB = {n: [None] * cnt for n, cnt in (("ffn_in_a", DEPTH), ("ffn_in_b", DEPTH), ("ffn_out", DEPTH), ("ab_in", n_even), ("ab_out", n_even),
                                         ("wqn", n_even), ("wqp", n_even), ("wk", n_even), ("wv", n_even), ("cd_in", n_odd),
                                         ("cd_out", n_odd))}
    dmod = [None] * DEPTH
    for l in reversed(range(DEPTH)):
        j, even = l // 2, l % 2 == 0
        S_in, xn, z, mix, merged, w_out, y, S_mid, xn2, (fa, fb), act, f = saved[l]
        df, dgate2 = gate_bwd(dS, f, mod[l], 5, lay, nt, f"ffn_gate_bwd{l}")
        da, db = ffn_out_dx_act(df, w_ffn_out[l], fa, fb, f"ffn_out_dx{l}")
        GB["ffn_out"][l] = mm(act, df, ta=True, name=f"ffn_out_dw{l}")
        GB["ffn_in_a"][l] = mm(xn2, da, ta=True, name=f"ffn_in_dwa{l}")
        GB["ffn_in_b"][l] = mm(xn2, db, ta=True, name=f"ffn_in_dwb{l}")
        dxn2 = ffn_in_dx(da, db, w_ffn_in[l], f"ffn_in_dx{l}")
        dS_mid, dss2, dg = norm_mod_bwd(S_mid, row(norm_ffn[l]), mod[l], 3, dxn2, dS, lay, nt, f"norm_ffn_bwd{l}")
        G["norm_ffn"][l] = dg
        dy, dgate1 = gate_bwd(dS_mid, y, mod[l], 2, lay, nt, f"mix_gate_bwd{l}")
        dmerged = mm(dy, w_out, tb=True, name=f"mix_out_dx{l}")
        d_w_out = mm(merged, dy, ta=True, name=f"mix_out_dw{l}")
        if even:
            q, k, v = mix
            dq, dkx, dkh, dvx, dvh = mla_bwd(q, k, v, dmerged, lay, f"mla_bwd{l}")
            (dz, dgkv, dgq, dwk, dwv, dwqn, dwqp, dgvn, dws, dbs) = ab_rows_bwd(
                z, tabc, tabs, dq, dkx, dkh, dvx, dvh, dmerged, row(mla_kv_norm[j]), row(mla_q_norm[j]), w_k[j], w_v[j],
                w_qn[j], w_qp[j], row(cmlp_v_norm[j]), cmlp_ws[j], cmlp_bs[j][:, :, None], lay, f"ab_rows_bwd{l}")
            G["mla_kv_norm"][j], G["mla_q_norm"][j], G["cmlp_v_norm"][j] = dgkv, dgq, dgvn
            G["cmlp_ws"][j], G["cmlp_bs"][j] = dws, dbs
            GB["wk"][j], GB["wv"][j], GB["wqn"][j], GB["wqp"][j], GB["ab_out"][j] = dwk, dwv, dwqn, dwqp, d_w_out
            w_in = w_ab_in[j]
        else:
            rq, rk, sq, ke, ve, decf, decb, sink, yret = mix
            dyret, drg, dgn = cd_merge_bwd(yret, z, row(ret_norm_full[j]), dmerged, lay, f"cd_merge_bwd{l}")
            drq, dkx, dkh, dvx, dvh, ddf, ddb = ret_bwd(rq, rk, z, decf, decb, dyret, lay, f"ret_bwd{l}")
            dsq, dkex, dkeh, dvex, dveh, dsink = swa_bwd(sq, ke, ve, sink, dmerged, lay, f"swa_bwd{l}")
            dz = cd_rows_bwd(z, tabc, tabs, drq, dsq, drg, (dkx, dkh), (dkex, dkeh), (dvex, dveh), (dvx, dvh), lay,
                             f"cd_rows_bwd{l}")
            G["ret_norm"][j], G["ret_decay_fwd"][j], G["ret_decay_bwd"][j] = dgn, ddf[:, 0, 0], ddb[:, 0, 0]
            G["swa_sink"][j] = dsink[:, :, 0, 0].reshape(-1)
            GB["cd_out"][j] = d_w_out
            w_in = w_cd_in[j]
        GB["ab_in" if even else "cd_in"][j] = mm(xn, dz, ta=True, name=f"mix_in_dw{l}")
        dxn = mm(dz, w_in, tb=True, name=f"mix_in_dx{l}")
        dS, dss1, dg = norm_mod_bwd(S_in, row(norm_mix[l]), mod[l], 0, dxn, dS_mid, lay, nt, f"norm_mix_bwd{l}")
        G["norm_mix"][l] = dg
        dmod[l] = jnp.concatenate([dss1, dgate1, dss2, dgate2], axis=1)
    grad_x = dS[:NX].reshape(B, SEQ, D)

    st = lambda n: jnp.stack([g.reshape((4 * rn_sh,) if n == "ret_norm" else W[n].shape[1:]) for g in G[n]])
    small_parts = {n: st(n) for n in G}
    small_parts["norm_final"] = d_norm_final.reshape(-1)
    dmod_local = jnp.stack(dmod).reshape(DEPTH, B + 1, 6 * D)
    names1 = ["norm_mix", "norm_ffn", "norm_final", "mla_q_norm", "mla_kv_norm", "cmlp_v_norm", "cmlp_ws", "cmlp_bs",
              "ret_decay_fwd", "ret_decay_bwd", "ret_norm", "swa_sink"]
    like1 = [dmod_local] + [small_parts[n] for n in names1]
    g1 = gather8(_pack(like1), "gather_small_grads")
    tot1 = _unpack(sum8(g1, "sum_small_grads"), like1)
    sg = dict(zip(names1, tot1[1:]))
    n_dm = math.prod(dmod_local.shape)
    dm_each = g1.reshape(8, -1)[:, :n_dm].reshape(8, DEPTH, B + 1, 6 * D)
    dmod_all = jnp.concatenate([jnp.transpose(dm_each[:, :, :B], (1, 0, 2, 3)).reshape(DEPTH, 8 * B, 6 * D),
                                tot1[0][:, B:B + 1], jnp.zeros((DEPTH, NC - 8 * B - 1, 6 * D), F32)], axis=1)
    dmod_sh = lax.dynamic_slice_in_dim(dmod_all, chip * Wc, Wc, axis=2)
    g_ada_w, g_ada_b_sh, dcs = ada_bwd(c_all, ada_w, dmod_sh, "ada_bwd")
    like2 = [dcs[8 * B], g_ada_b_sh]
    g2 = gather8(_pack(like2), "gather_ada_grads")
    tot2 = _unpack(sum8(g2, "sum_ada_grads"), like2)
    bc = lambda a: jnp.broadcast_to(a.reshape(1, D), (8, D))
    sg["c_ctx"] = cctx_grad(bc(tot2[0]), bc(c_ctx), "c_ctx_grad")[0]
    off = D + (-D) % 1024
    gab = g2.reshape(8, -1)[0::2, off:off + DEPTH * Wc].reshape(4, DEPTH, Wc)
    sg["ada_b"] = jnp.transpose(gab, (1, 0, 2)).reshape(DEPTH, 4 * Wc)
    sg["ret_norm"] = lax.dynamic_slice_in_dim(sg["ret_norm"], chip * rn_sh, rn_sh, axis=1)

    stack = lambda n: jnp.stack(GB[n])
    gs = {"ffn_in": jnp.concatenate([_shard_cols(stack("ffn_in_a"), 2), _shard_cols(stack("ffn_in_b"), 2)], axis=0),
          "ffn_out": _shard_rows(stack("ffn_out")),
          "ab_in": _shard_cols(_ab_in_unpermute(stack("ab_in"))), "ab_out": _shard_rows(stack("ab_out")),
          "mla_wq_b": _shard_cols(_join_heads(stack("wqn"), stack("wqp"))),
          "mla_wkv_b": _shard_cols(_join_heads(stack("wk"), stack("wv"))),
          "cd_in": _shard_cols(stack("cd_in")), "cd_out": _shard_rows(stack("cd_out"))}
    bufs = swap_other_half([gs[n] for n in _BIG], "swap_core_halves")
    parts = [chip_partial(gs[n], b, cidx, f"chip_partial_{n}") for n, b in zip(_BIG, bufs)]
    arrived = exchange_chips(parts, "exchange_chips")
    halves = []
    for n, p in zip(_BIG, arrived):
        halves += adam_sharded(p, W[n], M1[n], M2[n], cidx, f"adam_{n}")
    full = share_halves(halves, "share_core_halves")
    out = {n: tuple(full[4 * i:4 * i + 4]) for i, n in enumerate(_BIG)}

    like_s = [W[n] for n in _SMALL]
    dsm, msm, vsm = adam_rows(_pack(like_s), _pack([sg[n].reshape(W[n].shape) for n in _SMALL]), _pack([M1[n] for n in _SMALL]),
                                _pack([M2[n] for n in _SMALL]), "adam_small")
    for n, d_, m_, v_ in zip(_SMALL, _unpack(dsm, like_s), _unpack(msm, like_s), _unpack(vsm, like_s)):
        out[n] = (sg[n].reshape(W[n].shape), d_, m_, v_)
    flat2 = lambda a: a.reshape(-1, a.shape[-1])
    d_, m_, v_ = adam_rows(flat2(ada_w), flat2(g_ada_w), flat2(m_ada_w), flat2(v_ada_w), "adam_ada_w")
    out["ada_w"] = (g_ada_w, d_.reshape(ada_w.shape), m_.reshape(ada_w.shape), v_.reshape(ada_w.shape))

    return (loss, grad_x, *[out[n][0] for n in _WEIGHTS], *[out[n][1] for n in _WEIGHTS], *[out[n][2] for n in _WEIGHTS],
            *[out[n][3] for n in _WEIGHTS])
```

```python
import functools
import math

import jax
import jax.numpy as jnp
import numpy as np
from jax import lax
from jax.experimental import pallas as pl
from jax.experimental.pallas import tpu as pltpu

F32 = jnp.float32
BF16 = jnp.bfloat16
EPS = 1e-6
NEG_INF = -1e30
GRID_W = 64
ROPE_THETA = 10000.0
DEPTH = 4
MLA_HEADS, MLA_Q_LORA, MLA_KV_LORA, MLA_NOPE, MLA_ROPE, MLA_V = 4, 256, 256, 128, 64, 128
CMLP_GROUPS, CMLP_CHUNK = 4, 128
CMLP_WIDTH = 512
RET_HEADS, RET_QK, RET_V = 4, 64, 128
SWA_Q_HEADS, SWA_KV_HEADS, SWA_HEAD_DIM, SWA_WINDOW = 8, 2, 64, 128
SWA_GROUPS = SWA_Q_HEADS // SWA_KV_HEADS
AB_IN_P = 1664
ADAM_LR, ADAM_B1, ADAM_B2, ADAM_EPS, ADAM_WD, ADAM_STEP = 0.001, 0.9, 0.999, 1e-08, 0.01, 10

T = 256
SWA_SPAN = T + 2 * SWA_WINDOW
VMEM_LIMIT = 48 * 1024 * 1024
MESH = pl.DeviceIdType.MESH
ANY = pl.BlockSpec(memory_space=pl.ANY)


def _cparams(sem):
    return pltpu.CompilerParams(dimension_semantics=sem, vmem_limit_bytes=VMEM_LIMIT)


@functools.cache
def _bdot_fn(ca, cb):
    fa, fb = 1 - ca, 1 - cb

    def dg(p, q, cp, cq):
        return lax.dot_general(p.astype(BF16), q.astype(BF16), (((cp,), (cq,)), ((), ())), preferred_element_type=F32)

    @jax.custom_vjp
    def bd(a, b):
        return dg(a, b, ca, cb)

    def fwd(a, b):
        return dg(a, b, ca, cb), (a, b)

    def bwd(res, g):
        a, b = res
        da = dg(g, b, 1, fb) if ca == 1 else dg(b, g, fb, 1)
        db = dg(a, g, fa, 0) if cb == 0 else dg(g, a, 0, fa)
        return da, db

    bd.defvjp(fwd, bwd)
    return bd


def bdot(a, b):
    return _bdot_fn(1, 0)(a, b)


def bdot_nt(a, b):
    return _bdot_fn(1, 1)(a, b)


def bdot_tn(a, b):
    return _bdot_fn(0, 0)(a, b)


def _swap32(x):
    w = x.shape[-1]
    lane = lax.broadcasted_iota(jnp.int32, x.shape, 1)
    return jnp.where((lane & 32) == 0, pltpu.roll(x, w - 32, 1), pltpu.roll(x, 32, 1))


@jax.custom_vjp
def rope(x, c, s):
    return x * c + _swap32(x) * s


def _rope_fwd(x, c, s):
    return rope(x, c, s), (c, s)


def _rope_bwd(res, g):
    c, s = res
    return g * c + _swap32(g * s), jnp.zeros_like(c), jnp.zeros_like(s)


rope.defvjp(_rope_fwd, _rope_bwd)


def rms(x, g):
    return x * lax.rsqrt(jnp.mean(x * x, axis=-1, keepdims=True) + EPS) * g


def normmod(x, g, sh, sc):
    return rms(x, g) * (1.0 + sc) + sh


def log_sigmoid(x):
    return jnp.minimum(x, 0.0) - jnp.log(1.0 + jnp.exp(-jnp.abs(x)))


def _head_mask(shape, h):
    lane = lax.broadcasted_iota(jnp.int32, shape, 1)
    return ((lane >> 6) == h).astype(F32)


def _fold_matrix():
    i = lax.broadcasted_iota(jnp.int32, (256, 128), 0)
    j = lax.broadcasted_iota(jnp.int32, (256, 128), 1)
    return ((i & 63) == j).astype(F32)


def _expand_matrix(g):
    i = lax.broadcasted_iota(jnp.int32, (128, 256), 0)
    j = lax.broadcasted_iota(jnp.int32, (128, 256), 1)
    return (i == (j & 63) + 64 * g).astype(F32)


def _acc(ref, val, first):
    @pl.when(first)
    def _():
        ref[...] = val

    @pl.when(jnp.logical_not(first))
    def _():
        ref[...] += val


def _pick(n, cap, mult):
    best = None
    for d in range(mult, min(n, cap) + 1, mult):
        if n % d == 0:
            best = d
    return best if best is not None else n


def mm(a, b, *, ta=False, tb=False, name):
    M, K = (a.shape[1], a.shape[0]) if ta else a.shape
    N = b.shape[0] if tb else b.shape[1]
    if ta:
        tn = N
        tm = _pick(M, min(1536, (3 << 20) // tn), 128)
    else:
        tn = _pick(N, 768, 128)
        if tn < 256 and N <= 2304:
            tn = N
        tm = _pick(M, min(1536, (1 << 20) // tn), 128)
    tk = _pick(K, 2048 if not ta else 1024, 128) if K > 2816 or ta else K
    nk = K // tk
    grid = (M // tm, N // tn, nk)
    a_spec = pl.BlockSpec((tk, tm), lambda i, j, k: (k, i)) if ta else pl.BlockSpec((tm, tk), lambda i, j, k: (i, k))
    b_spec = pl.BlockSpec((tn, tk), lambda i, j, k: (j, k)) if tb else pl.BlockSpec((tk, tn), lambda i, j, k: (k, j))
    dims = (((0 if ta else 1,), (1 if tb else 0,)), ((), ()))

    def body(a_ref, b_ref, o_ref):
        part = lax.dot_general(a_ref[...], b_ref[...], dims, preferred_element_type=F32)
        if nk == 1:
            o_ref[...] = part
        else:
            _acc(o_ref, part, pl.program_id(2) == 0)

    return pl.pallas_call(
        body, grid=grid, in_specs=[a_spec, b_spec], out_specs=pl.BlockSpec((tm, tn), lambda i, j, k: (i, j)),
        out_shape=jax.ShapeDtypeStruct((M, N), F32),
        compiler_params=_cparams(("parallel", "parallel", "arbitrary")), name=name)(a, b)


def mm_gated(a, b, res, mod, gate_row, lay, *, name, n_tiles):
    K, N = b.shape
    M = n_tiles * T
    tn = _pick(N, 1024, 128)

    def body(a_ref, b_ref, r_ref, g_ref, y_ref, o_ref):
        y = lax.dot_general(a_ref[...], b_ref[...], (((1,), (0,)), ((), ())), preferred_element_type=F32)
        y_ref[...] = y
        o_ref[...] = r_ref[...] + g_ref[gate_row:gate_row + 1, :] * y

    return pl.pallas_call(
        body, grid=(n_tiles, N // tn),
        in_specs=[pl.BlockSpec((T, K), lambda i, j: (i, 0)), pl.BlockSpec((K, tn), lambda i, j: (0, j)),
                  pl.BlockSpec((T, tn), lambda i, j: (i, j)),
                  pl.BlockSpec((None, 6, tn), lambda i, j: (lay.mod_idx(i), 0, j))],
        out_specs=[pl.BlockSpec((T, tn), lambda i, j: (i, j)), pl.BlockSpec((T, tn), lambda i, j: (i, j))],
        out_shape=[jax.ShapeDtypeStruct((M, N), F32), jax.ShapeDtypeStruct((M, N), F32)],
        compiler_params=_cparams(("parallel", "parallel")), name=name)(a, b, res, mod)


class Layout:
    def __init__(self, B, SEQ, CTX, D):
        assert CTX == T and SEQ % T == 0 and SEQ >= SWA_SPAN
        self.B, self.SEQ, self.CTX, self.D = B, SEQ, CTX, D
        self.tps = SEQ // T
        self.nxt = B * self.tps
        self.nt = self.nxt + B
        self.NX, self.R = B * SEQ, B * SEQ + B * CTX
        self.nq = self.tps + 1

    def mod_idx(self, i):
        return jnp.where(i < self.nxt, i // self.tps, self.B)

    def rope_idx(self, i):
        return jnp.where(i < self.nxt, i % self.tps, self.tps)

    def first_of_mod(self, i):
        return jnp.logical_or(jnp.logical_and(i < self.nxt, i % self.tps == 0), i == self.nxt)

    def qrow(self, b, qi):
        return jnp.where(qi < self.tps, b * self.tps + qi, self.nxt + b)


def _row(w, col=0):
    return pl.BlockSpec((T, w), lambda i: (i, col))


def _full(shape):
    nd = len(shape)
    return pl.BlockSpec(shape, lambda i: (0,) * nd)


def _modspec(lay, rows, D):
    return pl.BlockSpec((None, rows, D), lambda i: (lay.mod_idx(i), 0, 0))


def _ropespec(lay, w):
    return pl.BlockSpec((T, w), lambda i: (lay.rope_idx(i), 0))


def _xh_specs(lay, w):
    nxt = lay.nxt
    return [pl.BlockSpec((T, w), lambda i: (jnp.minimum(i, nxt - 1), 0)), pl.BlockSpec((T, w), lambda i: (jnp.maximum(i - nxt, 0), 0))]


def _xh_pick(lay, x_ref, h_ref):
    return jnp.where(pl.program_id(0) < lay.nxt, x_ref[...], h_ref[...])


def norm_mod_fwd(S, g, mod, k0, lay, n_tiles, name):
    D = S.shape[1]

    def body(s_ref, g_ref, mod_ref, o_ref):
        o_ref[...] = normmod(s_ref[...], g_ref[...], mod_ref[k0:k0 + 1, :], mod_ref[k0 + 1:k0 + 2, :]).astype(BF16)

    return pl.pallas_call(
        body, grid=(n_tiles,), in_specs=[_row(D), _full((1, D)), _modspec(lay, 6, D)], out_specs=_row(D),
        out_shape=jax.ShapeDtypeStruct((n_tiles * T, D), BF16), compiler_params=_cparams(("parallel",)), name=name)(S, g, mod)


def norm_mod_bwd(S, g, mod, k0, dxn, ds_in, lay, n_tiles, name):
    D = S.shape[1]

    def body(s_ref, g_ref, mod_ref, dxn_ref, dsin_ref, ds_ref, dss_ref, dg_ref):
        i = pl.program_id(0)
        _, vjp = jax.vjp(normmod, s_ref[...], g_ref[...], mod_ref[k0:k0 + 1, :], mod_ref[k0 + 1:k0 + 2, :])
        dx, dg, dsh, dsc = vjp(dxn_ref[...])
        ds_ref[...] = dsin_ref[...] + dx
        _acc(dg_ref, dg, i == 0)
        _acc(dss_ref, jnp.concatenate([dsh, dsc], axis=0), lay.first_of_mod(i))

    return pl.pallas_call(
        body, grid=(n_tiles,), in_specs=[_row(D), _full((1, D)), _modspec(lay, 6, D), _row(D), _row(D)],
        out_specs=[_row(D), _modspec(lay, 2, D), _full((1, D))],
        out_shape=[jax.ShapeDtypeStruct((n_tiles * T, D), F32), jax.ShapeDtypeStruct((lay.B + 1, 2, D), F32),
                   jax.ShapeDtypeStruct((1, D), F32)],
        compiler_params=_cparams(("arbitrary",)), name=name)(S, g, mod, dxn, ds_in)


def gate_bwd(dS, y, mod, gate_row, lay, n_tiles, name):
    D = dS.shape[1]

    def body(ds_ref, y_ref, mod_ref, dy_ref, dgate_ref):
        i = pl.program_id(0)
        ds = ds_ref[...]
        dy_ref[...] = (mod_ref[gate_row:gate_row + 1, :] * ds).astype(BF16)
        _acc(dgate_ref, jnp.sum(ds * y_ref[...], axis=0, keepdims=True), lay.first_of_mod(i))

    return pl.pallas_call(
        body, grid=(n_tiles,), in_specs=[_row(D), _row(D), _modspec(lay, 6, D)],
        out_specs=[_row(D), _modspec(lay, 1, D)],
        out_shape=[jax.ShapeDtypeStruct((n_tiles * T, D), BF16), jax.ShapeDtypeStruct((lay.B + 1, 1, D), F32)],
        compiler_params=_cparams(("arbitrary",)), name=name)(dS, y, mod)


def _swiglu(a, b):
    return a * jax.nn.sigmoid(a) * b


def _ffn_tiles(M, F):
    tn = _pick(F, 1408, 128)
    return _pick(M, (3 << 18) // tn, 128), tn


def ffn_in_act(x, w, name):
    M, D = x.shape
    F = w.shape[1] // 2
    tm, tn = _ffn_tiles(M, F)
    nj = F // tn

    def body(x_ref, wa_ref, wb_ref, act_ref, a_ref, b_ref):
        dims = (((1,), (0,)), ((), ()))
        a = lax.dot_general(x_ref[...], wa_ref[...], dims, preferred_element_type=F32)
        b = lax.dot_general(x_ref[...], wb_ref[...], dims, preferred_element_type=F32)
        act_ref[...] = _swiglu(a, b).astype(BF16)
        a_ref[...] = a.astype(BF16)
        b_ref[...] = b.astype(BF16)

    out = pl.BlockSpec((tm, tn), lambda i, j: (i, j))
    return pl.pallas_call(
        body, grid=(M // tm, nj),
        in_specs=[pl.BlockSpec((tm, D), lambda i, j: (i, 0)), pl.BlockSpec((D, tn), lambda i, j: (0, j)),
                  pl.BlockSpec((D, tn), lambda i, j: (0, j + nj))],
        out_specs=[out, out, out], out_shape=[jax.ShapeDtypeStruct((M, F), BF16)] * 3,
        compiler_params=_cparams(("parallel", "parallel")), name=name)(x, w, w)


def ffn_out_dx_act(df, wt, a, b, name):
    M, D = df.shape
    F = wt.shape[1]
    tm, tn = _ffn_tiles(M, F)

    def body(df_ref, w_ref, a_ref, b_ref, da_ref, db_ref):
        dact = lax.dot_general(df_ref[...], w_ref[...], (((1,), (0,)), ((), ())), preferred_element_type=F32)
        _, vjp = jax.vjp(_swiglu, a_ref[...].astype(F32), b_ref[...].astype(F32))
        da, db = vjp(dact)
        da_ref[...] = da.astype(BF16)
        db_ref[...] = db.astype(BF16)

    blk = pl.BlockSpec((tm, tn), lambda i, j: (i, j))
    return pl.pallas_call(
        body, grid=(M // tm, F // tn),
        in_specs=[pl.BlockSpec((tm, D), lambda i, j: (i, 0)), pl.BlockSpec((D, tn), lambda i, j: (0, j)), blk, blk],
        out_specs=[blk, blk], out_shape=[jax.ShapeDtypeStruct((M, F), BF16)] * 2,
        compiler_params=_cparams(("parallel", "parallel")), name=name)(df, wt, a, b)


def ffn_in_dx(da, db, w, name):
    M, F = da.shape
    D = w.shape[0]
    tm, tn = _pick(M, 512, 128), _pick(D, 512, 128)

    def body(da_ref, db_ref, wa_ref, wb_ref, o_ref):
        dims = (((1,), (1,)), ((), ()))
        o_ref[...] = (lax.dot_general(da_ref[...], wa_ref[...], dims, preferred_element_type=F32)
                      + lax.dot_general(db_ref[...], wb_ref[...], dims, preferred_element_type=F32))

    return pl.pallas_call(
        body, grid=(M // tm, D // tn),
        in_specs=[pl.BlockSpec((tm, F), lambda i, j: (i, 0)), pl.BlockSpec((tm, F), lambda i, j: (i, 0)),
                  pl.BlockSpec((tn, F), lambda i, j: (j, 0)), pl.BlockSpec((tn, F), lambda i, j: (j, 1))],
        out_specs=pl.BlockSpec((tm, tn), lambda i, j: (i, j)), out_shape=jax.ShapeDtypeStruct((M, D), F32),
        compiler_params=_cparams(("parallel", "parallel")), name=name)(da, db, w, w)


def loss_head(S, g, target, lay, name):
    D = S.shape[1]
    nxt = lay.nxt

    def tile_loss(x, gg, t):
        err = rms(x, gg) - t
        return 0.5 * jnp.sum(jnp.mean(err * err, axis=-1))

    def body(s_ref, g_ref, t_ref, loss_ref, ds_ref, dg_ref):
        i = pl.program_id(0)

        @pl.when(i < nxt)
        def _():
            val, vjp = jax.vjp(tile_loss, s_ref[...], g_ref[...], t_ref[...])
            dx, dg, _ = vjp(jnp.ones((), F32))
            ds_ref[...] = dx
            _acc(dg_ref, dg, i == 0)
            _acc(loss_ref, jnp.full((8, 128), val, F32), i == 0)

        @pl.when(i >= nxt)
        def _():
            ds_ref[...] = jnp.zeros((T, D), F32)

    return pl.pallas_call(
        body, grid=(lay.nt,),
        in_specs=[_row(D), _full((1, D)), pl.BlockSpec((T, D), lambda i: (jnp.minimum(i, nxt - 1), 0))],
        out_specs=[_full((8, 128)), _row(D), _full((1, D))],
        out_shape=[jax.ShapeDtypeStruct((8, 128), F32), jax.ShapeDtypeStruct((lay.R, D), F32),
                   jax.ShapeDtypeStruct((1, D), F32)],
        compiler_params=_cparams(("arbitrary",)), name=name)(S, g, target)


def _ab_prep(zkv, zq, zpe, c256, s256, c128, s128, gkv, gq, wk, wv, wqn, wqp):
    kvn = rms(zkv, gkv)
    kn, v = bdot(kvn, wk), bdot(kvn, wv)
    qn = rms(zq, gq)
    qnope, qpe = bdot(qn, wqn), rope(bdot(qn, wqp), c256, s256)
    kpe = rope(zpe, c128, s128)
    fold = _fold_matrix()
    qparts, kparts = [], []
    for h in range(MLA_HEADS):
        qparts += [qnope[:, 128 * h:128 * (h + 1)], bdot(qpe * _head_mask(qpe.shape, h), fold)]
        kparts += [kn[:, 128 * h:128 * (h + 1)], kpe]
    return jnp.concatenate(qparts, axis=1), jnp.concatenate(kparts, axis=1), v


def _ab_prep_specs(lay):
    return [_row(256, 0), _row(256, 1), _row(128, 12), _ropespec(lay, 256), _ropespec(lay, 256), _ropespec(lay, 128),
            _ropespec(lay, 128), _full((1, 256)), _full((1, 256)), _full((256, 512)), _full((256, 512)),
            _full((256, 512)), _full((256, 256))]


def ab_prep_fwd(z, tabc, tabs, gkv, gq, wk, wv, wqn, wqp, lay, name):
    def body(*refs):
        ins, (q_ref, k_ref, v_ref) = refs[:13], refs[13:]
        q, k, v = _ab_prep(*[r[...].astype(F32) for r in ins])
        q_ref[...] = q.astype(BF16)
        k_ref[...] = k.astype(BF16)
        v_ref[...] = v.astype(BF16)

    R = lay.R
    return pl.pallas_call(
        body, grid=(lay.nt,), in_specs=_ab_prep_specs(lay), out_specs=[_row(1024), _row(1024), _row(512)],
        out_shape=[jax.ShapeDtypeStruct((R, 1024), BF16), jax.ShapeDtypeStruct((R, 1024), BF16),
                   jax.ShapeDtypeStruct((R, 512), BF16)],
        compiler_params=_cparams(("parallel",)), name=name)(z, z, z, tabc, tabs, tabc, tabs, gkv, gq, wk, wv, wqn, wqp)


_MLA_SCALE = (MLA_NOPE + MLA_ROPE) ** -0.5


def _mla_x(q, kx, vx, kh, vh):
    sx, sh = bdot_nt(q, kx) * _MLA_SCALE, bdot_nt(q, kh) * _MLA_SCALE
    m = lax.stop_gradient(jnp.maximum(jnp.max(sx, axis=-1, keepdims=True), jnp.max(sh, axis=-1, keepdims=True)))
    ex, eh = jnp.exp(sx - m), jnp.exp(sh - m)
    inv = 1.0 / (jnp.sum(ex, axis=-1, keepdims=True) + jnp.sum(eh, axis=-1, keepdims=True))
    return bdot(ex * inv, vx) + bdot(eh * inv, vh)


def _mla_h(q, kh, vh):
    sh = bdot_nt(q, kh) * _MLA_SCALE
    eh = jnp.exp(sh - lax.stop_gradient(jnp.max(sh, axis=-1, keepdims=True)))
    return bdot(eh * (1.0 / jnp.sum(eh, axis=-1, keepdims=True)), vh)


def _mla_specs(lay):
    nxt, SEQ = lay.nxt, lay.SEQ
    return [pl.BlockSpec((T, 256), lambda b, h, qi: (lay.qrow(b, qi), h)),
            pl.BlockSpec((SEQ, 256), lambda b, h, qi: (b, h)), pl.BlockSpec((SEQ, 128), lambda b, h, qi: (b, h)),
            pl.BlockSpec((T, 256), lambda b, h, qi: (nxt + b, h)), pl.BlockSpec((T, 128), lambda b, h, qi: (nxt + b, h))]


def mla_fwd(q, k, v, lay, name):
    tps = lay.tps

    def body(q_ref, kx_ref, vx_ref, kh_ref, vh_ref, o_ref):
        qi = pl.program_id(2)
        f = lambda r: r[...]

        @pl.when(qi < tps)
        def _():
            o_ref[...] = _mla_x(f(q_ref), f(kx_ref), f(vx_ref), f(kh_ref), f(vh_ref)).astype(BF16)

        @pl.when(qi == tps)
        def _():
            o_ref[...] = _mla_h(f(q_ref), f(kh_ref), f(vh_ref)).astype(BF16)

    return pl.pallas_call(
        body, grid=(lay.B, MLA_HEADS, lay.nq), in_specs=_mla_specs(lay),
        out_specs=pl.BlockSpec((T, 128), lambda b, h, qi: (lay.qrow(b, qi), h)),
        out_shape=jax.ShapeDtypeStruct((lay.R, 512), BF16),
        compiler_params=_cparams(("parallel", "parallel", "arbitrary")), name=name)(q, k, v, k, v)


def mla_bwd(q, k, v, dmerged, lay, name):
    tps, SEQ, B = lay.tps, lay.SEQ, lay.B

    def body(q_ref, kx_ref, vx_ref, kh_ref, vh_ref, do_ref, dq_ref, dkx_ref, dkh_ref, dvx_ref, dvh_ref):
        qi = pl.program_id(2)
        f = lambda r: r[...].astype(F32)

        @pl.when(qi < tps)
        def _():
            _, vjp = jax.vjp(_mla_x, f(q_ref), f(kx_ref), f(vx_ref), f(kh_ref), f(vh_ref))
            dq, dkx, dvx, dkh, dvh = vjp(do_ref[...])
            dq_ref[...] = dq
            _acc(dkx_ref, dkx, qi == 0)
            _acc(dvx_ref, dvx, qi == 0)
            _acc(dkh_ref, dkh, qi == 0)
            _acc(dvh_ref, dvh, qi == 0)

        @pl.when(qi == tps)
        def _():
            _, vjp = jax.vjp(_mla_h, f(q_ref), f(kh_ref), f(vh_ref))
            dq, dkh, dvh = vjp(do_ref[...])
            dq_ref[...] = dq
            dkh_ref[...] += dkh
            dvh_ref[...] += dvh

    return pl.pallas_call(
        body, grid=(B, MLA_HEADS, lay.nq),
        in_specs=_mla_specs(lay) + [pl.BlockSpec((T, 128), lambda b, h, qi: (lay.qrow(b, qi), h))],
        out_specs=[pl.BlockSpec((T, 256), lambda b, h, qi: (lay.qrow(b, qi), h)),
                   pl.BlockSpec((SEQ, 256), lambda b, h, qi: (b, h)), pl.BlockSpec((T, 256), lambda b, h, qi: (b, h)),
                   pl.BlockSpec((SEQ, 128), lambda b, h, qi: (b, h)), pl.BlockSpec((T, 128), lambda b, h, qi: (b, h))],
        out_shape=[jax.ShapeDtypeStruct((lay.R, 1024), F32), jax.ShapeDtypeStruct((lay.NX, 1024), F32),
                   jax.ShapeDtypeStruct((B * T, 1024), F32), jax.ShapeDtypeStruct((lay.NX, 512), F32),
                   jax.ShapeDtypeStruct((B * T, 512), F32)],
        compiler_params=_cparams(("parallel", "parallel", "arbitrary")), name=name)(q, k, v, k, v, dmerged)


def _cmlp_piece(zu, zv, g, ws, bs):
    u, v = jax.nn.gelu(zu), jax.nn.gelu(zv)
    v = v * lax.rsqrt(jnp.mean(v * v, axis=-1, keepdims=True) + EPS) * g
    return u * (bdot(ws, v) + bs)


def _pieces():
    return [(c, g) for c in range(T // CMLP_CHUNK) for g in range(CMLP_GROUPS)]


def cmlp_merge_fwd(z, o, gvn, ws, bs, lay, name):
    def body(zu_ref, zv_ref, o_ref, g_ref, ws_ref, bs_ref, m_ref):
        m_ref[:, 0:512] = o_ref[...]
        for c, g in _pieces():
            rows, cols = slice(128 * c, 128 * (c + 1)), slice(128 * g, 128 * (g + 1))
            piece = _cmlp_piece(zu_ref[rows, cols], zv_ref[rows, cols], g_ref[:, cols], ws_ref[g], bs_ref[g])
            m_ref[rows, 512 + 128 * g:512 + 128 * (g + 1)] = piece.astype(BF16)

    return pl.pallas_call(
        body, grid=(lay.nt,),
        in_specs=[_row(512, 1), _row(512, 2), _row(512), _full((1, 512)), _full((4, 128, 128)), _full((4, 128, 1))],
        out_specs=_row(1024), out_shape=jax.ShapeDtypeStruct((lay.R, 1024), BF16),
        compiler_params=_cparams(("parallel",)), name=name)(z, z, o, gvn, ws, bs)


def ab_rows_bwd(z, tabc, tabs, dq, dkx, dkh, dvx, dvh, dmerged, gkv, gq, wk, wv, wqn, wqp, gvn, ws, bs, lay, name):
    def body(*refs):
        prep_in = refs[:3] + refs[5:9] + refs[11:17]
        zu_ref, zv_ref = refs[3:5]
        dq_ref, dcm_ref = refs[9:11]
        gvn_ref, ws_ref, bs_ref = refs[17:20]
        dkx_ref, dkh_ref, dvx_ref, dvh_ref = refs[20:24]
        dz_ref, dgkv_ref, dgq_ref, dwk_ref, dwv_ref, dwqn_ref, dwqp_ref, dgvn_ref, dws_ref, dbs_ref = refs[24:]
        first = pl.program_id(0) == 0
        _, vjp = jax.vjp(_ab_prep, *[r[...].astype(F32) for r in prep_in])
        d = vjp((dq_ref[...], _xh_pick(lay, dkx_ref, dkh_ref), _xh_pick(lay, dvx_ref, dvh_ref)))
        dz_ref[:, 0:256] = d[0].astype(BF16)
        dz_ref[:, 256:512] = d[1].astype(BF16)
        dz_ref[:, 1536:1664] = d[2].astype(BF16)
        for ref, val in zip((dgkv_ref, dgq_ref, dwk_ref, dwv_ref, dwqn_ref, dwqp_ref), d[7:]):
            _acc(ref, val, first)
        dws = [0.0] * CMLP_GROUPS
        dbs = [0.0] * CMLP_GROUPS
        dgv = [0.0] * CMLP_GROUPS
        for c, g in _pieces():
            rows, cols = slice(128 * c, 128 * (c + 1)), slice(128 * g, 128 * (g + 1))
            _, vjp = jax.vjp(_cmlp_piece, zu_ref[rows, cols], zv_ref[rows, cols], gvn_ref[:, cols], ws_ref[g], bs_ref[g])
            dzu, dzv, dg_, dws_, dbs_ = vjp(dcm_ref[rows, cols])
            dz_ref[rows, 512 + 128 * g:512 + 128 * (g + 1)] = dzu.astype(BF16)
            dz_ref[rows, 1024 + 128 * g:1024 + 128 * (g + 1)] = dzv.astype(BF16)
            dws[g], dbs[g], dgv[g] = dws[g] + dws_, dbs[g] + dbs_, dgv[g] + dg_
        _acc(dgvn_ref, jnp.concatenate(dgv, axis=1), first)
        _acc(dws_ref, jnp.stack(dws), first)
        _acc(dbs_ref, jnp.stack(dbs), first)

    acc_shapes = [(1, 256), (1, 256), (256, 512), (256, 512), (256, 512), (256, 256), (1, 512), (4, 128, 128), (4, 128, 1)]
    return pl.pallas_call(
        body, grid=(lay.nt,),
        in_specs=_ab_prep_specs(lay)[:3] + [_row(512, 1), _row(512, 2)] + _ab_prep_specs(lay)[3:7]
        + [_row(1024), _row(512, 1)] + _ab_prep_specs(lay)[7:]
        + [_full((1, 512)), _full((4, 128, 128)), _full((4, 128, 1))] + _xh_specs(lay, 1024) + _xh_specs(lay, 512),
        out_specs=[_row(AB_IN_P)] + [_full(s) for s in acc_shapes],
        out_shape=[jax.ShapeDtypeStruct((lay.R, AB_IN_P), BF16)] + [jax.ShapeDtypeStruct(s, F32) for s in acc_shapes],
        compiler_params=_cparams(("arbitrary",)), name=name)(
            z, z, z, z, z, tabc, tabs, tabc, tabs, dq, dmerged, gkv, gq, wk, wv, wqn, wqp, gvn, ws, bs, dkx, dkh, dvx, dvh)


def _cd_prep(zrk, zrq, zsk, zsq0, zsq1, zsv, c256, s256, c128, s128):
    rk = rope(zrk * (RET_QK ** -0.5), c256, s256)
    rq = rope(zrq, c256, s256)
    sk = rope(zsk, c128, s128)
    sq0, sq1 = rope(zsq0, c256, s256), rope(zsq1, c256, s256)
    e0, e1 = _expand_matrix(0), _expand_matrix(1)
    return rq, rk, sq0, sq1, bdot(sk, e0), bdot(sk, e1), bdot(zsv, e0), bdot(zsv, e1)


def _cd_prep_specs(lay):
    return [_row(256, 0), _row(256, 4), _row(128, 6), _row(256, 7), _row(256, 8), _row(128, 7),
            _ropespec(lay, 256), _ropespec(lay, 256), _ropespec(lay, 128), _ropespec(lay, 128)]


def cd_prep_fwd(z, tabc, tabs, lay, name):
    def body(*refs):
        ins, (rq_ref, rk_ref, sq_ref, ke_ref, ve_ref) = refs[:10], refs[10:]
        rq, rk, sq0, sq1, k0, k1, v0, v1 = _cd_prep(*[r[...] for r in ins])
        rq_ref[...] = rq.astype(BF16)
        rk_ref[...] = rk.astype(BF16)
        for ref, (a, b) in ((sq_ref, (sq0, sq1)), (ke_ref, (k0, k1)), (ve_ref, (v0, v1))):
            ref[:, 0:256] = a.astype(BF16)
            ref[:, 256:512] = b.astype(BF16)

    R = lay.R
    return pl.pallas_call(
        body, grid=(lay.nt,), in_specs=_cd_prep_specs(lay),
        out_specs=[_row(256), _row(256), _row(512), _row(512), _row(512)],
        out_shape=[jax.ShapeDtypeStruct((R, w), BF16) for w in (256, 256, 512, 512, 512)],
        compiler_params=_cparams(("parallel",)), name=name)(z, z, z, z, z, z, tabc, tabs, tabc, tabs)


def _ret_sample(h, qs, ks, vs, df, db):
    lgf, lgb = log_sigmoid(df), log_sigmoid(db)
    idx = lax.broadcasted_iota(jnp.int32, (T, 1), 0).astype(F32)
    diff = idx - lax.broadcasted_iota(jnp.int32, (1, T), 1).astype(F32)
    intra = (jnp.where(diff >= 0, jnp.exp(lgf * jnp.maximum(diff, 0.0)), 0.0)
             + jnp.where(diff <= 0, jnp.exp(lgb * jnp.maximum(-diff, 0.0)), 0.0))
    qdf, kdf, cdf = jnp.exp(lgf * (idx + 1.0)), jnp.exp(lgf * (T - 1.0 - idx)), jnp.exp(lgf * T)
    qdb, kdb, cdb = jnp.exp(lgb * (T - idx)), jnp.exp(lgb * idx), jnp.exp(lgb * T)
    mask = _head_mask(qs[0].shape, h)
    qs = [q * mask for q in qs]
    ys = [bdot(bdot_nt(q, k) * intra, v) for q, k, v in zip(qs, ks, vs)]
    n = len(qs)
    state = bdot_tn(ks[0] * kdf, vs[0])
    for i in range(1, n):
        ys[i] = ys[i] + bdot(qs[i] * qdf, state)
        if i + 1 < n:
            state = state * cdf + bdot_tn(ks[i] * kdf, vs[i])
    state = bdot_tn(ks[0] * kdb, vs[0])
    for i in range(n - 1, 0, -1):
        ys[i] = ys[i] + bdot(qs[i] * qdb, state)
        if i > 1:
            state = state * cdb + bdot_tn(ks[i] * kdb, vs[i])
    return ys


def _ret_specs(lay):
    nxt, SEQ = lay.nxt, lay.SEQ
    xs = lambda w, col: pl.BlockSpec((SEQ, w), lambda b, h: (b, col(h)))
    hs = lambda w, col: pl.BlockSpec((T, w), lambda b, h: (nxt + b, col(h)))
    zero, head = (lambda h: 0), (lambda h: 2 + h)
    dec = pl.BlockSpec((None, 8, 128), lambda b, h: (h, 0, 0))
    return [xs(256, zero), hs(256, zero), xs(256, zero), hs(256, zero), xs(128, head), hs(128, head), dec, dec]


def _ret_tiles(x_ref, h_ref, tps, cast=None):
    tiles = [h_ref[...]] + [x_ref[i * T:(i + 1) * T, :] for i in range(tps)]
    return [t.astype(cast) for t in tiles] if cast is not None else tiles


def ret_fwd(rq, rk, z, decf, decb, lay, name):
    tps, SEQ = lay.tps, lay.SEQ

    def body(qx_ref, qh_ref, kx_ref, kh_ref, vx_ref, vh_ref, df_ref, db_ref, yx_ref, yh_ref):
        ys = _ret_sample(pl.program_id(1), _ret_tiles(qx_ref, qh_ref, tps), _ret_tiles(kx_ref, kh_ref, tps),
                         _ret_tiles(vx_ref, vh_ref, tps), df_ref[0:1, 0:1], db_ref[0:1, 0:1])
        yh_ref[...] = ys[0]
        for i in range(tps):
            yx_ref[i * T:(i + 1) * T, :] = ys[i + 1]

    return pl.pallas_call(
        body, grid=(lay.B, RET_HEADS), in_specs=_ret_specs(lay),
        out_specs=[pl.BlockSpec((SEQ, 128), lambda b, h: (b, h)), pl.BlockSpec((T, 128), lambda b, h: (b, h))],
        out_shape=[jax.ShapeDtypeStruct((lay.NX, 512), F32), jax.ShapeDtypeStruct((lay.B * T, 512), F32)],
        compiler_params=_cparams(("parallel", "arbitrary")), name=name)(rq, rq, rk, rk, z, z, decf, decb)


def ret_bwd(rq, rk, z, decf, decb, dy, lay, name):
    tps, SEQ, B = lay.tps, lay.SEQ, lay.B
    nxt = lay.nxt

    def body(qx_ref, qh_ref, kx_ref, kh_ref, vx_ref, vh_ref, df_ref, db_ref, dyx_ref, dyh_ref,
             dqx_ref, dqh_ref, dkx_ref, dkh_ref, dvx_ref, dvh_ref, ddf_ref, ddb_ref):
        h = pl.program_id(1)
        _, vjp = jax.vjp(functools.partial(_ret_sample, h), _ret_tiles(qx_ref, qh_ref, tps, F32),
                         _ret_tiles(kx_ref, kh_ref, tps, F32), _ret_tiles(vx_ref, vh_ref, tps), df_ref[0:1, 0:1],
                         db_ref[0:1, 0:1])
        dqs, dks, dvs, ddf, ddb = vjp(_ret_tiles(dyx_ref, dyh_ref, tps))
        first = h == 0
        _acc(dqh_ref, dqs[0], first)
        _acc(dkh_ref, dks[0], first)
        dvh_ref[...] = dvs[0]
        for i in range(tps):
            rows = slice(i * T, (i + 1) * T)
            _acc(dqx_ref.at[rows], dqs[i + 1], first)
            _acc(dkx_ref.at[rows], dks[i + 1], first)
            dvx_ref[rows, :] = dvs[i + 1]
        @pl.when(jnp.logical_and(pl.program_id(0) == 0, first))
        def _():
            ddf_ref[...] = jnp.zeros(ddf_ref.shape, F32)
            ddb_ref[...] = jnp.zeros(ddb_ref.shape, F32)

        ddf_ref[h] += jnp.broadcast_to(ddf, (8, 128))
        ddb_ref[h] += jnp.broadcast_to(ddb, (8, 128))

    acc_x, acc_h = pl.BlockSpec((SEQ, 256), lambda b, h: (b, 0)), pl.BlockSpec((T, 256), lambda b, h: (b, 0))
    head_x, head_h = pl.BlockSpec((SEQ, 128), lambda b, h: (b, h)), pl.BlockSpec((T, 128), lambda b, h: (b, h))
    dec = pl.BlockSpec((RET_HEADS, 8, 128), lambda b, h: (0, 0, 0))
    return pl.pallas_call(
        body, grid=(B, RET_HEADS),
        in_specs=_ret_specs(lay) + [head_x, pl.BlockSpec((T, 128), lambda b, h: (nxt + b, h))],
        out_specs=[acc_x, acc_h, acc_x, acc_h, head_x, head_h, dec, dec],
        out_shape=[jax.ShapeDtypeStruct((lay.NX, 256), F32), jax.ShapeDtypeStruct((B * T, 256), F32),
                   jax.ShapeDtypeStruct((lay.NX, 256), F32), jax.ShapeDtypeStruct((B * T, 256), F32),
                   jax.ShapeDtypeStruct((lay.NX, 512), F32), jax.ShapeDtypeStruct((B * T, 512), F32),
                   jax.ShapeDtypeStruct((RET_HEADS, 8, 128), F32), jax.ShapeDtypeStruct((RET_HEADS, 8, 128), F32)],
        compiler_params=_cparams(("arbitrary", "arbitrary")), name=name)(rq, rq, rk, rk, z, z, decf, decb, dy, dy)


_SWA_SCALE = SWA_HEAD_DIM ** -0.5


def _swa_head(qh, sw, kh, vw, vh, sink):
    sh = bdot_nt(qh, kh) * _SWA_SCALE
    m = jnp.maximum(jnp.max(sh, axis=-1, keepdims=True), sink)
    if sw is not None:
        m = jnp.maximum(m, jnp.max(sw, axis=-1, keepdims=True))
    m = lax.stop_gradient(m)
    eh, es = jnp.exp(sh - m), jnp.exp(sink - m)
    tot = jnp.sum(eh, axis=-1, keepdims=True) + es
    if sw is None:
        return bdot(eh * (1.0 / tot), vh)
    ew = jnp.exp(sw - m)
    inv = 1.0 / (tot + jnp.sum(ew, axis=-1, keepdims=True))
    return bdot(ew * inv, vw) + bdot(eh * inv, vh)


def _swa_x(t0, kpos0, sq, kw, vw, kh, vh, *sinks):
    t = t0 + lax.broadcasted_iota(jnp.int32, (T, 1), 0)
    pos = kpos0 + lax.broadcasted_iota(jnp.int32, (1, SWA_SPAN), 1)
    band = jnp.abs(t - pos) <= SWA_WINDOW
    out = 0.0
    for i in range(SWA_GROUPS):
        mi = _head_mask(sq.shape, i)
        qh = sq * mi
        sw = jnp.where(band, bdot_nt(qh, kw) * _SWA_SCALE, NEG_INF)
        out = out + _swa_head(qh, sw, kh, vw, vh, sinks[i]) * mi
    return out


def _swa_h(sq, kh, vh, *sinks):
    out = 0.0
    for i in range(SWA_GROUPS):
        mi = _head_mask(sq.shape, i)
        out = out + _swa_head(sq * mi, None, kh, None, vh, sinks[i]) * mi
    return out


def _swa_specs(lay):
    nxt, SEQ = lay.nxt, lay.SEQ
    return [pl.BlockSpec((T, 256), lambda g, b, qi: (lay.qrow(b, qi), g)),
            pl.BlockSpec((SEQ, 256), lambda g, b, qi: (b, g)), pl.BlockSpec((SEQ, 256), lambda g, b, qi: (b, g)),
            pl.BlockSpec((T, 256), lambda g, b, qi: (nxt + b, g)), pl.BlockSpec((T, 256), lambda g, b, qi: (nxt + b, g)),
            pl.BlockSpec((None, 4, 8, 128), lambda g, b, qi: (g, 0, 0, 0))]


def _swa_start(qi, SEQ):
    return pl.multiple_of(jnp.clip(qi * T - SWA_WINDOW, 0, SEQ - SWA_SPAN), SWA_WINDOW)


def swa_fwd(sq, kexp, vexp, sink, lay, name):
    tps, SEQ = lay.tps, lay.SEQ

    def body(sq_ref, kx_ref, vx_ref, kh_ref, vh_ref, sink_ref, o_ref):
        qi = pl.program_id(2)
        f = lambda r: r[...]
        sinks = [sink_ref[i][0:1, 0:1] for i in range(SWA_GROUPS)]

        @pl.when(qi < tps)
        def _():
            k0 = _swa_start(qi, SEQ)
            kw, vw = kx_ref[pl.ds(k0, SWA_SPAN), :], vx_ref[pl.ds(k0, SWA_SPAN), :]
            o_ref[...] = _swa_x(qi * T, k0, f(sq_ref), kw, vw, f(kh_ref), f(vh_ref), *sinks).astype(BF16)

        @pl.when(qi == tps)
        def _():
            o_ref[...] = _swa_h(f(sq_ref), f(kh_ref), f(vh_ref), *sinks).astype(BF16)

    return pl.pallas_call(
        body, grid=(SWA_KV_HEADS, lay.B, lay.nq), in_specs=_swa_specs(lay),
        out_specs=pl.BlockSpec((T, 256), lambda g, b, qi: (lay.qrow(b, qi), g)),
        out_shape=jax.ShapeDtypeStruct((lay.R, 512), BF16),
        compiler_params=_cparams(("parallel", "parallel", "arbitrary")), name=name)(sq, kexp, vexp, kexp, vexp, sink)


def swa_bwd(sq, kexp, vexp, sink, dmerged, lay, name):
    tps, SEQ, B = lay.tps, lay.SEQ, lay.B

    def body(sq_ref, kx_ref, vx_ref, kh_ref, vh_ref, sink_ref, do_ref, dsq_ref, dkx_ref, dkh_ref, dvx_ref, dvh_ref, dsink_ref):
        b, qi = pl.program_id(1), pl.program_id(2)
        f = lambda r: r[...].astype(F32)
        sinks = [sink_ref[i][0:1, 0:1] for i in range(SWA_GROUPS)]
        very_first = jnp.logical_and(b == 0, qi == 0)

        def acc_sink(ds):
            for i in range(SWA_GROUPS):
                _acc(dsink_ref.at[i], jnp.broadcast_to(ds[i], (8, 128)), very_first)

        @pl.when(qi == 0)
        def _():
            for ref in (dkx_ref, dkh_ref, dvx_ref, dvh_ref):
                ref[...] = jnp.zeros(ref.shape, F32)

        @pl.when(qi < tps)
        def _():
            k0 = _swa_start(qi, SEQ)
            win = pl.ds(k0, SWA_SPAN)
            kw, vw = kx_ref[win, :].astype(F32), vx_ref[win, :].astype(F32)
            _, vjp = jax.vjp(functools.partial(_swa_x, qi * T, k0), f(sq_ref), kw, vw, f(kh_ref), f(vh_ref), *sinks)
            d = vjp(do_ref[...])
            dsq_ref[...] = d[0]
            dkx_ref[win, :] += d[1]
            dvx_ref[win, :] += d[2]
            dkh_ref[...] += d[3]
            dvh_ref[...] += d[4]
            acc_sink(d[5:9])

        @pl.when(qi == tps)
        def _():
            _, vjp = jax.vjp(_swa_h, f(sq_ref), f(kh_ref), f(vh_ref), *sinks)
            d = vjp(do_ref[...])
            dsq_ref[...] = d[0]
            dkh_ref[...] += d[1]
            dvh_ref[...] += d[2]
            acc_sink(d[3:7])

    xs = pl.BlockSpec((SEQ, 256), lambda g, b, qi: (b, g))
    hs = pl.BlockSpec((T, 256), lambda g, b, qi: (b, g))
    return pl.pallas_call(
        body, grid=(SWA_KV_HEADS, B, lay.nq),
        in_specs=_swa_specs(lay) + [pl.BlockSpec((T, 256), lambda g, b, qi: (lay.qrow(b, qi), 2 + g))],
        out_specs=[pl.BlockSpec((T, 256), lambda g, b, qi: (lay.qrow(b, qi), g)), xs, hs, xs, hs,
                   pl.BlockSpec((None, 4, 8, 128), lambda g, b, qi: (g, 0, 0, 0))],
        out_shape=[jax.ShapeDtypeStruct((lay.R, 512), F32), jax.ShapeDtypeStruct((lay.NX, 512), F32),
                   jax.ShapeDtypeStruct((B * T, 512), F32), jax.ShapeDtypeStruct((lay.NX, 512), F32),
                   jax.ShapeDtypeStruct((B * T, 512), F32), jax.ShapeDtypeStruct((SWA_KV_HEADS, 4, 8, 128), F32)],
        compiler_params=_cparams(("arbitrary", "arbitrary", "arbitrary")), name=name)(
            sq, kexp, vexp, kexp, vexp, sink, dmerged)


def _cd_merge_piece(y, rg, g):
    return (y * lax.rsqrt(jnp.mean(y * y, axis=-1, keepdims=True) + EPS) * g) * (rg * jax.nn.sigmoid(rg))


def cd_merge_fwd(y, z, o, gn, lay, name):
    def body(yx_ref, yh_ref, rga_ref, rgb_ref, o_ref, g_ref, m_ref):
        y = _xh_pick(lay, yx_ref, yh_ref)
        for h in range(RET_HEADS):
            cols = slice(128 * h, 128 * (h + 1))
            rg_ref, rcols = (rga_ref, cols) if h < 2 else (rgb_ref, slice(128 * (h - 2), 128 * (h - 1)))
            m_ref[:, cols] = _cd_merge_piece(y[:, cols], rg_ref[:, rcols], g_ref[:, cols]).astype(BF16)
        m_ref[:, 512:1024] = o_ref[...]

    return pl.pallas_call(
        body, grid=(lay.nt,), in_specs=_xh_specs(lay, 512) + [_row(256, 5), _row(256, 6), _row(512), _full((1, 512))],
        out_specs=_row(1024), out_shape=jax.ShapeDtypeStruct((lay.R, 1024), BF16),
        compiler_params=_cparams(("parallel",)), name=name)(*y, z, z, o, gn)


def cd_merge_bwd(y, z, gn, dmerged, lay, name):
    def body(yx_ref, yh_ref, rga_ref, rgb_ref, g_ref, dm_ref, dy_ref, drg_ref, dg_ref):
        first = pl.program_id(0) == 0
        y = _xh_pick(lay, yx_ref, yh_ref)
        dgs = []
        for h in range(RET_HEADS):
            cols = slice(128 * h, 128 * (h + 1))
            rg_ref, rcols = (rga_ref, cols) if h < 2 else (rgb_ref, slice(128 * (h - 2), 128 * (h - 1)))
            _, vjp = jax.vjp(_cd_merge_piece, y[:, cols], rg_ref[:, rcols], g_ref[:, cols])
            dy, drg, dg = vjp(dm_ref[:, cols])
            dy_ref[:, cols] = dy
            drg_ref[:, cols] = drg
            dgs.append(dg)
        _acc(dg_ref, jnp.concatenate(dgs, axis=1), first)

    return pl.pallas_call(
        body, grid=(lay.nt,), in_specs=_xh_specs(lay, 512) + [_row(256, 5), _row(256, 6), _full((1, 512)), _row(512, 0)],
        out_specs=[_row(512), _row(512), _full((1, 512))],
        out_shape=[jax.ShapeDtypeStruct((lay.R, 512), F32), jax.ShapeDtypeStruct((lay.R, 512), F32),
                   jax.ShapeDtypeStruct((1, 512), F32)],
        compiler_params=_cparams(("arbitrary",)), name=name)(*y, z, z, gn, dmerged)


def cd_rows_bwd(z, tabc, tabs, dsq, drg, drq, drk, dke, dve, drv, lay, name):
    def body(*refs):
        ins = refs[:10]
        dsq_ref, drg_ref = refs[10:12]
        drq, drk, dke, dve, drv = (_xh_pick(lay, refs[12 + 2 * n], refs[13 + 2 * n]) for n in range(5))
        dz_ref = refs[22]
        _, vjp = jax.vjp(_cd_prep, *[r[...] for r in ins])
        cts = (drq, drk, dsq_ref[:, 0:256], dsq_ref[:, 256:512], dke[:, 0:256], dke[:, 256:512],
               dve[:, 0:256], dve[:, 256:512])
        dzrk, dzrq, dzsk, dzsq0, dzsq1, dzsv = vjp(cts)[:6]
        dz_ref[:, 0:256] = dzrk.astype(BF16)
        dz_ref[:, 256:768] = drv.astype(BF16)
        dz_ref[:, 768:896] = dzsk.astype(BF16)
        dz_ref[:, 896:1024] = dzsv.astype(BF16)
        dz_ref[:, 1024:1280] = dzrq.astype(BF16)
        dz_ref[:, 1280:1792] = drg_ref[...].astype(BF16)
        dz_ref[:, 1792:2048] = dzsq0.astype(BF16)
        dz_ref[:, 2048:2304] = dzsq1.astype(BF16)

    return pl.pallas_call(
        body, grid=(lay.nt,),
        in_specs=_cd_prep_specs(lay) + [_row(512), _row(512)] + _xh_specs(lay, 256) + _xh_specs(lay, 256)
        + _xh_specs(lay, 512) + _xh_specs(lay, 512) + _xh_specs(lay, 512),
        out_specs=_row(2304), out_shape=jax.ShapeDtypeStruct((lay.R, 2304), BF16),
        compiler_params=_cparams(("parallel",)), name=name)(
            z, z, z, z, z, z, tabc, tabs, tabc, tabs, dsq, drg, *drq, *drk, *dke, *dve, *drv)


def _pos():
    return lax.axis_index("x"), lax.axis_index("y"), lax.axis_index("c")


def _flip(v, bit):
    return 1 - v if bit else v


def _comm_call(name, body, ins, out_shapes, n_remote, n_local, aliases=None):
    return pl.pallas_call(
        body, in_specs=[ANY] * len(ins), out_specs=[ANY] * len(out_shapes), out_shape=out_shapes,
        scratch_shapes=[pltpu.SemaphoreType.DMA((n_remote,)), pltpu.SemaphoreType.DMA((n_remote,)),
                        pltpu.SemaphoreType.DMA((n_local,))],
        input_output_aliases=aliases or {}, name=name)(*ins)


def gather8(arr, name):
    def body(a_ref, o_ref, ssem, rsem, lsem):
        x, y, c = _pos()
        me = 4 * x + 2 * y + c
        loc = pltpu.make_async_copy(a_ref, o_ref.at[me], lsem.at[0])
        loc.start()
        cps = []
        for m in range(1, 8):
            peer = (_flip(x, m & 4), _flip(y, m & 2), _flip(c, m & 1))
            cps.append(pltpu.make_async_remote_copy(a_ref, o_ref.at[me], ssem.at[m - 1], rsem.at[m - 1],
                                                    device_id=peer, device_id_type=MESH))
            cps[-1].start()
        for cp in cps:
            cp.wait()
        loc.wait()

    return _comm_call(name, body, [arr], [jax.ShapeDtypeStruct((8,) + arr.shape, arr.dtype)], 7, 1)[0]


def gather_chips(arr, name):
    def body(a_ref, o_ref, ssem, rsem, lsem):
        x, y, c = _pos()
        k = 2 * x + y
        loc = pltpu.make_async_copy(a_ref, o_ref.at[k], lsem.at[0])
        loc.start()
        cps = []
        for m in range(1, 4):
            peer = (_flip(x, m & 2), _flip(y, m & 1), c)
            cps.append(pltpu.make_async_remote_copy(a_ref, o_ref.at[k], ssem.at[m - 1], rsem.at[m - 1],
                                                    device_id=peer, device_id_type=MESH))
            cps[-1].start()
        for cp in cps:
            cp.wait()
        loc.wait()

    return _comm_call(name, body, [arr], [jax.ShapeDtypeStruct((4,) + arr.shape, arr.dtype)], 3, 1)[0]


def gather_weights(arrs, name):
    n = len(arrs)

    def body(*refs):
        a_refs, o_refs, (isend, irecv, lsem) = refs[:n], refs[n:2 * n], refs[2 * n:]
        x, y, c = _pos()
        k = 2 * x + y
        sib = (x, y, 1 - c)
        chips = [(m, (_flip(x, m & 2), _flip(y, m & 1)), 2 * _flip(x, m & 2) + _flip(y, m & 1)) for m in range(1, 4)]
        waits = []
        for w, (a, o) in enumerate(zip(a_refs, o_refs)):
            H = a.shape[0] // 2
            own = pl.ds(c * H, H)
            loc = pltpu.make_async_copy(a, o.at[k], lsem.at[w])
            loc.start()
            first = [pltpu.make_async_remote_copy(a.at[own], o.at[k, own], isend.at[6 * w + m - 1], irecv.at[6 * w + m - 1],
                                                  device_id=(*chip, c), device_id_type=MESH) for m, chip, _ in chips]
            for cp in first:
                cp.start()
            waits.append((loc, first, H, own, a, o, w))
        for loc, first, H, own, a, o, w in waits:
            passed = []
            for (m, chip, kk), cp in zip(chips, first):
                pltpu.make_async_remote_copy(a.at[own], o.at[kk, own], isend.at[6 * w + m - 1], irecv.at[6 * w + m - 1],
                                             device_id=(*chip, c), device_id_type=MESH).wait_recv()
                fw = pltpu.make_async_remote_copy(o.at[kk, own], o.at[kk, own], isend.at[6 * w + 2 + m], irecv.at[6 * w + 2 + m],
                                                  device_id=sib, device_id_type=MESH)
                fw.start()
                passed.append(fw)
            for fw in passed:
                fw.wait_recv()
            for cp in first + passed:
                cp.wait_send()
            loc.wait()

    outs = [jax.ShapeDtypeStruct((4,) + a.shape, a.dtype) for a in arrs]
    return _comm_call(name, body, list(arrs), outs, 6 * n, n)


def swap_other_half(arrs, name):
    n = len(arrs)

    def body(*refs):
        a_refs, o_refs, (ssem, rsem, _) = refs[:n], refs[n:2 * n], refs[2 * n:]
        x, y, c = _pos()
        cps = []
        for w, (a, o) in enumerate(zip(a_refs, o_refs)):
            H = a.shape[1] // 2
            cps.append(pltpu.make_async_remote_copy(a.at[:, pl.ds((1 - c) * H, H)], o, ssem.at[w], rsem.at[w],
                                                    device_id=(x, y, 1 - c), device_id_type=MESH))
            cps[-1].start()
        for cp in cps:
            cp.wait()

    outs = [jax.ShapeDtypeStruct((4, a.shape[1] // 2) + a.shape[2:], a.dtype) for a in arrs]
    return _comm_call(name, body, list(arrs), outs, n, 1)


def exchange_chips(arrs, name):
    n = len(arrs)

    def body(*refs):
        a_refs, o_refs, (ssem, rsem, lsem) = refs[:n], refs[n:2 * n], refs[2 * n:]
        x, y, c = _pos()
        k = 2 * x + y
        cps = []
        for w, (a, o) in enumerate(zip(a_refs, o_refs)):
            cps.append(pltpu.make_async_copy(a.at[k], o.at[k], lsem.at[w]))
            cps[-1].start()
            for m in range(1, 4):
                px, py = _flip(x, m & 2), _flip(y, m & 1)
                cps.append(pltpu.make_async_remote_copy(a.at[2 * px + py], o.at[k], ssem.at[3 * w + m - 1], rsem.at[3 * w + m - 1],
                                                        device_id=(px, py, c), device_id_type=MESH))
                cps[-1].start()
        for cp in cps:
            cp.wait()

    return _comm_call(name, body, list(arrs), [jax.ShapeDtypeStruct(a.shape, a.dtype) for a in arrs], 3 * n, n)


def share_halves(arrs, name):
    n = len(arrs)

    def body(*refs):
        o_refs, (ssem, rsem, _) = refs[n:2 * n], refs[2 * n:]
        x, y, c = _pos()
        cps = []
        for w, o in enumerate(o_refs):
            H = o.shape[0] // 2
            mine = o.at[pl.ds(c * H, H)]
            cps.append(pltpu.make_async_remote_copy(mine, mine, ssem.at[w], rsem.at[w], device_id=(x, y, 1 - c),
                                                    device_id_type=MESH))
            cps[-1].start()
        for cp in cps:
            cp.wait()

    outs = [jax.ShapeDtypeStruct(a.shape, a.dtype) for a in arrs]
    return _comm_call(name, body, list(arrs), outs, n, 1, aliases={i: i for i in range(n)})


def _adamw(w, g, m, v):
    m = ADAM_B1 * m + (1.0 - ADAM_B1) * g
    v = ADAM_B2 * v + (1.0 - ADAM_B2) * (g * g)
    m_hat = m / (1.0 - ADAM_B1 ** ADAM_STEP)
    v_hat = v / (1.0 - ADAM_B2 ** ADAM_STEP)
    return -ADAM_LR * (m_hat / (jnp.sqrt(v_hat) + ADAM_EPS) + ADAM_WD * w), m, v


def _rows_tile(R, C):
    return _pick(R, max(8, (2 << 20) // (4 * C) // 8 * 8), 8)


def chip_partial(gs, buf, cidx, name):
    _, L, R, C = gs.shape
    H, tr = L // 2, _rows_tile(R, C)

    def body(c_ref, g_ref, b_ref, o_ref):
        o_ref[...] = (g_ref[...] + b_ref[...]).astype(BF16)

    return pl.pallas_call(
        body, grid_spec=pltpu.PrefetchScalarGridSpec(
            num_scalar_prefetch=1, grid=(4, H, R // tr),
            in_specs=[pl.BlockSpec((None, None, tr, C), lambda s, l, r, c: (s, c[0] * H + l, r, 0)),
                      pl.BlockSpec((None, None, tr, C), lambda s, l, r, c: (s, l, r, 0))],
            out_specs=pl.BlockSpec((None, None, tr, C), lambda s, l, r, c: (s, l, r, 0))),
        out_shape=jax.ShapeDtypeStruct((4, H, R, C), BF16),
        compiler_params=_cparams(("parallel", "parallel", "parallel")), name=name)(cidx, gs, buf)


def adam_sharded(parts, w, m, v, cidx, name):
    _, H, R, C = parts.shape
    tr = _rows_tile(R, C)

    def body(c_ref, p_ref, w_ref, m_ref, v_ref, g_out, d_out, m_out, v_out):
        g = p_ref[0].astype(F32)
        for s in range(1, 4):
            g = g + p_ref[s].astype(F32)
        g_out[...] = g
        d_out[...], m_out[...], v_out[...] = _adamw(w_ref[...], g, m_ref[...], v_ref[...])

    own = pl.BlockSpec((None, tr, C), lambda l, r, c: (c[0] * H + l, r, 0))
    return pl.pallas_call(
        body, grid_spec=pltpu.PrefetchScalarGridSpec(
            num_scalar_prefetch=1, grid=(H, R // tr),
            in_specs=[pl.BlockSpec((4, None, tr, C), lambda l, r, c: (0, l, r, 0)), own, own, own],
            out_specs=[own, own, own, own]),
        out_shape=[jax.ShapeDtypeStruct((2 * H, R, C), F32)] * 4,
        compiler_params=_cparams(("parallel", "parallel")), name=name)(cidx, parts, w, m, v)


def sum8(arr, name):
    n = arr.shape[1]
    tr = _pick(n, 512, 8)

    def body(a_ref, o_ref):
        s = a_ref[0]
        for j in range(1, 8):
            s = s + a_ref[j]
        o_ref[...] = s

    return pl.pallas_call(
        body, grid=(n // tr,), in_specs=[pl.BlockSpec((8, tr, 128), lambda i: (0, i, 0))],
        out_specs=pl.BlockSpec((tr, 128), lambda i: (i, 0)), out_shape=jax.ShapeDtypeStruct((n, 128), F32),
        compiler_params=_cparams(("parallel",)), name=name)(arr)


def adam_rows(w, g, m, v, name):
    n, C = w.shape
    tr = _rows_tile(n, C)

    def body(w_ref, g_ref, m_ref, v_ref, d_out, m_out, v_out):
        d_out[...], m_out[...], v_out[...] = _adamw(w_ref[...], g_ref[...], m_ref[...], v_ref[...])

    spec = pl.BlockSpec((tr, C), lambda i: (i, 0))
    return pl.pallas_call(
        body, grid=(n // tr,), in_specs=[spec] * 4, out_specs=[spec] * 3,
        out_shape=[jax.ShapeDtypeStruct((n, C), F32)] * 3, compiler_params=_cparams(("parallel",)), name=name)(w, g, m, v)


def _silu(c):
    return c * jax.nn.sigmoid(c)


def ada_fwd(c_all, w, b, name):
    NC, D = c_all.shape
    L, _, Wc = w.shape
    tn = _pick(Wc, 512, 128)

    def body(c_ref, w_ref, b_ref, o_ref):
        o_ref[...] = bdot(_silu(c_ref[...]), w_ref[...]) + b_ref[...]

    return pl.pallas_call(
        body, grid=(L, Wc // tn),
        in_specs=[pl.BlockSpec((NC, D), lambda l, j: (0, 0)), pl.BlockSpec((None, D, tn), lambda l, j: (l, 0, j)),
                  pl.BlockSpec((None, 1, tn), lambda l, j: (l, 0, j))],
        out_specs=pl.BlockSpec((None, NC, tn), lambda l, j: (l, 0, j)), out_shape=jax.ShapeDtypeStruct((L, NC, Wc), F32),
        compiler_params=_cparams(("parallel", "parallel")), name=name)(c_all, w, b)


def ada_bwd(c_all, w, dmod, name):
    NC, D = c_all.shape
    L, _, Wc = w.shape
    tn = _pick(Wc, 512, 128)

    def body(c_ref, w_ref, d_ref, dw_ref, db_ref, dc_ref):
        first = jnp.logical_and(pl.program_id(0) == 0, pl.program_id(1) == 0)
        f = lambda cs, ww: bdot(cs, ww)
        _, vjp = jax.vjp(f, _silu(c_ref[...]), w_ref[...])
        dcs, dw = vjp(d_ref[...])
        dw_ref[...] = dw
        db_ref[...] = jnp.sum(d_ref[...], axis=0, keepdims=True)
        _acc(dc_ref, dcs, first)

    return pl.pallas_call(
        body, grid=(L, Wc // tn),
        in_specs=[pl.BlockSpec((NC, D), lambda l, j: (0, 0)), pl.BlockSpec((None, D, tn), lambda l, j: (l, 0, j)),
                  pl.BlockSpec((None, NC, tn), lambda l, j: (l, 0, j))],
        out_specs=[pl.BlockSpec((None, D, tn), lambda l, j: (l, 0, j)), pl.BlockSpec((None, 1, tn), lambda l, j: (l, 0, j)),
                   pl.BlockSpec((NC, D), lambda l, j: (0, 0))],
        out_shape=[jax.ShapeDtypeStruct((L, D, Wc), F32), jax.ShapeDtypeStruct((L, 1, Wc), F32),
                   jax.ShapeDtypeStruct((NC, D), F32)],
        compiler_params=_cparams(("arbitrary", "arbitrary")), name=name)(c_all, w, dmod)


def cctx_grad(dcs_twice, c_ctx, name):
    def body(d_ref, c_ref, o_ref):
        _, vjp = jax.vjp(_silu, c_ref[...])
        o_ref[...] = vjp(0.5 * d_ref[...])[0]

    D = c_ctx.shape[1]
    return pl.pallas_call(body, out_shape=jax.ShapeDtypeStruct((8, D), F32), name=name)(dcs_twice, c_ctx)


def _rope_tables(SEQ):
    t = jnp.arange(SEQ)
    row, col = (t // GRID_W).astype(F32), (t % GRID_W).astype(F32)
    n_freq = 16
    freqs = ROPE_THETA ** (-jnp.arange(n_freq, dtype=F32) / n_freq)
    ang = jnp.concatenate([row[:, None] * freqs, col[:, None] * freqs], axis=-1)
    cos, sin = jnp.cos(ang), jnp.sin(ang)
    c = jnp.tile(jnp.concatenate([cos, cos], axis=1), (1, 4))
    s = jnp.tile(jnp.concatenate([-sin, sin], axis=1), (1, 4))
    return (jnp.concatenate([c, jnp.ones((T, 256), F32)], axis=0), jnp.concatenate([s, jnp.zeros((T, 256), F32)], axis=0))


def _unshard_cols(g):
    return jnp.transpose(g, (1, 2, 0, 3)).reshape(g.shape[1], g.shape[2], 4 * g.shape[3])


def _unshard_rows(g):
    return jnp.transpose(g, (1, 0, 2, 3)).reshape(g.shape[1], 4 * g.shape[2], g.shape[3])


def _shard_cols(w, n=4):
    L, R, C = w.shape
    return jnp.transpose(w.reshape(L, R, n, C // n), (2, 0, 1, 3))


def _shard_rows(w):
    L, R, C = w.shape
    return jnp.transpose(w.reshape(L, 4, R // 4, C), (1, 0, 2, 3))


def _ab_in_permute(w):
    L, D, _ = w.shape
    return jnp.concatenate([w[..., 0:256], w[..., 320:1600], w[..., 256:320], jnp.zeros((L, D, 64), w.dtype)], axis=-1)


def _ab_in_unpermute(g):
    return jnp.concatenate([g[..., 0:256], g[..., 1536:1600], g[..., 256:1536]], axis=-1)


def _split_heads(w, a):
    L, K, N = w.shape
    w4 = w.reshape(L, K, 4, N // 4)
    return w4[..., :a].reshape(L, K, 4 * a), w4[..., a:].reshape(L, K, N - 4 * a)


def _join_heads(p, q):
    L, K = p.shape[:2]
    return jnp.concatenate([p.reshape(L, K, 4, -1), q.reshape(L, K, 4, -1)], axis=-1).reshape(L, K, -1)


def _pack(arrs):
    parts = []
    for a in arrs:
        f = a.reshape(-1).astype(F32)
        parts.append(jnp.pad(f, (0, (-f.shape[0]) % 1024)))
    return jnp.concatenate(parts).reshape(-1, 128)


def _unpack(buf, like):
    flat, out, off = buf.reshape(-1), [], 0
    for a in like:
        n = math.prod(a.shape)
        out.append(flat[off:off + n].reshape(a.shape))
        off += n + (-n) % 1024
    return out


_SMALL = ("c_ctx", "ada_b", "norm_mix", "norm_ffn", "norm_final", "mla_q_norm", "mla_kv_norm", "cmlp_v_norm", "cmlp_ws",
          "cmlp_bs", "ret_decay_fwd", "ret_decay_bwd", "ret_norm", "swa_sink")
_BIG = ("ffn_in", "ffn_out", "ab_in", "ab_out", "mla_wq_b", "mla_wkv_b", "cd_in", "cd_out")
_WEIGHTS = ("c_ctx", "ada_w", "ada_b", "norm_mix", "norm_ffn", "norm_final", "ffn_in", "ffn_out", "ab_in", "ab_out",
            "mla_q_norm", "mla_kv_norm", "mla_wq_b", "mla_wkv_b", "cmlp_v_norm", "cmlp_ws", "cmlp_bs", "cd_in", "cd_out",
            "ret_decay_fwd", "ret_decay_bwd", "ret_norm", "swa_sink")


def kernel(x, c, ctx, c_ctx, ada_w, ada_b, norm_mix, norm_ffn, norm_final, ffn_in, ffn_out, ab_in, ab_out, mla_q_norm, mla_kv_norm, mla_wq_b, mla_wkv_b, cmlp_v_norm, cmlp_ws, cmlp_bs, cd_in, cd_out, ret_decay_fwd, ret_decay_bwd, ret_norm, swa_sink, loss_target, m_c_ctx, m_ada_w, m_ada_b, m_norm_mix, m_norm_ffn, m_norm_final, m_ffn_in, m_ffn_out, m_ab_in, m_ab_out, m_mla_q_norm, m_mla_kv_norm, m_mla_wq_b, m_mla_wkv_b, m_cmlp_v_norm, m_cmlp_ws, m_cmlp_bs, m_cd_in, m_cd_out, m_ret_decay_fwd, m_ret_decay_bwd, m_ret_norm, m_swa_sink, v_c_ctx, v_ada_w, v_ada_b, v_norm_mix, v_norm_ffn, v_norm_final, v_ffn_in, v_ffn_out, v_ab_in, v_ab_out, v_mla_q_norm, v_mla_kv_norm, v_mla_wq_b, v_mla_wkv_b, v_cmlp_v_norm, v_cmlp_ws, v_cmlp_bs, v_cd_in, v_cd_out, v_ret_decay_fwd, v_ret_decay_bwd, v_ret_norm, v_swa_sink):
    W = dict(c_ctx=c_ctx, ada_w=ada_w, ada_b=ada_b, norm_mix=norm_mix, norm_ffn=norm_ffn, norm_final=norm_final, ffn_in=ffn_in, ffn_out=ffn_out, ab_in=ab_in, ab_out=ab_out, mla_q_norm=mla_q_norm, mla_kv_norm=mla_kv_norm, mla_wq_b=mla_wq_b, mla_wkv_b=mla_wkv_b, cmlp_v_norm=cmlp_v_norm, cmlp_ws=cmlp_ws, cmlp_bs=cmlp_bs, cd_in=cd_in, cd_out=cd_out, ret_decay_fwd=ret_decay_fwd, ret_decay_bwd=ret_decay_bwd, ret_norm=ret_norm, swa_sink=swa_sink)
    M1 = dict(c_ctx=m_c_ctx, ada_w=m_ada_w, ada_b=m_ada_b, norm_mix=m_norm_mix, norm_ffn=m_norm_ffn, norm_final=m_norm_final, ffn_in=m_ffn_in, ffn_out=m_ffn_out, ab_in=m_ab_in, ab_out=m_ab_out, mla_q_norm=m_mla_q_norm, mla_kv_norm=m_mla_kv_norm, mla_wq_b=m_mla_wq_b, mla_wkv_b=m_mla_wkv_b, cmlp_v_norm=m_cmlp_v_norm, cmlp_ws=m_cmlp_ws, cmlp_bs=m_cmlp_bs, cd_in=m_cd_in, cd_out=m_cd_out, ret_decay_fwd=m_ret_decay_fwd, ret_decay_bwd=m_ret_decay_bwd, ret_norm=m_ret_norm, swa_sink=m_swa_sink)
    M2 = dict(c_ctx=v_c_ctx, ada_w=v_ada_w, ada_b=v_ada_b, norm_mix=v_norm_mix, norm_ffn=v_norm_ffn, norm_final=v_norm_final, ffn_in=v_ffn_in, ffn_out=v_ffn_out, ab_in=v_ab_in, ab_out=v_ab_out, mla_q_norm=v_mla_q_norm, mla_kv_norm=v_mla_kv_norm, mla_wq_b=v_mla_wq_b, mla_wkv_b=v_mla_wkv_b, cmlp_v_norm=v_cmlp_v_norm, cmlp_ws=v_cmlp_ws, cmlp_bs=v_cmlp_bs, cd_in=v_cd_in, cd_out=v_cd_out, ret_decay_fwd=v_ret_decay_fwd, ret_decay_bwd=v_ret_decay_bwd, ret_norm=v_ret_norm, swa_sink=v_swa_sink)

    B, SEQ, D = x.shape
    CTX = ctx.shape[1]
    lay = Layout(B, SEQ, CTX, D)
    nt, NX = lay.nt, lay.NX
    ix, iy, ic = lax.axis_index("x"), lax.axis_index("y"), lax.axis_index("c")
    chip, me = 2 * ix + iy, 4 * ix + 2 * iy + ic
    cidx = jnp.reshape(ic, (1,)).astype(jnp.int32)
    Wc = ada_w.shape[2]
    n_even, n_odd = ab_in.shape[0], cd_in.shape[0]

    rn_row = jnp.pad(ret_norm.reshape(1, -1), ((0, 0), (0, D - ret_norm.size)))
    pack0 = jnp.concatenate([c, rn_row, jnp.zeros((8 - (B + 1) % 8, D), F32)], axis=0) if (B + 1) % 8 else jnp.concatenate([c, rn_row], axis=0)
    g0 = gather8(pack0, "gather_cond")
    NC = -(-(8 * B + 1) // 16) * 16
    c_all = jnp.concatenate([g0[:, :B].reshape(8 * B, D), c_ctx[None], jnp.zeros((NC - 8 * B - 1, D), F32)], axis=0)
    rn_sh = ret_norm.shape[1]
    ret_norm_full = jnp.transpose(g0[0::2, B, :ret_norm.size].reshape(4, n_odd, rn_sh), (1, 0, 2)).reshape(n_odd, 4 * rn_sh)

    gw = gather_weights([W[n].astype(BF16) for n in _BIG], "gather_weights")
    w_ffn_in, w_ab_in, w_wq, w_wkv, w_cd_in = (_unshard_cols(gw[i]) for i in (0, 2, 4, 5, 6))
    w_ffn_out, w_ab_out, w_cd_out = (_unshard_rows(gw[i]) for i in (1, 3, 7))
    w_ab_in = _ab_in_permute(w_ab_in)
    w_ffn_out_t = jnp.transpose(w_ffn_out, (0, 2, 1))
    w_qn, w_qp = _split_heads(w_wq, MLA_NOPE)
    w_k, w_v = _split_heads(w_wkv, MLA_NOPE)

    ab_sh = lax.dynamic_slice_in_dim(ada_b, chip * Wc, Wc, axis=1)[:, None, :]
    mod_sh = ada_fwd(c_all, ada_w, ab_sh, "ada_fwd")
    mod_all = _unshard_cols(gather_chips(mod_sh, "gather_mod"))
    mod_mine = jnp.concatenate([lax.dynamic_slice_in_dim(mod_all, me * B, B, axis=1), mod_all[:, 8 * B:8 * B + 1]], axis=1)
    mod = mod_mine.reshape(DEPTH, B + 1, 6, D)

    tabc, tabs = _rope_tables(SEQ)
    bc8 = lambda a: jnp.broadcast_to(a.reshape(a.shape + (1, 1)), a.shape + (8, 128))
    row = lambda a: a.reshape(1, -1)

    S = jnp.concatenate([x.reshape(NX, D), ctx.reshape(B * CTX, D)], axis=0)
    saved = []
    for l in range(DEPTH):
        j, even = l // 2, l % 2 == 0
        xn = norm_mod_fwd(S, row(norm_mix[l]), mod[l], 0, lay, nt, f"norm_mix_fwd{l}")
        if even:
            z = mm(xn, w_ab_in[j], name=f"ab_in{l}")
            q, k, v = ab_prep_fwd(z, tabc, tabs, row(mla_kv_norm[j]), row(mla_q_norm[j]), w_k[j], w_v[j], w_qn[j], w_qp[j],
                                  lay, f"ab_prep{l}")
            o = mla_fwd(q, k, v, lay, f"mla{l}")
            merged = cmlp_merge_fwd(z, o, row(cmlp_v_norm[j]), cmlp_ws[j], cmlp_bs[j][:, :, None], lay, f"cmlp{l}")
            w_out, mix = w_ab_out[j], (q, k, v)
        else:
            z = mm(xn, w_cd_in[j], name=f"cd_in{l}")
            rq, rk, sq, ke, ve = cd_prep_fwd(z, tabc, tabs, lay, f"cd_prep{l}")
            decf, decb, sink = bc8(ret_decay_fwd[j]), bc8(ret_decay_bwd[j]), bc8(swa_sink[j].reshape(SWA_KV_HEADS, SWA_GROUPS))
            yret = ret_fwd(rq, rk, z, decf, decb, lay, f"ret{l}")
            osw = swa_fwd(sq, ke, ve, sink, lay, f"swa{l}")
            merged = cd_merge_fwd(yret, z, osw, row(ret_norm_full[j]), lay, f"cd_merge{l}")
            w_out, mix = w_cd_out[j], (rq, rk, sq, ke, ve, decf, decb, sink, yret)
        y, S_mid = mm_gated(merged, w_out, S, mod[l], 2, lay, name=f"mix_out{l}", n_tiles=nt)
        xn2 = norm_mod_fwd(S_mid, row(norm_ffn[l]), mod[l], 3, lay, nt, f"norm_ffn_fwd{l}")
        act, fa, fb = ffn_in_act(xn2, w_ffn_in[l], f"ffn_in{l}")
        f, S_new = mm_gated(act, w_ffn_out[l], S_mid, mod[l], 5, lay, name=f"ffn_out{l}", n_tiles=nt)
        saved.append((S, xn, z, mix, merged, w_out, y, S_mid, xn2, (fa, fb), act, f))
        S = S_new

    loss_blk, dS, d_norm_final = loss_head(S, row(norm_final), loss_target.reshape(NX, D), lay, "loss_head")
    loss = lax.psum(loss_blk[0, 0], ("x", "y", "c"))

    G = {n: [None] * W[n].shape[0] for n in ("norm_mix", "norm_ffn", "mla_q_norm", "mla_kv_norm", "cmlp_v_norm", "cmlp_ws",
                                             "cmlp_bs", "ret_decay_fwd", "ret_decay_bwd", "ret_norm", "swa_sink")}
    GB = {n: [None] * cnt for n, cnt in (("ffn_in_a", DEPTH), ("ffn_in_b", DEPTH), ("ffn_out", DEPTH), ("ab_in", n_even), ("ab_out", n_even),
                                         ("wqn", n_even), ("wqp", n_even), ("wk", n_even), ("wv", n_even), ("cd_in", n_odd),
                                         ("cd_out", n_odd))}
    dmod = [None] * DEPTH
    for l in reversed(range(DEPTH)):
        j, even = l // 2, l % 2 == 0
        S_in, xn, z, mix, merged, w_out, y, S_mid, xn2, (fa, fb), act, f = saved[l]
        df, dgate2 = gate_bwd(dS, f, mod[l], 5, lay, nt, f"ffn_gate_bwd{l}")
        da, db = ffn_out_dx_act(df, w_ffn_out_t[l], fa, fb, f"ffn_out_dx{l}")
        GB["ffn_out"][l] = mm(act, df, ta=True, name=f"ffn_out_dw{l}")
        GB["ffn_in_a"][l] = mm(xn2, da, ta=True, name=f"ffn_in_dwa{l}")
        GB["ffn_in_b"][l] = mm(xn2, db, ta=True, name=f"ffn_in_dwb{l}")
        dxn2 = ffn_in_dx(da, db, w_ffn_in[l], f"ffn_in_dx{l}")
        dS_mid, dss2, dg = norm_mod_bwd(S_mid, row(norm_ffn[l]), mod[l], 3, dxn2, dS, lay, nt, f"norm_ffn_bwd{l}")
        G["norm_ffn"][l] = dg
        dy, dgate1 = gate_bwd(dS_mid, y, mod[l], 2, lay, nt, f"mix_gate_bwd{l}")
        dmerged = mm(dy, w_out, tb=True, name=f"mix_out_dx{l}")
        d_w_out = mm(merged, dy, ta=True, name=f"mix_out_dw{l}")
        if even:
            q, k, v = mix
            dq, dkx, dkh, dvx, dvh = mla_bwd(q, k, v, dmerged, lay, f"mla_bwd{l}")
            (dz, dgkv, dgq, dwk, dwv, dwqn, dwqp, dgvn, dws, dbs) = ab_rows_bwd(
                z, tabc, tabs, dq, dkx, dkh, dvx, dvh, dmerged, row(mla_kv_norm[j]), row(mla_q_norm[j]), w_k[j], w_v[j],
                w_qn[j], w_qp[j], row(cmlp_v_norm[j]), cmlp_ws[j], cmlp_bs[j][:, :, None], lay, f"ab_rows_bwd{l}")
            G["mla_kv_norm"][j], G["mla_q_norm"][j], G["cmlp_v_norm"][j] = dgkv, dgq, dgvn
            G["cmlp_ws"][j], G["cmlp_bs"][j] = dws, dbs
            GB["wk"][j], GB["wv"][j], GB["wqn"][j], GB["wqp"][j], GB["ab_out"][j] = dwk, dwv, dwqn, dwqp, d_w_out
            w_in = w_ab_in[j]
        else:
            rq, rk, sq, ke, ve, decf, decb, sink, yret = mix
            dyret, drg, dgn = cd_merge_bwd(yret, z, row(ret_norm_full[j]), dmerged, lay, f"cd_merge_bwd{l}")
            dqx, dqh, dkx, dkh, dvx, dvh, ddf, ddb = ret_bwd(rq, rk, z, decf, decb, dyret, lay, f"ret_bwd{l}")
            dsq, dkex, dkeh, dvex, dveh, dsink = swa_bwd(sq, ke, ve, sink, dmerged, lay, f"swa_bwd{l}")
            dz = cd_rows_bwd(z, tabc, tabs, dsq, drg, (dqx, dqh), (dkx, dkh), (dkex, dkeh), (dvex, dveh), (dvx, dvh), lay,
                             f"cd_rows_bwd{l}")
            G["ret_norm"][j], G["ret_decay_fwd"][j], G["ret_decay_bwd"][j] = dgn, ddf[:, 0, 0], ddb[:, 0, 0]
            G["swa_sink"][j] = dsink[:, :, 0, 0].reshape(-1)
            GB["cd_out"][j] = d_w_out
            w_in = w_cd_in[j]
        GB["ab_in" if even else "cd_in"][j] = mm(xn, dz, ta=True, name=f"mix_in_dw{l}")
        dxn = mm(dz, w_in, tb=True, name=f"mix_in_dx{l}")
        dS, dss1, dg = norm_mod_bwd(S_in, row(norm_mix[l]), mod[l], 0, dxn, dS_mid, lay, nt, f"norm_mix_bwd{l}")
        G["norm_mix"][l] = dg
        dmod[l] = jnp.concatenate([dss1, dgate1, dss2, dgate2], axis=1)
    grad_x = dS[:NX].reshape(B, SEQ, D)

    st = lambda n: jnp.stack([g.reshape((4 * rn_sh,) if n == "ret_norm" else W[n].shape[1:]) for g in G[n]])
    small_parts = {n: st(n) for n in G}
    small_parts["norm_final"] = d_norm_final.reshape(-1)
    dmod_local = jnp.stack(dmod).reshape(DEPTH, B + 1, 6 * D)
    names1 = ["norm_mix", "norm_ffn", "norm_final", "mla_q_norm", "mla_kv_norm", "cmlp_v_norm", "cmlp_ws", "cmlp_bs",
              "ret_decay_fwd", "ret_decay_bwd", "ret_norm", "swa_sink"]
    like1 = [dmod_local] + [small_parts[n] for n in names1]
    g1 = gather8(_pack(like1), "gather_small_grads")
    tot1 = _unpack(sum8(g1, "sum_small_grads"), like1)
    sg = dict(zip(names1, tot1[1:]))
    n_dm = math.prod(dmod_local.shape)
    dm_each = g1.reshape(8, -1)[:, :n_dm].reshape(8, DEPTH, B + 1, 6 * D)
    dmod_all = jnp.concatenate([jnp.transpose(dm_each[:, :, :B], (1, 0, 2, 3)).reshape(DEPTH, 8 * B, 6 * D),
                                tot1[0][:, B:B + 1], jnp.zeros((DEPTH, NC - 8 * B - 1, 6 * D), F32)], axis=1)
    dmod_sh = lax.dynamic_slice_in_dim(dmod_all, chip * Wc, Wc, axis=2)
    g_ada_w, g_ada_b_sh, dcs = ada_bwd(c_all, ada_w, dmod_sh, "ada_bwd")
    like2 = [dcs[8 * B], g_ada_b_sh]
    g2 = gather8(_pack(like2), "gather_ada_grads")
    tot2 = _unpack(sum8(g2, "sum_ada_grads"), like2)
    bc = lambda a: jnp.broadcast_to(a.reshape(1, D), (8, D))
    sg["c_ctx"] = cctx_grad(bc(tot2[0]), bc(c_ctx), "c_ctx_grad")[0]
    off = D + (-D) % 1024
    gab = g2.reshape(8, -1)[0::2, off:off + DEPTH * Wc].reshape(4, DEPTH, Wc)
    sg["ada_b"] = jnp.transpose(gab, (1, 0, 2)).reshape(DEPTH, 4 * Wc)
    sg["ret_norm"] = lax.dynamic_slice_in_dim(sg["ret_norm"], chip * rn_sh, rn_sh, axis=1)

    stack = lambda n: jnp.stack(GB[n])
    gs = {"ffn_in": jnp.concatenate([_shard_cols(stack("ffn_in_a"), 2), _shard_cols(stack("ffn_in_b"), 2)], axis=0),
          "ffn_out": _shard_rows(stack("ffn_out")),
          "ab_in": _shard_cols(_ab_in_unpermute(stack("ab_in"))), "ab_out": _shard_rows(stack("ab_out")),
          "mla_wq_b": _shard_cols(_join_heads(stack("wqn"), stack("wqp"))),
          "mla_wkv_b": _shard_cols(_join_heads(stack("wk"), stack("wv"))),
          "cd_in": _shard_cols(stack("cd_in")), "cd_out": _shard_rows(stack("cd_out"))}
    bufs = swap_other_half([gs[n] for n in _BIG], "swap_core_halves")
    parts = [chip_partial(gs[n], b, cidx, f"chip_partial_{n}") for n, b in zip(_BIG, bufs)]
    arrived = exchange_chips(parts, "exchange_chips")
    halves = []
    for n, p in zip(_BIG, arrived):
        halves += adam_sharded(p, W[n], M1[n], M2[n], cidx, f"adam_{n}")
    full = share_halves(halves, "share_core_halves")
    out = {n: tuple(full[4 * i:4 * i + 4]) for i, n in enumerate(_BIG)}

    like_s = [W[n] for n in _SMALL]
    dsm, msm, vsm = adam_rows(_pack(like_s), _pack([sg[n].reshape(W[n].shape) for n in _SMALL]), _pack([M1[n] for n in _SMALL]),
                                _pack([M2[n] for n in _SMALL]), "adam_small")
    for n, d_, m_, v_ in zip(_SMALL, _unpack(dsm, like_s), _unpack(msm, like_s), _unpack(vsm, like_s)):
        out[n] = (sg[n].reshape(W[n].shape), d_, m_, v_)
    flat2 = lambda a: a.reshape(-1, a.shape[-1])
    d_, m_, v_ = adam_rows(flat2(ada_w), flat2(g_ada_w), flat2(m_ada_w), flat2(v_ada_w), "adam_ada_w")
    out["ada_w"] = (g_ada_w, d_.reshape(ada_w.shape), m_.reshape(ada_w.shape), v_.reshape(ada_w.shape))

    return (loss, grad_x, *[out[n][0] for n in _WEIGHTS], *[out[n][1] for n in _WEIGHTS], *[out[n][2] for n in _WEIGHTS],
            *[out[n][3] for n in _WEIGHTS])
```

```python
import functools
import math

import jax
import jax.numpy as jnp
import numpy as np
from jax import lax
from jax.experimental import pallas as pl
from jax.experimental.pallas import tpu as pltpu

F32 = jnp.float32
BF16 = jnp.bfloat16
EPS = 1e-6
NEG_INF = -1e30
GRID_W = 64
ROPE_THETA = 10000.0
DEPTH = 4
MLA_HEADS, MLA_Q_LORA, MLA_KV_LORA, MLA_NOPE, MLA_ROPE, MLA_V = 4, 256, 256, 128, 64, 128
CMLP_GROUPS, CMLP_CHUNK = 4, 128
CMLP_WIDTH = 512
RET_HEADS, RET_QK, RET_V = 4, 64, 128
SWA_Q_HEADS, SWA_KV_HEADS, SWA_HEAD_DIM, SWA_WINDOW = 8, 2, 64, 128
SWA_GROUPS = SWA_Q_HEADS // SWA_KV_HEADS
AB_IN_P = 1664
ADAM_LR, ADAM_B1, ADAM_B2, ADAM_EPS, ADAM_WD, ADAM_STEP = 0.001, 0.9, 0.999, 1e-08, 0.01, 10

T = 256
SWA_SPAN = T + 2 * SWA_WINDOW
VMEM_LIMIT = 48 * 1024 * 1024
MESH = pl.DeviceIdType.MESH
ANY = pl.BlockSpec(memory_space=pl.ANY)


def _cparams(sem):
    return pltpu.CompilerParams(dimension_semantics=sem, vmem_limit_bytes=VMEM_LIMIT)


@functools.cache
def _bdot_fn(ca, cb):
    fa, fb = 1 - ca, 1 - cb

    def dg(p, q, cp, cq):
        return lax.dot_general(p.astype(BF16), q.astype(BF16), (((cp,), (cq,)), ((), ())), preferred_element_type=F32)

    @jax.custom_vjp
    def bd(a, b):
        return dg(a, b, ca, cb)

    def fwd(a, b):
        return dg(a, b, ca, cb), (a, b)

    def bwd(res, g):
        a, b = res
        da = dg(g, b, 1, fb) if ca == 1 else dg(b, g, fb, 1)
        db = dg(a, g, fa, 0) if cb == 0 else dg(g, a, 0, fa)
        return da, db

    bd.defvjp(fwd, bwd)
    return bd


def bdot(a, b):
    return _bdot_fn(1, 0)(a, b)


def bdot_nt(a, b):
    return _bdot_fn(1, 1)(a, b)


def bdot_tn(a, b):
    return _bdot_fn(0, 0)(a, b)


def _swap32(x):
    w = x.shape[-1]
    lane = lax.broadcasted_iota(jnp.int32, x.shape, 1)
    return jnp.where((lane & 32) == 0, pltpu.roll(x, w - 32, 1), pltpu.roll(x, 32, 1))


@jax.custom_vjp
def rope(x, c, s):
    return x * c + _swap32(x) * s


def _rope_fwd(x, c, s):
    return rope(x, c, s), (c, s)


def _rope_bwd(res, g):
    c, s = res
    return g * c + _swap32(g * s), jnp.zeros_like(c), jnp.zeros_like(s)


rope.defvjp(_rope_fwd, _rope_bwd)


def rms(x, g):
    return x * lax.rsqrt(jnp.mean(x * x, axis=-1, keepdims=True) + EPS) * g


def normmod(x, g, sh, sc):
    return rms(x, g) * (1.0 + sc) + sh


def log_sigmoid(x):
    return jnp.minimum(x, 0.0) - jnp.log(1.0 + jnp.exp(-jnp.abs(x)))


def _head_mask(shape, h):
    lane = lax.broadcasted_iota(jnp.int32, shape, 1)
    return ((lane >> 6) == h).astype(F32)


def _fold_matrix():
    i = lax.broadcasted_iota(jnp.int32, (256, 128), 0)
    j = lax.broadcasted_iota(jnp.int32, (256, 128), 1)
    return ((i & 63) == j).astype(F32)


def _expand_matrix(g):
    i = lax.broadcasted_iota(jnp.int32, (128, 256), 0)
    j = lax.broadcasted_iota(jnp.int32, (128, 256), 1)
    return (i == (j & 63) + 64 * g).astype(F32)


def _acc(ref, val, first):
    @pl.when(first)
    def _():
        ref[...] = val

    @pl.when(jnp.logical_not(first))
    def _():
        ref[...] += val


def _pick(n, cap, mult):
    best = None
    for d in range(mult, min(n, cap) + 1, mult):
        if n % d == 0:
            best = d
    return best if best is not None else n


def mm(a, b, *, ta=False, tb=False, name):
    M, K = (a.shape[1], a.shape[0]) if ta else a.shape
    N = b.shape[0] if tb else b.shape[1]
    if ta:
        tn = N
        tm = _pick(M, min(1536, (3 << 20) // tn), 128)
    else:
        tn = _pick(N, 768, 128)
        if tn < 256 and N <= 2304:
            tn = N
        tm = _pick(M, min(1536, (1 << 20) // tn), 128)
    tk = _pick(K, 2048 if not ta else 1024, 128) if K > 2816 or ta else K
    nk = K // tk
    grid = (M // tm, N // tn, nk)
    a_spec = pl.BlockSpec((tk, tm), lambda i, j, k: (k, i)) if ta else pl.BlockSpec((tm, tk), lambda i, j, k: (i, k))
    b_spec = pl.BlockSpec((tn, tk), lambda i, j, k: (j, k)) if tb else pl.BlockSpec((tk, tn), lambda i, j, k: (k, j))
    dims = (((0 if ta else 1,), (1 if tb else 0,)), ((), ()))

    def body(a_ref, b_ref, o_ref):
        part = lax.dot_general(a_ref[...], b_ref[...], dims, preferred_element_type=F32)
        if nk == 1:
            o_ref[...] = part
        else:
            _acc(o_ref, part, pl.program_id(2) == 0)

    return pl.pallas_call(
        body, grid=grid, in_specs=[a_spec, b_spec], out_specs=pl.BlockSpec((tm, tn), lambda i, j, k: (i, j)),
        out_shape=jax.ShapeDtypeStruct((M, N), F32),
        compiler_params=_cparams(("parallel", "parallel", "arbitrary")), name=name)(a, b)


def mm_gated(a, b, res, mod, gate_row, lay, *, name, n_tiles, norm=None):
    K, N = b.shape
    M = n_tiles * T

    def body(*refs):
        a_ref, b_ref, r_ref, g_ref = refs[:4]
        y = lax.dot_general(a_ref[...], b_ref[...], (((1,), (0,)), ((), ())), preferred_element_type=F32)
        new = r_ref[...] + g_ref[gate_row:gate_row + 1, :] * y
        if norm is None:
            y_ref, o_ref = refs[4:]
        else:
            gn_ref, mn_ref, y_ref, o_ref, xn_ref = refs[4:]
            k0 = norm[2]
            xn_ref[...] = normmod(new, gn_ref[...], mn_ref[k0:k0 + 1, :], mn_ref[k0 + 1:k0 + 2, :]).astype(BF16)
        y_ref[...] = y.astype(BF16)
        o_ref[...] = new

    rows = pl.BlockSpec((T, N), lambda i: (i, 0))
    extra = [] if norm is None else [_full((1, N)), _modspec(lay, 6, N)]
    return pl.pallas_call(
        body, grid=(n_tiles,),
        in_specs=[pl.BlockSpec((T, K), lambda i: (i, 0)), _full((K, N)), rows, _modspec(lay, 6, N)] + extra,
        out_specs=[rows, rows] + ([] if norm is None else [rows]),
        out_shape=[jax.ShapeDtypeStruct((M, N), BF16), jax.ShapeDtypeStruct((M, N), F32)]
        + ([] if norm is None else [jax.ShapeDtypeStruct((M, N), BF16)]),
        compiler_params=_cparams(("parallel",)), name=name)(a, b, res, mod, *([] if norm is None else norm[:2]))


class Layout:
    def __init__(self, B, SEQ, CTX, D):
        assert CTX == T and SEQ % T == 0 and SEQ >= SWA_SPAN
        self.B, self.SEQ, self.CTX, self.D = B, SEQ, CTX, D
        self.tps = SEQ // T
        self.nxt = B * self.tps
        self.nt = self.nxt + B
        self.NX, self.R = B * SEQ, B * SEQ + B * CTX
        self.nq = self.tps + 1

    def mod_idx(self, i):
        return jnp.where(i < self.nxt, i // self.tps, self.B)

    def rope_idx(self, i):
        return jnp.where(i < self.nxt, i % self.tps, self.tps)

    def first_of_mod(self, i):
        return jnp.logical_or(jnp.logical_and(i < self.nxt, i % self.tps == 0), i == self.nxt)

    def qrow(self, b, qi):
        return jnp.where(qi < self.tps, b * self.tps + qi, self.nxt + b)


def _row(w, col=0):
    return pl.BlockSpec((T, w), lambda i: (i, col))


def _full(shape):
    nd = len(shape)
    return pl.BlockSpec(shape, lambda i: (0,) * nd)


def _modspec(lay, rows, D):
    return pl.BlockSpec((None, rows, D), lambda i: (lay.mod_idx(i), 0, 0))


def _ropespec(lay, w):
    return pl.BlockSpec((T, w), lambda i: (lay.rope_idx(i), 0))


def _xh_specs(lay, w):
    nxt = lay.nxt
    return [pl.BlockSpec((T, w), lambda i: (jnp.minimum(i, nxt - 1), 0)), pl.BlockSpec((T, w), lambda i: (jnp.maximum(i - nxt, 0), 0))]


def _xh_pick(lay, x_ref, h_ref):
    return jnp.where(pl.program_id(0) < lay.nxt, x_ref[...], h_ref[...])


def norm_mod_fwd(S, g, mod, k0, lay, n_tiles, name):
    D = S.shape[1]

    def body(s_ref, g_ref, mod_ref, o_ref):
        o_ref[...] = normmod(s_ref[...], g_ref[...], mod_ref[k0:k0 + 1, :], mod_ref[k0 + 1:k0 + 2, :]).astype(BF16)

    return pl.pallas_call(
        body, grid=(n_tiles,), in_specs=[_row(D), _full((1, D)), _modspec(lay, 6, D)], out_specs=_row(D),
        out_shape=jax.ShapeDtypeStruct((n_tiles * T, D), BF16), compiler_params=_cparams(("parallel",)), name=name)(S, g, mod)


def norm_mod_bwd(S, g, mod, k0, dxn, ds_in, lay, n_tiles, name, gate=None):
    D = S.shape[1]

    def body(*refs):
        s_ref, g_ref, mod_ref, dxn_ref, dsin_ref = refs[:5]
        i = pl.program_id(0)
        _, vjp = jax.vjp(normmod, s_ref[...], g_ref[...], mod_ref[k0:k0 + 1, :], mod_ref[k0 + 1:k0 + 2, :])
        dx, dg, dsh, dsc = vjp(dxn_ref[...])
        ds = dsin_ref[...] + dx
        if gate is None:
            ds_ref, dss_ref, dg_ref = refs[5:]
        else:
            y_ref, gmod_ref, ds_ref, dss_ref, dg_ref, dy_ref, dgate_ref = refs[5:]
            row = gate[2]
            dy_ref[...] = (gmod_ref[row:row + 1, :] * ds).astype(BF16)
            _acc(dgate_ref, jnp.sum(ds * y_ref[...], axis=0, keepdims=True), lay.first_of_mod(i))
        ds_ref[...] = ds
        _acc(dg_ref, dg, i == 0)
        _acc(dss_ref, jnp.concatenate([dsh, dsc], axis=0), lay.first_of_mod(i))

    R_ = n_tiles * T
    gated = gate is not None
    return pl.pallas_call(
        body, grid=(n_tiles,),
        in_specs=[_row(D), _full((1, D)), _modspec(lay, 6, D), _row(D), _row(D)] + ([_row(D), _modspec(lay, 6, D)] if gated else []),
        out_specs=[_row(D), _modspec(lay, 2, D), _full((1, D))] + ([_row(D), _modspec(lay, 1, D)] if gated else []),
        out_shape=[jax.ShapeDtypeStruct((R_, D), F32), jax.ShapeDtypeStruct((lay.B + 1, 2, D), F32),
                   jax.ShapeDtypeStruct((1, D), F32)]
        + ([jax.ShapeDtypeStruct((R_, D), BF16), jax.ShapeDtypeStruct((lay.B + 1, 1, D), F32)] if gated else []),
        compiler_params=_cparams(("arbitrary",)), name=name)(S, g, mod, dxn, ds_in, *(gate[:2] if gated else []))


def gate_bwd(dS, y, mod, gate_row, lay, n_tiles, name):
    D = dS.shape[1]

    def body(ds_ref, y_ref, mod_ref, dy_ref, dgate_ref):
        i = pl.program_id(0)
        ds = ds_ref[...]
        dy_ref[...] = (mod_ref[gate_row:gate_row + 1, :] * ds).astype(BF16)
        _acc(dgate_ref, jnp.sum(ds * y_ref[...], axis=0, keepdims=True), lay.first_of_mod(i))

    return pl.pallas_call(
        body, grid=(n_tiles,), in_specs=[_row(D), _row(D), _modspec(lay, 6, D)],
        out_specs=[_row(D), _modspec(lay, 1, D)],
        out_shape=[jax.ShapeDtypeStruct((n_tiles * T, D), BF16), jax.ShapeDtypeStruct((lay.B + 1, 1, D), F32)],
        compiler_params=_cparams(("arbitrary",)), name=name)(dS, y, mod)


def _swiglu(a, b):
    return a * jax.nn.sigmoid(a) * b


def _ffn_tiles(M, F):
    tn = _pick(F, 1408, 128)
    return _pick(M, (3 << 18) // tn, 128), tn


def ffn_in_act(x, w, name):
    M, D = x.shape
    F = w.shape[1] // 2
    tm, tn = _ffn_tiles(M, F)
    nj = F // tn

    def body(x_ref, wa_ref, wb_ref, act_ref, a_ref, b_ref):
        dims = (((1,), (0,)), ((), ()))
        a = lax.dot_general(x_ref[...], wa_ref[...], dims, preferred_element_type=F32)
        b = lax.dot_general(x_ref[...], wb_ref[...], dims, preferred_element_type=F32)
        act_ref[...] = _swiglu(a, b).astype(BF16)
        a_ref[...] = a.astype(BF16)
        b_ref[...] = b.astype(BF16)

    out = pl.BlockSpec((tm, tn), lambda i, j: (i, j))
    return pl.pallas_call(
        body, grid=(M // tm, nj),
        in_specs=[pl.BlockSpec((tm, D), lambda i, j: (i, 0)), pl.BlockSpec((D, tn), lambda i, j: (0, j)),
                  pl.BlockSpec((D, tn), lambda i, j: (0, j + nj))],
        out_specs=[out, out, out], out_shape=[jax.ShapeDtypeStruct((M, F), BF16)] * 3,
        compiler_params=_cparams(("parallel", "parallel")), name=name)(x, w, w)


def ffn_out_dx_act(df, wt, a, b, name):
    M, D = df.shape
    F = wt.shape[1]
    tm, tn = _ffn_tiles(M, F)

    def body(df_ref, w_ref, a_ref, b_ref, da_ref, db_ref):
        dact = lax.dot_general(df_ref[...], w_ref[...], (((1,), (0,)), ((), ())), preferred_element_type=F32)
        _, vjp = jax.vjp(_swiglu, a_ref[...].astype(F32), b_ref[...].astype(F32))
        da, db = vjp(dact)
        da_ref[...] = da.astype(BF16)
        db_ref[...] = db.astype(BF16)

    blk = pl.BlockSpec((tm, tn), lambda i, j: (i, j))
    return pl.pallas_call(
        body, grid=(M // tm, F // tn),
        in_specs=[pl.BlockSpec((tm, D), lambda i, j: (i, 0)), pl.BlockSpec((D, tn), lambda i, j: (0, j)), blk, blk],
        out_specs=[blk, blk], out_shape=[jax.ShapeDtypeStruct((M, F), BF16)] * 2,
        compiler_params=_cparams(("parallel", "parallel")), name=name)(df, wt, a, b)


def ffn_in_dx(da, db, w, name):
    M, F = da.shape
    D = w.shape[0]
    tm, tn = _pick(M, 512, 128), _pick(D, 512, 128)

    def body(da_ref, db_ref, wa_ref, wb_ref, o_ref):
        dims = (((1,), (1,)), ((), ()))
        o_ref[...] = (lax.dot_general(da_ref[...], wa_ref[...], dims, preferred_element_type=F32)
                      + lax.dot_general(db_ref[...], wb_ref[...], dims, preferred_element_type=F32))

    return pl.pallas_call(
        body, grid=(M // tm, D // tn),
        in_specs=[pl.BlockSpec((tm, F), lambda i, j: (i, 0)), pl.BlockSpec((tm, F), lambda i, j: (i, 0)),
                  pl.BlockSpec((tn, F), lambda i, j: (j, 0)), pl.BlockSpec((tn, F), lambda i, j: (j, 1))],
        out_specs=pl.BlockSpec((tm, tn), lambda i, j: (i, j)), out_shape=jax.ShapeDtypeStruct((M, D), F32),
        compiler_params=_cparams(("parallel", "parallel")), name=name)(da, db, w, w)


def loss_head(S, g, target, lay, name):
    D = S.shape[1]
    nxt = lay.nxt

    def tile_loss(x, gg, t):
        err = rms(x, gg) - t
        return 0.5 * jnp.sum(jnp.mean(err * err, axis=-1))

    def body(s_ref, g_ref, t_ref, loss_ref, ds_ref, dg_ref):
        i = pl.program_id(0)

        @pl.when(i < nxt)
        def _():
            val, vjp = jax.vjp(tile_loss, s_ref[...], g_ref[...], t_ref[...])
            dx, dg, _ = vjp(jnp.ones((), F32))
            ds_ref[...] = dx
            _acc(dg_ref, dg, i == 0)
            _acc(loss_ref, jnp.full((8, 128), val, F32), i == 0)

        @pl.when(i >= nxt)
        def _():
            ds_ref[...] = jnp.zeros((T, D), F32)

    return pl.pallas_call(
        body, grid=(lay.nt,),
        in_specs=[_row(D), _full((1, D)), pl.BlockSpec((T, D), lambda i: (jnp.minimum(i, nxt - 1), 0))],
        out_specs=[_full((8, 128)), _row(D), _full((1, D))],
        out_shape=[jax.ShapeDtypeStruct((8, 128), F32), jax.ShapeDtypeStruct((lay.R, D), F32),
                   jax.ShapeDtypeStruct((1, D), F32)],
        compiler_params=_cparams(("arbitrary",)), name=name)(S, g, target)


def _ab_prep(zkv, zq, zpe, c256, s256, c128, s128, gkv, gq, wk, wv, wqn, wqp):
    kvn = rms(zkv, gkv)
    kn, v = bdot(kvn, wk), bdot(kvn, wv)
    qn = rms(zq, gq)
    qnope, qpe = bdot(qn, wqn), rope(bdot(qn, wqp), c256, s256)
    kpe = rope(zpe, c128, s128)
    fold = _fold_matrix()
    qparts, kparts = [], []
    for h in range(MLA_HEADS):
        qparts += [qnope[:, 128 * h:128 * (h + 1)], bdot(qpe * _head_mask(qpe.shape, h), fold)]
        kparts += [kn[:, 128 * h:128 * (h + 1)], kpe]
    return jnp.concatenate(qparts, axis=1), jnp.concatenate(kparts, axis=1), v


def _ab_prep_specs(lay):
    return [_row(256, 0), _row(256, 1), _row(128, 12), _ropespec(lay, 256), _ropespec(lay, 256), _ropespec(lay, 128),
            _ropespec(lay, 128), _full((1, 256)), _full((1, 256)), _full((256, 512)), _full((256, 512)),
            _full((256, 512)), _full((256, 256))]


def ab_prep_fwd(z, tabc, tabs, gkv, gq, wk, wv, wqn, wqp, lay, name):
    def body(*refs):
        ins, (q_ref, k_ref, v_ref) = refs[:13], refs[13:]
        q, k, v = _ab_prep(*[r[...].astype(F32) for r in ins])
        q_ref[...] = q.astype(BF16)
        k_ref[...] = k.astype(BF16)
        v_ref[...] = v.astype(BF16)

    R = lay.R
    return pl.pallas_call(
        body, grid=(lay.nt,), in_specs=_ab_prep_specs(lay), out_specs=[_row(1024), _row(1024), _row(512)],
        out_shape=[jax.ShapeDtypeStruct((R, 1024), BF16), jax.ShapeDtypeStruct((R, 1024), BF16),
                   jax.ShapeDtypeStruct((R, 512), BF16)],
        compiler_params=_cparams(("parallel",)), name=name)(z, z, z, tabc, tabs, tabc, tabs, gkv, gq, wk, wv, wqn, wqp)


_MLA_SCALE = (MLA_NOPE + MLA_ROPE) ** -0.5


def _mla_x(q, kx, vx, kh, vh):
    sx, sh = bdot_nt(q, kx) * _MLA_SCALE, bdot_nt(q, kh) * _MLA_SCALE
    m = lax.stop_gradient(jnp.maximum(jnp.max(sx, axis=-1, keepdims=True), jnp.max(sh, axis=-1, keepdims=True)))
    ex, eh = jnp.exp(sx - m), jnp.exp(sh - m)
    inv = 1.0 / (jnp.sum(ex, axis=-1, keepdims=True) + jnp.sum(eh, axis=-1, keepdims=True))
    return bdot(ex * inv, vx) + bdot(eh * inv, vh)


def _mla_h(q, kh, vh):
    sh = bdot_nt(q, kh) * _MLA_SCALE
    eh = jnp.exp(sh - lax.stop_gradient(jnp.max(sh, axis=-1, keepdims=True)))
    return bdot(eh * (1.0 / jnp.sum(eh, axis=-1, keepdims=True)), vh)


def _mla_specs(lay):
    nxt, SEQ = lay.nxt, lay.SEQ
    return [pl.BlockSpec((T, 256), lambda b, h, qi: (lay.qrow(b, qi), h)),
            pl.BlockSpec((SEQ, 256), lambda b, h, qi: (b, h)), pl.BlockSpec((SEQ, 128), lambda b, h, qi: (b, h)),
            pl.BlockSpec((T, 256), lambda b, h, qi: (nxt + b, h)), pl.BlockSpec((T, 128), lambda b, h, qi: (nxt + b, h))]


def mla_fwd(q, k, v, lay, name):
    tps = lay.tps

    def body(q_ref, kx_ref, vx_ref, kh_ref, vh_ref, o_ref):
        qi = pl.program_id(2)
        f = lambda r: r[...]

        @pl.when(qi < tps)
        def _():
            o_ref[...] = _mla_x(f(q_ref), f(kx_ref), f(vx_ref), f(kh_ref), f(vh_ref)).astype(BF16)

        @pl.when(qi == tps)
        def _():
            o_ref[...] = _mla_h(f(q_ref), f(kh_ref), f(vh_ref)).astype(BF16)

    return pl.pallas_call(
        body, grid=(lay.B, MLA_HEADS, lay.nq), in_specs=_mla_specs(lay),
        out_specs=pl.BlockSpec((T, 128), lambda b, h, qi: (lay.qrow(b, qi), h)),
        out_shape=jax.ShapeDtypeStruct((lay.R, 512), BF16),
        compiler_params=_cparams(("parallel", "parallel", "arbitrary")), name=name)(q, k, v, k, v)


def mla_bwd(q, k, v, dmerged, lay, name):
    tps, SEQ, B = lay.tps, lay.SEQ, lay.B

    def body(q_ref, kx_ref, vx_ref, kh_ref, vh_ref, do_ref, dq_ref, dkx_ref, dkh_ref, dvx_ref, dvh_ref):
        qi = pl.program_id(2)
        f = lambda r: r[...].astype(F32)

        @pl.when(qi < tps)
        def _():
            _, vjp = jax.vjp(_mla_x, f(q_ref), f(kx_ref), f(vx_ref), f(kh_ref), f(vh_ref))
            dq, dkx, dvx, dkh, dvh = vjp(do_ref[...])
            dq_ref[...] = dq
            _acc(dkx_ref, dkx, qi == 0)
            _acc(dvx_ref, dvx, qi == 0)
            _acc(dkh_ref, dkh, qi == 0)
            _acc(dvh_ref, dvh, qi == 0)

        @pl.when(qi == tps)
        def _():
            _, vjp = jax.vjp(_mla_h, f(q_ref), f(kh_ref), f(vh_ref))
            dq, dkh, dvh = vjp(do_ref[...])
            dq_ref[...] = dq
            dkh_ref[...] += dkh
            dvh_ref[...] += dvh

    return pl.pallas_call(
        body, grid=(B, MLA_HEADS, lay.nq),
        in_specs=_mla_specs(lay) + [pl.BlockSpec((T, 128), lambda b, h, qi: (lay.qrow(b, qi), h))],
        out_specs=[pl.BlockSpec((T, 256), lambda b, h, qi: (lay.qrow(b, qi), h)),
                   pl.BlockSpec((SEQ, 256), lambda b, h, qi: (b, h)), pl.BlockSpec((T, 256), lambda b, h, qi: (b, h)),
                   pl.BlockSpec((SEQ, 128), lambda b, h, qi: (b, h)), pl.BlockSpec((T, 128), lambda b, h, qi: (b, h))],
        out_shape=[jax.ShapeDtypeStruct((lay.R, 1024), F32), jax.ShapeDtypeStruct((lay.NX, 1024), F32),
                   jax.ShapeDtypeStruct((B * T, 1024), F32), jax.ShapeDtypeStruct((lay.NX, 512), F32),
                   jax.ShapeDtypeStruct((B * T, 512), F32)],
        compiler_params=_cparams(("parallel", "parallel", "arbitrary")), name=name)(q, k, v, k, v, dmerged)


def _cmlp_piece(zu, zv, g, ws, bs):
    u, v = jax.nn.gelu(zu), jax.nn.gelu(zv)
    v = v * lax.rsqrt(jnp.mean(v * v, axis=-1, keepdims=True) + EPS) * g
    return u * (bdot(ws, v) + bs)


def _pieces():
    return [(c, g) for c in range(T // CMLP_CHUNK) for g in range(CMLP_GROUPS)]


def cmlp_merge_fwd(z, o, gvn, ws, bs, lay, name):
    def body(zu_ref, zv_ref, o_ref, g_ref, ws_ref, bs_ref, m_ref):
        m_ref[:, 0:512] = o_ref[...]
        for c, g in _pieces():
            rows, cols = slice(128 * c, 128 * (c + 1)), slice(128 * g, 128 * (g + 1))
            piece = _cmlp_piece(zu_ref[rows, cols], zv_ref[rows, cols], g_ref[:, cols], ws_ref[g], bs_ref[g])
            m_ref[rows, 512 + 128 * g:512 + 128 * (g + 1)] = piece.astype(BF16)

    return pl.pallas_call(
        body, grid=(lay.nt,),
        in_specs=[_row(512, 1), _row(512, 2), _row(512), _full((1, 512)), _full((4, 128, 128)), _full((4, 128, 1))],
        out_specs=_row(1024), out_shape=jax.ShapeDtypeStruct((lay.R, 1024), BF16),
        compiler_params=_cparams(("parallel",)), name=name)(z, z, o, gvn, ws, bs)


def ab_rows_bwd(z, tabc, tabs, dq, dkx, dkh, dvx, dvh, dmerged, gkv, gq, wk, wv, wqn, wqp, gvn, ws, bs, lay, name):
    def body(*refs):
        prep_in = refs[:3] + refs[5:9] + refs[11:17]
        zu_ref, zv_ref = refs[3:5]
        dq_ref, dcm_ref = refs[9:11]
        gvn_ref, ws_ref, bs_ref = refs[17:20]
        dkx_ref, dkh_ref, dvx_ref, dvh_ref = refs[20:24]
        dz_ref, dgkv_ref, dgq_ref, dwk_ref, dwv_ref, dwqn_ref, dwqp_ref, dgvn_ref, dws_ref, dbs_ref = refs[24:]
        first = pl.program_id(0) == 0
        _, vjp = jax.vjp(_ab_prep, *[r[...].astype(F32) for r in prep_in])
        d = vjp((dq_ref[...], _xh_pick(lay, dkx_ref, dkh_ref), _xh_pick(lay, dvx_ref, dvh_ref)))
        dz_ref[:, 0:256] = d[0].astype(BF16)
        dz_ref[:, 256:512] = d[1].astype(BF16)
        dz_ref[:, 1536:1664] = d[2].astype(BF16)
        for ref, val in zip((dgkv_ref, dgq_ref, dwk_ref, dwv_ref, dwqn_ref, dwqp_ref), d[7:]):
            _acc(ref, val, first)
        dws = [0.0] * CMLP_GROUPS
        dbs = [0.0] * CMLP_GROUPS
        dgv = [0.0] * CMLP_GROUPS
        for c, g in _pieces():
            rows, cols = slice(128 * c, 128 * (c + 1)), slice(128 * g, 128 * (g + 1))
            _, vjp = jax.vjp(_cmlp_piece, zu_ref[rows, cols], zv_ref[rows, cols], gvn_ref[:, cols], ws_ref[g], bs_ref[g])
            dzu, dzv, dg_, dws_, dbs_ = vjp(dcm_ref[rows, cols])
            dz_ref[rows, 512 + 128 * g:512 + 128 * (g + 1)] = dzu.astype(BF16)
            dz_ref[rows, 1024 + 128 * g:1024 + 128 * (g + 1)] = dzv.astype(BF16)
            dws[g], dbs[g], dgv[g] = dws[g] + dws_, dbs[g] + dbs_, dgv[g] + dg_
        _acc(dgvn_ref, jnp.concatenate(dgv, axis=1), first)
        _acc(dws_ref, jnp.stack(dws), first)
        _acc(dbs_ref, jnp.stack(dbs), first)

    acc_shapes = [(1, 256), (1, 256), (256, 512), (256, 512), (256, 512), (256, 256), (1, 512), (4, 128, 128), (4, 128, 1)]
    return pl.pallas_call(
        body, grid=(lay.nt,),
        in_specs=_ab_prep_specs(lay)[:3] + [_row(512, 1), _row(512, 2)] + _ab_prep_specs(lay)[3:7]
        + [_row(1024), _row(512, 1)] + _ab_prep_specs(lay)[7:]
        + [_full((1, 512)), _full((4, 128, 128)), _full((4, 128, 1))] + _xh_specs(lay, 1024) + _xh_specs(lay, 512),
        out_specs=[_row(AB_IN_P)] + [_full(s) for s in acc_shapes],
        out_shape=[jax.ShapeDtypeStruct((lay.R, AB_IN_P), BF16)] + [jax.ShapeDtypeStruct(s, F32) for s in acc_shapes],
        compiler_params=_cparams(("arbitrary",)), name=name)(
            z, z, z, z, z, tabc, tabs, tabc, tabs, dq, dmerged, gkv, gq, wk, wv, wqn, wqp, gvn, ws, bs, dkx, dkh, dvx, dvh)


def _cd_prep(zrk, zrq, zsk, zsq0, zsq1, zsv, c256, s256, c128, s128):
    rk = rope(zrk * (RET_QK ** -0.5), c256, s256)
    rq = rope(zrq, c256, s256)
    sk = rope(zsk, c128, s128)
    sq0, sq1 = rope(zsq0, c256, s256), rope(zsq1, c256, s256)
    e0, e1 = _expand_matrix(0), _expand_matrix(1)
    return rq, rk, sq0, sq1, bdot(sk, e0), bdot(sk, e1), bdot(zsv, e0), bdot(zsv, e1)


def _cd_prep_specs(lay):
    return [_row(256, 0), _row(256, 4), _row(128, 6), _row(256, 7), _row(256, 8), _row(128, 7),
            _ropespec(lay, 256), _ropespec(lay, 256), _ropespec(lay, 128), _ropespec(lay, 128)]


def cd_prep_fwd(z, tabc, tabs, lay, name):
    def body(*refs):
        ins, (rq_ref, rk_ref, sq_ref, ke_ref, ve_ref) = refs[:10], refs[10:]
        rq, rk, sq0, sq1, k0, k1, v0, v1 = _cd_prep(*[r[...] for r in ins])
        rq_ref[...] = rq.astype(BF16)
        rk_ref[...] = rk.astype(BF16)
        for ref, (a, b) in ((sq_ref, (sq0, sq1)), (ke_ref, (k0, k1)), (ve_ref, (v0, v1))):
            ref[:, 0:256] = a.astype(BF16)
            ref[:, 256:512] = b.astype(BF16)

    R = lay.R
    return pl.pallas_call(
        body, grid=(lay.nt,), in_specs=_cd_prep_specs(lay),
        out_specs=[_row(256), _row(256), _row(512), _row(512), _row(512)],
        out_shape=[jax.ShapeDtypeStruct((R, w), BF16) for w in (256, 256, 512, 512, 512)],
        compiler_params=_cparams(("parallel",)), name=name)(z, z, z, z, z, z, tabc, tabs, tabc, tabs)


def _ret_sample(h, qs, ks, vs, df, db):
    lgf, lgb = log_sigmoid(df), log_sigmoid(db)
    idx = lax.broadcasted_iota(jnp.int32, (T, 1), 0).astype(F32)
    diff = idx - lax.broadcasted_iota(jnp.int32, (1, T), 1).astype(F32)
    intra = (jnp.where(diff >= 0, jnp.exp(lgf * jnp.maximum(diff, 0.0)), 0.0)
             + jnp.where(diff <= 0, jnp.exp(lgb * jnp.maximum(-diff, 0.0)), 0.0))
    qdf, kdf, cdf = jnp.exp(lgf * (idx + 1.0)), jnp.exp(lgf * (T - 1.0 - idx)), jnp.exp(lgf * T)
    qdb, kdb, cdb = jnp.exp(lgb * (T - idx)), jnp.exp(lgb * idx), jnp.exp(lgb * T)
    mask = _head_mask(qs[0].shape, h)
    qs = [q * mask for q in qs]
    ys = [bdot(bdot_nt(q, k) * intra, v) for q, k, v in zip(qs, ks, vs)]
    n = len(qs)
    state = bdot_tn(ks[0] * kdf, vs[0])
    for i in range(1, n):
        ys[i] = ys[i] + bdot(qs[i] * qdf, state)
        if i + 1 < n:
            state = state * cdf + bdot_tn(ks[i] * kdf, vs[i])
    state = bdot_tn(ks[0] * kdb, vs[0])
    for i in range(n - 1, 0, -1):
        ys[i] = ys[i] + bdot(qs[i] * qdb, state)
        if i > 1:
            state = state * cdb + bdot_tn(ks[i] * kdb, vs[i])
    return ys


def _ret_specs(lay):
    nxt, SEQ = lay.nxt, lay.SEQ
    xs = lambda w, col: pl.BlockSpec((SEQ, w), lambda b, h: (b, col(h)))
    hs = lambda w, col: pl.BlockSpec((T, w), lambda b, h: (nxt + b, col(h)))
    zero, head = (lambda h: 0), (lambda h: 2 + h)
    dec = pl.BlockSpec((None, 8, 128), lambda b, h: (h, 0, 0))
    return [xs(256, zero), hs(256, zero), xs(256, zero), hs(256, zero), xs(128, head), hs(128, head), dec, dec]


def _ret_tiles(x_ref, h_ref, tps, cast=None):
    tiles = [h_ref[...]] + [x_ref[i * T:(i + 1) * T, :] for i in range(tps)]
    return [t.astype(cast) for t in tiles] if cast is not None else tiles


def ret_fwd(rq, rk, z, decf, decb, lay, name):
    tps, SEQ = lay.tps, lay.SEQ

    def body(qx_ref, qh_ref, kx_ref, kh_ref, vx_ref, vh_ref, df_ref, db_ref, yx_ref, yh_ref):
        ys = _ret_sample(pl.program_id(1), _ret_tiles(qx_ref, qh_ref, tps), _ret_tiles(kx_ref, kh_ref, tps),
                         _ret_tiles(vx_ref, vh_ref, tps), df_ref[0:1, 0:1], db_ref[0:1, 0:1])
        yh_ref[...] = ys[0]
        for i in range(tps):
            yx_ref[i * T:(i + 1) * T, :] = ys[i + 1]

    return pl.pallas_call(
        body, grid=(lay.B, RET_HEADS), in_specs=_ret_specs(lay),
        out_specs=[pl.BlockSpec((SEQ, 128), lambda b, h: (b, h)), pl.BlockSpec((T, 128), lambda b, h: (b, h))],
        out_shape=[jax.ShapeDtypeStruct((lay.NX, 512), F32), jax.ShapeDtypeStruct((lay.B * T, 512), F32)],
        compiler_params=_cparams(("parallel", "arbitrary")), name=name)(rq, rq, rk, rk, z, z, decf, decb)


def ret_bwd(rq, rk, z, decf, decb, dy, lay, name):
    tps, SEQ, B = lay.tps, lay.SEQ, lay.B
    nxt = lay.nxt

    def body(qx_ref, qh_ref, kx_ref, kh_ref, vx_ref, vh_ref, df_ref, db_ref, dyx_ref, dyh_ref,
             dqx_ref, dqh_ref, dkx_ref, dkh_ref, dvx_ref, dvh_ref, ddf_ref, ddb_ref):
        h = pl.program_id(1)
        _, vjp = jax.vjp(functools.partial(_ret_sample, h), _ret_tiles(qx_ref, qh_ref, tps, F32),
                         _ret_tiles(kx_ref, kh_ref, tps, F32), _ret_tiles(vx_ref, vh_ref, tps), df_ref[0:1, 0:1],
                         db_ref[0:1, 0:1])
        dqs, dks, dvs, ddf, ddb = vjp(_ret_tiles(dyx_ref, dyh_ref, tps))
        first = h == 0
        _acc(dqh_ref, dqs[0], first)
        _acc(dkh_ref, dks[0], first)
        dvh_ref[...] = dvs[0]
        for i in range(tps):
            rows = slice(i * T, (i + 1) * T)
            _acc(dqx_ref.at[rows], dqs[i + 1], first)
            _acc(dkx_ref.at[rows], dks[i + 1], first)
            dvx_ref[rows, :] = dvs[i + 1]
        @pl.when(jnp.logical_and(pl.program_id(0) == 0, first))
        def _():
            ddf_ref[...] = jnp.zeros(ddf_ref.shape, F32)
            ddb_ref[...] = jnp.zeros(ddb_ref.shape, F32)

        ddf_ref[h] += jnp.broadcast_to(ddf, (8, 128))
        ddb_ref[h] += jnp.broadcast_to(ddb, (8, 128))

    acc_x, acc_h = pl.BlockSpec((SEQ, 256), lambda b, h: (b, 0)), pl.BlockSpec((T, 256), lambda b, h: (b, 0))
    head_x, head_h = pl.BlockSpec((SEQ, 128), lambda b, h: (b, h)), pl.BlockSpec((T, 128), lambda b, h: (b, h))
    dec = pl.BlockSpec((RET_HEADS, 8, 128), lambda b, h: (0, 0, 0))
    return pl.pallas_call(
        body, grid=(B, RET_HEADS),
        in_specs=_ret_specs(lay) + [head_x, pl.BlockSpec((T, 128), lambda b, h: (nxt + b, h))],
        out_specs=[acc_x, acc_h, acc_x, acc_h, head_x, head_h, dec, dec],
        out_shape=[jax.ShapeDtypeStruct((lay.NX, 256), F32), jax.ShapeDtypeStruct((B * T, 256), F32),
                   jax.ShapeDtypeStruct((lay.NX, 256), F32), jax.ShapeDtypeStruct((B * T, 256), F32),
                   jax.ShapeDtypeStruct((lay.NX, 512), F32), jax.ShapeDtypeStruct((B * T, 512), F32),
                   jax.ShapeDtypeStruct((RET_HEADS, 8, 128), F32), jax.ShapeDtypeStruct((RET_HEADS, 8, 128), F32)],
        compiler_params=_cparams(("arbitrary", "arbitrary")), name=name)(rq, rq, rk, rk, z, z, decf, decb, dy, dy)


_SWA_SCALE = SWA_HEAD_DIM ** -0.5


def _swa_head(qh, sw, kh, vw, vh, sink):
    sh = bdot_nt(qh, kh) * _SWA_SCALE
    m = jnp.maximum(jnp.max(sh, axis=-1, keepdims=True), sink)
    if sw is not None:
        m = jnp.maximum(m, jnp.max(sw, axis=-1, keepdims=True))
    m = lax.stop_gradient(m)
    eh, es = jnp.exp(sh - m), jnp.exp(sink - m)
    tot = jnp.sum(eh, axis=-1, keepdims=True) + es
    if sw is None:
        return bdot(eh * (1.0 / tot), vh)
    ew = jnp.exp(sw - m)
    inv = 1.0 / (tot + jnp.sum(ew, axis=-1, keepdims=True))
    return bdot(ew * inv, vw) + bdot(eh * inv, vh)


def _swa_x(t0, kpos0, sq, kw, vw, kh, vh, *sinks):
    t = t0 + lax.broadcasted_iota(jnp.int32, (T, 1), 0)
    pos = kpos0 + lax.broadcasted_iota(jnp.int32, (1, SWA_SPAN), 1)
    band = jnp.abs(t - pos) <= SWA_WINDOW
    out = 0.0
    for i in range(SWA_GROUPS):
        mi = _head_mask(sq.shape, i)
        qh = sq * mi
        sw = jnp.where(band, bdot_nt(qh, kw) * _SWA_SCALE, NEG_INF)
        out = out + _swa_head(qh, sw, kh, vw, vh, sinks[i]) * mi
    return out


def _swa_h(sq, kh, vh, *sinks):
    out = 0.0
    for i in range(SWA_GROUPS):
        mi = _head_mask(sq.shape, i)
        out = out + _swa_head(sq * mi, None, kh, None, vh, sinks[i]) * mi
    return out


def _swa_specs(lay):
    nxt, SEQ = lay.nxt, lay.SEQ
    return [pl.BlockSpec((T, 256), lambda g, b, qi: (lay.qrow(b, qi), g)),
            pl.BlockSpec((SEQ, 256), lambda g, b, qi: (b, g)), pl.BlockSpec((SEQ, 256), lambda g, b, qi: (b, g)),
            pl.BlockSpec((T, 256), lambda g, b, qi: (nxt + b, g)), pl.BlockSpec((T, 256), lambda g, b, qi: (nxt + b, g)),
            pl.BlockSpec((None, 4, 8, 128), lambda g, b, qi: (g, 0, 0, 0))]


def _swa_start(qi, SEQ):
    return pl.multiple_of(jnp.clip(qi * T - SWA_WINDOW, 0, SEQ - SWA_SPAN), SWA_WINDOW)


def swa_fwd(sq, kexp, vexp, sink, lay, name):
    tps, SEQ = lay.tps, lay.SEQ

    def body(sq_ref, kx_ref, vx_ref, kh_ref, vh_ref, sink_ref, o_ref):
        qi = pl.program_id(2)
        f = lambda r: r[...]
        sinks = [sink_ref[i][0:1, 0:1] for i in range(SWA_GROUPS)]

        @pl.when(qi < tps)
        def _():
            k0 = _swa_start(qi, SEQ)
            kw, vw = kx_ref[pl.ds(k0, SWA_SPAN), :], vx_ref[pl.ds(k0, SWA_SPAN), :]
            o_ref[...] = _swa_x(qi * T, k0, f(sq_ref), kw, vw, f(kh_ref), f(vh_ref), *sinks).astype(BF16)

        @pl.when(qi == tps)
        def _():
            o_ref[...] = _swa_h(f(sq_ref), f(kh_ref), f(vh_ref), *sinks).astype(BF16)

    return pl.pallas_call(
        body, grid=(SWA_KV_HEADS, lay.B, lay.nq), in_specs=_swa_specs(lay),
        out_specs=pl.BlockSpec((T, 256), lambda g, b, qi: (lay.qrow(b, qi), g)),
        out_shape=jax.ShapeDtypeStruct((lay.R, 512), BF16),
        compiler_params=_cparams(("parallel", "parallel", "arbitrary")), name=name)(sq, kexp, vexp, kexp, vexp, sink)


def swa_bwd(sq, kexp, vexp, sink, dmerged, lay, name):
    tps, SEQ, B = lay.tps, lay.SEQ, lay.B

    def body(sq_ref, kx_ref, vx_ref, kh_ref, vh_ref, sink_ref, do_ref, dsq_ref, dkx_ref, dkh_ref, dvx_ref, dvh_ref, dsink_ref):
        b, qi = pl.program_id(1), pl.program_id(2)
        f = lambda r: r[...].astype(F32)
        sinks = [sink_ref[i][0:1, 0:1] for i in range(SWA_GROUPS)]
        very_first = jnp.logical_and(b == 0, qi == 0)

        def acc_sink(ds):
            for i in range(SWA_GROUPS):
                _acc(dsink_ref.at[i], jnp.broadcast_to(ds[i], (8, 128)), very_first)

        @pl.when(qi == 0)
        def _():
            for ref in (dkx_ref, dkh_ref, dvx_ref, dvh_ref):
                ref[...] = jnp.zeros(ref.shape, F32)

        @pl.when(qi < tps)
        def _():
            k0 = _swa_start(qi, SEQ)
            win = pl.ds(k0, SWA_SPAN)
            kw, vw = kx_ref[win, :].astype(F32), vx_ref[win, :].astype(F32)
            _, vjp = jax.vjp(functools.partial(_swa_x, qi * T, k0), f(sq_ref), kw, vw, f(kh_ref), f(vh_ref), *sinks)
            d = vjp(do_ref[...])
            dsq_ref[...] = d[0]
            dkx_ref[win, :] += d[1]
            dvx_ref[win, :] += d[2]
            dkh_ref[...] += d[3]
            dvh_ref[...] += d[4]
            acc_sink(d[5:9])

        @pl.when(qi == tps)
        def _():
            _, vjp = jax.vjp(_swa_h, f(sq_ref), f(kh_ref), f(vh_ref), *sinks)
            d = vjp(do_ref[...])
            dsq_ref[...] = d[0]
            dkh_ref[...] += d[1]
            dvh_ref[...] += d[2]
            acc_sink(d[3:7])

    xs = pl.BlockSpec((SEQ, 256), lambda g, b, qi: (b, g))
    hs = pl.BlockSpec((T, 256), lambda g, b, qi: (b, g))
    return pl.pallas_call(
        body, grid=(SWA_KV_HEADS, B, lay.nq),
        in_specs=_swa_specs(lay) + [pl.BlockSpec((T, 256), lambda g, b, qi: (lay.qrow(b, qi), 2 + g))],
        out_specs=[pl.BlockSpec((T, 256), lambda g, b, qi: (lay.qrow(b, qi), g)), xs, hs, xs, hs,
                   pl.BlockSpec((None, 4, 8, 128), lambda g, b, qi: (g, 0, 0, 0))],
        out_shape=[jax.ShapeDtypeStruct((lay.R, 512), F32), jax.ShapeDtypeStruct((lay.NX, 512), F32),
                   jax.ShapeDtypeStruct((B * T, 512), F32), jax.ShapeDtypeStruct((lay.NX, 512), F32),
                   jax.ShapeDtypeStruct((B * T, 512), F32), jax.ShapeDtypeStruct((SWA_KV_HEADS, 4, 8, 128), F32)],
        compiler_params=_cparams(("arbitrary", "arbitrary", "arbitrary")), name=name)(
            sq, kexp, vexp, kexp, vexp, sink, dmerged)


def _cd_merge_piece(y, rg, g):
    return (y * lax.rsqrt(jnp.mean(y * y, axis=-1, keepdims=True) + EPS) * g) * (rg * jax.nn.sigmoid(rg))


def cd_merge_fwd(y, z, o, gn, lay, name):
    def body(yx_ref, yh_ref, rga_ref, rgb_ref, o_ref, g_ref, m_ref):
        y = _xh_pick(lay, yx_ref, yh_ref)
        for h in range(RET_HEADS):
            cols = slice(128 * h, 128 * (h + 1))
            rg_ref, rcols = (rga_ref, cols) if h < 2 else (rgb_ref, slice(128 * (h - 2), 128 * (h - 1)))
            m_ref[:, cols] = _cd_merge_piece(y[:, cols], rg_ref[:, rcols], g_ref[:, cols]).astype(BF16)
        m_ref[:, 512:1024] = o_ref[...]

    return pl.pallas_call(
        body, grid=(lay.nt,), in_specs=_xh_specs(lay, 512) + [_row(256, 5), _row(256, 6), _row(512), _full((1, 512))],
        out_specs=_row(1024), out_shape=jax.ShapeDtypeStruct((lay.R, 1024), BF16),
        compiler_params=_cparams(("parallel",)), name=name)(*y, z, z, o, gn)


def cd_merge_bwd(y, z, gn, dmerged, lay, name):
    def body(yx_ref, yh_ref, rga_ref, rgb_ref, g_ref, dm_ref, dy_ref, drg_ref, dg_ref):
        first = pl.program_id(0) == 0
        y = _xh_pick(lay, yx_ref, yh_ref)
        dgs = []
        for h in range(RET_HEADS):
            cols = slice(128 * h, 128 * (h + 1))
            rg_ref, rcols = (rga_ref, cols) if h < 2 else (rgb_ref, slice(128 * (h - 2), 128 * (h - 1)))
            _, vjp = jax.vjp(_cd_merge_piece, y[:, cols], rg_ref[:, rcols], g_ref[:, cols])
            dy, drg, dg = vjp(dm_ref[:, cols])
            dy_ref[:, cols] = dy
            drg_ref[:, cols] = drg
            dgs.append(dg)
        _acc(dg_ref, jnp.concatenate(dgs, axis=1), first)

    return pl.pallas_call(
        body, grid=(lay.nt,), in_specs=_xh_specs(lay, 512) + [_row(256, 5), _row(256, 6), _full((1, 512)), _row(512, 0)],
        out_specs=[_row(512), _row(512), _full((1, 512))],
        out_shape=[jax.ShapeDtypeStruct((lay.R, 512), F32), jax.ShapeDtypeStruct((lay.R, 512), F32),
                   jax.ShapeDtypeStruct((1, 512), F32)],
        compiler_params=_cparams(("arbitrary",)), name=name)(*y, z, z, gn, dmerged)


def cd_rows_bwd(z, tabc, tabs, dsq, drg, drq, drk, dke, dve, drv, lay, name):
    def body(*refs):
        ins = refs[:10]
        dsq_ref, drg_ref = refs[10:12]
        drq, drk, dke, dve, drv = (_xh_pick(lay, refs[12 + 2 * n], refs[13 + 2 * n]) for n in range(5))
        dz_ref = refs[22]
        _, vjp = jax.vjp(_cd_prep, *[r[...] for r in ins])
        cts = (drq, drk, dsq_ref[:, 0:256], dsq_ref[:, 256:512], dke[:, 0:256], dke[:, 256:512],
               dve[:, 0:256], dve[:, 256:512])
        dzrk, dzrq, dzsk, dzsq0, dzsq1, dzsv = vjp(cts)[:6]
        dz_ref[:, 0:256] = dzrk.astype(BF16)
        dz_ref[:, 256:768] = drv.astype(BF16)
        dz_ref[:, 768:896] = dzsk.astype(BF16)
        dz_ref[:, 896:1024] = dzsv.astype(BF16)
        dz_ref[:, 1024:1280] = dzrq.astype(BF16)
        dz_ref[:, 1280:1792] = drg_ref[...].astype(BF16)
        dz_ref[:, 1792:2048] = dzsq0.astype(BF16)
        dz_ref[:, 2048:2304] = dzsq1.astype(BF16)

    return pl.pallas_call(
        body, grid=(lay.nt,),
        in_specs=_cd_prep_specs(lay) + [_row(512), _row(512)] + _xh_specs(lay, 256) + _xh_specs(lay, 256)
        + _xh_specs(lay, 512) + _xh_specs(lay, 512) + _xh_specs(lay, 512),
        out_specs=_row(2304), out_shape=jax.ShapeDtypeStruct((lay.R, 2304), BF16),
        compiler_params=_cparams(("parallel",)), name=name)(
            z, z, z, z, z, z, tabc, tabs, tabc, tabs, dsq, drg, *drq, *drk, *dke, *dve, *drv)


def _pos():
    return lax.axis_index("x"), lax.axis_index("y"), lax.axis_index("c")


def _flip(v, bit):
    return 1 - v if bit else v


def _comm_call(name, body, ins, out_shapes, n_remote, n_local, aliases=None):
    return pl.pallas_call(
        body, in_specs=[ANY] * len(ins), out_specs=[ANY] * len(out_shapes), out_shape=out_shapes,
        scratch_shapes=[pltpu.SemaphoreType.DMA((n_remote,)), pltpu.SemaphoreType.DMA((n_remote,)),
                        pltpu.SemaphoreType.DMA((n_local,))],
        input_output_aliases=aliases or {}, name=name)(*ins)


def gather8(arr, name):
    def body(a_ref, o_ref, ssem, rsem, lsem):
        x, y, c = _pos()
        me = 4 * x + 2 * y + c
        loc = pltpu.make_async_copy(a_ref, o_ref.at[me], lsem.at[0])
        loc.start()
        cps = []
        for m in range(1, 8):
            peer = (_flip(x, m & 4), _flip(y, m & 2), _flip(c, m & 1))
            cps.append(pltpu.make_async_remote_copy(a_ref, o_ref.at[me], ssem.at[m - 1], rsem.at[m - 1],
                                                    device_id=peer, device_id_type=MESH))
            cps[-1].start()
        for cp in cps:
            cp.wait()
        loc.wait()

    return _comm_call(name, body, [arr], [jax.ShapeDtypeStruct((8,) + arr.shape, arr.dtype)], 7, 1)[0]


def gather_chips(arr, name):
    def body(a_ref, o_ref, ssem, rsem, lsem):
        x, y, c = _pos()
        k = 2 * x + y
        loc = pltpu.make_async_copy(a_ref, o_ref.at[k], lsem.at[0])
        loc.start()
        cps = []
        for m in range(1, 4):
            peer = (_flip(x, m & 2), _flip(y, m & 1), c)
            cps.append(pltpu.make_async_remote_copy(a_ref, o_ref.at[k], ssem.at[m - 1], rsem.at[m - 1],
                                                    device_id=peer, device_id_type=MESH))
            cps[-1].start()
        for cp in cps:
            cp.wait()
        loc.wait()

    return _comm_call(name, body, [arr], [jax.ShapeDtypeStruct((4,) + arr.shape, arr.dtype)], 3, 1)[0]


def gather_weights(arrs, name):
    n = len(arrs)

    def body(*refs):
        a_refs, o_refs, (isend, irecv, _) = refs[:n], refs[n:2 * n], refs[2 * n:]
        x, y, c = _pos()
        k = 2 * x + y
        sib = (x, y, 1 - c)
        chips = [(m, (_flip(x, m & 2), _flip(y, m & 1)), 2 * _flip(x, m & 2) + _flip(y, m & 1)) for m in range(1, 4)]
        waits = []
        for w, (a, o) in enumerate(zip(a_refs, o_refs)):
            H = a.shape[0] // 2
            own = pl.ds(c * H, H)
            first = [pltpu.make_async_remote_copy(a.at[own], o.at[k, own], isend.at[6 * w + m - 1], irecv.at[6 * w + m - 1],
                                                  device_id=(*chip, c), device_id_type=MESH) for m, chip, _ in chips]
            for cp in first:
                cp.start()
            waits.append((first, H, own, a, o, w))
        for first, H, own, a, o, w in waits:
            passed = []
            for (m, chip, kk), cp in zip(chips, first):
                pltpu.make_async_remote_copy(a.at[own], o.at[kk, own], isend.at[6 * w + m - 1], irecv.at[6 * w + m - 1],
                                             device_id=(*chip, c), device_id_type=MESH).wait_recv()
                fw = pltpu.make_async_remote_copy(o.at[kk, own], o.at[kk, own], isend.at[6 * w + 2 + m], irecv.at[6 * w + 2 + m],
                                                  device_id=sib, device_id_type=MESH)
                fw.start()
                passed.append(fw)
            for fw in passed:
                fw.wait_recv()
            for cp in first + passed:
                cp.wait_send()

    outs = [jax.ShapeDtypeStruct((4,) + a.shape, a.dtype) for a in arrs]
    return _comm_call(name, body, list(arrs), outs, 6 * n, 1)


def swap_other_half(arrs, name):
    n = len(arrs)

    def body(*refs):
        a_refs, o_refs, (ssem, rsem, _) = refs[:n], refs[n:2 * n], refs[2 * n:]
        x, y, c = _pos()
        cps = []
        for w, (a, o) in enumerate(zip(a_refs, o_refs)):
            H = a.shape[1] // 2
            cps.append(pltpu.make_async_remote_copy(a.at[:, pl.ds((1 - c) * H, H)], o, ssem.at[w], rsem.at[w],
                                                    device_id=(x, y, 1 - c), device_id_type=MESH))
            cps[-1].start()
        for cp in cps:
            cp.wait()

    outs = [jax.ShapeDtypeStruct((4, a.shape[1] // 2) + a.shape[2:], a.dtype) for a in arrs]
    return _comm_call(name, body, list(arrs), outs, n, 1)


def exchange_chips(arrs, name):
    n = len(arrs)

    def body(*refs):
        a_refs, o_refs, (ssem, rsem, lsem) = refs[:n], refs[n:2 * n], refs[2 * n:]
        x, y, c = _pos()
        k = 2 * x + y
        cps = []
        for w, (a, o) in enumerate(zip(a_refs, o_refs)):
            cps.append(pltpu.make_async_copy(a.at[k], o.at[k], lsem.at[w]))
            cps[-1].start()
            for m in range(1, 4):
                px, py = _flip(x, m & 2), _flip(y, m & 1)
                cps.append(pltpu.make_async_remote_copy(a.at[2 * px + py], o.at[k], ssem.at[3 * w + m - 1], rsem.at[3 * w + m - 1],
                                                        device_id=(px, py, c), device_id_type=MESH))
                cps[-1].start()
        for cp in cps:
            cp.wait()

    return _comm_call(name, body, list(arrs), [jax.ShapeDtypeStruct(a.shape, a.dtype) for a in arrs], 3 * n, n)


def share_halves(arrs, name):
    n = len(arrs)

    def body(*refs):
        o_refs, (ssem, rsem, _) = refs[n:2 * n], refs[2 * n:]
        x, y, c = _pos()
        cps = []
        for w, o in enumerate(o_refs):
            H = o.shape[0] // 2
            mine = o.at[pl.ds(c * H, H)]
            cps.append(pltpu.make_async_remote_copy(mine, mine, ssem.at[w], rsem.at[w], device_id=(x, y, 1 - c),
                                                    device_id_type=MESH))
            cps[-1].start()
        for cp in cps:
            cp.wait()

    outs = [jax.ShapeDtypeStruct(a.shape, a.dtype) for a in arrs]
    return _comm_call(name, body, list(arrs), outs, n, 1, aliases={i: i for i in range(n)})


def _adamw(w, g, m, v):
    m = ADAM_B1 * m + (1.0 - ADAM_B1) * g
    v = ADAM_B2 * v + (1.0 - ADAM_B2) * (g * g)
    m_hat = m / (1.0 - ADAM_B1 ** ADAM_STEP)
    v_hat = v / (1.0 - ADAM_B2 ** ADAM_STEP)
    return -ADAM_LR * (m_hat / (jnp.sqrt(v_hat) + ADAM_EPS) + ADAM_WD * w), m, v


def _rows_tile(R, C):
    return _pick(R, max(8, (2 << 20) // (4 * C) // 8 * 8), 8)


def chip_partial(gs, buf, cidx, name):
    _, L, R, C = gs.shape
    H, tr = L // 2, _rows_tile(R, C)

    def body(c_ref, g_ref, b_ref, o_ref):
        o_ref[...] = (g_ref[...] + b_ref[...]).astype(BF16)

    return pl.pallas_call(
        body, grid_spec=pltpu.PrefetchScalarGridSpec(
            num_scalar_prefetch=1, grid=(4, H, R // tr),
            in_specs=[pl.BlockSpec((None, None, tr, C), lambda s, l, r, c: (s, c[0] * H + l, r, 0)),
                      pl.BlockSpec((None, None, tr, C), lambda s, l, r, c: (s, l, r, 0))],
            out_specs=pl.BlockSpec((None, None, tr, C), lambda s, l, r, c: (s, l, r, 0))),
        out_shape=jax.ShapeDtypeStruct((4, H, R, C), BF16),
        compiler_params=_cparams(("parallel", "parallel", "parallel")), name=name)(cidx, gs, buf)


def adam_sharded(parts, w, m, v, cidx, name):
    _, H, R, C = parts.shape
    tr = _rows_tile(R, C)

    def body(c_ref, p_ref, w_ref, m_ref, v_ref, g_out, d_out, m_out, v_out):
        g = p_ref[0].astype(F32)
        for s in range(1, 4):
            g = g + p_ref[s].astype(F32)
        g_out[...] = g
        d_out[...], m_out[...], v_out[...] = _adamw(w_ref[...], g, m_ref[...], v_ref[...])

    own = pl.BlockSpec((None, tr, C), lambda l, r, c: (c[0] * H + l, r, 0))
    return pl.pallas_call(
        body, grid_spec=pltpu.PrefetchScalarGridSpec(
            num_scalar_prefetch=1, grid=(H, R // tr),
            in_specs=[pl.BlockSpec((4, None, tr, C), lambda l, r, c: (0, l, r, 0)), own, own, own],
            out_specs=[own, own, own, own]),
        out_shape=[jax.ShapeDtypeStruct((2 * H, R, C), F32)] * 4,
        compiler_params=_cparams(("parallel", "parallel")), name=name)(cidx, parts, w, m, v)


def sum8(arr, name):
    n = arr.shape[1]
    tr = _pick(n, 512, 8)

    def body(a_ref, o_ref):
        s = a_ref[0]
        for j in range(1, 8):
            s = s + a_ref[j]
        o_ref[...] = s

    return pl.pallas_call(
        body, grid=(n // tr,), in_specs=[pl.BlockSpec((8, tr, 128), lambda i: (0, i, 0))],
        out_specs=pl.BlockSpec((tr, 128), lambda i: (i, 0)), out_shape=jax.ShapeDtypeStruct((n, 128), F32),
        compiler_params=_cparams(("parallel",)), name=name)(arr)


def adam_rows(w, g, m, v, name):
    n, C = w.shape
    tr = _rows_tile(n, C)

    def body(w_ref, g_ref, m_ref, v_ref, d_out, m_out, v_out):
        d_out[...], m_out[...], v_out[...] = _adamw(w_ref[...], g_ref[...], m_ref[...], v_ref[...])

    spec = pl.BlockSpec((tr, C), lambda i: (i, 0))
    return pl.pallas_call(
        body, grid=(n // tr,), in_specs=[spec] * 4, out_specs=[spec] * 3,
        out_shape=[jax.ShapeDtypeStruct((n, C), F32)] * 3, compiler_params=_cparams(("parallel",)), name=name)(w, g, m, v)


def _silu(c):
    return c * jax.nn.sigmoid(c)


def ada_fwd(c_all, w, b, name):
    NC, D = c_all.shape
    L, _, Wc = w.shape
    tn = _pick(Wc, 512, 128)

    def body(c_ref, w_ref, b_ref, o_ref):
        o_ref[...] = bdot(_silu(c_ref[...]), w_ref[...]) + b_ref[...]

    return pl.pallas_call(
        body, grid=(L, Wc // tn),
        in_specs=[pl.BlockSpec((NC, D), lambda l, j: (0, 0)), pl.BlockSpec((None, D, tn), lambda l, j: (l, 0, j)),
                  pl.BlockSpec((None, 1, tn), lambda l, j: (l, 0, j))],
        out_specs=pl.BlockSpec((None, NC, tn), lambda l, j: (l, 0, j)), out_shape=jax.ShapeDtypeStruct((L, NC, Wc), F32),
        compiler_params=_cparams(("parallel", "parallel")), name=name)(c_all, w, b)


def ada_bwd(c_all, w, dmod, name):
    NC, D = c_all.shape
    L, _, Wc = w.shape
    tn = _pick(Wc, 512, 128)

    def body(c_ref, w_ref, d_ref, dw_ref, db_ref, dc_ref):
        first = jnp.logical_and(pl.program_id(0) == 0, pl.program_id(1) == 0)
        f = lambda cs, ww: bdot(cs, ww)
        _, vjp = jax.vjp(f, _silu(c_ref[...]), w_ref[...])
        dcs, dw = vjp(d_ref[...])
        dw_ref[...] = dw
        db_ref[...] = jnp.sum(d_ref[...], axis=0, keepdims=True)
        _acc(dc_ref, dcs, first)

    return pl.pallas_call(
        body, grid=(L, Wc // tn),
        in_specs=[pl.BlockSpec((NC, D), lambda l, j: (0, 0)), pl.BlockSpec((None, D, tn), lambda l, j: (l, 0, j)),
                  pl.BlockSpec((None, NC, tn), lambda l, j: (l, 0, j))],
        out_specs=[pl.BlockSpec((None, D, tn), lambda l, j: (l, 0, j)), pl.BlockSpec((None, 1, tn), lambda l, j: (l, 0, j)),
                   pl.BlockSpec((NC, D), lambda l, j: (0, 0))],
        out_shape=[jax.ShapeDtypeStruct((L, D, Wc), F32), jax.ShapeDtypeStruct((L, 1, Wc), F32),
                   jax.ShapeDtypeStruct((NC, D), F32)],
        compiler_params=_cparams(("arbitrary", "arbitrary")), name=name)(c_all, w, dmod)


def cctx_grad(dcs_twice, c_ctx, name):
    def body(d_ref, c_ref, o_ref):
        _, vjp = jax.vjp(_silu, c_ref[...])
        o_ref[...] = vjp(0.5 * d_ref[...])[0]

    D = c_ctx.shape[1]
    return pl.pallas_call(body, out_shape=jax.ShapeDtypeStruct((8, D), F32), name=name)(dcs_twice, c_ctx)


def _rope_tables(SEQ):
    t = jnp.arange(SEQ)
    row, col = (t // GRID_W).astype(F32), (t % GRID_W).astype(F32)
    n_freq = 16
    freqs = ROPE_THETA ** (-jnp.arange(n_freq, dtype=F32) / n_freq)
    ang = jnp.concatenate([row[:, None] * freqs, col[:, None] * freqs], axis=-1)
    cos, sin = jnp.cos(ang), jnp.sin(ang)
    c = jnp.tile(jnp.concatenate([cos, cos], axis=1), (1, 4))
    s = jnp.tile(jnp.concatenate([-sin, sin], axis=1), (1, 4))
    return (jnp.concatenate([c, jnp.ones((T, 256), F32)], axis=0), jnp.concatenate([s, jnp.zeros((T, 256), F32)], axis=0))


def _unshard_cols(g):
    return jnp.transpose(g, (1, 2, 0, 3)).reshape(g.shape[1], g.shape[2], 4 * g.shape[3])


def _unshard_rows(g):
    return jnp.transpose(g, (1, 0, 2, 3)).reshape(g.shape[1], 4 * g.shape[2], g.shape[3])


def _shard_cols(w, n=4):
    L, R, C = w.shape
    return jnp.transpose(w.reshape(L, R, n, C // n), (2, 0, 1, 3))


def _shard_rows(w):
    L, R, C = w.shape
    return jnp.transpose(w.reshape(L, 4, R // 4, C), (1, 0, 2, 3))


def _ab_in_permute(w):
    L, D, _ = w.shape
    return jnp.concatenate([w[..., 0:256], w[..., 320:1600], w[..., 256:320], jnp.zeros((L, D, 64), w.dtype)], axis=-1)


def _ab_in_unpermute(g):
    return jnp.concatenate([g[..., 0:256], g[..., 1536:1600], g[..., 256:1536]], axis=-1)


def _split_heads(w, a):
    L, K, N = w.shape
    w4 = w.reshape(L, K, 4, N // 4)
    return w4[..., :a].reshape(L, K, 4 * a), w4[..., a:].reshape(L, K, N - 4 * a)


def _join_heads(p, q):
    L, K = p.shape[:2]
    return jnp.concatenate([p.reshape(L, K, 4, -1), q.reshape(L, K, 4, -1)], axis=-1).reshape(L, K, -1)


def _pack(arrs):
    parts = []
    for a in arrs:
        f = a.reshape(-1).astype(F32)
        parts.append(jnp.pad(f, (0, (-f.shape[0]) % 1024)))
    return jnp.concatenate(parts).reshape(-1, 128)


def _unpack(buf, like):
    flat, out, off = buf.reshape(-1), [], 0
    for a in like:
        n = math.prod(a.shape)
        out.append(flat[off:off + n].reshape(a.shape))
        off += n + (-n) % 1024
    return out


_SMALL = ("c_ctx", "ada_b", "norm_mix", "norm_ffn", "norm_final", "mla_q_norm", "mla_kv_norm", "cmlp_v_norm", "cmlp_ws",
          "cmlp_bs", "ret_decay_fwd", "ret_decay_bwd", "ret_norm", "swa_sink")
_BIG = ("ffn_in", "ffn_out", "ab_in", "ab_out", "mla_wq_b", "mla_wkv_b", "cd_in", "cd_out")
_WEIGHTS = ("c_ctx", "ada_w", "ada_b", "norm_mix", "norm_ffn", "norm_final", "ffn_in", "ffn_out", "ab_in", "ab_out",
            "mla_q_norm", "mla_kv_norm", "mla_wq_b", "mla_wkv_b", "cmlp_v_norm", "cmlp_ws", "cmlp_bs", "cd_in", "cd_out",
            "ret_decay_fwd", "ret_decay_bwd", "ret_norm", "swa_sink")


def kernel(x, c, ctx, c_ctx, ada_w, ada_b, norm_mix, norm_ffn, norm_final, ffn_in, ffn_out, ab_in, ab_out, mla_q_norm, mla_kv_norm, mla_wq_b, mla_wkv_b, cmlp_v_norm, cmlp_ws, cmlp_bs, cd_in, cd_out, ret_decay_fwd, ret_decay_bwd, ret_norm, swa_sink, loss_target, m_c_ctx, m_ada_w, m_ada_b, m_norm_mix, m_norm_ffn, m_norm_final, m_ffn_in, m_ffn_out, m_ab_in, m_ab_out, m_mla_q_norm, m_mla_kv_norm, m_mla_wq_b, m_mla_wkv_b, m_cmlp_v_norm, m_cmlp_ws, m_cmlp_bs, m_cd_in, m_cd_out, m_ret_decay_fwd, m_ret_decay_bwd, m_ret_norm, m_swa_sink, v_c_ctx, v_ada_w, v_ada_b, v_norm_mix, v_norm_ffn, v_norm_final, v_ffn_in, v_ffn_out, v_ab_in, v_ab_out, v_mla_q_norm, v_mla_kv_norm, v_mla_wq_b, v_mla_wkv_b, v_cmlp_v_norm, v_cmlp_ws, v_cmlp_bs, v_cd_in, v_cd_out, v_ret_decay_fwd, v_ret_decay_bwd, v_ret_norm, v_swa_sink):
    W = dict(c_ctx=c_ctx, ada_w=ada_w, ada_b=ada_b, norm_mix=norm_mix, norm_ffn=norm_ffn, norm_final=norm_final, ffn_in=ffn_in, ffn_out=ffn_out, ab_in=ab_in, ab_out=ab_out, mla_q_norm=mla_q_norm, mla_kv_norm=mla_kv_norm, mla_wq_b=mla_wq_b, mla_wkv_b=mla_wkv_b, cmlp_v_norm=cmlp_v_norm, cmlp_ws=cmlp_ws, cmlp_bs=cmlp_bs, cd_in=cd_in, cd_out=cd_out, ret_decay_fwd=ret_decay_fwd, ret_decay_bwd=ret_decay_bwd, ret_norm=ret_norm, swa_sink=swa_sink)
    M1 = dict(c_ctx=m_c_ctx, ada_w=m_ada_w, ada_b=m_ada_b, norm_mix=m_norm_mix, norm_ffn=m_norm_ffn, norm_final=m_norm_final, ffn_in=m_ffn_in, ffn_out=m_ffn_out, ab_in=m_ab_in, ab_out=m_ab_out, mla_q_norm=m_mla_q_norm, mla_kv_norm=m_mla_kv_norm, mla_wq_b=m_mla_wq_b, mla_wkv_b=m_mla_wkv_b, cmlp_v_norm=m_cmlp_v_norm, cmlp_ws=m_cmlp_ws, cmlp_bs=m_cmlp_bs, cd_in=m_cd_in, cd_out=m_cd_out, ret_decay_fwd=m_ret_decay_fwd, ret_decay_bwd=m_ret_decay_bwd, ret_norm=m_ret_norm, swa_sink=m_swa_sink)
    M2 = dict(c_ctx=v_c_ctx, ada_w=v_ada_w, ada_b=v_ada_b, norm_mix=v_norm_mix, norm_ffn=v_norm_ffn, norm_final=v_norm_final, ffn_in=v_ffn_in, ffn_out=v_ffn_out, ab_in=v_ab_in, ab_out=v_ab_out, mla_q_norm=v_mla_q_norm, mla_kv_norm=v_mla_kv_norm, mla_wq_b=v_mla_wq_b, mla_wkv_b=v_mla_wkv_b, cmlp_v_norm=v_cmlp_v_norm, cmlp_ws=v_cmlp_ws, cmlp_bs=v_cmlp_bs, cd_in=v_cd_in, cd_out=v_cd_out, ret_decay_fwd=v_ret_decay_fwd, ret_decay_bwd=v_ret_decay_bwd, ret_norm=v_ret_norm, swa_sink=v_swa_sink)

    B, SEQ, D = x.shape
    CTX = ctx.shape[1]
    lay = Layout(B, SEQ, CTX, D)
    nt, NX = lay.nt, lay.NX
    ix, iy, ic = lax.axis_index("x"), lax.axis_index("y"), lax.axis_index("c")
    chip, me = 2 * ix + iy, 4 * ix + 2 * iy + ic
    cidx = jnp.reshape(ic, (1,)).astype(jnp.int32)
    Wc = ada_w.shape[2]
    n_even, n_odd = ab_in.shape[0], cd_in.shape[0]

    rn_row = jnp.pad(ret_norm.reshape(1, -1), ((0, 0), (0, D - ret_norm.size)))
    pack0 = jnp.concatenate([c, rn_row, jnp.zeros((8 - (B + 1) % 8, D), F32)], axis=0) if (B + 1) % 8 else jnp.concatenate([c, rn_row], axis=0)
    g0 = gather8(pack0, "gather_cond")
    NC = -(-(8 * B + 1) // 16) * 16
    c_all = jnp.concatenate([g0[:, :B].reshape(8 * B, D), c_ctx[None], jnp.zeros((NC - 8 * B - 1, D), F32)], axis=0)
    rn_sh = ret_norm.shape[1]
    ret_norm_full = jnp.transpose(g0[0::2, B, :ret_norm.size].reshape(4, n_odd, rn_sh), (1, 0, 2)).reshape(n_odd, 4 * rn_sh)

    shards = [W[n].astype(BF16) for n in _BIG]
    gw = [lax.dynamic_update_index_in_dim(g, s, chip, 0) for g, s in zip(gather_weights(shards, "gather_weights"), shards)]
    w_ffn_in, w_ab_in, w_wq, w_wkv, w_cd_in = (_unshard_cols(gw[i]) for i in (0, 2, 4, 5, 6))
    w_ffn_out, w_ab_out, w_cd_out = (_unshard_rows(gw[i]) for i in (1, 3, 7))
    w_ab_in = _ab_in_permute(w_ab_in)
    w_ffn_out_t = jnp.transpose(w_ffn_out, (0, 2, 1))
    w_qn, w_qp = _split_heads(w_wq, MLA_NOPE)
    w_k, w_v = _split_heads(w_wkv, MLA_NOPE)

    ab_sh = lax.dynamic_slice_in_dim(ada_b, chip * Wc, Wc, axis=1)[:, None, :]
    mod_sh = ada_fwd(c_all, ada_w, ab_sh, "ada_fwd")
    mod_all = _unshard_cols(gather_chips(mod_sh, "gather_mod"))
    mod_mine = jnp.concatenate([lax.dynamic_slice_in_dim(mod_all, me * B, B, axis=1), mod_all[:, 8 * B:8 * B + 1]], axis=1)
    mod = mod_mine.reshape(DEPTH, B + 1, 6, D)

    tabc, tabs = _rope_tables(SEQ)
    bc8 = lambda a: jnp.broadcast_to(a.reshape(a.shape + (1, 1)), a.shape + (8, 128))
    row = lambda a: a.reshape(1, -1)

    S = jnp.concatenate([x.reshape(NX, D), ctx.reshape(B * CTX, D)], axis=0)
    saved = []
    xn = norm_mod_fwd(S, row(norm_mix[0]), mod[0], 0, lay, nt, "norm_mix_fwd0")
    for l in range(DEPTH):
        j, even = l // 2, l % 2 == 0
        if even:
            z = mm(xn, w_ab_in[j], name=f"ab_in{l}")
            q, k, v = ab_prep_fwd(z, tabc, tabs, row(mla_kv_norm[j]), row(mla_q_norm[j]), w_k[j], w_v[j], w_qn[j], w_qp[j],
                                  lay, f"ab_prep{l}")
            o = mla_fwd(q, k, v, lay, f"mla{l}")
            merged = cmlp_merge_fwd(z, o, row(cmlp_v_norm[j]), cmlp_ws[j], cmlp_bs[j][:, :, None], lay, f"cmlp{l}")
            w_out, mix = w_ab_out[j], (q, k, v)
        else:
            z = mm(xn, w_cd_in[j], name=f"cd_in{l}")
            rq, rk, sq, ke, ve = cd_prep_fwd(z, tabc, tabs, lay, f"cd_prep{l}")
            decf, decb, sink = bc8(ret_decay_fwd[j]), bc8(ret_decay_bwd[j]), bc8(swa_sink[j].reshape(SWA_KV_HEADS, SWA_GROUPS))
            yret = ret_fwd(rq, rk, z, decf, decb, lay, f"ret{l}")
            osw = swa_fwd(sq, ke, ve, sink, lay, f"swa{l}")
            merged = cd_merge_fwd(yret, z, osw, row(ret_norm_full[j]), lay, f"cd_merge{l}")
            w_out, mix = w_cd_out[j], (rq, rk, sq, ke, ve, decf, decb, sink, yret)
        y, S_mid, xn2 = mm_gated(merged, w_out, S, mod[l], 2, lay, name=f"mix_out{l}", n_tiles=nt,
                                 norm=(row(norm_ffn[l]), mod[l], 3))
        act, fa, fb = ffn_in_act(xn2, w_ffn_in[l], f"ffn_in{l}")
        nxt_norm = (row(norm_mix[l + 1]), mod[l + 1], 0) if l + 1 < DEPTH else None
        f, S_new, *xn_next = mm_gated(act, w_ffn_out[l], S_mid, mod[l], 5, lay, name=f"ffn_out{l}", n_tiles=nt, norm=nxt_norm)
        saved.append((S, xn, z, mix, merged, w_out, y, S_mid, xn2, (fa, fb), act, f))
        S, xn = S_new, (xn_next[0] if xn_next else None)

    loss_blk, dS, d_norm_final = loss_head(S, row(norm_final), loss_target.reshape(NX, D), lay, "loss_head")
    loss = lax.psum(loss_blk[0, 0], ("x", "y", "c"))

    G = {n: [None] * W[n].shape[0] for n in ("norm_mix", "norm_ffn", "mla_q_norm", "mla_kv_norm", "cmlp_v_norm", "cmlp_ws",
                                             "cmlp_bs", "ret_decay_fwd", "ret_decay_bwd", "ret_norm", "swa_sink")}
    GB = {n: [None] * cnt for n, cnt in (("ffn_in_a", DEPTH), ("ffn_in_b", DEPTH), ("ffn_out", DEPTH), ("ab_in", n_even), ("ab_out", n_even),
                                         ("wqn", n_even), ("wqp", n_even), ("wk", n_even), ("wv", n_even), ("cd_in", n_odd),
                                         ("cd_out", n_odd))}
    dmod = [None] * DEPTH
    df, dgate2 = gate_bwd(dS, saved[-1][-1], mod[DEPTH - 1], 5, lay, nt, f"ffn_gate_bwd{DEPTH - 1}")
    for l in reversed(range(DEPTH)):
        j, even = l // 2, l % 2 == 0
        S_in, xn, z, mix, merged, w_out, y, S_mid, xn2, (fa, fb), act, f = saved[l]
        da, db = ffn_out_dx_act(df, w_ffn_out_t[l], fa, fb, f"ffn_out_dx{l}")
        GB["ffn_out"][l] = mm(act, df, ta=True, name=f"ffn_out_dw{l}")
        GB["ffn_in_a"][l] = mm(xn2, da, ta=True, name=f"ffn_in_dwa{l}")
        GB["ffn_in_b"][l] = mm(xn2, db, ta=True, name=f"ffn_in_dwb{l}")
        dxn2 = ffn_in_dx(da, db, w_ffn_in[l], f"ffn_in_dx{l}")
        dS_mid, dss2, dg, dy, dgate1 = norm_mod_bwd(S_mid, row(norm_ffn[l]), mod[l], 3, dxn2, dS, lay, nt, f"norm_ffn_bwd{l}",
                                                    gate=(y, mod[l], 2))
        G["norm_ffn"][l] = dg
        dmerged = mm(dy, w_out, tb=True, name=f"mix_out_dx{l}")
        d_w_out = mm(merged, dy, ta=True, name=f"mix_out_dw{l}")
        if even:
            q, k, v = mix
            dq, dkx, dkh, dvx, dvh = mla_bwd(q, k, v, dmerged, lay, f"mla_bwd{l}")
            (dz, dgkv, dgq, dwk, dwv, dwqn, dwqp, dgvn, dws, dbs) = ab_rows_bwd(
                z, tabc, tabs, dq, dkx, dkh, dvx, dvh, dmerged, row(mla_kv_norm[j]), row(mla_q_norm[j]), w_k[j], w_v[j],
                w_qn[j], w_qp[j], row(cmlp_v_norm[j]), cmlp_ws[j], cmlp_bs[j][:, :, None], lay, f"ab_rows_bwd{l}")
            G["mla_kv_norm"][j], G["mla_q_norm"][j], G["cmlp_v_norm"][j] = dgkv, dgq, dgvn
            G["cmlp_ws"][j], G["cmlp_bs"][j] = dws, dbs
            GB["wk"][j], GB["wv"][j], GB["wqn"][j], GB["wqp"][j], GB["ab_out"][j] = dwk, dwv, dwqn, dwqp, d_w_out
            w_in = w_ab_in[j]
        else:
            rq, rk, sq, ke, ve, decf, decb, sink, yret = mix
            dyret, drg, dgn = cd_merge_bwd(yret, z, row(ret_norm_full[j]), dmerged, lay, f"cd_merge_bwd{l}")
            dqx, dqh, dkx, dkh, dvx, dvh, ddf, ddb = ret_bwd(rq, rk, z, decf, decb, dyret, lay, f"ret_bwd{l}")
            dsq, dkex, dkeh, dvex, dveh, dsink = swa_bwd(sq, ke, ve, sink, dmerged, lay, f"swa_bwd{l}")
            dz = cd_rows_bwd(z, tabc, tabs, dsq, drg, (dqx, dqh), (dkx, dkh), (dkex, dkeh), (dvex, dveh), (dvx, dvh), lay,
                             f"cd_rows_bwd{l}")
            G["ret_norm"][j], G["ret_decay_fwd"][j], G["ret_decay_bwd"][j] = dgn, ddf[:, 0, 0], ddb[:, 0, 0]
            G["swa_sink"][j] = dsink[:, :, 0, 0].reshape(-1)
            GB["cd_out"][j] = d_w_out
            w_in = w_cd_in[j]
        GB["ab_in" if even else "cd_in"][j] = mm(xn, dz, ta=True, name=f"mix_in_dw{l}")
        dxn = mm(dz, w_in, tb=True, name=f"mix_in_dx{l}")
        dmod_l = lambda dss1: jnp.concatenate([dss1, dgate1, dss2, dgate2], axis=1)
        if l > 0:
            dS, dss1, dg, df, dgate2_prev = norm_mod_bwd(S_in, row(norm_mix[l]), mod[l], 0, dxn, dS_mid, lay, nt,
                                                         f"norm_mix_bwd{l}", gate=(saved[l - 1][-1], mod[l - 1], 5))
            dmod[l], dgate2 = dmod_l(dss1), dgate2_prev
        else:
            dS, dss1, dg = norm_mod_bwd(S_in, row(norm_mix[l]), mod[l], 0, dxn, dS_mid, lay, nt, f"norm_mix_bwd{l}")
            dmod[l] = dmod_l(dss1)
        G["norm_mix"][l] = dg
    grad_x = dS[:NX].reshape(B, SEQ, D)

    st = lambda n: jnp.stack([g.reshape((4 * rn_sh,) if n == "ret_norm" else W[n].shape[1:]) for g in G[n]])
    small_parts = {n: st(n) for n in G}
    small_parts["norm_final"] = d_norm_final.reshape(-1)
    dmod_local = jnp.stack(dmod).reshape(DEPTH, B + 1, 6 * D)
    names1 = ["norm_mix", "norm_ffn", "norm_final", "mla_q_norm", "mla_kv_norm", "cmlp_v_norm", "cmlp_ws", "cmlp_bs",
              "ret_decay_fwd", "ret_decay_bwd", "ret_norm", "swa_sink"]
    like1 = [dmod_local] + [small_parts[n] for n in names1]
    g1 = gather8(_pack(like1), "gather_small_grads")
    tot1 = _unpack(sum8(g1, "sum_small_grads"), like1)
    sg = dict(zip(names1, tot1[1:]))
    n_dm = math.prod(dmod_local.shape)
    dm_each = g1.reshape(8, -1)[:, :n_dm].reshape(8, DEPTH, B + 1, 6 * D)
    dmod_all = jnp.concatenate([jnp.transpose(dm_each[:, :, :B], (1, 0, 2, 3)).reshape(DEPTH, 8 * B, 6 * D),
                                tot1[0][:, B:B + 1], jnp.zeros((DEPTH, NC - 8 * B - 1, 6 * D), F32)], axis=1)
    dmod_sh = lax.dynamic_slice_in_dim(dmod_all, chip * Wc, Wc, axis=2)
    g_ada_w, g_ada_b_sh, dcs = ada_bwd(c_all, ada_w, dmod_sh, "ada_bwd")
    like2 = [dcs[8 * B], g_ada_b_sh]
    g2 = gather8(_pack(like2), "gather_ada_grads")
    tot2 = _unpack(sum8(g2, "sum_ada_grads"), like2)
    bc = lambda a: jnp.broadcast_to(a.reshape(1, D), (8, D))
    sg["c_ctx"] = cctx_grad(bc(tot2[0]), bc(c_ctx), "c_ctx_grad")[0]
    off = D + (-D) % 1024
    gab = g2.reshape(8, -1)[0::2, off:off + DEPTH * Wc].reshape(4, DEPTH, Wc)
    sg["ada_b"] = jnp.transpose(gab, (1, 0, 2)).reshape(DEPTH, 4 * Wc)
    sg["ret_norm"] = lax.dynamic_slice_in_dim(sg["ret_norm"], chip * rn_sh, rn_sh, axis=1)

    stack = lambda n: jnp.stack(GB[n])
    gs = {"ffn_in": jnp.concatenate([_shard_cols(stack("ffn_in_a"), 2), _shard_cols(stack("ffn_in_b"), 2)], axis=0),
          "ffn_out": _shard_rows(stack("ffn_out")),
          "ab_in": _shard_cols(_ab_in_unpermute(stack("ab_in"))), "ab_out": _shard_rows(stack("ab_out")),
          "mla_wq_b": _shard_cols(_join_heads(stack("wqn"), stack("wqp"))),
          "mla_wkv_b": _shard_cols(_join_heads(stack("wk"), stack("wv"))),
          "cd_in": _shard_cols(stack("cd_in")), "cd_out": _shard_rows(stack("cd_out"))}
    bufs = swap_other_half([gs[n] for n in _BIG], "swap_core_halves")
    parts = [chip_partial(gs[n], b, cidx, f"chip_partial_{n}") for n, b in zip(_BIG, bufs)]
    arrived = exchange_chips(parts, "exchange_chips")
    halves = []
    for n, p in zip(_BIG, arrived):
        halves += adam_sharded(p, W[n], M1[n], M2[n], cidx, f"adam_{n}")
    full = share_halves(halves, "share_core_halves")
    out = {n: tuple(full[4 * i:4 * i + 4]) for i, n in enumerate(_BIG)}

    like_s = [W[n] for n in _SMALL]
    dsm, msm, vsm = adam_rows(_pack(like_s), _pack([sg[n].reshape(W[n].shape) for n in _SMALL]), _pack([M1[n] for n in _SMALL]),
                                _pack([M2[n] for n in _SMALL]), "adam_small")
    for n, d_, m_, v_ in zip(_SMALL, _unpack(dsm, like_s), _unpack(msm, like_s), _unpack(vsm, like_s)):
        out[n] = (sg[n].reshape(W[n].shape), d_, m_, v_)
    flat2 = lambda a: a.reshape(-1, a.shape[-1])
    d_, m_, v_ = adam_rows(flat2(ada_w), flat2(g_ada_w), flat2(m_ada_w), flat2(v_ada_w), "adam_ada_w")
    out["ada_w"] = (g_ada_w, d_.reshape(ada_w.shape), m_.reshape(ada_w.shape), v_.reshape(ada_w.shape))

    return (loss, grad_x, *[out[n][0] for n in _WEIGHTS], *[out[n][1] for n in _WEIGHTS], *[out[n][2] for n in _WEIGHTS],
            *[out[n][3] for n in _WEIGHTS])
```

```python
import functools
import math

import jax
import jax.numpy as jnp
import numpy as np
from jax import lax
from jax.experimental import pallas as pl
from jax.experimental.pallas import tpu as pltpu

F32 = jnp.float32
BF16 = jnp.bfloat16
EPS = 1e-6
NEG_INF = -1e30
GRID_W = 64
ROPE_THETA = 10000.0
DEPTH = 4
MLA_HEADS, MLA_Q_LORA, MLA_KV_LORA, MLA_NOPE, MLA_ROPE, MLA_V = 4, 256, 256, 128, 64, 128
CMLP_GROUPS, CMLP_CHUNK = 4, 128
CMLP_WIDTH = 512
RET_HEADS, RET_QK, RET_V = 4, 64, 128
SWA_Q_HEADS, SWA_KV_HEADS, SWA_HEAD_DIM, SWA_WINDOW = 8, 2, 64, 128
SWA_GROUPS = SWA_Q_HEADS // SWA_KV_HEADS
AB_IN_P = 1664
ADAM_LR, ADAM_B1, ADAM_B2, ADAM_EPS, ADAM_WD, ADAM_STEP = 0.001, 0.9, 0.999, 1e-08, 0.01, 10

T = 256
SWA_SPAN = T + 2 * SWA_WINDOW
VMEM_LIMIT = 48 * 1024 * 1024
MESH = pl.DeviceIdType.MESH
ANY = pl.BlockSpec(memory_space=pl.ANY)


def _cparams(sem):
    return pltpu.CompilerParams(dimension_semantics=sem, vmem_limit_bytes=VMEM_LIMIT)


@functools.cache
def _bdot_fn(ca, cb):
    fa, fb = 1 - ca, 1 - cb

    def dg(p, q, cp, cq):
        return lax.dot_general(p.astype(BF16), q.astype(BF16), (((cp,), (cq,)), ((), ())), preferred_element_type=F32)

    @jax.custom_vjp
    def bd(a, b):
        return dg(a, b, ca, cb)

    def fwd(a, b):
        return dg(a, b, ca, cb), (a, b)

    def bwd(res, g):
        a, b = res
        da = dg(g, b, 1, fb) if ca == 1 else dg(b, g, fb, 1)
        db = dg(a, g, fa, 0) if cb == 0 else dg(g, a, 0, fa)
        return da, db

    bd.defvjp(fwd, bwd)
    return bd


def bdot(a, b):
    return _bdot_fn(1, 0)(a, b)


def bdot_nt(a, b):
    return _bdot_fn(1, 1)(a, b)


def bdot_tn(a, b):
    return _bdot_fn(0, 0)(a, b)


def _swap32(x):
    w = x.shape[-1]
    lane = lax.broadcasted_iota(jnp.int32, x.shape, 1)
    return jnp.where((lane & 32) == 0, pltpu.roll(x, w - 32, 1), pltpu.roll(x, 32, 1))


@jax.custom_vjp
def rope(x, c, s):
    return x * c + _swap32(x) * s


def _rope_fwd(x, c, s):
    return rope(x, c, s), (c, s)


def _rope_bwd(res, g):
    c, s = res
    return g * c + _swap32(g * s), jnp.zeros_like(c), jnp.zeros_like(s)


rope.defvjp(_rope_fwd, _rope_bwd)


def rms(x, g):
    return x * lax.rsqrt(jnp.mean(x * x, axis=-1, keepdims=True) + EPS) * g


def normmod(x, g, sh, sc):
    return rms(x, g) * (1.0 + sc) + sh


def log_sigmoid(x):
    return jnp.minimum(x, 0.0) - jnp.log(1.0 + jnp.exp(-jnp.abs(x)))


def _head_mask(shape, h):
    lane = lax.broadcasted_iota(jnp.int32, shape, 1)
    return ((lane >> 6) == h).astype(F32)


def _fold_matrix():
    i = lax.broadcasted_iota(jnp.int32, (256, 128), 0)
    j = lax.broadcasted_iota(jnp.int32, (256, 128), 1)
    return ((i & 63) == j).astype(F32)


def _expand_matrix(g):
    i = lax.broadcasted_iota(jnp.int32, (128, 256), 0)
    j = lax.broadcasted_iota(jnp.int32, (128, 256), 1)
    return (i == (j & 63) + 64 * g).astype(F32)


def _acc(ref, val, first):
    @pl.when(first)
    def _():
        ref[...] = val

    @pl.when(jnp.logical_not(first))
    def _():
        ref[...] += val


def _pick(n, cap, mult):
    best = None
    for d in range(mult, min(n, cap) + 1, mult):
        if n % d == 0:
            best = d
    return best if best is not None else n


def mm(a, b, *, ta=False, tb=False, name, split=1):
    M, K = (a.shape[1], a.shape[0]) if ta else a.shape
    N = b.shape[0] if tb else b.shape[1]
    Ns = N // split
    assert split == 1 or (ta and Ns % 128 == 0)
    if ta:
        tn = N
        tm = _pick(M, min(1536, (3 << 20) // tn), 128)
    else:
        tn = _pick(N, 768, 128)
        if tn < 256 and N <= 2304:
            tn = N
        tm = _pick(M, min(1536, (1 << 20) // tn), 128)
    tk = _pick(K, 2048 if not ta else 1024, 128) if K > 2816 or ta else K
    nk = K // tk
    grid = (M // tm, N // tn, nk)
    a_spec = pl.BlockSpec((tk, tm), lambda i, j, k: (k, i)) if ta else pl.BlockSpec((tm, tk), lambda i, j, k: (i, k))
    b_spec = pl.BlockSpec((tn, tk), lambda i, j, k: (j, k)) if tb else pl.BlockSpec((tk, tn), lambda i, j, k: (k, j))
    dims = (((0 if ta else 1,), (1 if tb else 0,)), ((), ()))

    def body(a_ref, b_ref, o_ref):
        part = lax.dot_general(a_ref[...], b_ref[...], dims, preferred_element_type=F32)
        first = pl.program_id(2) == 0
        if split > 1:
            for s in range(split):
                _acc(o_ref.at[s], part[:, s * Ns:(s + 1) * Ns], first)
        elif nk == 1:
            o_ref[...] = part
        else:
            _acc(o_ref, part, first)

    if split > 1:
        out_spec, out_shape = pl.BlockSpec((split, tm, Ns), lambda i, j, k: (0, i, 0)), (split, M, Ns)
    else:
        out_spec, out_shape = pl.BlockSpec((tm, tn), lambda i, j, k: (i, j)), (M, N)
    return pl.pallas_call(
        body, grid=grid, in_specs=[a_spec, b_spec], out_specs=out_spec, out_shape=jax.ShapeDtypeStruct(out_shape, F32),
        compiler_params=_cparams(("parallel", "parallel", "arbitrary")), name=name)(a, b)


def mm_gated(a, b, res, mod, gate_row, lay, *, name, n_tiles, norm=None):
    K, N = b.shape
    M = n_tiles * T

    def body(*refs):
        a_ref, b_ref, r_ref, g_ref = refs[:4]
        y = lax.dot_general(a_ref[...], b_ref[...], (((1,), (0,)), ((), ())), preferred_element_type=F32)
        new = r_ref[...] + g_ref[gate_row:gate_row + 1, :] * y
        if norm is None:
            y_ref, o_ref = refs[4:]
        else:
            gn_ref, mn_ref, y_ref, o_ref, xn_ref = refs[4:]
            k0 = norm[2]
            xn_ref[...] = normmod(new, gn_ref[...], mn_ref[k0:k0 + 1, :], mn_ref[k0 + 1:k0 + 2, :]).astype(BF16)
        y_ref[...] = y.astype(BF16)
        o_ref[...] = new

    rows = pl.BlockSpec((T, N), lambda i: (i, 0))
    extra = [] if norm is None else [_full((1, N)), _modspec(lay, 6, N)]
    return pl.pallas_call(
        body, grid=(n_tiles,),
        in_specs=[pl.BlockSpec((T, K), lambda i: (i, 0)), _full((K, N)), rows, _modspec(lay, 6, N)] + extra,
        out_specs=[rows, rows] + ([] if norm is None else [rows]),
        out_shape=[jax.ShapeDtypeStruct((M, N), BF16), jax.ShapeDtypeStruct((M, N), F32)]
        + ([] if norm is None else [jax.ShapeDtypeStruct((M, N), BF16)]),
        compiler_params=_cparams(("parallel",)), name=name)(a, b, res, mod, *([] if norm is None else norm[:2]))


class Layout:
    def __init__(self, B, SEQ, CTX, D):
        assert CTX == T and SEQ % T == 0 and SEQ >= SWA_SPAN
        self.B, self.SEQ, self.CTX, self.D = B, SEQ, CTX, D
        self.tps = SEQ // T
        self.nxt = B * self.tps
        self.nt = self.nxt + B
        self.NX, self.R = B * SEQ, B * SEQ + B * CTX
        self.nq = self.tps + 1

    def mod_idx(self, i):
        return jnp.where(i < self.nxt, i // self.tps, self.B)

    def rope_idx(self, i):
        return jnp.where(i < self.nxt, i % self.tps, self.tps)

    def first_of_mod(self, i):
        return jnp.logical_or(jnp.logical_and(i < self.nxt, i % self.tps == 0), i == self.nxt)

    def qrow(self, b, qi):
        return jnp.where(qi < self.tps, b * self.tps + qi, self.nxt + b)


def _row(w, col=0):
    return pl.BlockSpec((T, w), lambda i: (i, col))


def _full(shape):
    nd = len(shape)
    return pl.BlockSpec(shape, lambda i: (0,) * nd)


def _modspec(lay, rows, D):
    return pl.BlockSpec((None, rows, D), lambda i: (lay.mod_idx(i), 0, 0))


def _ropespec(lay, w):
    return pl.BlockSpec((T, w), lambda i: (lay.rope_idx(i), 0))


def _xh_specs(lay, w):
    nxt = lay.nxt
    return [pl.BlockSpec((T, w), lambda i: (jnp.minimum(i, nxt - 1), 0)), pl.BlockSpec((T, w), lambda i: (jnp.maximum(i - nxt, 0), 0))]


def _xh_pick(lay, x_ref, h_ref):
    return jnp.where(pl.program_id(0) < lay.nxt, x_ref[...], h_ref[...])


def norm_mod_fwd(S, g, mod, k0, lay, n_tiles, name):
    D = S.shape[1]

    def body(s_ref, g_ref, mod_ref, o_ref):
        o_ref[...] = normmod(s_ref[...], g_ref[...], mod_ref[k0:k0 + 1, :], mod_ref[k0 + 1:k0 + 2, :]).astype(BF16)

    return pl.pallas_call(
        body, grid=(n_tiles,), in_specs=[_row(D), _full((1, D)), _modspec(lay, 6, D)], out_specs=_row(D),
        out_shape=jax.ShapeDtypeStruct((n_tiles * T, D), BF16), compiler_params=_cparams(("parallel",)), name=name)(S, g, mod)


def norm_mod_bwd(S, g, mod, k0, dxn, ds_in, lay, n_tiles, name, gate=None):
    D = S.shape[1]

    def body(*refs):
        s_ref, g_ref, mod_ref, dxn_ref, dsin_ref = refs[:5]
        i = pl.program_id(0)
        _, vjp = jax.vjp(normmod, s_ref[...], g_ref[...], mod_ref[k0:k0 + 1, :], mod_ref[k0 + 1:k0 + 2, :])
        dx, dg, dsh, dsc = vjp(dxn_ref[...])
        ds = dsin_ref[...] + dx
        if gate is None:
            ds_ref, dss_ref, dg_ref = refs[5:]
        else:
            y_ref, gmod_ref, ds_ref, dss_ref, dg_ref, dy_ref, dgate_ref = refs[5:]
            row = gate[2]
            dy_ref[...] = (gmod_ref[row:row + 1, :] * ds).astype(BF16)
            _acc(dgate_ref, jnp.sum(ds * y_ref[...], axis=0, keepdims=True), lay.first_of_mod(i))
        ds_ref[...] = ds
        _acc(dg_ref, dg, i == 0)
        _acc(dss_ref, jnp.concatenate([dsh, dsc], axis=0), lay.first_of_mod(i))

    R_ = n_tiles * T
    gated = gate is not None
    return pl.pallas_call(
        body, grid=(n_tiles,),
        in_specs=[_row(D), _full((1, D)), _modspec(lay, 6, D), _row(D), _row(D)] + ([_row(D), _modspec(lay, 6, D)] if gated else []),
        out_specs=[_row(D), _modspec(lay, 2, D), _full((1, D))] + ([_row(D), _modspec(lay, 1, D)] if gated else []),
        out_shape=[jax.ShapeDtypeStruct((R_, D), F32), jax.ShapeDtypeStruct((lay.B + 1, 2, D), F32),
                   jax.ShapeDtypeStruct((1, D), F32)]
        + ([jax.ShapeDtypeStruct((R_, D), BF16), jax.ShapeDtypeStruct((lay.B + 1, 1, D), F32)] if gated else []),
        compiler_params=_cparams(("arbitrary",)), name=name)(S, g, mod, dxn, ds_in, *(gate[:2] if gated else []))


def gate_bwd(dS, y, mod, gate_row, lay, n_tiles, name):
    D = dS.shape[1]

    def body(ds_ref, y_ref, mod_ref, dy_ref, dgate_ref):
        i = pl.program_id(0)
        ds = ds_ref[...]
        dy_ref[...] = (mod_ref[gate_row:gate_row + 1, :] * ds).astype(BF16)
        _acc(dgate_ref, jnp.sum(ds * y_ref[...], axis=0, keepdims=True), lay.first_of_mod(i))

    return pl.pallas_call(
        body, grid=(n_tiles,), in_specs=[_row(D), _row(D), _modspec(lay, 6, D)],
        out_specs=[_row(D), _modspec(lay, 1, D)],
        out_shape=[jax.ShapeDtypeStruct((n_tiles * T, D), BF16), jax.ShapeDtypeStruct((lay.B + 1, 1, D), F32)],
        compiler_params=_cparams(("arbitrary",)), name=name)(dS, y, mod)


def _swiglu(a, b):
    return a * jax.nn.sigmoid(a) * b


def _ffn_tiles(M, F):
    tn = _pick(F, 1408, 128)
    return _pick(M, (3 << 18) // tn, 128), tn


def ffn_in_act(x, w, name):
    M, D = x.shape
    F = w.shape[1] // 2
    tm, tn = _ffn_tiles(M, F)
    nj = F // tn

    def body(x_ref, wa_ref, wb_ref, act_ref, a_ref, b_ref):
        dims = (((1,), (0,)), ((), ()))
        a = lax.dot_general(x_ref[...], wa_ref[...], dims, preferred_element_type=F32)
        b = lax.dot_general(x_ref[...], wb_ref[...], dims, preferred_element_type=F32)
        act_ref[...] = _swiglu(a, b).astype(BF16)
        a_ref[...] = a.astype(BF16)
        b_ref[...] = b.astype(BF16)

    out = pl.BlockSpec((tm, tn), lambda i, j: (i, j))
    return pl.pallas_call(
        body, grid=(M // tm, nj),
        in_specs=[pl.BlockSpec((tm, D), lambda i, j: (i, 0)), pl.BlockSpec((D, tn), lambda i, j: (0, j)),
                  pl.BlockSpec((D, tn), lambda i, j: (0, j + nj))],
        out_specs=[out, out, out], out_shape=[jax.ShapeDtypeStruct((M, F), BF16)] * 3,
        compiler_params=_cparams(("parallel", "parallel")), name=name)(x, w, w)


def ffn_out_dx_act(df, wt, a, b, name):
    M, D = df.shape
    F = wt.shape[1]
    tm, tn = _ffn_tiles(M, F)

    def body(df_ref, w_ref, a_ref, b_ref, da_ref, db_ref):
        dact = lax.dot_general(df_ref[...], w_ref[...], (((1,), (0,)), ((), ())), preferred_element_type=F32)
        _, vjp = jax.vjp(_swiglu, a_ref[...].astype(F32), b_ref[...].astype(F32))
        da, db = vjp(dact)
        da_ref[...] = da.astype(BF16)
        db_ref[...] = db.astype(BF16)

    blk = pl.BlockSpec((tm, tn), lambda i, j: (i, j))
    return pl.pallas_call(
        body, grid=(M // tm, F // tn),
        in_specs=[pl.BlockSpec((tm, D), lambda i, j: (i, 0)), pl.BlockSpec((D, tn), lambda i, j: (0, j)), blk, blk],
        out_specs=[blk, blk], out_shape=[jax.ShapeDtypeStruct((M, F), BF16)] * 2,
        compiler_params=_cparams(("parallel", "parallel")), name=name)(df, wt, a, b)


def ffn_in_dx(da, db, w, name):
    M, F = da.shape
    D = w.shape[0]
    tm, tn = _pick(M, 512, 128), _pick(D, 512, 128)

    def body(da_ref, db_ref, wa_ref, wb_ref, o_ref):
        dims = (((1,), (1,)), ((), ()))
        o_ref[...] = (lax.dot_general(da_ref[...], wa_ref[...], dims, preferred_element_type=F32)
                      + lax.dot_general(db_ref[...], wb_ref[...], dims, preferred_element_type=F32))

    return pl.pallas_call(
        body, grid=(M // tm, D // tn),
        in_specs=[pl.BlockSpec((tm, F), lambda i, j: (i, 0)), pl.BlockSpec((tm, F), lambda i, j: (i, 0)),
                  pl.BlockSpec((tn, F), lambda i, j: (j, 0)), pl.BlockSpec((tn, F), lambda i, j: (j, 1))],
        out_specs=pl.BlockSpec((tm, tn), lambda i, j: (i, j)), out_shape=jax.ShapeDtypeStruct((M, D), F32),
        compiler_params=_cparams(("parallel", "parallel")), name=name)(da, db, w, w)


def loss_head(S, g, target, lay, name):
    D = S.shape[1]
    nxt = lay.nxt

    def tile_loss(x, gg, t):
        err = rms(x, gg) - t
        return 0.5 * jnp.sum(jnp.mean(err * err, axis=-1))

    def body(s_ref, g_ref, t_ref, loss_ref, ds_ref, dg_ref):
        i = pl.program_id(0)

        @pl.when(i < nxt)
        def _():
            val, vjp = jax.vjp(tile_loss, s_ref[...], g_ref[...], t_ref[...])
            dx, dg, _ = vjp(jnp.ones((), F32))
            ds_ref[...] = dx
            _acc(dg_ref, dg, i == 0)
            _acc(loss_ref, jnp.full((8, 128), val, F32), i == 0)

        @pl.when(i >= nxt)
        def _():
            ds_ref[...] = jnp.zeros((T, D), F32)

    return pl.pallas_call(
        body, grid=(lay.nt,),
        in_specs=[_row(D), _full((1, D)), pl.BlockSpec((T, D), lambda i: (jnp.minimum(i, nxt - 1), 0))],
        out_specs=[_full((8, 128)), _row(D), _full((1, D))],
        out_shape=[jax.ShapeDtypeStruct((8, 128), F32), jax.ShapeDtypeStruct((lay.R, D), F32),
                   jax.ShapeDtypeStruct((1, D), F32)],
        compiler_params=_cparams(("arbitrary",)), name=name)(S, g, target)


def _ab_prep(zkv, zq, zpe, c256, s256, c128, s128, gkv, gq, wk, wv, wqn, wqp):
    kvn = rms(zkv, gkv)
    kn, v = bdot(kvn, wk), bdot(kvn, wv)
    qn = rms(zq, gq)
    qnope, qpe = bdot(qn, wqn), rope(bdot(qn, wqp), c256, s256)
    kpe = rope(zpe, c128, s128)
    fold = _fold_matrix()
    qparts, kparts = [], []
    for h in range(MLA_HEADS):
        qparts += [qnope[:, 128 * h:128 * (h + 1)], bdot(qpe * _head_mask(qpe.shape, h), fold)]
        kparts += [kn[:, 128 * h:128 * (h + 1)], kpe]
    return jnp.concatenate(qparts, axis=1), jnp.concatenate(kparts, axis=1), v


def _ab_prep_specs(lay):
    return [_row(256, 0), _row(256, 1), _row(128, 12), _ropespec(lay, 256), _ropespec(lay, 256), _ropespec(lay, 128),
            _ropespec(lay, 128), _full((1, 256)), _full((1, 256)), _full((256, 512)), _full((256, 512)),
            _full((256, 512)), _full((256, 256))]


def ab_prep_fwd(z, tabc, tabs, gkv, gq, wk, wv, wqn, wqp, lay, name):
    def body(*refs):
        ins, (q_ref, k_ref, v_ref) = refs[:13], refs[13:]
        q, k, v = _ab_prep(*[r[...].astype(F32) for r in ins])
        q_ref[...] = q.astype(BF16)
        k_ref[...] = k.astype(BF16)
        v_ref[...] = v.astype(BF16)

    R = lay.R
    return pl.pallas_call(
        body, grid=(lay.nt,), in_specs=_ab_prep_specs(lay), out_specs=[_row(1024), _row(1024), _row(512)],
        out_shape=[jax.ShapeDtypeStruct((R, 1024), BF16), jax.ShapeDtypeStruct((R, 1024), BF16),
                   jax.ShapeDtypeStruct((R, 512), BF16)],
        compiler_params=_cparams(("parallel",)), name=name)(z, z, z, tabc, tabs, tabc, tabs, gkv, gq, wk, wv, wqn, wqp)


_MLA_SCALE = (MLA_NOPE + MLA_ROPE) ** -0.5


def _mla_x(q, kx, vx, kh, vh):
    sx, sh = bdot_nt(q, kx) * _MLA_SCALE, bdot_nt(q, kh) * _MLA_SCALE
    m = lax.stop_gradient(jnp.maximum(jnp.max(sx, axis=-1, keepdims=True), jnp.max(sh, axis=-1, keepdims=True)))
    ex, eh = jnp.exp(sx - m), jnp.exp(sh - m)
    inv = 1.0 / (jnp.sum(ex, axis=-1, keepdims=True) + jnp.sum(eh, axis=-1, keepdims=True))
    return bdot(ex * inv, vx) + bdot(eh * inv, vh)


def _mla_h(q, kh, vh):
    sh = bdot_nt(q, kh) * _MLA_SCALE
    eh = jnp.exp(sh - lax.stop_gradient(jnp.max(sh, axis=-1, keepdims=True)))
    return bdot(eh * (1.0 / jnp.sum(eh, axis=-1, keepdims=True)), vh)


def _mla_specs(lay):
    nxt, SEQ = lay.nxt, lay.SEQ
    return [pl.BlockSpec((T, 256), lambda b, h, qi: (lay.qrow(b, qi), h)),
            pl.BlockSpec((SEQ, 256), lambda b, h, qi: (b, h)), pl.BlockSpec((SEQ, 128), lambda b, h, qi: (b, h)),
            pl.BlockSpec((T, 256), lambda b, h, qi: (nxt + b, h)), pl.BlockSpec((T, 128), lambda b, h, qi: (nxt + b, h))]


def mla_fwd(q, k, v, lay, name):
    tps = lay.tps

    def body(q_ref, kx_ref, vx_ref, kh_ref, vh_ref, o_ref):
        qi = pl.program_id(2)
        f = lambda r: r[...]

        @pl.when(qi < tps)
        def _():
            o_ref[...] = _mla_x(f(q_ref), f(kx_ref), f(vx_ref), f(kh_ref), f(vh_ref)).astype(BF16)

        @pl.when(qi == tps)
        def _():
            o_ref[...] = _mla_h(f(q_ref), f(kh_ref), f(vh_ref)).astype(BF16)

    return pl.pallas_call(
        body, grid=(lay.B, MLA_HEADS, lay.nq), in_specs=_mla_specs(lay),
        out_specs=pl.BlockSpec((T, 128), lambda b, h, qi: (lay.qrow(b, qi), h)),
        out_shape=jax.ShapeDtypeStruct((lay.R, 512), BF16),
        compiler_params=_cparams(("parallel", "parallel", "arbitrary")), name=name)(q, k, v, k, v)


def mla_bwd(q, k, v, dmerged, lay, name):
    tps, SEQ, B = lay.tps, lay.SEQ, lay.B

    def body(q_ref, kx_ref, vx_ref, kh_ref, vh_ref, do_ref, dq_ref, dkx_ref, dkh_ref, dvx_ref, dvh_ref):
        qi = pl.program_id(2)
        f = lambda r: r[...].astype(F32)

        @pl.when(qi < tps)
        def _():
            _, vjp = jax.vjp(_mla_x, f(q_ref), f(kx_ref), f(vx_ref), f(kh_ref), f(vh_ref))
            dq, dkx, dvx, dkh, dvh = vjp(do_ref[...])
            dq_ref[...] = dq
            _acc(dkx_ref, dkx, qi == 0)
            _acc(dvx_ref, dvx, qi == 0)
            _acc(dkh_ref, dkh, qi == 0)
            _acc(dvh_ref, dvh, qi == 0)

        @pl.when(qi == tps)
        def _():
            _, vjp = jax.vjp(_mla_h, f(q_ref), f(kh_ref), f(vh_ref))
            dq, dkh, dvh = vjp(do_ref[...])
            dq_ref[...] = dq
            dkh_ref[...] += dkh
            dvh_ref[...] += dvh

    return pl.pallas_call(
        body, grid=(B, MLA_HEADS, lay.nq),
        in_specs=_mla_specs(lay) + [pl.BlockSpec((T, 128), lambda b, h, qi: (lay.qrow(b, qi), h))],
        out_specs=[pl.BlockSpec((T, 256), lambda b, h, qi: (lay.qrow(b, qi), h)),
                   pl.BlockSpec((SEQ, 256), lambda b, h, qi: (b, h)), pl.BlockSpec((T, 256), lambda b, h, qi: (b, h)),
                   pl.BlockSpec((SEQ, 128), lambda b, h, qi: (b, h)), pl.BlockSpec((T, 128), lambda b, h, qi: (b, h))],
        out_shape=[jax.ShapeDtypeStruct((lay.R, 1024), F32), jax.ShapeDtypeStruct((lay.NX, 1024), F32),
                   jax.ShapeDtypeStruct((B * T, 1024), F32), jax.ShapeDtypeStruct((lay.NX, 512), F32),
                   jax.ShapeDtypeStruct((B * T, 512), F32)],
        compiler_params=_cparams(("parallel", "parallel", "arbitrary")), name=name)(q, k, v, k, v, dmerged)


def _cmlp_piece(zu, zv, g, ws, bs):
    u, v = jax.nn.gelu(zu), jax.nn.gelu(zv)
    v = v * lax.rsqrt(jnp.mean(v * v, axis=-1, keepdims=True) + EPS) * g
    return u * (bdot(ws, v) + bs)


def _pieces():
    return [(c, g) for c in range(T // CMLP_CHUNK) for g in range(CMLP_GROUPS)]


def cmlp_merge_fwd(z, o, gvn, ws, bs, lay, name):
    def body(zu_ref, zv_ref, o_ref, g_ref, ws_ref, bs_ref, m_ref):
        m_ref[:, 0:512] = o_ref[...]
        for c, g in _pieces():
            rows, cols = slice(128 * c, 128 * (c + 1)), slice(128 * g, 128 * (g + 1))
            piece = _cmlp_piece(zu_ref[rows, cols], zv_ref[rows, cols], g_ref[:, cols], ws_ref[g], bs_ref[g])
            m_ref[rows, 512 + 128 * g:512 + 128 * (g + 1)] = piece.astype(BF16)

    return pl.pallas_call(
        body, grid=(lay.nt,),
        in_specs=[_row(512, 1), _row(512, 2), _row(512), _full((1, 512)), _full((4, 128, 128)), _full((4, 128, 1))],
        out_specs=_row(1024), out_shape=jax.ShapeDtypeStruct((lay.R, 1024), BF16),
        compiler_params=_cparams(("parallel",)), name=name)(z, z, o, gvn, ws, bs)


def ab_rows_bwd(z, tabc, tabs, dq, dkx, dkh, dvx, dvh, dmerged, gkv, gq, wk, wv, wqn, wqp, gvn, ws, bs, lay, name):
    def body(*refs):
        prep_in = refs[:3] + refs[5:9] + refs[11:17]
        zu_ref, zv_ref = refs[3:5]
        dq_ref, dcm_ref = refs[9:11]
        gvn_ref, ws_ref, bs_ref = refs[17:20]
        dkx_ref, dkh_ref, dvx_ref, dvh_ref = refs[20:24]
        dz_ref, dgkv_ref, dgq_ref, dwk_ref, dwv_ref, dwqn_ref, dwqp_ref, dgvn_ref, dws_ref, dbs_ref = refs[24:]
        first = pl.program_id(0) == 0
        _, vjp = jax.vjp(_ab_prep, *[r[...].astype(F32) for r in prep_in])
        d = vjp((dq_ref[...], _xh_pick(lay, dkx_ref, dkh_ref), _xh_pick(lay, dvx_ref, dvh_ref)))
        dz_ref[:, 0:256] = d[0].astype(BF16)
        dz_ref[:, 256:512] = d[1].astype(BF16)
        dz_ref[:, 1536:1664] = d[2].astype(BF16)
        for ref, val in zip((dgkv_ref, dgq_ref, dwk_ref, dwv_ref, dwqn_ref, dwqp_ref), d[7:]):
            _acc(ref, val, first)
        dws = [0.0] * CMLP_GROUPS
        dbs = [0.0] * CMLP_GROUPS
        dgv = [0.0] * CMLP_GROUPS
        for c, g in _pieces():
            rows, cols = slice(128 * c, 128 * (c + 1)), slice(128 * g, 128 * (g + 1))
            _, vjp = jax.vjp(_cmlp_piece, zu_ref[rows, cols], zv_ref[rows, cols], gvn_ref[:, cols], ws_ref[g], bs_ref[g])
            dzu, dzv, dg_, dws_, dbs_ = vjp(dcm_ref[rows, cols])
            dz_ref[rows, 512 + 128 * g:512 + 128 * (g + 1)] = dzu.astype(BF16)
            dz_ref[rows, 1024 + 128 * g:1024 + 128 * (g + 1)] = dzv.astype(BF16)
            dws[g], dbs[g], dgv[g] = dws[g] + dws_, dbs[g] + dbs_, dgv[g] + dg_
        _acc(dgvn_ref, jnp.concatenate(dgv, axis=1), first)
        _acc(dws_ref, jnp.stack(dws), first)
        _acc(dbs_ref, jnp.stack(dbs), first)

    acc_shapes = [(1, 256), (1, 256), (256, 512), (256, 512), (256, 512), (256, 256), (1, 512), (4, 128, 128), (4, 128, 1)]
    return pl.pallas_call(
        body, grid=(lay.nt,),
        in_specs=_ab_prep_specs(lay)[:3] + [_row(512, 1), _row(512, 2)] + _ab_prep_specs(lay)[3:7]
        + [_row(1024), _row(512, 1)] + _ab_prep_specs(lay)[7:]
        + [_full((1, 512)), _full((4, 128, 128)), _full((4, 128, 1))] + _xh_specs(lay, 1024) + _xh_specs(lay, 512),
        out_specs=[_row(AB_IN_P)] + [_full(s) for s in acc_shapes],
        out_shape=[jax.ShapeDtypeStruct((lay.R, AB_IN_P), BF16)] + [jax.ShapeDtypeStruct(s, F32) for s in acc_shapes],
        compiler_params=_cparams(("arbitrary",)), name=name)(
            z, z, z, z, z, tabc, tabs, tabc, tabs, dq, dmerged, gkv, gq, wk, wv, wqn, wqp, gvn, ws, bs, dkx, dkh, dvx, dvh)


def _cd_prep(zrk, zrq, zsk, zsq0, zsq1, zsv, c256, s256, c128, s128):
    rk = rope(zrk * (RET_QK ** -0.5), c256, s256)
    rq = rope(zrq, c256, s256)
    sk = rope(zsk, c128, s128)
    sq0, sq1 = rope(zsq0, c256, s256), rope(zsq1, c256, s256)
    e0, e1 = _expand_matrix(0), _expand_matrix(1)
    return rq, rk, sq0, sq1, bdot(sk, e0), bdot(sk, e1), bdot(zsv, e0), bdot(zsv, e1)


def _cd_prep_specs(lay):
    return [_row(256, 0), _row(256, 4), _row(128, 6), _row(256, 7), _row(256, 8), _row(128, 7),
            _ropespec(lay, 256), _ropespec(lay, 256), _ropespec(lay, 128), _ropespec(lay, 128)]


def cd_prep_fwd(z, tabc, tabs, lay, name):
    def body(*refs):
        ins, (rq_ref, rk_ref, sq_ref, ke_ref, ve_ref) = refs[:10], refs[10:]
        rq, rk, sq0, sq1, k0, k1, v0, v1 = _cd_prep(*[r[...] for r in ins])
        rq_ref[...] = rq.astype(BF16)
        rk_ref[...] = rk.astype(BF16)
        for ref, (a, b) in ((sq_ref, (sq0, sq1)), (ke_ref, (k0, k1)), (ve_ref, (v0, v1))):
            ref[:, 0:256] = a.astype(BF16)
            ref[:, 256:512] = b.astype(BF16)

    R = lay.R
    return pl.pallas_call(
        body, grid=(lay.nt,), in_specs=_cd_prep_specs(lay),
        out_specs=[_row(256), _row(256), _row(512), _row(512), _row(512)],
        out_shape=[jax.ShapeDtypeStruct((R, w), BF16) for w in (256, 256, 512, 512, 512)],
        compiler_params=_cparams(("parallel",)), name=name)(z, z, z, z, z, z, tabc, tabs, tabc, tabs)


def _ret_sample(h, qs, ks, vs, df, db):
    lgf, lgb = log_sigmoid(df), log_sigmoid(db)
    idx = lax.broadcasted_iota(jnp.int32, (T, 1), 0).astype(F32)
    diff = idx - lax.broadcasted_iota(jnp.int32, (1, T), 1).astype(F32)
    intra = (jnp.where(diff >= 0, jnp.exp(lgf * jnp.maximum(diff, 0.0)), 0.0)
             + jnp.where(diff <= 0, jnp.exp(lgb * jnp.maximum(-diff, 0.0)), 0.0))
    qdf, kdf, cdf = jnp.exp(lgf * (idx + 1.0)), jnp.exp(lgf * (T - 1.0 - idx)), jnp.exp(lgf * T)
    qdb, kdb, cdb = jnp.exp(lgb * (T - idx)), jnp.exp(lgb * idx), jnp.exp(lgb * T)
    mask = _head_mask(qs[0].shape, h)
    qs = [q * mask for q in qs]
    ys = [bdot(bdot_nt(q, k) * intra, v) for q, k, v in zip(qs, ks, vs)]
    n = len(qs)
    state = bdot_tn(ks[0] * kdf, vs[0])
    for i in range(1, n):
        ys[i] = ys[i] + bdot(qs[i] * qdf, state)
        if i + 1 < n:
            state = state * cdf + bdot_tn(ks[i] * kdf, vs[i])
    state = bdot_tn(ks[0] * kdb, vs[0])
    for i in range(n - 1, 0, -1):
        ys[i] = ys[i] + bdot(qs[i] * qdb, state)
        if i > 1:
            state = state * cdb + bdot_tn(ks[i] * kdb, vs[i])
    return ys


def _ret_specs(lay):
    nxt, SEQ = lay.nxt, lay.SEQ
    xs = lambda w, col: pl.BlockSpec((SEQ, w), lambda b, h: (b, col(h)))
    hs = lambda w, col: pl.BlockSpec((T, w), lambda b, h: (nxt + b, col(h)))
    zero, head = (lambda h: 0), (lambda h: 2 + h)
    dec = pl.BlockSpec((None, 8, 128), lambda b, h: (h, 0, 0))
    return [xs(256, zero), hs(256, zero), xs(256, zero), hs(256, zero), xs(128, head), hs(128, head), dec, dec]


def _ret_tiles(x_ref, h_ref, tps, cast=None):
    tiles = [h_ref[...]] + [x_ref[i * T:(i + 1) * T, :] for i in range(tps)]
    return [t.astype(cast) for t in tiles] if cast is not None else tiles


def ret_fwd(rq, rk, z, decf, decb, lay, name):
    tps, SEQ = lay.tps, lay.SEQ

    def body(qx_ref, qh_ref, kx_ref, kh_ref, vx_ref, vh_ref, df_ref, db_ref, yx_ref, yh_ref):
        ys = _ret_sample(pl.program_id(1), _ret_tiles(qx_ref, qh_ref, tps), _ret_tiles(kx_ref, kh_ref, tps),
                         _ret_tiles(vx_ref, vh_ref, tps), df_ref[0:1, 0:1], db_ref[0:1, 0:1])
        yh_ref[...] = ys[0]
        for i in range(tps):
            yx_ref[i * T:(i + 1) * T, :] = ys[i + 1]

    return pl.pallas_call(
        body, grid=(lay.B, RET_HEADS), in_specs=_ret_specs(lay),
        out_specs=[pl.BlockSpec((SEQ, 128), lambda b, h: (b, h)), pl.BlockSpec((T, 128), lambda b, h: (b, h))],
        out_shape=[jax.ShapeDtypeStruct((lay.NX, 512), F32), jax.ShapeDtypeStruct((lay.B * T, 512), F32)],
        compiler_params=_cparams(("parallel", "arbitrary")), name=name)(rq, rq, rk, rk, z, z, decf, decb)


def ret_bwd(rq, rk, z, decf, decb, dy, lay, name):
    tps, SEQ, B = lay.tps, lay.SEQ, lay.B
    nxt = lay.nxt

    def body(qx_ref, qh_ref, kx_ref, kh_ref, vx_ref, vh_ref, df_ref, db_ref, dyx_ref, dyh_ref,
             dqx_ref, dqh_ref, dkx_ref, dkh_ref, dvx_ref, dvh_ref, ddf_ref, ddb_ref):
        h = pl.program_id(1)
        _, vjp = jax.vjp(functools.partial(_ret_sample, h), _ret_tiles(qx_ref, qh_ref, tps, F32),
                         _ret_tiles(kx_ref, kh_ref, tps, F32), _ret_tiles(vx_ref, vh_ref, tps), df_ref[0:1, 0:1],
                         db_ref[0:1, 0:1])
        dqs, dks, dvs, ddf, ddb = vjp(_ret_tiles(dyx_ref, dyh_ref, tps))
        first = h == 0
        _acc(dqh_ref, dqs[0], first)
        _acc(dkh_ref, dks[0], first)
        dvh_ref[...] = dvs[0]
        for i in range(tps):
            rows = slice(i * T, (i + 1) * T)
            _acc(dqx_ref.at[rows], dqs[i + 1], first)
            _acc(dkx_ref.at[rows], dks[i + 1], first)
            dvx_ref[rows, :] = dvs[i + 1]
        @pl.when(jnp.logical_and(pl.program_id(0) == 0, first))
        def _():
            ddf_ref[...] = jnp.zeros(ddf_ref.shape, F32)
            ddb_ref[...] = jnp.zeros(ddb_ref.shape, F32)

        ddf_ref[h] += jnp.broadcast_to(ddf, (8, 128))
        ddb_ref[h] += jnp.broadcast_to(ddb, (8, 128))

    acc_x, acc_h = pl.BlockSpec((SEQ, 256), lambda b, h: (b, 0)), pl.BlockSpec((T, 256), lambda b, h: (b, 0))
    head_x, head_h = pl.BlockSpec((SEQ, 128), lambda b, h: (b, h)), pl.BlockSpec((T, 128), lambda b, h: (b, h))
    dec = pl.BlockSpec((RET_HEADS, 8, 128), lambda b, h: (0, 0, 0))
    return pl.pallas_call(
        body, grid=(B, RET_HEADS),
        in_specs=_ret_specs(lay) + [head_x, pl.BlockSpec((T, 128), lambda b, h: (nxt + b, h))],
        out_specs=[acc_x, acc_h, acc_x, acc_h, head_x, head_h, dec, dec],
        out_shape=[jax.ShapeDtypeStruct((lay.NX, 256), F32), jax.ShapeDtypeStruct((B * T, 256), F32),
                   jax.ShapeDtypeStruct((lay.NX, 256), F32), jax.ShapeDtypeStruct((B * T, 256), F32),
                   jax.ShapeDtypeStruct((lay.NX, 512), F32), jax.ShapeDtypeStruct((B * T, 512), F32),
                   jax.ShapeDtypeStruct((RET_HEADS, 8, 128), F32), jax.ShapeDtypeStruct((RET_HEADS, 8, 128), F32)],
        compiler_params=_cparams(("arbitrary", "arbitrary")), name=name)(rq, rq, rk, rk, z, z, decf, decb, dy, dy)


_SWA_SCALE = SWA_HEAD_DIM ** -0.5


def _swa_head(qh, sw, kh, vw, vh, sink):
    sh = bdot_nt(qh, kh) * _SWA_SCALE
    m = jnp.maximum(jnp.max(sh, axis=-1, keepdims=True), sink)
    if sw is not None:
        m = jnp.maximum(m, jnp.max(sw, axis=-1, keepdims=True))
    m = lax.stop_gradient(m)
    eh, es = jnp.exp(sh - m), jnp.exp(sink - m)
    tot = jnp.sum(eh, axis=-1, keepdims=True) + es
    if sw is None:
        return bdot(eh * (1.0 / tot), vh)
    ew = jnp.exp(sw - m)
    inv = 1.0 / (tot + jnp.sum(ew, axis=-1, keepdims=True))
    return bdot(ew * inv, vw) + bdot(eh * inv, vh)


def _swa_x(t0, kpos0, sq, kw, vw, kh, vh, *sinks):
    t = t0 + lax.broadcasted_iota(jnp.int32, (T, 1), 0)
    pos = kpos0 + lax.broadcasted_iota(jnp.int32, (1, SWA_SPAN), 1)
    band = jnp.abs(t - pos) <= SWA_WINDOW
    out = 0.0
    for i in range(SWA_GROUPS):
        mi = _head_mask(sq.shape, i)
        qh = sq * mi
        sw = jnp.where(band, bdot_nt(qh, kw) * _SWA_SCALE, NEG_INF)
        out = out + _swa_head(qh, sw, kh, vw, vh, sinks[i]) * mi
    return out


def _swa_h(sq, kh, vh, *sinks):
    out = 0.0
    for i in range(SWA_GROUPS):
        mi = _head_mask(sq.shape, i)
        out = out + _swa_head(sq * mi, None, kh, None, vh, sinks[i]) * mi
    return out


def _swa_specs(lay):
    nxt, SEQ = lay.nxt, lay.SEQ
    return [pl.BlockSpec((T, 256), lambda g, b, qi: (lay.qrow(b, qi), g)),
            pl.BlockSpec((SEQ, 256), lambda g, b, qi: (b, g)), pl.BlockSpec((SEQ, 256), lambda g, b, qi: (b, g)),
            pl.BlockSpec((T, 256), lambda g, b, qi: (nxt + b, g)), pl.BlockSpec((T, 256), lambda g, b, qi: (nxt + b, g)),
            pl.BlockSpec((None, 4, 8, 128), lambda g, b, qi: (g, 0, 0, 0))]


def _swa_start(qi, SEQ):
    return pl.multiple_of(jnp.clip(qi * T - SWA_WINDOW, 0, SEQ - SWA_SPAN), SWA_WINDOW)


def swa_fwd(sq, kexp, vexp, sink, lay, name):
    tps, SEQ = lay.tps, lay.SEQ

    def body(sq_ref, kx_ref, vx_ref, kh_ref, vh_ref, sink_ref, o_ref):
        qi = pl.program_id(2)
        f = lambda r: r[...]
        sinks = [sink_ref[i][0:1, 0:1] for i in range(SWA_GROUPS)]

        @pl.when(qi < tps)
        def _():
            k0 = _swa_start(qi, SEQ)
            kw, vw = kx_ref[pl.ds(k0, SWA_SPAN), :], vx_ref[pl.ds(k0, SWA_SPAN), :]
            o_ref[...] = _swa_x(qi * T, k0, f(sq_ref), kw, vw, f(kh_ref), f(vh_ref), *sinks).astype(BF16)

        @pl.when(qi == tps)
        def _():
            o_ref[...] = _swa_h(f(sq_ref), f(kh_ref), f(vh_ref), *sinks).astype(BF16)

    return pl.pallas_call(
        body, grid=(SWA_KV_HEADS, lay.B, lay.nq), in_specs=_swa_specs(lay),
        out_specs=pl.BlockSpec((T, 256), lambda g, b, qi: (lay.qrow(b, qi), g)),
        out_shape=jax.ShapeDtypeStruct((lay.R, 512), BF16),
        compiler_params=_cparams(("parallel", "parallel", "arbitrary")), name=name)(sq, kexp, vexp, kexp, vexp, sink)


def swa_bwd(sq, kexp, vexp, sink, dmerged, lay, name):
    tps, SEQ, B = lay.tps, lay.SEQ, lay.B

    def body(sq_ref, kx_ref, vx_ref, kh_ref, vh_ref, sink_ref, do_ref, dsq_ref, dkx_ref, dkh_ref, dvx_ref, dvh_ref, dsink_ref):
        b, qi = pl.program_id(1), pl.program_id(2)
        f = lambda r: r[...].astype(F32)
        sinks = [sink_ref[i][0:1, 0:1] for i in range(SWA_GROUPS)]
        very_first = jnp.logical_and(b == 0, qi == 0)

        def acc_sink(ds):
            for i in range(SWA_GROUPS):
                _acc(dsink_ref.at[i], jnp.broadcast_to(ds[i], (8, 128)), very_first)

        @pl.when(qi == 0)
        def _():
            for ref in (dkx_ref, dkh_ref, dvx_ref, dvh_ref):
                ref[...] = jnp.zeros(ref.shape, F32)

        @pl.when(qi < tps)
        def _():
            k0 = _swa_start(qi, SEQ)
            win = pl.ds(k0, SWA_SPAN)
            kw, vw = kx_ref[win, :].astype(F32), vx_ref[win, :].astype(F32)
            _, vjp = jax.vjp(functools.partial(_swa_x, qi * T, k0), f(sq_ref), kw, vw, f(kh_ref), f(vh_ref), *sinks)
            d = vjp(do_ref[...])
            dsq_ref[...] = d[0]
            dkx_ref[win, :] += d[1]
            dvx_ref[win, :] += d[2]
            dkh_ref[...] += d[3]
            dvh_ref[...] += d[4]
            acc_sink(d[5:9])

        @pl.when(qi == tps)
        def _():
            _, vjp = jax.vjp(_swa_h, f(sq_ref), f(kh_ref), f(vh_ref), *sinks)
            d = vjp(do_ref[...])
            dsq_ref[...] = d[0]
            dkh_ref[...] += d[1]
            dvh_ref[...] += d[2]
            acc_sink(d[3:7])

    xs = pl.BlockSpec((SEQ, 256), lambda g, b, qi: (b, g))
    hs = pl.BlockSpec((T, 256), lambda g, b, qi: (b, g))
    return pl.pallas_call(
        body, grid=(SWA_KV_HEADS, B, lay.nq),
        in_specs=_swa_specs(lay) + [pl.BlockSpec((T, 256), lambda g, b, qi: (lay.qrow(b, qi), 2 + g))],
        out_specs=[pl.BlockSpec((T, 256), lambda g, b, qi: (lay.qrow(b, qi), g)), xs, hs, xs, hs,
                   pl.BlockSpec((None, 4, 8, 128), lambda g, b, qi: (g, 0, 0, 0))],
        out_shape=[jax.ShapeDtypeStruct((lay.R, 512), F32), jax.ShapeDtypeStruct((lay.NX, 512), F32),
                   jax.ShapeDtypeStruct((B * T, 512), F32), jax.ShapeDtypeStruct((lay.NX, 512), F32),
                   jax.ShapeDtypeStruct((B * T, 512), F32), jax.ShapeDtypeStruct((SWA_KV_HEADS, 4, 8, 128), F32)],
        compiler_params=_cparams(("arbitrary", "arbitrary", "arbitrary")), name=name)(
            sq, kexp, vexp, kexp, vexp, sink, dmerged)


def _cd_merge_piece(y, rg, g):
    return (y * lax.rsqrt(jnp.mean(y * y, axis=-1, keepdims=True) + EPS) * g) * (rg * jax.nn.sigmoid(rg))


def cd_merge_fwd(y, z, o, gn, lay, name):
    def body(yx_ref, yh_ref, rga_ref, rgb_ref, o_ref, g_ref, m_ref):
        y = _xh_pick(lay, yx_ref, yh_ref)
        for h in range(RET_HEADS):
            cols = slice(128 * h, 128 * (h + 1))
            rg_ref, rcols = (rga_ref, cols) if h < 2 else (rgb_ref, slice(128 * (h - 2), 128 * (h - 1)))
            m_ref[:, cols] = _cd_merge_piece(y[:, cols], rg_ref[:, rcols], g_ref[:, cols]).astype(BF16)
        m_ref[:, 512:1024] = o_ref[...]

    return pl.pallas_call(
        body, grid=(lay.nt,), in_specs=_xh_specs(lay, 512) + [_row(256, 5), _row(256, 6), _row(512), _full((1, 512))],
        out_specs=_row(1024), out_shape=jax.ShapeDtypeStruct((lay.R, 1024), BF16),
        compiler_params=_cparams(("parallel",)), name=name)(*y, z, z, o, gn)


def cd_merge_bwd(y, z, gn, dmerged, lay, name):
    def body(yx_ref, yh_ref, rga_ref, rgb_ref, g_ref, dm_ref, dy_ref, drg_ref, dg_ref):
        first = pl.program_id(0) == 0
        y = _xh_pick(lay, yx_ref, yh_ref)
        dgs = []
        for h in range(RET_HEADS):
            cols = slice(128 * h, 128 * (h + 1))
            rg_ref, rcols = (rga_ref, cols) if h < 2 else (rgb_ref, slice(128 * (h - 2), 128 * (h - 1)))
            _, vjp = jax.vjp(_cd_merge_piece, y[:, cols], rg_ref[:, rcols], g_ref[:, cols])
            dy, drg, dg = vjp(dm_ref[:, cols])
            dy_ref[:, cols] = dy
            drg_ref[:, cols] = drg
            dgs.append(dg)
        _acc(dg_ref, jnp.concatenate(dgs, axis=1), first)

    return pl.pallas_call(
        body, grid=(lay.nt,), in_specs=_xh_specs(lay, 512) + [_row(256, 5), _row(256, 6), _full((1, 512)), _row(512, 0)],
        out_specs=[_row(512), _row(512), _full((1, 512))],
        out_shape=[jax.ShapeDtypeStruct((lay.R, 512), F32), jax.ShapeDtypeStruct((lay.R, 512), F32),
                   jax.ShapeDtypeStruct((1, 512), F32)],
        compiler_params=_cparams(("arbitrary",)), name=name)(*y, z, z, gn, dmerged)


def cd_rows_bwd(z, tabc, tabs, dsq, drg, drq, drk, dke, dve, drv, lay, name):
    def body(*refs):
        ins = refs[:10]
        dsq_ref, drg_ref = refs[10:12]
        drq, drk, dke, dve, drv = (_xh_pick(lay, refs[12 + 2 * n], refs[13 + 2 * n]) for n in range(5))
        dz_ref = refs[22]
        _, vjp = jax.vjp(_cd_prep, *[r[...] for r in ins])
        cts = (drq, drk, dsq_ref[:, 0:256], dsq_ref[:, 256:512], dke[:, 0:256], dke[:, 256:512],
               dve[:, 0:256], dve[:, 256:512])
        dzrk, dzrq, dzsk, dzsq0, dzsq1, dzsv = vjp(cts)[:6]
        dz_ref[:, 0:256] = dzrk.astype(BF16)
        dz_ref[:, 256:768] = drv.astype(BF16)
        dz_ref[:, 768:896] = dzsk.astype(BF16)
        dz_ref[:, 896:1024] = dzsv.astype(BF16)
        dz_ref[:, 1024:1280] = dzrq.astype(BF16)
        dz_ref[:, 1280:1792] = drg_ref[...].astype(BF16)
        dz_ref[:, 1792:2048] = dzsq0.astype(BF16)
        dz_ref[:, 2048:2304] = dzsq1.astype(BF16)

    return pl.pallas_call(
        body, grid=(lay.nt,),
        in_specs=_cd_prep_specs(lay) + [_row(512), _row(512)] + _xh_specs(lay, 256) + _xh_specs(lay, 256)
        + _xh_specs(lay, 512) + _xh_specs(lay, 512) + _xh_specs(lay, 512),
        out_specs=_row(2304), out_shape=jax.ShapeDtypeStruct((lay.R, 2304), BF16),
        compiler_params=_cparams(("parallel",)), name=name)(
            z, z, z, z, z, z, tabc, tabs, tabc, tabs, dsq, drg, *drq, *drk, *dke, *dve, *drv)


def _pos():
    return lax.axis_index("x"), lax.axis_index("y"), lax.axis_index("c")


def _flip(v, bit):
    return 1 - v if bit else v


def _comm_call(name, body, ins, out_shapes, n_remote, n_local, aliases=None):
    return pl.pallas_call(
        body, in_specs=[ANY] * len(ins), out_specs=[ANY] * len(out_shapes), out_shape=out_shapes,
        scratch_shapes=[pltpu.SemaphoreType.DMA((n_remote,)), pltpu.SemaphoreType.DMA((n_remote,)),
                        pltpu.SemaphoreType.DMA((n_local,))],
        input_output_aliases=aliases or {}, name=name)(*ins)


def gather8(arr, name):
    def body(a_ref, o_ref, ssem, rsem, lsem):
        x, y, c = _pos()
        me = 4 * x + 2 * y + c
        loc = pltpu.make_async_copy(a_ref, o_ref.at[me], lsem.at[0])
        loc.start()
        cps = []
        for m in range(1, 8):
            peer = (_flip(x, m & 4), _flip(y, m & 2), _flip(c, m & 1))
            cps.append(pltpu.make_async_remote_copy(a_ref, o_ref.at[me], ssem.at[m - 1], rsem.at[m - 1],
                                                    device_id=peer, device_id_type=MESH))
            cps[-1].start()
        for cp in cps:
            cp.wait()
        loc.wait()

    return _comm_call(name, body, [arr], [jax.ShapeDtypeStruct((8,) + arr.shape, arr.dtype)], 7, 1)[0]


def gather_chips(arr, name):
    def body(a_ref, o_ref, ssem, rsem, lsem):
        x, y, c = _pos()
        k = 2 * x + y
        loc = pltpu.make_async_copy(a_ref, o_ref.at[k], lsem.at[0])
        loc.start()
        cps = []
        for m in range(1, 4):
            peer = (_flip(x, m & 2), _flip(y, m & 1), c)
            cps.append(pltpu.make_async_remote_copy(a_ref, o_ref.at[k], ssem.at[m - 1], rsem.at[m - 1],
                                                    device_id=peer, device_id_type=MESH))
            cps[-1].start()
        for cp in cps:
            cp.wait()
        loc.wait()

    return _comm_call(name, body, [arr], [jax.ShapeDtypeStruct((4,) + arr.shape, arr.dtype)], 3, 1)[0]


def gather_weights(arrs, name):
    n = len(arrs)

    def body(*refs):
        a_refs, o_refs, (isend, irecv, _) = refs[:n], refs[n:2 * n], refs[2 * n:]
        x, y, c = _pos()
        k = 2 * x + y
        sib = (x, y, 1 - c)
        chips = [(m, (_flip(x, m & 2), _flip(y, m & 1)), 2 * _flip(x, m & 2) + _flip(y, m & 1)) for m in range(1, 4)]
        waits = []
        for w, (a, o) in enumerate(zip(a_refs, o_refs)):
            H = a.shape[0] // 2
            own = pl.ds(c * H, H)
            first = [pltpu.make_async_remote_copy(a.at[own], o.at[k, own], isend.at[6 * w + m - 1], irecv.at[6 * w + m - 1],
                                                  device_id=(*chip, c), device_id_type=MESH) for m, chip, _ in chips]
            for cp in first:
                cp.start()
            waits.append((first, H, own, a, o, w))
        for first, H, own, a, o, w in waits:
            passed = []
            for (m, chip, kk), cp in zip(chips, first):
                pltpu.make_async_remote_copy(a.at[own], o.at[kk, own], isend.at[6 * w + m - 1], irecv.at[6 * w + m - 1],
                                             device_id=(*chip, c), device_id_type=MESH).wait_recv()
                fw = pltpu.make_async_remote_copy(o.at[kk, own], o.at[kk, own], isend.at[6 * w + 2 + m], irecv.at[6 * w + 2 + m],
                                                  device_id=sib, device_id_type=MESH)
                fw.start()
                passed.append(fw)
            for fw in passed:
                fw.wait_recv()
            for cp in first + passed:
                cp.wait_send()

    outs = [jax.ShapeDtypeStruct((4,) + a.shape, a.dtype) for a in arrs]
    return _comm_call(name, body, list(arrs), outs, 6 * n, 1)


def swap_other_half(arrs, name):
    n = len(arrs)

    def body(*refs):
        a_refs, o_refs, (ssem, rsem, _) = refs[:n], refs[n:2 * n], refs[2 * n:]
        x, y, c = _pos()
        cps = []
        for w, (a, o) in enumerate(zip(a_refs, o_refs)):
            H = a.shape[1] // 2
            cps.append(pltpu.make_async_remote_copy(a.at[:, pl.ds((1 - c) * H, H)], o, ssem.at[w], rsem.at[w],
                                                    device_id=(x, y, 1 - c), device_id_type=MESH))
            cps[-1].start()
        for cp in cps:
            cp.wait()

    outs = [jax.ShapeDtypeStruct((4, a.shape[1] // 2) + a.shape[2:], a.dtype) for a in arrs]
    return _comm_call(name, body, list(arrs), outs, n, 1)


def exchange_chips(arrs, name):
    n = len(arrs)

    def body(*refs):
        a_refs, o_refs, (ssem, rsem, lsem) = refs[:n], refs[n:2 * n], refs[2 * n:]
        x, y, c = _pos()
        k = 2 * x + y
        cps = []
        for w, (a, o) in enumerate(zip(a_refs, o_refs)):
            cps.append(pltpu.make_async_copy(a.at[k], o.at[k], lsem.at[w]))
            cps[-1].start()
            for m in range(1, 4):
                px, py = _flip(x, m & 2), _flip(y, m & 1)
                cps.append(pltpu.make_async_remote_copy(a.at[2 * px + py], o.at[k], ssem.at[3 * w + m - 1], rsem.at[3 * w + m - 1],
                                                        device_id=(px, py, c), device_id_type=MESH))
                cps[-1].start()
        for cp in cps:
            cp.wait()

    return _comm_call(name, body, list(arrs), [jax.ShapeDtypeStruct(a.shape, a.dtype) for a in arrs], 3 * n, n)


def share_halves(arrs, name):
    n = len(arrs)

    def body(*refs):
        o_refs, (ssem, rsem, _) = refs[n:2 * n], refs[2 * n:]
        x, y, c = _pos()
        cps = []
        for w, o in enumerate(o_refs):
            H = o.shape[0] // 2
            mine = o.at[pl.ds(c * H, H)]
            cps.append(pltpu.make_async_remote_copy(mine, mine, ssem.at[w], rsem.at[w], device_id=(x, y, 1 - c),
                                                    device_id_type=MESH))
            cps[-1].start()
        for cp in cps:
            cp.wait()

    outs = [jax.ShapeDtypeStruct(a.shape, a.dtype) for a in arrs]
    return _comm_call(name, body, list(arrs), outs, n, 1, aliases={i: i for i in range(n)})


def _adamw(w, g, m, v):
    m = ADAM_B1 * m + (1.0 - ADAM_B1) * g
    v = ADAM_B2 * v + (1.0 - ADAM_B2) * (g * g)
    m_hat = m / (1.0 - ADAM_B1 ** ADAM_STEP)
    v_hat = v / (1.0 - ADAM_B2 ** ADAM_STEP)
    return -ADAM_LR * (m_hat / (jnp.sqrt(v_hat) + ADAM_EPS) + ADAM_WD * w), m, v


def _rows_tile(R, C):
    return _pick(R, max(8, (2 << 20) // (4 * C) // 8 * 8), 8)


def chip_partial(gs, buf, cidx, name):
    _, L, R, C = gs.shape
    H, tr = L // 2, _rows_tile(R, C)

    def body(c_ref, g_ref, b_ref, o_ref):
        o_ref[...] = (g_ref[...] + b_ref[...]).astype(BF16)

    return pl.pallas_call(
        body, grid_spec=pltpu.PrefetchScalarGridSpec(
            num_scalar_prefetch=1, grid=(4, H, R // tr),
            in_specs=[pl.BlockSpec((None, None, tr, C), lambda s, l, r, c: (s, c[0] * H + l, r, 0)),
                      pl.BlockSpec((None, None, tr, C), lambda s, l, r, c: (s, l, r, 0))],
            out_specs=pl.BlockSpec((None, None, tr, C), lambda s, l, r, c: (s, l, r, 0))),
        out_shape=jax.ShapeDtypeStruct((4, H, R, C), BF16),
        compiler_params=_cparams(("parallel", "parallel", "parallel")), name=name)(cidx, gs, buf)


def sum_chips(parts, cidx, name):
    _, H, R, C = parts.shape
    tr = _rows_tile(R, C)

    def body(c_ref, p_ref, g_out):
        g = p_ref[0].astype(F32)
        for s in range(1, 4):
            g = g + p_ref[s].astype(F32)
        g_out[...] = g

    return pl.pallas_call(
        body, grid_spec=pltpu.PrefetchScalarGridSpec(
            num_scalar_prefetch=1, grid=(H, R // tr),
            in_specs=[pl.BlockSpec((4, None, tr, C), lambda l, r, c: (0, l, r, 0))],
            out_specs=pl.BlockSpec((None, tr, C), lambda l, r, c: (c[0] * H + l, r, 0))),
        out_shape=jax.ShapeDtypeStruct((2 * H, R, C), F32),
        compiler_params=_cparams(("parallel", "parallel")), name=name)(cidx, parts)


def sum8(arr, name):
    n = arr.shape[1]
    tr = _pick(n, 512, 8)

    def body(a_ref, o_ref):
        s = a_ref[0]
        for j in range(1, 8):
            s = s + a_ref[j]
        o_ref[...] = s

    return pl.pallas_call(
        body, grid=(n // tr,), in_specs=[pl.BlockSpec((8, tr, 128), lambda i: (0, i, 0))],
        out_specs=pl.BlockSpec((tr, 128), lambda i: (i, 0)), out_shape=jax.ShapeDtypeStruct((n, 128), F32),
        compiler_params=_cparams(("parallel",)), name=name)(arr)


def adam_rows(w, g, m, v, name, emit_g=False):
    n, C = w.shape
    tr = _rows_tile(n, C)

    def body(w_ref, g_ref, m_ref, v_ref, *outs):
        d_out, m_out, v_out = outs[-3:]
        d_out[...], m_out[...], v_out[...] = _adamw(w_ref[...], g_ref[...], m_ref[...], v_ref[...])
        if emit_g:
            outs[0][...] = g_ref[...]

    spec = pl.BlockSpec((tr, C), lambda i: (i, 0))
    n_out = 4 if emit_g else 3
    return pl.pallas_call(
        body, grid=(n // tr,), in_specs=[spec] * 4, out_specs=[spec] * n_out,
        out_shape=[jax.ShapeDtypeStruct((n, C), F32)] * n_out, compiler_params=_cparams(("parallel",)), name=name)(w, g, m, v)


def _silu(c):
    return c * jax.nn.sigmoid(c)


def ada_fwd(c_all, w, b, name):
    NC, D = c_all.shape
    L, _, Wc = w.shape
    tn = _pick(Wc, 512, 128)

    def body(c_ref, w_ref, b_ref, o_ref):
        o_ref[...] = bdot(_silu(c_ref[...]), w_ref[...]) + b_ref[...]

    return pl.pallas_call(
        body, grid=(L, Wc // tn),
        in_specs=[pl.BlockSpec((NC, D), lambda l, j: (0, 0)), pl.BlockSpec((None, D, tn), lambda l, j: (l, 0, j)),
                  pl.BlockSpec((None, 1, tn), lambda l, j: (l, 0, j))],
        out_specs=pl.BlockSpec((None, NC, tn), lambda l, j: (l, 0, j)), out_shape=jax.ShapeDtypeStruct((L, NC, Wc), F32),
        compiler_params=_cparams(("parallel", "parallel")), name=name)(c_all, w, b)


def ada_bwd(c_all, w, dmod, name):
    NC, D = c_all.shape
    L, _, Wc = w.shape
    tn = _pick(Wc, 512, 128)

    def body(c_ref, w_ref, d_ref, dw_ref, db_ref, dc_ref):
        first = jnp.logical_and(pl.program_id(0) == 0, pl.program_id(1) == 0)
        f = lambda cs, ww: bdot(cs, ww)
        _, vjp = jax.vjp(f, _silu(c_ref[...]), w_ref[...])
        dcs, dw = vjp(d_ref[...])
        dw_ref[...] = dw
        db_ref[...] = jnp.sum(d_ref[...], axis=0, keepdims=True)
        _acc(dc_ref, dcs, first)

    return pl.pallas_call(
        body, grid=(L, Wc // tn),
        in_specs=[pl.BlockSpec((NC, D), lambda l, j: (0, 0)), pl.BlockSpec((None, D, tn), lambda l, j: (l, 0, j)),
                  pl.BlockSpec((None, NC, tn), lambda l, j: (l, 0, j))],
        out_specs=[pl.BlockSpec((None, D, tn), lambda l, j: (l, 0, j)), pl.BlockSpec((None, 1, tn), lambda l, j: (l, 0, j)),
                   pl.BlockSpec((NC, D), lambda l, j: (0, 0))],
        out_shape=[jax.ShapeDtypeStruct((L, D, Wc), F32), jax.ShapeDtypeStruct((L, 1, Wc), F32),
                   jax.ShapeDtypeStruct((NC, D), F32)],
        compiler_params=_cparams(("arbitrary", "arbitrary")), name=name)(c_all, w, dmod)


def cctx_grad(dcs_twice, c_ctx, name):
    def body(d_ref, c_ref, o_ref):
        _, vjp = jax.vjp(_silu, c_ref[...])
        o_ref[...] = vjp(0.5 * d_ref[...])[0]

    D = c_ctx.shape[1]
    return pl.pallas_call(body, out_shape=jax.ShapeDtypeStruct((8, D), F32), name=name)(dcs_twice, c_ctx)


def _rope_tables(SEQ):
    t = jnp.arange(SEQ)
    row, col = (t // GRID_W).astype(F32), (t % GRID_W).astype(F32)
    n_freq = 16
    freqs = ROPE_THETA ** (-jnp.arange(n_freq, dtype=F32) / n_freq)
    ang = jnp.concatenate([row[:, None] * freqs, col[:, None] * freqs], axis=-1)
    cos, sin = jnp.cos(ang), jnp.sin(ang)
    c = jnp.tile(jnp.concatenate([cos, cos], axis=1), (1, 4))
    s = jnp.tile(jnp.concatenate([-sin, sin], axis=1), (1, 4))
    return (jnp.concatenate([c, jnp.ones((T, 256), F32)], axis=0), jnp.concatenate([s, jnp.zeros((T, 256), F32)], axis=0))


def _unshard_cols(g):
    return jnp.transpose(g, (1, 2, 0, 3)).reshape(g.shape[1], g.shape[2], 4 * g.shape[3])


def _unshard_rows(g):
    return jnp.transpose(g, (1, 0, 2, 3)).reshape(g.shape[1], 4 * g.shape[2], g.shape[3])


def _shard_cols(w, n=4):
    L, R, C = w.shape
    return jnp.transpose(w.reshape(L, R, n, C // n), (2, 0, 1, 3))


def _shard_rows(w):
    L, R, C = w.shape
    return jnp.transpose(w.reshape(L, 4, R // 4, C), (1, 0, 2, 3))


def _ab_in_permute(w):
    L, D, _ = w.shape
    return jnp.concatenate([w[..., 0:256], w[..., 320:1600], w[..., 256:320], jnp.zeros((L, D, 64), w.dtype)], axis=-1)


def _ab_in_unpermute(g):
    return jnp.concatenate([g[..., 0:256], g[..., 1536:1600], g[..., 256:1536]], axis=-1)


def _split_heads(w, a):
    L, K, N = w.shape
    w4 = w.reshape(L, K, 4, N // 4)
    return w4[..., :a].reshape(L, K, 4 * a), w4[..., a:].reshape(L, K, N - 4 * a)


def _join_heads(p, q):
    L, K = p.shape[:2]
    return jnp.concatenate([p.reshape(L, K, 4, -1), q.reshape(L, K, 4, -1)], axis=-1).reshape(L, K, -1)


def _pack(arrs):
    parts = []
    for a in arrs:
        f = a.reshape(-1).astype(F32)
        parts.append(jnp.pad(f, (0, (-f.shape[0]) % 1024)))
    return jnp.concatenate(parts).reshape(-1, 128)


def _unpack(buf, like):
    out, r0 = [], 0
    for a in like:
        n = math.prod(a.shape)
        rows = (n + (-n) % 1024) // 128
        out.append(buf[r0:r0 + rows].reshape(-1)[:n].reshape(a.shape))
        r0 += rows
    return out


_SMALL = ("c_ctx", "ada_b", "norm_mix", "norm_ffn", "norm_final", "mla_q_norm", "mla_kv_norm", "cmlp_v_norm", "cmlp_ws",
          "cmlp_bs", "ret_decay_fwd", "ret_decay_bwd", "ret_norm", "swa_sink")
_BIG = ("ffn_in", "ffn_out", "ab_in", "ab_out", "mla_wq_b", "mla_wkv_b", "cd_in", "cd_out")
_WEIGHTS = ("c_ctx", "ada_w", "ada_b", "norm_mix", "norm_ffn", "norm_final", "ffn_in", "ffn_out", "ab_in", "ab_out",
            "mla_q_norm", "mla_kv_norm", "mla_wq_b", "mla_wkv_b", "cmlp_v_norm", "cmlp_ws", "cmlp_bs", "cd_in", "cd_out",
            "ret_decay_fwd", "ret_decay_bwd", "ret_norm", "swa_sink")


def kernel(x, c, ctx, c_ctx, ada_w, ada_b, norm_mix, norm_ffn, norm_final, ffn_in, ffn_out, ab_in, ab_out, mla_q_norm, mla_kv_norm, mla_wq_b, mla_wkv_b, cmlp_v_norm, cmlp_ws, cmlp_bs, cd_in, cd_out, ret_decay_fwd, ret_decay_bwd, ret_norm, swa_sink, loss_target, m_c_ctx, m_ada_w, m_ada_b, m_norm_mix, m_norm_ffn, m_norm_final, m_ffn_in, m_ffn_out, m_ab_in, m_ab_out, m_mla_q_norm, m_mla_kv_norm, m_mla_wq_b, m_mla_wkv_b, m_cmlp_v_norm, m_cmlp_ws, m_cmlp_bs, m_cd_in, m_cd_out, m_ret_decay_fwd, m_ret_decay_bwd, m_ret_norm, m_swa_sink, v_c_ctx, v_ada_w, v_ada_b, v_norm_mix, v_norm_ffn, v_norm_final, v_ffn_in, v_ffn_out, v_ab_in, v_ab_out, v_mla_q_norm, v_mla_kv_norm, v_mla_wq_b, v_mla_wkv_b, v_cmlp_v_norm, v_cmlp_ws, v_cmlp_bs, v_cd_in, v_cd_out, v_ret_decay_fwd, v_ret_decay_bwd, v_ret_norm, v_swa_sink):
    W = dict(c_ctx=c_ctx, ada_w=ada_w, ada_b=ada_b, norm_mix=norm_mix, norm_ffn=norm_ffn, norm_final=norm_final, ffn_in=ffn_in, ffn_out=ffn_out, ab_in=ab_in, ab_out=ab_out, mla_q_norm=mla_q_norm, mla_kv_norm=mla_kv_norm, mla_wq_b=mla_wq_b, mla_wkv_b=mla_wkv_b, cmlp_v_norm=cmlp_v_norm, cmlp_ws=cmlp_ws, cmlp_bs=cmlp_bs, cd_in=cd_in, cd_out=cd_out, ret_decay_fwd=ret_decay_fwd, ret_decay_bwd=ret_decay_bwd, ret_norm=ret_norm, swa_sink=swa_sink)
    M1 = dict(c_ctx=m_c_ctx, ada_w=m_ada_w, ada_b=m_ada_b, norm_mix=m_norm_mix, norm_ffn=m_norm_ffn, norm_final=m_norm_final, ffn_in=m_ffn_in, ffn_out=m_ffn_out, ab_in=m_ab_in, ab_out=m_ab_out, mla_q_norm=m_mla_q_norm, mla_kv_norm=m_mla_kv_norm, mla_wq_b=m_mla_wq_b, mla_wkv_b=m_mla_wkv_b, cmlp_v_norm=m_cmlp_v_norm, cmlp_ws=m_cmlp_ws, cmlp_bs=m_cmlp_bs, cd_in=m_cd_in, cd_out=m_cd_out, ret_decay_fwd=m_ret_decay_fwd, ret_decay_bwd=m_ret_decay_bwd, ret_norm=m_ret_norm, swa_sink=m_swa_sink)
    M2 = dict(c_ctx=v_c_ctx, ada_w=v_ada_w, ada_b=v_ada_b, norm_mix=v_norm_mix, norm_ffn=v_norm_ffn, norm_final=v_norm_final, ffn_in=v_ffn_in, ffn_out=v_ffn_out, ab_in=v_ab_in, ab_out=v_ab_out, mla_q_norm=v_mla_q_norm, mla_kv_norm=v_mla_kv_norm, mla_wq_b=v_mla_wq_b, mla_wkv_b=v_mla_wkv_b, cmlp_v_norm=v_cmlp_v_norm, cmlp_ws=v_cmlp_ws, cmlp_bs=v_cmlp_bs, cd_in=v_cd_in, cd_out=v_cd_out, ret_decay_fwd=v_ret_decay_fwd, ret_decay_bwd=v_ret_decay_bwd, ret_norm=v_ret_norm, swa_sink=v_swa_sink)

    B, SEQ, D = x.shape
    CTX = ctx.shape[1]
    lay = Layout(B, SEQ, CTX, D)
    nt, NX = lay.nt, lay.NX
    ix, iy, ic = lax.axis_index("x"), lax.axis_index("y"), lax.axis_index("c")
    chip, me = 2 * ix + iy, 4 * ix + 2 * iy + ic
    cidx = jnp.reshape(ic, (1,)).astype(jnp.int32)
    Wc = ada_w.shape[2]
    n_even, n_odd = ab_in.shape[0], cd_in.shape[0]

    rn_row = jnp.pad(ret_norm.reshape(1, -1), ((0, 0), (0, D - ret_norm.size)))
    pack0 = jnp.concatenate([c, rn_row, jnp.zeros((8 - (B + 1) % 8, D), F32)], axis=0) if (B + 1) % 8 else jnp.concatenate([c, rn_row], axis=0)
    g0 = gather8(pack0, "gather_cond")
    NC = -(-(8 * B + 1) // 16) * 16
    c_all = jnp.concatenate([g0[:, :B].reshape(8 * B, D), c_ctx[None], jnp.zeros((NC - 8 * B - 1, D), F32)], axis=0)
    rn_sh = ret_norm.shape[1]
    ret_norm_full = jnp.transpose(g0[0::2, B, :ret_norm.size].reshape(4, n_odd, rn_sh), (1, 0, 2)).reshape(n_odd, 4 * rn_sh)

    shards = [W[n].astype(BF16) for n in _BIG]
    gw = [lax.dynamic_update_index_in_dim(g, s, chip, 0) for g, s in zip(gather_weights(shards, "gather_weights"), shards)]
    w_ffn_in, w_ab_in, w_wq, w_wkv, w_cd_in = (_unshard_cols(gw[i]) for i in (0, 2, 4, 5, 6))
    w_ffn_out, w_ab_out, w_cd_out = (_unshard_rows(gw[i]) for i in (1, 3, 7))
    w_ab_in = _ab_in_permute(w_ab_in)
    w_ffn_out_t = jnp.transpose(w_ffn_out, (0, 2, 1))
    w_qn, w_qp = _split_heads(w_wq, MLA_NOPE)
    w_k, w_v = _split_heads(w_wkv, MLA_NOPE)

    ab_sh = lax.dynamic_slice_in_dim(ada_b, chip * Wc, Wc, axis=1)[:, None, :]
    mod_sh = ada_fwd(c_all, ada_w, ab_sh, "ada_fwd")
    mod_all = _unshard_cols(gather_chips(mod_sh, "gather_mod"))
    mod_mine = jnp.concatenate([lax.dynamic_slice_in_dim(mod_all, me * B, B, axis=1), mod_all[:, 8 * B:8 * B + 1]], axis=1)
    mod = mod_mine.reshape(DEPTH, B + 1, 6, D)

    tabc, tabs = _rope_tables(SEQ)
    bc8 = lambda a: jnp.broadcast_to(a.reshape(a.shape + (1, 1)), a.shape + (8, 128))
    row = lambda a: a.reshape(1, -1)

    S = jnp.concatenate([x.reshape(NX, D), ctx.reshape(B * CTX, D)], axis=0)
    saved = []
    xn = norm_mod_fwd(S, row(norm_mix[0]), mod[0], 0, lay, nt, "norm_mix_fwd0")
    for l in range(DEPTH):
        j, even = l // 2, l % 2 == 0
        if even:
            z = mm(xn, w_ab_in[j], name=f"ab_in{l}")
            q, k, v = ab_prep_fwd(z, tabc, tabs, row(mla_kv_norm[j]), row(mla_q_norm[j]), w_k[j], w_v[j], w_qn[j], w_qp[j],
                                  lay, f"ab_prep{l}")
            o = mla_fwd(q, k, v, lay, f"mla{l}")
            merged = cmlp_merge_fwd(z, o, row(cmlp_v_norm[j]), cmlp_ws[j], cmlp_bs[j][:, :, None], lay, f"cmlp{l}")
            w_out, mix = w_ab_out[j], (q, k, v)
        else:
            z = mm(xn, w_cd_in[j], name=f"cd_in{l}")
            rq, rk, sq, ke, ve = cd_prep_fwd(z, tabc, tabs, lay, f"cd_prep{l}")
            decf, decb, sink = bc8(ret_decay_fwd[j]), bc8(ret_decay_bwd[j]), bc8(swa_sink[j].reshape(SWA_KV_HEADS, SWA_GROUPS))
            yret = ret_fwd(rq, rk, z, decf, decb, lay, f"ret{l}")
            osw = swa_fwd(sq, ke, ve, sink, lay, f"swa{l}")
            merged = cd_merge_fwd(yret, z, osw, row(ret_norm_full[j]), lay, f"cd_merge{l}")
            w_out, mix = w_cd_out[j], (rq, rk, sq, ke, ve, decf, decb, sink, yret)
        y, S_mid, xn2 = mm_gated(merged, w_out, S, mod[l], 2, lay, name=f"mix_out{l}", n_tiles=nt,
                                 norm=(row(norm_ffn[l]), mod[l], 3))
        act, fa, fb = ffn_in_act(xn2, w_ffn_in[l], f"ffn_in{l}")
        nxt_norm = (row(norm_mix[l + 1]), mod[l + 1], 0) if l + 1 < DEPTH else None
        f, S_new, *xn_next = mm_gated(act, w_ffn_out[l], S_mid, mod[l], 5, lay, name=f"ffn_out{l}", n_tiles=nt, norm=nxt_norm)
        saved.append((S, xn, z, mix, merged, w_out, y, S_mid, xn2, (fa, fb), act, f))
        S, xn = S_new, (xn_next[0] if xn_next else None)

    loss_blk, dS, d_norm_final = loss_head(S, row(norm_final), loss_target.reshape(NX, D), lay, "loss_head")
    loss = lax.psum(loss_blk[0, 0], ("x", "y", "c"))

    G = {n: [None] * W[n].shape[0] for n in ("norm_mix", "norm_ffn", "mla_q_norm", "mla_kv_norm", "cmlp_v_norm", "cmlp_ws",
                                             "cmlp_bs", "ret_decay_fwd", "ret_decay_bwd", "ret_norm", "swa_sink")}
    GB = {n: [None] * cnt for n, cnt in (("ffn_in_a", DEPTH), ("ffn_in_b", DEPTH), ("ffn_out", DEPTH), ("ab_in", n_even), ("ab_out", n_even),
                                         ("wqn", n_even), ("wqp", n_even), ("wk", n_even), ("wv", n_even), ("cd_in", n_odd),
                                         ("cd_out", n_odd))}
    dmod = [None] * DEPTH
    df, dgate2 = gate_bwd(dS, saved[-1][-1], mod[DEPTH - 1], 5, lay, nt, f"ffn_gate_bwd{DEPTH - 1}")
    for l in reversed(range(DEPTH)):
        j, even = l // 2, l % 2 == 0
        S_in, xn, z, mix, merged, w_out, y, S_mid, xn2, (fa, fb), act, f = saved[l]
        da, db = ffn_out_dx_act(df, w_ffn_out_t[l], fa, fb, f"ffn_out_dx{l}")
        GB["ffn_out"][l] = mm(act, df, ta=True, name=f"ffn_out_dw{l}")
        GB["ffn_in_a"][l] = mm(xn2, da, ta=True, name=f"ffn_in_dwa{l}", split=2)
        GB["ffn_in_b"][l] = mm(xn2, db, ta=True, name=f"ffn_in_dwb{l}", split=2)
        dxn2 = ffn_in_dx(da, db, w_ffn_in[l], f"ffn_in_dx{l}")
        dS_mid, dss2, dg, dy, dgate1 = norm_mod_bwd(S_mid, row(norm_ffn[l]), mod[l], 3, dxn2, dS, lay, nt, f"norm_ffn_bwd{l}",
                                                    gate=(y, mod[l], 2))
        G["norm_ffn"][l] = dg
        dmerged = mm(dy, w_out, tb=True, name=f"mix_out_dx{l}")
        d_w_out = mm(merged, dy, ta=True, name=f"mix_out_dw{l}")
        if even:
            q, k, v = mix
            dq, dkx, dkh, dvx, dvh = mla_bwd(q, k, v, dmerged, lay, f"mla_bwd{l}")
            (dz, dgkv, dgq, dwk, dwv, dwqn, dwqp, dgvn, dws, dbs) = ab_rows_bwd(
                z, tabc, tabs, dq, dkx, dkh, dvx, dvh, dmerged, row(mla_kv_norm[j]), row(mla_q_norm[j]), w_k[j], w_v[j],
                w_qn[j], w_qp[j], row(cmlp_v_norm[j]), cmlp_ws[j], cmlp_bs[j][:, :, None], lay, f"ab_rows_bwd{l}")
            G["mla_kv_norm"][j], G["mla_q_norm"][j], G["cmlp_v_norm"][j] = dgkv, dgq, dgvn
            G["cmlp_ws"][j], G["cmlp_bs"][j] = dws, dbs
            GB["wk"][j], GB["wv"][j], GB["wqn"][j], GB["wqp"][j], GB["ab_out"][j] = dwk, dwv, dwqn, dwqp, d_w_out
            w_in = w_ab_in[j]
        else:
            rq, rk, sq, ke, ve, decf, decb, sink, yret = mix
            dyret, drg, dgn = cd_merge_bwd(yret, z, row(ret_norm_full[j]), dmerged, lay, f"cd_merge_bwd{l}")
            dqx, dqh, dkx, dkh, dvx, dvh, ddf, ddb = ret_bwd(rq, rk, z, decf, decb, dyret, lay, f"ret_bwd{l}")
            dsq, dkex, dkeh, dvex, dveh, dsink = swa_bwd(sq, ke, ve, sink, dmerged, lay, f"swa_bwd{l}")
            dz = cd_rows_bwd(z, tabc, tabs, dsq, drg, (dqx, dqh), (dkx, dkh), (dkex, dkeh), (dvex, dveh), (dvx, dvh), lay,
                             f"cd_rows_bwd{l}")
            G["ret_norm"][j], G["ret_decay_fwd"][j], G["ret_decay_bwd"][j] = dgn, ddf[:, 0, 0], ddb[:, 0, 0]
            G["swa_sink"][j] = dsink[:, :, 0, 0].reshape(-1)
            GB["cd_out"][j] = d_w_out
            w_in = w_cd_in[j]
        GB["ab_in" if even else "cd_in"][j] = mm(xn, dz, ta=True, name=f"mix_in_dw{l}")
        dxn = mm(dz, w_in, tb=True, name=f"mix_in_dx{l}")
        dmod_l = lambda dss1: jnp.concatenate([dss1, dgate1, dss2, dgate2], axis=1)
        if l > 0:
            dS, dss1, dg, df, dgate2_prev = norm_mod_bwd(S_in, row(norm_mix[l]), mod[l], 0, dxn, dS_mid, lay, nt,
                                                         f"norm_mix_bwd{l}", gate=(saved[l - 1][-1], mod[l - 1], 5))
            dmod[l], dgate2 = dmod_l(dss1), dgate2_prev
        else:
            dS, dss1, dg = norm_mod_bwd(S_in, row(norm_mix[l]), mod[l], 0, dxn, dS_mid, lay, nt, f"norm_mix_bwd{l}")
            dmod[l] = dmod_l(dss1)
        G["norm_mix"][l] = dg
    grad_x = dS[:NX].reshape(B, SEQ, D)

    st = lambda n: jnp.stack([g.reshape((4 * rn_sh,) if n == "ret_norm" else W[n].shape[1:]) for g in G[n]])
    small_parts = {n: st(n) for n in G}
    small_parts["norm_final"] = d_norm_final.reshape(-1)
    dmod_local = jnp.stack(dmod).reshape(DEPTH, B + 1, 6 * D)
    names1 = ["norm_mix", "norm_ffn", "norm_final", "mla_q_norm", "mla_kv_norm", "cmlp_v_norm", "cmlp_ws", "cmlp_bs",
              "ret_decay_fwd", "ret_decay_bwd", "ret_norm", "swa_sink"]
    like1 = [dmod_local] + [small_parts[n] for n in names1]
    g1 = gather8(_pack(like1), "gather_small_grads")
    tot1 = _unpack(sum8(g1, "sum_small_grads"), like1)
    sg = dict(zip(names1, tot1[1:]))
    n_dm = math.prod(dmod_local.shape)
    dm_each = g1[:, :-(-n_dm // 128)].reshape(8, -1)[:, :n_dm].reshape(8, DEPTH, B + 1, 6 * D)
    dmod_all = jnp.concatenate([jnp.transpose(dm_each[:, :, :B], (1, 0, 2, 3)).reshape(DEPTH, 8 * B, 6 * D),
                                tot1[0][:, B:B + 1], jnp.zeros((DEPTH, NC - 8 * B - 1, 6 * D), F32)], axis=1)
    dmod_sh = lax.dynamic_slice_in_dim(dmod_all, chip * Wc, Wc, axis=2)
    g_ada_w, g_ada_b_sh, dcs = ada_bwd(c_all, ada_w, dmod_sh, "ada_bwd")
    like2 = [dcs[8 * B], g_ada_b_sh]
    g2 = gather8(_pack(like2), "gather_ada_grads")
    tot2 = _unpack(sum8(g2, "sum_ada_grads"), like2)
    bc = lambda a: jnp.broadcast_to(a.reshape(1, D), (8, D))
    sg["c_ctx"] = cctx_grad(bc(tot2[0]), bc(c_ctx), "c_ctx_grad")[0]
    off = D + (-D) % 1024
    gab = g2[0::2, off // 128:(off + DEPTH * Wc) // 128].reshape(4, DEPTH, Wc)
    sg["ada_b"] = jnp.transpose(gab, (1, 0, 2)).reshape(DEPTH, 4 * Wc)
    sg["ret_norm"] = lax.dynamic_slice_in_dim(sg["ret_norm"], chip * rn_sh, rn_sh, axis=1)

    stack = lambda n: jnp.stack(GB[n])
    gs = {"ffn_in": jnp.concatenate([jnp.stack(GB[n], axis=1) for n in ("ffn_in_a", "ffn_in_b")], axis=0),
          "ffn_out": _shard_rows(stack("ffn_out")),
          "ab_in": _shard_cols(_ab_in_unpermute(stack("ab_in"))), "ab_out": _shard_rows(stack("ab_out")),
          "mla_wq_b": _shard_cols(_join_heads(stack("wqn"), stack("wqp"))),
          "mla_wkv_b": _shard_cols(_join_heads(stack("wk"), stack("wv"))),
          "cd_in": _shard_cols(stack("cd_in")), "cd_out": _shard_rows(stack("cd_out"))}
    bufs = swap_other_half([gs[n] for n in _BIG], "swap_core_halves")
    parts = [chip_partial(gs[n], b, cidx, f"chip_partial_{n}") for n, b in zip(_BIG, bufs)]
    arrived = exchange_chips(parts, "exchange_chips")
    grads = share_halves([sum_chips(p, cidx, f"sum_chips_{n}") for n, p in zip(_BIG, arrived)], "share_core_halves")
    flat2 = lambda a: a.reshape(-1, a.shape[-1])
    out = {}
    for n, g in zip(_BIG, grads):
        res = adam_rows(flat2(W[n]), flat2(g), flat2(M1[n]), flat2(M2[n]), f"adam_{n}", emit_g=True)
        out[n] = tuple(r.reshape(W[n].shape) for r in res)

    like_s = [W[n] for n in _SMALL]
    dsm, msm, vsm = adam_rows(_pack(like_s), _pack([sg[n].reshape(W[n].shape) for n in _SMALL]), _pack([M1[n] for n in _SMALL]),
                                _pack([M2[n] for n in _SMALL]), "adam_small")
    for n, d_, m_, v_ in zip(_SMALL, _unpack(dsm, like_s), _unpack(msm, like_s), _unpack(vsm, like_s)):
        out[n] = (sg[n].reshape(W[n].shape), d_, m_, v_)
    d_, m_, v_ = adam_rows(flat2(ada_w), flat2(g_ada_w), flat2(m_ada_w), flat2(v_ada_w), "adam_ada_w")
    out["ada_w"] = (g_ada_w, d_.reshape(ada_w.shape), m_.reshape(ada_w.shape), v_.reshape(ada_w.shape))

    return (loss, grad_x, *[out[n][0] for n in _WEIGHTS], *[out[n][1] for n in _WEIGHTS], *[out[n][2] for n in _WEIGHTS],
            *[out[n][3] for n in _WEIGHTS])
```

```python
import functools
import math

import jax
import jax.numpy as jnp
import numpy as np
from jax import lax
from jax.experimental import pallas as pl
from jax.experimental.pallas import tpu as pltpu

F32 = jnp.float32
BF16 = jnp.bfloat16
EPS = 1e-6
NEG_INF = -1e30
GRID_W = 64
ROPE_THETA = 10000.0
DEPTH = 4
MLA_HEADS, MLA_Q_LORA, MLA_KV_LORA, MLA_NOPE, MLA_ROPE, MLA_V = 4, 256, 256, 128, 64, 128
CMLP_GROUPS, CMLP_CHUNK = 4, 128
CMLP_WIDTH = 512
RET_HEADS, RET_QK, RET_V = 4, 64, 128
SWA_Q_HEADS, SWA_KV_HEADS, SWA_HEAD_DIM, SWA_WINDOW = 8, 2, 64, 128
SWA_GROUPS = SWA_Q_HEADS // SWA_KV_HEADS
AB_IN_P = 1664
ADAM_LR, ADAM_B1, ADAM_B2, ADAM_EPS, ADAM_WD, ADAM_STEP = 0.001, 0.9, 0.999, 1e-08, 0.01, 10

T = 256
SWA_SPAN = T + 2 * SWA_WINDOW
VMEM_LIMIT = 48 * 1024 * 1024
MESH = pl.DeviceIdType.MESH
ANY = pl.BlockSpec(memory_space=pl.ANY)


def _cparams(sem):
    return pltpu.CompilerParams(dimension_semantics=sem, vmem_limit_bytes=VMEM_LIMIT)


@functools.cache
def _bdot_fn(ca, cb):
    fa, fb = 1 - ca, 1 - cb

    def dg(p, q, cp, cq):
        return lax.dot_general(p.astype(BF16), q.astype(BF16), (((cp,), (cq,)), ((), ())), preferred_element_type=F32)

    @jax.custom_vjp
    def bd(a, b):
        return dg(a, b, ca, cb)

    def fwd(a, b):
        return dg(a, b, ca, cb), (a, b)

    def bwd(res, g):
        a, b = res
        da = dg(g, b, 1, fb) if ca == 1 else dg(b, g, fb, 1)
        db = dg(a, g, fa, 0) if cb == 0 else dg(g, a, 0, fa)
        return da, db

    bd.defvjp(fwd, bwd)
    return bd


def bdot(a, b):
    return _bdot_fn(1, 0)(a, b)


def bdot_nt(a, b):
    return _bdot_fn(1, 1)(a, b)


def bdot_tn(a, b):
    return _bdot_fn(0, 0)(a, b)


def _swap32(x):
    w = x.shape[-1]
    lane = lax.broadcasted_iota(jnp.int32, x.shape, 1)
    return jnp.where((lane & 32) == 0, pltpu.roll(x, w - 32, 1), pltpu.roll(x, 32, 1))


@jax.custom_vjp
def rope(x, c, s):
    return x * c + _swap32(x) * s


def _rope_fwd(x, c, s):
    return rope(x, c, s), (c, s)


def _rope_bwd(res, g):
    c, s = res
    return g * c + _swap32(g * s), jnp.zeros_like(c), jnp.zeros_like(s)


rope.defvjp(_rope_fwd, _rope_bwd)


def rms(x, g):
    return x * lax.rsqrt(jnp.mean(x * x, axis=-1, keepdims=True) + EPS) * g


def normmod(x, g, sh, sc):
    return rms(x, g) * (1.0 + sc) + sh


def log_sigmoid(x):
    return jnp.minimum(x, 0.0) - jnp.log(1.0 + jnp.exp(-jnp.abs(x)))


def _head_mask(shape, h):
    lane = lax.broadcasted_iota(jnp.int32, shape, 1)
    return ((lane >> 6) == h).astype(F32)


def _fold_matrix():
    i = lax.broadcasted_iota(jnp.int32, (256, 128), 0)
    j = lax.broadcasted_iota(jnp.int32, (256, 128), 1)
    return ((i & 63) == j).astype(F32)


def _expand_matrix(g):
    i = lax.broadcasted_iota(jnp.int32, (128, 256), 0)
    j = lax.broadcasted_iota(jnp.int32, (128, 256), 1)
    return (i == (j & 63) + 64 * g).astype(F32)


def _acc(ref, val, first):
    @pl.when(first)
    def _():
        ref[...] = val

    @pl.when(jnp.logical_not(first))
    def _():
        ref[...] += val


def _pick(n, cap, mult):
    best = None
    for d in range(mult, min(n, cap) + 1, mult):
        if n % d == 0:
            best = d
    return best if best is not None else n


def mm(a, b, *, ta=False, tb=False, name, split=1):
    M, K = (a.shape[1], a.shape[0]) if ta else a.shape
    N = b.shape[0] if tb else b.shape[1]
    Ns = N // split
    assert split == 1 or (ta and Ns % 128 == 0)
    if ta:
        tn = N
        tm = _pick(M, min(1536, (3 << 20) // tn), 128)
    else:
        tn = _pick(N, 768, 128)
        if tn < 256 and N <= 2304:
            tn = N
        tm = _pick(M, min(1536, (1 << 20) // tn), 128)
    tk = _pick(K, 2048 if not ta else 1024, 128) if K > 2816 or ta else K
    nk = K // tk
    grid = (M // tm, N // tn, nk)
    a_spec = pl.BlockSpec((tk, tm), lambda i, j, k: (k, i)) if ta else pl.BlockSpec((tm, tk), lambda i, j, k: (i, k))
    b_spec = pl.BlockSpec((tn, tk), lambda i, j, k: (j, k)) if tb else pl.BlockSpec((tk, tn), lambda i, j, k: (k, j))
    dims = (((0 if ta else 1,), (1 if tb else 0,)), ((), ()))

    def body(a_ref, b_ref, o_ref):
        part = lax.dot_general(a_ref[...], b_ref[...], dims, preferred_element_type=F32)
        first = pl.program_id(2) == 0
        if split > 1:
            for s in range(split):
                _acc(o_ref.at[s], part[:, s * Ns:(s + 1) * Ns], first)
        elif nk == 1:
            o_ref[...] = part
        else:
            _acc(o_ref, part, first)

    if split > 1:
        out_spec, out_shape = pl.BlockSpec((split, tm, Ns), lambda i, j, k: (0, i, 0)), (split, M, Ns)
    else:
        out_spec, out_shape = pl.BlockSpec((tm, tn), lambda i, j, k: (i, j)), (M, N)
    return pl.pallas_call(
        body, grid=grid, in_specs=[a_spec, b_spec], out_specs=out_spec, out_shape=jax.ShapeDtypeStruct(out_shape, F32),
        compiler_params=_cparams(("parallel", "parallel", "arbitrary")), name=name)(a, b)


def mm_gated(a, b, res, mod, gate_row, lay, *, name, n_tiles, norm=None):
    K, N = b.shape
    M = n_tiles * T

    def body(*refs):
        a_ref, b_ref, r_ref, g_ref = refs[:4]
        y = lax.dot_general(a_ref[...], b_ref[...], (((1,), (0,)), ((), ())), preferred_element_type=F32)
        new = r_ref[...] + g_ref[gate_row:gate_row + 1, :] * y
        if norm is None:
            y_ref, o_ref = refs[4:]
        else:
            gn_ref, mn_ref, y_ref, o_ref, xn_ref = refs[4:]
            k0 = norm[2]
            xn_ref[...] = normmod(new, gn_ref[...], mn_ref[k0:k0 + 1, :], mn_ref[k0 + 1:k0 + 2, :]).astype(BF16)
        y_ref[...] = y.astype(BF16)
        o_ref[...] = new

    rows = pl.BlockSpec((T, N), lambda i: (i, 0))
    extra = [] if norm is None else [_full((1, N)), _modspec(lay, 6, N)]
    return pl.pallas_call(
        body, grid=(n_tiles,),
        in_specs=[pl.BlockSpec((T, K), lambda i: (i, 0)), _full((K, N)), rows, _modspec(lay, 6, N)] + extra,
        out_specs=[rows, rows] + ([] if norm is None else [rows]),
        out_shape=[jax.ShapeDtypeStruct((M, N), BF16), jax.ShapeDtypeStruct((M, N), F32)]
        + ([] if norm is None else [jax.ShapeDtypeStruct((M, N), BF16)]),
        compiler_params=_cparams(("parallel",)), name=name)(a, b, res, mod, *([] if norm is None else norm[:2]))


class Layout:
    def __init__(self, B, SEQ, CTX, D):
        assert CTX == T and SEQ % T == 0 and SEQ >= SWA_SPAN
        self.B, self.SEQ, self.CTX, self.D = B, SEQ, CTX, D
        self.tps = SEQ // T
        self.nxt = B * self.tps
        self.nt = self.nxt + B
        self.NX, self.R = B * SEQ, B * SEQ + B * CTX
        self.nq = self.tps + 1

    def mod_idx(self, i):
        return jnp.where(i < self.nxt, i // self.tps, self.B)

    def rope_idx(self, i):
        return jnp.where(i < self.nxt, i % self.tps, self.tps)

    def first_of_mod(self, i):
        return jnp.logical_or(jnp.logical_and(i < self.nxt, i % self.tps == 0), i == self.nxt)

    def qrow(self, b, qi):
        return jnp.where(qi < self.tps, b * self.tps + qi, self.nxt + b)


def _row(w, col=0):
    return pl.BlockSpec((T, w), lambda i: (i, col))


def _full(shape):
    nd = len(shape)
    return pl.BlockSpec(shape, lambda i: (0,) * nd)


def _modspec(lay, rows, D):
    return pl.BlockSpec((None, rows, D), lambda i: (lay.mod_idx(i), 0, 0))


def _ropespec(lay, w):
    return pl.BlockSpec((T, w), lambda i: (lay.rope_idx(i), 0))


def _xh_specs(lay, w):
    nxt = lay.nxt
    return [pl.BlockSpec((T, w), lambda i: (jnp.minimum(i, nxt - 1), 0)), pl.BlockSpec((T, w), lambda i: (jnp.maximum(i - nxt, 0), 0))]


def _xh_pick(lay, x_ref, h_ref):
    return jnp.where(pl.program_id(0) < lay.nxt, x_ref[...], h_ref[...])


def norm_mod_fwd(S, g, mod, k0, lay, n_tiles, name):
    D = S.shape[1]

    def body(s_ref, g_ref, mod_ref, o_ref):
        o_ref[...] = normmod(s_ref[...], g_ref[...], mod_ref[k0:k0 + 1, :], mod_ref[k0 + 1:k0 + 2, :]).astype(BF16)

    return pl.pallas_call(
        body, grid=(n_tiles,), in_specs=[_row(D), _full((1, D)), _modspec(lay, 6, D)], out_specs=_row(D),
        out_shape=jax.ShapeDtypeStruct((n_tiles * T, D), BF16), compiler_params=_cparams(("parallel",)), name=name)(S, g, mod)


def norm_mod_bwd(S, g, mod, k0, dxn, ds_in, lay, n_tiles, name, gate=None):
    D = S.shape[1]

    def body(*refs):
        s_ref, g_ref, mod_ref, dxn_ref, dsin_ref = refs[:5]
        i = pl.program_id(0)
        _, vjp = jax.vjp(normmod, s_ref[...], g_ref[...], mod_ref[k0:k0 + 1, :], mod_ref[k0 + 1:k0 + 2, :])
        dx, dg, dsh, dsc = vjp(dxn_ref[...])
        ds = dsin_ref[...] + dx
        if gate is None:
            ds_ref, dss_ref, dg_ref = refs[5:]
        else:
            y_ref, gmod_ref, ds_ref, dss_ref, dg_ref, dy_ref, dgate_ref = refs[5:]
            row = gate[2]
            dy_ref[...] = (gmod_ref[row:row + 1, :] * ds).astype(BF16)
            _acc(dgate_ref, jnp.sum(ds * y_ref[...], axis=0, keepdims=True), lay.first_of_mod(i))
        ds_ref[...] = ds
        _acc(dg_ref, dg, i == 0)
        _acc(dss_ref, jnp.concatenate([dsh, dsc], axis=0), lay.first_of_mod(i))

    R_ = n_tiles * T
    gated = gate is not None
    return pl.pallas_call(
        body, grid=(n_tiles,),
        in_specs=[_row(D), _full((1, D)), _modspec(lay, 6, D), _row(D), _row(D)] + ([_row(D), _modspec(lay, 6, D)] if gated else []),
        out_specs=[_row(D), _modspec(lay, 2, D), _full((1, D))] + ([_row(D), _modspec(lay, 1, D)] if gated else []),
        out_shape=[jax.ShapeDtypeStruct((R_, D), F32), jax.ShapeDtypeStruct((lay.B + 1, 2, D), F32),
                   jax.ShapeDtypeStruct((1, D), F32)]
        + ([jax.ShapeDtypeStruct((R_, D), BF16), jax.ShapeDtypeStruct((lay.B + 1, 1, D), F32)] if gated else []),
        compiler_params=_cparams(("arbitrary",)), name=name)(S, g, mod, dxn, ds_in, *(gate[:2] if gated else []))


def gate_bwd(dS, y, mod, gate_row, lay, n_tiles, name):
    D = dS.shape[1]

    def body(ds_ref, y_ref, mod_ref, dy_ref, dgate_ref):
        i = pl.program_id(0)
        ds = ds_ref[...]
        dy_ref[...] = (mod_ref[gate_row:gate_row + 1, :] * ds).astype(BF16)
        _acc(dgate_ref, jnp.sum(ds * y_ref[...], axis=0, keepdims=True), lay.first_of_mod(i))

    return pl.pallas_call(
        body, grid=(n_tiles,), in_specs=[_row(D), _row(D), _modspec(lay, 6, D)],
        out_specs=[_row(D), _modspec(lay, 1, D)],
        out_shape=[jax.ShapeDtypeStruct((n_tiles * T, D), BF16), jax.ShapeDtypeStruct((lay.B + 1, 1, D), F32)],
        compiler_params=_cparams(("arbitrary",)), name=name)(dS, y, mod)


def _swiglu(a, b):
    return a * jax.nn.sigmoid(a) * b


def _ffn_tiles(M, F):
    tn = _pick(F, 1408, 128)
    return _pick(M, (3 << 18) // tn, 128), tn


def ffn_in_act(x, w, name):
    M, D = x.shape
    F = w.shape[1] // 2
    tm, tn = _ffn_tiles(M, F)
    nj = F // tn

    def body(x_ref, wa_ref, wb_ref, act_ref, a_ref, b_ref):
        dims = (((1,), (0,)), ((), ()))
        a = lax.dot_general(x_ref[...], wa_ref[...], dims, preferred_element_type=F32)
        b = lax.dot_general(x_ref[...], wb_ref[...], dims, preferred_element_type=F32)
        act_ref[...] = _swiglu(a, b).astype(BF16)
        a_ref[...] = a.astype(BF16)
        b_ref[...] = b.astype(BF16)

    out = pl.BlockSpec((tm, tn), lambda i, j: (i, j))
    return pl.pallas_call(
        body, grid=(M // tm, nj),
        in_specs=[pl.BlockSpec((tm, D), lambda i, j: (i, 0)), pl.BlockSpec((D, tn), lambda i, j: (0, j)),
                  pl.BlockSpec((D, tn), lambda i, j: (0, j + nj))],
        out_specs=[out, out, out], out_shape=[jax.ShapeDtypeStruct((M, F), BF16)] * 3,
        compiler_params=_cparams(("parallel", "parallel")), name=name)(x, w, w)


def ffn_out_dx_act(df, wt, a, b, name):
    M, D = df.shape
    F = wt.shape[1]
    tm, tn = _ffn_tiles(M, F)

    def body(df_ref, w_ref, a_ref, b_ref, da_ref, db_ref):
        dact = lax.dot_general(df_ref[...], w_ref[...], (((1,), (0,)), ((), ())), preferred_element_type=F32)
        _, vjp = jax.vjp(_swiglu, a_ref[...].astype(F32), b_ref[...].astype(F32))
        da, db = vjp(dact)
        da_ref[...] = da.astype(BF16)
        db_ref[...] = db.astype(BF16)

    blk = pl.BlockSpec((tm, tn), lambda i, j: (i, j))
    return pl.pallas_call(
        body, grid=(M // tm, F // tn),
        in_specs=[pl.BlockSpec((tm, D), lambda i, j: (i, 0)), pl.BlockSpec((D, tn), lambda i, j: (0, j)), blk, blk],
        out_specs=[blk, blk], out_shape=[jax.ShapeDtypeStruct((M, F), BF16)] * 2,
        compiler_params=_cparams(("parallel", "parallel")), name=name)(df, wt, a, b)


def ffn_in_dx(da, db, w, name):
    M, F = da.shape
    D = w.shape[0]
    tm, tn = _pick(M, 512, 128), _pick(D, 512, 128)

    def body(da_ref, db_ref, wa_ref, wb_ref, o_ref):
        dims = (((1,), (1,)), ((), ()))
        o_ref[...] = (lax.dot_general(da_ref[...], wa_ref[...], dims, preferred_element_type=F32)
                      + lax.dot_general(db_ref[...], wb_ref[...], dims, preferred_element_type=F32))

    return pl.pallas_call(
        body, grid=(M // tm, D // tn),
        in_specs=[pl.BlockSpec((tm, F), lambda i, j: (i, 0)), pl.BlockSpec((tm, F), lambda i, j: (i, 0)),
                  pl.BlockSpec((tn, F), lambda i, j: (j, 0)), pl.BlockSpec((tn, F), lambda i, j: (j, 1))],
        out_specs=pl.BlockSpec((tm, tn), lambda i, j: (i, j)), out_shape=jax.ShapeDtypeStruct((M, D), F32),
        compiler_params=_cparams(("parallel", "parallel")), name=name)(da, db, w, w)


def loss_head(S, g, target, lay, name):
    D = S.shape[1]
    nxt = lay.nxt

    def tile_loss(x, gg, t):
        err = rms(x, gg) - t
        return 0.5 * jnp.sum(jnp.mean(err * err, axis=-1))

    def body(s_ref, g_ref, t_ref, loss_ref, ds_ref, dg_ref):
        i = pl.program_id(0)

        @pl.when(i < nxt)
        def _():
            val, vjp = jax.vjp(tile_loss, s_ref[...], g_ref[...], t_ref[...])
            dx, dg, _ = vjp(jnp.ones((), F32))
            ds_ref[...] = dx
            _acc(dg_ref, dg, i == 0)
            _acc(loss_ref, jnp.full((8, 128), val, F32), i == 0)

        @pl.when(i >= nxt)
        def _():
            ds_ref[...] = jnp.zeros((T, D), F32)

    return pl.pallas_call(
        body, grid=(lay.nt,),
        in_specs=[_row(D), _full((1, D)), pl.BlockSpec((T, D), lambda i: (jnp.minimum(i, nxt - 1), 0))],
        out_specs=[_full((8, 128)), _row(D), _full((1, D))],
        out_shape=[jax.ShapeDtypeStruct((8, 128), F32), jax.ShapeDtypeStruct((lay.R, D), F32),
                   jax.ShapeDtypeStruct((1, D), F32)],
        compiler_params=_cparams(("arbitrary",)), name=name)(S, g, target)


def _ab_prep(zkv, zq, zpe, c256, s256, c128, s128, gkv, gq, wk, wv, wqn, wqp):
    kvn = rms(zkv, gkv)
    kn, v = bdot(kvn, wk), bdot(kvn, wv)
    qn = rms(zq, gq)
    qnope, qpe = bdot(qn, wqn), rope(bdot(qn, wqp), c256, s256)
    kpe = rope(zpe, c128, s128)
    fold = _fold_matrix()
    qparts, kparts = [], []
    for h in range(MLA_HEADS):
        qparts += [qnope[:, 128 * h:128 * (h + 1)], bdot(qpe * _head_mask(qpe.shape, h), fold)]
        kparts += [kn[:, 128 * h:128 * (h + 1)], kpe]
    return jnp.concatenate(qparts, axis=1), jnp.concatenate(kparts, axis=1), v


def _ab_prep_specs(lay):
    return [_row(256, 0), _row(256, 1), _row(128, 12), _ropespec(lay, 256), _ropespec(lay, 256), _ropespec(lay, 128),
            _ropespec(lay, 128), _full((1, 256)), _full((1, 256)), _full((256, 512)), _full((256, 512)),
            _full((256, 512)), _full((256, 256))]


def ab_prep_fwd(z, tabc, tabs, gkv, gq, wk, wv, wqn, wqp, lay, name):
    def body(*refs):
        ins, (q_ref, k_ref, v_ref) = refs[:13], refs[13:]
        q, k, v = _ab_prep(*[r[...].astype(F32) for r in ins])
        q_ref[...] = q.astype(BF16)
        k_ref[...] = k.astype(BF16)
        v_ref[...] = v.astype(BF16)

    R = lay.R
    return pl.pallas_call(
        body, grid=(lay.nt,), in_specs=_ab_prep_specs(lay), out_specs=[_row(1024), _row(1024), _row(512)],
        out_shape=[jax.ShapeDtypeStruct((R, 1024), BF16), jax.ShapeDtypeStruct((R, 1024), BF16),
                   jax.ShapeDtypeStruct((R, 512), BF16)],
        compiler_params=_cparams(("parallel",)), name=name)(z, z, z, tabc, tabs, tabc, tabs, gkv, gq, wk, wv, wqn, wqp)


_MLA_SCALE = (MLA_NOPE + MLA_ROPE) ** -0.5


def _mla_x(q, kx, vx, kh, vh):
    sx, sh = bdot_nt(q, kx) * _MLA_SCALE, bdot_nt(q, kh) * _MLA_SCALE
    m = lax.stop_gradient(jnp.maximum(jnp.max(sx, axis=-1, keepdims=True), jnp.max(sh, axis=-1, keepdims=True)))
    ex, eh = jnp.exp(sx - m), jnp.exp(sh - m)
    inv = 1.0 / (jnp.sum(ex, axis=-1, keepdims=True) + jnp.sum(eh, axis=-1, keepdims=True))
    return bdot(ex * inv, vx) + bdot(eh * inv, vh)


def _mla_h(q, kh, vh):
    sh = bdot_nt(q, kh) * _MLA_SCALE
    eh = jnp.exp(sh - lax.stop_gradient(jnp.max(sh, axis=-1, keepdims=True)))
    return bdot(eh * (1.0 / jnp.sum(eh, axis=-1, keepdims=True)), vh)


def _mla_specs(lay):
    nxt, SEQ = lay.nxt, lay.SEQ
    return [pl.BlockSpec((T, 256), lambda b, h, qi: (lay.qrow(b, qi), h)),
            pl.BlockSpec((SEQ, 256), lambda b, h, qi: (b, h)), pl.BlockSpec((SEQ, 128), lambda b, h, qi: (b, h)),
            pl.BlockSpec((T, 256), lambda b, h, qi: (nxt + b, h)), pl.BlockSpec((T, 128), lambda b, h, qi: (nxt + b, h))]


def mla_fwd(q, k, v, lay, name):
    tps = lay.tps

    def body(q_ref, kx_ref, vx_ref, kh_ref, vh_ref, o_ref):
        qi = pl.program_id(2)
        f = lambda r: r[...]

        @pl.when(qi < tps)
        def _():
            o_ref[...] = _mla_x(f(q_ref), f(kx_ref), f(vx_ref), f(kh_ref), f(vh_ref)).astype(BF16)

        @pl.when(qi == tps)
        def _():
            o_ref[...] = _mla_h(f(q_ref), f(kh_ref), f(vh_ref)).astype(BF16)

    return pl.pallas_call(
        body, grid=(lay.B, MLA_HEADS, lay.nq), in_specs=_mla_specs(lay),
        out_specs=pl.BlockSpec((T, 128), lambda b, h, qi: (lay.qrow(b, qi), h)),
        out_shape=jax.ShapeDtypeStruct((lay.R, 512), BF16),
        compiler_params=_cparams(("parallel", "parallel", "arbitrary")), name=name)(q, k, v, k, v)


def mla_bwd(q, k, v, dmerged, lay, name):
    tps, SEQ, B = lay.tps, lay.SEQ, lay.B

    def body(q_ref, kx_ref, vx_ref, kh_ref, vh_ref, do_ref, dq_ref, dkx_ref, dkh_ref, dvx_ref, dvh_ref):
        qi = pl.program_id(2)
        f = lambda r: r[...].astype(F32)

        @pl.when(qi < tps)
        def _():
            _, vjp = jax.vjp(_mla_x, f(q_ref), f(kx_ref), f(vx_ref), f(kh_ref), f(vh_ref))
            dq, dkx, dvx, dkh, dvh = vjp(do_ref[...])
            dq_ref[...] = dq
            _acc(dkx_ref, dkx, qi == 0)
            _acc(dvx_ref, dvx, qi == 0)
            _acc(dkh_ref, dkh, qi == 0)
            _acc(dvh_ref, dvh, qi == 0)

        @pl.when(qi == tps)
        def _():
            _, vjp = jax.vjp(_mla_h, f(q_ref), f(kh_ref), f(vh_ref))
            dq, dkh, dvh = vjp(do_ref[...])
            dq_ref[...] = dq
            dkh_ref[...] += dkh
            dvh_ref[...] += dvh

    return pl.pallas_call(
        body, grid=(B, MLA_HEADS, lay.nq),
        in_specs=_mla_specs(lay) + [pl.BlockSpec((T, 128), lambda b, h, qi: (lay.qrow(b, qi), h))],
        out_specs=[pl.BlockSpec((T, 256), lambda b, h, qi: (lay.qrow(b, qi), h)),
                   pl.BlockSpec((SEQ, 256), lambda b, h, qi: (b, h)), pl.BlockSpec((T, 256), lambda b, h, qi: (b, h)),
                   pl.BlockSpec((SEQ, 128), lambda b, h, qi: (b, h)), pl.BlockSpec((T, 128), lambda b, h, qi: (b, h))],
        out_shape=[jax.ShapeDtypeStruct((lay.R, 1024), F32), jax.ShapeDtypeStruct((lay.NX, 1024), F32),
                   jax.ShapeDtypeStruct((B * T, 1024), F32), jax.ShapeDtypeStruct((lay.NX, 512), F32),
                   jax.ShapeDtypeStruct((B * T, 512), F32)],
        compiler_params=_cparams(("parallel", "parallel", "arbitrary")), name=name)(q, k, v, k, v, dmerged)


def _cmlp_piece(zu, zv, g, ws, bs):
    u, v = jax.nn.gelu(zu), jax.nn.gelu(zv)
    v = v * lax.rsqrt(jnp.mean(v * v, axis=-1, keepdims=True) + EPS) * g
    return u * (bdot(ws, v) + bs)


def _pieces():
    return [(c, g) for c in range(T // CMLP_CHUNK) for g in range(CMLP_GROUPS)]


def cmlp_merge_fwd(z, o, gvn, ws, bs, lay, name):
    def body(zu_ref, zv_ref, o_ref, g_ref, ws_ref, bs_ref, m_ref):
        m_ref[:, 0:512] = o_ref[...]
        for c, g in _pieces():
            rows, cols = slice(128 * c, 128 * (c + 1)), slice(128 * g, 128 * (g + 1))
            piece = _cmlp_piece(zu_ref[rows, cols], zv_ref[rows, cols], g_ref[:, cols], ws_ref[g], bs_ref[g])
            m_ref[rows, 512 + 128 * g:512 + 128 * (g + 1)] = piece.astype(BF16)

    return pl.pallas_call(
        body, grid=(lay.nt,),
        in_specs=[_row(512, 1), _row(512, 2), _row(512), _full((1, 512)), _full((4, 128, 128)), _full((4, 128, 1))],
        out_specs=_row(1024), out_shape=jax.ShapeDtypeStruct((lay.R, 1024), BF16),
        compiler_params=_cparams(("parallel",)), name=name)(z, z, o, gvn, ws, bs)


def ab_rows_bwd(z, tabc, tabs, dq, dkx, dkh, dvx, dvh, dmerged, gkv, gq, wk, wv, wqn, wqp, gvn, ws, bs, lay, name):
    def body(*refs):
        prep_in = refs[:3] + refs[5:9] + refs[11:17]
        zu_ref, zv_ref = refs[3:5]
        dq_ref, dcm_ref = refs[9:11]
        gvn_ref, ws_ref, bs_ref = refs[17:20]
        dkx_ref, dkh_ref, dvx_ref, dvh_ref = refs[20:24]
        dz_ref, dgkv_ref, dgq_ref, dwk_ref, dwv_ref, dwqn_ref, dwqp_ref, dgvn_ref, dws_ref, dbs_ref = refs[24:]
        first = pl.program_id(0) == 0
        _, vjp = jax.vjp(_ab_prep, *[r[...].astype(F32) for r in prep_in])
        d = vjp((dq_ref[...], _xh_pick(lay, dkx_ref, dkh_ref), _xh_pick(lay, dvx_ref, dvh_ref)))
        dz_ref[:, 0:256] = d[0].astype(BF16)
        dz_ref[:, 256:512] = d[1].astype(BF16)
        dz_ref[:, 1536:1664] = d[2].astype(BF16)
        for ref, val in zip((dgkv_ref, dgq_ref, dwk_ref, dwv_ref, dwqn_ref, dwqp_ref), d[7:]):
            _acc(ref, val, first)
        dws = [0.0] * CMLP_GROUPS
        dbs = [0.0] * CMLP_GROUPS
        dgv = [0.0] * CMLP_GROUPS
        for c, g in _pieces():
            rows, cols = slice(128 * c, 128 * (c + 1)), slice(128 * g, 128 * (g + 1))
            _, vjp = jax.vjp(_cmlp_piece, zu_ref[rows, cols], zv_ref[rows, cols], gvn_ref[:, cols], ws_ref[g], bs_ref[g])
            dzu, dzv, dg_, dws_, dbs_ = vjp(dcm_ref[rows, cols])
            dz_ref[rows, 512 + 128 * g:512 + 128 * (g + 1)] = dzu.astype(BF16)
            dz_ref[rows, 1024 + 128 * g:1024 + 128 * (g + 1)] = dzv.astype(BF16)
            dws[g], dbs[g], dgv[g] = dws[g] + dws_, dbs[g] + dbs_, dgv[g] + dg_
        _acc(dgvn_ref, jnp.concatenate(dgv, axis=1), first)
        _acc(dws_ref, jnp.stack(dws), first)
        _acc(dbs_ref, jnp.stack(dbs), first)

    acc_shapes = [(1, 256), (1, 256), (256, 512), (256, 512), (256, 512), (256, 256), (1, 512), (4, 128, 128), (4, 128, 1)]
    return pl.pallas_call(
        body, grid=(lay.nt,),
        in_specs=_ab_prep_specs(lay)[:3] + [_row(512, 1), _row(512, 2)] + _ab_prep_specs(lay)[3:7]
        + [_row(1024), _row(512, 1)] + _ab_prep_specs(lay)[7:]
        + [_full((1, 512)), _full((4, 128, 128)), _full((4, 128, 1))] + _xh_specs(lay, 1024) + _xh_specs(lay, 512),
        out_specs=[_row(AB_IN_P)] + [_full(s) for s in acc_shapes],
        out_shape=[jax.ShapeDtypeStruct((lay.R, AB_IN_P), BF16)] + [jax.ShapeDtypeStruct(s, F32) for s in acc_shapes],
        compiler_params=_cparams(("arbitrary",)), name=name)(
            z, z, z, z, z, tabc, tabs, tabc, tabs, dq, dmerged, gkv, gq, wk, wv, wqn, wqp, gvn, ws, bs, dkx, dkh, dvx, dvh)


def _cd_prep(zrk, zrq, zsk, zsq0, zsq1, zsv, c256, s256, c128, s128):
    rk = rope(zrk * (RET_QK ** -0.5), c256, s256)
    rq = rope(zrq, c256, s256)
    sk = rope(zsk, c128, s128)
    sq0, sq1 = rope(zsq0, c256, s256), rope(zsq1, c256, s256)
    e0, e1 = _expand_matrix(0), _expand_matrix(1)
    return rq, rk, sq0, sq1, bdot(sk, e0), bdot(sk, e1), bdot(zsv, e0), bdot(zsv, e1)


def _cd_prep_specs(lay):
    return [_row(256, 0), _row(256, 4), _row(128, 6), _row(256, 7), _row(256, 8), _row(128, 7),
            _ropespec(lay, 256), _ropespec(lay, 256), _ropespec(lay, 128), _ropespec(lay, 128)]


def cd_prep_fwd(z, tabc, tabs, lay, name):
    def body(*refs):
        ins, (rq_ref, rk_ref, sq_ref, ke_ref, ve_ref) = refs[:10], refs[10:]
        rq, rk, sq0, sq1, k0, k1, v0, v1 = _cd_prep(*[r[...] for r in ins])
        rq_ref[...] = rq.astype(BF16)
        rk_ref[...] = rk.astype(BF16)
        for ref, (a, b) in ((sq_ref, (sq0, sq1)), (ke_ref, (k0, k1)), (ve_ref, (v0, v1))):
            ref[:, 0:256] = a.astype(BF16)
            ref[:, 256:512] = b.astype(BF16)

    R = lay.R
    return pl.pallas_call(
        body, grid=(lay.nt,), in_specs=_cd_prep_specs(lay),
        out_specs=[_row(256), _row(256), _row(512), _row(512), _row(512)],
        out_shape=[jax.ShapeDtypeStruct((R, w), BF16) for w in (256, 256, 512, 512, 512)],
        compiler_params=_cparams(("parallel",)), name=name)(z, z, z, z, z, z, tabc, tabs, tabc, tabs)


def _ret_sample(h, qs, ks, vs, df, db):
    lgf, lgb = log_sigmoid(df), log_sigmoid(db)
    idx = lax.broadcasted_iota(jnp.int32, (T, 1), 0).astype(F32)
    diff = idx - lax.broadcasted_iota(jnp.int32, (1, T), 1).astype(F32)
    intra = (jnp.where(diff >= 0, jnp.exp(lgf * jnp.maximum(diff, 0.0)), 0.0)
             + jnp.where(diff <= 0, jnp.exp(lgb * jnp.maximum(-diff, 0.0)), 0.0))
    qdf, kdf, cdf = jnp.exp(lgf * (idx + 1.0)), jnp.exp(lgf * (T - 1.0 - idx)), jnp.exp(lgf * T)
    qdb, kdb, cdb = jnp.exp(lgb * (T - idx)), jnp.exp(lgb * idx), jnp.exp(lgb * T)
    mask = _head_mask(qs[0].shape, h)
    qs = [q * mask for q in qs]
    ys = [bdot(bdot_nt(q, k) * intra, v) for q, k, v in zip(qs, ks, vs)]
    n = len(qs)
    state = bdot_tn(ks[0] * kdf, vs[0])
    for i in range(1, n):
        ys[i] = ys[i] + bdot(qs[i] * qdf, state)
        if i + 1 < n:
            state = state * cdf + bdot_tn(ks[i] * kdf, vs[i])
    state = bdot_tn(ks[0] * kdb, vs[0])
    for i in range(n - 1, 0, -1):
        ys[i] = ys[i] + bdot(qs[i] * qdb, state)
        if i > 1:
            state = state * cdb + bdot_tn(ks[i] * kdb, vs[i])
    return ys


def _ret_specs(lay):
    nxt, SEQ = lay.nxt, lay.SEQ
    xs = lambda w, col: pl.BlockSpec((SEQ, w), lambda b, h: (b, col(h)))
    hs = lambda w, col: pl.BlockSpec((T, w), lambda b, h: (nxt + b, col(h)))
    zero, head = (lambda h: 0), (lambda h: 2 + h)
    dec = pl.BlockSpec((None, 8, 128), lambda b, h: (h, 0, 0))
    return [xs(256, zero), hs(256, zero), xs(256, zero), hs(256, zero), xs(128, head), hs(128, head), dec, dec]


def _ret_tiles(x_ref, h_ref, tps, cast=None):
    tiles = [h_ref[...]] + [x_ref[i * T:(i + 1) * T, :] for i in range(tps)]
    return [t.astype(cast) for t in tiles] if cast is not None else tiles


def ret_fwd(rq, rk, z, decf, decb, lay, name):
    tps, SEQ = lay.tps, lay.SEQ

    def body(qx_ref, qh_ref, kx_ref, kh_ref, vx_ref, vh_ref, df_ref, db_ref, yx_ref, yh_ref):
        ys = _ret_sample(pl.program_id(1), _ret_tiles(qx_ref, qh_ref, tps), _ret_tiles(kx_ref, kh_ref, tps),
                         _ret_tiles(vx_ref, vh_ref, tps), df_ref[0:1, 0:1], db_ref[0:1, 0:1])
        yh_ref[...] = ys[0]
        for i in range(tps):
            yx_ref[i * T:(i + 1) * T, :] = ys[i + 1]

    return pl.pallas_call(
        body, grid=(lay.B, RET_HEADS), in_specs=_ret_specs(lay),
        out_specs=[pl.BlockSpec((SEQ, 128), lambda b, h: (b, h)), pl.BlockSpec((T, 128), lambda b, h: (b, h))],
        out_shape=[jax.ShapeDtypeStruct((lay.NX, 512), F32), jax.ShapeDtypeStruct((lay.B * T, 512), F32)],
        compiler_params=_cparams(("parallel", "arbitrary")), name=name)(rq, rq, rk, rk, z, z, decf, decb)


def ret_bwd(rq, rk, z, decf, decb, dy, lay, name):
    tps, SEQ, B = lay.tps, lay.SEQ, lay.B
    nxt = lay.nxt

    def body(qx_ref, qh_ref, kx_ref, kh_ref, vx_ref, vh_ref, df_ref, db_ref, dyx_ref, dyh_ref,
             dqx_ref, dqh_ref, dkx_ref, dkh_ref, dvx_ref, dvh_ref, ddf_ref, ddb_ref):
        h = pl.program_id(1)
        _, vjp = jax.vjp(functools.partial(_ret_sample, h), _ret_tiles(qx_ref, qh_ref, tps, F32),
                         _ret_tiles(kx_ref, kh_ref, tps, F32), _ret_tiles(vx_ref, vh_ref, tps), df_ref[0:1, 0:1],
                         db_ref[0:1, 0:1])
        dqs, dks, dvs, ddf, ddb = vjp(_ret_tiles(dyx_ref, dyh_ref, tps))
        first = h == 0
        _acc(dqh_ref, dqs[0], first)
        _acc(dkh_ref, dks[0], first)
        dvh_ref[...] = dvs[0]
        for i in range(tps):
            rows = slice(i * T, (i + 1) * T)
            _acc(dqx_ref.at[rows], dqs[i + 1], first)
            _acc(dkx_ref.at[rows], dks[i + 1], first)
            dvx_ref[rows, :] = dvs[i + 1]
        @pl.when(jnp.logical_and(pl.program_id(0) == 0, first))
        def _():
            ddf_ref[...] = jnp.zeros(ddf_ref.shape, F32)
            ddb_ref[...] = jnp.zeros(ddb_ref.shape, F32)

        ddf_ref[h] += jnp.broadcast_to(ddf, (8, 128))
        ddb_ref[h] += jnp.broadcast_to(ddb, (8, 128))

    acc_x, acc_h = pl.BlockSpec((SEQ, 256), lambda b, h: (b, 0)), pl.BlockSpec((T, 256), lambda b, h: (b, 0))
    head_x, head_h = pl.BlockSpec((SEQ, 128), lambda b, h: (b, h)), pl.BlockSpec((T, 128), lambda b, h: (b, h))
    dec = pl.BlockSpec((RET_HEADS, 8, 128), lambda b, h: (0, 0, 0))
    return pl.pallas_call(
        body, grid=(B, RET_HEADS),
        in_specs=_ret_specs(lay) + [head_x, pl.BlockSpec((T, 128), lambda b, h: (nxt + b, h))],
        out_specs=[acc_x, acc_h, acc_x, acc_h, head_x, head_h, dec, dec],
        out_shape=[jax.ShapeDtypeStruct((lay.NX, 256), F32), jax.ShapeDtypeStruct((B * T, 256), F32),
                   jax.ShapeDtypeStruct((lay.NX, 256), F32), jax.ShapeDtypeStruct((B * T, 256), F32),
                   jax.ShapeDtypeStruct((lay.NX, 512), F32), jax.ShapeDtypeStruct((B * T, 512), F32),
                   jax.ShapeDtypeStruct((RET_HEADS, 8, 128), F32), jax.ShapeDtypeStruct((RET_HEADS, 8, 128), F32)],
        compiler_params=_cparams(("arbitrary", "arbitrary")), name=name)(rq, rq, rk, rk, z, z, decf, decb, dy, dy)


_SWA_SCALE = SWA_HEAD_DIM ** -0.5


def _swa_head(qh, sw, kh, vw, vh, sink):
    sh = bdot_nt(qh, kh) * _SWA_SCALE
    m = jnp.maximum(jnp.max(sh, axis=-1, keepdims=True), sink)
    if sw is not None:
        m = jnp.maximum(m, jnp.max(sw, axis=-1, keepdims=True))
    m = lax.stop_gradient(m)
    eh, es = jnp.exp(sh - m), jnp.exp(sink - m)
    tot = jnp.sum(eh, axis=-1, keepdims=True) + es
    if sw is None:
        return bdot(eh * (1.0 / tot), vh)
    ew = jnp.exp(sw - m)
    inv = 1.0 / (tot + jnp.sum(ew, axis=-1, keepdims=True))
    return bdot(ew * inv, vw) + bdot(eh * inv, vh)


def _swa_x(t0, kpos0, sq, kw, vw, kh, vh, *sinks):
    t = t0 + lax.broadcasted_iota(jnp.int32, (T, 1), 0)
    pos = kpos0 + lax.broadcasted_iota(jnp.int32, (1, SWA_SPAN), 1)
    band = jnp.abs(t - pos) <= SWA_WINDOW
    out = 0.0
    for i in range(SWA_GROUPS):
        mi = _head_mask(sq.shape, i)
        qh = sq * mi
        sw = jnp.where(band, bdot_nt(qh, kw) * _SWA_SCALE, NEG_INF)
        out = out + _swa_head(qh, sw, kh, vw, vh, sinks[i]) * mi
    return out


def _swa_h(sq, kh, vh, *sinks):
    out = 0.0
    for i in range(SWA_GROUPS):
        mi = _head_mask(sq.shape, i)
        out = out + _swa_head(sq * mi, None, kh, None, vh, sinks[i]) * mi
    return out


def _swa_specs(lay):
    nxt, SEQ = lay.nxt, lay.SEQ
    return [pl.BlockSpec((T, 256), lambda g, b, qi: (lay.qrow(b, qi), g)),
            pl.BlockSpec((SEQ, 256), lambda g, b, qi: (b, g)), pl.BlockSpec((SEQ, 256), lambda g, b, qi: (b, g)),
            pl.BlockSpec((T, 256), lambda g, b, qi: (nxt + b, g)), pl.BlockSpec((T, 256), lambda g, b, qi: (nxt + b, g)),
            pl.BlockSpec((None, 4, 8, 128), lambda g, b, qi: (g, 0, 0, 0))]


def _swa_start(qi, SEQ):
    return pl.multiple_of(jnp.clip(qi * T - SWA_WINDOW, 0, SEQ - SWA_SPAN), SWA_WINDOW)


def swa_fwd(sq, kexp, vexp, sink, lay, name):
    tps, SEQ = lay.tps, lay.SEQ

    def body(sq_ref, kx_ref, vx_ref, kh_ref, vh_ref, sink_ref, o_ref):
        qi = pl.program_id(2)
        f = lambda r: r[...]
        sinks = [sink_ref[i][0:1, 0:1] for i in range(SWA_GROUPS)]

        @pl.when(qi < tps)
        def _():
            k0 = _swa_start(qi, SEQ)
            kw, vw = kx_ref[pl.ds(k0, SWA_SPAN), :], vx_ref[pl.ds(k0, SWA_SPAN), :]
            o_ref[...] = _swa_x(qi * T, k0, f(sq_ref), kw, vw, f(kh_ref), f(vh_ref), *sinks).astype(BF16)

        @pl.when(qi == tps)
        def _():
            o_ref[...] = _swa_h(f(sq_ref), f(kh_ref), f(vh_ref), *sinks).astype(BF16)

    return pl.pallas_call(
        body, grid=(SWA_KV_HEADS, lay.B, lay.nq), in_specs=_swa_specs(lay),
        out_specs=pl.BlockSpec((T, 256), lambda g, b, qi: (lay.qrow(b, qi), g)),
        out_shape=jax.ShapeDtypeStruct((lay.R, 512), BF16),
        compiler_params=_cparams(("parallel", "parallel", "arbitrary")), name=name)(sq, kexp, vexp, kexp, vexp, sink)


def swa_bwd(sq, kexp, vexp, sink, dmerged, lay, name):
    tps, SEQ, B = lay.tps, lay.SEQ, lay.B

    def body(sq_ref, kx_ref, vx_ref, kh_ref, vh_ref, sink_ref, do_ref, dsq_ref, dkx_ref, dkh_ref, dvx_ref, dvh_ref, dsink_ref):
        b, qi = pl.program_id(1), pl.program_id(2)
        f = lambda r: r[...].astype(F32)
        sinks = [sink_ref[i][0:1, 0:1] for i in range(SWA_GROUPS)]
        very_first = jnp.logical_and(b == 0, qi == 0)

        def acc_sink(ds):
            for i in range(SWA_GROUPS):
                _acc(dsink_ref.at[i], jnp.broadcast_to(ds[i], (8, 128)), very_first)

        @pl.when(qi == 0)
        def _():
            for ref in (dkx_ref, dkh_ref, dvx_ref, dvh_ref):
                ref[...] = jnp.zeros(ref.shape, F32)

        @pl.when(qi < tps)
        def _():
            k0 = _swa_start(qi, SEQ)
            win = pl.ds(k0, SWA_SPAN)
            kw, vw = kx_ref[win, :].astype(F32), vx_ref[win, :].astype(F32)
            _, vjp = jax.vjp(functools.partial(_swa_x, qi * T, k0), f(sq_ref), kw, vw, f(kh_ref), f(vh_ref), *sinks)
            d = vjp(do_ref[...])
            dsq_ref[...] = d[0]
            dkx_ref[win, :] += d[1]
            dvx_ref[win, :] += d[2]
            dkh_ref[...] += d[3]
            dvh_ref[...] += d[4]
            acc_sink(d[5:9])

        @pl.when(qi == tps)
        def _():
            _, vjp = jax.vjp(_swa_h, f(sq_ref), f(kh_ref), f(vh_ref), *sinks)
            d = vjp(do_ref[...])
            dsq_ref[...] = d[0]
            dkh_ref[...] += d[1]
            dvh_ref[...] += d[2]
            acc_sink(d[3:7])

    xs = pl.BlockSpec((SEQ, 256), lambda g, b, qi: (b, g))
    hs = pl.BlockSpec((T, 256), lambda g, b, qi: (b, g))
    return pl.pallas_call(
        body, grid=(SWA_KV_HEADS, B, lay.nq),
        in_specs=_swa_specs(lay) + [pl.BlockSpec((T, 256), lambda g, b, qi: (lay.qrow(b, qi), 2 + g))],
        out_specs=[pl.BlockSpec((T, 256), lambda g, b, qi: (lay.qrow(b, qi), g)), xs, hs, xs, hs,
                   pl.BlockSpec((None, 4, 8, 128), lambda g, b, qi: (g, 0, 0, 0))],
        out_shape=[jax.ShapeDtypeStruct((lay.R, 512), F32), jax.ShapeDtypeStruct((lay.NX, 512), F32),
                   jax.ShapeDtypeStruct((B * T, 512), F32), jax.ShapeDtypeStruct((lay.NX, 512), F32),
                   jax.ShapeDtypeStruct((B * T, 512), F32), jax.ShapeDtypeStruct((SWA_KV_HEADS, 4, 8, 128), F32)],
        compiler_params=_cparams(("arbitrary", "arbitrary", "arbitrary")), name=name)(
            sq, kexp, vexp, kexp, vexp, sink, dmerged)


def _cd_merge_piece(y, rg, g):
    return (y * lax.rsqrt(jnp.mean(y * y, axis=-1, keepdims=True) + EPS) * g) * (rg * jax.nn.sigmoid(rg))


def cd_merge_fwd(y, z, o, gn, lay, name):
    def body(yx_ref, yh_ref, rga_ref, rgb_ref, o_ref, g_ref, m_ref):
        y = _xh_pick(lay, yx_ref, yh_ref)
        for h in range(RET_HEADS):
            cols = slice(128 * h, 128 * (h + 1))
            rg_ref, rcols = (rga_ref, cols) if h < 2 else (rgb_ref, slice(128 * (h - 2), 128 * (h - 1)))
            m_ref[:, cols] = _cd_merge_piece(y[:, cols], rg_ref[:, rcols], g_ref[:, cols]).astype(BF16)
        m_ref[:, 512:1024] = o_ref[...]

    return pl.pallas_call(
        body, grid=(lay.nt,), in_specs=_xh_specs(lay, 512) + [_row(256, 5), _row(256, 6), _row(512), _full((1, 512))],
        out_specs=_row(1024), out_shape=jax.ShapeDtypeStruct((lay.R, 1024), BF16),
        compiler_params=_cparams(("parallel",)), name=name)(*y, z, z, o, gn)


def cd_merge_bwd(y, z, gn, dmerged, lay, name):
    def body(yx_ref, yh_ref, rga_ref, rgb_ref, g_ref, dm_ref, dy_ref, drg_ref, dg_ref):
        first = pl.program_id(0) == 0
        y = _xh_pick(lay, yx_ref, yh_ref)
        dgs = []
        for h in range(RET_HEADS):
            cols = slice(128 * h, 128 * (h + 1))
            rg_ref, rcols = (rga_ref, cols) if h < 2 else (rgb_ref, slice(128 * (h - 2), 128 * (h - 1)))
            _, vjp = jax.vjp(_cd_merge_piece, y[:, cols], rg_ref[:, rcols], g_ref[:, cols])
            dy, drg, dg = vjp(dm_ref[:, cols])
            dy_ref[:, cols] = dy
            drg_ref[:, cols] = drg
            dgs.append(dg)
        _acc(dg_ref, jnp.concatenate(dgs, axis=1), first)

    return pl.pallas_call(
        body, grid=(lay.nt,), in_specs=_xh_specs(lay, 512) + [_row(256, 5), _row(256, 6), _full((1, 512)), _row(512, 0)],
        out_specs=[_row(512), _row(512), _full((1, 512))],
        out_shape=[jax.ShapeDtypeStruct((lay.R, 512), F32), jax.ShapeDtypeStruct((lay.R, 512), F32),
                   jax.ShapeDtypeStruct((1, 512), F32)],
        compiler_params=_cparams(("arbitrary",)), name=name)(*y, z, z, gn, dmerged)


def cd_rows_bwd(z, tabc, tabs, dsq, drg, drq, drk, dke, dve, drv, lay, name):
    def body(*refs):
        ins = refs[:10]
        dsq_ref, drg_ref = refs[10:12]
        drq, drk, dke, dve, drv = (_xh_pick(lay, refs[12 + 2 * n], refs[13 + 2 * n]) for n in range(5))
        dz_ref = refs[22]
        _, vjp = jax.vjp(_cd_prep, *[r[...] for r in ins])
        cts = (drq, drk, dsq_ref[:, 0:256], dsq_ref[:, 256:512], dke[:, 0:256], dke[:, 256:512],
               dve[:, 0:256], dve[:, 256:512])
        dzrk, dzrq, dzsk, dzsq0, dzsq1, dzsv = vjp(cts)[:6]
        dz_ref[:, 0:256] = dzrk.astype(BF16)
        dz_ref[:, 256:768] = drv.astype(BF16)
        dz_ref[:, 768:896] = dzsk.astype(BF16)
        dz_ref[:, 896:1024] = dzsv.astype(BF16)
        dz_ref[:, 1024:1280] = dzrq.astype(BF16)
        dz_ref[:, 1280:1792] = drg_ref[...].astype(BF16)
        dz_ref[:, 1792:2048] = dzsq0.astype(BF16)
        dz_ref[:, 2048:2304] = dzsq1.astype(BF16)

    return pl.pallas_call(
        body, grid=(lay.nt,),
        in_specs=_cd_prep_specs(lay) + [_row(512), _row(512)] + _xh_specs(lay, 256) + _xh_specs(lay, 256)
        + _xh_specs(lay, 512) + _xh_specs(lay, 512) + _xh_specs(lay, 512),
        out_specs=_row(2304), out_shape=jax.ShapeDtypeStruct((lay.R, 2304), BF16),
        compiler_params=_cparams(("parallel",)), name=name)(
            z, z, z, z, z, z, tabc, tabs, tabc, tabs, dsq, drg, *drq, *drk, *dke, *dve, *drv)


def _pos():
    return lax.axis_index("x"), lax.axis_index("y"), lax.axis_index("c")


def _flip(v, bit):
    return 1 - v if bit else v


def _comm_call(name, body, ins, out_shapes, n_remote, n_local, aliases=None):
    return pl.pallas_call(
        body, in_specs=[ANY] * len(ins), out_specs=[ANY] * len(out_shapes), out_shape=out_shapes,
        scratch_shapes=[pltpu.SemaphoreType.DMA((n_remote,)), pltpu.SemaphoreType.DMA((n_remote,)),
                        pltpu.SemaphoreType.DMA((n_local,))],
        input_output_aliases=aliases or {}, name=name)(*ins)


def gather8(arr, name):
    def body(a_ref, o_ref, ssem, rsem, lsem):
        x, y, c = _pos()
        me = 4 * x + 2 * y + c
        loc = pltpu.make_async_copy(a_ref, o_ref.at[me], lsem.at[0])
        loc.start()
        cps = []
        for m in range(1, 8):
            peer = (_flip(x, m & 4), _flip(y, m & 2), _flip(c, m & 1))
            cps.append(pltpu.make_async_remote_copy(a_ref, o_ref.at[me], ssem.at[m - 1], rsem.at[m - 1],
                                                    device_id=peer, device_id_type=MESH))
            cps[-1].start()
        for cp in cps:
            cp.wait()
        loc.wait()

    return _comm_call(name, body, [arr], [jax.ShapeDtypeStruct((8,) + arr.shape, arr.dtype)], 7, 1)[0]


def gather_chips(arr, name):
    def body(a_ref, o_ref, ssem, rsem, lsem):
        x, y, c = _pos()
        k = 2 * x + y
        loc = pltpu.make_async_copy(a_ref, o_ref.at[k], lsem.at[0])
        loc.start()
        cps = []
        for m in range(1, 4):
            peer = (_flip(x, m & 2), _flip(y, m & 1), c)
            cps.append(pltpu.make_async_remote_copy(a_ref, o_ref.at[k], ssem.at[m - 1], rsem.at[m - 1],
                                                    device_id=peer, device_id_type=MESH))
            cps[-1].start()
        for cp in cps:
            cp.wait()
        loc.wait()

    return _comm_call(name, body, [arr], [jax.ShapeDtypeStruct((4,) + arr.shape, arr.dtype)], 3, 1)[0]


def gather_weights(arrs, name):
    n = len(arrs)

    def body(*refs):
        a_refs, o_refs, (isend, irecv, _) = refs[:n], refs[n:2 * n], refs[2 * n:]
        x, y, c = _pos()
        k = 2 * x + y
        sib = (x, y, 1 - c)
        chips = [(m, (_flip(x, m & 2), _flip(y, m & 1)), 2 * _flip(x, m & 2) + _flip(y, m & 1)) for m in range(1, 4)]
        waits = []
        for w, (a, o) in enumerate(zip(a_refs, o_refs)):
            H = a.shape[0] // 2
            own = pl.ds(c * H, H)
            first = [pltpu.make_async_remote_copy(a.at[own], o.at[k, own], isend.at[7 * w + m - 1], irecv.at[7 * w + m - 1],
                                                  device_id=(*chip, c), device_id_type=MESH) for m, chip, _ in chips]
            first.append(pltpu.make_async_remote_copy(a, o.at[k], isend.at[7 * w + 6], irecv.at[7 * w + 6], device_id=sib,
                                                      device_id_type=MESH))
            for cp in first:
                cp.start()
            waits.append((first, H, own, a, o, w))
        for first, H, own, a, o, w in waits:
            first[3].wait_recv()
            passed = []
            for m, chip, kk in chips:
                pltpu.make_async_remote_copy(a.at[own], o.at[kk, own], isend.at[7 * w + m - 1], irecv.at[7 * w + m - 1],
                                             device_id=(*chip, c), device_id_type=MESH).wait_recv()
                fw = pltpu.make_async_remote_copy(o.at[kk, own], o.at[kk, own], isend.at[7 * w + 2 + m], irecv.at[7 * w + 2 + m],
                                                  device_id=sib, device_id_type=MESH)
                fw.start()
                passed.append(fw)
            for fw in passed:
                fw.wait_recv()
            for cp in first + passed:
                cp.wait_send()

    outs = [jax.ShapeDtypeStruct((4,) + a.shape, a.dtype) for a in arrs]
    return _comm_call(name, body, list(arrs), outs, 7 * n, 1)


def swap_other_half(arrs, name):
    n = len(arrs)

    def body(*refs):
        a_refs, o_refs, (ssem, rsem, _) = refs[:n], refs[n:2 * n], refs[2 * n:]
        x, y, c = _pos()
        cps = []
        for w, (a, o) in enumerate(zip(a_refs, o_refs)):
            H = a.shape[1] // 2
            cps.append(pltpu.make_async_remote_copy(a.at[:, pl.ds((1 - c) * H, H)], o, ssem.at[w], rsem.at[w],
                                                    device_id=(x, y, 1 - c), device_id_type=MESH))
            cps[-1].start()
        for cp in cps:
            cp.wait()

    outs = [jax.ShapeDtypeStruct((4, a.shape[1] // 2) + a.shape[2:], a.dtype) for a in arrs]
    return _comm_call(name, body, list(arrs), outs, n, 1)


def exchange_chips(arrs, name):
    n = len(arrs)

    def body(*refs):
        a_refs, o_refs, (ssem, rsem, lsem) = refs[:n], refs[n:2 * n], refs[2 * n:]
        x, y, c = _pos()
        k = 2 * x + y
        cps = []
        for w, (a, o) in enumerate(zip(a_refs, o_refs)):
            cps.append(pltpu.make_async_copy(a.at[k], o.at[k], lsem.at[w]))
            cps[-1].start()
            for m in range(1, 4):
                px, py = _flip(x, m & 2), _flip(y, m & 1)
                cps.append(pltpu.make_async_remote_copy(a.at[2 * px + py], o.at[k], ssem.at[3 * w + m - 1], rsem.at[3 * w + m - 1],
                                                        device_id=(px, py, c), device_id_type=MESH))
                cps[-1].start()
        for cp in cps:
            cp.wait()

    return _comm_call(name, body, list(arrs), [jax.ShapeDtypeStruct(a.shape, a.dtype) for a in arrs], 3 * n, n)


def share_halves(arrs, name):
    n = len(arrs)

    def body(*refs):
        o_refs, (ssem, rsem, _) = refs[n:2 * n], refs[2 * n:]
        x, y, c = _pos()
        cps = []
        for w, o in enumerate(o_refs):
            H = o.shape[0] // 2
            mine = o.at[pl.ds(c * H, H)]
            cps.append(pltpu.make_async_remote_copy(mine, mine, ssem.at[w], rsem.at[w], device_id=(x, y, 1 - c),
                                                    device_id_type=MESH))
            cps[-1].start()
        for cp in cps:
            cp.wait()

    outs = [jax.ShapeDtypeStruct(a.shape, a.dtype) for a in arrs]
    return _comm_call(name, body, list(arrs), outs, n, 1, aliases={i: i for i in range(n)})


def _adamw(w, g, m, v):
    m = ADAM_B1 * m + (1.0 - ADAM_B1) * g
    v = ADAM_B2 * v + (1.0 - ADAM_B2) * (g * g)
    m_hat = m / (1.0 - ADAM_B1 ** ADAM_STEP)
    v_hat = v / (1.0 - ADAM_B2 ** ADAM_STEP)
    return -ADAM_LR * (m_hat / (jnp.sqrt(v_hat) + ADAM_EPS) + ADAM_WD * w), m, v


def _rows_tile(R, C):
    return _pick(R, max(8, (2 << 20) // (4 * C) // 8 * 8), 8)


def chip_partial(gs, buf, cidx, name):
    _, L, R, C = gs.shape
    H, tr = L // 2, _rows_tile(R, C)

    def body(c_ref, g_ref, b_ref, o_ref):
        o_ref[...] = (g_ref[...] + b_ref[...]).astype(BF16)

    return pl.pallas_call(
        body, grid_spec=pltpu.PrefetchScalarGridSpec(
            num_scalar_prefetch=1, grid=(4, H, R // tr),
            in_specs=[pl.BlockSpec((None, None, tr, C), lambda s, l, r, c: (s, c[0] * H + l, r, 0)),
                      pl.BlockSpec((None, None, tr, C), lambda s, l, r, c: (s, l, r, 0))],
            out_specs=pl.BlockSpec((None, None, tr, C), lambda s, l, r, c: (s, l, r, 0))),
        out_shape=jax.ShapeDtypeStruct((4, H, R, C), BF16),
        compiler_params=_cparams(("parallel", "parallel", "parallel")), name=name)(cidx, gs, buf)


def sum_chips(parts, cidx, name):
    _, H, R, C = parts.shape
    tr = _rows_tile(R, C)

    def body(c_ref, p_ref, g_out):
        g = p_ref[0].astype(F32)
        for s in range(1, 4):
            g = g + p_ref[s].astype(F32)
        g_out[...] = g

    return pl.pallas_call(
        body, grid_spec=pltpu.PrefetchScalarGridSpec(
            num_scalar_prefetch=1, grid=(H, R // tr),
            in_specs=[pl.BlockSpec((4, None, tr, C), lambda l, r, c: (0, l, r, 0))],
            out_specs=pl.BlockSpec((None, tr, C), lambda l, r, c: (c[0] * H + l, r, 0))),
        out_shape=jax.ShapeDtypeStruct((2 * H, R, C), F32),
        compiler_params=_cparams(("parallel", "parallel")), name=name)(cidx, parts)


def sum8(arr, name):
    n = arr.shape[1]
    tr = _pick(n, 512, 8)

    def body(a_ref, o_ref):
        s = a_ref[0]
        for j in range(1, 8):
            s = s + a_ref[j]
        o_ref[...] = s

    return pl.pallas_call(
        body, grid=(n // tr,), in_specs=[pl.BlockSpec((8, tr, 128), lambda i: (0, i, 0))],
        out_specs=pl.BlockSpec((tr, 128), lambda i: (i, 0)), out_shape=jax.ShapeDtypeStruct((n, 128), F32),
        compiler_params=_cparams(("parallel",)), name=name)(arr)


def adam_rows(w, g, m, v, name, emit_g=False):
    n, C = w.shape
    tr = _rows_tile(n, C)

    def body(w_ref, g_ref, m_ref, v_ref, *outs):
        d_out, m_out, v_out = outs[-3:]
        d_out[...], m_out[...], v_out[...] = _adamw(w_ref[...], g_ref[...], m_ref[...], v_ref[...])
        if emit_g:
            outs[0][...] = g_ref[...]

    spec = pl.BlockSpec((tr, C), lambda i: (i, 0))
    n_out = 4 if emit_g else 3
    return pl.pallas_call(
        body, grid=(n // tr,), in_specs=[spec] * 4, out_specs=[spec] * n_out,
        out_shape=[jax.ShapeDtypeStruct((n, C), F32)] * n_out, compiler_params=_cparams(("parallel",)), name=name)(w, g, m, v)


def _silu(c):
    return c * jax.nn.sigmoid(c)


def ada_fwd(c_all, w, b, name):
    NC, D = c_all.shape
    L, _, Wc = w.shape
    tn = _pick(Wc, 512, 128)

    def body(c_ref, w_ref, b_ref, o_ref):
        o_ref[...] = bdot(_silu(c_ref[...]), w_ref[...]) + b_ref[...]

    return pl.pallas_call(
        body, grid=(L, Wc // tn),
        in_specs=[pl.BlockSpec((NC, D), lambda l, j: (0, 0)), pl.BlockSpec((None, D, tn), lambda l, j: (l, 0, j)),
                  pl.BlockSpec((None, 1, tn), lambda l, j: (l, 0, j))],
        out_specs=pl.BlockSpec((None, NC, tn), lambda l, j: (l, 0, j)), out_shape=jax.ShapeDtypeStruct((L, NC, Wc), F32),
        compiler_params=_cparams(("parallel", "parallel")), name=name)(c_all, w, b)


def ada_bwd(c_all, w, dmod, name):
    NC, D = c_all.shape
    L, _, Wc = w.shape
    tn = _pick(Wc, 512, 128)

    def body(c_ref, w_ref, d_ref, dw_ref, db_ref, dc_ref):
        first = jnp.logical_and(pl.program_id(0) == 0, pl.program_id(1) == 0)
        f = lambda cs, ww: bdot(cs, ww)
        _, vjp = jax.vjp(f, _silu(c_ref[...]), w_ref[...])
        dcs, dw = vjp(d_ref[...])
        dw_ref[...] = dw
        db_ref[...] = jnp.sum(d_ref[...], axis=0, keepdims=True)
        _acc(dc_ref, dcs, first)

    return pl.pallas_call(
        body, grid=(L, Wc // tn),
        in_specs=[pl.BlockSpec((NC, D), lambda l, j: (0, 0)), pl.BlockSpec((None, D, tn), lambda l, j: (l, 0, j)),
                  pl.BlockSpec((None, NC, tn), lambda l, j: (l, 0, j))],
        out_specs=[pl.BlockSpec((None, D, tn), lambda l, j: (l, 0, j)), pl.BlockSpec((None, 1, tn), lambda l, j: (l, 0, j)),
                   pl.BlockSpec((NC, D), lambda l, j: (0, 0))],
        out_shape=[jax.ShapeDtypeStruct((L, D, Wc), F32), jax.ShapeDtypeStruct((L, 1, Wc), F32),
                   jax.ShapeDtypeStruct((NC, D), F32)],
        compiler_params=_cparams(("arbitrary", "arbitrary")), name=name)(c_all, w, dmod)


def cctx_grad(dcs_twice, c_ctx, name):
    def body(d_ref, c_ref, o_ref):
        _, vjp = jax.vjp(_silu, c_ref[...])
        o_ref[...] = vjp(0.5 * d_ref[...])[0]

    D = c_ctx.shape[1]
    return pl.pallas_call(body, out_shape=jax.ShapeDtypeStruct((8, D), F32), name=name)(dcs_twice, c_ctx)


def _rope_tables(SEQ):
    t = jnp.arange(SEQ)
    row, col = (t // GRID_W).astype(F32), (t % GRID_W).astype(F32)
    n_freq = 16
    freqs = ROPE_THETA ** (-jnp.arange(n_freq, dtype=F32) / n_freq)
    ang = jnp.concatenate([row[:, None] * freqs, col[:, None] * freqs], axis=-1)
    cos, sin = jnp.cos(ang), jnp.sin(ang)
    c = jnp.tile(jnp.concatenate([cos, cos], axis=1), (1, 4))
    s = jnp.tile(jnp.concatenate([-sin, sin], axis=1), (1, 4))
    return (jnp.concatenate([c, jnp.ones((T, 256), F32)], axis=0), jnp.concatenate([s, jnp.zeros((T, 256), F32)], axis=0))


def _unshard_cols(g):
    return jnp.transpose(g, (1, 2, 0, 3)).reshape(g.shape[1], g.shape[2], 4 * g.shape[3])


def _unshard_rows(g):
    return jnp.transpose(g, (1, 0, 2, 3)).reshape(g.shape[1], 4 * g.shape[2], g.shape[3])


def _shard_cols(w, n=4):
    L, R, C = w.shape
    return jnp.transpose(w.reshape(L, R, n, C // n), (2, 0, 1, 3))


def _shard_rows(w):
    L, R, C = w.shape
    return jnp.transpose(w.reshape(L, 4, R // 4, C), (1, 0, 2, 3))


def _ab_in_permute(w):
    L, D, _ = w.shape
    return jnp.concatenate([w[..., 0:256], w[..., 320:1600], w[..., 256:320], jnp.zeros((L, D, 64), w.dtype)], axis=-1)


def _ab_in_unpermute(g):
    return jnp.concatenate([g[..., 0:256], g[..., 1536:1600], g[..., 256:1536]], axis=-1)


def _split_heads(w, a):
    L, K, N = w.shape
    w4 = w.reshape(L, K, 4, N // 4)
    return w4[..., :a].reshape(L, K, 4 * a), w4[..., a:].reshape(L, K, N - 4 * a)


def _join_heads(p, q):
    L, K = p.shape[:2]
    return jnp.concatenate([p.reshape(L, K, 4, -1), q.reshape(L, K, 4, -1)], axis=-1).reshape(L, K, -1)


def _pack(arrs):
    parts = []
    for a in arrs:
        f = a.reshape(-1).astype(F32)
        parts.append(jnp.pad(f, (0, (-f.shape[0]) % 1024)))
    return jnp.concatenate(parts).reshape(-1, 128)


def _unpack(buf, like):
    out, r0 = [], 0
    for a in like:
        n = math.prod(a.shape)
        rows = (n + (-n) % 1024) // 128
        out.append(buf[r0:r0 + rows].reshape(-1)[:n].reshape(a.shape))
        r0 += rows
    return out


_SMALL = ("c_ctx", "ada_b", "norm_mix", "norm_ffn", "norm_final", "mla_q_norm", "mla_kv_norm", "cmlp_v_norm", "cmlp_ws",
          "cmlp_bs", "ret_decay_fwd", "ret_decay_bwd", "ret_norm", "swa_sink")
_BIG = ("ffn_in", "ffn_out", "ab_in", "ab_out", "mla_wq_b", "mla_wkv_b", "cd_in", "cd_out")
_WEIGHTS = ("c_ctx", "ada_w", "ada_b", "norm_mix", "norm_ffn", "norm_final", "ffn_in", "ffn_out", "ab_in", "ab_out",
            "mla_q_norm", "mla_kv_norm", "mla_wq_b", "mla_wkv_b", "cmlp_v_norm", "cmlp_ws", "cmlp_bs", "cd_in", "cd_out",
            "ret_decay_fwd", "ret_decay_bwd", "ret_norm", "swa_sink")


def kernel(x, c, ctx, c_ctx, ada_w, ada_b, norm_mix, norm_ffn, norm_final, ffn_in, ffn_out, ab_in, ab_out, mla_q_norm, mla_kv_norm, mla_wq_b, mla_wkv_b, cmlp_v_norm, cmlp_ws, cmlp_bs, cd_in, cd_out, ret_decay_fwd, ret_decay_bwd, ret_norm, swa_sink, loss_target, m_c_ctx, m_ada_w, m_ada_b, m_norm_mix, m_norm_ffn, m_norm_final, m_ffn_in, m_ffn_out, m_ab_in, m_ab_out, m_mla_q_norm, m_mla_kv_norm, m_mla_wq_b, m_mla_wkv_b, m_cmlp_v_norm, m_cmlp_ws, m_cmlp_bs, m_cd_in, m_cd_out, m_ret_decay_fwd, m_ret_decay_bwd, m_ret_norm, m_swa_sink, v_c_ctx, v_ada_w, v_ada_b, v_norm_mix, v_norm_ffn, v_norm_final, v_ffn_in, v_ffn_out, v_ab_in, v_ab_out, v_mla_q_norm, v_mla_kv_norm, v_mla_wq_b, v_mla_wkv_b, v_cmlp_v_norm, v_cmlp_ws, v_cmlp_bs, v_cd_in, v_cd_out, v_ret_decay_fwd, v_ret_decay_bwd, v_ret_norm, v_swa_sink):
    W = dict(c_ctx=c_ctx, ada_w=ada_w, ada_b=ada_b, norm_mix=norm_mix, norm_ffn=norm_ffn, norm_final=norm_final, ffn_in=ffn_in, ffn_out=ffn_out, ab_in=ab_in, ab_out=ab_out, mla_q_norm=mla_q_norm, mla_kv_norm=mla_kv_norm, mla_wq_b=mla_wq_b, mla_wkv_b=mla_wkv_b, cmlp_v_norm=cmlp_v_norm, cmlp_ws=cmlp_ws, cmlp_bs=cmlp_bs, cd_in=cd_in, cd_out=cd_out, ret_decay_fwd=ret_decay_fwd, ret_decay_bwd=ret_decay_bwd, ret_norm=ret_norm, swa_sink=swa_sink)
    M1 = dict(c_ctx=m_c_ctx, ada_w=m_ada_w, ada_b=m_ada_b, norm_mix=m_norm_mix, norm_ffn=m_norm_ffn, norm_final=m_norm_final, ffn_in=m_ffn_in, ffn_out=m_ffn_out, ab_in=m_ab_in, ab_out=m_ab_out, mla_q_norm=m_mla_q_norm, mla_kv_norm=m_mla_kv_norm, mla_wq_b=m_mla_wq_b, mla_wkv_b=m_mla_wkv_b, cmlp_v_norm=m_cmlp_v_norm, cmlp_ws=m_cmlp_ws, cmlp_bs=m_cmlp_bs, cd_in=m_cd_in, cd_out=m_cd_out, ret_decay_fwd=m_ret_decay_fwd, ret_decay_bwd=m_ret_decay_bwd, ret_norm=m_ret_norm, swa_sink=m_swa_sink)
    M2 = dict(c_ctx=v_c_ctx, ada_w=v_ada_w, ada_b=v_ada_b, norm_mix=v_norm_mix, norm_ffn=v_norm_ffn, norm_final=v_norm_final, ffn_in=v_ffn_in, ffn_out=v_ffn_out, ab_in=v_ab_in, ab_out=v_ab_out, mla_q_norm=v_mla_q_norm, mla_kv_norm=v_mla_kv_norm, mla_wq_b=v_mla_wq_b, mla_wkv_b=v_mla_wkv_b, cmlp_v_norm=v_cmlp_v_norm, cmlp_ws=v_cmlp_ws, cmlp_bs=v_cmlp_bs, cd_in=v_cd_in, cd_out=v_cd_out, ret_decay_fwd=v_ret_decay_fwd, ret_decay_bwd=v_ret_decay_bwd, ret_norm=v_ret_norm, swa_sink=v_swa_sink)

    B, SEQ, D = x.shape
    CTX = ctx.shape[1]
    lay = Layout(B, SEQ, CTX, D)
    nt, NX = lay.nt, lay.NX
    ix, iy, ic = lax.axis_index("x"), lax.axis_index("y"), lax.axis_index("c")
    chip, me = 2 * ix + iy, 4 * ix + 2 * iy + ic
    cidx = jnp.reshape(ic, (1,)).astype(jnp.int32)
    Wc = ada_w.shape[2]
    n_even, n_odd = ab_in.shape[0], cd_in.shape[0]

    rn_row = jnp.pad(ret_norm.reshape(1, -1), ((0, 0), (0, D - ret_norm.size)))
    pack0 = jnp.concatenate([c, rn_row, jnp.zeros((8 - (B + 1) % 8, D), F32)], axis=0) if (B + 1) % 8 else jnp.concatenate([c, rn_row], axis=0)
    g0 = gather8(pack0, "gather_cond")
    NC = -(-(8 * B + 1) // 16) * 16
    c_all = jnp.concatenate([g0[:, :B].reshape(8 * B, D), c_ctx[None], jnp.zeros((NC - 8 * B - 1, D), F32)], axis=0)
    rn_sh = ret_norm.shape[1]
    ret_norm_full = jnp.transpose(g0[0::2, B, :ret_norm.size].reshape(4, n_odd, rn_sh), (1, 0, 2)).reshape(n_odd, 4 * rn_sh)

    gw = gather_weights([W[n].astype(BF16) for n in _BIG], "gather_weights")
    w_ffn_in, w_ab_in, w_wq, w_wkv, w_cd_in = (_unshard_cols(gw[i]) for i in (0, 2, 4, 5, 6))
    w_ffn_out, w_ab_out, w_cd_out = (_unshard_rows(gw[i]) for i in (1, 3, 7))
    w_ab_in = _ab_in_permute(w_ab_in)
    w_ffn_out_t = jnp.transpose(w_ffn_out, (0, 2, 1))
    w_qn, w_qp = _split_heads(w_wq, MLA_NOPE)
    w_k, w_v = _split_heads(w_wkv, MLA_NOPE)

    ab_sh = lax.dynamic_slice_in_dim(ada_b, chip * Wc, Wc, axis=1)[:, None, :]
    mod_sh = ada_fwd(c_all, ada_w, ab_sh, "ada_fwd")
    mod_all = _unshard_cols(gather_chips(mod_sh, "gather_mod"))
    mod_mine = jnp.concatenate([lax.dynamic_slice_in_dim(mod_all, me * B, B, axis=1), mod_all[:, 8 * B:8 * B + 1]], axis=1)
    mod = mod_mine.reshape(DEPTH, B + 1, 6, D)

    tabc, tabs = _rope_tables(SEQ)
    bc8 = lambda a: jnp.broadcast_to(a.reshape(a.shape + (1, 1)), a.shape + (8, 128))
    row = lambda a: a.reshape(1, -1)

    S = jnp.concatenate([x.reshape(NX, D), ctx.reshape(B * CTX, D)], axis=0)
    saved = []
    xn = norm_mod_fwd(S, row(norm_mix[0]), mod[0], 0, lay, nt, "norm_mix_fwd0")
    for l in range(DEPTH):
        j, even = l // 2, l % 2 == 0
        if even:
            z = mm(xn, w_ab_in[j], name=f"ab_in{l}")
            q, k, v = ab_prep_fwd(z, tabc, tabs, row(mla_kv_norm[j]), row(mla_q_norm[j]), w_k[j], w_v[j], w_qn[j], w_qp[j],
                                  lay, f"ab_prep{l}")
            o = mla_fwd(q, k, v, lay, f"mla{l}")
            merged = cmlp_merge_fwd(z, o, row(cmlp_v_norm[j]), cmlp_ws[j], cmlp_bs[j][:, :, None], lay, f"cmlp{l}")
            w_out, mix = w_ab_out[j], (q, k, v)
        else:
            z = mm(xn, w_cd_in[j], name=f"cd_in{l}")
            rq, rk, sq, ke, ve = cd_prep_fwd(z, tabc, tabs, lay, f"cd_prep{l}")
            decf, decb, sink = bc8(ret_decay_fwd[j]), bc8(ret_decay_bwd[j]), bc8(swa_sink[j].reshape(SWA_KV_HEADS, SWA_GROUPS))
            yret = ret_fwd(rq, rk, z, decf, decb, lay, f"ret{l}")
            osw = swa_fwd(sq, ke, ve, sink, lay, f"swa{l}")
            merged = cd_merge_fwd(yret, z, osw, row(ret_norm_full[j]), lay, f"cd_merge{l}")
            w_out, mix = w_cd_out[j], (rq, rk, sq, ke, ve, decf, decb, sink, yret)
        y, S_mid, xn2 = mm_gated(merged, w_out, S, mod[l], 2, lay, name=f"mix_out{l}", n_tiles=nt,
                                 norm=(row(norm_ffn[l]), mod[l], 3))
        act, fa, fb = ffn_in_act(xn2, w_ffn_in[l], f"ffn_in{l}")
        nxt_norm = (row(norm_mix[l + 1]), mod[l + 1], 0) if l + 1 < DEPTH else None
        f, S_new, *xn_next = mm_gated(act, w_ffn_out[l], S_mid, mod[l], 5, lay, name=f"ffn_out{l}", n_tiles=nt, norm=nxt_norm)
        saved.append((S, xn, z, mix, merged, w_out, y, S_mid, xn2, (fa, fb), act, f))
        S, xn = S_new, (xn_next[0] if xn_next else None)

    loss_blk, dS, d_norm_final = loss_head(S, row(norm_final), loss_target.reshape(NX, D), lay, "loss_head")
    loss = lax.psum(loss_blk[0, 0], ("x", "y", "c"))

    G = {n: [None] * W[n].shape[0] for n in ("norm_mix", "norm_ffn", "mla_q_norm", "mla_kv_norm", "cmlp_v_norm", "cmlp_ws",
                                             "cmlp_bs", "ret_decay_fwd", "ret_decay_bwd", "ret_norm", "swa_sink")}
    GB = {n: [None] * cnt for n, cnt in (("ffn_in_a", DEPTH), ("ffn_in_b", DEPTH), ("ffn_out", DEPTH), ("ab_in", n_even), ("ab_out", n_even),
                                         ("wqn", n_even), ("wqp", n_even), ("wk", n_even), ("wv", n_even), ("cd_in", n_odd),
                                         ("cd_out", n_odd))}
    dmod = [None] * DEPTH
    df, dgate2 = gate_bwd(dS, saved[-1][-1], mod[DEPTH - 1], 5, lay, nt, f"ffn_gate_bwd{DEPTH - 1}")
    for l in reversed(range(DEPTH)):
        j, even = l // 2, l % 2 == 0
        S_in, xn, z, mix, merged, w_out, y, S_mid, xn2, (fa, fb), act, f = saved[l]
        da, db = ffn_out_dx_act(df, w_ffn_out_t[l], fa, fb, f"ffn_out_dx{l}")
        GB["ffn_out"][l] = mm(act, df, ta=True, name=f"ffn_out_dw{l}")
        GB["ffn_in_a"][l] = mm(xn2, da, ta=True, name=f"ffn_in_dwa{l}", split=2)
        GB["ffn_in_b"][l] = mm(xn2, db, ta=True, name=f"ffn_in_dwb{l}", split=2)
        dxn2 = ffn_in_dx(da, db, w_ffn_in[l], f"ffn_in_dx{l}")
        dS_mid, dss2, dg, dy, dgate1 = norm_mod_bwd(S_mid, row(norm_ffn[l]), mod[l], 3, dxn2, dS, lay, nt, f"norm_ffn_bwd{l}",
                                                    gate=(y, mod[l], 2))
        G["norm_ffn"][l] = dg
        dmerged = mm(dy, w_out, tb=True, name=f"mix_out_dx{l}")
        d_w_out = mm(merged, dy, ta=True, name=f"mix_out_dw{l}")
        if even:
            q, k, v = mix
            dq, dkx, dkh, dvx, dvh = mla_bwd(q, k, v, dmerged, lay, f"mla_bwd{l}")
            (dz, dgkv, dgq, dwk, dwv, dwqn, dwqp, dgvn, dws, dbs) = ab_rows_bwd(
                z, tabc, tabs, dq, dkx, dkh, dvx, dvh, dmerged, row(mla_kv_norm[j]), row(mla_q_norm[j]), w_k[j], w_v[j],
                w_qn[j], w_qp[j], row(cmlp_v_norm[j]), cmlp_ws[j], cmlp_bs[j][:, :, None], lay, f"ab_rows_bwd{l}")
            G["mla_kv_norm"][j], G["mla_q_norm"][j], G["cmlp_v_norm"][j] = dgkv, dgq, dgvn
            G["cmlp_ws"][j], G["cmlp_bs"][j] = dws, dbs
            GB["wk"][j], GB["wv"][j], GB["wqn"][j], GB["wqp"][j], GB["ab_out"][j] = dwk, dwv, dwqn, dwqp, d_w_out
            w_in = w_ab_in[j]
        else:
            rq, rk, sq, ke, ve, decf, decb, sink, yret = mix
            dyret, drg, dgn = cd_merge_bwd(yret, z, row(ret_norm_full[j]), dmerged, lay, f"cd_merge_bwd{l}")
            dqx, dqh, dkx, dkh, dvx, dvh, ddf, ddb = ret_bwd(rq, rk, z, decf, decb, dyret, lay, f"ret_bwd{l}")
            dsq, dkex, dkeh, dvex, dveh, dsink = swa_bwd(sq, ke, ve, sink, dmerged, lay, f"swa_bwd{l}")
            dz = cd_rows_bwd(z, tabc, tabs, dsq, drg, (dqx, dqh), (dkx, dkh), (dkex, dkeh), (dvex, dveh), (dvx, dvh), lay,
                             f"cd_rows_bwd{l}")
            G["ret_norm"][j], G["ret_decay_fwd"][j], G["ret_decay_bwd"][j] = dgn, ddf[:, 0, 0], ddb[:, 0, 0]
            G["swa_sink"][j] = dsink[:, :, 0, 0].reshape(-1)
            GB["cd_out"][j] = d_w_out
            w_in = w_cd_in[j]
        GB["ab_in" if even else "cd_in"][j] = mm(xn, dz, ta=True, name=f"mix_in_dw{l}")
        dxn = mm(dz, w_in, tb=True, name=f"mix_in_dx{l}")
        dmod_l = lambda dss1: jnp.concatenate([dss1, dgate1, dss2, dgate2], axis=1)
        if l > 0:
            dS, dss1, dg, df, dgate2_prev = norm_mod_bwd(S_in, row(norm_mix[l]), mod[l], 0, dxn, dS_mid, lay, nt,
                                                         f"norm_mix_bwd{l}", gate=(saved[l - 1][-1], mod[l - 1], 5))
            dmod[l], dgate2 = dmod_l(dss1), dgate2_prev
        else:
            dS, dss1, dg = norm_mod_bwd(S_in, row(norm_mix[l]), mod[l], 0, dxn, dS_mid, lay, nt, f"norm_mix_bwd{l}")
            dmod[l] = dmod_l(dss1)
        G["norm_mix"][l] = dg
    grad_x = dS[:NX].reshape(B, SEQ, D)

    st = lambda n: jnp.stack([g.reshape((4 * rn_sh,) if n == "ret_norm" else W[n].shape[1:]) for g in G[n]])
    small_parts = {n: st(n) for n in G}
    small_parts["norm_final"] = d_norm_final.reshape(-1)
    dmod_local = jnp.stack(dmod).reshape(DEPTH, B + 1, 6 * D)
    names1 = ["norm_mix", "norm_ffn", "norm_final", "mla_q_norm", "mla_kv_norm", "cmlp_v_norm", "cmlp_ws", "cmlp_bs",
              "ret_decay_fwd", "ret_decay_bwd", "ret_norm", "swa_sink"]
    like1 = [dmod_local] + [small_parts[n] for n in names1]
    g1 = gather8(_pack(like1), "gather_small_grads")
    tot1 = _unpack(sum8(g1, "sum_small_grads"), like1)
    sg = dict(zip(names1, tot1[1:]))
    n_dm = math.prod(dmod_local.shape)
    dm_each = g1[:, :-(-n_dm // 128)].reshape(8, -1)[:, :n_dm].reshape(8, DEPTH, B + 1, 6 * D)
    dmod_all = jnp.concatenate([jnp.transpose(dm_each[:, :, :B], (1, 0, 2, 3)).reshape(DEPTH, 8 * B, 6 * D),
                                tot1[0][:, B:B + 1], jnp.zeros((DEPTH, NC - 8 * B - 1, 6 * D), F32)], axis=1)
    dmod_sh = lax.dynamic_slice_in_dim(dmod_all, chip * Wc, Wc, axis=2)
    g_ada_w, g_ada_b_sh, dcs = ada_bwd(c_all, ada_w, dmod_sh, "ada_bwd")
    like2 = [dcs[8 * B], g_ada_b_sh]
    g2 = gather8(_pack(like2), "gather_ada_grads")
    tot2 = _unpack(sum8(g2, "sum_ada_grads"), like2)
    bc = lambda a: jnp.broadcast_to(a.reshape(1, D), (8, D))
    sg["c_ctx"] = cctx_grad(bc(tot2[0]), bc(c_ctx), "c_ctx_grad")[0]
    off = D + (-D) % 1024
    gab = g2[0::2, off // 128:(off + DEPTH * Wc) // 128].reshape(4, DEPTH, Wc)
    sg["ada_b"] = jnp.transpose(gab, (1, 0, 2)).reshape(DEPTH, 4 * Wc)
    sg["ret_norm"] = lax.dynamic_slice_in_dim(sg["ret_norm"], chip * rn_sh, rn_sh, axis=1)

    stack = lambda n: jnp.stack(GB[n])
    gs = {"ffn_in": jnp.concatenate([jnp.stack(GB[n], axis=1) for n in ("ffn_in_a", "ffn_in_b")], axis=0),
          "ffn_out": _shard_rows(stack("ffn_out")),
          "ab_in": _shard_cols(_ab_in_unpermute(stack("ab_in"))), "ab_out": _shard_rows(stack("ab_out")),
          "mla_wq_b": _shard_cols(_join_heads(stack("wqn"), stack("wqp"))),
          "mla_wkv_b": _shard_cols(_join_heads(stack("wk"), stack("wv"))),
          "cd_in": _shard_cols(stack("cd_in")), "cd_out": _shard_rows(stack("cd_out"))}
    bufs = swap_other_half([gs[n] for n in _BIG], "swap_core_halves")
    parts = [chip_partial(gs[n], b, cidx, f"chip_partial_{n}") for n, b in zip(_BIG, bufs)]
    arrived = exchange_chips(parts, "exchange_chips")
    grads = share_halves([sum_chips(p, cidx, f"sum_chips_{n}") for n, p in zip(_BIG, arrived)], "share_core_halves")
    flat2 = lambda a: a.reshape(-1, a.shape[-1])
    out = {}
    for n, g in zip(_BIG, grads):
        res = adam_rows(flat2(W[n]), flat2(g), flat2(M1[n]), flat2(M2[n]), f"adam_{n}", emit_g=True)
        out[n] = tuple(r.reshape(W[n].shape) for r in res)

    like_s = [W[n] for n in _SMALL]
    dsm, msm, vsm = adam_rows(_pack(like_s), _pack([sg[n].reshape(W[n].shape) for n in _SMALL]), _pack([M1[n] for n in _SMALL]),
                                _pack([M2[n] for n in _SMALL]), "adam_small")
    for n, d_, m_, v_ in zip(_SMALL, _unpack(dsm, like_s), _unpack(msm, like_s), _unpack(vsm, like_s)):
        out[n] = (sg[n].reshape(W[n].shape), d_, m_, v_)
    d_, m_, v_ = adam_rows(flat2(ada_w), flat2(g_ada_w), flat2(m_ada_w), flat2(v_ada_w), "adam_ada_w")
    out["ada_w"] = (g_ada_w, d_.reshape(ada_w.shape), m_.reshape(ada_w.shape), v_.reshape(ada_w.shape))

    return (loss, grad_x, *[out[n][0] for n in _WEIGHTS], *[out[n][1] for n in _WEIGHTS], *[out[n][2] for n in _WEIGHTS],
            *[out[n][3] for n in _WEIGHTS])
```

```python
import functools
import math

import jax
import jax.numpy as jnp
import numpy as np
from jax import lax
from jax.experimental import pallas as pl
from jax.experimental.pallas import tpu as pltpu

F32 = jnp.float32
BF16 = jnp.bfloat16
EPS = 1e-6
NEG_INF = -1e30
GRID_W = 64
ROPE_THETA = 10000.0
DEPTH = 4
MLA_HEADS, MLA_Q_LORA, MLA_KV_LORA, MLA_NOPE, MLA_ROPE, MLA_V = 4, 256, 256, 128, 64, 128
CMLP_GROUPS, CMLP_CHUNK = 4, 128
CMLP_WIDTH = 512
RET_HEADS, RET_QK, RET_V = 4, 64, 128
SWA_Q_HEADS, SWA_KV_HEADS, SWA_HEAD_DIM, SWA_WINDOW = 8, 2, 64, 128
SWA_GROUPS = SWA_Q_HEADS // SWA_KV_HEADS
AB_IN_P = 1664
ADAM_LR, ADAM_B1, ADAM_B2, ADAM_EPS, ADAM_WD, ADAM_STEP = 0.001, 0.9, 0.999, 1e-08, 0.01, 10

T = 256
SWA_SPAN = T + 2 * SWA_WINDOW
VMEM_LIMIT = 48 * 1024 * 1024
MESH = pl.DeviceIdType.MESH
ANY = pl.BlockSpec(memory_space=pl.ANY)


def _cparams(sem):
    return pltpu.CompilerParams(dimension_semantics=sem, vmem_limit_bytes=VMEM_LIMIT)


@functools.cache
def _bdot_fn(ca, cb):
    fa, fb = 1 - ca, 1 - cb

    def dg(p, q, cp, cq):
        return lax.dot_general(p.astype(BF16), q.astype(BF16), (((cp,), (cq,)), ((), ())), preferred_element_type=F32)

    @jax.custom_vjp
    def bd(a, b):
        return dg(a, b, ca, cb)

    def fwd(a, b):
        return dg(a, b, ca, cb), (a, b)

    def bwd(res, g):
        a, b = res
        da = dg(g, b, 1, fb) if ca == 1 else dg(b, g, fb, 1)
        db = dg(a, g, fa, 0) if cb == 0 else dg(g, a, 0, fa)
        return da, db

    bd.defvjp(fwd, bwd)
    return bd


def bdot(a, b):
    return _bdot_fn(1, 0)(a, b)


def bdot_nt(a, b):
    return _bdot_fn(1, 1)(a, b)


def bdot_tn(a, b):
    return _bdot_fn(0, 0)(a, b)


def _swap32(x):
    w = x.shape[-1]
    lane = lax.broadcasted_iota(jnp.int32, x.shape, 1)
    return jnp.where((lane & 32) == 0, pltpu.roll(x, w - 32, 1), pltpu.roll(x, 32, 1))


@jax.custom_vjp
def rope(x, c, s):
    return x * c + _swap32(x) * s


def _rope_fwd(x, c, s):
    return rope(x, c, s), (c, s)


def _rope_bwd(res, g):
    c, s = res
    return g * c + _swap32(g * s), jnp.zeros_like(c), jnp.zeros_like(s)


rope.defvjp(_rope_fwd, _rope_bwd)


def rms(x, g):
    return x * lax.rsqrt(jnp.mean(x * x, axis=-1, keepdims=True) + EPS) * g


def normmod(x, g, sh, sc):
    return rms(x, g) * (1.0 + sc) + sh


def log_sigmoid(x):
    return jnp.minimum(x, 0.0) - jnp.log(1.0 + jnp.exp(-jnp.abs(x)))


def _head_mask(shape, h):
    lane = lax.broadcasted_iota(jnp.int32, shape, 1)
    return ((lane >> 6) == h).astype(F32)


def _fold_matrix():
    i = lax.broadcasted_iota(jnp.int32, (256, 128), 0)
    j = lax.broadcasted_iota(jnp.int32, (256, 128), 1)
    return ((i & 63) == j).astype(F32)


def _expand_matrix(g):
    i = lax.broadcasted_iota(jnp.int32, (128, 256), 0)
    j = lax.broadcasted_iota(jnp.int32, (128, 256), 1)
    return (i == (j & 63) + 64 * g).astype(F32)


def _acc(ref, val, first):
    @pl.when(first)
    def _():
        ref[...] = val

    @pl.when(jnp.logical_not(first))
    def _():
        ref[...] += val


def _pick(n, cap, mult):
    best = None
    for d in range(mult, min(n, cap) + 1, mult):
        if n % d == 0:
            best = d
    return best if best is not None else n


def mm(a, b, *, ta=False, tb=False, name, split=1):
    M, K = (a.shape[1], a.shape[0]) if ta else a.shape
    N = b.shape[0] if tb else b.shape[1]
    Ns = N // split
    assert split == 1 or (ta and Ns % 128 == 0)
    if ta:
        tn = N
        tm = _pick(M, min(1536, (3 << 20) // tn), 128)
    else:
        tn = _pick(N, 768, 128)
        if tn < 256 and N <= 2304:
            tn = N
        tm = _pick(M, min(1536, (1 << 20) // tn), 128)
    tk = _pick(K, 2048 if not ta else 1024, 128) if K > 2816 or ta else K
    nk = K // tk
    grid = (M // tm, N // tn, nk)
    a_spec = pl.BlockSpec((tk, tm), lambda i, j, k: (k, i)) if ta else pl.BlockSpec((tm, tk), lambda i, j, k: (i, k))
    b_spec = pl.BlockSpec((tn, tk), lambda i, j, k: (j, k)) if tb else pl.BlockSpec((tk, tn), lambda i, j, k: (k, j))
    dims = (((0 if ta else 1,), (1 if tb else 0,)), ((), ()))

    def body(a_ref, b_ref, o_ref):
        part = lax.dot_general(a_ref[...], b_ref[...], dims, preferred_element_type=F32)
        first = pl.program_id(2) == 0
        if split > 1:
            for s in range(split):
                _acc(o_ref.at[s], part[:, s * Ns:(s + 1) * Ns], first)
        elif nk == 1:
            o_ref[...] = part
        else:
            _acc(o_ref, part, first)

    if split > 1:
        out_spec, out_shape = pl.BlockSpec((split, tm, Ns), lambda i, j, k: (0, i, 0)), (split, M, Ns)
    else:
        out_spec, out_shape = pl.BlockSpec((tm, tn), lambda i, j, k: (i, j)), (M, N)
    return pl.pallas_call(
        body, grid=grid, in_specs=[a_spec, b_spec], out_specs=out_spec, out_shape=jax.ShapeDtypeStruct(out_shape, F32),
        compiler_params=_cparams(("parallel", "parallel", "arbitrary")), name=name)(a, b)


def mm_gated(a, b, res, mod, gate_row, lay, *, name, n_tiles, norm=None):
    K, N = b.shape
    M = n_tiles * T

    def body(*refs):
        a_ref, b_ref, r_ref, g_ref = refs[:4]
        y = lax.dot_general(a_ref[...], b_ref[...], (((1,), (0,)), ((), ())), preferred_element_type=F32)
        new = r_ref[...] + g_ref[gate_row:gate_row + 1, :] * y
        if norm is None:
            y_ref, o_ref = refs[4:]
        else:
            gn_ref, mn_ref, y_ref, o_ref, xn_ref = refs[4:]
            k0 = norm[2]
            xn_ref[...] = normmod(new, gn_ref[...], mn_ref[k0:k0 + 1, :], mn_ref[k0 + 1:k0 + 2, :]).astype(BF16)
        y_ref[...] = y.astype(BF16)
        o_ref[...] = new

    rows = pl.BlockSpec((T, N), lambda i: (i, 0))
    extra = [] if norm is None else [_full((1, N)), _modspec(lay, 6, N)]
    return pl.pallas_call(
        body, grid=(n_tiles,),
        in_specs=[pl.BlockSpec((T, K), lambda i: (i, 0)), _full((K, N)), rows, _modspec(lay, 6, N)] + extra,
        out_specs=[rows, rows] + ([] if norm is None else [rows]),
        out_shape=[jax.ShapeDtypeStruct((M, N), BF16), jax.ShapeDtypeStruct((M, N), F32)]
        + ([] if norm is None else [jax.ShapeDtypeStruct((M, N), BF16)]),
        compiler_params=_cparams(("parallel",)), name=name)(a, b, res, mod, *([] if norm is None else norm[:2]))


class Layout:
    def __init__(self, B, SEQ, CTX, D):
        assert CTX == T and SEQ % T == 0 and SEQ >= SWA_SPAN
        self.B, self.SEQ, self.CTX, self.D = B, SEQ, CTX, D
        self.tps = SEQ // T
        self.nxt = B * self.tps
        self.nt = self.nxt + B
        self.NX, self.R = B * SEQ, B * SEQ + B * CTX
        self.nq = self.tps + 1

    def mod_idx(self, i):
        return jnp.where(i < self.nxt, i // self.tps, self.B)

    def rope_idx(self, i):
        return jnp.where(i < self.nxt, i % self.tps, self.tps)

    def first_of_mod(self, i):
        return jnp.logical_or(jnp.logical_and(i < self.nxt, i % self.tps == 0), i == self.nxt)

    def qrow(self, b, qi):
        return jnp.where(qi < self.tps, b * self.tps + qi, self.nxt + b)


def _row(w, col=0):
    return pl.BlockSpec((T, w), lambda i: (i, col))


def _full(shape):
    nd = len(shape)
    return pl.BlockSpec(shape, lambda i: (0,) * nd)


def _modspec(lay, rows, D):
    return pl.BlockSpec((None, rows, D), lambda i: (lay.mod_idx(i), 0, 0))


def _ropespec(lay, w):
    return pl.BlockSpec((T, w), lambda i: (lay.rope_idx(i), 0))


def _xh_specs(lay, w):
    nxt = lay.nxt
    return [pl.BlockSpec((T, w), lambda i: (jnp.minimum(i, nxt - 1), 0)), pl.BlockSpec((T, w), lambda i: (jnp.maximum(i - nxt, 0), 0))]


def _xh_pick(lay, x_ref, h_ref):
    return jnp.where(pl.program_id(0) < lay.nxt, x_ref[...], h_ref[...])


def norm_mod_fwd(S, g, mod, k0, lay, n_tiles, name):
    D = S.shape[1]

    def body(s_ref, g_ref, mod_ref, o_ref):
        o_ref[...] = normmod(s_ref[...], g_ref[...], mod_ref[k0:k0 + 1, :], mod_ref[k0 + 1:k0 + 2, :]).astype(BF16)

    return pl.pallas_call(
        body, grid=(n_tiles,), in_specs=[_row(D), _full((1, D)), _modspec(lay, 6, D)], out_specs=_row(D),
        out_shape=jax.ShapeDtypeStruct((n_tiles * T, D), BF16), compiler_params=_cparams(("parallel",)), name=name)(S, g, mod)


def norm_mod_bwd(S, g, mod, k0, dxn, ds_in, lay, n_tiles, name, gate=None):
    D = S.shape[1]

    def body(*refs):
        s_ref, g_ref, mod_ref, dxn_ref, dsin_ref = refs[:5]
        i = pl.program_id(0)
        _, vjp = jax.vjp(normmod, s_ref[...], g_ref[...], mod_ref[k0:k0 + 1, :], mod_ref[k0 + 1:k0 + 2, :])
        dx, dg, dsh, dsc = vjp(dxn_ref[...])
        ds = dsin_ref[...] + dx
        if gate is None:
            ds_ref, dss_ref, dg_ref = refs[5:]
        else:
            y_ref, gmod_ref, ds_ref, dss_ref, dg_ref, dy_ref, dgate_ref = refs[5:]
            row = gate[2]
            dy_ref[...] = (gmod_ref[row:row + 1, :] * ds).astype(BF16)
            _acc(dgate_ref, jnp.sum(ds * y_ref[...], axis=0, keepdims=True), lay.first_of_mod(i))
        ds_ref[...] = ds
        _acc(dg_ref, dg, i == 0)
        _acc(dss_ref, jnp.concatenate([dsh, dsc], axis=0), lay.first_of_mod(i))

    R_ = n_tiles * T
    gated = gate is not None
    return pl.pallas_call(
        body, grid=(n_tiles,),
        in_specs=[_row(D), _full((1, D)), _modspec(lay, 6, D), _row(D), _row(D)] + ([_row(D), _modspec(lay, 6, D)] if gated else []),
        out_specs=[_row(D), _modspec(lay, 2, D), _full((1, D))] + ([_row(D), _modspec(lay, 1, D)] if gated else []),
        out_shape=[jax.ShapeDtypeStruct((R_, D), F32), jax.ShapeDtypeStruct((lay.B + 1, 2, D), F32),
                   jax.ShapeDtypeStruct((1, D), F32)]
        + ([jax.ShapeDtypeStruct((R_, D), BF16), jax.ShapeDtypeStruct((lay.B + 1, 1, D), F32)] if gated else []),
        compiler_params=_cparams(("arbitrary",)), name=name)(S, g, mod, dxn, ds_in, *(gate[:2] if gated else []))


def gate_bwd(dS, y, mod, gate_row, lay, n_tiles, name):
    D = dS.shape[1]

    def body(ds_ref, y_ref, mod_ref, dy_ref, dgate_ref):
        i = pl.program_id(0)
        ds = ds_ref[...]
        dy_ref[...] = (mod_ref[gate_row:gate_row + 1, :] * ds).astype(BF16)
        _acc(dgate_ref, jnp.sum(ds * y_ref[...], axis=0, keepdims=True), lay.first_of_mod(i))

    return pl.pallas_call(
        body, grid=(n_tiles,), in_specs=[_row(D), _row(D), _modspec(lay, 6, D)],
        out_specs=[_row(D), _modspec(lay, 1, D)],
        out_shape=[jax.ShapeDtypeStruct((n_tiles * T, D), BF16), jax.ShapeDtypeStruct((lay.B + 1, 1, D), F32)],
        compiler_params=_cparams(("arbitrary",)), name=name)(dS, y, mod)


def _swiglu(a, b):
    return a * jax.nn.sigmoid(a) * b


def _ffn_tiles(M, F):
    tn = _pick(F, 1408, 128)
    return _pick(M, (3 << 18) // tn, 128), tn


def ffn_in_act(x, ws, l, name):
    M, D = x.shape
    Cs = ws.shape[3]
    tm = _pick(M, (3 << 18) // Cs, 128)

    def body(x_ref, wa_ref, wb_ref, act_ref, a_ref, b_ref):
        dims = (((1,), (0,)), ((), ()))
        a = lax.dot_general(x_ref[...], wa_ref[...], dims, preferred_element_type=F32)
        b = lax.dot_general(x_ref[...], wb_ref[...], dims, preferred_element_type=F32)
        act_ref[...] = _swiglu(a, b).astype(BF16)
        a_ref[...] = a.astype(BF16)
        b_ref[...] = b.astype(BF16)

    out = pl.BlockSpec((tm, Cs), lambda i, j: (i, j))
    return pl.pallas_call(
        body, grid=(M // tm, 2),
        in_specs=[pl.BlockSpec((tm, D), lambda i, j: (i, 0)), pl.BlockSpec((None, None, D, Cs), lambda i, j: (j, l, 0, 0)),
                  pl.BlockSpec((None, None, D, Cs), lambda i, j: (2 + j, l, 0, 0))],
        out_specs=[out, out, out], out_shape=[jax.ShapeDtypeStruct((M, 2 * Cs), BF16)] * 3,
        compiler_params=_cparams(("parallel", "parallel")), name=name)(x, ws, ws)


def ffn_out_dx_act(df, wt, a, b, name):
    M, D = df.shape
    F = wt.shape[1]
    tm, tn = _ffn_tiles(M, F)

    def body(df_ref, w_ref, a_ref, b_ref, da_ref, db_ref):
        dact = lax.dot_general(df_ref[...], w_ref[...], (((1,), (0,)), ((), ())), preferred_element_type=F32)
        _, vjp = jax.vjp(_swiglu, a_ref[...].astype(F32), b_ref[...].astype(F32))
        da, db = vjp(dact)
        da_ref[...] = da.astype(BF16)
        db_ref[...] = db.astype(BF16)

    blk = pl.BlockSpec((tm, tn), lambda i, j: (i, j))
    return pl.pallas_call(
        body, grid=(M // tm, F // tn),
        in_specs=[pl.BlockSpec((tm, D), lambda i, j: (i, 0)), pl.BlockSpec((D, tn), lambda i, j: (0, j)), blk, blk],
        out_specs=[blk, blk], out_shape=[jax.ShapeDtypeStruct((M, F), BF16)] * 2,
        compiler_params=_cparams(("parallel", "parallel")), name=name)(df, wt, a, b)


def ffn_in_dx(da, db, ws, l, name):
    M, F = da.shape
    D, Cs = ws.shape[2], ws.shape[3]
    tm, tn = _pick(M, 512, 128), _pick(D, 512, 128)

    def body(da_ref, db_ref, w0_ref, w1_ref, w2_ref, w3_ref, o_ref):
        dims = (((1,), (1,)), ((), ()))
        dot = lambda g_ref, s, w_ref: lax.dot_general(g_ref[:, s * Cs:(s + 1) * Cs], w_ref[...], dims, preferred_element_type=F32)
        o_ref[...] = (dot(da_ref, 0, w0_ref) + dot(da_ref, 1, w1_ref)) + (dot(db_ref, 0, w2_ref) + dot(db_ref, 1, w3_ref))

    shard = lambda s: pl.BlockSpec((None, None, tn, Cs), lambda i, j: (s, l, j, 0))
    return pl.pallas_call(
        body, grid=(M // tm, D // tn),
        in_specs=[pl.BlockSpec((tm, F), lambda i, j: (i, 0)), pl.BlockSpec((tm, F), lambda i, j: (i, 0))]
        + [shard(s) for s in range(4)],
        out_specs=pl.BlockSpec((tm, tn), lambda i, j: (i, j)), out_shape=jax.ShapeDtypeStruct((M, D), F32),
        compiler_params=_cparams(("parallel", "parallel")), name=name)(da, db, ws, ws, ws, ws)


def loss_head(S, g, target, lay, name):
    D = S.shape[1]
    nxt = lay.nxt

    def tile_loss(x, gg, t):
        err = rms(x, gg) - t
        return 0.5 * jnp.sum(jnp.mean(err * err, axis=-1))

    def body(s_ref, g_ref, t_ref, loss_ref, ds_ref, dg_ref):
        i = pl.program_id(0)

        @pl.when(i < nxt)
        def _():
            val, vjp = jax.vjp(tile_loss, s_ref[...], g_ref[...], t_ref[...])
            dx, dg, _ = vjp(jnp.ones((), F32))
            ds_ref[...] = dx
            _acc(dg_ref, dg, i == 0)
            _acc(loss_ref, jnp.full((8, 128), val, F32), i == 0)

        @pl.when(i >= nxt)
        def _():
            ds_ref[...] = jnp.zeros((T, D), F32)

    return pl.pallas_call(
        body, grid=(lay.nt,),
        in_specs=[_row(D), _full((1, D)), pl.BlockSpec((T, D), lambda i: (jnp.minimum(i, nxt - 1), 0))],
        out_specs=[_full((8, 128)), _row(D), _full((1, D))],
        out_shape=[jax.ShapeDtypeStruct((8, 128), F32), jax.ShapeDtypeStruct((lay.R, D), F32),
                   jax.ShapeDtypeStruct((1, D), F32)],
        compiler_params=_cparams(("arbitrary",)), name=name)(S, g, target)


def _ab_prep(zkv, zq, zpe, c256, s256, c128, s128, gkv, gq, wk, wv, wqn, wqp):
    kvn = rms(zkv, gkv)
    kn, v = bdot(kvn, wk), bdot(kvn, wv)
    qn = rms(zq, gq)
    qnope, qpe = bdot(qn, wqn), rope(bdot(qn, wqp), c256, s256)
    kpe = rope(zpe, c128, s128)
    fold = _fold_matrix()
    qparts, kparts = [], []
    for h in range(MLA_HEADS):
        qparts += [qnope[:, 128 * h:128 * (h + 1)], bdot(qpe * _head_mask(qpe.shape, h), fold)]
        kparts += [kn[:, 128 * h:128 * (h + 1)], kpe]
    return jnp.concatenate(qparts, axis=1), jnp.concatenate(kparts, axis=1), v


def _ab_prep_specs(lay):
    return [_row(256, 0), _row(256, 1), _row(128, 12), _ropespec(lay, 256), _ropespec(lay, 256), _ropespec(lay, 128),
            _ropespec(lay, 128), _full((1, 256)), _full((1, 256)), _full((256, 512)), _full((256, 512)),
            _full((256, 512)), _full((256, 256))]


def ab_prep_fwd(z, tabc, tabs, gkv, gq, wk, wv, wqn, wqp, lay, name):
    def body(*refs):
        ins, (q_ref, k_ref, v_ref) = refs[:13], refs[13:]
        q, k, v = _ab_prep(*[r[...].astype(F32) for r in ins])
        q_ref[...] = q.astype(BF16)
        k_ref[...] = k.astype(BF16)
        v_ref[...] = v.astype(BF16)

    R = lay.R
    return pl.pallas_call(
        body, grid=(lay.nt,), in_specs=_ab_prep_specs(lay), out_specs=[_row(1024), _row(1024), _row(512)],
        out_shape=[jax.ShapeDtypeStruct((R, 1024), BF16), jax.ShapeDtypeStruct((R, 1024), BF16),
                   jax.ShapeDtypeStruct((R, 512), BF16)],
        compiler_params=_cparams(("parallel",)), name=name)(z, z, z, tabc, tabs, tabc, tabs, gkv, gq, wk, wv, wqn, wqp)


_MLA_SCALE = (MLA_NOPE + MLA_ROPE) ** -0.5


def _mla_x(q, kx, vx, kh, vh):
    sx, sh = bdot_nt(q, kx) * _MLA_SCALE, bdot_nt(q, kh) * _MLA_SCALE
    m = lax.stop_gradient(jnp.maximum(jnp.max(sx, axis=-1, keepdims=True), jnp.max(sh, axis=-1, keepdims=True)))
    ex, eh = jnp.exp(sx - m), jnp.exp(sh - m)
    inv = 1.0 / (jnp.sum(ex, axis=-1, keepdims=True) + jnp.sum(eh, axis=-1, keepdims=True))
    return bdot(ex * inv, vx) + bdot(eh * inv, vh)


def _mla_h(q, kh, vh):
    sh = bdot_nt(q, kh) * _MLA_SCALE
    eh = jnp.exp(sh - lax.stop_gradient(jnp.max(sh, axis=-1, keepdims=True)))
    return bdot(eh * (1.0 / jnp.sum(eh, axis=-1, keepdims=True)), vh)


def _mla_specs(lay):
    nxt, SEQ = lay.nxt, lay.SEQ
    return [pl.BlockSpec((T, 256), lambda b, h, qi: (lay.qrow(b, qi), h)),
            pl.BlockSpec((SEQ, 256), lambda b, h, qi: (b, h)), pl.BlockSpec((SEQ, 128), lambda b, h, qi: (b, h)),
            pl.BlockSpec((T, 256), lambda b, h, qi: (nxt + b, h)), pl.BlockSpec((T, 128), lambda b, h, qi: (nxt + b, h))]


def mla_fwd(q, k, v, lay, name):
    tps = lay.tps

    def body(q_ref, kx_ref, vx_ref, kh_ref, vh_ref, o_ref):
        qi = pl.program_id(2)
        f = lambda r: r[...]

        @pl.when(qi < tps)
        def _():
            o_ref[...] = _mla_x(f(q_ref), f(kx_ref), f(vx_ref), f(kh_ref), f(vh_ref)).astype(BF16)

        @pl.when(qi == tps)
        def _():
            o_ref[...] = _mla_h(f(q_ref), f(kh_ref), f(vh_ref)).astype(BF16)

    return pl.pallas_call(
        body, grid=(lay.B, MLA_HEADS, lay.nq), in_specs=_mla_specs(lay),
        out_specs=pl.BlockSpec((T, 128), lambda b, h, qi: (lay.qrow(b, qi), h)),
        out_shape=jax.ShapeDtypeStruct((lay.R, 512), BF16),
        compiler_params=_cparams(("parallel", "parallel", "arbitrary")), name=name)(q, k, v, k, v)


def mla_bwd(q, k, v, dmerged, lay, name):
    tps, SEQ, B = lay.tps, lay.SEQ, lay.B

    def body(q_ref, kx_ref, vx_ref, kh_ref, vh_ref, do_ref, dq_ref, dkx_ref, dkh_ref, dvx_ref, dvh_ref):
        qi = pl.program_id(2)
        f = lambda r: r[...].astype(F32)

        @pl.when(qi < tps)
        def _():
            _, vjp = jax.vjp(_mla_x, f(q_ref), f(kx_ref), f(vx_ref), f(kh_ref), f(vh_ref))
            dq, dkx, dvx, dkh, dvh = vjp(do_ref[...])
            dq_ref[...] = dq
            _acc(dkx_ref, dkx, qi == 0)
            _acc(dvx_ref, dvx, qi == 0)
            _acc(dkh_ref, dkh, qi == 0)
            _acc(dvh_ref, dvh, qi == 0)

        @pl.when(qi == tps)
        def _():
            _, vjp = jax.vjp(_mla_h, f(q_ref), f(kh_ref), f(vh_ref))
            dq, dkh, dvh = vjp(do_ref[...])
            dq_ref[...] = dq
            dkh_ref[...] += dkh
            dvh_ref[...] += dvh

    return pl.pallas_call(
        body, grid=(B, MLA_HEADS, lay.nq),
        in_specs=_mla_specs(lay) + [pl.BlockSpec((T, 128), lambda b, h, qi: (lay.qrow(b, qi), h))],
        out_specs=[pl.BlockSpec((T, 256), lambda b, h, qi: (lay.qrow(b, qi), h)),
                   pl.BlockSpec((SEQ, 256), lambda b, h, qi: (b, h)), pl.BlockSpec((T, 256), lambda b, h, qi: (b, h)),
                   pl.BlockSpec((SEQ, 128), lambda b, h, qi: (b, h)), pl.BlockSpec((T, 128), lambda b, h, qi: (b, h))],
        out_shape=[jax.ShapeDtypeStruct((lay.R, 1024), F32), jax.ShapeDtypeStruct((lay.NX, 1024), F32),
                   jax.ShapeDtypeStruct((B * T, 1024), F32), jax.ShapeDtypeStruct((lay.NX, 512), F32),
                   jax.ShapeDtypeStruct((B * T, 512), F32)],
        compiler_params=_cparams(("parallel", "parallel", "arbitrary")), name=name)(q, k, v, k, v, dmerged)


def _cmlp_piece(zu, zv, g, ws, bs):
    u, v = jax.nn.gelu(zu), jax.nn.gelu(zv)
    v = v * lax.rsqrt(jnp.mean(v * v, axis=-1, keepdims=True) + EPS) * g
    return u * (bdot(ws, v) + bs)


def _pieces():
    return [(c, g) for c in range(T // CMLP_CHUNK) for g in range(CMLP_GROUPS)]


def cmlp_merge_fwd(z, o, gvn, ws, bs, lay, name):
    def body(zu_ref, zv_ref, o_ref, g_ref, ws_ref, bs_ref, m_ref):
        m_ref[:, 0:512] = o_ref[...]
        for c, g in _pieces():
            rows, cols = slice(128 * c, 128 * (c + 1)), slice(128 * g, 128 * (g + 1))
            piece = _cmlp_piece(zu_ref[rows, cols], zv_ref[rows, cols], g_ref[:, cols], ws_ref[g], bs_ref[g])
            m_ref[rows, 512 + 128 * g:512 + 128 * (g + 1)] = piece.astype(BF16)

    return pl.pallas_call(
        body, grid=(lay.nt,),
        in_specs=[_row(512, 1), _row(512, 2), _row(512), _full((1, 512)), _full((4, 128, 128)), _full((4, 128, 1))],
        out_specs=_row(1024), out_shape=jax.ShapeDtypeStruct((lay.R, 1024), BF16),
        compiler_params=_cparams(("parallel",)), name=name)(z, z, o, gvn, ws, bs)


def ab_rows_bwd(z, tabc, tabs, dq, dkx, dkh, dvx, dvh, dmerged, gkv, gq, wk, wv, wqn, wqp, gvn, ws, bs, lay, name):
    def body(*refs):
        prep_in = refs[:3] + refs[5:9] + refs[11:17]
        zu_ref, zv_ref = refs[3:5]
        dq_ref, dcm_ref = refs[9:11]
        gvn_ref, ws_ref, bs_ref = refs[17:20]
        dkx_ref, dkh_ref, dvx_ref, dvh_ref = refs[20:24]
        dz_ref, dgkv_ref, dgq_ref, dwk_ref, dwv_ref, dwqn_ref, dwqp_ref, dgvn_ref, dws_ref, dbs_ref = refs[24:]
        first = pl.program_id(0) == 0
        _, vjp = jax.vjp(_ab_prep, *[r[...].astype(F32) for r in prep_in])
        d = vjp((dq_ref[...], _xh_pick(lay, dkx_ref, dkh_ref), _xh_pick(lay, dvx_ref, dvh_ref)))
        dz_ref[:, 0:256] = d[0].astype(BF16)
        dz_ref[:, 256:512] = d[1].astype(BF16)
        dz_ref[:, 1536:1664] = d[2].astype(BF16)
        for ref, val in zip((dgkv_ref, dgq_ref, dwk_ref, dwv_ref, dwqn_ref, dwqp_ref), d[7:]):
            _acc(ref, val, first)
        dws = [0.0] * CMLP_GROUPS
        dbs = [0.0] * CMLP_GROUPS
        dgv = [0.0] * CMLP_GROUPS
        for c, g in _pieces():
            rows, cols = slice(128 * c, 128 * (c + 1)), slice(128 * g, 128 * (g + 1))
            _, vjp = jax.vjp(_cmlp_piece, zu_ref[rows, cols], zv_ref[rows, cols], gvn_ref[:, cols], ws_ref[g], bs_ref[g])
            dzu, dzv, dg_, dws_, dbs_ = vjp(dcm_ref[rows, cols])
            dz_ref[rows, 512 + 128 * g:512 + 128 * (g + 1)] = dzu.astype(BF16)
            dz_ref[rows, 1024 + 128 * g:1024 + 128 * (g + 1)] = dzv.astype(BF16)
            dws[g], dbs[g], dgv[g] = dws[g] + dws_, dbs[g] + dbs_, dgv[g] + dg_
        _acc(dgvn_ref, jnp.concatenate(dgv, axis=1), first)
        _acc(dws_ref, jnp.stack(dws), first)
        _acc(dbs_ref, jnp.stack(dbs), first)

    acc_shapes = [(1, 256), (1, 256), (256, 512), (256, 512), (256, 512), (256, 256), (1, 512), (4, 128, 128), (4, 128, 1)]
    return pl.pallas_call(
        body, grid=(lay.nt,),
        in_specs=_ab_prep_specs(lay)[:3] + [_row(512, 1), _row(512, 2)] + _ab_prep_specs(lay)[3:7]
        + [_row(1024), _row(512, 1)] + _ab_prep_specs(lay)[7:]
        + [_full((1, 512)), _full((4, 128, 128)), _full((4, 128, 1))] + _xh_specs(lay, 1024) + _xh_specs(lay, 512),
        out_specs=[_row(AB_IN_P)] + [_full(s) for s in acc_shapes],
        out_shape=[jax.ShapeDtypeStruct((lay.R, AB_IN_P), BF16)] + [jax.ShapeDtypeStruct(s, F32) for s in acc_shapes],
        compiler_params=_cparams(("arbitrary",)), name=name)(
            z, z, z, z, z, tabc, tabs, tabc, tabs, dq, dmerged, gkv, gq, wk, wv, wqn, wqp, gvn, ws, bs, dkx, dkh, dvx, dvh)


def _cd_prep(zrk, zrq, zsk, zsq0, zsq1, zsv, c256, s256, c128, s128):
    rk = rope(zrk * (RET_QK ** -0.5), c256, s256)
    rq = rope(zrq, c256, s256)
    sk = rope(zsk, c128, s128)
    sq0, sq1 = rope(zsq0, c256, s256), rope(zsq1, c256, s256)
    e0, e1 = _expand_matrix(0), _expand_matrix(1)
    return rq, rk, sq0, sq1, bdot(sk, e0), bdot(sk, e1), bdot(zsv, e0), bdot(zsv, e1)


def _cd_prep_specs(lay):
    return [_row(256, 0), _row(256, 4), _row(128, 6), _row(256, 7), _row(256, 8), _row(128, 7),
            _ropespec(lay, 256), _ropespec(lay, 256), _ropespec(lay, 128), _ropespec(lay, 128)]


def cd_prep_fwd(z, tabc, tabs, lay, name):
    def body(*refs):
        ins, (rq_ref, rk_ref, sq_ref, ke_ref, ve_ref) = refs[:10], refs[10:]
        rq, rk, sq0, sq1, k0, k1, v0, v1 = _cd_prep(*[r[...] for r in ins])
        rq_ref[...] = rq.astype(BF16)
        rk_ref[...] = rk.astype(BF16)
        for ref, (a, b) in ((sq_ref, (sq0, sq1)), (ke_ref, (k0, k1)), (ve_ref, (v0, v1))):
            ref[:, 0:256] = a.astype(BF16)
            ref[:, 256:512] = b.astype(BF16)

    R = lay.R
    return pl.pallas_call(
        body, grid=(lay.nt,), in_specs=_cd_prep_specs(lay),
        out_specs=[_row(256), _row(256), _row(512), _row(512), _row(512)],
        out_shape=[jax.ShapeDtypeStruct((R, w), BF16) for w in (256, 256, 512, 512, 512)],
        compiler_params=_cparams(("parallel",)), name=name)(z, z, z, z, z, z, tabc, tabs, tabc, tabs)


def _ret_sample(h, qs, ks, vs, df, db):
    lgf, lgb = log_sigmoid(df), log_sigmoid(db)
    idx = lax.broadcasted_iota(jnp.int32, (T, 1), 0).astype(F32)
    diff = idx - lax.broadcasted_iota(jnp.int32, (1, T), 1).astype(F32)
    intra = (jnp.where(diff >= 0, jnp.exp(lgf * jnp.maximum(diff, 0.0)), 0.0)
             + jnp.where(diff <= 0, jnp.exp(lgb * jnp.maximum(-diff, 0.0)), 0.0))
    qdf, kdf, cdf = jnp.exp(lgf * (idx + 1.0)), jnp.exp(lgf * (T - 1.0 - idx)), jnp.exp(lgf * T)
    qdb, kdb, cdb = jnp.exp(lgb * (T - idx)), jnp.exp(lgb * idx), jnp.exp(lgb * T)
    mask = _head_mask(qs[0].shape, h)
    qs = [q * mask for q in qs]
    ys = [bdot(bdot_nt(q, k) * intra, v) for q, k, v in zip(qs, ks, vs)]
    n = len(qs)
    state = bdot_tn(ks[0] * kdf, vs[0])
    for i in range(1, n):
        ys[i] = ys[i] + bdot(qs[i] * qdf, state)
        if i + 1 < n:
            state = state * cdf + bdot_tn(ks[i] * kdf, vs[i])
    state = bdot_tn(ks[0] * kdb, vs[0])
    for i in range(n - 1, 0, -1):
        ys[i] = ys[i] + bdot(qs[i] * qdb, state)
        if i > 1:
            state = state * cdb + bdot_tn(ks[i] * kdb, vs[i])
    return ys


def _ret_specs(lay):
    nxt, SEQ = lay.nxt, lay.SEQ
    xs = lambda w, col: pl.BlockSpec((SEQ, w), lambda b, h: (b, col(h)))
    hs = lambda w, col: pl.BlockSpec((T, w), lambda b, h: (nxt + b, col(h)))
    zero, head = (lambda h: 0), (lambda h: 2 + h)
    dec = pl.BlockSpec((None, 8, 128), lambda b, h: (h, 0, 0))
    return [xs(256, zero), hs(256, zero), xs(256, zero), hs(256, zero), xs(128, head), hs(128, head), dec, dec]


def _ret_tiles(x_ref, h_ref, tps, cast=None):
    tiles = [h_ref[...]] + [x_ref[i * T:(i + 1) * T, :] for i in range(tps)]
    return [t.astype(cast) for t in tiles] if cast is not None else tiles


def ret_fwd(rq, rk, z, decf, decb, lay, name):
    tps, SEQ = lay.tps, lay.SEQ

    def body(qx_ref, qh_ref, kx_ref, kh_ref, vx_ref, vh_ref, df_ref, db_ref, yx_ref, yh_ref):
        ys = _ret_sample(pl.program_id(1), _ret_tiles(qx_ref, qh_ref, tps), _ret_tiles(kx_ref, kh_ref, tps),
                         _ret_tiles(vx_ref, vh_ref, tps), df_ref[0:1, 0:1], db_ref[0:1, 0:1])
        yh_ref[...] = ys[0]
        for i in range(tps):
            yx_ref[i * T:(i + 1) * T, :] = ys[i + 1]

    return pl.pallas_call(
        body, grid=(lay.B, RET_HEADS), in_specs=_ret_specs(lay),
        out_specs=[pl.BlockSpec((SEQ, 128), lambda b, h: (b, h)), pl.BlockSpec((T, 128), lambda b, h: (b, h))],
        out_shape=[jax.ShapeDtypeStruct((lay.NX, 512), F32), jax.ShapeDtypeStruct((lay.B * T, 512), F32)],
        compiler_params=_cparams(("parallel", "arbitrary")), name=name)(rq, rq, rk, rk, z, z, decf, decb)


def ret_bwd(rq, rk, z, decf, decb, dy, lay, name):
    tps, SEQ, B = lay.tps, lay.SEQ, lay.B
    nxt = lay.nxt

    def body(qx_ref, qh_ref, kx_ref, kh_ref, vx_ref, vh_ref, df_ref, db_ref, dyx_ref, dyh_ref,
             dqx_ref, dqh_ref, dkx_ref, dkh_ref, dvx_ref, dvh_ref, ddf_ref, ddb_ref):
        h = pl.program_id(1)
        _, vjp = jax.vjp(functools.partial(_ret_sample, h), _ret_tiles(qx_ref, qh_ref, tps, F32),
                         _ret_tiles(kx_ref, kh_ref, tps, F32), _ret_tiles(vx_ref, vh_ref, tps), df_ref[0:1, 0:1],
                         db_ref[0:1, 0:1])
        dqs, dks, dvs, ddf, ddb = vjp(_ret_tiles(dyx_ref, dyh_ref, tps))
        first = h == 0
        _acc(dqh_ref, dqs[0], first)
        _acc(dkh_ref, dks[0], first)
        dvh_ref[...] = dvs[0]
        for i in range(tps):
            rows = slice(i * T, (i + 1) * T)
            _acc(dqx_ref.at[rows], dqs[i + 1], first)
            _acc(dkx_ref.at[rows], dks[i + 1], first)
            dvx_ref[rows, :] = dvs[i + 1]
        @pl.when(jnp.logical_and(pl.program_id(0) == 0, first))
        def _():
            ddf_ref[...] = jnp.zeros(ddf_ref.shape, F32)
            ddb_ref[...] = jnp.zeros(ddb_ref.shape, F32)

        ddf_ref[h] += jnp.broadcast_to(ddf, (8, 128))
        ddb_ref[h] += jnp.broadcast_to(ddb, (8, 128))

    acc_x, acc_h = pl.BlockSpec((SEQ, 256), lambda b, h: (b, 0)), pl.BlockSpec((T, 256), lambda b, h: (b, 0))
    head_x, head_h = pl.BlockSpec((SEQ, 128), lambda b, h: (b, h)), pl.BlockSpec((T, 128), lambda b, h: (b, h))
    dec = pl.BlockSpec((RET_HEADS, 8, 128), lambda b, h: (0, 0, 0))
    return pl.pallas_call(
        body, grid=(B, RET_HEADS),
        in_specs=_ret_specs(lay) + [head_x, pl.BlockSpec((T, 128), lambda b, h: (nxt + b, h))],
        out_specs=[acc_x, acc_h, acc_x, acc_h, head_x, head_h, dec, dec],
        out_shape=[jax.ShapeDtypeStruct((lay.NX, 256), F32), jax.ShapeDtypeStruct((B * T, 256), F32),
                   jax.ShapeDtypeStruct((lay.NX, 256), F32), jax.ShapeDtypeStruct((B * T, 256), F32),
                   jax.ShapeDtypeStruct((lay.NX, 512), F32), jax.ShapeDtypeStruct((B * T, 512), F32),
                   jax.ShapeDtypeStruct((RET_HEADS, 8, 128), F32), jax.ShapeDtypeStruct((RET_HEADS, 8, 128), F32)],
        compiler_params=_cparams(("arbitrary", "arbitrary")), name=name)(rq, rq, rk, rk, z, z, decf, decb, dy, dy)


_SWA_SCALE = SWA_HEAD_DIM ** -0.5


def _swa_head(qh, sw, kh, vw, vh, sink):
    sh = bdot_nt(qh, kh) * _SWA_SCALE
    m = jnp.maximum(jnp.max(sh, axis=-1, keepdims=True), sink)
    if sw is not None:
        m = jnp.maximum(m, jnp.max(sw, axis=-1, keepdims=True))
    m = lax.stop_gradient(m)
    eh, es = jnp.exp(sh - m), jnp.exp(sink - m)
    tot = jnp.sum(eh, axis=-1, keepdims=True) + es
    if sw is None:
        return bdot(eh * (1.0 / tot), vh)
    ew = jnp.exp(sw - m)
    inv = 1.0 / (tot + jnp.sum(ew, axis=-1, keepdims=True))
    return bdot(ew * inv, vw) + bdot(eh * inv, vh)


def _swa_x(t0, kpos0, sq, kw, vw, kh, vh, *sinks):
    t = t0 + lax.broadcasted_iota(jnp.int32, (T, 1), 0)
    pos = kpos0 + lax.broadcasted_iota(jnp.int32, (1, SWA_SPAN), 1)
    band = jnp.abs(t - pos) <= SWA_WINDOW
    out = 0.0
    for i in range(SWA_GROUPS):
        mi = _head_mask(sq.shape, i)
        qh = sq * mi
        sw = jnp.where(band, bdot_nt(qh, kw) * _SWA_SCALE, NEG_INF)
        out = out + _swa_head(qh, sw, kh, vw, vh, sinks[i]) * mi
    return out


def _swa_h(sq, kh, vh, *sinks):
    out = 0.0
    for i in range(SWA_GROUPS):
        mi = _head_mask(sq.shape, i)
        out = out + _swa_head(sq * mi, None, kh, None, vh, sinks[i]) * mi
    return out


def _swa_specs(lay):
    nxt, SEQ = lay.nxt, lay.SEQ
    return [pl.BlockSpec((T, 256), lambda g, b, qi: (lay.qrow(b, qi), g)),
            pl.BlockSpec((SEQ, 256), lambda g, b, qi: (b, g)), pl.BlockSpec((SEQ, 256), lambda g, b, qi: (b, g)),
            pl.BlockSpec((T, 256), lambda g, b, qi: (nxt + b, g)), pl.BlockSpec((T, 256), lambda g, b, qi: (nxt + b, g)),
            pl.BlockSpec((None, 4, 8, 128), lambda g, b, qi: (g, 0, 0, 0))]


def _swa_start(qi, SEQ):
    return pl.multiple_of(jnp.clip(qi * T - SWA_WINDOW, 0, SEQ - SWA_SPAN), SWA_WINDOW)


def swa_fwd(sq, kexp, vexp, sink, lay, name):
    tps, SEQ = lay.tps, lay.SEQ

    def body(sq_ref, kx_ref, vx_ref, kh_ref, vh_ref, sink_ref, o_ref):
        qi = pl.program_id(2)
        f = lambda r: r[...]
        sinks = [sink_ref[i][0:1, 0:1] for i in range(SWA_GROUPS)]

        @pl.when(qi < tps)
        def _():
            k0 = _swa_start(qi, SEQ)
            kw, vw = kx_ref[pl.ds(k0, SWA_SPAN), :], vx_ref[pl.ds(k0, SWA_SPAN), :]
            o_ref[...] = _swa_x(qi * T, k0, f(sq_ref), kw, vw, f(kh_ref), f(vh_ref), *sinks).astype(BF16)

        @pl.when(qi == tps)
        def _():
            o_ref[...] = _swa_h(f(sq_ref), f(kh_ref), f(vh_ref), *sinks).astype(BF16)

    return pl.pallas_call(
        body, grid=(SWA_KV_HEADS, lay.B, lay.nq), in_specs=_swa_specs(lay),
        out_specs=pl.BlockSpec((T, 256), lambda g, b, qi: (lay.qrow(b, qi), g)),
        out_shape=jax.ShapeDtypeStruct((lay.R, 512), BF16),
        compiler_params=_cparams(("parallel", "parallel", "arbitrary")), name=name)(sq, kexp, vexp, kexp, vexp, sink)


def swa_bwd(sq, kexp, vexp, sink, dmerged, lay, name):
    tps, SEQ, B = lay.tps, lay.SEQ, lay.B

    def body(sq_ref, kx_ref, vx_ref, kh_ref, vh_ref, sink_ref, do_ref, dsq_ref, dkx_ref, dkh_ref, dvx_ref, dvh_ref, dsink_ref):
        b, qi = pl.program_id(1), pl.program_id(2)
        f = lambda r: r[...].astype(F32)
        sinks = [sink_ref[i][0:1, 0:1] for i in range(SWA_GROUPS)]
        very_first = jnp.logical_and(b == 0, qi == 0)

        def acc_sink(ds):
            for i in range(SWA_GROUPS):
                _acc(dsink_ref.at[i], jnp.broadcast_to(ds[i], (8, 128)), very_first)

        @pl.when(qi == 0)
        def _():
            for ref in (dkx_ref, dkh_ref, dvx_ref, dvh_ref):
                ref[...] = jnp.zeros(ref.shape, F32)

        @pl.when(qi < tps)
        def _():
            k0 = _swa_start(qi, SEQ)
            win = pl.ds(k0, SWA_SPAN)
            kw, vw = kx_ref[win, :].astype(F32), vx_ref[win, :].astype(F32)
            _, vjp = jax.vjp(functools.partial(_swa_x, qi * T, k0), f(sq_ref), kw, vw, f(kh_ref), f(vh_ref), *sinks)
            d = vjp(do_ref[...])
            dsq_ref[...] = d[0]
            dkx_ref[win, :] += d[1]
            dvx_ref[win, :] += d[2]
            dkh_ref[...] += d[3]
            dvh_ref[...] += d[4]
            acc_sink(d[5:9])

        @pl.when(qi == tps)
        def _():
            _, vjp = jax.vjp(_swa_h, f(sq_ref), f(kh_ref), f(vh_ref), *sinks)
            d = vjp(do_ref[...])
            dsq_ref[...] = d[0]
            dkh_ref[...] += d[1]
            dvh_ref[...] += d[2]
            acc_sink(d[3:7])

    xs = pl.BlockSpec((SEQ, 256), lambda g, b, qi: (b, g))
    hs = pl.BlockSpec((T, 256), lambda g, b, qi: (b, g))
    return pl.pallas_call(
        body, grid=(SWA_KV_HEADS, B, lay.nq),
        in_specs=_swa_specs(lay) + [pl.BlockSpec((T, 256), lambda g, b, qi: (lay.qrow(b, qi), 2 + g))],
        out_specs=[pl.BlockSpec((T, 256), lambda g, b, qi: (lay.qrow(b, qi), g)), xs, hs, xs, hs,
                   pl.BlockSpec((None, 4, 8, 128), lambda g, b, qi: (g, 0, 0, 0))],
        out_shape=[jax.ShapeDtypeStruct((lay.R, 512), F32), jax.ShapeDtypeStruct((lay.NX, 512), F32),
                   jax.ShapeDtypeStruct((B * T, 512), F32), jax.ShapeDtypeStruct((lay.NX, 512), F32),
                   jax.ShapeDtypeStruct((B * T, 512), F32), jax.ShapeDtypeStruct((SWA_KV_HEADS, 4, 8, 128), F32)],
        compiler_params=_cparams(("arbitrary", "arbitrary", "arbitrary")), name=name)(
            sq, kexp, vexp, kexp, vexp, sink, dmerged)


def _cd_merge_piece(y, rg, g):
    return (y * lax.rsqrt(jnp.mean(y * y, axis=-1, keepdims=True) + EPS) * g) * (rg * jax.nn.sigmoid(rg))


def cd_merge_fwd(y, z, o, gn, lay, name):
    def body(yx_ref, yh_ref, rga_ref, rgb_ref, o_ref, g_ref, m_ref):
        y = _xh_pick(lay, yx_ref, yh_ref)
        for h in range(RET_HEADS):
            cols = slice(128 * h, 128 * (h + 1))
            rg_ref, rcols = (rga_ref, cols) if h < 2 else (rgb_ref, slice(128 * (h - 2), 128 * (h - 1)))
            m_ref[:, cols] = _cd_merge_piece(y[:, cols], rg_ref[:, rcols], g_ref[:, cols]).astype(BF16)
        m_ref[:, 512:1024] = o_ref[...]

    return pl.pallas_call(
        body, grid=(lay.nt,), in_specs=_xh_specs(lay, 512) + [_row(256, 5), _row(256, 6), _row(512), _full((1, 512))],
        out_specs=_row(1024), out_shape=jax.ShapeDtypeStruct((lay.R, 1024), BF16),
        compiler_params=_cparams(("parallel",)), name=name)(*y, z, z, o, gn)


def cd_merge_bwd(y, z, gn, dmerged, lay, name):
    def body(yx_ref, yh_ref, rga_ref, rgb_ref, g_ref, dm_ref, dy_ref, drg_ref, dg_ref):
        first = pl.program_id(0) == 0
        y = _xh_pick(lay, yx_ref, yh_ref)
        dgs = []
        for h in range(RET_HEADS):
            cols = slice(128 * h, 128 * (h + 1))
            rg_ref, rcols = (rga_ref, cols) if h < 2 else (rgb_ref, slice(128 * (h - 2), 128 * (h - 1)))
            _, vjp = jax.vjp(_cd_merge_piece, y[:, cols], rg_ref[:, rcols], g_ref[:, cols])
            dy, drg, dg = vjp(dm_ref[:, cols])
            dy_ref[:, cols] = dy
            drg_ref[:, cols] = drg
            dgs.append(dg)
        _acc(dg_ref, jnp.concatenate(dgs, axis=1), first)

    return pl.pallas_call(
        body, grid=(lay.nt,), in_specs=_xh_specs(lay, 512) + [_row(256, 5), _row(256, 6), _full((1, 512)), _row(512, 0)],
        out_specs=[_row(512), _row(512), _full((1, 512))],
        out_shape=[jax.ShapeDtypeStruct((lay.R, 512), F32), jax.ShapeDtypeStruct((lay.R, 512), F32),
                   jax.ShapeDtypeStruct((1, 512), F32)],
        compiler_params=_cparams(("arbitrary",)), name=name)(*y, z, z, gn, dmerged)


def cd_rows_bwd(z, tabc, tabs, dsq, drg, drq, drk, dke, dve, drv, lay, name):
    def body(*refs):
        ins = refs[:10]
        dsq_ref, drg_ref = refs[10:12]
        drq, drk, dke, dve, drv = (_xh_pick(lay, refs[12 + 2 * n], refs[13 + 2 * n]) for n in range(5))
        dz_ref = refs[22]
        _, vjp = jax.vjp(_cd_prep, *[r[...] for r in ins])
        cts = (drq, drk, dsq_ref[:, 0:256], dsq_ref[:, 256:512], dke[:, 0:256], dke[:, 256:512],
               dve[:, 0:256], dve[:, 256:512])
        dzrk, dzrq, dzsk, dzsq0, dzsq1, dzsv = vjp(cts)[:6]
        dz_ref[:, 0:256] = dzrk.astype(BF16)
        dz_ref[:, 256:768] = drv.astype(BF16)
        dz_ref[:, 768:896] = dzsk.astype(BF16)
        dz_ref[:, 896:1024] = dzsv.astype(BF16)
        dz_ref[:, 1024:1280] = dzrq.astype(BF16)
        dz_ref[:, 1280:1792] = drg_ref[...].astype(BF16)
        dz_ref[:, 1792:2048] = dzsq0.astype(BF16)
        dz_ref[:, 2048:2304] = dzsq1.astype(BF16)

    return pl.pallas_call(
        body, grid=(lay.nt,),
        in_specs=_cd_prep_specs(lay) + [_row(512), _row(512)] + _xh_specs(lay, 256) + _xh_specs(lay, 256)
        + _xh_specs(lay, 512) + _xh_specs(lay, 512) + _xh_specs(lay, 512),
        out_specs=_row(2304), out_shape=jax.ShapeDtypeStruct((lay.R, 2304), BF16),
        compiler_params=_cparams(("parallel",)), name=name)(
            z, z, z, z, z, z, tabc, tabs, tabc, tabs, dsq, drg, *drq, *drk, *dke, *dve, *drv)


def _pos():
    return lax.axis_index("x"), lax.axis_index("y"), lax.axis_index("c")


def _flip(v, bit):
    return 1 - v if bit else v


def _comm_call(name, body, ins, out_shapes, n_remote, n_local, aliases=None):
    return pl.pallas_call(
        body, in_specs=[ANY] * len(ins), out_specs=[ANY] * len(out_shapes), out_shape=out_shapes,
        scratch_shapes=[pltpu.SemaphoreType.DMA((n_remote,)), pltpu.SemaphoreType.DMA((n_remote,)),
                        pltpu.SemaphoreType.DMA((n_local,))],
        input_output_aliases=aliases or {}, name=name)(*ins)


def gather8(arr, name):
    def body(a_ref, o_ref, ssem, rsem, lsem):
        x, y, c = _pos()
        me = 4 * x + 2 * y + c
        loc = pltpu.make_async_copy(a_ref, o_ref.at[me], lsem.at[0])
        loc.start()
        cps = []
        for m in range(1, 8):
            peer = (_flip(x, m & 4), _flip(y, m & 2), _flip(c, m & 1))
            cps.append(pltpu.make_async_remote_copy(a_ref, o_ref.at[me], ssem.at[m - 1], rsem.at[m - 1],
                                                    device_id=peer, device_id_type=MESH))
            cps[-1].start()
        for cp in cps:
            cp.wait()
        loc.wait()

    return _comm_call(name, body, [arr], [jax.ShapeDtypeStruct((8,) + arr.shape, arr.dtype)], 7, 1)[0]


def gather_chips(arr, name):
    def body(a_ref, o_ref, ssem, rsem, lsem):
        x, y, c = _pos()
        k = 2 * x + y
        loc = pltpu.make_async_copy(a_ref, o_ref.at[k], lsem.at[0])
        loc.start()
        cps = []
        for m in range(1, 4):
            peer = (_flip(x, m & 2), _flip(y, m & 1), c)
            cps.append(pltpu.make_async_remote_copy(a_ref, o_ref.at[k], ssem.at[m - 1], rsem.at[m - 1],
                                                    device_id=peer, device_id_type=MESH))
            cps[-1].start()
        for cp in cps:
            cp.wait()
        loc.wait()

    return _comm_call(name, body, [arr], [jax.ShapeDtypeStruct((4,) + arr.shape, arr.dtype)], 3, 1)[0]


def gather_weights(arrs, name):
    n = len(arrs)

    def body(*refs):
        a_refs, o_refs, (isend, irecv, _) = refs[:n], refs[n:2 * n], refs[2 * n:]
        x, y, c = _pos()
        k = 2 * x + y
        sib = (x, y, 1 - c)
        chips = [(m, (_flip(x, m & 2), _flip(y, m & 1)), 2 * _flip(x, m & 2) + _flip(y, m & 1)) for m in range(1, 4)]
        waits = []
        for w, (a, o) in enumerate(zip(a_refs, o_refs)):
            H = a.shape[0] // 2
            own = pl.ds(c * H, H)
            first = [pltpu.make_async_remote_copy(a.at[own], o.at[k, own], isend.at[7 * w + m - 1], irecv.at[7 * w + m - 1],
                                                  device_id=(*chip, c), device_id_type=MESH) for m, chip, _ in chips]
            first.append(pltpu.make_async_remote_copy(a, o.at[k], isend.at[7 * w + 6], irecv.at[7 * w + 6], device_id=sib,
                                                      device_id_type=MESH))
            for cp in first:
                cp.start()
            waits.append((first, H, own, a, o, w))
        for first, H, own, a, o, w in waits:
            first[3].wait_recv()
            passed = []
            for m, chip, kk in chips:
                pltpu.make_async_remote_copy(a.at[own], o.at[kk, own], isend.at[7 * w + m - 1], irecv.at[7 * w + m - 1],
                                             device_id=(*chip, c), device_id_type=MESH).wait_recv()
                fw = pltpu.make_async_remote_copy(o.at[kk, own], o.at[kk, own], isend.at[7 * w + 2 + m], irecv.at[7 * w + 2 + m],
                                                  device_id=sib, device_id_type=MESH)
                fw.start()
                passed.append(fw)
            for fw in passed:
                fw.wait_recv()
            for cp in first + passed:
                cp.wait_send()

    outs = [jax.ShapeDtypeStruct((4,) + a.shape, a.dtype) for a in arrs]
    return _comm_call(name, body, list(arrs), outs, 7 * n, 1)


def swap_other_half(arrs, name):
    n = len(arrs)

    def body(*refs):
        a_refs, o_refs, (ssem, rsem, _) = refs[:n], refs[n:2 * n], refs[2 * n:]
        x, y, c = _pos()
        cps = []
        for w, (a, o) in enumerate(zip(a_refs, o_refs)):
            H = a.shape[1] // 2
            cps.append(pltpu.make_async_remote_copy(a.at[:, pl.ds((1 - c) * H, H)], o, ssem.at[w], rsem.at[w],
                                                    device_id=(x, y, 1 - c), device_id_type=MESH))
            cps[-1].start()
        for cp in cps:
            cp.wait()

    outs = [jax.ShapeDtypeStruct((4, a.shape[1] // 2) + a.shape[2:], a.dtype) for a in arrs]
    return _comm_call(name, body, list(arrs), outs, n, 1)


def exchange_chips(arrs, name):
    n = len(arrs)

    def body(*refs):
        a_refs, o_refs, (ssem, rsem, lsem) = refs[:n], refs[n:2 * n], refs[2 * n:]
        x, y, c = _pos()
        k = 2 * x + y
        cps = []
        for w, (a, o) in enumerate(zip(a_refs, o_refs)):
            cps.append(pltpu.make_async_copy(a.at[k], o.at[k], lsem.at[w]))
            cps[-1].start()
            for m in range(1, 4):
                px, py = _flip(x, m & 2), _flip(y, m & 1)
                cps.append(pltpu.make_async_remote_copy(a.at[2 * px + py], o.at[k], ssem.at[3 * w + m - 1], rsem.at[3 * w + m - 1],
                                                        device_id=(px, py, c), device_id_type=MESH))
                cps[-1].start()
        for cp in cps:
            cp.wait()

    return _comm_call(name, body, list(arrs), [jax.ShapeDtypeStruct(a.shape, a.dtype) for a in arrs], 3 * n, n)


def share_halves(arrs, name):
    n = len(arrs)

    def body(*refs):
        o_refs, (ssem, rsem, _) = refs[n:2 * n], refs[2 * n:]
        x, y, c = _pos()
        cps = []
        for w, o in enumerate(o_refs):
            H = o.shape[0] // 2
            mine = o.at[pl.ds(c * H, H)]
            cps.append(pltpu.make_async_remote_copy(mine, mine, ssem.at[w], rsem.at[w], device_id=(x, y, 1 - c),
                                                    device_id_type=MESH))
            cps[-1].start()
        for cp in cps:
            cp.wait()

    outs = [jax.ShapeDtypeStruct(a.shape, a.dtype) for a in arrs]
    return _comm_call(name, body, list(arrs), outs, n, 1, aliases={i: i for i in range(n)})


def _adamw(w, g, m, v):
    m = ADAM_B1 * m + (1.0 - ADAM_B1) * g
    v = ADAM_B2 * v + (1.0 - ADAM_B2) * (g * g)
    m_hat = m / (1.0 - ADAM_B1 ** ADAM_STEP)
    v_hat = v / (1.0 - ADAM_B2 ** ADAM_STEP)
    return -ADAM_LR * (m_hat / (jnp.sqrt(v_hat) + ADAM_EPS) + ADAM_WD * w), m, v


def _rows_tile(R, C):
    return _pick(R, max(8, (2 << 20) // (4 * C) // 8 * 8), 8)


def chip_partial(gs, buf, cidx, name):
    _, L, R, C = gs.shape
    H, tr = L // 2, _rows_tile(R, C)

    def body(c_ref, g_ref, b_ref, o_ref):
        o_ref[...] = (g_ref[...] + b_ref[...]).astype(BF16)

    return pl.pallas_call(
        body, grid_spec=pltpu.PrefetchScalarGridSpec(
            num_scalar_prefetch=1, grid=(4, H, R // tr),
            in_specs=[pl.BlockSpec((None, None, tr, C), lambda s, l, r, c: (s, c[0] * H + l, r, 0)),
                      pl.BlockSpec((None, None, tr, C), lambda s, l, r, c: (s, l, r, 0))],
            out_specs=pl.BlockSpec((None, None, tr, C), lambda s, l, r, c: (s, l, r, 0))),
        out_shape=jax.ShapeDtypeStruct((4, H, R, C), BF16),
        compiler_params=_cparams(("parallel", "parallel", "parallel")), name=name)(cidx, gs, buf)


def sum_chips(parts, cidx, name):
    _, H, R, C = parts.shape
    tr = _rows_tile(R, C)

    def body(c_ref, p_ref, g_out):
        g = p_ref[0].astype(F32)
        for s in range(1, 4):
            g = g + p_ref[s].astype(F32)
        g_out[...] = g

    return pl.pallas_call(
        body, grid_spec=pltpu.PrefetchScalarGridSpec(
            num_scalar_prefetch=1, grid=(H, R // tr),
            in_specs=[pl.BlockSpec((4, None, tr, C), lambda l, r, c: (0, l, r, 0))],
            out_specs=pl.BlockSpec((None, tr, C), lambda l, r, c: (c[0] * H + l, r, 0))),
        out_shape=jax.ShapeDtypeStruct((2 * H, R, C), F32),
        compiler_params=_cparams(("parallel", "parallel")), name=name)(cidx, parts)


def sum8(arr, name):
    n = arr.shape[1]
    tr = _pick(n, 512, 8)

    def body(a_ref, o_ref):
        s = a_ref[0]
        for j in range(1, 8):
            s = s + a_ref[j]
        o_ref[...] = s

    return pl.pallas_call(
        body, grid=(n // tr,), in_specs=[pl.BlockSpec((8, tr, 128), lambda i: (0, i, 0))],
        out_specs=pl.BlockSpec((tr, 128), lambda i: (i, 0)), out_shape=jax.ShapeDtypeStruct((n, 128), F32),
        compiler_params=_cparams(("parallel",)), name=name)(arr)


def adam_rows(w, g, m, v, name, emit_g=False):
    n, C = w.shape
    tr = _rows_tile(n, C)

    def body(w_ref, g_ref, m_ref, v_ref, *outs):
        d_out, m_out, v_out = outs[-3:]
        d_out[...], m_out[...], v_out[...] = _adamw(w_ref[...], g_ref[...], m_ref[...], v_ref[...])
        if emit_g:
            outs[0][...] = g_ref[...]

    spec = pl.BlockSpec((tr, C), lambda i: (i, 0))
    n_out = 4 if emit_g else 3
    return pl.pallas_call(
        body, grid=(n // tr,), in_specs=[spec] * 4, out_specs=[spec] * n_out,
        out_shape=[jax.ShapeDtypeStruct((n, C), F32)] * n_out, compiler_params=_cparams(("parallel",)), name=name)(w, g, m, v)


def _silu(c):
    return c * jax.nn.sigmoid(c)


def ada_fwd(c_all, w, b, name):
    NC, D = c_all.shape
    L, _, Wc = w.shape
    tn = _pick(Wc, 512, 128)

    def body(c_ref, w_ref, b_ref, o_ref):
        o_ref[...] = bdot(_silu(c_ref[...]), w_ref[...]) + b_ref[...]

    return pl.pallas_call(
        body, grid=(L, Wc // tn),
        in_specs=[pl.BlockSpec((NC, D), lambda l, j: (0, 0)), pl.BlockSpec((None, D, tn), lambda l, j: (l, 0, j)),
                  pl.BlockSpec((None, 1, tn), lambda l, j: (l, 0, j))],
        out_specs=pl.BlockSpec((None, NC, tn), lambda l, j: (l, 0, j)), out_shape=jax.ShapeDtypeStruct((L, NC, Wc), F32),
        compiler_params=_cparams(("parallel", "parallel")), name=name)(c_all, w, b)


def ada_bwd(c_all, w, dmod, name):
    NC, D = c_all.shape
    L, _, Wc = w.shape
    tn = _pick(Wc, 512, 128)

    def body(c_ref, w_ref, d_ref, dw_ref, db_ref, dc_ref):
        first = jnp.logical_and(pl.program_id(0) == 0, pl.program_id(1) == 0)
        f = lambda cs, ww: bdot(cs, ww)
        _, vjp = jax.vjp(f, _silu(c_ref[...]), w_ref[...])
        dcs, dw = vjp(d_ref[...])
        dw_ref[...] = dw
        db_ref[...] = jnp.sum(d_ref[...], axis=0, keepdims=True)
        _acc(dc_ref, dcs, first)

    return pl.pallas_call(
        body, grid=(L, Wc // tn),
        in_specs=[pl.BlockSpec((NC, D), lambda l, j: (0, 0)), pl.BlockSpec((None, D, tn), lambda l, j: (l, 0, j)),
                  pl.BlockSpec((None, NC, tn), lambda l, j: (l, 0, j))],
        out_specs=[pl.BlockSpec((None, D, tn), lambda l, j: (l, 0, j)), pl.BlockSpec((None, 1, tn), lambda l, j: (l, 0, j)),
                   pl.BlockSpec((NC, D), lambda l, j: (0, 0))],
        out_shape=[jax.ShapeDtypeStruct((L, D, Wc), F32), jax.ShapeDtypeStruct((L, 1, Wc), F32),
                   jax.ShapeDtypeStruct((NC, D), F32)],
        compiler_params=_cparams(("arbitrary", "arbitrary")), name=name)(c_all, w, dmod)


def cctx_grad(dcs_twice, c_ctx, name):
    def body(d_ref, c_ref, o_ref):
        _, vjp = jax.vjp(_silu, c_ref[...])
        o_ref[...] = vjp(0.5 * d_ref[...])[0]

    D = c_ctx.shape[1]
    return pl.pallas_call(body, out_shape=jax.ShapeDtypeStruct((8, D), F32), name=name)(dcs_twice, c_ctx)


def _rope_tables(SEQ):
    t = jnp.arange(SEQ)
    row, col = (t // GRID_W).astype(F32), (t % GRID_W).astype(F32)
    n_freq = 16
    freqs = ROPE_THETA ** (-jnp.arange(n_freq, dtype=F32) / n_freq)
    ang = jnp.concatenate([row[:, None] * freqs, col[:, None] * freqs], axis=-1)
    cos, sin = jnp.cos(ang), jnp.sin(ang)
    c = jnp.tile(jnp.concatenate([cos, cos], axis=1), (1, 4))
    s = jnp.tile(jnp.concatenate([-sin, sin], axis=1), (1, 4))
    return (jnp.concatenate([c, jnp.ones((T, 256), F32)], axis=0), jnp.concatenate([s, jnp.zeros((T, 256), F32)], axis=0))


def _unshard_cols(g):
    return jnp.transpose(g, (1, 2, 0, 3)).reshape(g.shape[1], g.shape[2], 4 * g.shape[3])


def _unshard_rows(g):
    return jnp.transpose(g, (1, 0, 2, 3)).reshape(g.shape[1], 4 * g.shape[2], g.shape[3])


def _shard_cols(w, n=4):
    L, R, C = w.shape
    return jnp.transpose(w.reshape(L, R, n, C // n), (2, 0, 1, 3))


def _shard_rows(w):
    L, R, C = w.shape
    return jnp.transpose(w.reshape(L, 4, R // 4, C), (1, 0, 2, 3))


def _ab_in_permute(w):
    L, D, _ = w.shape
    return jnp.concatenate([w[..., 0:256], w[..., 320:1600], w[..., 256:320], jnp.zeros((L, D, 64), w.dtype)], axis=-1)


def _ab_in_unpermute(g):
    return jnp.concatenate([g[..., 0:256], g[..., 1536:1600], g[..., 256:1536]], axis=-1)


def _split_heads(w, a):
    L, K, N = w.shape
    w4 = w.reshape(L, K, 4, N // 4)
    return w4[..., :a].reshape(L, K, 4 * a), w4[..., a:].reshape(L, K, N - 4 * a)


def _join_heads(p, q):
    L, K = p.shape[:2]
    return jnp.concatenate([p.reshape(L, K, 4, -1), q.reshape(L, K, 4, -1)], axis=-1).reshape(L, K, -1)


def _pack(arrs):
    parts = []
    for a in arrs:
        f = a.reshape(-1).astype(F32)
        parts.append(jnp.pad(f, (0, (-f.shape[0]) % 1024)))
    return jnp.concatenate(parts).reshape(-1, 128)


def _unpack(buf, like):
    out, r0 = [], 0
    for a in like:
        n = math.prod(a.shape)
        rows = (n + (-n) % 1024) // 128
        out.append(buf[r0:r0 + rows].reshape(-1)[:n].reshape(a.shape))
        r0 += rows
    return out


_SMALL = ("c_ctx", "ada_b", "norm_mix", "norm_ffn", "norm_final", "mla_q_norm", "mla_kv_norm", "cmlp_v_norm", "cmlp_ws",
          "cmlp_bs", "ret_decay_fwd", "ret_decay_bwd", "ret_norm", "swa_sink")
_BIG = ("ffn_in", "ffn_out", "ab_in", "ab_out", "mla_wq_b", "mla_wkv_b", "cd_in", "cd_out")
_WEIGHTS = ("c_ctx", "ada_w", "ada_b", "norm_mix", "norm_ffn", "norm_final", "ffn_in", "ffn_out", "ab_in", "ab_out",
            "mla_q_norm", "mla_kv_norm", "mla_wq_b", "mla_wkv_b", "cmlp_v_norm", "cmlp_ws", "cmlp_bs", "cd_in", "cd_out",
            "ret_decay_fwd", "ret_decay_bwd", "ret_norm", "swa_sink")


def kernel(x, c, ctx, c_ctx, ada_w, ada_b, norm_mix, norm_ffn, norm_final, ffn_in, ffn_out, ab_in, ab_out, mla_q_norm, mla_kv_norm, mla_wq_b, mla_wkv_b, cmlp_v_norm, cmlp_ws, cmlp_bs, cd_in, cd_out, ret_decay_fwd, ret_decay_bwd, ret_norm, swa_sink, loss_target, m_c_ctx, m_ada_w, m_ada_b, m_norm_mix, m_norm_ffn, m_norm_final, m_ffn_in, m_ffn_out, m_ab_in, m_ab_out, m_mla_q_norm, m_mla_kv_norm, m_mla_wq_b, m_mla_wkv_b, m_cmlp_v_norm, m_cmlp_ws, m_cmlp_bs, m_cd_in, m_cd_out, m_ret_decay_fwd, m_ret_decay_bwd, m_ret_norm, m_swa_sink, v_c_ctx, v_ada_w, v_ada_b, v_norm_mix, v_norm_ffn, v_norm_final, v_ffn_in, v_ffn_out, v_ab_in, v_ab_out, v_mla_q_norm, v_mla_kv_norm, v_mla_wq_b, v_mla_wkv_b, v_cmlp_v_norm, v_cmlp_ws, v_cmlp_bs, v_cd_in, v_cd_out, v_ret_decay_fwd, v_ret_decay_bwd, v_ret_norm, v_swa_sink):
    W = dict(c_ctx=c_ctx, ada_w=ada_w, ada_b=ada_b, norm_mix=norm_mix, norm_ffn=norm_ffn, norm_final=norm_final, ffn_in=ffn_in, ffn_out=ffn_out, ab_in=ab_in, ab_out=ab_out, mla_q_norm=mla_q_norm, mla_kv_norm=mla_kv_norm, mla_wq_b=mla_wq_b, mla_wkv_b=mla_wkv_b, cmlp_v_norm=cmlp_v_norm, cmlp_ws=cmlp_ws, cmlp_bs=cmlp_bs, cd_in=cd_in, cd_out=cd_out, ret_decay_fwd=ret_decay_fwd, ret_decay_bwd=ret_decay_bwd, ret_norm=ret_norm, swa_sink=swa_sink)
    M1 = dict(c_ctx=m_c_ctx, ada_w=m_ada_w, ada_b=m_ada_b, norm_mix=m_norm_mix, norm_ffn=m_norm_ffn, norm_final=m_norm_final, ffn_in=m_ffn_in, ffn_out=m_ffn_out, ab_in=m_ab_in, ab_out=m_ab_out, mla_q_norm=m_mla_q_norm, mla_kv_norm=m_mla_kv_norm, mla_wq_b=m_mla_wq_b, mla_wkv_b=m_mla_wkv_b, cmlp_v_norm=m_cmlp_v_norm, cmlp_ws=m_cmlp_ws, cmlp_bs=m_cmlp_bs, cd_in=m_cd_in, cd_out=m_cd_out, ret_decay_fwd=m_ret_decay_fwd, ret_decay_bwd=m_ret_decay_bwd, ret_norm=m_ret_norm, swa_sink=m_swa_sink)
    M2 = dict(c_ctx=v_c_ctx, ada_w=v_ada_w, ada_b=v_ada_b, norm_mix=v_norm_mix, norm_ffn=v_norm_ffn, norm_final=v_norm_final, ffn_in=v_ffn_in, ffn_out=v_ffn_out, ab_in=v_ab_in, ab_out=v_ab_out, mla_q_norm=v_mla_q_norm, mla_kv_norm=v_mla_kv_norm, mla_wq_b=v_mla_wq_b, mla_wkv_b=v_mla_wkv_b, cmlp_v_norm=v_cmlp_v_norm, cmlp_ws=v_cmlp_ws, cmlp_bs=v_cmlp_bs, cd_in=v_cd_in, cd_out=v_cd_out, ret_decay_fwd=v_ret_decay_fwd, ret_decay_bwd=v_ret_decay_bwd, ret_norm=v_ret_norm, swa_sink=v_swa_sink)

    B, SEQ, D = x.shape
    CTX = ctx.shape[1]
    lay = Layout(B, SEQ, CTX, D)
    nt, NX = lay.nt, lay.NX
    ix, iy, ic = lax.axis_index("x"), lax.axis_index("y"), lax.axis_index("c")
    chip, me = 2 * ix + iy, 4 * ix + 2 * iy + ic
    cidx = jnp.reshape(ic, (1,)).astype(jnp.int32)
    Wc = ada_w.shape[2]
    n_even, n_odd = ab_in.shape[0], cd_in.shape[0]

    rn_row = jnp.pad(ret_norm.reshape(1, -1), ((0, 0), (0, D - ret_norm.size)))
    pack0 = jnp.concatenate([c, rn_row, jnp.zeros((8 - (B + 1) % 8, D), F32)], axis=0) if (B + 1) % 8 else jnp.concatenate([c, rn_row], axis=0)
    g0 = gather8(pack0, "gather_cond")
    NC = -(-(8 * B + 1) // 16) * 16
    c_all = jnp.concatenate([g0[:, :B].reshape(8 * B, D), c_ctx[None], jnp.zeros((NC - 8 * B - 1, D), F32)], axis=0)
    rn_sh = ret_norm.shape[1]
    ret_norm_full = jnp.transpose(g0[0::2, B, :ret_norm.size].reshape(4, n_odd, rn_sh), (1, 0, 2)).reshape(n_odd, 4 * rn_sh)

    gw = gather_weights([W[n].astype(BF16) for n in _BIG], "gather_weights")
    w_ab_in, w_wq, w_wkv, w_cd_in = (_unshard_cols(gw[i]) for i in (2, 4, 5, 6))
    w_ffn_out, w_ab_out, w_cd_out = (_unshard_rows(gw[i]) for i in (1, 3, 7))
    w_ab_in = _ab_in_permute(w_ab_in)
    w_ffn_out_t = jnp.transpose(w_ffn_out, (0, 2, 1))
    w_qn, w_qp = _split_heads(w_wq, MLA_NOPE)
    w_k, w_v = _split_heads(w_wkv, MLA_NOPE)

    ab_sh = lax.dynamic_slice_in_dim(ada_b, chip * Wc, Wc, axis=1)[:, None, :]
    mod_sh = ada_fwd(c_all, ada_w, ab_sh, "ada_fwd")
    mod_all = _unshard_cols(gather_chips(mod_sh, "gather_mod"))
    mod_mine = jnp.concatenate([lax.dynamic_slice_in_dim(mod_all, me * B, B, axis=1), mod_all[:, 8 * B:8 * B + 1]], axis=1)
    mod = mod_mine.reshape(DEPTH, B + 1, 6, D)

    tabc, tabs = _rope_tables(SEQ)
    bc8 = lambda a: jnp.broadcast_to(a.reshape(a.shape + (1, 1)), a.shape + (8, 128))
    row = lambda a: a.reshape(1, -1)

    S = jnp.concatenate([x.reshape(NX, D), ctx.reshape(B * CTX, D)], axis=0)
    saved = []
    xn = norm_mod_fwd(S, row(norm_mix[0]), mod[0], 0, lay, nt, "norm_mix_fwd0")
    for l in range(DEPTH):
        j, even = l // 2, l % 2 == 0
        if even:
            z = mm(xn, w_ab_in[j], name=f"ab_in{l}")
            q, k, v = ab_prep_fwd(z, tabc, tabs, row(mla_kv_norm[j]), row(mla_q_norm[j]), w_k[j], w_v[j], w_qn[j], w_qp[j],
                                  lay, f"ab_prep{l}")
            o = mla_fwd(q, k, v, lay, f"mla{l}")
            merged = cmlp_merge_fwd(z, o, row(cmlp_v_norm[j]), cmlp_ws[j], cmlp_bs[j][:, :, None], lay, f"cmlp{l}")
            w_out, mix = w_ab_out[j], (q, k, v)
        else:
            z = mm(xn, w_cd_in[j], name=f"cd_in{l}")
            rq, rk, sq, ke, ve = cd_prep_fwd(z, tabc, tabs, lay, f"cd_prep{l}")
            decf, decb, sink = bc8(ret_decay_fwd[j]), bc8(ret_decay_bwd[j]), bc8(swa_sink[j].reshape(SWA_KV_HEADS, SWA_GROUPS))
            yret = ret_fwd(rq, rk, z, decf, decb, lay, f"ret{l}")
            osw = swa_fwd(sq, ke, ve, sink, lay, f"swa{l}")
            merged = cd_merge_fwd(yret, z, osw, row(ret_norm_full[j]), lay, f"cd_merge{l}")
            w_out, mix = w_cd_out[j], (rq, rk, sq, ke, ve, decf, decb, sink, yret)
        y, S_mid, xn2 = mm_gated(merged, w_out, S, mod[l], 2, lay, name=f"mix_out{l}", n_tiles=nt,
                                 norm=(row(norm_ffn[l]), mod[l], 3))
        act, fa, fb = ffn_in_act(xn2, gw[0], l, f"ffn_in{l}")
        nxt_norm = (row(norm_mix[l + 1]), mod[l + 1], 0) if l + 1 < DEPTH else None
        f, S_new, *xn_next = mm_gated(act, w_ffn_out[l], S_mid, mod[l], 5, lay, name=f"ffn_out{l}", n_tiles=nt, norm=nxt_norm)
        saved.append((S, xn, z, mix, merged, w_out, y, S_mid, xn2, (fa, fb), act, f))
        S, xn = S_new, (xn_next[0] if xn_next else None)

    loss_blk, dS, d_norm_final = loss_head(S, row(norm_final), loss_target.reshape(NX, D), lay, "loss_head")
    loss = lax.psum(loss_blk[0, 0], ("x", "y", "c"))

    G = {n: [None] * W[n].shape[0] for n in ("norm_mix", "norm_ffn", "mla_q_norm", "mla_kv_norm", "cmlp_v_norm", "cmlp_ws",
                                             "cmlp_bs", "ret_decay_fwd", "ret_decay_bwd", "ret_norm", "swa_sink")}
    GB = {n: [None] * cnt for n, cnt in (("ffn_in_a", DEPTH), ("ffn_in_b", DEPTH), ("ffn_out", DEPTH), ("ab_in", n_even), ("ab_out", n_even),
                                         ("wqn", n_even), ("wqp", n_even), ("wk", n_even), ("wv", n_even), ("cd_in", n_odd),
                                         ("cd_out", n_odd))}
    dmod = [None] * DEPTH
    df, dgate2 = gate_bwd(dS, saved[-1][-1], mod[DEPTH - 1], 5, lay, nt, f"ffn_gate_bwd{DEPTH - 1}")
    for l in reversed(range(DEPTH)):
        j, even = l // 2, l % 2 == 0
        S_in, xn, z, mix, merged, w_out, y, S_mid, xn2, (fa, fb), act, f = saved[l]
        da, db = ffn_out_dx_act(df, w_ffn_out_t[l], fa, fb, f"ffn_out_dx{l}")
        GB["ffn_out"][l] = mm(act, df, ta=True, name=f"ffn_out_dw{l}")
        GB["ffn_in_a"][l] = mm(xn2, da, ta=True, name=f"ffn_in_dwa{l}", split=2)
        GB["ffn_in_b"][l] = mm(xn2, db, ta=True, name=f"ffn_in_dwb{l}", split=2)
        dxn2 = ffn_in_dx(da, db, gw[0], l, f"ffn_in_dx{l}")
        dS_mid, dss2, dg, dy, dgate1 = norm_mod_bwd(S_mid, row(norm_ffn[l]), mod[l], 3, dxn2, dS, lay, nt, f"norm_ffn_bwd{l}",
                                                    gate=(y, mod[l], 2))
        G["norm_ffn"][l] = dg
        dmerged = mm(dy, w_out, tb=True, name=f"mix_out_dx{l}")
        d_w_out = mm(merged, dy, ta=True, name=f"mix_out_dw{l}")
        if even:
            q, k, v = mix
            dq, dkx, dkh, dvx, dvh = mla_bwd(q, k, v, dmerged, lay, f"mla_bwd{l}")
            (dz, dgkv, dgq, dwk, dwv, dwqn, dwqp, dgvn, dws, dbs) = ab_rows_bwd(
                z, tabc, tabs, dq, dkx, dkh, dvx, dvh, dmerged, row(mla_kv_norm[j]), row(mla_q_norm[j]), w_k[j], w_v[j],
                w_qn[j], w_qp[j], row(cmlp_v_norm[j]), cmlp_ws[j], cmlp_bs[j][:, :, None], lay, f"ab_rows_bwd{l}")
            G["mla_kv_norm"][j], G["mla_q_norm"][j], G["cmlp_v_norm"][j] = dgkv, dgq, dgvn
            G["cmlp_ws"][j], G["cmlp_bs"][j] = dws, dbs
            GB["wk"][j], GB["wv"][j], GB["wqn"][j], GB["wqp"][j], GB["ab_out"][j] = dwk, dwv, dwqn, dwqp, d_w_out
            w_in = w_ab_in[j]
        else:
            rq, rk, sq, ke, ve, decf, decb, sink, yret = mix
            dyret, drg, dgn = cd_merge_bwd(yret, z, row(ret_norm_full[j]), dmerged, lay, f"cd_merge_bwd{l}")
            dqx, dqh, dkx, dkh, dvx, dvh, ddf, ddb = ret_bwd(rq, rk, z, decf, decb, dyret, lay, f"ret_bwd{l}")
            dsq, dkex, dkeh, dvex, dveh, dsink = swa_bwd(sq, ke, ve, sink, dmerged, lay, f"swa_bwd{l}")
            dz = cd_rows_bwd(z, tabc, tabs, dsq, drg, (dqx, dqh), (dkx, dkh), (dkex, dkeh), (dvex, dveh), (dvx, dvh), lay,
                             f"cd_rows_bwd{l}")
            G["ret_norm"][j], G["ret_decay_fwd"][j], G["ret_decay_bwd"][j] = dgn, ddf[:, 0, 0], ddb[:, 0, 0]
            G["swa_sink"][j] = dsink[:, :, 0, 0].reshape(-1)
            GB["cd_out"][j] = d_w_out
            w_in = w_cd_in[j]
        GB["ab_in" if even else "cd_in"][j] = mm(xn, dz, ta=True, name=f"mix_in_dw{l}")
        dxn = mm(dz, w_in, tb=True, name=f"mix_in_dx{l}")
        dmod_l = lambda dss1: jnp.concatenate([dss1, dgate1, dss2, dgate2], axis=1)
        if l > 0:
            dS, dss1, dg, df, dgate2_prev = norm_mod_bwd(S_in, row(norm_mix[l]), mod[l], 0, dxn, dS_mid, lay, nt,
                                                         f"norm_mix_bwd{l}", gate=(saved[l - 1][-1], mod[l - 1], 5))
            dmod[l], dgate2 = dmod_l(dss1), dgate2_prev
        else:
            dS, dss1, dg = norm_mod_bwd(S_in, row(norm_mix[l]), mod[l], 0, dxn, dS_mid, lay, nt, f"norm_mix_bwd{l}")
            dmod[l] = dmod_l(dss1)
        G["norm_mix"][l] = dg
    grad_x = dS[:NX].reshape(B, SEQ, D)

    st = lambda n: jnp.stack([g.reshape((4 * rn_sh,) if n == "ret_norm" else W[n].shape[1:]) for g in G[n]])
    small_parts = {n: st(n) for n in G}
    small_parts["norm_final"] = d_norm_final.reshape(-1)
    dmod_local = jnp.stack(dmod).reshape(DEPTH, B + 1, 6 * D)
    names1 = ["norm_mix", "norm_ffn", "norm_final", "mla_q_norm", "mla_kv_norm", "cmlp_v_norm", "cmlp_ws", "cmlp_bs",
              "ret_decay_fwd", "ret_decay_bwd", "ret_norm", "swa_sink"]
    like1 = [dmod_local] + [small_parts[n] for n in names1]
    g1 = gather8(_pack(like1), "gather_small_grads")
    tot1 = _unpack(sum8(g1, "sum_small_grads"), like1)
    sg = dict(zip(names1, tot1[1:]))
    n_dm = math.prod(dmod_local.shape)
    dm_each = g1[:, :-(-n_dm // 128)].reshape(8, -1)[:, :n_dm].reshape(8, DEPTH, B + 1, 6 * D)
    dmod_all = jnp.concatenate([jnp.transpose(dm_each[:, :, :B], (1, 0, 2, 3)).reshape(DEPTH, 8 * B, 6 * D),
                                tot1[0][:, B:B + 1], jnp.zeros((DEPTH, NC - 8 * B - 1, 6 * D), F32)], axis=1)
    dmod_sh = lax.dynamic_slice_in_dim(dmod_all, chip * Wc, Wc, axis=2)
    g_ada_w, g_ada_b_sh, dcs = ada_bwd(c_all, ada_w, dmod_sh, "ada_bwd")
    like2 = [dcs[8 * B], g_ada_b_sh]
    g2 = gather8(_pack(like2), "gather_ada_grads")
    tot2 = _unpack(sum8(g2, "sum_ada_grads"), like2)
    bc = lambda a: jnp.broadcast_to(a.reshape(1, D), (8, D))
    sg["c_ctx"] = cctx_grad(bc(tot2[0]), bc(c_ctx), "c_ctx_grad")[0]
    off = D + (-D) % 1024
    gab = g2[0::2, off // 128:(off + DEPTH * Wc) // 128].reshape(4, DEPTH, Wc)
    sg["ada_b"] = jnp.transpose(gab, (1, 0, 2)).reshape(DEPTH, 4 * Wc)
    sg["ret_norm"] = lax.dynamic_slice_in_dim(sg["ret_norm"], chip * rn_sh, rn_sh, axis=1)

    stack = lambda n: jnp.stack(GB[n])
    gs = {"ffn_in": jnp.concatenate([jnp.stack(GB[n], axis=1) for n in ("ffn_in_a", "ffn_in_b")], axis=0),
          "ffn_out": _shard_rows(stack("ffn_out")),
          "ab_in": _shard_cols(_ab_in_unpermute(stack("ab_in"))), "ab_out": _shard_rows(stack("ab_out")),
          "mla_wq_b": _shard_cols(_join_heads(stack("wqn"), stack("wqp"))),
          "mla_wkv_b": _shard_cols(_join_heads(stack("wk"), stack("wv"))),
          "cd_in": _shard_cols(stack("cd_in")), "cd_out": _shard_rows(stack("cd_out"))}
    bufs = swap_other_half([gs[n] for n in _BIG], "swap_core_halves")
    parts = [chip_partial(gs[n], b, cidx, f"chip_partial_{n}") for n, b in zip(_BIG, bufs)]
    arrived = exchange_chips(parts, "exchange_chips")
    grads = share_halves([sum_chips(p, cidx, f"sum_chips_{n}") for n, p in zip(_BIG, arrived)], "share_core_halves")
    flat2 = lambda a: a.reshape(-1, a.shape[-1])
    out = {}
    for n, g in zip(_BIG, grads):
        res = adam_rows(flat2(W[n]), flat2(g), flat2(M1[n]), flat2(M2[n]), f"adam_{n}", emit_g=True)
        out[n] = tuple(r.reshape(W[n].shape) for r in res)

    like_s = [W[n] for n in _SMALL]
    dsm, msm, vsm = adam_rows(_pack(like_s), _pack([sg[n].reshape(W[n].shape) for n in _SMALL]), _pack([M1[n] for n in _SMALL]),
                                _pack([M2[n] for n in _SMALL]), "adam_small")
    for n, d_, m_, v_ in zip(_SMALL, _unpack(dsm, like_s), _unpack(msm, like_s), _unpack(vsm, like_s)):
        out[n] = (sg[n].reshape(W[n].shape), d_, m_, v_)
    d_, m_, v_ = adam_rows(flat2(ada_w), flat2(g_ada_w), flat2(m_ada_w), flat2(v_ada_w), "adam_ada_w")
    out["ada_w"] = (g_ada_w, d_.reshape(ada_w.shape), m_.reshape(ada_w.shape), v_.reshape(ada_w.shape))

    return (loss, grad_x, *[out[n][0] for n in _WEIGHTS], *[out[n][1] for n in _WEIGHTS], *[out[n][2] for n in _WEIGHTS],
            *[out[n][3] for n in _WEIGHTS])
```

```python
import functools
import math

import jax
import jax.numpy as jnp
import numpy as np
from jax import lax
from jax.experimental import pallas as pl
from jax.experimental.pallas import tpu as pltpu

F32 = jnp.float32
BF16 = jnp.bfloat16
EPS = 1e-6
NEG_INF = -1e30
GRID_W = 64
ROPE_THETA = 10000.0
DEPTH = 4
MLA_HEADS, MLA_Q_LORA, MLA_KV_LORA, MLA_NOPE, MLA_ROPE, MLA_V = 4, 256, 256, 128, 64, 128
CMLP_GROUPS, CMLP_CHUNK = 4, 128
CMLP_WIDTH = 512
RET_HEADS, RET_QK, RET_V = 4, 64, 128
SWA_Q_HEADS, SWA_KV_HEADS, SWA_HEAD_DIM, SWA_WINDOW = 8, 2, 64, 128
SWA_GROUPS = SWA_Q_HEADS // SWA_KV_HEADS
AB_IN_P = 1664
ADAM_LR, ADAM_B1, ADAM_B2, ADAM_EPS, ADAM_WD, ADAM_STEP = 0.001, 0.9, 0.999, 1e-08, 0.01, 10

T = 256
SWA_SPAN = T + 2 * SWA_WINDOW
VMEM_LIMIT = 48 * 1024 * 1024
MESH = pl.DeviceIdType.MESH
ANY = pl.BlockSpec(memory_space=pl.ANY)


def _cparams(sem):
    return pltpu.CompilerParams(dimension_semantics=sem, vmem_limit_bytes=VMEM_LIMIT)


@functools.cache
def _bdot_fn(ca, cb):
    fa, fb = 1 - ca, 1 - cb

    def dg(p, q, cp, cq):
        return lax.dot_general(p.astype(BF16), q.astype(BF16), (((cp,), (cq,)), ((), ())), preferred_element_type=F32)

    @jax.custom_vjp
    def bd(a, b):
        return dg(a, b, ca, cb)

    def fwd(a, b):
        return dg(a, b, ca, cb), (a, b)

    def bwd(res, g):
        a, b = res
        da = dg(g, b, 1, fb) if ca == 1 else dg(b, g, fb, 1)
        db = dg(a, g, fa, 0) if cb == 0 else dg(g, a, 0, fa)
        return da, db

    bd.defvjp(fwd, bwd)
    return bd


def bdot(a, b):
    return _bdot_fn(1, 0)(a, b)


def bdot_nt(a, b):
    return _bdot_fn(1, 1)(a, b)


def bdot_tn(a, b):
    return _bdot_fn(0, 0)(a, b)


def _swap32(x):
    w = x.shape[-1]
    lane = lax.broadcasted_iota(jnp.int32, x.shape, 1)
    return jnp.where((lane & 32) == 0, pltpu.roll(x, w - 32, 1), pltpu.roll(x, 32, 1))


@jax.custom_vjp
def rope(x, c, s):
    return x * c + _swap32(x) * s


def _rope_fwd(x, c, s):
    return rope(x, c, s), (c, s)


def _rope_bwd(res, g):
    c, s = res
    return g * c + _swap32(g * s), jnp.zeros_like(c), jnp.zeros_like(s)


rope.defvjp(_rope_fwd, _rope_bwd)


def rms(x, g):
    return x * lax.rsqrt(jnp.mean(x * x, axis=-1, keepdims=True) + EPS) * g


def normmod(x, g, sh, sc):
    return rms(x, g) * (1.0 + sc) + sh


def log_sigmoid(x):
    return jnp.minimum(x, 0.0) - jnp.log(1.0 + jnp.exp(-jnp.abs(x)))


def _head_mask(shape, h):
    lane = lax.broadcasted_iota(jnp.int32, shape, 1)
    return ((lane >> 6) == h).astype(F32)


def _fold_matrix():
    i = lax.broadcasted_iota(jnp.int32, (256, 128), 0)
    j = lax.broadcasted_iota(jnp.int32, (256, 128), 1)
    return ((i & 63) == j).astype(F32)


def _expand_matrix(g):
    i = lax.broadcasted_iota(jnp.int32, (128, 256), 0)
    j = lax.broadcasted_iota(jnp.int32, (128, 256), 1)
    return (i == (j & 63) + 64 * g).astype(F32)


def _acc(ref, val, first):
    @pl.when(first)
    def _():
        ref[...] = val

    @pl.when(jnp.logical_not(first))
    def _():
        ref[...] += val


def _pick(n, cap, mult):
    best = None
    for d in range(mult, min(n, cap) + 1, mult):
        if n % d == 0:
            best = d
    return best if best is not None else n


def mm(a, b, *, ta=False, tb=False, name, split=1):
    M, K = (a.shape[1], a.shape[0]) if ta else a.shape
    N = b.shape[0] if tb else b.shape[1]
    Ns = N // split
    assert split == 1 or (ta and Ns % 128 == 0)
    if ta:
        tn = N
        tm = _pick(M, min(1536, (3 << 20) // tn), 128)
    else:
        tn = _pick(N, 768, 128)
        if tn < 256 and N <= 2304:
            tn = N
        tm = _pick(M, min(1536, (1 << 20) // tn), 128)
    tk = _pick(K, 2048 if not ta else 1024, 128) if K > 2816 or ta else K
    nk = K // tk
    grid = (M // tm, N // tn, nk)
    a_spec = pl.BlockSpec((tk, tm), lambda i, j, k: (k, i)) if ta else pl.BlockSpec((tm, tk), lambda i, j, k: (i, k))
    b_spec = pl.BlockSpec((tn, tk), lambda i, j, k: (j, k)) if tb else pl.BlockSpec((tk, tn), lambda i, j, k: (k, j))
    dims = (((0 if ta else 1,), (1 if tb else 0,)), ((), ()))

    def body(a_ref, b_ref, o_ref):
        part = lax.dot_general(a_ref[...], b_ref[...], dims, preferred_element_type=F32)
        first = pl.program_id(2) == 0
        if split > 1:
            for s in range(split):
                _acc(o_ref.at[s], part[:, s * Ns:(s + 1) * Ns], first)
        elif nk == 1:
            o_ref[...] = part
        else:
            _acc(o_ref, part, first)

    if split > 1:
        out_spec, out_shape = pl.BlockSpec((split, tm, Ns), lambda i, j, k: (0, i, 0)), (split, M, Ns)
    else:
        out_spec, out_shape = pl.BlockSpec((tm, tn), lambda i, j, k: (i, j)), (M, N)
    return pl.pallas_call(
        body, grid=grid, in_specs=[a_spec, b_spec], out_specs=out_spec, out_shape=jax.ShapeDtypeStruct(out_shape, F32),
        compiler_params=_cparams(("parallel", "parallel", "arbitrary")), name=name)(a, b)


def mm_gated(a, b, res, mod, gate_row, lay, *, name, n_tiles, norm=None):
    K, N = b.shape
    M = n_tiles * T

    def body(*refs):
        a_ref, b_ref, r_ref, g_ref = refs[:4]
        y = lax.dot_general(a_ref[...], b_ref[...], (((1,), (0,)), ((), ())), preferred_element_type=F32)
        new = r_ref[...] + g_ref[gate_row:gate_row + 1, :] * y
        if norm is None:
            y_ref, o_ref = refs[4:]
        else:
            gn_ref, mn_ref, y_ref, o_ref, xn_ref = refs[4:]
            k0 = norm[2]
            xn_ref[...] = normmod(new, gn_ref[...], mn_ref[k0:k0 + 1, :], mn_ref[k0 + 1:k0 + 2, :]).astype(BF16)
        y_ref[...] = y.astype(BF16)
        o_ref[...] = new

    rows = pl.BlockSpec((T, N), lambda i: (i, 0))
    extra = [] if norm is None else [_full((1, N)), _modspec(lay, 6, N)]
    return pl.pallas_call(
        body, grid=(n_tiles,),
        in_specs=[pl.BlockSpec((T, K), lambda i: (i, 0)), _full((K, N)), rows, _modspec(lay, 6, N)] + extra,
        out_specs=[rows, rows] + ([] if norm is None else [rows]),
        out_shape=[jax.ShapeDtypeStruct((M, N), BF16), jax.ShapeDtypeStruct((M, N), F32)]
        + ([] if norm is None else [jax.ShapeDtypeStruct((M, N), BF16)]),
        compiler_params=_cparams(("parallel",)), name=name)(a, b, res, mod, *([] if norm is None else norm[:2]))


class Layout:
    def __init__(self, B, SEQ, CTX, D):
        assert CTX == T and SEQ % T == 0 and SEQ >= SWA_SPAN
        self.B, self.SEQ, self.CTX, self.D = B, SEQ, CTX, D
        self.tps = SEQ // T
        self.nxt = B * self.tps
        self.nt = self.nxt + B
        self.NX, self.R = B * SEQ, B * SEQ + B * CTX
        self.nq = self.tps + 1

    def mod_idx(self, i):
        return jnp.where(i < self.nxt, i // self.tps, self.B)

    def rope_idx(self, i):
        return jnp.where(i < self.nxt, i % self.tps, self.tps)

    def first_of_mod(self, i):
        return jnp.logical_or(jnp.logical_and(i < self.nxt, i % self.tps == 0), i == self.nxt)

    def qrow(self, b, qi):
        return jnp.where(qi < self.tps, b * self.tps + qi, self.nxt + b)


def _row(w, col=0):
    return pl.BlockSpec((T, w), lambda i: (i, col))


def _full(shape):
    nd = len(shape)
    return pl.BlockSpec(shape, lambda i: (0,) * nd)


def _modspec(lay, rows, D):
    return pl.BlockSpec((None, rows, D), lambda i: (lay.mod_idx(i), 0, 0))


def _ropespec(lay, w):
    return pl.BlockSpec((T, w), lambda i: (lay.rope_idx(i), 0))


def _xh_specs(lay, w):
    nxt = lay.nxt
    return [pl.BlockSpec((T, w), lambda i: (jnp.minimum(i, nxt - 1), 0)), pl.BlockSpec((T, w), lambda i: (jnp.maximum(i - nxt, 0), 0))]


def _xh_pick(lay, x_ref, h_ref):
    return jnp.where(pl.program_id(0) < lay.nxt, x_ref[...], h_ref[...])


def norm_mod_fwd(S, g, mod, k0, lay, n_tiles, name):
    D = S.shape[1]

    def body(s_ref, g_ref, mod_ref, o_ref):
        o_ref[...] = normmod(s_ref[...], g_ref[...], mod_ref[k0:k0 + 1, :], mod_ref[k0 + 1:k0 + 2, :]).astype(BF16)

    return pl.pallas_call(
        body, grid=(n_tiles,), in_specs=[_row(D), _full((1, D)), _modspec(lay, 6, D)], out_specs=_row(D),
        out_shape=jax.ShapeDtypeStruct((n_tiles * T, D), BF16), compiler_params=_cparams(("parallel",)), name=name)(S, g, mod)


def norm_mod_bwd(S, g, mod, k0, dxn, ds_in, lay, n_tiles, name, gate=None):
    D = S.shape[1]

    def body(*refs):
        s_ref, g_ref, mod_ref, dxn_ref, dsin_ref = refs[:5]
        i = pl.program_id(0)
        _, vjp = jax.vjp(normmod, s_ref[...], g_ref[...], mod_ref[k0:k0 + 1, :], mod_ref[k0 + 1:k0 + 2, :])
        dx, dg, dsh, dsc = vjp(dxn_ref[...])
        ds = dsin_ref[...] + dx
        if gate is None:
            ds_ref, dss_ref, dg_ref = refs[5:]
        else:
            y_ref, gmod_ref, ds_ref, dss_ref, dg_ref, dy_ref, dgate_ref = refs[5:]
            row = gate[2]
            dy_ref[...] = (gmod_ref[row:row + 1, :] * ds).astype(BF16)
            _acc(dgate_ref, jnp.sum(ds * y_ref[...], axis=0, keepdims=True), lay.first_of_mod(i))
        ds_ref[...] = ds
        _acc(dg_ref, dg, i == 0)
        _acc(dss_ref, jnp.concatenate([dsh, dsc], axis=0), lay.first_of_mod(i))

    R_ = n_tiles * T
    gated = gate is not None
    return pl.pallas_call(
        body, grid=(n_tiles,),
        in_specs=[_row(D), _full((1, D)), _modspec(lay, 6, D), _row(D), _row(D)] + ([_row(D), _modspec(lay, 6, D)] if gated else []),
        out_specs=[_row(D), _modspec(lay, 2, D), _full((1, D))] + ([_row(D), _modspec(lay, 1, D)] if gated else []),
        out_shape=[jax.ShapeDtypeStruct((R_, D), F32), jax.ShapeDtypeStruct((lay.B + 1, 2, D), F32),
                   jax.ShapeDtypeStruct((1, D), F32)]
        + ([jax.ShapeDtypeStruct((R_, D), BF16), jax.ShapeDtypeStruct((lay.B + 1, 1, D), F32)] if gated else []),
        compiler_params=_cparams(("arbitrary",)), name=name)(S, g, mod, dxn, ds_in, *(gate[:2] if gated else []))


def gate_bwd(dS, y, mod, gate_row, lay, n_tiles, name):
    D = dS.shape[1]

    def body(ds_ref, y_ref, mod_ref, dy_ref, dgate_ref):
        i = pl.program_id(0)
        ds = ds_ref[...]
        dy_ref[...] = (mod_ref[gate_row:gate_row + 1, :] * ds).astype(BF16)
        _acc(dgate_ref, jnp.sum(ds * y_ref[...], axis=0, keepdims=True), lay.first_of_mod(i))

    return pl.pallas_call(
        body, grid=(n_tiles,), in_specs=[_row(D), _row(D), _modspec(lay, 6, D)],
        out_specs=[_row(D), _modspec(lay, 1, D)],
        out_shape=[jax.ShapeDtypeStruct((n_tiles * T, D), BF16), jax.ShapeDtypeStruct((lay.B + 1, 1, D), F32)],
        compiler_params=_cparams(("arbitrary",)), name=name)(dS, y, mod)


def _swiglu(a, b):
    return a * jax.nn.sigmoid(a) * b


def _ffn_tiles(M, F):
    tn = _pick(F, 1408, 128)
    return _pick(M, (3 << 18) // tn, 128), tn


def ffn_in_act(x, ws, l, name):
    M, D = x.shape
    Cs = ws.shape[3]
    tm = _pick(M, (3 << 18) // Cs, 128)

    def body(x_ref, wa_ref, wb_ref, act_ref, a_ref, b_ref):
        dims = (((1,), (0,)), ((), ()))
        a = lax.dot_general(x_ref[...], wa_ref[...], dims, preferred_element_type=F32)
        b = lax.dot_general(x_ref[...], wb_ref[...], dims, preferred_element_type=F32)
        act_ref[...] = _swiglu(a, b).astype(BF16)
        a_ref[...] = a.astype(BF16)
        b_ref[...] = b.astype(BF16)

    out = pl.BlockSpec((tm, Cs), lambda i, j: (i, j))
    return pl.pallas_call(
        body, grid=(M // tm, 2),
        in_specs=[pl.BlockSpec((tm, D), lambda i, j: (i, 0)), pl.BlockSpec((None, None, D, Cs), lambda i, j: (j, l, 0, 0)),
                  pl.BlockSpec((None, None, D, Cs), lambda i, j: (2 + j, l, 0, 0))],
        out_specs=[out, out, out], out_shape=[jax.ShapeDtypeStruct((M, 2 * Cs), BF16)] * 3,
        compiler_params=_cparams(("parallel", "parallel")), name=name)(x, ws, ws)


def ffn_out_dx_act(df, wt, a, b, name):
    M, D = df.shape
    F = wt.shape[1]
    tm, tn = _ffn_tiles(M, F)

    def body(df_ref, w_ref, a_ref, b_ref, da_ref, db_ref):
        dact = lax.dot_general(df_ref[...], w_ref[...], (((1,), (0,)), ((), ())), preferred_element_type=F32)
        _, vjp = jax.vjp(_swiglu, a_ref[...].astype(F32), b_ref[...].astype(F32))
        da, db = vjp(dact)
        da_ref[...] = da.astype(BF16)
        db_ref[...] = db.astype(BF16)

    blk = pl.BlockSpec((tm, tn), lambda i, j: (i, j))
    return pl.pallas_call(
        body, grid=(M // tm, F // tn),
        in_specs=[pl.BlockSpec((tm, D), lambda i, j: (i, 0)), pl.BlockSpec((D, tn), lambda i, j: (0, j)), blk, blk],
        out_specs=[blk, blk], out_shape=[jax.ShapeDtypeStruct((M, F), BF16)] * 2,
        compiler_params=_cparams(("parallel", "parallel")), name=name)(df, wt, a, b)


def ffn_in_dx(da, db, ws, l, name):
    M, F = da.shape
    D, Cs = ws.shape[2], ws.shape[3]
    tm, tn = _pick(M, 512, 128), _pick(D, 512, 128)

    def body(da_ref, db_ref, w0_ref, w1_ref, w2_ref, w3_ref, o_ref):
        dims = (((1,), (1,)), ((), ()))
        dot = lambda g_ref, s, w_ref: lax.dot_general(g_ref[:, s * Cs:(s + 1) * Cs], w_ref[...], dims, preferred_element_type=F32)
        o_ref[...] = (dot(da_ref, 0, w0_ref) + dot(da_ref, 1, w1_ref)) + (dot(db_ref, 0, w2_ref) + dot(db_ref, 1, w3_ref))

    shard = lambda s: pl.BlockSpec((None, None, tn, Cs), lambda i, j: (s, l, j, 0))
    return pl.pallas_call(
        body, grid=(M // tm, D // tn),
        in_specs=[pl.BlockSpec((tm, F), lambda i, j: (i, 0)), pl.BlockSpec((tm, F), lambda i, j: (i, 0))]
        + [shard(s) for s in range(4)],
        out_specs=pl.BlockSpec((tm, tn), lambda i, j: (i, j)), out_shape=jax.ShapeDtypeStruct((M, D), F32),
        compiler_params=_cparams(("parallel", "parallel")), name=name)(da, db, ws, ws, ws, ws)


def loss_head(S, g, target, lay, name):
    D = S.shape[1]
    nxt = lay.nxt

    def tile_loss(x, gg, t):
        err = rms(x, gg) - t
        return 0.5 * jnp.sum(jnp.mean(err * err, axis=-1))

    def body(s_ref, g_ref, t_ref, loss_ref, ds_ref, dg_ref):
        i = pl.program_id(0)

        @pl.when(i < nxt)
        def _():
            val, vjp = jax.vjp(tile_loss, s_ref[...], g_ref[...], t_ref[...])
            dx, dg, _ = vjp(jnp.ones((), F32))
            ds_ref[...] = dx
            _acc(dg_ref, dg, i == 0)
            _acc(loss_ref, jnp.full((8, 128), val, F32), i == 0)

        @pl.when(i >= nxt)
        def _():
            ds_ref[...] = jnp.zeros((T, D), F32)

    return pl.pallas_call(
        body, grid=(lay.nt,),
        in_specs=[_row(D), _full((1, D)), pl.BlockSpec((T, D), lambda i: (jnp.minimum(i, nxt - 1), 0))],
        out_specs=[_full((8, 128)), _row(D), _full((1, D))],
        out_shape=[jax.ShapeDtypeStruct((8, 128), F32), jax.ShapeDtypeStruct((lay.R, D), F32),
                   jax.ShapeDtypeStruct((1, D), F32)],
        compiler_params=_cparams(("arbitrary",)), name=name)(S, g, target)


def _ab_prep(zkv, zq, zpe, c256, s256, c128, s128, gkv, gq, wk, wv, wqn, wqp):
    kvn = rms(zkv, gkv)
    kn, v = bdot(kvn, wk), bdot(kvn, wv)
    qn = rms(zq, gq)
    qnope, qpe = bdot(qn, wqn), rope(bdot(qn, wqp), c256, s256)
    kpe = rope(zpe, c128, s128)
    fold = _fold_matrix()
    qparts, kparts = [], []
    for h in range(MLA_HEADS):
        qparts += [qnope[:, 128 * h:128 * (h + 1)], bdot(qpe * _head_mask(qpe.shape, h), fold)]
        kparts += [kn[:, 128 * h:128 * (h + 1)], kpe]
    return jnp.concatenate(qparts, axis=1), jnp.concatenate(kparts, axis=1), v


def _ab_prep_specs(lay):
    return [_row(256, 0), _row(256, 1), _row(128, 12), _ropespec(lay, 256), _ropespec(lay, 256), _ropespec(lay, 128),
            _ropespec(lay, 128), _full((1, 256)), _full((1, 256)), _full((256, 512)), _full((256, 512)),
            _full((256, 512)), _full((256, 256))]


def ab_prep_fwd(z, tabc, tabs, gkv, gq, wk, wv, wqn, wqp, lay, name):
    def body(*refs):
        ins, (q_ref, k_ref, v_ref) = refs[:13], refs[13:]
        q, k, v = _ab_prep(*[r[...].astype(F32) for r in ins])
        q_ref[...] = q.astype(BF16)
        k_ref[...] = k.astype(BF16)
        v_ref[...] = v.astype(BF16)

    R = lay.R
    return pl.pallas_call(
        body, grid=(lay.nt,), in_specs=_ab_prep_specs(lay), out_specs=[_row(1024), _row(1024), _row(512)],
        out_shape=[jax.ShapeDtypeStruct((R, 1024), BF16), jax.ShapeDtypeStruct((R, 1024), BF16),
                   jax.ShapeDtypeStruct((R, 512), BF16)],
        compiler_params=_cparams(("parallel",)), name=name)(z, z, z, tabc, tabs, tabc, tabs, gkv, gq, wk, wv, wqn, wqp)


_MLA_SCALE = (MLA_NOPE + MLA_ROPE) ** -0.5


def _mla_x(q, kx, vx, kh, vh):
    sx, sh = bdot_nt(q, kx) * _MLA_SCALE, bdot_nt(q, kh) * _MLA_SCALE
    m = lax.stop_gradient(jnp.maximum(jnp.max(sx, axis=-1, keepdims=True), jnp.max(sh, axis=-1, keepdims=True)))
    ex, eh = jnp.exp(sx - m), jnp.exp(sh - m)
    inv = 1.0 / (jnp.sum(ex, axis=-1, keepdims=True) + jnp.sum(eh, axis=-1, keepdims=True))
    return (bdot(ex, vx) + bdot(eh, vh)) * inv


def _mla_h(q, kh, vh):
    sh = bdot_nt(q, kh) * _MLA_SCALE
    eh = jnp.exp(sh - lax.stop_gradient(jnp.max(sh, axis=-1, keepdims=True)))
    return bdot(eh, vh) * (1.0 / jnp.sum(eh, axis=-1, keepdims=True))


def _mla_specs(lay):
    nxt, SEQ = lay.nxt, lay.SEQ
    return [pl.BlockSpec((T, 256), lambda b, h, qi: (lay.qrow(b, qi), h)),
            pl.BlockSpec((SEQ, 256), lambda b, h, qi: (b, h)), pl.BlockSpec((SEQ, 128), lambda b, h, qi: (b, h)),
            pl.BlockSpec((T, 256), lambda b, h, qi: (nxt + b, h)), pl.BlockSpec((T, 128), lambda b, h, qi: (nxt + b, h))]


def mla_fwd(q, k, v, lay, name):
    tps = lay.tps

    def body(q_ref, kx_ref, vx_ref, kh_ref, vh_ref, o_ref):
        qi = pl.program_id(2)
        f = lambda r: r[...]

        @pl.when(qi < tps)
        def _():
            o_ref[...] = _mla_x(f(q_ref), f(kx_ref), f(vx_ref), f(kh_ref), f(vh_ref)).astype(BF16)

        @pl.when(qi == tps)
        def _():
            o_ref[...] = _mla_h(f(q_ref), f(kh_ref), f(vh_ref)).astype(BF16)

    return pl.pallas_call(
        body, grid=(lay.B, MLA_HEADS, lay.nq), in_specs=_mla_specs(lay),
        out_specs=pl.BlockSpec((T, 128), lambda b, h, qi: (lay.qrow(b, qi), h)),
        out_shape=jax.ShapeDtypeStruct((lay.R, 512), BF16),
        compiler_params=_cparams(("parallel", "parallel", "arbitrary")), name=name)(q, k, v, k, v)


def mla_bwd(q, k, v, dmerged, lay, name):
    tps, SEQ, B = lay.tps, lay.SEQ, lay.B

    def body(q_ref, kx_ref, vx_ref, kh_ref, vh_ref, do_ref, dq_ref, dkx_ref, dkh_ref, dvx_ref, dvh_ref):
        qi = pl.program_id(2)
        f = lambda r: r[...].astype(F32)

        @pl.when(qi < tps)
        def _():
            _, vjp = jax.vjp(_mla_x, f(q_ref), f(kx_ref), f(vx_ref), f(kh_ref), f(vh_ref))
            dq, dkx, dvx, dkh, dvh = vjp(do_ref[...])
            dq_ref[...] = dq
            _acc(dkx_ref, dkx, qi == 0)
            _acc(dvx_ref, dvx, qi == 0)
            _acc(dkh_ref, dkh, qi == 0)
            _acc(dvh_ref, dvh, qi == 0)

        @pl.when(qi == tps)
        def _():
            _, vjp = jax.vjp(_mla_h, f(q_ref), f(kh_ref), f(vh_ref))
            dq, dkh, dvh = vjp(do_ref[...])
            dq_ref[...] = dq
            dkh_ref[...] += dkh
            dvh_ref[...] += dvh

    return pl.pallas_call(
        body, grid=(B, MLA_HEADS, lay.nq),
        in_specs=_mla_specs(lay) + [pl.BlockSpec((T, 128), lambda b, h, qi: (lay.qrow(b, qi), h))],
        out_specs=[pl.BlockSpec((T, 256), lambda b, h, qi: (lay.qrow(b, qi), h)),
                   pl.BlockSpec((SEQ, 256), lambda b, h, qi: (b, h)), pl.BlockSpec((T, 256), lambda b, h, qi: (b, h)),
                   pl.BlockSpec((SEQ, 128), lambda b, h, qi: (b, h)), pl.BlockSpec((T, 128), lambda b, h, qi: (b, h))],
        out_shape=[jax.ShapeDtypeStruct((lay.R, 1024), F32), jax.ShapeDtypeStruct((lay.NX, 1024), F32),
                   jax.ShapeDtypeStruct((B * T, 1024), F32), jax.ShapeDtypeStruct((lay.NX, 512), F32),
                   jax.ShapeDtypeStruct((B * T, 512), F32)],
        compiler_params=_cparams(("parallel", "parallel", "arbitrary")), name=name)(q, k, v, k, v, dmerged)


def _cmlp_piece(zu, zv, g, ws, bs):
    u, v = jax.nn.gelu(zu), jax.nn.gelu(zv)
    v = v * lax.rsqrt(jnp.mean(v * v, axis=-1, keepdims=True) + EPS) * g
    return u * (bdot(ws, v) + bs)


def _pieces():
    return [(c, g) for c in range(T // CMLP_CHUNK) for g in range(CMLP_GROUPS)]


def cmlp_merge_fwd(z, o, gvn, ws, bs, lay, name):
    def body(zu_ref, zv_ref, o_ref, g_ref, ws_ref, bs_ref, m_ref):
        m_ref[:, 0:512] = o_ref[...]
        for c, g in _pieces():
            rows, cols = slice(128 * c, 128 * (c + 1)), slice(128 * g, 128 * (g + 1))
            piece = _cmlp_piece(zu_ref[rows, cols], zv_ref[rows, cols], g_ref[:, cols], ws_ref[g], bs_ref[g])
            m_ref[rows, 512 + 128 * g:512 + 128 * (g + 1)] = piece.astype(BF16)

    return pl.pallas_call(
        body, grid=(lay.nt,),
        in_specs=[_row(512, 1), _row(512, 2), _row(512), _full((1, 512)), _full((4, 128, 128)), _full((4, 128, 1))],
        out_specs=_row(1024), out_shape=jax.ShapeDtypeStruct((lay.R, 1024), BF16),
        compiler_params=_cparams(("parallel",)), name=name)(z, z, o, gvn, ws, bs)


def ab_rows_bwd(z, tabc, tabs, dq, dkx, dkh, dvx, dvh, dmerged, gkv, gq, wk, wv, wqn, wqp, gvn, ws, bs, lay, name):
    def body(*refs):
        prep_in = refs[:3] + refs[5:9] + refs[11:17]
        zu_ref, zv_ref = refs[3:5]
        dq_ref, dcm_ref = refs[9:11]
        gvn_ref, ws_ref, bs_ref = refs[17:20]
        dkx_ref, dkh_ref, dvx_ref, dvh_ref = refs[20:24]
        dz_ref, dgkv_ref, dgq_ref, dwk_ref, dwv_ref, dwqn_ref, dwqp_ref, dgvn_ref, dws_ref, dbs_ref = refs[24:]
        first = pl.program_id(0) == 0
        _, vjp = jax.vjp(_ab_prep, *[r[...].astype(F32) for r in prep_in])
        d = vjp((dq_ref[...], _xh_pick(lay, dkx_ref, dkh_ref), _xh_pick(lay, dvx_ref, dvh_ref)))
        dz_ref[:, 0:256] = d[0].astype(BF16)
        dz_ref[:, 256:512] = d[1].astype(BF16)
        dz_ref[:, 1536:1664] = d[2].astype(BF16)
        for ref, val in zip((dgkv_ref, dgq_ref, dwk_ref, dwv_ref, dwqn_ref, dwqp_ref), d[7:]):
            _acc(ref, val, first)
        dws = [0.0] * CMLP_GROUPS
        dbs = [0.0] * CMLP_GROUPS
        dgv = [0.0] * CMLP_GROUPS
        for c, g in _pieces():
            rows, cols = slice(128 * c, 128 * (c + 1)), slice(128 * g, 128 * (g + 1))
            _, vjp = jax.vjp(_cmlp_piece, zu_ref[rows, cols], zv_ref[rows, cols], gvn_ref[:, cols], ws_ref[g], bs_ref[g])
            dzu, dzv, dg_, dws_, dbs_ = vjp(dcm_ref[rows, cols])
            dz_ref[rows, 512 + 128 * g:512 + 128 * (g + 1)] = dzu.astype(BF16)
            dz_ref[rows, 1024 + 128 * g:1024 + 128 * (g + 1)] = dzv.astype(BF16)
            dws[g], dbs[g], dgv[g] = dws[g] + dws_, dbs[g] + dbs_, dgv[g] + dg_
        _acc(dgvn_ref, jnp.concatenate(dgv, axis=1), first)
        _acc(dws_ref, jnp.stack(dws), first)
        _acc(dbs_ref, jnp.stack(dbs), first)

    acc_shapes = [(1, 256), (1, 256), (256, 512), (256, 512), (256, 512), (256, 256), (1, 512), (4, 128, 128), (4, 128, 1)]
    return pl.pallas_call(
        body, grid=(lay.nt,),
        in_specs=_ab_prep_specs(lay)[:3] + [_row(512, 1), _row(512, 2)] + _ab_prep_specs(lay)[3:7]
        + [_row(1024), _row(512, 1)] + _ab_prep_specs(lay)[7:]
        + [_full((1, 512)), _full((4, 128, 128)), _full((4, 128, 1))] + _xh_specs(lay, 1024) + _xh_specs(lay, 512),
        out_specs=[_row(AB_IN_P)] + [_full(s) for s in acc_shapes],
        out_shape=[jax.ShapeDtypeStruct((lay.R, AB_IN_P), BF16)] + [jax.ShapeDtypeStruct(s, F32) for s in acc_shapes],
        compiler_params=_cparams(("arbitrary",)), name=name)(
            z, z, z, z, z, tabc, tabs, tabc, tabs, dq, dmerged, gkv, gq, wk, wv, wqn, wqp, gvn, ws, bs, dkx, dkh, dvx, dvh)


def _cd_prep(zrk, zrq, zsk, zsq0, zsq1, zsv, c256, s256, c128, s128):
    rk = rope(zrk * (RET_QK ** -0.5), c256, s256)
    rq = rope(zrq, c256, s256)
    sk = rope(zsk, c128, s128)
    sq0, sq1 = rope(zsq0, c256, s256), rope(zsq1, c256, s256)
    e0, e1 = _expand_matrix(0), _expand_matrix(1)
    return rq, rk, sq0, sq1, bdot(sk, e0), bdot(sk, e1), bdot(zsv, e0), bdot(zsv, e1)


def _cd_prep_specs(lay):
    return [_row(256, 0), _row(256, 4), _row(128, 6), _row(256, 7), _row(256, 8), _row(128, 7),
            _ropespec(lay, 256), _ropespec(lay, 256), _ropespec(lay, 128), _ropespec(lay, 128)]


def cd_prep_fwd(z, tabc, tabs, lay, name):
    def body(*refs):
        ins, (rq_ref, rk_ref, sq_ref, ke_ref, ve_ref) = refs[:10], refs[10:]
        rq, rk, sq0, sq1, k0, k1, v0, v1 = _cd_prep(*[r[...] for r in ins])
        rq_ref[...] = rq.astype(BF16)
        rk_ref[...] = rk.astype(BF16)
        for ref, (a, b) in ((sq_ref, (sq0, sq1)), (ke_ref, (k0, k1)), (ve_ref, (v0, v1))):
            ref[:, 0:256] = a.astype(BF16)
            ref[:, 256:512] = b.astype(BF16)

    R = lay.R
    return pl.pallas_call(
        body, grid=(lay.nt,), in_specs=_cd_prep_specs(lay),
        out_specs=[_row(256), _row(256), _row(512), _row(512), _row(512)],
        out_shape=[jax.ShapeDtypeStruct((R, w), BF16) for w in (256, 256, 512, 512, 512)],
        compiler_params=_cparams(("parallel",)), name=name)(z, z, z, z, z, z, tabc, tabs, tabc, tabs)


def _ret_sample(h, qs, ks, vs, df, db):
    lgf, lgb = log_sigmoid(df), log_sigmoid(db)
    idx = lax.broadcasted_iota(jnp.int32, (T, 1), 0).astype(F32)
    diff = idx - lax.broadcasted_iota(jnp.int32, (1, T), 1).astype(F32)
    intra = (jnp.where(diff >= 0, jnp.exp(lgf * jnp.maximum(diff, 0.0)), 0.0)
             + jnp.where(diff <= 0, jnp.exp(lgb * jnp.maximum(-diff, 0.0)), 0.0))
    qdf, kdf, cdf = jnp.exp(lgf * (idx + 1.0)), jnp.exp(lgf * (T - 1.0 - idx)), jnp.exp(lgf * T)
    qdb, kdb, cdb = jnp.exp(lgb * (T - idx)), jnp.exp(lgb * idx), jnp.exp(lgb * T)
    mask = _head_mask(qs[0].shape, h)
    qs = [q * mask for q in qs]
    ys = [bdot(bdot_nt(q, k) * intra, v) for q, k, v in zip(qs, ks, vs)]
    n = len(qs)
    state = bdot_tn(ks[0] * kdf, vs[0])
    for i in range(1, n):
        ys[i] = ys[i] + bdot(qs[i] * qdf, state)
        if i + 1 < n:
            state = state * cdf + bdot_tn(ks[i] * kdf, vs[i])
    state = bdot_tn(ks[0] * kdb, vs[0])
    for i in range(n - 1, 0, -1):
        ys[i] = ys[i] + bdot(qs[i] * qdb, state)
        if i > 1:
            state = state * cdb + bdot_tn(ks[i] * kdb, vs[i])
    return ys


def _ret_specs(lay):
    nxt, SEQ = lay.nxt, lay.SEQ
    xs = lambda w, col: pl.BlockSpec((SEQ, w), lambda b, h: (b, col(h)))
    hs = lambda w, col: pl.BlockSpec((T, w), lambda b, h: (nxt + b, col(h)))
    zero, head = (lambda h: 0), (lambda h: 2 + h)
    dec = pl.BlockSpec((None, 8, 128), lambda b, h: (h, 0, 0))
    return [xs(256, zero), hs(256, zero), xs(256, zero), hs(256, zero), xs(128, head), hs(128, head), dec, dec]


def _ret_tiles(x_ref, h_ref, tps, cast=None):
    tiles = [h_ref[...]] + [x_ref[i * T:(i + 1) * T, :] for i in range(tps)]
    return [t.astype(cast) for t in tiles] if cast is not None else tiles


def ret_fwd(rq, rk, z, decf, decb, lay, name):
    tps, SEQ = lay.tps, lay.SEQ

    def body(qx_ref, qh_ref, kx_ref, kh_ref, vx_ref, vh_ref, df_ref, db_ref, yx_ref, yh_ref):
        ys = _ret_sample(pl.program_id(1), _ret_tiles(qx_ref, qh_ref, tps), _ret_tiles(kx_ref, kh_ref, tps),
                         _ret_tiles(vx_ref, vh_ref, tps), df_ref[0:1, 0:1], db_ref[0:1, 0:1])
        yh_ref[...] = ys[0]
        for i in range(tps):
            yx_ref[i * T:(i + 1) * T, :] = ys[i + 1]

    return pl.pallas_call(
        body, grid=(lay.B, RET_HEADS), in_specs=_ret_specs(lay),
        out_specs=[pl.BlockSpec((SEQ, 128), lambda b, h: (b, h)), pl.BlockSpec((T, 128), lambda b, h: (b, h))],
        out_shape=[jax.ShapeDtypeStruct((lay.NX, 512), F32), jax.ShapeDtypeStruct((lay.B * T, 512), F32)],
        compiler_params=_cparams(("parallel", "arbitrary")), name=name)(rq, rq, rk, rk, z, z, decf, decb)


def ret_bwd(rq, rk, z, decf, decb, dy, lay, name):
    tps, SEQ, B = lay.tps, lay.SEQ, lay.B
    nxt = lay.nxt

    def body(qx_ref, qh_ref, kx_ref, kh_ref, vx_ref, vh_ref, df_ref, db_ref, dyx_ref, dyh_ref,
             dqx_ref, dqh_ref, dkx_ref, dkh_ref, dvx_ref, dvh_ref, ddf_ref, ddb_ref):
        h = pl.program_id(1)
        _, vjp = jax.vjp(functools.partial(_ret_sample, h), _ret_tiles(qx_ref, qh_ref, tps, F32),
                         _ret_tiles(kx_ref, kh_ref, tps, F32), _ret_tiles(vx_ref, vh_ref, tps), df_ref[0:1, 0:1],
                         db_ref[0:1, 0:1])
        dqs, dks, dvs, ddf, ddb = vjp(_ret_tiles(dyx_ref, dyh_ref, tps))
        first = h == 0
        _acc(dqh_ref, dqs[0], first)
        _acc(dkh_ref, dks[0], first)
        dvh_ref[...] = dvs[0]
        for i in range(tps):
            rows = slice(i * T, (i + 1) * T)
            _acc(dqx_ref.at[rows], dqs[i + 1], first)
            _acc(dkx_ref.at[rows], dks[i + 1], first)
            dvx_ref[rows, :] = dvs[i + 1]
        @pl.when(jnp.logical_and(pl.program_id(0) == 0, first))
        def _():
            ddf_ref[...] = jnp.zeros(ddf_ref.shape, F32)
            ddb_ref[...] = jnp.zeros(ddb_ref.shape, F32)

        ddf_ref[h] += jnp.broadcast_to(ddf, (8, 128))
        ddb_ref[h] += jnp.broadcast_to(ddb, (8, 128))

    acc_x, acc_h = pl.BlockSpec((SEQ, 256), lambda b, h: (b, 0)), pl.BlockSpec((T, 256), lambda b, h: (b, 0))
    head_x, head_h = pl.BlockSpec((SEQ, 128), lambda b, h: (b, h)), pl.BlockSpec((T, 128), lambda b, h: (b, h))
    dec = pl.BlockSpec((RET_HEADS, 8, 128), lambda b, h: (0, 0, 0))
    return pl.pallas_call(
        body, grid=(B, RET_HEADS),
        in_specs=_ret_specs(lay) + [head_x, pl.BlockSpec((T, 128), lambda b, h: (nxt + b, h))],
        out_specs=[acc_x, acc_h, acc_x, acc_h, head_x, head_h, dec, dec],
        out_shape=[jax.ShapeDtypeStruct((lay.NX, 256), F32), jax.ShapeDtypeStruct((B * T, 256), F32),
                   jax.ShapeDtypeStruct((lay.NX, 256), F32), jax.ShapeDtypeStruct((B * T, 256), F32),
                   jax.ShapeDtypeStruct((lay.NX, 512), F32), jax.ShapeDtypeStruct((B * T, 512), F32),
                   jax.ShapeDtypeStruct((RET_HEADS, 8, 128), F32), jax.ShapeDtypeStruct((RET_HEADS, 8, 128), F32)],
        compiler_params=_cparams(("arbitrary", "arbitrary")), name=name)(rq, rq, rk, rk, z, z, decf, decb, dy, dy)


_SWA_SCALE = SWA_HEAD_DIM ** -0.5


def _swa_head(qh, sw, kh, vw, vh, sink):
    sh = bdot_nt(qh, kh) * _SWA_SCALE
    m = jnp.maximum(jnp.max(sh, axis=-1, keepdims=True), sink)
    if sw is not None:
        m = jnp.maximum(m, jnp.max(sw, axis=-1, keepdims=True))
    m = lax.stop_gradient(m)
    eh, es = jnp.exp(sh - m), jnp.exp(sink - m)
    tot = jnp.sum(eh, axis=-1, keepdims=True) + es
    if sw is None:
        return bdot(eh, vh) * (1.0 / tot)
    ew = jnp.exp(sw - m)
    return (bdot(ew, vw) + bdot(eh, vh)) * (1.0 / (tot + jnp.sum(ew, axis=-1, keepdims=True)))


def _swa_x(t0, kpos0, sq, kw, vw, kh, vh, *sinks):
    t = t0 + lax.broadcasted_iota(jnp.int32, (T, 1), 0)
    pos = kpos0 + lax.broadcasted_iota(jnp.int32, (1, SWA_SPAN), 1)
    band = jnp.abs(t - pos) <= SWA_WINDOW
    out = 0.0
    for i in range(SWA_GROUPS):
        mi = _head_mask(sq.shape, i)
        qh = sq * mi
        sw = jnp.where(band, bdot_nt(qh, kw) * _SWA_SCALE, NEG_INF)
        out = out + _swa_head(qh, sw, kh, vw, vh, sinks[i]) * mi
    return out


def _swa_h(sq, kh, vh, *sinks):
    out = 0.0
    for i in range(SWA_GROUPS):
        mi = _head_mask(sq.shape, i)
        out = out + _swa_head(sq * mi, None, kh, None, vh, sinks[i]) * mi
    return out


def _swa_specs(lay):
    nxt, SEQ = lay.nxt, lay.SEQ
    return [pl.BlockSpec((T, 256), lambda g, b, qi: (lay.qrow(b, qi), g)),
            pl.BlockSpec((SEQ, 256), lambda g, b, qi: (b, g)), pl.BlockSpec((SEQ, 256), lambda g, b, qi: (b, g)),
            pl.BlockSpec((T, 256), lambda g, b, qi: (nxt + b, g)), pl.BlockSpec((T, 256), lambda g, b, qi: (nxt + b, g)),
            pl.BlockSpec((None, 4, 8, 128), lambda g, b, qi: (g, 0, 0, 0))]


def _swa_start(qi, SEQ):
    return pl.multiple_of(jnp.clip(qi * T - SWA_WINDOW, 0, SEQ - SWA_SPAN), SWA_WINDOW)


def swa_fwd(sq, kexp, vexp, sink, lay, name):
    tps, SEQ = lay.tps, lay.SEQ

    def body(sq_ref, kx_ref, vx_ref, kh_ref, vh_ref, sink_ref, o_ref):
        qi = pl.program_id(2)
        f = lambda r: r[...]
        sinks = [sink_ref[i][0:1, 0:1] for i in range(SWA_GROUPS)]

        @pl.when(qi < tps)
        def _():
            k0 = _swa_start(qi, SEQ)
            kw, vw = kx_ref[pl.ds(k0, SWA_SPAN), :], vx_ref[pl.ds(k0, SWA_SPAN), :]
            o_ref[...] = _swa_x(qi * T, k0, f(sq_ref), kw, vw, f(kh_ref), f(vh_ref), *sinks).astype(BF16)

        @pl.when(qi == tps)
        def _():
            o_ref[...] = _swa_h(f(sq_ref), f(kh_ref), f(vh_ref), *sinks).astype(BF16)

    return pl.pallas_call(
        body, grid=(SWA_KV_HEADS, lay.B, lay.nq), in_specs=_swa_specs(lay),
        out_specs=pl.BlockSpec((T, 256), lambda g, b, qi: (lay.qrow(b, qi), g)),
        out_shape=jax.ShapeDtypeStruct((lay.R, 512), BF16),
        compiler_params=_cparams(("parallel", "parallel", "arbitrary")), name=name)(sq, kexp, vexp, kexp, vexp, sink)


def swa_bwd(sq, kexp, vexp, sink, dmerged, lay, name):
    tps, SEQ, B = lay.tps, lay.SEQ, lay.B

    def body(sq_ref, kx_ref, vx_ref, kh_ref, vh_ref, sink_ref, do_ref, dsq_ref, dkx_ref, dkh_ref, dvx_ref, dvh_ref, dsink_ref):
        b, qi = pl.program_id(1), pl.program_id(2)
        f = lambda r: r[...].astype(F32)
        sinks = [sink_ref[i][0:1, 0:1] for i in range(SWA_GROUPS)]
        very_first = jnp.logical_and(b == 0, qi == 0)

        def acc_sink(ds):
            for i in range(SWA_GROUPS):
                _acc(dsink_ref.at[i], jnp.broadcast_to(ds[i], (8, 128)), very_first)

        @pl.when(qi == 0)
        def _():
            for ref in (dkx_ref, dkh_ref, dvx_ref, dvh_ref):
                ref[...] = jnp.zeros(ref.shape, F32)

        @pl.when(qi < tps)
        def _():
            k0 = _swa_start(qi, SEQ)
            win = pl.ds(k0, SWA_SPAN)
            kw, vw = kx_ref[win, :].astype(F32), vx_ref[win, :].astype(F32)
            _, vjp = jax.vjp(functools.partial(_swa_x, qi * T, k0), f(sq_ref), kw, vw, f(kh_ref), f(vh_ref), *sinks)
            d = vjp(do_ref[...])
            dsq_ref[...] = d[0]
            dkx_ref[win, :] += d[1]
            dvx_ref[win, :] += d[2]
            dkh_ref[...] += d[3]
            dvh_ref[...] += d[4]
            acc_sink(d[5:9])

        @pl.when(qi == tps)
        def _():
            _, vjp = jax.vjp(_swa_h, f(sq_ref), f(kh_ref), f(vh_ref), *sinks)
            d = vjp(do_ref[...])
            dsq_ref[...] = d[0]
            dkh_ref[...] += d[1]
            dvh_ref[...] += d[2]
            acc_sink(d[3:7])

    xs = pl.BlockSpec((SEQ, 256), lambda g, b, qi: (b, g))
    hs = pl.BlockSpec((T, 256), lambda g, b, qi: (b, g))
    return pl.pallas_call(
        body, grid=(SWA_KV_HEADS, B, lay.nq),
        in_specs=_swa_specs(lay) + [pl.BlockSpec((T, 256), lambda g, b, qi: (lay.qrow(b, qi), 2 + g))],
        out_specs=[pl.BlockSpec((T, 256), lambda g, b, qi: (lay.qrow(b, qi), g)), xs, hs, xs, hs,
                   pl.BlockSpec((None, 4, 8, 128), lambda g, b, qi: (g, 0, 0, 0))],
        out_shape=[jax.ShapeDtypeStruct((lay.R, 512), F32), jax.ShapeDtypeStruct((lay.NX, 512), F32),
                   jax.ShapeDtypeStruct((B * T, 512), F32), jax.ShapeDtypeStruct((lay.NX, 512), F32),
                   jax.ShapeDtypeStruct((B * T, 512), F32), jax.ShapeDtypeStruct((SWA_KV_HEADS, 4, 8, 128), F32)],
        compiler_params=_cparams(("arbitrary", "arbitrary", "arbitrary")), name=name)(
            sq, kexp, vexp, kexp, vexp, sink, dmerged)


def _cd_merge_piece(y, rg, g):
    return (y * lax.rsqrt(jnp.mean(y * y, axis=-1, keepdims=True) + EPS) * g) * (rg * jax.nn.sigmoid(rg))


def cd_merge_fwd(y, z, o, gn, lay, name):
    def body(yx_ref, yh_ref, rga_ref, rgb_ref, o_ref, g_ref, m_ref):
        y = _xh_pick(lay, yx_ref, yh_ref)
        for h in range(RET_HEADS):
            cols = slice(128 * h, 128 * (h + 1))
            rg_ref, rcols = (rga_ref, cols) if h < 2 else (rgb_ref, slice(128 * (h - 2), 128 * (h - 1)))
            m_ref[:, cols] = _cd_merge_piece(y[:, cols], rg_ref[:, rcols], g_ref[:, cols]).astype(BF16)
        m_ref[:, 512:1024] = o_ref[...]

    return pl.pallas_call(
        body, grid=(lay.nt,), in_specs=_xh_specs(lay, 512) + [_row(256, 5), _row(256, 6), _row(512), _full((1, 512))],
        out_specs=_row(1024), out_shape=jax.ShapeDtypeStruct((lay.R, 1024), BF16),
        compiler_params=_cparams(("parallel",)), name=name)(*y, z, z, o, gn)


def cd_merge_bwd(y, z, gn, dmerged, lay, name):
    def body(yx_ref, yh_ref, rga_ref, rgb_ref, g_ref, dm_ref, dy_ref, drg_ref, dg_ref):
        first = pl.program_id(0) == 0
        y = _xh_pick(lay, yx_ref, yh_ref)
        dgs = []
        for h in range(RET_HEADS):
            cols = slice(128 * h, 128 * (h + 1))
            rg_ref, rcols = (rga_ref, cols) if h < 2 else (rgb_ref, slice(128 * (h - 2), 128 * (h - 1)))
            _, vjp = jax.vjp(_cd_merge_piece, y[:, cols], rg_ref[:, rcols], g_ref[:, cols])
            dy, drg, dg = vjp(dm_ref[:, cols])
            dy_ref[:, cols] = dy
            drg_ref[:, cols] = drg
            dgs.append(dg)
        _acc(dg_ref, jnp.concatenate(dgs, axis=1), first)

    return pl.pallas_call(
        body, grid=(lay.nt,), in_specs=_xh_specs(lay, 512) + [_row(256, 5), _row(256, 6), _full((1, 512)), _row(512, 0)],
        out_specs=[_row(512), _row(512), _full((1, 512))],
        out_shape=[jax.ShapeDtypeStruct((lay.R, 512), F32), jax.ShapeDtypeStruct((lay.R, 512), F32),
                   jax.ShapeDtypeStruct((1, 512), F32)],
        compiler_params=_cparams(("arbitrary",)), name=name)(*y, z, z, gn, dmerged)


def cd_rows_bwd(z, tabc, tabs, dsq, drg, drq, drk, dke, dve, drv, lay, name):
    def body(*refs):
        ins = refs[:10]
        dsq_ref, drg_ref = refs[10:12]
        drq, drk, dke, dve, drv = (_xh_pick(lay, refs[12 + 2 * n], refs[13 + 2 * n]) for n in range(5))
        dz_ref = refs[22]
        _, vjp = jax.vjp(_cd_prep, *[r[...] for r in ins])
        cts = (drq, drk, dsq_ref[:, 0:256], dsq_ref[:, 256:512], dke[:, 0:256], dke[:, 256:512],
               dve[:, 0:256], dve[:, 256:512])
        dzrk, dzrq, dzsk, dzsq0, dzsq1, dzsv = vjp(cts)[:6]
        dz_ref[:, 0:256] = dzrk.astype(BF16)
        dz_ref[:, 256:768] = drv.astype(BF16)
        dz_ref[:, 768:896] = dzsk.astype(BF16)
        dz_ref[:, 896:1024] = dzsv.astype(BF16)
        dz_ref[:, 1024:1280] = dzrq.astype(BF16)
        dz_ref[:, 1280:1792] = drg_ref[...].astype(BF16)
        dz_ref[:, 1792:2048] = dzsq0.astype(BF16)
        dz_ref[:, 2048:2304] = dzsq1.astype(BF16)

    return pl.pallas_call(
        body, grid=(lay.nt,),
        in_specs=_cd_prep_specs(lay) + [_row(512), _row(512)] + _xh_specs(lay, 256) + _xh_specs(lay, 256)
        + _xh_specs(lay, 512) + _xh_specs(lay, 512) + _xh_specs(lay, 512),
        out_specs=_row(2304), out_shape=jax.ShapeDtypeStruct((lay.R, 2304), BF16),
        compiler_params=_cparams(("parallel",)), name=name)(
            z, z, z, z, z, z, tabc, tabs, tabc, tabs, dsq, drg, *drq, *drk, *dke, *dve, *drv)


def _pos():
    return lax.axis_index("x"), lax.axis_index("y"), lax.axis_index("c")


def _flip(v, bit):
    return 1 - v if bit else v


def _comm_call(name, body, ins, out_shapes, n_remote, n_local, aliases=None):
    return pl.pallas_call(
        body, in_specs=[ANY] * len(ins), out_specs=[ANY] * len(out_shapes), out_shape=out_shapes,
        scratch_shapes=[pltpu.SemaphoreType.DMA((n_remote,)), pltpu.SemaphoreType.DMA((n_remote,)),
                        pltpu.SemaphoreType.DMA((n_local,))],
        input_output_aliases=aliases or {}, name=name)(*ins)


def gather8(arr, name):
    def body(a_ref, o_ref, ssem, rsem, lsem):
        x, y, c = _pos()
        me = 4 * x + 2 * y + c
        loc = pltpu.make_async_copy(a_ref, o_ref.at[me], lsem.at[0])
        loc.start()
        cps = []
        for m in range(1, 8):
            peer = (_flip(x, m & 4), _flip(y, m & 2), _flip(c, m & 1))
            cps.append(pltpu.make_async_remote_copy(a_ref, o_ref.at[me], ssem.at[m - 1], rsem.at[m - 1],
                                                    device_id=peer, device_id_type=MESH))
            cps[-1].start()
        for cp in cps:
            cp.wait()
        loc.wait()

    return _comm_call(name, body, [arr], [jax.ShapeDtypeStruct((8,) + arr.shape, arr.dtype)], 7, 1)[0]


def gather_chips(arr, name):
    def body(a_ref, o_ref, ssem, rsem, lsem):
        x, y, c = _pos()
        k = 2 * x + y
        loc = pltpu.make_async_copy(a_ref, o_ref.at[k], lsem.at[0])
        loc.start()
        cps = []
        for m in range(1, 4):
            peer = (_flip(x, m & 2), _flip(y, m & 1), c)
            cps.append(pltpu.make_async_remote_copy(a_ref, o_ref.at[k], ssem.at[m - 1], rsem.at[m - 1],
                                                    device_id=peer, device_id_type=MESH))
            cps[-1].start()
        for cp in cps:
            cp.wait()
        loc.wait()

    return _comm_call(name, body, [arr], [jax.ShapeDtypeStruct((4,) + arr.shape, arr.dtype)], 3, 1)[0]


def gather_weights(arrs, name):
    n = len(arrs)

    def body(*refs):
        a_refs, o_refs, (isend, irecv, _) = refs[:n], refs[n:2 * n], refs[2 * n:]
        x, y, c = _pos()
        k = 2 * x + y
        sib = (x, y, 1 - c)
        chips = [(m, (_flip(x, m & 2), _flip(y, m & 1)), 2 * _flip(x, m & 2) + _flip(y, m & 1)) for m in range(1, 4)]
        waits = []
        for w, (a, o) in enumerate(zip(a_refs, o_refs)):
            H = a.shape[0] // 2
            own = pl.ds(c * H, H)
            first = [pltpu.make_async_remote_copy(a.at[own], o.at[k, own], isend.at[7 * w + m - 1], irecv.at[7 * w + m - 1],
                                                  device_id=(*chip, c), device_id_type=MESH) for m, chip, _ in chips]
            first.append(pltpu.make_async_remote_copy(a, o.at[k], isend.at[7 * w + 6], irecv.at[7 * w + 6], device_id=sib,
                                                      device_id_type=MESH))
            for cp in first:
                cp.start()
            waits.append((first, H, own, a, o, w))
        for first, H, own, a, o, w in waits:
            first[3].wait_recv()
            passed = []
            for m, chip, kk in chips:
                pltpu.make_async_remote_copy(a.at[own], o.at[kk, own], isend.at[7 * w + m - 1], irecv.at[7 * w + m - 1],
                                             device_id=(*chip, c), device_id_type=MESH).wait_recv()
                fw = pltpu.make_async_remote_copy(o.at[kk, own], o.at[kk, own], isend.at[7 * w + 2 + m], irecv.at[7 * w + 2 + m],
                                                  device_id=sib, device_id_type=MESH)
                fw.start()
                passed.append(fw)
            for fw in passed:
                fw.wait_recv()
            for cp in first + passed:
                cp.wait_send()

    outs = [jax.ShapeDtypeStruct((4,) + a.shape, a.dtype) for a in arrs]
    return _comm_call(name, body, list(arrs), outs, 7 * n, 1)


def swap_other_half(arrs, name):
    n = len(arrs)

    def body(*refs):
        a_refs, o_refs, (ssem, rsem, _) = refs[:n], refs[n:2 * n], refs[2 * n:]
        x, y, c = _pos()
        cps = []
        for w, (a, o) in enumerate(zip(a_refs, o_refs)):
            H = a.shape[1] // 2
            cps.append(pltpu.make_async_remote_copy(a.at[:, pl.ds((1 - c) * H, H)], o, ssem.at[w], rsem.at[w],
                                                    device_id=(x, y, 1 - c), device_id_type=MESH))
            cps[-1].start()
        for cp in cps:
            cp.wait()

    outs = [jax.ShapeDtypeStruct((4, a.shape[1] // 2) + a.shape[2:], a.dtype) for a in arrs]
    return _comm_call(name, body, list(arrs), outs, n, 1)


def exchange_chips(arrs, name):
    n = len(arrs)

    def body(*refs):
        a_refs, o_refs, (ssem, rsem, lsem) = refs[:n], refs[n:2 * n], refs[2 * n:]
        x, y, c = _pos()
        k = 2 * x + y
        cps = []
        for w, (a, o) in enumerate(zip(a_refs, o_refs)):
            cps.append(pltpu.make_async_copy(a.at[k], o.at[k], lsem.at[w]))
            cps[-1].start()
            for m in range(1, 4):
                px, py = _flip(x, m & 2), _flip(y, m & 1)
                cps.append(pltpu.make_async_remote_copy(a.at[2 * px + py], o.at[k], ssem.at[3 * w + m - 1], rsem.at[3 * w + m - 1],
                                                        device_id=(px, py, c), device_id_type=MESH))
                cps[-1].start()
        for cp in cps:
            cp.wait()

    return _comm_call(name, body, list(arrs), [jax.ShapeDtypeStruct(a.shape, a.dtype) for a in arrs], 3 * n, n)


def share_halves(arrs, name):
    n = len(arrs)

    def body(*refs):
        o_refs, (ssem, rsem, _) = refs[n:2 * n], refs[2 * n:]
        x, y, c = _pos()
        cps = []
        for w, o in enumerate(o_refs):
            H = o.shape[0] // 2
            mine = o.at[pl.ds(c * H, H)]
            cps.append(pltpu.make_async_remote_copy(mine, mine, ssem.at[w], rsem.at[w], device_id=(x, y, 1 - c),
                                                    device_id_type=MESH))
            cps[-1].start()
        for cp in cps:
            cp.wait()

    outs = [jax.ShapeDtypeStruct(a.shape, a.dtype) for a in arrs]
    return _comm_call(name, body, list(arrs), outs, n, 1, aliases={i: i for i in range(n)})


def _adamw(w, g, m, v):
    m = ADAM_B1 * m + (1.0 - ADAM_B1) * g
    v = ADAM_B2 * v + (1.0 - ADAM_B2) * (g * g)
    m_hat = m / (1.0 - ADAM_B1 ** ADAM_STEP)
    v_hat = v / (1.0 - ADAM_B2 ** ADAM_STEP)
    return -ADAM_LR * (m_hat / (jnp.sqrt(v_hat) + ADAM_EPS) + ADAM_WD * w), m, v


def _rows_tile(R, C):
    return _pick(R, max(8, (2 << 20) // (4 * C) // 8 * 8), 8)


def chip_partial(gs, buf, cidx, name):
    _, L, R, C = gs.shape
    H, tr = L // 2, _rows_tile(R, C)

    def body(c_ref, g_ref, b_ref, o_ref):
        o_ref[...] = (g_ref[...] + b_ref[...]).astype(BF16)

    return pl.pallas_call(
        body, grid_spec=pltpu.PrefetchScalarGridSpec(
            num_scalar_prefetch=1, grid=(4, H, R // tr),
            in_specs=[pl.BlockSpec((None, None, tr, C), lambda s, l, r, c: (s, c[0] * H + l, r, 0)),
                      pl.BlockSpec((None, None, tr, C), lambda s, l, r, c: (s, l, r, 0))],
            out_specs=pl.BlockSpec((None, None, tr, C), lambda s, l, r, c: (s, l, r, 0))),
        out_shape=jax.ShapeDtypeStruct((4, H, R, C), BF16),
        compiler_params=_cparams(("parallel", "parallel", "parallel")), name=name)(cidx, gs, buf)


def sum_chips(parts, cidx, name):
    _, H, R, C = parts.shape
    tr = _rows_tile(R, C)

    def body(c_ref, p_ref, g_out):
        g = p_ref[0].astype(F32)
        for s in range(1, 4):
            g = g + p_ref[s].astype(F32)
        g_out[...] = g

    return pl.pallas_call(
        body, grid_spec=pltpu.PrefetchScalarGridSpec(
            num_scalar_prefetch=1, grid=(H, R // tr),
            in_specs=[pl.BlockSpec((4, None, tr, C), lambda l, r, c: (0, l, r, 0))],
            out_specs=pl.BlockSpec((None, tr, C), lambda l, r, c: (c[0] * H + l, r, 0))),
        out_shape=jax.ShapeDtypeStruct((2 * H, R, C), F32),
        compiler_params=_cparams(("parallel", "parallel")), name=name)(cidx, parts)


def sum8(arr, name):
    n = arr.shape[1]
    tr = _pick(n, 512, 8)

    def body(a_ref, o_ref):
        s = a_ref[0]
        for j in range(1, 8):
            s = s + a_ref[j]
        o_ref[...] = s

    return pl.pallas_call(
        body, grid=(n // tr,), in_specs=[pl.BlockSpec((8, tr, 128), lambda i: (0, i, 0))],
        out_specs=pl.BlockSpec((tr, 128), lambda i: (i, 0)), out_shape=jax.ShapeDtypeStruct((n, 128), F32),
        compiler_params=_cparams(("parallel",)), name=name)(arr)


def adam_rows(w, g, m, v, name, emit_g=False):
    n, C = w.shape
    tr = _rows_tile(n, C)

    def body(w_ref, g_ref, m_ref, v_ref, *outs):
        d_out, m_out, v_out = outs[-3:]
        d_out[...], m_out[...], v_out[...] = _adamw(w_ref[...], g_ref[...], m_ref[...], v_ref[...])
        if emit_g:
            outs[0][...] = g_ref[...]

    spec = pl.BlockSpec((tr, C), lambda i: (i, 0))
    n_out = 4 if emit_g else 3
    return pl.pallas_call(
        body, grid=(n // tr,), in_specs=[spec] * 4, out_specs=[spec] * n_out,
        out_shape=[jax.ShapeDtypeStruct((n, C), F32)] * n_out, compiler_params=_cparams(("parallel",)), name=name)(w, g, m, v)


def _silu(c):
    return c * jax.nn.sigmoid(c)


def ada_fwd(c_all, w, b, name):
    NC, D = c_all.shape
    L, _, Wc = w.shape
    tn = _pick(Wc, 512, 128)

    def body(c_ref, w_ref, b_ref, o_ref):
        o_ref[...] = bdot(_silu(c_ref[...]), w_ref[...]) + b_ref[...]

    return pl.pallas_call(
        body, grid=(L, Wc // tn),
        in_specs=[pl.BlockSpec((NC, D), lambda l, j: (0, 0)), pl.BlockSpec((None, D, tn), lambda l, j: (l, 0, j)),
                  pl.BlockSpec((None, 1, tn), lambda l, j: (l, 0, j))],
        out_specs=pl.BlockSpec((None, NC, tn), lambda l, j: (l, 0, j)), out_shape=jax.ShapeDtypeStruct((L, NC, Wc), F32),
        compiler_params=_cparams(("parallel", "parallel")), name=name)(c_all, w, b)


def ada_bwd(c_all, w, dmod, name):
    NC, D = c_all.shape
    L, _, Wc = w.shape
    tn = _pick(Wc, 512, 128)

    def body(c_ref, w_ref, d_ref, dw_ref, db_ref, dc_ref):
        first = jnp.logical_and(pl.program_id(0) == 0, pl.program_id(1) == 0)
        f = lambda cs, ww: bdot(cs, ww)
        _, vjp = jax.vjp(f, _silu(c_ref[...]), w_ref[...])
        dcs, dw = vjp(d_ref[...])
        dw_ref[...] = dw
        db_ref[...] = jnp.sum(d_ref[...], axis=0, keepdims=True)
        _acc(dc_ref, dcs, first)

    return pl.pallas_call(
        body, grid=(L, Wc // tn),
        in_specs=[pl.BlockSpec((NC, D), lambda l, j: (0, 0)), pl.BlockSpec((None, D, tn), lambda l, j: (l, 0, j)),
                  pl.BlockSpec((None, NC, tn), lambda l, j: (l, 0, j))],
        out_specs=[pl.BlockSpec((None, D, tn), lambda l, j: (l, 0, j)), pl.BlockSpec((None, 1, tn), lambda l, j: (l, 0, j)),
                   pl.BlockSpec((NC, D), lambda l, j: (0, 0))],
        out_shape=[jax.ShapeDtypeStruct((L, D, Wc), F32), jax.ShapeDtypeStruct((L, 1, Wc), F32),
                   jax.ShapeDtypeStruct((NC, D), F32)],
        compiler_params=_cparams(("arbitrary", "arbitrary")), name=name)(c_all, w, dmod)


def cctx_grad(dcs_twice, c_ctx, name):
    def body(d_ref, c_ref, o_ref):
        _, vjp = jax.vjp(_silu, c_ref[...])
        o_ref[...] = vjp(0.5 * d_ref[...])[0]

    D = c_ctx.shape[1]
    return pl.pallas_call(body, out_shape=jax.ShapeDtypeStruct((8, D), F32), name=name)(dcs_twice, c_ctx)


def _rope_tables(SEQ):
    t = jnp.arange(SEQ)
    row, col = (t // GRID_W).astype(F32), (t % GRID_W).astype(F32)
    n_freq = 16
    freqs = ROPE_THETA ** (-jnp.arange(n_freq, dtype=F32) / n_freq)
    ang = jnp.concatenate([row[:, None] * freqs, col[:, None] * freqs], axis=-1)
    cos, sin = jnp.cos(ang), jnp.sin(ang)
    c = jnp.tile(jnp.concatenate([cos, cos], axis=1), (1, 4))
    s = jnp.tile(jnp.concatenate([-sin, sin], axis=1), (1, 4))
    return (jnp.concatenate([c, jnp.ones((T, 256), F32)], axis=0), jnp.concatenate([s, jnp.zeros((T, 256), F32)], axis=0))


def _unshard_cols(g):
    return jnp.transpose(g, (1, 2, 0, 3)).reshape(g.shape[1], g.shape[2], 4 * g.shape[3])


def _unshard_rows(g):
    return jnp.transpose(g, (1, 0, 2, 3)).reshape(g.shape[1], 4 * g.shape[2], g.shape[3])


def _shard_cols(w, n=4):
    L, R, C = w.shape
    return jnp.transpose(w.reshape(L, R, n, C // n), (2, 0, 1, 3))


def _shard_rows(w):
    L, R, C = w.shape
    return jnp.transpose(w.reshape(L, 4, R // 4, C), (1, 0, 2, 3))


def _ab_in_permute(w):
    L, D, _ = w.shape
    return jnp.concatenate([w[..., 0:256], w[..., 320:1600], w[..., 256:320], jnp.zeros((L, D, 64), w.dtype)], axis=-1)


def _ab_in_unpermute(g):
    return jnp.concatenate([g[..., 0:256], g[..., 1536:1600], g[..., 256:1536]], axis=-1)


def _split_heads(w, a):
    L, K, N = w.shape
    w4 = w.reshape(L, K, 4, N // 4)
    return w4[..., :a].reshape(L, K, 4 * a), w4[..., a:].reshape(L, K, N - 4 * a)


def _join_heads(p, q):
    L, K = p.shape[:2]
    return jnp.concatenate([p.reshape(L, K, 4, -1), q.reshape(L, K, 4, -1)], axis=-1).reshape(L, K, -1)


def _pack(arrs):
    parts = []
    for a in arrs:
        f = a.reshape(-1).astype(F32)
        parts.append(jnp.pad(f, (0, (-f.shape[0]) % 1024)))
    return jnp.concatenate(parts).reshape(-1, 128)


def _unpack(buf, like):
    out, r0 = [], 0
    for a in like:
        n = math.prod(a.shape)
        rows = (n + (-n) % 1024) // 128
        out.append(buf[r0:r0 + rows].reshape(-1)[:n].reshape(a.shape))
        r0 += rows
    return out


_SMALL = ("c_ctx", "ada_b", "norm_mix", "norm_ffn", "norm_final", "mla_q_norm", "mla_kv_norm", "cmlp_v_norm", "cmlp_ws",
          "cmlp_bs", "ret_decay_fwd", "ret_decay_bwd", "ret_norm", "swa_sink")
_BIG = ("ffn_in", "ffn_out", "ab_in", "ab_out", "mla_wq_b", "mla_wkv_b", "cd_in", "cd_out")
_WEIGHTS = ("c_ctx", "ada_w", "ada_b", "norm_mix", "norm_ffn", "norm_final", "ffn_in", "ffn_out", "ab_in", "ab_out",
            "mla_q_norm", "mla_kv_norm", "mla_wq_b", "mla_wkv_b", "cmlp_v_norm", "cmlp_ws", "cmlp_bs", "cd_in", "cd_out",
            "ret_decay_fwd", "ret_decay_bwd", "ret_norm", "swa_sink")


def kernel(x, c, ctx, c_ctx, ada_w, ada_b, norm_mix, norm_ffn, norm_final, ffn_in, ffn_out, ab_in, ab_out, mla_q_norm, mla_kv_norm, mla_wq_b, mla_wkv_b, cmlp_v_norm, cmlp_ws, cmlp_bs, cd_in, cd_out, ret_decay_fwd, ret_decay_bwd, ret_norm, swa_sink, loss_target, m_c_ctx, m_ada_w, m_ada_b, m_norm_mix, m_norm_ffn, m_norm_final, m_ffn_in, m_ffn_out, m_ab_in, m_ab_out, m_mla_q_norm, m_mla_kv_norm, m_mla_wq_b, m_mla_wkv_b, m_cmlp_v_norm, m_cmlp_ws, m_cmlp_bs, m_cd_in, m_cd_out, m_ret_decay_fwd, m_ret_decay_bwd, m_ret_norm, m_swa_sink, v_c_ctx, v_ada_w, v_ada_b, v_norm_mix, v_norm_ffn, v_norm_final, v_ffn_in, v_ffn_out, v_ab_in, v_ab_out, v_mla_q_norm, v_mla_kv_norm, v_mla_wq_b, v_mla_wkv_b, v_cmlp_v_norm, v_cmlp_ws, v_cmlp_bs, v_cd_in, v_cd_out, v_ret_decay_fwd, v_ret_decay_bwd, v_ret_norm, v_swa_sink):
    W = dict(c_ctx=c_ctx, ada_w=ada_w, ada_b=ada_b, norm_mix=norm_mix, norm_ffn=norm_ffn, norm_final=norm_final, ffn_in=ffn_in, ffn_out=ffn_out, ab_in=ab_in, ab_out=ab_out, mla_q_norm=mla_q_norm, mla_kv_norm=mla_kv_norm, mla_wq_b=mla_wq_b, mla_wkv_b=mla_wkv_b, cmlp_v_norm=cmlp_v_norm, cmlp_ws=cmlp_ws, cmlp_bs=cmlp_bs, cd_in=cd_in, cd_out=cd_out, ret_decay_fwd=ret_decay_fwd, ret_decay_bwd=ret_decay_bwd, ret_norm=ret_norm, swa_sink=swa_sink)
    M1 = dict(c_ctx=m_c_ctx, ada_w=m_ada_w, ada_b=m_ada_b, norm_mix=m_norm_mix, norm_ffn=m_norm_ffn, norm_final=m_norm_final, ffn_in=m_ffn_in, ffn_out=m_ffn_out, ab_in=m_ab_in, ab_out=m_ab_out, mla_q_norm=m_mla_q_norm, mla_kv_norm=m_mla_kv_norm, mla_wq_b=m_mla_wq_b, mla_wkv_b=m_mla_wkv_b, cmlp_v_norm=m_cmlp_v_norm, cmlp_ws=m_cmlp_ws, cmlp_bs=m_cmlp_bs, cd_in=m_cd_in, cd_out=m_cd_out, ret_decay_fwd=m_ret_decay_fwd, ret_decay_bwd=m_ret_decay_bwd, ret_norm=m_ret_norm, swa_sink=m_swa_sink)
    M2 = dict(c_ctx=v_c_ctx, ada_w=v_ada_w, ada_b=v_ada_b, norm_mix=v_norm_mix, norm_ffn=v_norm_ffn, norm_final=v_norm_final, ffn_in=v_ffn_in, ffn_out=v_ffn_out, ab_in=v_ab_in, ab_out=v_ab_out, mla_q_norm=v_mla_q_norm, mla_kv_norm=v_mla_kv_norm, mla_wq_b=v_mla_wq_b, mla_wkv_b=v_mla_wkv_b, cmlp_v_norm=v_cmlp_v_norm, cmlp_ws=v_cmlp_ws, cmlp_bs=v_cmlp_bs, cd_in=v_cd_in, cd_out=v_cd_out, ret_decay_fwd=v_ret_decay_fwd, ret_decay_bwd=v_ret_decay_bwd, ret_norm=v_ret_norm, swa_sink=v_swa_sink)

    B, SEQ, D = x.shape
    CTX = ctx.shape[1]
    lay = Layout(B, SEQ, CTX, D)
    nt, NX = lay.nt, lay.NX
    ix, iy, ic = lax.axis_index("x"), lax.axis_index("y"), lax.axis_index("c")
    chip, me = 2 * ix + iy, 4 * ix + 2 * iy + ic
    cidx = jnp.reshape(ic, (1,)).astype(jnp.int32)
    Wc = ada_w.shape[2]
    n_even, n_odd = ab_in.shape[0], cd_in.shape[0]

    rn_row = jnp.pad(ret_norm.reshape(1, -1), ((0, 0), (0, D - ret_norm.size)))
    pack0 = jnp.concatenate([c, rn_row, jnp.zeros((8 - (B + 1) % 8, D), F32)], axis=0) if (B + 1) % 8 else jnp.concatenate([c, rn_row], axis=0)
    g0 = gather8(pack0, "gather_cond")
    NC = -(-(8 * B + 1) // 16) * 16
    c_all = jnp.concatenate([g0[:, :B].reshape(8 * B, D), c_ctx[None], jnp.zeros((NC - 8 * B - 1, D), F32)], axis=0)
    rn_sh = ret_norm.shape[1]
    ret_norm_full = jnp.transpose(g0[0::2, B, :ret_norm.size].reshape(4, n_odd, rn_sh), (1, 0, 2)).reshape(n_odd, 4 * rn_sh)

    gw = gather_weights([W[n].astype(BF16) for n in _BIG], "gather_weights")
    w_ab_in, w_wq, w_wkv, w_cd_in = (_unshard_cols(gw[i]) for i in (2, 4, 5, 6))
    w_ffn_out, w_ab_out, w_cd_out = (_unshard_rows(gw[i]) for i in (1, 3, 7))
    w_ab_in = _ab_in_permute(w_ab_in)
    w_ffn_out_t = jnp.transpose(w_ffn_out, (0, 2, 1))
    w_qn, w_qp = _split_heads(w_wq, MLA_NOPE)
    w_k, w_v = _split_heads(w_wkv, MLA_NOPE)

    ab_sh = lax.dynamic_slice_in_dim(ada_b, chip * Wc, Wc, axis=1)[:, None, :]
    mod_sh = ada_fwd(c_all, ada_w, ab_sh, "ada_fwd")
    mod_all = _unshard_cols(gather_chips(mod_sh, "gather_mod"))
    mod_mine = jnp.concatenate([lax.dynamic_slice_in_dim(mod_all, me * B, B, axis=1), mod_all[:, 8 * B:8 * B + 1]], axis=1)
    mod = mod_mine.reshape(DEPTH, B + 1, 6, D)

    tabc, tabs = _rope_tables(SEQ)
    bc8 = lambda a: jnp.broadcast_to(a.reshape(a.shape + (1, 1)), a.shape + (8, 128))
    row = lambda a: a.reshape(1, -1)

    S = jnp.concatenate([x.reshape(NX, D), ctx.reshape(B * CTX, D)], axis=0)
    saved = []
    xn = norm_mod_fwd(S, row(norm_mix[0]), mod[0], 0, lay, nt, "norm_mix_fwd0")
    for l in range(DEPTH):
        j, even = l // 2, l % 2 == 0
        if even:
            z = mm(xn, w_ab_in[j], name=f"ab_in{l}")
            q, k, v = ab_prep_fwd(z, tabc, tabs, row(mla_kv_norm[j]), row(mla_q_norm[j]), w_k[j], w_v[j], w_qn[j], w_qp[j],
                                  lay, f"ab_prep{l}")
            o = mla_fwd(q, k, v, lay, f"mla{l}")
            merged = cmlp_merge_fwd(z, o, row(cmlp_v_norm[j]), cmlp_ws[j], cmlp_bs[j][:, :, None], lay, f"cmlp{l}")
            w_out, mix = w_ab_out[j], (q, k, v)
        else:
            z = mm(xn, w_cd_in[j], name=f"cd_in{l}")
            rq, rk, sq, ke, ve = cd_prep_fwd(z, tabc, tabs, lay, f"cd_prep{l}")
            decf, decb, sink = bc8(ret_decay_fwd[j]), bc8(ret_decay_bwd[j]), bc8(swa_sink[j].reshape(SWA_KV_HEADS, SWA_GROUPS))
            yret = ret_fwd(rq, rk, z, decf, decb, lay, f"ret{l}")
            osw = swa_fwd(sq, ke, ve, sink, lay, f"swa{l}")
            merged = cd_merge_fwd(yret, z, osw, row(ret_norm_full[j]), lay, f"cd_merge{l}")
            w_out, mix = w_cd_out[j], (rq, rk, sq, ke, ve, decf, decb, sink, yret)
        y, S_mid, xn2 = mm_gated(merged, w_out, S, mod[l], 2, lay, name=f"mix_out{l}", n_tiles=nt,
                                 norm=(row(norm_ffn[l]), mod[l], 3))
        act, fa, fb = ffn_in_act(xn2, gw[0], l, f"ffn_in{l}")
        nxt_norm = (row(norm_mix[l + 1]), mod[l + 1], 0) if l + 1 < DEPTH else None
        f, S_new, *xn_next = mm_gated(act, w_ffn_out[l], S_mid, mod[l], 5, lay, name=f"ffn_out{l}", n_tiles=nt, norm=nxt_norm)
        saved.append((S, xn, z, mix, merged, w_out, y, S_mid, xn2, (fa, fb), act, f))
        S, xn = S_new, (xn_next[0] if xn_next else None)

    loss_blk, dS, d_norm_final = loss_head(S, row(norm_final), loss_target.reshape(NX, D), lay, "loss_head")
    loss = lax.psum(loss_blk[0, 0], ("x", "y", "c"))

    G = {n: [None] * W[n].shape[0] for n in ("norm_mix", "norm_ffn", "mla_q_norm", "mla_kv_norm", "cmlp_v_norm", "cmlp_ws",
                                             "cmlp_bs", "ret_decay_fwd", "ret_decay_bwd", "ret_norm", "swa_sink")}
    GB = {n: [None] * cnt for n, cnt in (("ffn_in_a", DEPTH), ("ffn_in_b", DEPTH), ("ffn_out", DEPTH), ("ab_in", n_even), ("ab_out", n_even),
                                         ("wqn", n_even), ("wqp", n_even), ("wk", n_even), ("wv", n_even), ("cd_in", n_odd),
                                         ("cd_out", n_odd))}
    dmod = [None] * DEPTH
    df, dgate2 = gate_bwd(dS, saved[-1][-1], mod[DEPTH - 1], 5, lay, nt, f"ffn_gate_bwd{DEPTH - 1}")
    for l in reversed(range(DEPTH)):
        j, even = l // 2, l % 2 == 0
        S_in, xn, z, mix, merged, w_out, y, S_mid, xn2, (fa, fb), act, f = saved[l]
        da, db = ffn_out_dx_act(df, w_ffn_out_t[l], fa, fb, f"ffn_out_dx{l}")
        GB["ffn_out"][l] = mm(act, df, ta=True, name=f"ffn_out_dw{l}")
        GB["ffn_in_a"][l] = mm(xn2, da, ta=True, name=f"ffn_in_dwa{l}", split=2)
        GB["ffn_in_b"][l] = mm(xn2, db, ta=True, name=f"ffn_in_dwb{l}", split=2)
        dxn2 = ffn_in_dx(da, db, gw[0], l, f"ffn_in_dx{l}")
        dS_mid, dss2, dg, dy, dgate1 = norm_mod_bwd(S_mid, row(norm_ffn[l]), mod[l], 3, dxn2, dS, lay, nt, f"norm_ffn_bwd{l}",
                                                    gate=(y, mod[l], 2))
        G["norm_ffn"][l] = dg
        dmerged = mm(dy, w_out, tb=True, name=f"mix_out_dx{l}")
        d_w_out = mm(merged, dy, ta=True, name=f"mix_out_dw{l}")
        if even:
            q, k, v = mix
            dq, dkx, dkh, dvx, dvh = mla_bwd(q, k, v, dmerged, lay, f"mla_bwd{l}")
            (dz, dgkv, dgq, dwk, dwv, dwqn, dwqp, dgvn, dws, dbs) = ab_rows_bwd(
                z, tabc, tabs, dq, dkx, dkh, dvx, dvh, dmerged, row(mla_kv_norm[j]), row(mla_q_norm[j]), w_k[j], w_v[j],
                w_qn[j], w_qp[j], row(cmlp_v_norm[j]), cmlp_ws[j], cmlp_bs[j][:, :, None], lay, f"ab_rows_bwd{l}")
            G["mla_kv_norm"][j], G["mla_q_norm"][j], G["cmlp_v_norm"][j] = dgkv, dgq, dgvn
            G["cmlp_ws"][j], G["cmlp_bs"][j] = dws, dbs
            GB["wk"][j], GB["wv"][j], GB["wqn"][j], GB["wqp"][j], GB["ab_out"][j] = dwk, dwv, dwqn, dwqp, d_w_out
            w_in = w_ab_in[j]
        else:
            rq, rk, sq, ke, ve, decf, decb, sink, yret = mix
            dyret, drg, dgn = cd_merge_bwd(yret, z, row(ret_norm_full[j]), dmerged, lay, f"cd_merge_bwd{l}")
            dqx, dqh, dkx, dkh, dvx, dvh, ddf, ddb = ret_bwd(rq, rk, z, decf, decb, dyret, lay, f"ret_bwd{l}")
            dsq, dkex, dkeh, dvex, dveh, dsink = swa_bwd(sq, ke, ve, sink, dmerged, lay, f"swa_bwd{l}")
            dz = cd_rows_bwd(z, tabc, tabs, dsq, drg, (dqx, dqh), (dkx, dkh), (dkex, dkeh), (dvex, dveh), (dvx, dvh), lay,
                             f"cd_rows_bwd{l}")
            G["ret_norm"][j], G["ret_decay_fwd"][j], G["ret_decay_bwd"][j] = dgn, ddf[:, 0, 0], ddb[:, 0, 0]
            G["swa_sink"][j] = dsink[:, :, 0, 0].reshape(-1)
            GB["cd_out"][j] = d_w_out
            w_in = w_cd_in[j]
        GB["ab_in" if even else "cd_in"][j] = mm(xn, dz, ta=True, name=f"mix_in_dw{l}")
        dxn = mm(dz, w_in, tb=True, name=f"mix_in_dx{l}")
        dmod_l = lambda dss1: jnp.concatenate([dss1, dgate1, dss2, dgate2], axis=1)
        if l > 0:
            dS, dss1, dg, df, dgate2_prev = norm_mod_bwd(S_in, row(norm_mix[l]), mod[l], 0, dxn, dS_mid, lay, nt,
                                                         f"norm_mix_bwd{l}", gate=(saved[l - 1][-1], mod[l - 1], 5))
            dmod[l], dgate2 = dmod_l(dss1), dgate2_prev
        else:
            dS, dss1, dg = norm_mod_bwd(S_in, row(norm_mix[l]), mod[l], 0, dxn, dS_mid, lay, nt, f"norm_mix_bwd{l}")
            dmod[l] = dmod_l(dss1)
        G["norm_mix"][l] = dg
    grad_x = dS[:NX].reshape(B, SEQ, D)

    st = lambda n: jnp.stack([g.reshape((4 * rn_sh,) if n == "ret_norm" else W[n].shape[1:]) for g in G[n]])
    small_parts = {n: st(n) for n in G}
    small_parts["norm_final"] = d_norm_final.reshape(-1)
    dmod_local = jnp.stack(dmod).reshape(DEPTH, B + 1, 6 * D)
    names1 = ["norm_mix", "norm_ffn", "norm_final", "mla_q_norm", "mla_kv_norm", "cmlp_v_norm", "cmlp_ws", "cmlp_bs",
              "ret_decay_fwd", "ret_decay_bwd", "ret_norm", "swa_sink"]
    like1 = [dmod_local] + [small_parts[n] for n in names1]
    g1 = gather8(_pack(like1), "gather_small_grads")
    tot1 = _unpack(sum8(g1, "sum_small_grads"), like1)
    sg = dict(zip(names1, tot1[1:]))
    n_dm = math.prod(dmod_local.shape)
    dm_each = g1[:, :-(-n_dm // 128)].reshape(8, -1)[:, :n_dm].reshape(8, DEPTH, B + 1, 6 * D)
    dmod_all = jnp.concatenate([jnp.transpose(dm_each[:, :, :B], (1, 0, 2, 3)).reshape(DEPTH, 8 * B, 6 * D),
                                tot1[0][:, B:B + 1], jnp.zeros((DEPTH, NC - 8 * B - 1, 6 * D), F32)], axis=1)
    dmod_sh = lax.dynamic_slice_in_dim(dmod_all, chip * Wc, Wc, axis=2)
    g_ada_w, g_ada_b_sh, dcs = ada_bwd(c_all, ada_w, dmod_sh, "ada_bwd")
    like2 = [dcs[8 * B], g_ada_b_sh]
    g2 = gather8(_pack(like2), "gather_ada_grads")
    tot2 = _unpack(sum8(g2, "sum_ada_grads"), like2)
    bc = lambda a: jnp.broadcast_to(a.reshape(1, D), (8, D))
    sg["c_ctx"] = cctx_grad(bc(tot2[0]), bc(c_ctx), "c_ctx_grad")[0]
    off = D + (-D) % 1024
    gab = g2[0::2, off // 128:(off + DEPTH * Wc) // 128].reshape(4, DEPTH, Wc)
    sg["ada_b"] = jnp.transpose(gab, (1, 0, 2)).reshape(DEPTH, 4 * Wc)
    sg["ret_norm"] = lax.dynamic_slice_in_dim(sg["ret_norm"], chip * rn_sh, rn_sh, axis=1)

    stack = lambda n: jnp.stack(GB[n])
    gs = {"ffn_in": jnp.concatenate([jnp.stack(GB[n], axis=1) for n in ("ffn_in_a", "ffn_in_b")], axis=0),
          "ffn_out": _shard_rows(stack("ffn_out")),
          "ab_in": _shard_cols(_ab_in_unpermute(stack("ab_in"))), "ab_out": _shard_rows(stack("ab_out")),
          "mla_wq_b": _shard_cols(_join_heads(stack("wqn"), stack("wqp"))),
          "mla_wkv_b": _shard_cols(_join_heads(stack("wk"), stack("wv"))),
          "cd_in": _shard_cols(stack("cd_in")), "cd_out": _shard_rows(stack("cd_out"))}
    bufs = swap_other_half([gs[n] for n in _BIG], "swap_core_halves")
    parts = [chip_partial(gs[n], b, cidx, f"chip_partial_{n}") for n, b in zip(_BIG, bufs)]
    arrived = exchange_chips(parts, "exchange_chips")
    grads = share_halves([sum_chips(p, cidx, f"sum_chips_{n}") for n, p in zip(_BIG, arrived)], "share_core_halves")
    flat2 = lambda a: a.reshape(-1, a.shape[-1])
    out = {}
    for n, g in zip(_BIG, grads):
        res = adam_rows(flat2(W[n]), flat2(g), flat2(M1[n]), flat2(M2[n]), f"adam_{n}", emit_g=True)
        out[n] = tuple(r.reshape(W[n].shape) for r in res)

    like_s = [W[n] for n in _SMALL]
    dsm, msm, vsm = adam_rows(_pack(like_s), _pack([sg[n].reshape(W[n].shape) for n in _SMALL]), _pack([M1[n] for n in _SMALL]),
                                _pack([M2[n] for n in _SMALL]), "adam_small")
    for n, d_, m_, v_ in zip(_SMALL, _unpack(dsm, like_s), _unpack(msm, like_s), _unpack(vsm, like_s)):
        out[n] = (sg[n].reshape(W[n].shape), d_, m_, v_)
    d_, m_, v_ = adam_rows(flat2(ada_w), flat2(g_ada_w), flat2(m_ada_w), flat2(v_ada_w), "adam_ada_w")
    out["ada_w"] = (g_ada_w, d_.reshape(ada_w.shape), m_.reshape(ada_w.shape), v_.reshape(ada_w.shape))

    return (loss, grad_x, *[out[n][0] for n in _WEIGHTS], *[out[n][1] for n in _WEIGHTS], *[out[n][2] for n in _WEIGHTS],
            *[out[n][3] for n in _WEIGHTS])
```

```python
import functools
import math

import jax
import jax.numpy as jnp
import numpy as np
from jax import lax
from jax.experimental import pallas as pl
from jax.experimental.pallas import tpu as pltpu

F32 = jnp.float32
BF16 = jnp.bfloat16
EPS = 1e-6
NEG_INF = -1e30
GRID_W = 64
ROPE_THETA = 10000.0
DEPTH = 4
MLA_HEADS, MLA_Q_LORA, MLA_KV_LORA, MLA_NOPE, MLA_ROPE, MLA_V = 4, 256, 256, 128, 64, 128
CMLP_GROUPS, CMLP_CHUNK = 4, 128
CMLP_WIDTH = 512
RET_HEADS, RET_QK, RET_V = 4, 64, 128
SWA_Q_HEADS, SWA_KV_HEADS, SWA_HEAD_DIM, SWA_WINDOW = 8, 2, 64, 128
SWA_GROUPS = SWA_Q_HEADS // SWA_KV_HEADS
AB_IN_P = 1664
ADAM_LR, ADAM_B1, ADAM_B2, ADAM_EPS, ADAM_WD, ADAM_STEP = 0.001, 0.9, 0.999, 1e-08, 0.01, 10

T = 256
SWA_SPAN = T + 2 * SWA_WINDOW
VMEM_LIMIT = 48 * 1024 * 1024
MESH = pl.DeviceIdType.MESH
ANY = pl.BlockSpec(memory_space=pl.ANY)


def _cparams(sem):
    return pltpu.CompilerParams(dimension_semantics=sem, vmem_limit_bytes=VMEM_LIMIT)


@functools.cache
def _bdot_fn(ca, cb):
    fa, fb = 1 - ca, 1 - cb

    def dg(p, q, cp, cq):
        return lax.dot_general(p.astype(BF16), q.astype(BF16), (((cp,), (cq,)), ((), ())), preferred_element_type=F32)

    @jax.custom_vjp
    def bd(a, b):
        return dg(a, b, ca, cb)

    def fwd(a, b):
        return dg(a, b, ca, cb), (a, b)

    def bwd(res, g):
        a, b = res
        da = dg(g, b, 1, fb) if ca == 1 else dg(b, g, fb, 1)
        db = dg(a, g, fa, 0) if cb == 0 else dg(g, a, 0, fa)
        return da, db

    bd.defvjp(fwd, bwd)
    return bd


def bdot(a, b):
    return _bdot_fn(1, 0)(a, b)


def bdot_nt(a, b):
    return _bdot_fn(1, 1)(a, b)


def bdot_tn(a, b):
    return _bdot_fn(0, 0)(a, b)


def _swap32(x):
    w = x.shape[-1]
    lane = lax.broadcasted_iota(jnp.int32, x.shape, 1)
    return jnp.where((lane & 32) == 0, pltpu.roll(x, w - 32, 1), pltpu.roll(x, 32, 1))


@jax.custom_vjp
def rope(x, c, s):
    return x * c + _swap32(x) * s


def _rope_fwd(x, c, s):
    return rope(x, c, s), (c, s)


def _rope_bwd(res, g):
    c, s = res
    return g * c + _swap32(g * s), jnp.zeros_like(c), jnp.zeros_like(s)


rope.defvjp(_rope_fwd, _rope_bwd)


def rms(x, g):
    return x * lax.rsqrt(jnp.mean(x * x, axis=-1, keepdims=True) + EPS) * g


def normmod(x, g, sh, sc):
    return rms(x, g) * (1.0 + sc) + sh


def log_sigmoid(x):
    return jnp.minimum(x, 0.0) - jnp.log(1.0 + jnp.exp(-jnp.abs(x)))


def _head_mask(shape, h):
    lane = lax.broadcasted_iota(jnp.int32, shape, 1)
    return ((lane >> 6) == h).astype(F32)


def _fold_matrix():
    i = lax.broadcasted_iota(jnp.int32, (256, 128), 0)
    j = lax.broadcasted_iota(jnp.int32, (256, 128), 1)
    return ((i & 63) == j).astype(F32)


def _expand_matrix(g):
    i = lax.broadcasted_iota(jnp.int32, (128, 256), 0)
    j = lax.broadcasted_iota(jnp.int32, (128, 256), 1)
    return (i == (j & 63) + 64 * g).astype(F32)


def _acc(ref, val, first):
    @pl.when(first)
    def _():
        ref[...] = val

    @pl.when(jnp.logical_not(first))
    def _():
        ref[...] += val


def _pick(n, cap, mult):
    best = None
    for d in range(mult, min(n, cap) + 1, mult):
        if n % d == 0:
            best = d
    return best if best is not None else n


def mm(a, b, *, ta=False, tb=False, name, split=1):
    M, K = (a.shape[1], a.shape[0]) if ta else a.shape
    N = b.shape[0] if tb else b.shape[1]
    Ns = N // split
    assert split == 1 or (ta and Ns % 128 == 0)
    if ta:
        tn = N
        tm = _pick(M, min(1536, (3 << 20) // tn), 128)
    else:
        tn = _pick(N, 768, 128)
        if tn < 256 and N <= 2304:
            tn = N
        tm = _pick(M, min(1536, (1 << 20) // tn), 128)
    tk = _pick(K, 2048 if not ta else 1024, 128) if K > 2816 or ta else K
    nk = K // tk
    grid = (M // tm, N // tn, nk)
    a_spec = pl.BlockSpec((tk, tm), lambda i, j, k: (k, i)) if ta else pl.BlockSpec((tm, tk), lambda i, j, k: (i, k))
    b_spec = pl.BlockSpec((tn, tk), lambda i, j, k: (j, k)) if tb else pl.BlockSpec((tk, tn), lambda i, j, k: (k, j))
    dims = (((0 if ta else 1,), (1 if tb else 0,)), ((), ()))

    def body(a_ref, b_ref, o_ref):
        part = lax.dot_general(a_ref[...], b_ref[...], dims, preferred_element_type=F32)
        first = pl.program_id(2) == 0
        if split > 1:
            for s in range(split):
                _acc(o_ref.at[s], part[:, s * Ns:(s + 1) * Ns], first)
        elif nk == 1:
            o_ref[...] = part
        else:
            _acc(o_ref, part, first)

    if split > 1:
        out_spec, out_shape = pl.BlockSpec((split, tm, Ns), lambda i, j, k: (0, i, 0)), (split, M, Ns)
    else:
        out_spec, out_shape = pl.BlockSpec((tm, tn), lambda i, j, k: (i, j)), (M, N)
    return pl.pallas_call(
        body, grid=grid, in_specs=[a_spec, b_spec], out_specs=out_spec, out_shape=jax.ShapeDtypeStruct(out_shape, F32),
        compiler_params=_cparams(("parallel", "parallel", "arbitrary")), name=name)(a, b)


def mm_gated(a, b, res, mod, gate_row, lay, *, name, n_tiles, norm=None):
    K, N = b.shape
    M = n_tiles * T

    def body(*refs):
        a_ref, b_ref, r_ref, g_ref = refs[:4]
        y = lax.dot_general(a_ref[...], b_ref[...], (((1,), (0,)), ((), ())), preferred_element_type=F32)
        new = r_ref[...] + g_ref[gate_row:gate_row + 1, :] * y
        if norm is None:
            y_ref, o_ref = refs[4:]
        else:
            gn_ref, mn_ref, y_ref, o_ref, xn_ref = refs[4:]
            k0 = norm[2]
            xn_ref[...] = normmod(new, gn_ref[...], mn_ref[k0:k0 + 1, :], mn_ref[k0 + 1:k0 + 2, :]).astype(BF16)
        y_ref[...] = y.astype(BF16)
        o_ref[...] = new

    rows = pl.BlockSpec((T, N), lambda i: (i, 0))
    extra = [] if norm is None else [_full((1, N)), _modspec(lay, 6, N)]
    return pl.pallas_call(
        body, grid=(n_tiles,),
        in_specs=[pl.BlockSpec((T, K), lambda i: (i, 0)), _full((K, N)), rows, _modspec(lay, 6, N)] + extra,
        out_specs=[rows, rows] + ([] if norm is None else [rows]),
        out_shape=[jax.ShapeDtypeStruct((M, N), BF16), jax.ShapeDtypeStruct((M, N), F32)]
        + ([] if norm is None else [jax.ShapeDtypeStruct((M, N), BF16)]),
        compiler_params=_cparams(("parallel",)), name=name)(a, b, res, mod, *([] if norm is None else norm[:2]))


class Layout:
    def __init__(self, B, SEQ, CTX, D):
        assert CTX == T and SEQ % T == 0 and SEQ >= SWA_SPAN
        self.B, self.SEQ, self.CTX, self.D = B, SEQ, CTX, D
        self.tps = SEQ // T
        self.nxt = B * self.tps
        self.nt = self.nxt + B
        self.NX, self.R = B * SEQ, B * SEQ + B * CTX
        self.nq = self.tps + 1

    def mod_idx(self, i):
        return jnp.where(i < self.nxt, i // self.tps, self.B)

    def rope_idx(self, i):
        return jnp.where(i < self.nxt, i % self.tps, self.tps)

    def first_of_mod(self, i):
        return jnp.logical_or(jnp.logical_and(i < self.nxt, i % self.tps == 0), i == self.nxt)

    def qrow(self, b, qi):
        return jnp.where(qi < self.tps, b * self.tps + qi, self.nxt + b)


def _row(w, col=0):
    return pl.BlockSpec((T, w), lambda i: (i, col))


def _full(shape):
    nd = len(shape)
    return pl.BlockSpec(shape, lambda i: (0,) * nd)


def _modspec(lay, rows, D):
    return pl.BlockSpec((None, rows, D), lambda i: (lay.mod_idx(i), 0, 0))


def _ropespec(lay, w):
    return pl.BlockSpec((T, w), lambda i: (lay.rope_idx(i), 0))


def _xh_specs(lay, w):
    nxt = lay.nxt
    return [pl.BlockSpec((T, w), lambda i: (jnp.minimum(i, nxt - 1), 0)), pl.BlockSpec((T, w), lambda i: (jnp.maximum(i - nxt, 0), 0))]


def _xh_pick(lay, x_ref, h_ref):
    return jnp.where(pl.program_id(0) < lay.nxt, x_ref[...], h_ref[...])


def norm_mod_fwd(S, g, mod, k0, lay, n_tiles, name):
    D = S.shape[1]

    def body(s_ref, g_ref, mod_ref, o_ref):
        o_ref[...] = normmod(s_ref[...], g_ref[...], mod_ref[k0:k0 + 1, :], mod_ref[k0 + 1:k0 + 2, :]).astype(BF16)

    return pl.pallas_call(
        body, grid=(n_tiles,), in_specs=[_row(D), _full((1, D)), _modspec(lay, 6, D)], out_specs=_row(D),
        out_shape=jax.ShapeDtypeStruct((n_tiles * T, D), BF16), compiler_params=_cparams(("parallel",)), name=name)(S, g, mod)


def norm_mod_bwd(S, g, mod, k0, dxn, ds_in, lay, n_tiles, name, gate=None):
    D = S.shape[1]

    def body(*refs):
        s_ref, g_ref, mod_ref, dxn_ref, dsin_ref = refs[:5]
        i = pl.program_id(0)
        _, vjp = jax.vjp(normmod, s_ref[...], g_ref[...], mod_ref[k0:k0 + 1, :], mod_ref[k0 + 1:k0 + 2, :])
        dx, dg, dsh, dsc = vjp(dxn_ref[...])
        ds = dsin_ref[...] + dx
        if gate is None:
            ds_ref, dss_ref, dg_ref = refs[5:]
        else:
            y_ref, gmod_ref, ds_ref, dss_ref, dg_ref, dy_ref, dgate_ref = refs[5:]
            row = gate[2]
            dy_ref[...] = (gmod_ref[row:row + 1, :] * ds).astype(BF16)
            _acc(dgate_ref, jnp.sum(ds * y_ref[...], axis=0, keepdims=True), lay.first_of_mod(i))
        ds_ref[...] = ds
        _acc(dg_ref, dg, i == 0)
        _acc(dss_ref, jnp.concatenate([dsh, dsc], axis=0), lay.first_of_mod(i))

    R_ = n_tiles * T
    gated = gate is not None
    return pl.pallas_call(
        body, grid=(n_tiles,),
        in_specs=[_row(D), _full((1, D)), _modspec(lay, 6, D), _row(D), _row(D)] + ([_row(D), _modspec(lay, 6, D)] if gated else []),
        out_specs=[_row(D), _modspec(lay, 2, D), _full((1, D))] + ([_row(D), _modspec(lay, 1, D)] if gated else []),
        out_shape=[jax.ShapeDtypeStruct((R_, D), F32), jax.ShapeDtypeStruct((lay.B + 1, 2, D), F32),
                   jax.ShapeDtypeStruct((1, D), F32)]
        + ([jax.ShapeDtypeStruct((R_, D), BF16), jax.ShapeDtypeStruct((lay.B + 1, 1, D), F32)] if gated else []),
        compiler_params=_cparams(("arbitrary",)), name=name)(S, g, mod, dxn, ds_in, *(gate[:2] if gated else []))


def gate_bwd(dS, y, mod, gate_row, lay, n_tiles, name):
    D = dS.shape[1]

    def body(ds_ref, y_ref, mod_ref, dy_ref, dgate_ref):
        i = pl.program_id(0)
        ds = ds_ref[...]
        dy_ref[...] = (mod_ref[gate_row:gate_row + 1, :] * ds).astype(BF16)
        _acc(dgate_ref, jnp.sum(ds * y_ref[...], axis=0, keepdims=True), lay.first_of_mod(i))

    return pl.pallas_call(
        body, grid=(n_tiles,), in_specs=[_row(D), _row(D), _modspec(lay, 6, D)],
        out_specs=[_row(D), _modspec(lay, 1, D)],
        out_shape=[jax.ShapeDtypeStruct((n_tiles * T, D), BF16), jax.ShapeDtypeStruct((lay.B + 1, 1, D), F32)],
        compiler_params=_cparams(("arbitrary",)), name=name)(dS, y, mod)


def _swiglu(a, b):
    return a * jax.nn.sigmoid(a) * b


def _ffn_tiles(M, F):
    tn = _pick(F, 1408, 128)
    return _pick(M, (3 << 18) // tn, 128), tn


def ffn_in_act(x, ws, l, name):
    M, D = x.shape
    Cs = ws.shape[3]
    tm = _pick(M, (3 << 18) // Cs, 128)

    def body(x_ref, wa_ref, wb_ref, act_ref, a_ref, b_ref):
        dims = (((1,), (0,)), ((), ()))
        a = lax.dot_general(x_ref[...], wa_ref[...], dims, preferred_element_type=F32)
        b = lax.dot_general(x_ref[...], wb_ref[...], dims, preferred_element_type=F32)
        act_ref[...] = _swiglu(a, b).astype(BF16)
        a_ref[...] = a.astype(BF16)
        b_ref[...] = b.astype(BF16)

    out = pl.BlockSpec((tm, Cs), lambda i, j: (i, j))
    return pl.pallas_call(
        body, grid=(M // tm, 2),
        in_specs=[pl.BlockSpec((tm, D), lambda i, j: (i, 0)), pl.BlockSpec((None, None, D, Cs), lambda i, j: (j, l, 0, 0)),
                  pl.BlockSpec((None, None, D, Cs), lambda i, j: (2 + j, l, 0, 0))],
        out_specs=[out, out, out], out_shape=[jax.ShapeDtypeStruct((M, 2 * Cs), BF16)] * 3,
        compiler_params=_cparams(("parallel", "parallel")), name=name)(x, ws, ws)


def ffn_out_dx_act(df, wt, a, b, name):
    M, D = df.shape
    F = wt.shape[1]
    tm, tn = _ffn_tiles(M, F)

    def body(df_ref, w_ref, a_ref, b_ref, da_ref, db_ref):
        dact = lax.dot_general(df_ref[...], w_ref[...], (((1,), (0,)), ((), ())), preferred_element_type=F32)
        _, vjp = jax.vjp(_swiglu, a_ref[...].astype(F32), b_ref[...].astype(F32))
        da, db = vjp(dact)
        da_ref[...] = da.astype(BF16)
        db_ref[...] = db.astype(BF16)

    blk = pl.BlockSpec((tm, tn), lambda i, j: (i, j))
    return pl.pallas_call(
        body, grid=(M // tm, F // tn),
        in_specs=[pl.BlockSpec((tm, D), lambda i, j: (i, 0)), pl.BlockSpec((D, tn), lambda i, j: (0, j)), blk, blk],
        out_specs=[blk, blk], out_shape=[jax.ShapeDtypeStruct((M, F), BF16)] * 2,
        compiler_params=_cparams(("parallel", "parallel")), name=name)(df, wt, a, b)


def ffn_in_dx(da, db, ws, l, name):
    M, F = da.shape
    D, Cs = ws.shape[2], ws.shape[3]
    tm, tn = _pick(M, 512, 128), _pick(D, 512, 128)

    def body(da_ref, db_ref, w0_ref, w1_ref, w2_ref, w3_ref, o_ref):
        dims = (((1,), (1,)), ((), ()))
        dot = lambda g_ref, s, w_ref: lax.dot_general(g_ref[:, s * Cs:(s + 1) * Cs], w_ref[...], dims, preferred_element_type=F32)
        o_ref[...] = (dot(da_ref, 0, w0_ref) + dot(da_ref, 1, w1_ref)) + (dot(db_ref, 0, w2_ref) + dot(db_ref, 1, w3_ref))

    shard = lambda s: pl.BlockSpec((None, None, tn, Cs), lambda i, j: (s, l, j, 0))
    return pl.pallas_call(
        body, grid=(M // tm, D // tn),
        in_specs=[pl.BlockSpec((tm, F), lambda i, j: (i, 0)), pl.BlockSpec((tm, F), lambda i, j: (i, 0))]
        + [shard(s) for s in range(4)],
        out_specs=pl.BlockSpec((tm, tn), lambda i, j: (i, j)), out_shape=jax.ShapeDtypeStruct((M, D), F32),
        compiler_params=_cparams(("parallel", "parallel")), name=name)(da, db, ws, ws, ws, ws)


def loss_head(S, g, target, lay, name):
    D = S.shape[1]
    nxt = lay.nxt

    def tile_loss(x, gg, t):
        err = rms(x, gg) - t
        return 0.5 * jnp.sum(jnp.mean(err * err, axis=-1))

    def body(s_ref, g_ref, t_ref, loss_ref, ds_ref, dg_ref):
        i = pl.program_id(0)

        @pl.when(i < nxt)
        def _():
            val, vjp = jax.vjp(tile_loss, s_ref[...], g_ref[...], t_ref[...])
            dx, dg, _ = vjp(jnp.ones((), F32))
            ds_ref[...] = dx
            _acc(dg_ref, dg, i == 0)
            _acc(loss_ref, jnp.full((8, 128), val, F32), i == 0)

        @pl.when(i >= nxt)
        def _():
            ds_ref[...] = jnp.zeros((T, D), F32)

    return pl.pallas_call(
        body, grid=(lay.nt,),
        in_specs=[_row(D), _full((1, D)), pl.BlockSpec((T, D), lambda i: (jnp.minimum(i, nxt - 1), 0))],
        out_specs=[_full((8, 128)), _row(D), _full((1, D))],
        out_shape=[jax.ShapeDtypeStruct((8, 128), F32), jax.ShapeDtypeStruct((lay.R, D), F32),
                   jax.ShapeDtypeStruct((1, D), F32)],
        compiler_params=_cparams(("arbitrary",)), name=name)(S, g, target)


def _ab_prep(zkv, zq, zpe, c256, s256, c128, s128, gkv, gq, wk, wv, wqn, wqp):
    kvn = rms(zkv, gkv)
    kn, v = bdot(kvn, wk), bdot(kvn, wv)
    qn = rms(zq, gq)
    qnope, qpe = bdot(qn, wqn) * _MLA_SCALE, rope(bdot(qn, wqp), c256, s256) * _MLA_SCALE
    kpe = rope(zpe, c128, s128)
    fold = _fold_matrix()
    qparts, kparts = [], []
    for h in range(MLA_HEADS):
        qparts += [qnope[:, 128 * h:128 * (h + 1)], bdot(qpe * _head_mask(qpe.shape, h), fold)]
        kparts += [kn[:, 128 * h:128 * (h + 1)], kpe]
    return jnp.concatenate(qparts, axis=1), jnp.concatenate(kparts, axis=1), v


def _ab_prep_specs(lay):
    return [_row(256, 0), _row(256, 1), _row(128, 12), _ropespec(lay, 256), _ropespec(lay, 256), _ropespec(lay, 128),
            _ropespec(lay, 128), _full((1, 256)), _full((1, 256)), _full((256, 512)), _full((256, 512)),
            _full((256, 512)), _full((256, 256))]


def ab_prep_fwd(z, tabc, tabs, gkv, gq, wk, wv, wqn, wqp, lay, name):
    def body(*refs):
        ins, (q_ref, k_ref, v_ref) = refs[:13], refs[13:]
        q, k, v = _ab_prep(*[r[...].astype(F32) for r in ins])
        q_ref[...] = q.astype(BF16)
        k_ref[...] = k.astype(BF16)
        v_ref[...] = v.astype(BF16)

    R = lay.R
    return pl.pallas_call(
        body, grid=(lay.nt,), in_specs=_ab_prep_specs(lay), out_specs=[_row(1024), _row(1024), _row(512)],
        out_shape=[jax.ShapeDtypeStruct((R, 1024), BF16), jax.ShapeDtypeStruct((R, 1024), BF16),
                   jax.ShapeDtypeStruct((R, 512), BF16)],
        compiler_params=_cparams(("parallel",)), name=name)(z, z, z, tabc, tabs, tabc, tabs, gkv, gq, wk, wv, wqn, wqp)


_MLA_SCALE = (MLA_NOPE + MLA_ROPE) ** -0.5


def _mla_x(q, kx, vx, kh, vh):
    sx, sh = bdot_nt(q, kx), bdot_nt(q, kh)
    m = lax.stop_gradient(jnp.maximum(jnp.max(sx, axis=-1, keepdims=True), jnp.max(sh, axis=-1, keepdims=True)))
    ex, eh = jnp.exp(sx - m), jnp.exp(sh - m)
    inv = 1.0 / (jnp.sum(ex, axis=-1, keepdims=True) + jnp.sum(eh, axis=-1, keepdims=True))
    return (bdot(ex, vx) + bdot(eh, vh)) * inv


def _mla_h(q, kh, vh):
    sh = bdot_nt(q, kh)
    eh = jnp.exp(sh - lax.stop_gradient(jnp.max(sh, axis=-1, keepdims=True)))
    return bdot(eh, vh) * (1.0 / jnp.sum(eh, axis=-1, keepdims=True))


def _mla_specs(lay):
    nxt, SEQ = lay.nxt, lay.SEQ
    return [pl.BlockSpec((T, 256), lambda b, h, qi: (lay.qrow(b, qi), h)),
            pl.BlockSpec((SEQ, 256), lambda b, h, qi: (b, h)), pl.BlockSpec((SEQ, 128), lambda b, h, qi: (b, h)),
            pl.BlockSpec((T, 256), lambda b, h, qi: (nxt + b, h)), pl.BlockSpec((T, 128), lambda b, h, qi: (nxt + b, h))]


def mla_fwd(q, k, v, lay, name):
    tps = lay.tps

    def body(q_ref, kx_ref, vx_ref, kh_ref, vh_ref, o_ref):
        qi = pl.program_id(2)
        f = lambda r: r[...]

        @pl.when(qi < tps)
        def _():
            o_ref[...] = _mla_x(f(q_ref), f(kx_ref), f(vx_ref), f(kh_ref), f(vh_ref)).astype(BF16)

        @pl.when(qi == tps)
        def _():
            o_ref[...] = _mla_h(f(q_ref), f(kh_ref), f(vh_ref)).astype(BF16)

    return pl.pallas_call(
        body, grid=(lay.B, MLA_HEADS, lay.nq), in_specs=_mla_specs(lay),
        out_specs=pl.BlockSpec((T, 128), lambda b, h, qi: (lay.qrow(b, qi), h)),
        out_shape=jax.ShapeDtypeStruct((lay.R, 512), BF16),
        compiler_params=_cparams(("parallel", "parallel", "arbitrary")), name=name)(q, k, v, k, v)


def mla_bwd(q, k, v, dmerged, lay, name):
    tps, SEQ, B = lay.tps, lay.SEQ, lay.B

    def body(q_ref, kx_ref, vx_ref, kh_ref, vh_ref, do_ref, dq_ref, dkx_ref, dkh_ref, dvx_ref, dvh_ref):
        qi = pl.program_id(2)
        f = lambda r: r[...].astype(F32)

        @pl.when(qi < tps)
        def _():
            _, vjp = jax.vjp(_mla_x, f(q_ref), f(kx_ref), f(vx_ref), f(kh_ref), f(vh_ref))
            dq, dkx, dvx, dkh, dvh = vjp(do_ref[...])
            dq_ref[...] = dq
            _acc(dkx_ref, dkx, qi == 0)
            _acc(dvx_ref, dvx, qi == 0)
            _acc(dkh_ref, dkh, qi == 0)
            _acc(dvh_ref, dvh, qi == 0)

        @pl.when(qi == tps)
        def _():
            _, vjp = jax.vjp(_mla_h, f(q_ref), f(kh_ref), f(vh_ref))
            dq, dkh, dvh = vjp(do_ref[...])
            dq_ref[...] = dq
            dkh_ref[...] += dkh
            dvh_ref[...] += dvh

    return pl.pallas_call(
        body, grid=(B, MLA_HEADS, lay.nq),
        in_specs=_mla_specs(lay) + [pl.BlockSpec((T, 128), lambda b, h, qi: (lay.qrow(b, qi), h))],
        out_specs=[pl.BlockSpec((T, 256), lambda b, h, qi: (lay.qrow(b, qi), h)),
                   pl.BlockSpec((SEQ, 256), lambda b, h, qi: (b, h)), pl.BlockSpec((T, 256), lambda b, h, qi: (b, h)),
                   pl.BlockSpec((SEQ, 128), lambda b, h, qi: (b, h)), pl.BlockSpec((T, 128), lambda b, h, qi: (b, h))],
        out_shape=[jax.ShapeDtypeStruct((lay.R, 1024), F32), jax.ShapeDtypeStruct((lay.NX, 1024), F32),
                   jax.ShapeDtypeStruct((B * T, 1024), F32), jax.ShapeDtypeStruct((lay.NX, 512), F32),
                   jax.ShapeDtypeStruct((B * T, 512), F32)],
        compiler_params=_cparams(("parallel", "parallel", "arbitrary")), name=name)(q, k, v, k, v, dmerged)


def _cmlp_piece(zu, zv, g, ws, bs):
    u, v = jax.nn.gelu(zu), jax.nn.gelu(zv)
    v = v * lax.rsqrt(jnp.mean(v * v, axis=-1, keepdims=True) + EPS) * g
    return u * (bdot(ws, v) + bs)


def _pieces():
    return [(c, g) for c in range(T // CMLP_CHUNK) for g in range(CMLP_GROUPS)]


def cmlp_merge_fwd(z, o, gvn, ws, bs, lay, name):
    def body(zu_ref, zv_ref, o_ref, g_ref, ws_ref, bs_ref, m_ref):
        m_ref[:, 0:512] = o_ref[...]
        for c, g in _pieces():
            rows, cols = slice(128 * c, 128 * (c + 1)), slice(128 * g, 128 * (g + 1))
            piece = _cmlp_piece(zu_ref[rows, cols], zv_ref[rows, cols], g_ref[:, cols], ws_ref[g], bs_ref[g])
            m_ref[rows, 512 + 128 * g:512 + 128 * (g + 1)] = piece.astype(BF16)

    return pl.pallas_call(
        body, grid=(lay.nt,),
        in_specs=[_row(512, 1), _row(512, 2), _row(512), _full((1, 512)), _full((4, 128, 128)), _full((4, 128, 1))],
        out_specs=_row(1024), out_shape=jax.ShapeDtypeStruct((lay.R, 1024), BF16),
        compiler_params=_cparams(("parallel",)), name=name)(z, z, o, gvn, ws, bs)


def ab_rows_bwd(z, tabc, tabs, dq, dkx, dkh, dvx, dvh, dmerged, gkv, gq, wk, wv, wqn, wqp, gvn, ws, bs, lay, name):
    def body(*refs):
        prep_in = refs[:3] + refs[5:9] + refs[11:17]
        zu_ref, zv_ref = refs[3:5]
        dq_ref, dcm_ref = refs[9:11]
        gvn_ref, ws_ref, bs_ref = refs[17:20]
        dkx_ref, dkh_ref, dvx_ref, dvh_ref = refs[20:24]
        dz_ref, dgkv_ref, dgq_ref, dwk_ref, dwv_ref, dwqn_ref, dwqp_ref, dgvn_ref, dws_ref, dbs_ref = refs[24:]
        first = pl.program_id(0) == 0
        _, vjp = jax.vjp(_ab_prep, *[r[...].astype(F32) for r in prep_in])
        d = vjp((dq_ref[...], _xh_pick(lay, dkx_ref, dkh_ref), _xh_pick(lay, dvx_ref, dvh_ref)))
        dz_ref[:, 0:256] = d[0].astype(BF16)
        dz_ref[:, 256:512] = d[1].astype(BF16)
        dz_ref[:, 1536:1664] = d[2].astype(BF16)
        for ref, val in zip((dgkv_ref, dgq_ref, dwk_ref, dwv_ref, dwqn_ref, dwqp_ref), d[7:]):
            _acc(ref, val, first)
        dws = [0.0] * CMLP_GROUPS
        dbs = [0.0] * CMLP_GROUPS
        dgv = [0.0] * CMLP_GROUPS
        for c, g in _pieces():
            rows, cols = slice(128 * c, 128 * (c + 1)), slice(128 * g, 128 * (g + 1))
            _, vjp = jax.vjp(_cmlp_piece, zu_ref[rows, cols], zv_ref[rows, cols], gvn_ref[:, cols], ws_ref[g], bs_ref[g])
            dzu, dzv, dg_, dws_, dbs_ = vjp(dcm_ref[rows, cols])
            dz_ref[rows, 512 + 128 * g:512 + 128 * (g + 1)] = dzu.astype(BF16)
            dz_ref[rows, 1024 + 128 * g:1024 + 128 * (g + 1)] = dzv.astype(BF16)
            dws[g], dbs[g], dgv[g] = dws[g] + dws_, dbs[g] + dbs_, dgv[g] + dg_
        _acc(dgvn_ref, jnp.concatenate(dgv, axis=1), first)
        _acc(dws_ref, jnp.stack(dws), first)
        _acc(dbs_ref, jnp.stack(dbs), first)

    acc_shapes = [(1, 256), (1, 256), (256, 512), (256, 512), (256, 512), (256, 256), (1, 512), (4, 128, 128), (4, 128, 1)]
    return pl.pallas_call(
        body, grid=(lay.nt,),
        in_specs=_ab_prep_specs(lay)[:3] + [_row(512, 1), _row(512, 2)] + _ab_prep_specs(lay)[3:7]
        + [_row(1024), _row(512, 1)] + _ab_prep_specs(lay)[7:]
        + [_full((1, 512)), _full((4, 128, 128)), _full((4, 128, 1))] + _xh_specs(lay, 1024) + _xh_specs(lay, 512),
        out_specs=[_row(AB_IN_P)] + [_full(s) for s in acc_shapes],
        out_shape=[jax.ShapeDtypeStruct((lay.R, AB_IN_P), BF16)] + [jax.ShapeDtypeStruct(s, F32) for s in acc_shapes],
        compiler_params=_cparams(("arbitrary",)), name=name)(
            z, z, z, z, z, tabc, tabs, tabc, tabs, dq, dmerged, gkv, gq, wk, wv, wqn, wqp, gvn, ws, bs, dkx, dkh, dvx, dvh)


def _cd_prep(zrk, zrq, zsk, zsq0, zsq1, zsv, c256, s256, c128, s128):
    rk = rope(zrk * (RET_QK ** -0.5), c256, s256)
    rq = rope(zrq, c256, s256)
    sk = rope(zsk, c128, s128)
    sq0, sq1 = rope(zsq0 * _SWA_SCALE, c256, s256), rope(zsq1 * _SWA_SCALE, c256, s256)
    e0, e1 = _expand_matrix(0), _expand_matrix(1)
    return rq, rk, sq0, sq1, bdot(sk, e0), bdot(sk, e1), bdot(zsv, e0), bdot(zsv, e1)


def _cd_prep_specs(lay):
    return [_row(256, 0), _row(256, 4), _row(128, 6), _row(256, 7), _row(256, 8), _row(128, 7),
            _ropespec(lay, 256), _ropespec(lay, 256), _ropespec(lay, 128), _ropespec(lay, 128)]


def cd_prep_fwd(z, tabc, tabs, lay, name):
    def body(*refs):
        ins, (rq_ref, rk_ref, sq_ref, ke_ref, ve_ref) = refs[:10], refs[10:]
        rq, rk, sq0, sq1, k0, k1, v0, v1 = _cd_prep(*[r[...] for r in ins])
        rq_ref[...] = rq.astype(BF16)
        rk_ref[...] = rk.astype(BF16)
        for ref, (a, b) in ((sq_ref, (sq0, sq1)), (ke_ref, (k0, k1)), (ve_ref, (v0, v1))):
            ref[:, 0:256] = a.astype(BF16)
            ref[:, 256:512] = b.astype(BF16)

    R = lay.R
    return pl.pallas_call(
        body, grid=(lay.nt,), in_specs=_cd_prep_specs(lay),
        out_specs=[_row(256), _row(256), _row(512), _row(512), _row(512)],
        out_shape=[jax.ShapeDtypeStruct((R, w), BF16) for w in (256, 256, 512, 512, 512)],
        compiler_params=_cparams(("parallel",)), name=name)(z, z, z, z, z, z, tabc, tabs, tabc, tabs)


def _ret_sample(h, qs, ks, vs, df, db):
    lgf, lgb = log_sigmoid(df), log_sigmoid(db)
    idx = lax.broadcasted_iota(jnp.int32, (T, 1), 0).astype(F32)
    diff = idx - lax.broadcasted_iota(jnp.int32, (1, T), 1).astype(F32)
    intra = (jnp.where(diff >= 0, jnp.exp(lgf * jnp.maximum(diff, 0.0)), 0.0)
             + jnp.where(diff <= 0, jnp.exp(lgb * jnp.maximum(-diff, 0.0)), 0.0))
    qdf, kdf, cdf = jnp.exp(lgf * (idx + 1.0)), jnp.exp(lgf * (T - 1.0 - idx)), jnp.exp(lgf * T)
    qdb, kdb, cdb = jnp.exp(lgb * (T - idx)), jnp.exp(lgb * idx), jnp.exp(lgb * T)
    mask = _head_mask(qs[0].shape, h)
    qs = [q * mask for q in qs]
    ys = [bdot(bdot_nt(q, k) * intra, v) for q, k, v in zip(qs, ks, vs)]
    n = len(qs)
    state = bdot_tn(ks[0] * kdf, vs[0])
    for i in range(1, n):
        ys[i] = ys[i] + bdot(qs[i] * qdf, state)
        if i + 1 < n:
            state = state * cdf + bdot_tn(ks[i] * kdf, vs[i])
    state = bdot_tn(ks[0] * kdb, vs[0])
    for i in range(n - 1, 0, -1):
        ys[i] = ys[i] + bdot(qs[i] * qdb, state)
        if i > 1:
            state = state * cdb + bdot_tn(ks[i] * kdb, vs[i])
    return ys


def _ret_specs(lay):
    nxt, SEQ = lay.nxt, lay.SEQ
    xs = lambda w, col: pl.BlockSpec((SEQ, w), lambda b, h: (b, col(h)))
    hs = lambda w, col: pl.BlockSpec((T, w), lambda b, h: (nxt + b, col(h)))
    zero, head = (lambda h: 0), (lambda h: 2 + h)
    dec = pl.BlockSpec((None, 8, 128), lambda b, h: (h, 0, 0))
    return [xs(256, zero), hs(256, zero), xs(256, zero), hs(256, zero), xs(128, head), hs(128, head), dec, dec]


def _ret_tiles(x_ref, h_ref, tps, cast=None):
    tiles = [h_ref[...]] + [x_ref[i * T:(i + 1) * T, :] for i in range(tps)]
    return [t.astype(cast) for t in tiles] if cast is not None else tiles


def ret_fwd(rq, rk, z, decf, decb, lay, name):
    tps, SEQ = lay.tps, lay.SEQ

    def body(qx_ref, qh_ref, kx_ref, kh_ref, vx_ref, vh_ref, df_ref, db_ref, yx_ref, yh_ref):
        ys = _ret_sample(pl.program_id(1), _ret_tiles(qx_ref, qh_ref, tps), _ret_tiles(kx_ref, kh_ref, tps),
                         _ret_tiles(vx_ref, vh_ref, tps), df_ref[0:1, 0:1], db_ref[0:1, 0:1])
        yh_ref[...] = ys[0]
        for i in range(tps):
            yx_ref[i * T:(i + 1) * T, :] = ys[i + 1]

    return pl.pallas_call(
        body, grid=(lay.B, RET_HEADS), in_specs=_ret_specs(lay),
        out_specs=[pl.BlockSpec((SEQ, 128), lambda b, h: (b, h)), pl.BlockSpec((T, 128), lambda b, h: (b, h))],
        out_shape=[jax.ShapeDtypeStruct((lay.NX, 512), F32), jax.ShapeDtypeStruct((lay.B * T, 512), F32)],
        compiler_params=_cparams(("parallel", "arbitrary")), name=name)(rq, rq, rk, rk, z, z, decf, decb)


def ret_bwd(rq, rk, z, decf, decb, dy, lay, name):
    tps, SEQ, B = lay.tps, lay.SEQ, lay.B
    nxt = lay.nxt

    def body(qx_ref, qh_ref, kx_ref, kh_ref, vx_ref, vh_ref, df_ref, db_ref, dyx_ref, dyh_ref,
             dqx_ref, dqh_ref, dkx_ref, dkh_ref, dvx_ref, dvh_ref, ddf_ref, ddb_ref):
        h = pl.program_id(1)
        _, vjp = jax.vjp(functools.partial(_ret_sample, h), _ret_tiles(qx_ref, qh_ref, tps, F32),
                         _ret_tiles(kx_ref, kh_ref, tps, F32), _ret_tiles(vx_ref, vh_ref, tps), df_ref[0:1, 0:1],
                         db_ref[0:1, 0:1])
        dqs, dks, dvs, ddf, ddb = vjp(_ret_tiles(dyx_ref, dyh_ref, tps))
        first = h == 0
        _acc(dqh_ref, dqs[0], first)
        _acc(dkh_ref, dks[0], first)
        dvh_ref[...] = dvs[0]
        for i in range(tps):
            rows = slice(i * T, (i + 1) * T)
            _acc(dqx_ref.at[rows], dqs[i + 1], first)
            _acc(dkx_ref.at[rows], dks[i + 1], first)
            dvx_ref[rows, :] = dvs[i + 1]
        @pl.when(jnp.logical_and(pl.program_id(0) == 0, first))
        def _():
            ddf_ref[...] = jnp.zeros(ddf_ref.shape, F32)
            ddb_ref[...] = jnp.zeros(ddb_ref.shape, F32)

        ddf_ref[h] += jnp.broadcast_to(ddf, (8, 128))
        ddb_ref[h] += jnp.broadcast_to(ddb, (8, 128))

    acc_x, acc_h = pl.BlockSpec((SEQ, 256), lambda b, h: (b, 0)), pl.BlockSpec((T, 256), lambda b, h: (b, 0))
    head_x, head_h = pl.BlockSpec((SEQ, 128), lambda b, h: (b, h)), pl.BlockSpec((T, 128), lambda b, h: (b, h))
    dec = pl.BlockSpec((RET_HEADS, 8, 128), lambda b, h: (0, 0, 0))
    return pl.pallas_call(
        body, grid=(B, RET_HEADS),
        in_specs=_ret_specs(lay) + [head_x, pl.BlockSpec((T, 128), lambda b, h: (nxt + b, h))],
        out_specs=[acc_x, acc_h, acc_x, acc_h, head_x, head_h, dec, dec],
        out_shape=[jax.ShapeDtypeStruct((lay.NX, 256), F32), jax.ShapeDtypeStruct((B * T, 256), F32),
                   jax.ShapeDtypeStruct((lay.NX, 256), F32), jax.ShapeDtypeStruct((B * T, 256), F32),
                   jax.ShapeDtypeStruct((lay.NX, 512), F32), jax.ShapeDtypeStruct((B * T, 512), F32),
                   jax.ShapeDtypeStruct((RET_HEADS, 8, 128), F32), jax.ShapeDtypeStruct((RET_HEADS, 8, 128), F32)],
        compiler_params=_cparams(("arbitrary", "arbitrary")), name=name)(rq, rq, rk, rk, z, z, decf, decb, dy, dy)


_SWA_SCALE = SWA_HEAD_DIM ** -0.5


def _swa_head(qh, sw, kh, vw, vh, sink):
    sh = bdot_nt(qh, kh)
    m = jnp.maximum(jnp.max(sh, axis=-1, keepdims=True), sink)
    if sw is not None:
        m = jnp.maximum(m, jnp.max(sw, axis=-1, keepdims=True))
    m = lax.stop_gradient(m)
    eh, es = jnp.exp(sh - m), jnp.exp(sink - m)
    tot = jnp.sum(eh, axis=-1, keepdims=True) + es
    if sw is None:
        return bdot(eh, vh) * (1.0 / tot)
    ew = jnp.exp(sw - m)
    return (bdot(ew, vw) + bdot(eh, vh)) * (1.0 / (tot + jnp.sum(ew, axis=-1, keepdims=True)))


def _swa_x(t0, kpos0, sq, kw, vw, kh, vh, *sinks):
    t = t0 + lax.broadcasted_iota(jnp.int32, (T, 1), 0)
    pos = kpos0 + lax.broadcasted_iota(jnp.int32, (1, SWA_SPAN), 1)
    band = jnp.abs(t - pos) <= SWA_WINDOW
    out = 0.0
    for i in range(SWA_GROUPS):
        mi = _head_mask(sq.shape, i)
        qh = sq * mi
        sw = jnp.where(band, bdot_nt(qh, kw), NEG_INF)
        out = out + _swa_head(qh, sw, kh, vw, vh, sinks[i]) * mi
    return out


def _swa_h(sq, kh, vh, *sinks):
    out = 0.0
    for i in range(SWA_GROUPS):
        mi = _head_mask(sq.shape, i)
        out = out + _swa_head(sq * mi, None, kh, None, vh, sinks[i]) * mi
    return out


def _swa_specs(lay):
    nxt, SEQ = lay.nxt, lay.SEQ
    return [pl.BlockSpec((T, 256), lambda g, b, qi: (lay.qrow(b, qi), g)),
            pl.BlockSpec((SEQ, 256), lambda g, b, qi: (b, g)), pl.BlockSpec((SEQ, 256), lambda g, b, qi: (b, g)),
            pl.BlockSpec((T, 256), lambda g, b, qi: (nxt + b, g)), pl.BlockSpec((T, 256), lambda g, b, qi: (nxt + b, g)),
            pl.BlockSpec((None, 4, 8, 128), lambda g, b, qi: (g, 0, 0, 0))]


def _swa_start(qi, SEQ):
    return pl.multiple_of(jnp.clip(qi * T - SWA_WINDOW, 0, SEQ - SWA_SPAN), SWA_WINDOW)


def swa_fwd(sq, kexp, vexp, sink, lay, name):
    tps, SEQ = lay.tps, lay.SEQ

    def body(sq_ref, kx_ref, vx_ref, kh_ref, vh_ref, sink_ref, o_ref):
        qi = pl.program_id(2)
        f = lambda r: r[...]
        sinks = [sink_ref[i][0:1, 0:1] for i in range(SWA_GROUPS)]

        @pl.when(qi < tps)
        def _():
            k0 = _swa_start(qi, SEQ)
            kw, vw = kx_ref[pl.ds(k0, SWA_SPAN), :], vx_ref[pl.ds(k0, SWA_SPAN), :]
            o_ref[...] = _swa_x(qi * T, k0, f(sq_ref), kw, vw, f(kh_ref), f(vh_ref), *sinks).astype(BF16)

        @pl.when(qi == tps)
        def _():
            o_ref[...] = _swa_h(f(sq_ref), f(kh_ref), f(vh_ref), *sinks).astype(BF16)

    return pl.pallas_call(
        body, grid=(SWA_KV_HEADS, lay.B, lay.nq), in_specs=_swa_specs(lay),
        out_specs=pl.BlockSpec((T, 256), lambda g, b, qi: (lay.qrow(b, qi), g)),
        out_shape=jax.ShapeDtypeStruct((lay.R, 512), BF16),
        compiler_params=_cparams(("parallel", "parallel", "arbitrary")), name=name)(sq, kexp, vexp, kexp, vexp, sink)


def swa_bwd(sq, kexp, vexp, sink, dmerged, lay, name):
    tps, SEQ, B = lay.tps, lay.SEQ, lay.B

    def body(sq_ref, kx_ref, vx_ref, kh_ref, vh_ref, sink_ref, do_ref, dsq_ref, dkx_ref, dkh_ref, dvx_ref, dvh_ref, dsink_ref):
        b, qi = pl.program_id(1), pl.program_id(2)
        f = lambda r: r[...].astype(F32)
        sinks = [sink_ref[i][0:1, 0:1] for i in range(SWA_GROUPS)]
        very_first = jnp.logical_and(b == 0, qi == 0)

        def acc_sink(ds):
            for i in range(SWA_GROUPS):
                _acc(dsink_ref.at[i], jnp.broadcast_to(ds[i], (8, 128)), very_first)

        @pl.when(qi == 0)
        def _():
            for ref in (dkx_ref, dkh_ref, dvx_ref, dvh_ref):
                ref[...] = jnp.zeros(ref.shape, F32)

        @pl.when(qi < tps)
        def _():
            k0 = _swa_start(qi, SEQ)
            win = pl.ds(k0, SWA_SPAN)
            kw, vw = kx_ref[win, :].astype(F32), vx_ref[win, :].astype(F32)
            _, vjp = jax.vjp(functools.partial(_swa_x, qi * T, k0), f(sq_ref), kw, vw, f(kh_ref), f(vh_ref), *sinks)
            d = vjp(do_ref[...])
            dsq_ref[...] = d[0]
            dkx_ref[win, :] += d[1]
            dvx_ref[win, :] += d[2]
            dkh_ref[...] += d[3]
            dvh_ref[...] += d[4]
            acc_sink(d[5:9])

        @pl.when(qi == tps)
        def _():
            _, vjp = jax.vjp(_swa_h, f(sq_ref), f(kh_ref), f(vh_ref), *sinks)
            d = vjp(do_ref[...])
            dsq_ref[...] = d[0]
            dkh_ref[...] += d[1]
            dvh_ref[...] += d[2]
            acc_sink(d[3:7])

    xs = pl.BlockSpec((SEQ, 256), lambda g, b, qi: (b, g))
    hs = pl.BlockSpec((T, 256), lambda g, b, qi: (b, g))
    return pl.pallas_call(
        body, grid=(SWA_KV_HEADS, B, lay.nq),
        in_specs=_swa_specs(lay) + [pl.BlockSpec((T, 256), lambda g, b, qi: (lay.qrow(b, qi), 2 + g))],
        out_specs=[pl.BlockSpec((T, 256), lambda g, b, qi: (lay.qrow(b, qi), g)), xs, hs, xs, hs,
                   pl.BlockSpec((None, 4, 8, 128), lambda g, b, qi: (g, 0, 0, 0))],
        out_shape=[jax.ShapeDtypeStruct((lay.R, 512), F32), jax.ShapeDtypeStruct((lay.NX, 512), F32),
                   jax.ShapeDtypeStruct((B * T, 512), F32), jax.ShapeDtypeStruct((lay.NX, 512), F32),
                   jax.ShapeDtypeStruct((B * T, 512), F32), jax.ShapeDtypeStruct((SWA_KV_HEADS, 4, 8, 128), F32)],
        compiler_params=_cparams(("arbitrary", "arbitrary", "arbitrary")), name=name)(
            sq, kexp, vexp, kexp, vexp, sink, dmerged)


def _cd_merge_piece(y, rg, g):
    return (y * lax.rsqrt(jnp.mean(y * y, axis=-1, keepdims=True) + EPS) * g) * (rg * jax.nn.sigmoid(rg))


def cd_merge_fwd(y, z, o, gn, lay, name):
    def body(yx_ref, yh_ref, rga_ref, rgb_ref, o_ref, g_ref, m_ref):
        y = _xh_pick(lay, yx_ref, yh_ref)
        for h in range(RET_HEADS):
            cols = slice(128 * h, 128 * (h + 1))
            rg_ref, rcols = (rga_ref, cols) if h < 2 else (rgb_ref, slice(128 * (h - 2), 128 * (h - 1)))
            m_ref[:, cols] = _cd_merge_piece(y[:, cols], rg_ref[:, rcols], g_ref[:, cols]).astype(BF16)
        m_ref[:, 512:1024] = o_ref[...]

    return pl.pallas_call(
        body, grid=(lay.nt,), in_specs=_xh_specs(lay, 512) + [_row(256, 5), _row(256, 6), _row(512), _full((1, 512))],
        out_specs=_row(1024), out_shape=jax.ShapeDtypeStruct((lay.R, 1024), BF16),
        compiler_params=_cparams(("parallel",)), name=name)(*y, z, z, o, gn)


def cd_merge_bwd(y, z, gn, dmerged, lay, name):
    def body(yx_ref, yh_ref, rga_ref, rgb_ref, g_ref, dm_ref, dy_ref, drg_ref, dg_ref):
        first = pl.program_id(0) == 0
        y = _xh_pick(lay, yx_ref, yh_ref)
        dgs = []
        for h in range(RET_HEADS):
            cols = slice(128 * h, 128 * (h + 1))
            rg_ref, rcols = (rga_ref, cols) if h < 2 else (rgb_ref, slice(128 * (h - 2), 128 * (h - 1)))
            _, vjp = jax.vjp(_cd_merge_piece, y[:, cols], rg_ref[:, rcols], g_ref[:, cols])
            dy, drg, dg = vjp(dm_ref[:, cols])
            dy_ref[:, cols] = dy
            drg_ref[:, cols] = drg
            dgs.append(dg)
        _acc(dg_ref, jnp.concatenate(dgs, axis=1), first)

    return pl.pallas_call(
        body, grid=(lay.nt,), in_specs=_xh_specs(lay, 512) + [_row(256, 5), _row(256, 6), _full((1, 512)), _row(512, 0)],
        out_specs=[_row(512), _row(512), _full((1, 512))],
        out_shape=[jax.ShapeDtypeStruct((lay.R, 512), F32), jax.ShapeDtypeStruct((lay.R, 512), F32),
                   jax.ShapeDtypeStruct((1, 512), F32)],
        compiler_params=_cparams(("arbitrary",)), name=name)(*y, z, z, gn, dmerged)


def cd_rows_bwd(z, tabc, tabs, dsq, drg, drq, drk, dke, dve, drv, lay, name):
    def body(*refs):
        ins = refs[:10]
        dsq_ref, drg_ref = refs[10:12]
        drq, drk, dke, dve, drv = (_xh_pick(lay, refs[12 + 2 * n], refs[13 + 2 * n]) for n in range(5))
        dz_ref = refs[22]
        _, vjp = jax.vjp(_cd_prep, *[r[...] for r in ins])
        cts = (drq, drk, dsq_ref[:, 0:256], dsq_ref[:, 256:512], dke[:, 0:256], dke[:, 256:512],
               dve[:, 0:256], dve[:, 256:512])
        dzrk, dzrq, dzsk, dzsq0, dzsq1, dzsv = vjp(cts)[:6]
        dz_ref[:, 0:256] = dzrk.astype(BF16)
        dz_ref[:, 256:768] = drv.astype(BF16)
        dz_ref[:, 768:896] = dzsk.astype(BF16)
        dz_ref[:, 896:1024] = dzsv.astype(BF16)
        dz_ref[:, 1024:1280] = dzrq.astype(BF16)
        dz_ref[:, 1280:1792] = drg_ref[...].astype(BF16)
        dz_ref[:, 1792:2048] = dzsq0.astype(BF16)
        dz_ref[:, 2048:2304] = dzsq1.astype(BF16)

    return pl.pallas_call(
        body, grid=(lay.nt,),
        in_specs=_cd_prep_specs(lay) + [_row(512), _row(512)] + _xh_specs(lay, 256) + _xh_specs(lay, 256)
        + _xh_specs(lay, 512) + _xh_specs(lay, 512) + _xh_specs(lay, 512),
        out_specs=_row(2304), out_shape=jax.ShapeDtypeStruct((lay.R, 2304), BF16),
        compiler_params=_cparams(("parallel",)), name=name)(
            z, z, z, z, z, z, tabc, tabs, tabc, tabs, dsq, drg, *drq, *drk, *dke, *dve, *drv)


def _pos():
    return lax.axis_index("x"), lax.axis_index("y"), lax.axis_index("c")


def _flip(v, bit):
    return 1 - v if bit else v


def _comm_call(name, body, ins, out_shapes, n_remote, n_local, aliases=None):
    return pl.pallas_call(
        body, in_specs=[ANY] * len(ins), out_specs=[ANY] * len(out_shapes), out_shape=out_shapes,
        scratch_shapes=[pltpu.SemaphoreType.DMA((n_remote,)), pltpu.SemaphoreType.DMA((n_remote,)),
                        pltpu.SemaphoreType.DMA((n_local,))],
        input_output_aliases=aliases or {}, name=name)(*ins)


def gather8(arr, name):
    def body(a_ref, o_ref, ssem, rsem, lsem):
        x, y, c = _pos()
        me = 4 * x + 2 * y + c
        loc = pltpu.make_async_copy(a_ref, o_ref.at[me], lsem.at[0])
        loc.start()
        cps = []
        for m in range(1, 8):
            peer = (_flip(x, m & 4), _flip(y, m & 2), _flip(c, m & 1))
            cps.append(pltpu.make_async_remote_copy(a_ref, o_ref.at[me], ssem.at[m - 1], rsem.at[m - 1],
                                                    device_id=peer, device_id_type=MESH))
            cps[-1].start()
        for cp in cps:
            cp.wait()
        loc.wait()

    return _comm_call(name, body, [arr], [jax.ShapeDtypeStruct((8,) + arr.shape, arr.dtype)], 7, 1)[0]


def gather_chips(arr, name):
    def body(a_ref, o_ref, ssem, rsem, lsem):
        x, y, c = _pos()
        k = 2 * x + y
        loc = pltpu.make_async_copy(a_ref, o_ref.at[k], lsem.at[0])
        loc.start()
        cps = []
        for m in range(1, 4):
            peer = (_flip(x, m & 2), _flip(y, m & 1), c)
            cps.append(pltpu.make_async_remote_copy(a_ref, o_ref.at[k], ssem.at[m - 1], rsem.at[m - 1],
                                                    device_id=peer, device_id_type=MESH))
            cps[-1].start()
        for cp in cps:
            cp.wait()
        loc.wait()

    return _comm_call(name, body, [arr], [jax.ShapeDtypeStruct((4,) + arr.shape, arr.dtype)], 3, 1)[0]


def gather_weights(arrs, name):
    n = len(arrs)

    def body(*refs):
        a_refs, o_refs, (isend, irecv, _) = refs[:n], refs[n:2 * n], refs[2 * n:]
        x, y, c = _pos()
        k = 2 * x + y
        sib = (x, y, 1 - c)
        chips = [(m, (_flip(x, m & 2), _flip(y, m & 1)), 2 * _flip(x, m & 2) + _flip(y, m & 1)) for m in range(1, 4)]
        waits = []
        for w, (a, o) in enumerate(zip(a_refs, o_refs)):
            H = a.shape[0] // 2
            own = pl.ds(c * H, H)
            first = [pltpu.make_async_remote_copy(a.at[own], o.at[k, own], isend.at[7 * w + m - 1], irecv.at[7 * w + m - 1],
                                                  device_id=(*chip, c), device_id_type=MESH) for m, chip, _ in chips]
            first.append(pltpu.make_async_remote_copy(a, o.at[k], isend.at[7 * w + 6], irecv.at[7 * w + 6], device_id=sib,
                                                      device_id_type=MESH))
            for cp in first:
                cp.start()
            waits.append((first, H, own, a, o, w))
        for first, H, own, a, o, w in waits:
            first[3].wait_recv()
            passed = []
            for m, chip, kk in chips:
                pltpu.make_async_remote_copy(a.at[own], o.at[kk, own], isend.at[7 * w + m - 1], irecv.at[7 * w + m - 1],
                                             device_id=(*chip, c), device_id_type=MESH).wait_recv()
                fw = pltpu.make_async_remote_copy(o.at[kk, own], o.at[kk, own], isend.at[7 * w + 2 + m], irecv.at[7 * w + 2 + m],
                                                  device_id=sib, device_id_type=MESH)
                fw.start()
                passed.append(fw)
            for fw in passed:
                fw.wait_recv()
            for cp in first + passed:
                cp.wait_send()

    outs = [jax.ShapeDtypeStruct((4,) + a.shape, a.dtype) for a in arrs]
    return _comm_call(name, body, list(arrs), outs, 7 * n, 1)


def swap_other_half(arrs, name):
    n = len(arrs)

    def body(*refs):
        a_refs, o_refs, (ssem, rsem, _) = refs[:n], refs[n:2 * n], refs[2 * n:]
        x, y, c = _pos()
        cps = []
        for w, (a, o) in enumerate(zip(a_refs, o_refs)):
            H = a.shape[1] // 2
            cps.append(pltpu.make_async_remote_copy(a.at[:, pl.ds((1 - c) * H, H)], o, ssem.at[w], rsem.at[w],
                                                    device_id=(x, y, 1 - c), device_id_type=MESH))
            cps[-1].start()
        for cp in cps:
            cp.wait()

    outs = [jax.ShapeDtypeStruct((4, a.shape[1] // 2) + a.shape[2:], a.dtype) for a in arrs]
    return _comm_call(name, body, list(arrs), outs, n, 1)


def exchange_chips(arrs, name):
    n = len(arrs)

    def body(*refs):
        a_refs, o_refs, (ssem, rsem, lsem) = refs[:n], refs[n:2 * n], refs[2 * n:]
        x, y, c = _pos()
        k = 2 * x + y
        cps = []
        for w, (a, o) in enumerate(zip(a_refs, o_refs)):
            cps.append(pltpu.make_async_copy(a.at[k], o.at[k], lsem.at[w]))
            cps[-1].start()
            for m in range(1, 4):
                px, py = _flip(x, m & 2), _flip(y, m & 1)
                cps.append(pltpu.make_async_remote_copy(a.at[2 * px + py], o.at[k], ssem.at[3 * w + m - 1], rsem.at[3 * w + m - 1],
                                                        device_id=(px, py, c), device_id_type=MESH))
                cps[-1].start()
        for cp in cps:
            cp.wait()

    return _comm_call(name, body, list(arrs), [jax.ShapeDtypeStruct(a.shape, a.dtype) for a in arrs], 3 * n, n)


def share_halves(arrs, name):
    n = len(arrs)

    def body(*refs):
        o_refs, (ssem, rsem, _) = refs[n:2 * n], refs[2 * n:]
        x, y, c = _pos()
        cps = []
        for w, o in enumerate(o_refs):
            H = o.shape[0] // 2
            mine = o.at[pl.ds(c * H, H)]
            cps.append(pltpu.make_async_remote_copy(mine, mine, ssem.at[w], rsem.at[w], device_id=(x, y, 1 - c),
                                                    device_id_type=MESH))
            cps[-1].start()
        for cp in cps:
            cp.wait()

    outs = [jax.ShapeDtypeStruct(a.shape, a.dtype) for a in arrs]
    return _comm_call(name, body, list(arrs), outs, n, 1, aliases={i: i for i in range(n)})


def _adamw(w, g, m, v):
    m = ADAM_B1 * m + (1.0 - ADAM_B1) * g
    v = ADAM_B2 * v + (1.0 - ADAM_B2) * (g * g)
    m_hat = m / (1.0 - ADAM_B1 ** ADAM_STEP)
    v_hat = v / (1.0 - ADAM_B2 ** ADAM_STEP)
    return -ADAM_LR * (m_hat / (jnp.sqrt(v_hat) + ADAM_EPS) + ADAM_WD * w), m, v


def _rows_tile(R, C):
    return _pick(R, max(8, (2 << 20) // (4 * C) // 8 * 8), 8)


def chip_partial(gs, buf, cidx, name):
    _, L, R, C = gs.shape
    H, tr = L // 2, _rows_tile(R, C)

    def body(c_ref, g_ref, b_ref, o_ref):
        o_ref[...] = (g_ref[...] + b_ref[...]).astype(BF16)

    return pl.pallas_call(
        body, grid_spec=pltpu.PrefetchScalarGridSpec(
            num_scalar_prefetch=1, grid=(4, H, R // tr),
            in_specs=[pl.BlockSpec((None, None, tr, C), lambda s, l, r, c: (s, c[0] * H + l, r, 0)),
                      pl.BlockSpec((None, None, tr, C), lambda s, l, r, c: (s, l, r, 0))],
            out_specs=pl.BlockSpec((None, None, tr, C), lambda s, l, r, c: (s, l, r, 0))),
        out_shape=jax.ShapeDtypeStruct((4, H, R, C), BF16),
        compiler_params=_cparams(("parallel", "parallel", "parallel")), name=name)(cidx, gs, buf)


def sum_chips(parts, cidx, name):
    _, H, R, C = parts.shape
    tr = _rows_tile(R, C)

    def body(c_ref, p_ref, g_out):
        g = p_ref[0].astype(F32)
        for s in range(1, 4):
            g = g + p_ref[s].astype(F32)
        g_out[...] = g

    return pl.pallas_call(
        body, grid_spec=pltpu.PrefetchScalarGridSpec(
            num_scalar_prefetch=1, grid=(H, R // tr),
            in_specs=[pl.BlockSpec((4, None, tr, C), lambda l, r, c: (0, l, r, 0))],
            out_specs=pl.BlockSpec((None, tr, C), lambda l, r, c: (c[0] * H + l, r, 0))),
        out_shape=jax.ShapeDtypeStruct((2 * H, R, C), F32),
        compiler_params=_cparams(("parallel", "parallel")), name=name)(cidx, parts)


def sum8(arr, name):
    n = arr.shape[1]
    tr = _pick(n, 512, 8)

    def body(a_ref, o_ref):
        s = a_ref[0]
        for j in range(1, 8):
            s = s + a_ref[j]
        o_ref[...] = s

    return pl.pallas_call(
        body, grid=(n // tr,), in_specs=[pl.BlockSpec((8, tr, 128), lambda i: (0, i, 0))],
        out_specs=pl.BlockSpec((tr, 128), lambda i: (i, 0)), out_shape=jax.ShapeDtypeStruct((n, 128), F32),
        compiler_params=_cparams(("parallel",)), name=name)(arr)


def adam_rows(w, g, m, v, name, emit_g=False):
    n, C = w.shape
    tr = _rows_tile(n, C)

    def body(w_ref, g_ref, m_ref, v_ref, *outs):
        d_out, m_out, v_out = outs[-3:]
        d_out[...], m_out[...], v_out[...] = _adamw(w_ref[...], g_ref[...], m_ref[...], v_ref[...])
        if emit_g:
            outs[0][...] = g_ref[...]

    spec = pl.BlockSpec((tr, C), lambda i: (i, 0))
    n_out = 4 if emit_g else 3
    return pl.pallas_call(
        body, grid=(n // tr,), in_specs=[spec] * 4, out_specs=[spec] * n_out,
        out_shape=[jax.ShapeDtypeStruct((n, C), F32)] * n_out, compiler_params=_cparams(("parallel",)), name=name)(w, g, m, v)


def _silu(c):
    return c * jax.nn.sigmoid(c)


def ada_fwd(c_all, w, b, name):
    NC, D = c_all.shape
    L, _, Wc = w.shape
    tn = _pick(Wc, 512, 128)

    def body(c_ref, w_ref, b_ref, o_ref):
        o_ref[...] = bdot(_silu(c_ref[...]), w_ref[...]) + b_ref[...]

    return pl.pallas_call(
        body, grid=(L, Wc // tn),
        in_specs=[pl.BlockSpec((NC, D), lambda l, j: (0, 0)), pl.BlockSpec((None, D, tn), lambda l, j: (l, 0, j)),
                  pl.BlockSpec((None, 1, tn), lambda l, j: (l, 0, j))],
        out_specs=pl.BlockSpec((None, NC, tn), lambda l, j: (l, 0, j)), out_shape=jax.ShapeDtypeStruct((L, NC, Wc), F32),
        compiler_params=_cparams(("parallel", "parallel")), name=name)(c_all, w, b)


def ada_bwd(c_all, w, dmod, name):
    NC, D = c_all.shape
    L, _, Wc = w.shape
    tn = _pick(Wc, 512, 128)

    def body(c_ref, w_ref, d_ref, dw_ref, db_ref, dc_ref):
        first = jnp.logical_and(pl.program_id(0) == 0, pl.program_id(1) == 0)
        f = lambda cs, ww: bdot(cs, ww)
        _, vjp = jax.vjp(f, _silu(c_ref[...]), w_ref[...])
        dcs, dw = vjp(d_ref[...])
        dw_ref[...] = dw
        db_ref[...] = jnp.sum(d_ref[...], axis=0, keepdims=True)
        _acc(dc_ref, dcs, first)

    return pl.pallas_call(
        body, grid=(L, Wc // tn),
        in_specs=[pl.BlockSpec((NC, D), lambda l, j: (0, 0)), pl.BlockSpec((None, D, tn), lambda l, j: (l, 0, j)),
                  pl.BlockSpec((None, NC, tn), lambda l, j: (l, 0, j))],
        out_specs=[pl.BlockSpec((None, D, tn), lambda l, j: (l, 0, j)), pl.BlockSpec((None, 1, tn), lambda l, j: (l, 0, j)),
                   pl.BlockSpec((NC, D), lambda l, j: (0, 0))],
        out_shape=[jax.ShapeDtypeStruct((L, D, Wc), F32), jax.ShapeDtypeStruct((L, 1, Wc), F32),
                   jax.ShapeDtypeStruct((NC, D), F32)],
        compiler_params=_cparams(("arbitrary", "arbitrary")), name=name)(c_all, w, dmod)


def cctx_grad(dcs_twice, c_ctx, name):
    def body(d_ref, c_ref, o_ref):
        _, vjp = jax.vjp(_silu, c_ref[...])
        o_ref[...] = vjp(0.5 * d_ref[...])[0]

    D = c_ctx.shape[1]
    return pl.pallas_call(body, out_shape=jax.ShapeDtypeStruct((8, D), F32), name=name)(dcs_twice, c_ctx)


def _rope_tables(SEQ):
    t = jnp.arange(SEQ)
    row, col = (t // GRID_W).astype(F32), (t % GRID_W).astype(F32)
    n_freq = 16
    freqs = ROPE_THETA ** (-jnp.arange(n_freq, dtype=F32) / n_freq)
    ang = jnp.concatenate([row[:, None] * freqs, col[:, None] * freqs], axis=-1)
    cos, sin = jnp.cos(ang), jnp.sin(ang)
    c = jnp.tile(jnp.concatenate([cos, cos], axis=1), (1, 4))
    s = jnp.tile(jnp.concatenate([-sin, sin], axis=1), (1, 4))
    return (jnp.concatenate([c, jnp.ones((T, 256), F32)], axis=0), jnp.concatenate([s, jnp.zeros((T, 256), F32)], axis=0))


def _unshard_cols(g):
    return jnp.transpose(g, (1, 2, 0, 3)).reshape(g.shape[1], g.shape[2], 4 * g.shape[3])


def _unshard_rows(g):
    return jnp.transpose(g, (1, 0, 2, 3)).reshape(g.shape[1], 4 * g.shape[2], g.shape[3])


def _shard_cols(w, n=4):
    L, R, C = w.shape
    return jnp.transpose(w.reshape(L, R, n, C // n), (2, 0, 1, 3))


def _shard_rows(w):
    L, R, C = w.shape
    return jnp.transpose(w.reshape(L, 4, R // 4, C), (1, 0, 2, 3))


def _ab_in_permute(w):
    L, D, _ = w.shape
    return jnp.concatenate([w[..., 0:256], w[..., 320:1600], w[..., 256:320], jnp.zeros((L, D, 64), w.dtype)], axis=-1)


def _ab_in_unpermute(g):
    return jnp.concatenate([g[..., 0:256], g[..., 1536:1600], g[..., 256:1536]], axis=-1)


def _split_heads(w, a):
    L, K, N = w.shape
    w4 = w.reshape(L, K, 4, N // 4)
    return w4[..., :a].reshape(L, K, 4 * a), w4[..., a:].reshape(L, K, N - 4 * a)


def _join_heads(p, q):
    L, K = p.shape[:2]
    return jnp.concatenate([p.reshape(L, K, 4, -1), q.reshape(L, K, 4, -1)], axis=-1).reshape(L, K, -1)


def _pack(arrs):
    parts = []
    for a in arrs:
        f = a.reshape(-1).astype(F32)
        parts.append(jnp.pad(f, (0, (-f.shape[0]) % 1024)))
    return jnp.concatenate(parts).reshape(-1, 128)


def _unpack(buf, like):
    out, r0 = [], 0
    for a in like:
        n = math.prod(a.shape)
        rows = (n + (-n) % 1024) // 128
        out.append(buf[r0:r0 + rows].reshape(-1)[:n].reshape(a.shape))
        r0 += rows
    return out


_SMALL = ("c_ctx", "ada_b", "norm_mix", "norm_ffn", "norm_final", "mla_q_norm", "mla_kv_norm", "cmlp_v_norm", "cmlp_ws",
          "cmlp_bs", "ret_decay_fwd", "ret_decay_bwd", "ret_norm", "swa_sink")
_BIG = ("ffn_in", "ffn_out", "ab_in", "ab_out", "mla_wq_b", "mla_wkv_b", "cd_in", "cd_out")
_WEIGHTS = ("c_ctx", "ada_w", "ada_b", "norm_mix", "norm_ffn", "norm_final", "ffn_in", "ffn_out", "ab_in", "ab_out",
            "mla_q_norm", "mla_kv_norm", "mla_wq_b", "mla_wkv_b", "cmlp_v_norm", "cmlp_ws", "cmlp_bs", "cd_in", "cd_out",
            "ret_decay_fwd", "ret_decay_bwd", "ret_norm", "swa_sink")


def kernel(x, c, ctx, c_ctx, ada_w, ada_b, norm_mix, norm_ffn, norm_final, ffn_in, ffn_out, ab_in, ab_out, mla_q_norm, mla_kv_norm, mla_wq_b, mla_wkv_b, cmlp_v_norm, cmlp_ws, cmlp_bs, cd_in, cd_out, ret_decay_fwd, ret_decay_bwd, ret_norm, swa_sink, loss_target, m_c_ctx, m_ada_w, m_ada_b, m_norm_mix, m_norm_ffn, m_norm_final, m_ffn_in, m_ffn_out, m_ab_in, m_ab_out, m_mla_q_norm, m_mla_kv_norm, m_mla_wq_b, m_mla_wkv_b, m_cmlp_v_norm, m_cmlp_ws, m_cmlp_bs, m_cd_in, m_cd_out, m_ret_decay_fwd, m_ret_decay_bwd, m_ret_norm, m_swa_sink, v_c_ctx, v_ada_w, v_ada_b, v_norm_mix, v_norm_ffn, v_norm_final, v_ffn_in, v_ffn_out, v_ab_in, v_ab_out, v_mla_q_norm, v_mla_kv_norm, v_mla_wq_b, v_mla_wkv_b, v_cmlp_v_norm, v_cmlp_ws, v_cmlp_bs, v_cd_in, v_cd_out, v_ret_decay_fwd, v_ret_decay_bwd, v_ret_norm, v_swa_sink):
    W = dict(c_ctx=c_ctx, ada_w=ada_w, ada_b=ada_b, norm_mix=norm_mix, norm_ffn=norm_ffn, norm_final=norm_final, ffn_in=ffn_in, ffn_out=ffn_out, ab_in=ab_in, ab_out=ab_out, mla_q_norm=mla_q_norm, mla_kv_norm=mla_kv_norm, mla_wq_b=mla_wq_b, mla_wkv_b=mla_wkv_b, cmlp_v_norm=cmlp_v_norm, cmlp_ws=cmlp_ws, cmlp_bs=cmlp_bs, cd_in=cd_in, cd_out=cd_out, ret_decay_fwd=ret_decay_fwd, ret_decay_bwd=ret_decay_bwd, ret_norm=ret_norm, swa_sink=swa_sink)
    M1 = dict(c_ctx=m_c_ctx, ada_w=m_ada_w, ada_b=m_ada_b, norm_mix=m_norm_mix, norm_ffn=m_norm_ffn, norm_final=m_norm_final, ffn_in=m_ffn_in, ffn_out=m_ffn_out, ab_in=m_ab_in, ab_out=m_ab_out, mla_q_norm=m_mla_q_norm, mla_kv_norm=m_mla_kv_norm, mla_wq_b=m_mla_wq_b, mla_wkv_b=m_mla_wkv_b, cmlp_v_norm=m_cmlp_v_norm, cmlp_ws=m_cmlp_ws, cmlp_bs=m_cmlp_bs, cd_in=m_cd_in, cd_out=m_cd_out, ret_decay_fwd=m_ret_decay_fwd, ret_decay_bwd=m_ret_decay_bwd, ret_norm=m_ret_norm, swa_sink=m_swa_sink)
    M2 = dict(c_ctx=v_c_ctx, ada_w=v_ada_w, ada_b=v_ada_b, norm_mix=v_norm_mix, norm_ffn=v_norm_ffn, norm_final=v_norm_final, ffn_in=v_ffn_in, ffn_out=v_ffn_out, ab_in=v_ab_in, ab_out=v_ab_out, mla_q_norm=v_mla_q_norm, mla_kv_norm=v_mla_kv_norm, mla_wq_b=v_mla_wq_b, mla_wkv_b=v_mla_wkv_b, cmlp_v_norm=v_cmlp_v_norm, cmlp_ws=v_cmlp_ws, cmlp_bs=v_cmlp_bs, cd_in=v_cd_in, cd_out=v_cd_out, ret_decay_fwd=v_ret_decay_fwd, ret_decay_bwd=v_ret_decay_bwd, ret_norm=v_ret_norm, swa_sink=v_swa_sink)

    B, SEQ, D = x.shape
    CTX = ctx.shape[1]
    lay = Layout(B, SEQ, CTX, D)
    nt, NX = lay.nt, lay.NX
    ix, iy, ic = lax.axis_index("x"), lax.axis_index("y"), lax.axis_index("c")
    chip, me = 2 * ix + iy, 4 * ix + 2 * iy + ic
    cidx = jnp.reshape(ic, (1,)).astype(jnp.int32)
    Wc = ada_w.shape[2]
    n_even, n_odd = ab_in.shape[0], cd_in.shape[0]

    rn_row = jnp.pad(ret_norm.reshape(1, -1), ((0, 0), (0, D - ret_norm.size)))
    pack0 = jnp.concatenate([c, rn_row, jnp.zeros((8 - (B + 1) % 8, D), F32)], axis=0) if (B + 1) % 8 else jnp.concatenate([c, rn_row], axis=0)
    g0 = gather8(pack0, "gather_cond")
    NC = -(-(8 * B + 1) // 16) * 16
    c_all = jnp.concatenate([g0[:, :B].reshape(8 * B, D), c_ctx[None], jnp.zeros((NC - 8 * B - 1, D), F32)], axis=0)
    rn_sh = ret_norm.shape[1]
    ret_norm_full = jnp.transpose(g0[0::2, B, :ret_norm.size].reshape(4, n_odd, rn_sh), (1, 0, 2)).reshape(n_odd, 4 * rn_sh)

    gw = gather_weights([W[n].astype(BF16) for n in _BIG], "gather_weights")
    w_ab_in, w_wq, w_wkv, w_cd_in = (_unshard_cols(gw[i]) for i in (2, 4, 5, 6))
    w_ffn_out, w_ab_out, w_cd_out = (_unshard_rows(gw[i]) for i in (1, 3, 7))
    w_ab_in = _ab_in_permute(w_ab_in)
    w_ffn_out_t = jnp.transpose(w_ffn_out, (0, 2, 1))
    w_qn, w_qp = _split_heads(w_wq, MLA_NOPE)
    w_k, w_v = _split_heads(w_wkv, MLA_NOPE)

    ab_sh = lax.dynamic_slice_in_dim(ada_b, chip * Wc, Wc, axis=1)[:, None, :]
    mod_sh = ada_fwd(c_all, ada_w, ab_sh, "ada_fwd")
    mod_all = _unshard_cols(gather_chips(mod_sh, "gather_mod"))
    mod_mine = jnp.concatenate([lax.dynamic_slice_in_dim(mod_all, me * B, B, axis=1), mod_all[:, 8 * B:8 * B + 1]], axis=1)
    mod = mod_mine.reshape(DEPTH, B + 1, 6, D)

    tabc, tabs = _rope_tables(SEQ)
    bc8 = lambda a: jnp.broadcast_to(a.reshape(a.shape + (1, 1)), a.shape + (8, 128))
    row = lambda a: a.reshape(1, -1)

    S = jnp.concatenate([x.reshape(NX, D), ctx.reshape(B * CTX, D)], axis=0)
    saved = []
    xn = norm_mod_fwd(S, row(norm_mix[0]), mod[0], 0, lay, nt, "norm_mix_fwd0")
    for l in range(DEPTH):
        j, even = l // 2, l % 2 == 0
        if even:
            z = mm(xn, w_ab_in[j], name=f"ab_in{l}")
            q, k, v = ab_prep_fwd(z, tabc, tabs, row(mla_kv_norm[j]), row(mla_q_norm[j]), w_k[j], w_v[j], w_qn[j], w_qp[j],
                                  lay, f"ab_prep{l}")
            o = mla_fwd(q, k, v, lay, f"mla{l}")
            merged = cmlp_merge_fwd(z, o, row(cmlp_v_norm[j]), cmlp_ws[j], cmlp_bs[j][:, :, None], lay, f"cmlp{l}")
            w_out, mix = w_ab_out[j], (q, k, v)
        else:
            z = mm(xn, w_cd_in[j], name=f"cd_in{l}")
            rq, rk, sq, ke, ve = cd_prep_fwd(z, tabc, tabs, lay, f"cd_prep{l}")
            decf, decb, sink = bc8(ret_decay_fwd[j]), bc8(ret_decay_bwd[j]), bc8(swa_sink[j].reshape(SWA_KV_HEADS, SWA_GROUPS))
            yret = ret_fwd(rq, rk, z, decf, decb, lay, f"ret{l}")
            osw = swa_fwd(sq, ke, ve, sink, lay, f"swa{l}")
            merged = cd_merge_fwd(yret, z, osw, row(ret_norm_full[j]), lay, f"cd_merge{l}")
            w_out, mix = w_cd_out[j], (rq, rk, sq, ke, ve, decf, decb, sink, yret)
        y, S_mid, xn2 = mm_gated(merged, w_out, S, mod[l], 2, lay, name=f"mix_out{l}", n_tiles=nt,
                                 norm=(row(norm_ffn[l]), mod[l], 3))
        act, fa, fb = ffn_in_act(xn2, gw[0], l, f"ffn_in{l}")
        nxt_norm = (row(norm_mix[l + 1]), mod[l + 1], 0) if l + 1 < DEPTH else None
        f, S_new, *xn_next = mm_gated(act, w_ffn_out[l], S_mid, mod[l], 5, lay, name=f"ffn_out{l}", n_tiles=nt, norm=nxt_norm)
        saved.append((S, xn, z, mix, merged, w_out, y, S_mid, xn2, (fa, fb), act, f))
        S, xn = S_new, (xn_next[0] if xn_next else None)

    loss_blk, dS, d_norm_final = loss_head(S, row(norm_final), loss_target.reshape(NX, D), lay, "loss_head")
    loss = lax.psum(loss_blk[0, 0], ("x", "y", "c"))

    G = {n: [None] * W[n].shape[0] for n in ("norm_mix", "norm_ffn", "mla_q_norm", "mla_kv_norm", "cmlp_v_norm", "cmlp_ws",
                                             "cmlp_bs", "ret_decay_fwd", "ret_decay_bwd", "ret_norm", "swa_sink")}
    GB = {n: [None] * cnt for n, cnt in (("ffn_in_a", DEPTH), ("ffn_in_b", DEPTH), ("ffn_out", DEPTH), ("ab_in", n_even), ("ab_out", n_even),
                                         ("wqn", n_even), ("wqp", n_even), ("wk", n_even), ("wv", n_even), ("cd_in", n_odd),
                                         ("cd_out", n_odd))}
    dmod = [None] * DEPTH
    df, dgate2 = gate_bwd(dS, saved[-1][-1], mod[DEPTH - 1], 5, lay, nt, f"ffn_gate_bwd{DEPTH - 1}")
    for l in reversed(range(DEPTH)):
        j, even = l // 2, l % 2 == 0
        S_in, xn, z, mix, merged, w_out, y, S_mid, xn2, (fa, fb), act, f = saved[l]
        da, db = ffn_out_dx_act(df, w_ffn_out_t[l], fa, fb, f"ffn_out_dx{l}")
        GB["ffn_out"][l] = mm(act, df, ta=True, name=f"ffn_out_dw{l}")
        GB["ffn_in_a"][l] = mm(xn2, da, ta=True, name=f"ffn_in_dwa{l}", split=2)
        GB["ffn_in_b"][l] = mm(xn2, db, ta=True, name=f"ffn_in_dwb{l}", split=2)
        dxn2 = ffn_in_dx(da, db, gw[0], l, f"ffn_in_dx{l}")
        dS_mid, dss2, dg, dy, dgate1 = norm_mod_bwd(S_mid, row(norm_ffn[l]), mod[l], 3, dxn2, dS, lay, nt, f"norm_ffn_bwd{l}",
                                                    gate=(y, mod[l], 2))
        G["norm_ffn"][l] = dg
        dmerged = mm(dy, w_out, tb=True, name=f"mix_out_dx{l}")
        d_w_out = mm(merged, dy, ta=True, name=f"mix_out_dw{l}")
        if even:
            q, k, v = mix
            dq, dkx, dkh, dvx, dvh = mla_bwd(q, k, v, dmerged, lay, f"mla_bwd{l}")
            (dz, dgkv, dgq, dwk, dwv, dwqn, dwqp, dgvn, dws, dbs) = ab_rows_bwd(
                z, tabc, tabs, dq, dkx, dkh, dvx, dvh, dmerged, row(mla_kv_norm[j]), row(mla_q_norm[j]), w_k[j], w_v[j],
                w_qn[j], w_qp[j], row(cmlp_v_norm[j]), cmlp_ws[j], cmlp_bs[j][:, :, None], lay, f"ab_rows_bwd{l}")
            G["mla_kv_norm"][j], G["mla_q_norm"][j], G["cmlp_v_norm"][j] = dgkv, dgq, dgvn
            G["cmlp_ws"][j], G["cmlp_bs"][j] = dws, dbs
            GB["wk"][j], GB["wv"][j], GB["wqn"][j], GB["wqp"][j], GB["ab_out"][j] = dwk, dwv, dwqn, dwqp, d_w_out
            w_in = w_ab_in[j]
        else:
            rq, rk, sq, ke, ve, decf, decb, sink, yret = mix
            dyret, drg, dgn = cd_merge_bwd(yret, z, row(ret_norm_full[j]), dmerged, lay, f"cd_merge_bwd{l}")
            dqx, dqh, dkx, dkh, dvx, dvh, ddf, ddb = ret_bwd(rq, rk, z, decf, decb, dyret, lay, f"ret_bwd{l}")
            dsq, dkex, dkeh, dvex, dveh, dsink = swa_bwd(sq, ke, ve, sink, dmerged, lay, f"swa_bwd{l}")
            dz = cd_rows_bwd(z, tabc, tabs, dsq, drg, (dqx, dqh), (dkx, dkh), (dkex, dkeh), (dvex, dveh), (dvx, dvh), lay,
                             f"cd_rows_bwd{l}")
            G["ret_norm"][j], G["ret_decay_fwd"][j], G["ret_decay_bwd"][j] = dgn, ddf[:, 0, 0], ddb[:, 0, 0]
            G["swa_sink"][j] = dsink[:, :, 0, 0].reshape(-1)
            GB["cd_out"][j] = d_w_out
            w_in = w_cd_in[j]
        GB["ab_in" if even else "cd_in"][j] = mm(xn, dz, ta=True, name=f"mix_in_dw{l}")
        dxn = mm(dz, w_in, tb=True, name=f"mix_in_dx{l}")
        dmod_l = lambda dss1: jnp.concatenate([dss1, dgate1, dss2, dgate2], axis=1)
        if l > 0:
            dS, dss1, dg, df, dgate2_prev = norm_mod_bwd(S_in, row(norm_mix[l]), mod[l], 0, dxn, dS_mid, lay, nt,
                                                         f"norm_mix_bwd{l}", gate=(saved[l - 1][-1], mod[l - 1], 5))
            dmod[l], dgate2 = dmod_l(dss1), dgate2_prev
        else:
            dS, dss1, dg = norm_mod_bwd(S_in, row(norm_mix[l]), mod[l], 0, dxn, dS_mid, lay, nt, f"norm_mix_bwd{l}")
            dmod[l] = dmod_l(dss1)
        G["norm_mix"][l] = dg
    grad_x = dS[:NX].reshape(B, SEQ, D)

    st = lambda n: jnp.stack([g.reshape((4 * rn_sh,) if n == "ret_norm" else W[n].shape[1:]) for g in G[n]])
    small_parts = {n: st(n) for n in G}
    small_parts["norm_final"] = d_norm_final.reshape(-1)
    dmod_local = jnp.stack(dmod).reshape(DEPTH, B + 1, 6 * D)
    names1 = ["norm_mix", "norm_ffn", "norm_final", "mla_q_norm", "mla_kv_norm", "cmlp_v_norm", "cmlp_ws", "cmlp_bs",
              "ret_decay_fwd", "ret_decay_bwd", "ret_norm", "swa_sink"]
    like1 = [dmod_local] + [small_parts[n] for n in names1]
    g1 = gather8(_pack(like1), "gather_small_grads")
    tot1 = _unpack(sum8(g1, "sum_small_grads"), like1)
    sg = dict(zip(names1, tot1[1:]))
    n_dm = math.prod(dmod_local.shape)
    dm_each = g1[:, :-(-n_dm // 128)].reshape(8, -1)[:, :n_dm].reshape(8, DEPTH, B + 1, 6 * D)
    dmod_all = jnp.concatenate([jnp.transpose(dm_each[:, :, :B], (1, 0, 2, 3)).reshape(DEPTH, 8 * B, 6 * D),
                                tot1[0][:, B:B + 1], jnp.zeros((DEPTH, NC - 8 * B - 1, 6 * D), F32)], axis=1)
    dmod_sh = lax.dynamic_slice_in_dim(dmod_all, chip * Wc, Wc, axis=2)
    g_ada_w, g_ada_b_sh, dcs = ada_bwd(c_all, ada_w, dmod_sh, "ada_bwd")
    like2 = [dcs[8 * B], g_ada_b_sh]
    g2 = gather8(_pack(like2), "gather_ada_grads")
    tot2 = _unpack(sum8(g2, "sum_ada_grads"), like2)
    bc = lambda a: jnp.broadcast_to(a.reshape(1, D), (8, D))
    sg["c_ctx"] = cctx_grad(bc(tot2[0]), bc(c_ctx), "c_ctx_grad")[0]
    off = D + (-D) % 1024
    gab = g2[0::2, off // 128:(off + DEPTH * Wc) // 128].reshape(4, DEPTH, Wc)
    sg["ada_b"] = jnp.transpose(gab, (1, 0, 2)).reshape(DEPTH, 4 * Wc)
    sg["ret_norm"] = lax.dynamic_slice_in_dim(sg["ret_norm"], chip * rn_sh, rn_sh, axis=1)

    stack = lambda n: jnp.stack(GB[n])
    gs = {"ffn_in": jnp.concatenate([jnp.stack(GB[n], axis=1) for n in ("ffn_in_a", "ffn_in_b")], axis=0),
          "ffn_out": _shard_rows(stack("ffn_out")),
          "ab_in": _shard_cols(_ab_in_unpermute(stack("ab_in"))), "ab_out": _shard_rows(stack("ab_out")),
          "mla_wq_b": _shard_cols(_join_heads(stack("wqn"), stack("wqp"))),
          "mla_wkv_b": _shard_cols(_join_heads(stack("wk"), stack("wv"))),
          "cd_in": _shard_cols(stack("cd_in")), "cd_out": _shard_rows(stack("cd_out"))}
    bufs = swap_other_half([gs[n] for n in _BIG], "swap_core_halves")
    parts = [chip_partial(gs[n], b, cidx, f"chip_partial_{n}") for n, b in zip(_BIG, bufs)]
    arrived = exchange_chips(parts, "exchange_chips")
    grads = share_halves([sum_chips(p, cidx, f"sum_chips_{n}") for n, p in zip(_BIG, arrived)], "share_core_halves")
    flat2 = lambda a: a.reshape(-1, a.shape[-1])
    out = {}
    for n, g in zip(_BIG, grads):
        res = adam_rows(flat2(W[n]), flat2(g), flat2(M1[n]), flat2(M2[n]), f"adam_{n}", emit_g=True)
        out[n] = tuple(r.reshape(W[n].shape) for r in res)

    like_s = [W[n] for n in _SMALL]
    dsm, msm, vsm = adam_rows(_pack(like_s), _pack([sg[n].reshape(W[n].shape) for n in _SMALL]), _pack([M1[n] for n in _SMALL]),
                                _pack([M2[n] for n in _SMALL]), "adam_small")
    for n, d_, m_, v_ in zip(_SMALL, _unpack(dsm, like_s), _unpack(msm, like_s), _unpack(vsm, like_s)):
        out[n] = (sg[n].reshape(W[n].shape), d_, m_, v_)
    d_, m_, v_ = adam_rows(flat2(ada_w), flat2(g_ada_w), flat2(m_ada_w), flat2(v_ada_w), "adam_ada_w")
    out["ada_w"] = (g_ada_w, d_.reshape(ada_w.shape), m_.reshape(ada_w.shape), v_.reshape(ada_w.shape))

    return (loss, grad_x, *[out[n][0] for n in _WEIGHTS], *[out[n][1] for n in _WEIGHTS], *[out[n][2] for n in _WEIGHTS],
            *[out[n][3] for n in _WEIGHTS])
```

```python
import functools
import math

import jax
import jax.numpy as jnp
import numpy as np
from jax import lax
from jax.experimental import pallas as pl
from jax.experimental.pallas import tpu as pltpu

F32 = jnp.float32
BF16 = jnp.bfloat16
EPS = 1e-6
NEG_INF = -1e30
GRID_W = 64
ROPE_THETA = 10000.0
DEPTH = 4
MLA_HEADS, MLA_Q_LORA, MLA_KV_LORA, MLA_NOPE, MLA_ROPE, MLA_V = 4, 256, 256, 128, 64, 128
CMLP_GROUPS, CMLP_CHUNK = 4, 128
CMLP_WIDTH = 512
RET_HEADS, RET_QK, RET_V = 4, 64, 128
SWA_Q_HEADS, SWA_KV_HEADS, SWA_HEAD_DIM, SWA_WINDOW = 8, 2, 64, 128
SWA_GROUPS = SWA_Q_HEADS // SWA_KV_HEADS
AB_IN_P = 1664
ADAM_LR, ADAM_B1, ADAM_B2, ADAM_EPS, ADAM_WD, ADAM_STEP = 0.001, 0.9, 0.999, 1e-08, 0.01, 10

T = 256
SWA_SPAN = T + 2 * SWA_WINDOW
VMEM_LIMIT = 48 * 1024 * 1024
MESH = pl.DeviceIdType.MESH
ANY = pl.BlockSpec(memory_space=pl.ANY)


def _cparams(sem):
    return pltpu.CompilerParams(dimension_semantics=sem, vmem_limit_bytes=VMEM_LIMIT)


@functools.cache
def _bdot_fn(ca, cb):
    fa, fb = 1 - ca, 1 - cb

    def dg(p, q, cp, cq):
        return lax.dot_general(p.astype(BF16), q.astype(BF16), (((cp,), (cq,)), ((), ())), preferred_element_type=F32)

    @jax.custom_vjp
    def bd(a, b):
        return dg(a, b, ca, cb)

    def fwd(a, b):
        return dg(a, b, ca, cb), (a, b)

    def bwd(res, g):
        a, b = res
        da = dg(g, b, 1, fb) if ca == 1 else dg(b, g, fb, 1)
        db = dg(a, g, fa, 0) if cb == 0 else dg(g, a, 0, fa)
        return da, db

    bd.defvjp(fwd, bwd)
    return bd


def bdot(a, b):
    return _bdot_fn(1, 0)(a, b)


def bdot_nt(a, b):
    return _bdot_fn(1, 1)(a, b)


def bdot_tn(a, b):
    return _bdot_fn(0, 0)(a, b)


def _swap32(x):
    w = x.shape[-1]
    lane = lax.broadcasted_iota(jnp.int32, x.shape, 1)
    return jnp.where((lane & 32) == 0, pltpu.roll(x, w - 32, 1), pltpu.roll(x, 32, 1))


@jax.custom_vjp
def rope(x, c, s):
    return x * c + _swap32(x) * s


def _rope_fwd(x, c, s):
    return rope(x, c, s), (c, s)


def _rope_bwd(res, g):
    c, s = res
    return g * c + _swap32(g * s), jnp.zeros_like(c), jnp.zeros_like(s)


rope.defvjp(_rope_fwd, _rope_bwd)


def rms(x, g):
    return x * lax.rsqrt(jnp.mean(x * x, axis=-1, keepdims=True) + EPS) * g


def normmod(x, g, sh, sc):
    return rms(x, g) * (1.0 + sc) + sh


def log_sigmoid(x):
    return jnp.minimum(x, 0.0) - jnp.log(1.0 + jnp.exp(-jnp.abs(x)))


def _head_mask(shape, h):
    lane = lax.broadcasted_iota(jnp.int32, shape, 1)
    return ((lane >> 6) == h).astype(F32)


def _fold_matrix():
    i = lax.broadcasted_iota(jnp.int32, (256, 128), 0)
    j = lax.broadcasted_iota(jnp.int32, (256, 128), 1)
    return ((i & 63) == j).astype(F32)


def _expand_matrix(g):
    i = lax.broadcasted_iota(jnp.int32, (128, 256), 0)
    j = lax.broadcasted_iota(jnp.int32, (128, 256), 1)
    return (i == (j & 63) + 64 * g).astype(F32)


def _acc(ref, val, first):
    @pl.when(first)
    def _():
        ref[...] = val

    @pl.when(jnp.logical_not(first))
    def _():
        ref[...] += val


def _pick(n, cap, mult):
    best = None
    for d in range(mult, min(n, cap) + 1, mult):
        if n % d == 0:
            best = d
    return best if best is not None else n


def mm(a, b, *, ta=False, tb=False, name, split=1):
    M, K = (a.shape[1], a.shape[0]) if ta else a.shape
    N = b.shape[0] if tb else b.shape[1]
    Ns = N // split
    assert split == 1 or (ta and Ns % 128 == 0)
    if ta:
        tn = N
        tm = _pick(M, min(1536, (3 << 20) // tn), 128)
    else:
        tn = _pick(N, 768, 128)
        if tn < 256 and N <= 2304:
            tn = N
        tm = _pick(M, min(1536, (1 << 20) // tn), 128)
    tk = _pick(K, 2048 if not ta else 1024, 128) if K > 2816 or ta else K
    nk = K // tk
    grid = (M // tm, N // tn, nk)
    a_spec = pl.BlockSpec((tk, tm), lambda i, j, k: (k, i)) if ta else pl.BlockSpec((tm, tk), lambda i, j, k: (i, k))
    b_spec = pl.BlockSpec((tn, tk), lambda i, j, k: (j, k)) if tb else pl.BlockSpec((tk, tn), lambda i, j, k: (k, j))
    dims = (((0 if ta else 1,), (1 if tb else 0,)), ((), ()))

    def body(a_ref, b_ref, o_ref):
        part = lax.dot_general(a_ref[...], b_ref[...], dims, preferred_element_type=F32)
        first = pl.program_id(2) == 0
        if split > 1:
            for s in range(split):
                _acc(o_ref.at[s], part[:, s * Ns:(s + 1) * Ns], first)
        elif nk == 1:
            o_ref[...] = part
        else:
            _acc(o_ref, part, first)

    if split > 1:
        out_spec, out_shape = pl.BlockSpec((split, tm, Ns), lambda i, j, k: (0, i, 0)), (split, M, Ns)
    else:
        out_spec, out_shape = pl.BlockSpec((tm, tn), lambda i, j, k: (i, j)), (M, N)
    return pl.pallas_call(
        body, grid=grid, in_specs=[a_spec, b_spec], out_specs=out_spec, out_shape=jax.ShapeDtypeStruct(out_shape, F32),
        compiler_params=_cparams(("parallel", "parallel", "arbitrary")), name=name)(a, b)


def mm_gated(a, b, res, mod, gate_row, lay, *, name, n_tiles, norm=None):
    K, N = b.shape
    M = n_tiles * T

    def body(*refs):
        a_ref, b_ref, r_ref, g_ref = refs[:4]
        y = lax.dot_general(a_ref[...], b_ref[...], (((1,), (0,)), ((), ())), preferred_element_type=F32)
        new = r_ref[...] + g_ref[gate_row:gate_row + 1, :] * y
        if norm is None:
            y_ref, o_ref = refs[4:]
        else:
            gn_ref, mn_ref, y_ref, o_ref, xn_ref = refs[4:]
            k0 = norm[2]
            xn_ref[...] = normmod(new, gn_ref[...], mn_ref[k0:k0 + 1, :], mn_ref[k0 + 1:k0 + 2, :]).astype(BF16)
        y_ref[...] = y.astype(BF16)
        o_ref[...] = new

    rows = pl.BlockSpec((T, N), lambda i: (i, 0))
    extra = [] if norm is None else [_full((1, N)), _modspec(lay, 6, N)]
    return pl.pallas_call(
        body, grid=(n_tiles,),
        in_specs=[pl.BlockSpec((T, K), lambda i: (i, 0)), _full((K, N)), rows, _modspec(lay, 6, N)] + extra,
        out_specs=[rows, rows] + ([] if norm is None else [rows]),
        out_shape=[jax.ShapeDtypeStruct((M, N), BF16), jax.ShapeDtypeStruct((M, N), F32)]
        + ([] if norm is None else [jax.ShapeDtypeStruct((M, N), BF16)]),
        compiler_params=_cparams(("parallel",)), name=name)(a, b, res, mod, *([] if norm is None else norm[:2]))


class Layout:
    def __init__(self, B, SEQ, CTX, D):
        assert CTX == T and SEQ % T == 0 and SEQ >= SWA_SPAN
        self.B, self.SEQ, self.CTX, self.D = B, SEQ, CTX, D
        self.tps = SEQ // T
        self.nxt = B * self.tps
        self.nt = self.nxt + B
        self.NX, self.R = B * SEQ, B * SEQ + B * CTX
        self.nq = self.tps + 1

    def mod_idx(self, i):
        return jnp.where(i < self.nxt, i // self.tps, self.B)

    def rope_idx(self, i):
        return jnp.where(i < self.nxt, i % self.tps, self.tps)

    def first_of_mod(self, i):
        return jnp.logical_or(jnp.logical_and(i < self.nxt, i % self.tps == 0), i == self.nxt)

    def qrow(self, b, qi):
        return jnp.where(qi < self.tps, b * self.tps + qi, self.nxt + b)


def _row(w, col=0):
    return pl.BlockSpec((T, w), lambda i: (i, col))


def _full(shape):
    nd = len(shape)
    return pl.BlockSpec(shape, lambda i: (0,) * nd)


def _modspec(lay, rows, D):
    return pl.BlockSpec((None, rows, D), lambda i: (lay.mod_idx(i), 0, 0))


def _ropespec(lay, w):
    return pl.BlockSpec((T, w), lambda i: (lay.rope_idx(i), 0))


def _xh_specs(lay, w):
    nxt = lay.nxt
    return [pl.BlockSpec((T, w), lambda i: (jnp.minimum(i, nxt - 1), 0)), pl.BlockSpec((T, w), lambda i: (jnp.maximum(i - nxt, 0), 0))]


def _xh_pick(lay, x_ref, h_ref):
    return jnp.where(pl.program_id(0) < lay.nxt, x_ref[...], h_ref[...])


def norm_mod_fwd(S, g, mod, k0, lay, n_tiles, name):
    D = S.shape[1]

    def body(s_ref, g_ref, mod_ref, o_ref):
        o_ref[...] = normmod(s_ref[...], g_ref[...], mod_ref[k0:k0 + 1, :], mod_ref[k0 + 1:k0 + 2, :]).astype(BF16)

    return pl.pallas_call(
        body, grid=(n_tiles,), in_specs=[_row(D), _full((1, D)), _modspec(lay, 6, D)], out_specs=_row(D),
        out_shape=jax.ShapeDtypeStruct((n_tiles * T, D), BF16), compiler_params=_cparams(("parallel",)), name=name)(S, g, mod)


def norm_mod_bwd(S, g, mod, k0, dxn, ds_in, lay, n_tiles, name, gate=None):
    D = S.shape[1]

    def body(*refs):
        s_ref, g_ref, mod_ref, dxn_ref, dsin_ref = refs[:5]
        i = pl.program_id(0)
        _, vjp = jax.vjp(normmod, s_ref[...], g_ref[...], mod_ref[k0:k0 + 1, :], mod_ref[k0 + 1:k0 + 2, :])
        dx, dg, dsh, dsc = vjp(dxn_ref[...])
        ds = dsin_ref[...] + dx
        if gate is None:
            ds_ref, dss_ref, dg_ref = refs[5:]
        else:
            y_ref, gmod_ref, ds_ref, dss_ref, dg_ref, dy_ref, dgate_ref = refs[5:]
            row = gate[2]
            dy_ref[...] = (gmod_ref[row:row + 1, :] * ds).astype(BF16)
            _acc(dgate_ref, jnp.sum(ds * y_ref[...], axis=0, keepdims=True), lay.first_of_mod(i))
        ds_ref[...] = ds
        _acc(dg_ref, dg, i == 0)
        _acc(dss_ref, jnp.concatenate([dsh, dsc], axis=0), lay.first_of_mod(i))

    R_ = n_tiles * T
    gated = gate is not None
    return pl.pallas_call(
        body, grid=(n_tiles,),
        in_specs=[_row(D), _full((1, D)), _modspec(lay, 6, D), _row(D), _row(D)] + ([_row(D), _modspec(lay, 6, D)] if gated else []),
        out_specs=[_row(D), _modspec(lay, 2, D), _full((1, D))] + ([_row(D), _modspec(lay, 1, D)] if gated else []),
        out_shape=[jax.ShapeDtypeStruct((R_, D), F32), jax.ShapeDtypeStruct((lay.B + 1, 2, D), F32),
                   jax.ShapeDtypeStruct((1, D), F32)]
        + ([jax.ShapeDtypeStruct((R_, D), BF16), jax.ShapeDtypeStruct((lay.B + 1, 1, D), F32)] if gated else []),
        compiler_params=_cparams(("arbitrary",)), name=name)(S, g, mod, dxn, ds_in, *(gate[:2] if gated else []))


def gate_bwd(dS, y, mod, gate_row, lay, n_tiles, name):
    D = dS.shape[1]

    def body(ds_ref, y_ref, mod_ref, dy_ref, dgate_ref):
        i = pl.program_id(0)
        ds = ds_ref[...]
        dy_ref[...] = (mod_ref[gate_row:gate_row + 1, :] * ds).astype(BF16)
        _acc(dgate_ref, jnp.sum(ds * y_ref[...], axis=0, keepdims=True), lay.first_of_mod(i))

    return pl.pallas_call(
        body, grid=(n_tiles,), in_specs=[_row(D), _row(D), _modspec(lay, 6, D)],
        out_specs=[_row(D), _modspec(lay, 1, D)],
        out_shape=[jax.ShapeDtypeStruct((n_tiles * T, D), BF16), jax.ShapeDtypeStruct((lay.B + 1, 1, D), F32)],
        compiler_params=_cparams(("arbitrary",)), name=name)(dS, y, mod)


def _swiglu(a, b):
    return a * jax.nn.sigmoid(a) * b


def _ffn_tiles(M, F):
    tn = _pick(F, 1408, 128)
    return _pick(M, (3 << 18) // tn, 128), tn


def ffn_in_act(x, ws, l, name):
    M, D = x.shape
    Cs = ws.shape[3]
    tm = _pick(M, (3 << 18) // Cs, 128)

    def body(x_ref, wa_ref, wb_ref, act_ref, a_ref, b_ref):
        dims = (((1,), (0,)), ((), ()))
        a = lax.dot_general(x_ref[...], wa_ref[...], dims, preferred_element_type=F32)
        b = lax.dot_general(x_ref[...], wb_ref[...], dims, preferred_element_type=F32)
        act_ref[...] = _swiglu(a, b).astype(BF16)
        a_ref[...] = a.astype(BF16)
        b_ref[...] = b.astype(BF16)

    out = pl.BlockSpec((tm, Cs), lambda j, i: (i, j))
    return pl.pallas_call(
        body, grid=(2, M // tm),
        in_specs=[pl.BlockSpec((tm, D), lambda j, i: (i, 0)), pl.BlockSpec((None, None, D, Cs), lambda j, i: (j, l, 0, 0)),
                  pl.BlockSpec((None, None, D, Cs), lambda j, i: (2 + j, l, 0, 0))],
        out_specs=[out, out, out], out_shape=[jax.ShapeDtypeStruct((M, 2 * Cs), BF16)] * 3,
        compiler_params=_cparams(("parallel", "parallel")), name=name)(x, ws, ws)


def ffn_out_dx_act(df, wt, a, b, name):
    M, D = df.shape
    F = wt.shape[1]
    tm, tn = _ffn_tiles(M, F)

    def body(df_ref, w_ref, a_ref, b_ref, da_ref, db_ref):
        dact = lax.dot_general(df_ref[...], w_ref[...], (((1,), (0,)), ((), ())), preferred_element_type=F32)
        _, vjp = jax.vjp(_swiglu, a_ref[...].astype(F32), b_ref[...].astype(F32))
        da, db = vjp(dact)
        da_ref[...] = da.astype(BF16)
        db_ref[...] = db.astype(BF16)

    blk = pl.BlockSpec((tm, tn), lambda j, i: (i, j))
    return pl.pallas_call(
        body, grid=(F // tn, M // tm),
        in_specs=[pl.BlockSpec((tm, D), lambda j, i: (i, 0)), pl.BlockSpec((D, tn), lambda j, i: (0, j)), blk, blk],
        out_specs=[blk, blk], out_shape=[jax.ShapeDtypeStruct((M, F), BF16)] * 2,
        compiler_params=_cparams(("parallel", "parallel")), name=name)(df, wt, a, b)


def ffn_in_dx(da, db, ws, l, name):
    M, F = da.shape
    D, Cs = ws.shape[2], ws.shape[3]
    tm, tn = _pick(M, 512, 128), _pick(D, 1024, 128)

    def body(da_ref, db_ref, w0_ref, w1_ref, w2_ref, w3_ref, o_ref):
        dims = (((1,), (1,)), ((), ()))
        dot = lambda g_ref, s, w_ref: lax.dot_general(g_ref[:, s * Cs:(s + 1) * Cs], w_ref[...], dims, preferred_element_type=F32)
        o_ref[...] = (dot(da_ref, 0, w0_ref) + dot(da_ref, 1, w1_ref)) + (dot(db_ref, 0, w2_ref) + dot(db_ref, 1, w3_ref))

    shard = lambda s: pl.BlockSpec((None, None, tn, Cs), lambda i, j: (s, l, j, 0))
    return pl.pallas_call(
        body, grid=(M // tm, D // tn),
        in_specs=[pl.BlockSpec((tm, F), lambda i, j: (i, 0)), pl.BlockSpec((tm, F), lambda i, j: (i, 0))]
        + [shard(s) for s in range(4)],
        out_specs=pl.BlockSpec((tm, tn), lambda i, j: (i, j)), out_shape=jax.ShapeDtypeStruct((M, D), F32),
        compiler_params=_cparams(("parallel", "parallel")), name=name)(da, db, ws, ws, ws, ws)


def loss_head(S, g, target, lay, name):
    D = S.shape[1]
    nxt = lay.nxt

    def tile_loss(x, gg, t):
        err = rms(x, gg) - t
        return 0.5 * jnp.sum(jnp.mean(err * err, axis=-1))

    def body(s_ref, g_ref, t_ref, loss_ref, ds_ref, dg_ref):
        i = pl.program_id(0)

        @pl.when(i < nxt)
        def _():
            val, vjp = jax.vjp(tile_loss, s_ref[...], g_ref[...], t_ref[...])
            dx, dg, _ = vjp(jnp.ones((), F32))
            ds_ref[...] = dx
            _acc(dg_ref, dg, i == 0)
            _acc(loss_ref, jnp.full((8, 128), val, F32), i == 0)

        @pl.when(i >= nxt)
        def _():
            ds_ref[...] = jnp.zeros((T, D), F32)

    return pl.pallas_call(
        body, grid=(lay.nt,),
        in_specs=[_row(D), _full((1, D)), pl.BlockSpec((T, D), lambda i: (jnp.minimum(i, nxt - 1), 0))],
        out_specs=[_full((8, 128)), _row(D), _full((1, D))],
        out_shape=[jax.ShapeDtypeStruct((8, 128), F32), jax.ShapeDtypeStruct((lay.R, D), F32),
                   jax.ShapeDtypeStruct((1, D), F32)],
        compiler_params=_cparams(("arbitrary",)), name=name)(S, g, target)


def _ab_prep(zkv, zq, zpe, c256, s256, c128, s128, gkv, gq, wk, wv, wqn, wqp):
    kvn = rms(zkv, gkv)
    kn, v = bdot(kvn, wk), bdot(kvn, wv)
    qn = rms(zq, gq)
    qnope, qpe = bdot(qn, wqn) * _MLA_SCALE, rope(bdot(qn, wqp), c256, s256) * _MLA_SCALE
    kpe = rope(zpe, c128, s128)
    fold = _fold_matrix()
    qparts, kparts = [], []
    for h in range(MLA_HEADS):
        qparts += [qnope[:, 128 * h:128 * (h + 1)], bdot(qpe * _head_mask(qpe.shape, h), fold)]
        kparts += [kn[:, 128 * h:128 * (h + 1)], kpe]
    return jnp.concatenate(qparts, axis=1), jnp.concatenate(kparts, axis=1), v


def _ab_prep_specs(lay):
    return [_row(256, 0), _row(256, 1), _row(128, 12), _ropespec(lay, 256), _ropespec(lay, 256), _ropespec(lay, 128),
            _ropespec(lay, 128), _full((1, 256)), _full((1, 256)), _full((256, 512)), _full((256, 512)),
            _full((256, 512)), _full((256, 256))]


def ab_prep_fwd(z, tabc, tabs, gkv, gq, wk, wv, wqn, wqp, lay, name):
    def body(*refs):
        ins, (q_ref, k_ref, v_ref) = refs[:13], refs[13:]
        q, k, v = _ab_prep(*[r[...].astype(F32) for r in ins])
        q_ref[...] = q.astype(BF16)
        k_ref[...] = k.astype(BF16)
        v_ref[...] = v.astype(BF16)

    R = lay.R
    return pl.pallas_call(
        body, grid=(lay.nt,), in_specs=_ab_prep_specs(lay), out_specs=[_row(1024), _row(1024), _row(512)],
        out_shape=[jax.ShapeDtypeStruct((R, 1024), BF16), jax.ShapeDtypeStruct((R, 1024), BF16),
                   jax.ShapeDtypeStruct((R, 512), BF16)],
        compiler_params=_cparams(("parallel",)), name=name)(z, z, z, tabc, tabs, tabc, tabs, gkv, gq, wk, wv, wqn, wqp)


_MLA_SCALE = (MLA_NOPE + MLA_ROPE) ** -0.5


def _mla_x(q, kx, vx, kh, vh):
    sx, sh = bdot_nt(q, kx), bdot_nt(q, kh)
    m = lax.stop_gradient(jnp.maximum(jnp.max(sx, axis=-1, keepdims=True), jnp.max(sh, axis=-1, keepdims=True)))
    ex, eh = jnp.exp(sx - m), jnp.exp(sh - m)
    inv = 1.0 / (jnp.sum(ex, axis=-1, keepdims=True) + jnp.sum(eh, axis=-1, keepdims=True))
    return (bdot(ex, vx) + bdot(eh, vh)) * inv


def _mla_h(q, kh, vh):
    sh = bdot_nt(q, kh)
    eh = jnp.exp(sh - lax.stop_gradient(jnp.max(sh, axis=-1, keepdims=True)))
    return bdot(eh, vh) * (1.0 / jnp.sum(eh, axis=-1, keepdims=True))


def _mla_specs(lay):
    nxt, SEQ = lay.nxt, lay.SEQ
    return [pl.BlockSpec((T, 256), lambda b, h, qi: (lay.qrow(b, qi), h)),
            pl.BlockSpec((SEQ, 256), lambda b, h, qi: (b, h)), pl.BlockSpec((SEQ, 128), lambda b, h, qi: (b, h)),
            pl.BlockSpec((T, 256), lambda b, h, qi: (nxt + b, h)), pl.BlockSpec((T, 128), lambda b, h, qi: (nxt + b, h))]


def mla_fwd(q, k, v, lay, name):
    tps = lay.tps

    def body(q_ref, kx_ref, vx_ref, kh_ref, vh_ref, o_ref):
        qi = pl.program_id(2)
        f = lambda r: r[...]

        @pl.when(qi < tps)
        def _():
            o_ref[...] = _mla_x(f(q_ref), f(kx_ref), f(vx_ref), f(kh_ref), f(vh_ref)).astype(BF16)

        @pl.when(qi == tps)
        def _():
            o_ref[...] = _mla_h(f(q_ref), f(kh_ref), f(vh_ref)).astype(BF16)

    return pl.pallas_call(
        body, grid=(lay.B, MLA_HEADS, lay.nq), in_specs=_mla_specs(lay),
        out_specs=pl.BlockSpec((T, 128), lambda b, h, qi: (lay.qrow(b, qi), h)),
        out_shape=jax.ShapeDtypeStruct((lay.R, 512), BF16),
        compiler_params=_cparams(("parallel", "parallel", "arbitrary")), name=name)(q, k, v, k, v)


def mla_bwd(q, k, v, dmerged, lay, name):
    tps, SEQ, B = lay.tps, lay.SEQ, lay.B

    def body(q_ref, kx_ref, vx_ref, kh_ref, vh_ref, do_ref, dq_ref, dkx_ref, dkh_ref, dvx_ref, dvh_ref):
        qi = pl.program_id(2)
        f = lambda r: r[...].astype(F32)

        @pl.when(qi < tps)
        def _():
            _, vjp = jax.vjp(_mla_x, f(q_ref), f(kx_ref), f(vx_ref), f(kh_ref), f(vh_ref))
            dq, dkx, dvx, dkh, dvh = vjp(do_ref[...])
            dq_ref[...] = dq
            _acc(dkx_ref, dkx, qi == 0)
            _acc(dvx_ref, dvx, qi == 0)
            _acc(dkh_ref, dkh, qi == 0)
            _acc(dvh_ref, dvh, qi == 0)

        @pl.when(qi == tps)
        def _():
            _, vjp = jax.vjp(_mla_h, f(q_ref), f(kh_ref), f(vh_ref))
            dq, dkh, dvh = vjp(do_ref[...])
            dq_ref[...] = dq
            dkh_ref[...] += dkh
            dvh_ref[...] += dvh

    return pl.pallas_call(
        body, grid=(B, MLA_HEADS, lay.nq),
        in_specs=_mla_specs(lay) + [pl.BlockSpec((T, 128), lambda b, h, qi: (lay.qrow(b, qi), h))],
        out_specs=[pl.BlockSpec((T, 256), lambda b, h, qi: (lay.qrow(b, qi), h)),
                   pl.BlockSpec((SEQ, 256), lambda b, h, qi: (b, h)), pl.BlockSpec((T, 256), lambda b, h, qi: (b, h)),
                   pl.BlockSpec((SEQ, 128), lambda b, h, qi: (b, h)), pl.BlockSpec((T, 128), lambda b, h, qi: (b, h))],
        out_shape=[jax.ShapeDtypeStruct((lay.R, 1024), F32), jax.ShapeDtypeStruct((lay.NX, 1024), F32),
                   jax.ShapeDtypeStruct((B * T, 1024), F32), jax.ShapeDtypeStruct((lay.NX, 512), F32),
                   jax.ShapeDtypeStruct((B * T, 512), F32)],
        compiler_params=_cparams(("parallel", "parallel", "arbitrary")), name=name)(q, k, v, k, v, dmerged)


def _cmlp_piece(zu, zv, g, ws, bs):
    u, v = jax.nn.gelu(zu), jax.nn.gelu(zv)
    v = v * lax.rsqrt(jnp.mean(v * v, axis=-1, keepdims=True) + EPS) * g
    return u * (bdot(ws, v) + bs)


def _pieces():
    return [(c, g) for c in range(T // CMLP_CHUNK) for g in range(CMLP_GROUPS)]


def cmlp_merge_fwd(z, o, gvn, ws, bs, lay, name):
    def body(zu_ref, zv_ref, o_ref, g_ref, ws_ref, bs_ref, m_ref):
        m_ref[:, 0:512] = o_ref[...]
        for c, g in _pieces():
            rows, cols = slice(128 * c, 128 * (c + 1)), slice(128 * g, 128 * (g + 1))
            piece = _cmlp_piece(zu_ref[rows, cols], zv_ref[rows, cols], g_ref[:, cols], ws_ref[g], bs_ref[g])
            m_ref[rows, 512 + 128 * g:512 + 128 * (g + 1)] = piece.astype(BF16)

    return pl.pallas_call(
        body, grid=(lay.nt,),
        in_specs=[_row(512, 1), _row(512, 2), _row(512), _full((1, 512)), _full((4, 128, 128)), _full((4, 128, 1))],
        out_specs=_row(1024), out_shape=jax.ShapeDtypeStruct((lay.R, 1024), BF16),
        compiler_params=_cparams(("parallel",)), name=name)(z, z, o, gvn, ws, bs)


def ab_rows_bwd(z, tabc, tabs, dq, dkx, dkh, dvx, dvh, dmerged, gkv, gq, wk, wv, wqn, wqp, gvn, ws, bs, lay, name):
    def body(*refs):
        prep_in = refs[:3] + refs[5:9] + refs[11:17]
        zu_ref, zv_ref = refs[3:5]
        dq_ref, dcm_ref = refs[9:11]
        gvn_ref, ws_ref, bs_ref = refs[17:20]
        dkx_ref, dkh_ref, dvx_ref, dvh_ref = refs[20:24]
        dz_ref, dgkv_ref, dgq_ref, dwk_ref, dwv_ref, dwqn_ref, dwqp_ref, dgvn_ref, dws_ref, dbs_ref = refs[24:]
        first = pl.program_id(0) == 0
        _, vjp = jax.vjp(_ab_prep, *[r[...].astype(F32) for r in prep_in])
        d = vjp((dq_ref[...], _xh_pick(lay, dkx_ref, dkh_ref), _xh_pick(lay, dvx_ref, dvh_ref)))
        dz_ref[:, 0:256] = d[0].astype(BF16)
        dz_ref[:, 256:512] = d[1].astype(BF16)
        dz_ref[:, 1536:1664] = d[2].astype(BF16)
        for ref, val in zip((dgkv_ref, dgq_ref, dwk_ref, dwv_ref, dwqn_ref, dwqp_ref), d[7:]):
            _acc(ref, val, first)
        dws = [0.0] * CMLP_GROUPS
        dbs = [0.0] * CMLP_GROUPS
        dgv = [0.0] * CMLP_GROUPS
        for c, g in _pieces():
            rows, cols = slice(128 * c, 128 * (c + 1)), slice(128 * g, 128 * (g + 1))
            _, vjp = jax.vjp(_cmlp_piece, zu_ref[rows, cols], zv_ref[rows, cols], gvn_ref[:, cols], ws_ref[g], bs_ref[g])
            dzu, dzv, dg_, dws_, dbs_ = vjp(dcm_ref[rows, cols])
            dz_ref[rows, 512 + 128 * g:512 + 128 * (g + 1)] = dzu.astype(BF16)
            dz_ref[rows, 1024 + 128 * g:1024 + 128 * (g + 1)] = dzv.astype(BF16)
            dws[g], dbs[g], dgv[g] = dws[g] + dws_, dbs[g] + dbs_, dgv[g] + dg_
        _acc(dgvn_ref, jnp.concatenate(dgv, axis=1), first)
        _acc(dws_ref, jnp.stack(dws), first)
        _acc(dbs_ref, jnp.stack(dbs), first)

    acc_shapes = [(1, 256), (1, 256), (256, 512), (256, 512), (256, 512), (256, 256), (1, 512), (4, 128, 128), (4, 128, 1)]
    return pl.pallas_call(
        body, grid=(lay.nt,),
        in_specs=_ab_prep_specs(lay)[:3] + [_row(512, 1), _row(512, 2)] + _ab_prep_specs(lay)[3:7]
        + [_row(1024), _row(512, 1)] + _ab_prep_specs(lay)[7:]
        + [_full((1, 512)), _full((4, 128, 128)), _full((4, 128, 1))] + _xh_specs(lay, 1024) + _xh_specs(lay, 512),
        out_specs=[_row(AB_IN_P)] + [_full(s) for s in acc_shapes],
        out_shape=[jax.ShapeDtypeStruct((lay.R, AB_IN_P), BF16)] + [jax.ShapeDtypeStruct(s, F32) for s in acc_shapes],
        compiler_params=_cparams(("arbitrary",)), name=name)(
            z, z, z, z, z, tabc, tabs, tabc, tabs, dq, dmerged, gkv, gq, wk, wv, wqn, wqp, gvn, ws, bs, dkx, dkh, dvx, dvh)


def _cd_prep(zrk, zrq, zsk, zsq0, zsq1, zsv, c256, s256, c128, s128):
    rk = rope(zrk * (RET_QK ** -0.5), c256, s256)
    rq = rope(zrq, c256, s256)
    sk = rope(zsk, c128, s128)
    sq0, sq1 = rope(zsq0 * _SWA_SCALE, c256, s256), rope(zsq1 * _SWA_SCALE, c256, s256)
    e0, e1 = _expand_matrix(0), _expand_matrix(1)
    return rq, rk, sq0, sq1, bdot(sk, e0), bdot(sk, e1), bdot(zsv, e0), bdot(zsv, e1)


def _cd_prep_specs(lay):
    return [_row(256, 0), _row(256, 4), _row(128, 6), _row(256, 7), _row(256, 8), _row(128, 7),
            _ropespec(lay, 256), _ropespec(lay, 256), _ropespec(lay, 128), _ropespec(lay, 128)]


def cd_prep_fwd(z, tabc, tabs, lay, name):
    def body(*refs):
        ins, (rq_ref, rk_ref, sq_ref, ke_ref, ve_ref) = refs[:10], refs[10:]
        rq, rk, sq0, sq1, k0, k1, v0, v1 = _cd_prep(*[r[...] for r in ins])
        rq_ref[...] = rq.astype(BF16)
        rk_ref[...] = rk.astype(BF16)
        for ref, (a, b) in ((sq_ref, (sq0, sq1)), (ke_ref, (k0, k1)), (ve_ref, (v0, v1))):
            ref[:, 0:256] = a.astype(BF16)
            ref[:, 256:512] = b.astype(BF16)

    R = lay.R
    return pl.pallas_call(
        body, grid=(lay.nt,), in_specs=_cd_prep_specs(lay),
        out_specs=[_row(256), _row(256), _row(512), _row(512), _row(512)],
        out_shape=[jax.ShapeDtypeStruct((R, w), BF16) for w in (256, 256, 512, 512, 512)],
        compiler_params=_cparams(("parallel",)), name=name)(z, z, z, z, z, z, tabc, tabs, tabc, tabs)


def _ret_sample(h, qs, ks, vs, df, db):
    lgf, lgb = log_sigmoid(df), log_sigmoid(db)
    idx = lax.broadcasted_iota(jnp.int32, (T, 1), 0).astype(F32)
    diff = idx - lax.broadcasted_iota(jnp.int32, (1, T), 1).astype(F32)
    intra = (jnp.where(diff >= 0, jnp.exp(lgf * jnp.maximum(diff, 0.0)), 0.0)
             + jnp.where(diff <= 0, jnp.exp(lgb * jnp.maximum(-diff, 0.0)), 0.0))
    qdf, kdf, cdf = jnp.exp(lgf * (idx + 1.0)), jnp.exp(lgf * (T - 1.0 - idx)), jnp.exp(lgf * T)
    qdb, kdb, cdb = jnp.exp(lgb * (T - idx)), jnp.exp(lgb * idx), jnp.exp(lgb * T)
    mask = _head_mask(qs[0].shape, h)
    qs = [q * mask for q in qs]
    ys = [bdot(bdot_nt(q, k) * intra, v) for q, k, v in zip(qs, ks, vs)]
    n = len(qs)
    state = bdot_tn(ks[0] * kdf, vs[0])
    for i in range(1, n):
        ys[i] = ys[i] + bdot(qs[i] * qdf, state)
        if i + 1 < n:
            state = state * cdf + bdot_tn(ks[i] * kdf, vs[i])
    state = bdot_tn(ks[0] * kdb, vs[0])
    for i in range(n - 1, 0, -1):
        ys[i] = ys[i] + bdot(qs[i] * qdb, state)
        if i > 1:
            state = state * cdb + bdot_tn(ks[i] * kdb, vs[i])
    return ys


def _ret_specs(lay):
    nxt, SEQ = lay.nxt, lay.SEQ
    xs = lambda w, col: pl.BlockSpec((SEQ, w), lambda b, h: (b, col(h)))
    hs = lambda w, col: pl.BlockSpec((T, w), lambda b, h: (nxt + b, col(h)))
    zero, head = (lambda h: 0), (lambda h: 2 + h)
    dec = pl.BlockSpec((None, 8, 128), lambda b, h: (h, 0, 0))
    return [xs(256, zero), hs(256, zero), xs(256, zero), hs(256, zero), xs(128, head), hs(128, head), dec, dec]


def _ret_tiles(x_ref, h_ref, tps, cast=None):
    tiles = [h_ref[...]] + [x_ref[i * T:(i + 1) * T, :] for i in range(tps)]
    return [t.astype(cast) for t in tiles] if cast is not None else tiles


def ret_fwd(rq, rk, z, decf, decb, lay, name):
    tps, SEQ = lay.tps, lay.SEQ

    def body(qx_ref, qh_ref, kx_ref, kh_ref, vx_ref, vh_ref, df_ref, db_ref, yx_ref, yh_ref):
        ys = _ret_sample(pl.program_id(1), _ret_tiles(qx_ref, qh_ref, tps), _ret_tiles(kx_ref, kh_ref, tps),
                         _ret_tiles(vx_ref, vh_ref, tps), df_ref[0:1, 0:1], db_ref[0:1, 0:1])
        yh_ref[...] = ys[0]
        for i in range(tps):
            yx_ref[i * T:(i + 1) * T, :] = ys[i + 1]

    return pl.pallas_call(
        body, grid=(lay.B, RET_HEADS), in_specs=_ret_specs(lay),
        out_specs=[pl.BlockSpec((SEQ, 128), lambda b, h: (b, h)), pl.BlockSpec((T, 128), lambda b, h: (b, h))],
        out_shape=[jax.ShapeDtypeStruct((lay.NX, 512), F32), jax.ShapeDtypeStruct((lay.B * T, 512), F32)],
        compiler_params=_cparams(("parallel", "arbitrary")), name=name)(rq, rq, rk, rk, z, z, decf, decb)


def ret_bwd(rq, rk, z, decf, decb, dy, lay, name):
    tps, SEQ, B = lay.tps, lay.SEQ, lay.B
    nxt = lay.nxt

    def body(qx_ref, qh_ref, kx_ref, kh_ref, vx_ref, vh_ref, df_ref, db_ref, dyx_ref, dyh_ref,
             dqx_ref, dqh_ref, dkx_ref, dkh_ref, dvx_ref, dvh_ref, ddf_ref, ddb_ref):
        h = pl.program_id(1)
        _, vjp = jax.vjp(functools.partial(_ret_sample, h), _ret_tiles(qx_ref, qh_ref, tps, F32),
                         _ret_tiles(kx_ref, kh_ref, tps, F32), _ret_tiles(vx_ref, vh_ref, tps), df_ref[0:1, 0:1],
                         db_ref[0:1, 0:1])
        dqs, dks, dvs, ddf, ddb = vjp(_ret_tiles(dyx_ref, dyh_ref, tps))
        first = h == 0
        _acc(dqh_ref, dqs[0], first)
        _acc(dkh_ref, dks[0], first)
        dvh_ref[...] = dvs[0]
        for i in range(tps):
            rows = slice(i * T, (i + 1) * T)
            _acc(dqx_ref.at[rows], dqs[i + 1], first)
            _acc(dkx_ref.at[rows], dks[i + 1], first)
            dvx_ref[rows, :] = dvs[i + 1]
        @pl.when(jnp.logical_and(pl.program_id(0) == 0, first))
        def _():
            ddf_ref[...] = jnp.zeros(ddf_ref.shape, F32)
            ddb_ref[...] = jnp.zeros(ddb_ref.shape, F32)

        ddf_ref[h] += jnp.broadcast_to(ddf, (8, 128))
        ddb_ref[h] += jnp.broadcast_to(ddb, (8, 128))

    acc_x, acc_h = pl.BlockSpec((SEQ, 256), lambda b, h: (b, 0)), pl.BlockSpec((T, 256), lambda b, h: (b, 0))
    head_x, head_h = pl.BlockSpec((SEQ, 128), lambda b, h: (b, h)), pl.BlockSpec((T, 128), lambda b, h: (b, h))
    dec = pl.BlockSpec((RET_HEADS, 8, 128), lambda b, h: (0, 0, 0))
    return pl.pallas_call(
        body, grid=(B, RET_HEADS),
        in_specs=_ret_specs(lay) + [head_x, pl.BlockSpec((T, 128), lambda b, h: (nxt + b, h))],
        out_specs=[acc_x, acc_h, acc_x, acc_h, head_x, head_h, dec, dec],
        out_shape=[jax.ShapeDtypeStruct((lay.NX, 256), F32), jax.ShapeDtypeStruct((B * T, 256), F32),
                   jax.ShapeDtypeStruct((lay.NX, 256), F32), jax.ShapeDtypeStruct((B * T, 256), F32),
                   jax.ShapeDtypeStruct((lay.NX, 512), F32), jax.ShapeDtypeStruct((B * T, 512), F32),
                   jax.ShapeDtypeStruct((RET_HEADS, 8, 128), F32), jax.ShapeDtypeStruct((RET_HEADS, 8, 128), F32)],
        compiler_params=_cparams(("arbitrary", "arbitrary")), name=name)(rq, rq, rk, rk, z, z, decf, decb, dy, dy)


_SWA_SCALE = SWA_HEAD_DIM ** -0.5


def _swa_head(qh, sw, kh, vw, vh, sink):
    sh = bdot_nt(qh, kh)
    m = jnp.maximum(jnp.max(sh, axis=-1, keepdims=True), sink)
    if sw is not None:
        m = jnp.maximum(m, jnp.max(sw, axis=-1, keepdims=True))
    m = lax.stop_gradient(m)
    eh, es = jnp.exp(sh - m), jnp.exp(sink - m)
    tot = jnp.sum(eh, axis=-1, keepdims=True) + es
    if sw is None:
        return bdot(eh, vh) * (1.0 / tot)
    ew = jnp.exp(sw - m)
    return (bdot(ew, vw) + bdot(eh, vh)) * (1.0 / (tot + jnp.sum(ew, axis=-1, keepdims=True)))


def _swa_x(t0, kpos0, sq, kw, vw, kh, vh, *sinks):
    t = t0 + lax.broadcasted_iota(jnp.int32, (T, 1), 0)
    pos = kpos0 + lax.broadcasted_iota(jnp.int32, (1, SWA_SPAN), 1)
    band = jnp.abs(t - pos) <= SWA_WINDOW
    out = 0.0
    for i in range(SWA_GROUPS):
        mi = _head_mask(sq.shape, i)
        qh = sq * mi
        sw = jnp.where(band, bdot_nt(qh, kw), NEG_INF)
        out = out + _swa_head(qh, sw, kh, vw, vh, sinks[i]) * mi
    return out


def _swa_h(sq, kh, vh, *sinks):
    out = 0.0
    for i in range(SWA_GROUPS):
        mi = _head_mask(sq.shape, i)
        out = out + _swa_head(sq * mi, None, kh, None, vh, sinks[i]) * mi
    return out


def _swa_specs(lay):
    nxt, SEQ = lay.nxt, lay.SEQ
    return [pl.BlockSpec((T, 256), lambda g, b, qi: (lay.qrow(b, qi), g)),
            pl.BlockSpec((SEQ, 256), lambda g, b, qi: (b, g)), pl.BlockSpec((SEQ, 256), lambda g, b, qi: (b, g)),
            pl.BlockSpec((T, 256), lambda g, b, qi: (nxt + b, g)), pl.BlockSpec((T, 256), lambda g, b, qi: (nxt + b, g)),
            pl.BlockSpec((None, 4, 8, 128), lambda g, b, qi: (g, 0, 0, 0))]


def _swa_start(qi, SEQ):
    return pl.multiple_of(jnp.clip(qi * T - SWA_WINDOW, 0, SEQ - SWA_SPAN), SWA_WINDOW)


def swa_fwd(sq, kexp, vexp, sink, lay, name):
    tps, SEQ = lay.tps, lay.SEQ

    def body(sq_ref, kx_ref, vx_ref, kh_ref, vh_ref, sink_ref, o_ref):
        qi = pl.program_id(2)
        f = lambda r: r[...]
        sinks = [sink_ref[i][0:1, 0:1] for i in range(SWA_GROUPS)]

        @pl.when(qi < tps)
        def _():
            k0 = _swa_start(qi, SEQ)
            kw, vw = kx_ref[pl.ds(k0, SWA_SPAN), :], vx_ref[pl.ds(k0, SWA_SPAN), :]
            o_ref[...] = _swa_x(qi * T, k0, f(sq_ref), kw, vw, f(kh_ref), f(vh_ref), *sinks).astype(BF16)

        @pl.when(qi == tps)
        def _():
            o_ref[...] = _swa_h(f(sq_ref), f(kh_ref), f(vh_ref), *sinks).astype(BF16)

    return pl.pallas_call(
        body, grid=(SWA_KV_HEADS, lay.B, lay.nq), in_specs=_swa_specs(lay),
        out_specs=pl.BlockSpec((T, 256), lambda g, b, qi: (lay.qrow(b, qi), g)),
        out_shape=jax.ShapeDtypeStruct((lay.R, 512), BF16),
        compiler_params=_cparams(("parallel", "parallel", "arbitrary")), name=name)(sq, kexp, vexp, kexp, vexp, sink)


def swa_bwd(sq, kexp, vexp, sink, dmerged, lay, name):
    tps, SEQ, B = lay.tps, lay.SEQ, lay.B

    def body(sq_ref, kx_ref, vx_ref, kh_ref, vh_ref, sink_ref, do_ref, dsq_ref, dkx_ref, dkh_ref, dvx_ref, dvh_ref, dsink_ref):
        b, qi = pl.program_id(1), pl.program_id(2)
        f = lambda r: r[...].astype(F32)
        sinks = [sink_ref[i][0:1, 0:1] for i in range(SWA_GROUPS)]
        very_first = jnp.logical_and(b == 0, qi == 0)

        def acc_sink(ds):
            for i in range(SWA_GROUPS):
                _acc(dsink_ref.at[i], jnp.broadcast_to(ds[i], (8, 128)), very_first)

        @pl.when(qi == 0)
        def _():
            for ref in (dkx_ref, dkh_ref, dvx_ref, dvh_ref):
                ref[...] = jnp.zeros(ref.shape, F32)

        @pl.when(qi < tps)
        def _():
            k0 = _swa_start(qi, SEQ)
            win = pl.ds(k0, SWA_SPAN)
            kw, vw = kx_ref[win, :].astype(F32), vx_ref[win, :].astype(F32)
            _, vjp = jax.vjp(functools.partial(_swa_x, qi * T, k0), f(sq_ref), kw, vw, f(kh_ref), f(vh_ref), *sinks)
            d = vjp(do_ref[...])
            dsq_ref[...] = d[0]
            dkx_ref[win, :] += d[1]
            dvx_ref[win, :] += d[2]
            dkh_ref[...] += d[3]
            dvh_ref[...] += d[4]
            acc_sink(d[5:9])

        @pl.when(qi == tps)
        def _():
            _, vjp = jax.vjp(_swa_h, f(sq_ref), f(kh_ref), f(vh_ref), *sinks)
            d = vjp(do_ref[...])
            dsq_ref[...] = d[0]
            dkh_ref[...] += d[1]
            dvh_ref[...] += d[2]
            acc_sink(d[3:7])

    xs = pl.BlockSpec((SEQ, 256), lambda g, b, qi: (b, g))
    hs = pl.BlockSpec((T, 256), lambda g, b, qi: (b, g))
    return pl.pallas_call(
        body, grid=(SWA_KV_HEADS, B, lay.nq),
        in_specs=_swa_specs(lay) + [pl.BlockSpec((T, 256), lambda g, b, qi: (lay.qrow(b, qi), 2 + g))],
        out_specs=[pl.BlockSpec((T, 256), lambda g, b, qi: (lay.qrow(b, qi), g)), xs, hs, xs, hs,
                   pl.BlockSpec((None, 4, 8, 128), lambda g, b, qi: (g, 0, 0, 0))],
        out_shape=[jax.ShapeDtypeStruct((lay.R, 512), F32), jax.ShapeDtypeStruct((lay.NX, 512), F32),
                   jax.ShapeDtypeStruct((B * T, 512), F32), jax.ShapeDtypeStruct((lay.NX, 512), F32),
                   jax.ShapeDtypeStruct((B * T, 512), F32), jax.ShapeDtypeStruct((SWA_KV_HEADS, 4, 8, 128), F32)],
        compiler_params=_cparams(("arbitrary", "arbitrary", "arbitrary")), name=name)(
            sq, kexp, vexp, kexp, vexp, sink, dmerged)


def _cd_merge_piece(y, rg, g):
    return (y * lax.rsqrt(jnp.mean(y * y, axis=-1, keepdims=True) + EPS) * g) * (rg * jax.nn.sigmoid(rg))


def cd_merge_fwd(y, z, o, gn, lay, name):
    def body(yx_ref, yh_ref, rga_ref, rgb_ref, o_ref, g_ref, m_ref):
        y = _xh_pick(lay, yx_ref, yh_ref)
        for h in range(RET_HEADS):
            cols = slice(128 * h, 128 * (h + 1))
            rg_ref, rcols = (rga_ref, cols) if h < 2 else (rgb_ref, slice(128 * (h - 2), 128 * (h - 1)))
            m_ref[:, cols] = _cd_merge_piece(y[:, cols], rg_ref[:, rcols], g_ref[:, cols]).astype(BF16)
        m_ref[:, 512:1024] = o_ref[...]

    return pl.pallas_call(
        body, grid=(lay.nt,), in_specs=_xh_specs(lay, 512) + [_row(256, 5), _row(256, 6), _row(512), _full((1, 512))],
        out_specs=_row(1024), out_shape=jax.ShapeDtypeStruct((lay.R, 1024), BF16),
        compiler_params=_cparams(("parallel",)), name=name)(*y, z, z, o, gn)


def cd_merge_bwd(y, z, gn, dmerged, lay, name):
    def body(yx_ref, yh_ref, rga_ref, rgb_ref, g_ref, dm_ref, dy_ref, drg_ref, dg_ref):
        first = pl.program_id(0) == 0
        y = _xh_pick(lay, yx_ref, yh_ref)
        dgs = []
        for h in range(RET_HEADS):
            cols = slice(128 * h, 128 * (h + 1))
            rg_ref, rcols = (rga_ref, cols) if h < 2 else (rgb_ref, slice(128 * (h - 2), 128 * (h - 1)))
            _, vjp = jax.vjp(_cd_merge_piece, y[:, cols], rg_ref[:, rcols], g_ref[:, cols])
            dy, drg, dg = vjp(dm_ref[:, cols])
            dy_ref[:, cols] = dy
            drg_ref[:, cols] = drg
            dgs.append(dg)
        _acc(dg_ref, jnp.concatenate(dgs, axis=1), first)

    return pl.pallas_call(
        body, grid=(lay.nt,), in_specs=_xh_specs(lay, 512) + [_row(256, 5), _row(256, 6), _full((1, 512)), _row(512, 0)],
        out_specs=[_row(512), _row(512), _full((1, 512))],
        out_shape=[jax.ShapeDtypeStruct((lay.R, 512), F32), jax.ShapeDtypeStruct((lay.R, 512), F32),
                   jax.ShapeDtypeStruct((1, 512), F32)],
        compiler_params=_cparams(("arbitrary",)), name=name)(*y, z, z, gn, dmerged)


def cd_rows_bwd(z, tabc, tabs, dsq, drg, drq, drk, dke, dve, drv, lay, name):
    def body(*refs):
        ins = refs[:10]
        dsq_ref, drg_ref = refs[10:12]
        drq, drk, dke, dve, drv = (_xh_pick(lay, refs[12 + 2 * n], refs[13 + 2 * n]) for n in range(5))
        dz_ref = refs[22]
        _, vjp = jax.vjp(_cd_prep, *[r[...] for r in ins])
        cts = (drq, drk, dsq_ref[:, 0:256], dsq_ref[:, 256:512], dke[:, 0:256], dke[:, 256:512],
               dve[:, 0:256], dve[:, 256:512])
        dzrk, dzrq, dzsk, dzsq0, dzsq1, dzsv = vjp(cts)[:6]
        dz_ref[:, 0:256] = dzrk.astype(BF16)
        dz_ref[:, 256:768] = drv.astype(BF16)
        dz_ref[:, 768:896] = dzsk.astype(BF16)
        dz_ref[:, 896:1024] = dzsv.astype(BF16)
        dz_ref[:, 1024:1280] = dzrq.astype(BF16)
        dz_ref[:, 1280:1792] = drg_ref[...].astype(BF16)
        dz_ref[:, 1792:2048] = dzsq0.astype(BF16)
        dz_ref[:, 2048:2304] = dzsq1.astype(BF16)

    return pl.pallas_call(
        body, grid=(lay.nt,),
        in_specs=_cd_prep_specs(lay) + [_row(512), _row(512)] + _xh_specs(lay, 256) + _xh_specs(lay, 256)
        + _xh_specs(lay, 512) + _xh_specs(lay, 512) + _xh_specs(lay, 512),
        out_specs=_row(2304), out_shape=jax.ShapeDtypeStruct((lay.R, 2304), BF16),
        compiler_params=_cparams(("parallel",)), name=name)(
            z, z, z, z, z, z, tabc, tabs, tabc, tabs, dsq, drg, *drq, *drk, *dke, *dve, *drv)


def _pos():
    return lax.axis_index("x"), lax.axis_index("y"), lax.axis_index("c")


def _flip(v, bit):
    return 1 - v if bit else v


def _comm_call(name, body, ins, out_shapes, n_remote, n_local, aliases=None):
    return pl.pallas_call(
        body, in_specs=[ANY] * len(ins), out_specs=[ANY] * len(out_shapes), out_shape=out_shapes,
        scratch_shapes=[pltpu.SemaphoreType.DMA((n_remote,)), pltpu.SemaphoreType.DMA((n_remote,)),
                        pltpu.SemaphoreType.DMA((n_local,))],
        input_output_aliases=aliases or {}, name=name)(*ins)


def gather8(arr, name):
    def body(a_ref, o_ref, ssem, rsem, lsem):
        x, y, c = _pos()
        me = 4 * x + 2 * y + c
        loc = pltpu.make_async_copy(a_ref, o_ref.at[me], lsem.at[0])
        loc.start()
        cps = []
        for m in range(1, 8):
            peer = (_flip(x, m & 4), _flip(y, m & 2), _flip(c, m & 1))
            cps.append(pltpu.make_async_remote_copy(a_ref, o_ref.at[me], ssem.at[m - 1], rsem.at[m - 1],
                                                    device_id=peer, device_id_type=MESH))
            cps[-1].start()
        for cp in cps:
            cp.wait()
        loc.wait()

    return _comm_call(name, body, [arr], [jax.ShapeDtypeStruct((8,) + arr.shape, arr.dtype)], 7, 1)[0]


def gather_chips(arr, name):
    def body(a_ref, o_ref, ssem, rsem, lsem):
        x, y, c = _pos()
        k = 2 * x + y
        loc = pltpu.make_async_copy(a_ref, o_ref.at[k], lsem.at[0])
        loc.start()
        cps = []
        for m in range(1, 4):
            peer = (_flip(x, m & 2), _flip(y, m & 1), c)
            cps.append(pltpu.make_async_remote_copy(a_ref, o_ref.at[k], ssem.at[m - 1], rsem.at[m - 1],
                                                    device_id=peer, device_id_type=MESH))
            cps[-1].start()
        for cp in cps:
            cp.wait()
        loc.wait()

    return _comm_call(name, body, [arr], [jax.ShapeDtypeStruct((4,) + arr.shape, arr.dtype)], 3, 1)[0]


def gather_weights(arrs, name):
    n = len(arrs)

    def body(*refs):
        a_refs, o_refs, (isend, irecv, _) = refs[:n], refs[n:2 * n], refs[2 * n:]
        x, y, c = _pos()
        k = 2 * x + y
        sib = (x, y, 1 - c)
        chips = [(m, (_flip(x, m & 2), _flip(y, m & 1)), 2 * _flip(x, m & 2) + _flip(y, m & 1)) for m in range(1, 4)]
        waits = []
        for w, (a, o) in enumerate(zip(a_refs, o_refs)):
            H = a.shape[0] // 2
            own = pl.ds(c * H, H)
            first = [pltpu.make_async_remote_copy(a.at[own], o.at[k, own], isend.at[7 * w + m - 1], irecv.at[7 * w + m - 1],
                                                  device_id=(*chip, c), device_id_type=MESH) for m, chip, _ in chips]
            first.append(pltpu.make_async_remote_copy(a, o.at[k], isend.at[7 * w + 6], irecv.at[7 * w + 6], device_id=sib,
                                                      device_id_type=MESH))
            for cp in first:
                cp.start()
            waits.append((first, H, own, a, o, w))
        for first, H, own, a, o, w in waits:
            first[3].wait_recv()
            passed = []
            for m, chip, kk in chips:
                pltpu.make_async_remote_copy(a.at[own], o.at[kk, own], isend.at[7 * w + m - 1], irecv.at[7 * w + m - 1],
                                             device_id=(*chip, c), device_id_type=MESH).wait_recv()
                fw = pltpu.make_async_remote_copy(o.at[kk, own], o.at[kk, own], isend.at[7 * w + 2 + m], irecv.at[7 * w + 2 + m],
                                                  device_id=sib, device_id_type=MESH)
                fw.start()
                passed.append(fw)
            for fw in passed:
                fw.wait_recv()
            for cp in first + passed:
                cp.wait_send()

    outs = [jax.ShapeDtypeStruct((4,) + a.shape, a.dtype) for a in arrs]
    return _comm_call(name, body, list(arrs), outs, 7 * n, 1)


def swap_other_half(arrs, name):
    n = len(arrs)

    def body(*refs):
        a_refs, o_refs, (ssem, rsem, _) = refs[:n], refs[n:2 * n], refs[2 * n:]
        x, y, c = _pos()
        cps = []
        for w, (a, o) in enumerate(zip(a_refs, o_refs)):
            H = a.shape[1] // 2
            cps.append(pltpu.make_async_remote_copy(a.at[:, pl.ds((1 - c) * H, H)], o, ssem.at[w], rsem.at[w],
                                                    device_id=(x, y, 1 - c), device_id_type=MESH))
            cps[-1].start()
        for cp in cps:
            cp.wait()

    outs = [jax.ShapeDtypeStruct((4, a.shape[1] // 2) + a.shape[2:], a.dtype) for a in arrs]
    return _comm_call(name, body, list(arrs), outs, n, 1)


def exchange_chips(arrs, name):
    n = len(arrs)

    def body(*refs):
        a_refs, o_refs, (ssem, rsem, lsem) = refs[:n], refs[n:2 * n], refs[2 * n:]
        x, y, c = _pos()
        k = 2 * x + y
        cps = []
        for w, (a, o) in enumerate(zip(a_refs, o_refs)):
            cps.append(pltpu.make_async_copy(a.at[k], o.at[k], lsem.at[w]))
            cps[-1].start()
            for m in range(1, 4):
                px, py = _flip(x, m & 2), _flip(y, m & 1)
                cps.append(pltpu.make_async_remote_copy(a.at[2 * px + py], o.at[k], ssem.at[3 * w + m - 1], rsem.at[3 * w + m - 1],
                                                        device_id=(px, py, c), device_id_type=MESH))
                cps[-1].start()
        for cp in cps:
            cp.wait()

    return _comm_call(name, body, list(arrs), [jax.ShapeDtypeStruct(a.shape, a.dtype) for a in arrs], 3 * n, n)


def share_halves(arrs, name):
    n = len(arrs)

    def body(*refs):
        o_refs, (ssem, rsem, _) = refs[n:2 * n], refs[2 * n:]
        x, y, c = _pos()
        cps = []
        for w, o in enumerate(o_refs):
            H = o.shape[0] // 2
            mine = o.at[pl.ds(c * H, H)]
            cps.append(pltpu.make_async_remote_copy(mine, mine, ssem.at[w], rsem.at[w], device_id=(x, y, 1 - c),
                                                    device_id_type=MESH))
            cps[-1].start()
        for cp in cps:
            cp.wait()

    outs = [jax.ShapeDtypeStruct(a.shape, a.dtype) for a in arrs]
    return _comm_call(name, body, list(arrs), outs, n, 1, aliases={i: i for i in range(n)})


def _adamw(w, g, m, v):
    m = ADAM_B1 * m + (1.0 - ADAM_B1) * g
    v = ADAM_B2 * v + (1.0 - ADAM_B2) * (g * g)
    m_hat = m / (1.0 - ADAM_B1 ** ADAM_STEP)
    v_hat = v / (1.0 - ADAM_B2 ** ADAM_STEP)
    return -ADAM_LR * (m_hat / (jnp.sqrt(v_hat) + ADAM_EPS) + ADAM_WD * w), m, v


def _rows_tile(R, C):
    return _pick(R, max(8, (2 << 20) // (4 * C) // 8 * 8), 8)


def chip_partial(gs, buf, cidx, name):
    _, L, R, C = gs.shape
    H, tr = L // 2, _rows_tile(R, C)

    def body(c_ref, g_ref, b_ref, o_ref):
        o_ref[...] = (g_ref[...] + b_ref[...]).astype(BF16)

    return pl.pallas_call(
        body, grid_spec=pltpu.PrefetchScalarGridSpec(
            num_scalar_prefetch=1, grid=(4, H, R // tr),
            in_specs=[pl.BlockSpec((None, None, tr, C), lambda s, l, r, c: (s, c[0] * H + l, r, 0)),
                      pl.BlockSpec((None, None, tr, C), lambda s, l, r, c: (s, l, r, 0))],
            out_specs=pl.BlockSpec((None, None, tr, C), lambda s, l, r, c: (s, l, r, 0))),
        out_shape=jax.ShapeDtypeStruct((4, H, R, C), BF16),
        compiler_params=_cparams(("parallel", "parallel", "parallel")), name=name)(cidx, gs, buf)


def sum_chips(parts, cidx, name):
    _, H, R, C = parts.shape
    tr = _rows_tile(R, C)

    def body(c_ref, p_ref, g_out):
        g = p_ref[0].astype(F32)
        for s in range(1, 4):
            g = g + p_ref[s].astype(F32)
        g_out[...] = g

    return pl.pallas_call(
        body, grid_spec=pltpu.PrefetchScalarGridSpec(
            num_scalar_prefetch=1, grid=(H, R // tr),
            in_specs=[pl.BlockSpec((4, None, tr, C), lambda l, r, c: (0, l, r, 0))],
            out_specs=pl.BlockSpec((None, tr, C), lambda l, r, c: (c[0] * H + l, r, 0))),
        out_shape=jax.ShapeDtypeStruct((2 * H, R, C), F32),
        compiler_params=_cparams(("parallel", "parallel")), name=name)(cidx, parts)


def sum8(arr, name):
    n = arr.shape[1]
    tr = _pick(n, 512, 8)

    def body(a_ref, o_ref):
        s = a_ref[0]
        for j in range(1, 8):
            s = s + a_ref[j]
        o_ref[...] = s

    return pl.pallas_call(
        body, grid=(n // tr,), in_specs=[pl.BlockSpec((8, tr, 128), lambda i: (0, i, 0))],
        out_specs=pl.BlockSpec((tr, 128), lambda i: (i, 0)), out_shape=jax.ShapeDtypeStruct((n, 128), F32),
        compiler_params=_cparams(("parallel",)), name=name)(arr)


def adam_rows(w, g, m, v, name, emit_g=False):
    n, C = w.shape
    tr = _rows_tile(n, C)

    def body(w_ref, g_ref, m_ref, v_ref, *outs):
        d_out, m_out, v_out = outs[-3:]
        d_out[...], m_out[...], v_out[...] = _adamw(w_ref[...], g_ref[...], m_ref[...], v_ref[...])
        if emit_g:
            outs[0][...] = g_ref[...]

    spec = pl.BlockSpec((tr, C), lambda i: (i, 0))
    n_out = 4 if emit_g else 3
    return pl.pallas_call(
        body, grid=(n // tr,), in_specs=[spec] * 4, out_specs=[spec] * n_out,
        out_shape=[jax.ShapeDtypeStruct((n, C), F32)] * n_out, compiler_params=_cparams(("parallel",)), name=name)(w, g, m, v)


def _silu(c):
    return c * jax.nn.sigmoid(c)


def ada_fwd(c_all, w, b, name):
    NC, D = c_all.shape
    L, _, Wc = w.shape
    tn = _pick(Wc, 512, 128)

    def body(c_ref, w_ref, b_ref, o_ref):
        o_ref[...] = bdot(_silu(c_ref[...]), w_ref[...]) + b_ref[...]

    return pl.pallas_call(
        body, grid=(L, Wc // tn),
        in_specs=[pl.BlockSpec((NC, D), lambda l, j: (0, 0)), pl.BlockSpec((None, D, tn), lambda l, j: (l, 0, j)),
                  pl.BlockSpec((None, 1, tn), lambda l, j: (l, 0, j))],
        out_specs=pl.BlockSpec((None, NC, tn), lambda l, j: (l, 0, j)), out_shape=jax.ShapeDtypeStruct((L, NC, Wc), F32),
        compiler_params=_cparams(("parallel", "parallel")), name=name)(c_all, w, b)


def ada_bwd(c_all, w, dmod, name):
    NC, D = c_all.shape
    L, _, Wc = w.shape
    tn = _pick(Wc, 512, 128)

    def body(c_ref, w_ref, d_ref, dw_ref, db_ref, dc_ref):
        first = jnp.logical_and(pl.program_id(0) == 0, pl.program_id(1) == 0)
        f = lambda cs, ww: bdot(cs, ww)
        _, vjp = jax.vjp(f, _silu(c_ref[...]), w_ref[...])
        dcs, dw = vjp(d_ref[...])
        dw_ref[...] = dw
        db_ref[...] = jnp.sum(d_ref[...], axis=0, keepdims=True)
        _acc(dc_ref, dcs, first)

    return pl.pallas_call(
        body, grid=(L, Wc // tn),
        in_specs=[pl.BlockSpec((NC, D), lambda l, j: (0, 0)), pl.BlockSpec((None, D, tn), lambda l, j: (l, 0, j)),
                  pl.BlockSpec((None, NC, tn), lambda l, j: (l, 0, j))],
        out_specs=[pl.BlockSpec((None, D, tn), lambda l, j: (l, 0, j)), pl.BlockSpec((None, 1, tn), lambda l, j: (l, 0, j)),
                   pl.BlockSpec((NC, D), lambda l, j: (0, 0))],
        out_shape=[jax.ShapeDtypeStruct((L, D, Wc), F32), jax.ShapeDtypeStruct((L, 1, Wc), F32),
                   jax.ShapeDtypeStruct((NC, D), F32)],
        compiler_params=_cparams(("arbitrary", "arbitrary")), name=name)(c_all, w, dmod)


def cctx_grad(dcs_twice, c_ctx, name):
    def body(d_ref, c_ref, o_ref):
        _, vjp = jax.vjp(_silu, c_ref[...])
        o_ref[...] = vjp(0.5 * d_ref[...])[0]

    D = c_ctx.shape[1]
    return pl.pallas_call(body, out_shape=jax.ShapeDtypeStruct((8, D), F32), name=name)(dcs_twice, c_ctx)


def _rope_tables(SEQ):
    t = jnp.arange(SEQ)
    row, col = (t // GRID_W).astype(F32), (t % GRID_W).astype(F32)
    n_freq = 16
    freqs = ROPE_THETA ** (-jnp.arange(n_freq, dtype=F32) / n_freq)
    ang = jnp.concatenate([row[:, None] * freqs, col[:, None] * freqs], axis=-1)
    cos, sin = jnp.cos(ang), jnp.sin(ang)
    c = jnp.tile(jnp.concatenate([cos, cos], axis=1), (1, 4))
    s = jnp.tile(jnp.concatenate([-sin, sin], axis=1), (1, 4))
    return (jnp.concatenate([c, jnp.ones((T, 256), F32)], axis=0), jnp.concatenate([s, jnp.zeros((T, 256), F32)], axis=0))


def _unshard_cols(g):
    return jnp.transpose(g, (1, 2, 0, 3)).reshape(g.shape[1], g.shape[2], 4 * g.shape[3])


def _unshard_rows(g):
    return jnp.transpose(g, (1, 0, 2, 3)).reshape(g.shape[1], 4 * g.shape[2], g.shape[3])


def _shard_cols(w, n=4):
    L, R, C = w.shape
    return jnp.transpose(w.reshape(L, R, n, C // n), (2, 0, 1, 3))


def _shard_rows(w):
    L, R, C = w.shape
    return jnp.transpose(w.reshape(L, 4, R // 4, C), (1, 0, 2, 3))


def _ab_in_permute(w):
    L, D, _ = w.shape
    return jnp.concatenate([w[..., 0:256], w[..., 320:1600], w[..., 256:320], jnp.zeros((L, D, 64), w.dtype)], axis=-1)


def _ab_in_unpermute(g):
    return jnp.concatenate([g[..., 0:256], g[..., 1536:1600], g[..., 256:1536]], axis=-1)


def _split_heads(w, a):
    L, K, N = w.shape
    w4 = w.reshape(L, K, 4, N // 4)
    return w4[..., :a].reshape(L, K, 4 * a), w4[..., a:].reshape(L, K, N - 4 * a)


def _join_heads(p, q):
    L, K = p.shape[:2]
    return jnp.concatenate([p.reshape(L, K, 4, -1), q.reshape(L, K, 4, -1)], axis=-1).reshape(L, K, -1)


def _pack(arrs):
    parts = []
    for a in arrs:
        f = a.reshape(-1).astype(F32)
        parts.append(jnp.pad(f, (0, (-f.shape[0]) % 1024)))
    return jnp.concatenate(parts).reshape(-1, 128)


def _unpack(buf, like):
    out, r0 = [], 0
    for a in like:
        n = math.prod(a.shape)
        rows = (n + (-n) % 1024) // 128
        out.append(buf[r0:r0 + rows].reshape(-1)[:n].reshape(a.shape))
        r0 += rows
    return out


_SMALL = ("c_ctx", "ada_b", "norm_mix", "norm_ffn", "norm_final", "mla_q_norm", "mla_kv_norm", "cmlp_v_norm", "cmlp_ws",
          "cmlp_bs", "ret_decay_fwd", "ret_decay_bwd", "ret_norm", "swa_sink")
_BIG = ("ffn_in", "ffn_out", "ab_in", "ab_out", "mla_wq_b", "mla_wkv_b", "cd_in", "cd_out")
_WEIGHTS = ("c_ctx", "ada_w", "ada_b", "norm_mix", "norm_ffn", "norm_final", "ffn_in", "ffn_out", "ab_in", "ab_out",
            "mla_q_norm", "mla_kv_norm", "mla_wq_b", "mla_wkv_b", "cmlp_v_norm", "cmlp_ws", "cmlp_bs", "cd_in", "cd_out",
            "ret_decay_fwd", "ret_decay_bwd", "ret_norm", "swa_sink")


def kernel(x, c, ctx, c_ctx, ada_w, ada_b, norm_mix, norm_ffn, norm_final, ffn_in, ffn_out, ab_in, ab_out, mla_q_norm, mla_kv_norm, mla_wq_b, mla_wkv_b, cmlp_v_norm, cmlp_ws, cmlp_bs, cd_in, cd_out, ret_decay_fwd, ret_decay_bwd, ret_norm, swa_sink, loss_target, m_c_ctx, m_ada_w, m_ada_b, m_norm_mix, m_norm_ffn, m_norm_final, m_ffn_in, m_ffn_out, m_ab_in, m_ab_out, m_mla_q_norm, m_mla_kv_norm, m_mla_wq_b, m_mla_wkv_b, m_cmlp_v_norm, m_cmlp_ws, m_cmlp_bs, m_cd_in, m_cd_out, m_ret_decay_fwd, m_ret_decay_bwd, m_ret_norm, m_swa_sink, v_c_ctx, v_ada_w, v_ada_b, v_norm_mix, v_norm_ffn, v_norm_final, v_ffn_in, v_ffn_out, v_ab_in, v_ab_out, v_mla_q_norm, v_mla_kv_norm, v_mla_wq_b, v_mla_wkv_b, v_cmlp_v_norm, v_cmlp_ws, v_cmlp_bs, v_cd_in, v_cd_out, v_ret_decay_fwd, v_ret_decay_bwd, v_ret_norm, v_swa_sink):
    W = dict(c_ctx=c_ctx, ada_w=ada_w, ada_b=ada_b, norm_mix=norm_mix, norm_ffn=norm_ffn, norm_final=norm_final, ffn_in=ffn_in, ffn_out=ffn_out, ab_in=ab_in, ab_out=ab_out, mla_q_norm=mla_q_norm, mla_kv_norm=mla_kv_norm, mla_wq_b=mla_wq_b, mla_wkv_b=mla_wkv_b, cmlp_v_norm=cmlp_v_norm, cmlp_ws=cmlp_ws, cmlp_bs=cmlp_bs, cd_in=cd_in, cd_out=cd_out, ret_decay_fwd=ret_decay_fwd, ret_decay_bwd=ret_decay_bwd, ret_norm=ret_norm, swa_sink=swa_sink)
    M1 = dict(c_ctx=m_c_ctx, ada_w=m_ada_w, ada_b=m_ada_b, norm_mix=m_norm_mix, norm_ffn=m_norm_ffn, norm_final=m_norm_final, ffn_in=m_ffn_in, ffn_out=m_ffn_out, ab_in=m_ab_in, ab_out=m_ab_out, mla_q_norm=m_mla_q_norm, mla_kv_norm=m_mla_kv_norm, mla_wq_b=m_mla_wq_b, mla_wkv_b=m_mla_wkv_b, cmlp_v_norm=m_cmlp_v_norm, cmlp_ws=m_cmlp_ws, cmlp_bs=m_cmlp_bs, cd_in=m_cd_in, cd_out=m_cd_out, ret_decay_fwd=m_ret_decay_fwd, ret_decay_bwd=m_ret_decay_bwd, ret_norm=m_ret_norm, swa_sink=m_swa_sink)
    M2 = dict(c_ctx=v_c_ctx, ada_w=v_ada_w, ada_b=v_ada_b, norm_mix=v_norm_mix, norm_ffn=v_norm_ffn, norm_final=v_norm_final, ffn_in=v_ffn_in, ffn_out=v_ffn_out, ab_in=v_ab_in, ab_out=v_ab_out, mla_q_norm=v_mla_q_norm, mla_kv_norm=v_mla_kv_norm, mla_wq_b=v_mla_wq_b, mla_wkv_b=v_mla_wkv_b, cmlp_v_norm=v_cmlp_v_norm, cmlp_ws=v_cmlp_ws, cmlp_bs=v_cmlp_bs, cd_in=v_cd_in, cd_out=v_cd_out, ret_decay_fwd=v_ret_decay_fwd, ret_decay_bwd=v_ret_decay_bwd, ret_norm=v_ret_norm, swa_sink=v_swa_sink)

    B, SEQ, D = x.shape
    CTX = ctx.shape[1]
    lay = Layout(B, SEQ, CTX, D)
    nt, NX = lay.nt, lay.NX
    ix, iy, ic = lax.axis_index("x"), lax.axis_index("y"), lax.axis_index("c")
    chip, me = 2 * ix + iy, 4 * ix + 2 * iy + ic
    cidx = jnp.reshape(ic, (1,)).astype(jnp.int32)
    Wc = ada_w.shape[2]
    n_even, n_odd = ab_in.shape[0], cd_in.shape[0]

    rn_row = jnp.pad(ret_norm.reshape(1, -1), ((0, 0), (0, D - ret_norm.size)))
    pack0 = jnp.concatenate([c, rn_row, jnp.zeros((8 - (B + 1) % 8, D), F32)], axis=0) if (B + 1) % 8 else jnp.concatenate([c, rn_row], axis=0)
    g0 = gather8(pack0, "gather_cond")
    NC = -(-(8 * B + 1) // 16) * 16
    c_all = jnp.concatenate([g0[:, :B].reshape(8 * B, D), c_ctx[None], jnp.zeros((NC - 8 * B - 1, D), F32)], axis=0)
    rn_sh = ret_norm.shape[1]
    ret_norm_full = jnp.transpose(g0[0::2, B, :ret_norm.size].reshape(4, n_odd, rn_sh), (1, 0, 2)).reshape(n_odd, 4 * rn_sh)

    gw = gather_weights([W[n].astype(BF16) for n in _BIG], "gather_weights")
    w_ab_in, w_wq, w_wkv, w_cd_in = (_unshard_cols(gw[i]) for i in (2, 4, 5, 6))
    w_ffn_out, w_ab_out, w_cd_out = (_unshard_rows(gw[i]) for i in (1, 3, 7))
    w_ab_in = _ab_in_permute(w_ab_in)
    w_ffn_out_t = jnp.transpose(w_ffn_out, (0, 2, 1))
    w_qn, w_qp = _split_heads(w_wq, MLA_NOPE)
    w_k, w_v = _split_heads(w_wkv, MLA_NOPE)

    ab_sh = lax.dynamic_slice_in_dim(ada_b, chip * Wc, Wc, axis=1)[:, None, :]
    mod_sh = ada_fwd(c_all, ada_w, ab_sh, "ada_fwd")
    mod_all = _unshard_cols(gather_chips(mod_sh, "gather_mod"))
    mod_mine = jnp.concatenate([lax.dynamic_slice_in_dim(mod_all, me * B, B, axis=1), mod_all[:, 8 * B:8 * B + 1]], axis=1)
    mod = mod_mine.reshape(DEPTH, B + 1, 6, D)

    tabc, tabs = _rope_tables(SEQ)
    bc8 = lambda a: jnp.broadcast_to(a.reshape(a.shape + (1, 1)), a.shape + (8, 128))
    row = lambda a: a.reshape(1, -1)

    S = jnp.concatenate([x.reshape(NX, D), ctx.reshape(B * CTX, D)], axis=0)
    saved = []
    xn = norm_mod_fwd(S, row(norm_mix[0]), mod[0], 0, lay, nt, "norm_mix_fwd0")
    for l in range(DEPTH):
        j, even = l // 2, l % 2 == 0
        if even:
            z = mm(xn, w_ab_in[j], name=f"ab_in{l}")
            q, k, v = ab_prep_fwd(z, tabc, tabs, row(mla_kv_norm[j]), row(mla_q_norm[j]), w_k[j], w_v[j], w_qn[j], w_qp[j],
                                  lay, f"ab_prep{l}")
            o = mla_fwd(q, k, v, lay, f"mla{l}")
            merged = cmlp_merge_fwd(z, o, row(cmlp_v_norm[j]), cmlp_ws[j], cmlp_bs[j][:, :, None], lay, f"cmlp{l}")
            w_out, mix = w_ab_out[j], (q, k, v)
        else:
            z = mm(xn, w_cd_in[j], name=f"cd_in{l}")
            rq, rk, sq, ke, ve = cd_prep_fwd(z, tabc, tabs, lay, f"cd_prep{l}")
            decf, decb, sink = bc8(ret_decay_fwd[j]), bc8(ret_decay_bwd[j]), bc8(swa_sink[j].reshape(SWA_KV_HEADS, SWA_GROUPS))
            yret = ret_fwd(rq, rk, z, decf, decb, lay, f"ret{l}")
            osw = swa_fwd(sq, ke, ve, sink, lay, f"swa{l}")
            merged = cd_merge_fwd(yret, z, osw, row(ret_norm_full[j]), lay, f"cd_merge{l}")
            w_out, mix = w_cd_out[j], (rq, rk, sq, ke, ve, decf, decb, sink, yret)
        y, S_mid, xn2 = mm_gated(merged, w_out, S, mod[l], 2, lay, name=f"mix_out{l}", n_tiles=nt,
                                 norm=(row(norm_ffn[l]), mod[l], 3))
        act, fa, fb = ffn_in_act(xn2, gw[0], l, f"ffn_in{l}")
        nxt_norm = (row(norm_mix[l + 1]), mod[l + 1], 0) if l + 1 < DEPTH else None
        f, S_new, *xn_next = mm_gated(act, w_ffn_out[l], S_mid, mod[l], 5, lay, name=f"ffn_out{l}", n_tiles=nt, norm=nxt_norm)
        saved.append((S, xn, z, mix, merged, w_out, y, S_mid, xn2, (fa, fb), act, f))
        S, xn = S_new, (xn_next[0] if xn_next else None)

    loss_blk, dS, d_norm_final = loss_head(S, row(norm_final), loss_target.reshape(NX, D), lay, "loss_head")
    loss = lax.psum(loss_blk[0, 0], ("x", "y", "c"))

    G = {n: [None] * W[n].shape[0] for n in ("norm_mix", "norm_ffn", "mla_q_norm", "mla_kv_norm", "cmlp_v_norm", "cmlp_ws",
                                             "cmlp_bs", "ret_decay_fwd", "ret_decay_bwd", "ret_norm", "swa_sink")}
    GB = {n: [None] * cnt for n, cnt in (("ffn_in_a", DEPTH), ("ffn_in_b", DEPTH), ("ffn_out", DEPTH), ("ab_in", n_even), ("ab_out", n_even),
                                         ("wqn", n_even), ("wqp", n_even), ("wk", n_even), ("wv", n_even), ("cd_in", n_odd),
                                         ("cd_out", n_odd))}
    dmod = [None] * DEPTH
    df, dgate2 = gate_bwd(dS, saved[-1][-1], mod[DEPTH - 1], 5, lay, nt, f"ffn_gate_bwd{DEPTH - 1}")
    for l in reversed(range(DEPTH)):
        j, even = l // 2, l % 2 == 0
        S_in, xn, z, mix, merged, w_out, y, S_mid, xn2, (fa, fb), act, f = saved[l]
        da, db = ffn_out_dx_act(df, w_ffn_out_t[l], fa, fb, f"ffn_out_dx{l}")
        GB["ffn_out"][l] = mm(act, df, ta=True, name=f"ffn_out_dw{l}")
        GB["ffn_in_a"][l] = mm(xn2, da, ta=True, name=f"ffn_in_dwa{l}", split=2)
        GB["ffn_in_b"][l] = mm(xn2, db, ta=True, name=f"ffn_in_dwb{l}", split=2)
        dxn2 = ffn_in_dx(da, db, gw[0], l, f"ffn_in_dx{l}")
        dS_mid, dss2, dg, dy, dgate1 = norm_mod_bwd(S_mid, row(norm_ffn[l]), mod[l], 3, dxn2, dS, lay, nt, f"norm_ffn_bwd{l}",
                                                    gate=(y, mod[l], 2))
        G["norm_ffn"][l] = dg
        dmerged = mm(dy, w_out, tb=True, name=f"mix_out_dx{l}")
        d_w_out = mm(merged, dy, ta=True, name=f"mix_out_dw{l}")
        if even:
            q, k, v = mix
            dq, dkx, dkh, dvx, dvh = mla_bwd(q, k, v, dmerged, lay, f"mla_bwd{l}")
            (dz, dgkv, dgq, dwk, dwv, dwqn, dwqp, dgvn, dws, dbs) = ab_rows_bwd(
                z, tabc, tabs, dq, dkx, dkh, dvx, dvh, dmerged, row(mla_kv_norm[j]), row(mla_q_norm[j]), w_k[j], w_v[j],
                w_qn[j], w_qp[j], row(cmlp_v_norm[j]), cmlp_ws[j], cmlp_bs[j][:, :, None], lay, f"ab_rows_bwd{l}")
            G["mla_kv_norm"][j], G["mla_q_norm"][j], G["cmlp_v_norm"][j] = dgkv, dgq, dgvn
            G["cmlp_ws"][j], G["cmlp_bs"][j] = dws, dbs
            GB["wk"][j], GB["wv"][j], GB["wqn"][j], GB["wqp"][j], GB["ab_out"][j] = dwk, dwv, dwqn, dwqp, d_w_out
            w_in = w_ab_in[j]
        else:
            rq, rk, sq, ke, ve, decf, decb, sink, yret = mix
            dyret, drg, dgn = cd_merge_bwd(yret, z, row(ret_norm_full[j]), dmerged, lay, f"cd_merge_bwd{l}")
            dqx, dqh, dkx, dkh, dvx, dvh, ddf, ddb = ret_bwd(rq, rk, z, decf, decb, dyret, lay, f"ret_bwd{l}")
            dsq, dkex, dkeh, dvex, dveh, dsink = swa_bwd(sq, ke, ve, sink, dmerged, lay, f"swa_bwd{l}")
            dz = cd_rows_bwd(z, tabc, tabs, dsq, drg, (dqx, dqh), (dkx, dkh), (dkex, dkeh), (dvex, dveh), (dvx, dvh), lay,
                             f"cd_rows_bwd{l}")
            G["ret_norm"][j], G["ret_decay_fwd"][j], G["ret_decay_bwd"][j] = dgn, ddf[:, 0, 0], ddb[:, 0, 0]
            G["swa_sink"][j] = dsink[:, :, 0, 0].reshape(-1)
            GB["cd_out"][j] = d_w_out
            w_in = w_cd_in[j]
        GB["ab_in" if even else "cd_in"][j] = mm(xn, dz, ta=True, name=f"mix_in_dw{l}")
        dxn = mm(dz, w_in, tb=True, name=f"mix_in_dx{l}")
        dmod_l = lambda dss1: jnp.concatenate([dss1, dgate1, dss2, dgate2], axis=1)
        if l > 0:
            dS, dss1, dg, df, dgate2_prev = norm_mod_bwd(S_in, row(norm_mix[l]), mod[l], 0, dxn, dS_mid, lay, nt,
                                                         f"norm_mix_bwd{l}", gate=(saved[l - 1][-1], mod[l - 1], 5))
            dmod[l], dgate2 = dmod_l(dss1), dgate2_prev
        else:
            dS, dss1, dg = norm_mod_bwd(S_in, row(norm_mix[l]), mod[l], 0, dxn, dS_mid, lay, nt, f"norm_mix_bwd{l}")
            dmod[l] = dmod_l(dss1)
        G["norm_mix"][l] = dg
    grad_x = dS[:NX].reshape(B, SEQ, D)

    st = lambda n: jnp.stack([g.reshape((4 * rn_sh,) if n == "ret_norm" else W[n].shape[1:]) for g in G[n]])
    small_parts = {n: st(n) for n in G}
    small_parts["norm_final"] = d_norm_final.reshape(-1)
    dmod_local = jnp.stack(dmod).reshape(DEPTH, B + 1, 6 * D)
    names1 = ["norm_mix", "norm_ffn", "norm_final", "mla_q_norm", "mla_kv_norm", "cmlp_v_norm", "cmlp_ws", "cmlp_bs",
              "ret_decay_fwd", "ret_decay_bwd", "ret_norm", "swa_sink"]
    like1 = [dmod_local] + [small_parts[n] for n in names1]
    g1 = gather8(_pack(like1), "gather_small_grads")
    tot1 = _unpack(sum8(g1, "sum_small_grads"), like1)
    sg = dict(zip(names1, tot1[1:]))
    n_dm = math.prod(dmod_local.shape)
    dm_each = g1[:, :-(-n_dm // 128)].reshape(8, -1)[:, :n_dm].reshape(8, DEPTH, B + 1, 6 * D)
    dmod_all = jnp.concatenate([jnp.transpose(dm_each[:, :, :B], (1, 0, 2, 3)).reshape(DEPTH, 8 * B, 6 * D),
                                tot1[0][:, B:B + 1], jnp.zeros((DEPTH, NC - 8 * B - 1, 6 * D), F32)], axis=1)
    dmod_sh = lax.dynamic_slice_in_dim(dmod_all, chip * Wc, Wc, axis=2)
    g_ada_w, g_ada_b_sh, dcs = ada_bwd(c_all, ada_w, dmod_sh, "ada_bwd")
    like2 = [dcs[8 * B], g_ada_b_sh]
    g2 = gather8(_pack(like2), "gather_ada_grads")
    tot2 = _unpack(sum8(g2, "sum_ada_grads"), like2)
    bc = lambda a: jnp.broadcast_to(a.reshape(1, D), (8, D))
    sg["c_ctx"] = cctx_grad(bc(tot2[0]), bc(c_ctx), "c_ctx_grad")[0]
    off = D + (-D) % 1024
    gab = g2[0::2, off // 128:(off + DEPTH * Wc) // 128].reshape(4, DEPTH, Wc)
    sg["ada_b"] = jnp.transpose(gab, (1, 0, 2)).reshape(DEPTH, 4 * Wc)
    sg["ret_norm"] = lax.dynamic_slice_in_dim(sg["ret_norm"], chip * rn_sh, rn_sh, axis=1)

    stack = lambda n: jnp.stack(GB[n])
    gs = {"ffn_in": jnp.concatenate([jnp.stack(GB[n], axis=1) for n in ("ffn_in_a", "ffn_in_b")], axis=0),
          "ffn_out": _shard_rows(stack("ffn_out")),
          "ab_in": _shard_cols(_ab_in_unpermute(stack("ab_in"))), "ab_out": _shard_rows(stack("ab_out")),
          "mla_wq_b": _shard_cols(_join_heads(stack("wqn"), stack("wqp"))),
          "mla_wkv_b": _shard_cols(_join_heads(stack("wk"), stack("wv"))),
          "cd_in": _shard_cols(stack("cd_in")), "cd_out": _shard_rows(stack("cd_out"))}
    bufs = swap_other_half([gs[n] for n in _BIG], "swap_core_halves")
    parts = [chip_partial(gs[n], b, cidx, f"chip_partial_{n}") for n, b in zip(_BIG, bufs)]
    arrived = exchange_chips(parts, "exchange_chips")
    grads = share_halves([sum_chips(p, cidx, f"sum_chips_{n}") for n, p in zip(_BIG, arrived)], "share_core_halves")
    flat2 = lambda a: a.reshape(-1, a.shape[-1])
    out = {}
    for n, g in zip(_BIG, grads):
        res = adam_rows(flat2(W[n]), flat2(g), flat2(M1[n]), flat2(M2[n]), f"adam_{n}", emit_g=True)
        out[n] = tuple(r.reshape(W[n].shape) for r in res)

    like_s = [W[n] for n in _SMALL]
    dsm, msm, vsm = adam_rows(_pack(like_s), _pack([sg[n].reshape(W[n].shape) for n in _SMALL]), _pack([M1[n] for n in _SMALL]),
                                _pack([M2[n] for n in _SMALL]), "adam_small")
    for n, d_, m_, v_ in zip(_SMALL, _unpack(dsm, like_s), _unpack(msm, like_s), _unpack(vsm, like_s)):
        out[n] = (sg[n].reshape(W[n].shape), d_, m_, v_)
    d_, m_, v_ = adam_rows(flat2(ada_w), flat2(g_ada_w), flat2(m_ada_w), flat2(v_ada_w), "adam_ada_w")
    out["ada_w"] = (g_ada_w, d_.reshape(ada_w.shape), m_.reshape(ada_w.shape), v_.reshape(ada_w.shape))

    return (loss, grad_x, *[out[n][0] for n in _WEIGHTS], *[out[n][1] for n in _WEIGHTS], *[out[n][2] for n in _WEIGHTS],
            *[out[n][3] for n in _WEIGHTS])
```

```python
import functools
import math

import jax
import jax.numpy as jnp
import numpy as np
from jax import lax
from jax.experimental import pallas as pl
from jax.experimental.pallas import tpu as pltpu

F32 = jnp.float32
BF16 = jnp.bfloat16
EPS = 1e-6
NEG_INF = -1e30
GRID_W = 64
ROPE_THETA = 10000.0
DEPTH = 4
MLA_HEADS, MLA_Q_LORA, MLA_KV_LORA, MLA_NOPE, MLA_ROPE, MLA_V = 4, 256, 256, 128, 64, 128
CMLP_GROUPS, CMLP_CHUNK = 4, 128
CMLP_WIDTH = 512
RET_HEADS, RET_QK, RET_V = 4, 64, 128
SWA_Q_HEADS, SWA_KV_HEADS, SWA_HEAD_DIM, SWA_WINDOW = 8, 2, 64, 128
SWA_GROUPS = SWA_Q_HEADS // SWA_KV_HEADS
AB_IN_P = 1664
ADAM_LR, ADAM_B1, ADAM_B2, ADAM_EPS, ADAM_WD, ADAM_STEP = 0.001, 0.9, 0.999, 1e-08, 0.01, 10

T = 256
SWA_SPAN = T + 2 * SWA_WINDOW
VMEM_LIMIT = 48 * 1024 * 1024
MESH = pl.DeviceIdType.MESH
ANY = pl.BlockSpec(memory_space=pl.ANY)


def _cparams(sem):
    return pltpu.CompilerParams(dimension_semantics=sem, vmem_limit_bytes=VMEM_LIMIT)


@functools.cache
def _bdot_fn(ca, cb):
    fa, fb = 1 - ca, 1 - cb

    def dg(p, q, cp, cq):
        return lax.dot_general(p.astype(BF16), q.astype(BF16), (((cp,), (cq,)), ((), ())), preferred_element_type=F32)

    @jax.custom_vjp
    def bd(a, b):
        return dg(a, b, ca, cb)

    def fwd(a, b):
        return dg(a, b, ca, cb), (a, b)

    def bwd(res, g):
        a, b = res
        da = dg(g, b, 1, fb) if ca == 1 else dg(b, g, fb, 1)
        db = dg(a, g, fa, 0) if cb == 0 else dg(g, a, 0, fa)
        return da, db

    bd.defvjp(fwd, bwd)
    return bd


def bdot(a, b):
    return _bdot_fn(1, 0)(a, b)


def bdot_nt(a, b):
    return _bdot_fn(1, 1)(a, b)


def bdot_tn(a, b):
    return _bdot_fn(0, 0)(a, b)


def _swap32(x):
    w = x.shape[-1]
    lane = lax.broadcasted_iota(jnp.int32, x.shape, 1)
    return jnp.where((lane & 32) == 0, pltpu.roll(x, w - 32, 1), pltpu.roll(x, 32, 1))


@jax.custom_vjp
def rope(x, c, s):
    return x * c + _swap32(x) * s


def _rope_fwd(x, c, s):
    return rope(x, c, s), (c, s)


def _rope_bwd(res, g):
    c, s = res
    return g * c + _swap32(g * s), jnp.zeros_like(c), jnp.zeros_like(s)


rope.defvjp(_rope_fwd, _rope_bwd)


def rms(x, g):
    return x * lax.rsqrt(jnp.mean(x * x, axis=-1, keepdims=True) + EPS) * g


def normmod(x, g, sh, sc):
    return rms(x, g) * (1.0 + sc) + sh


def log_sigmoid(x):
    return jnp.minimum(x, 0.0) - jnp.log(1.0 + jnp.exp(-jnp.abs(x)))


def _head_mask(shape, h):
    lane = lax.broadcasted_iota(jnp.int32, shape, 1)
    return ((lane >> 6) == h).astype(F32)


def _fold_matrix():
    i = lax.broadcasted_iota(jnp.int32, (256, 128), 0)
    j = lax.broadcasted_iota(jnp.int32, (256, 128), 1)
    return ((i & 63) == j).astype(F32)


def _expand_matrix(g):
    i = lax.broadcasted_iota(jnp.int32, (128, 256), 0)
    j = lax.broadcasted_iota(jnp.int32, (128, 256), 1)
    return (i == (j & 63) + 64 * g).astype(F32)


def _acc(ref, val, first):
    @pl.when(first)
    def _():
        ref[...] = val

    @pl.when(jnp.logical_not(first))
    def _():
        ref[...] += val


def _pick(n, cap, mult):
    best = None
    for d in range(mult, min(n, cap) + 1, mult):
        if n % d == 0:
            best = d
    return best if best is not None else n


def mm(a, b, *, ta=False, tb=False, name, split=1):
    M, K = (a.shape[1], a.shape[0]) if ta else a.shape
    N = b.shape[0] if tb else b.shape[1]
    Ns = N // split
    assert split == 1 or (ta and Ns % 128 == 0)
    if ta:
        tn = N
        tm = _pick(M, min(1536, (3 << 20) // tn), 128)
    else:
        tn = _pick(N, 768, 128)
        if tn < 256 and N <= 2304:
            tn = N
        tm = _pick(M, min(1536, (1 << 20) // tn), 128)
    tk = _pick(K, 2048 if not ta else 1024, 128) if K > 2816 or ta else K
    nk = K // tk
    grid = (M // tm, N // tn, nk)
    a_spec = pl.BlockSpec((tk, tm), lambda i, j, k: (k, i)) if ta else pl.BlockSpec((tm, tk), lambda i, j, k: (i, k))
    b_spec = pl.BlockSpec((tn, tk), lambda i, j, k: (j, k)) if tb else pl.BlockSpec((tk, tn), lambda i, j, k: (k, j))
    dims = (((0 if ta else 1,), (1 if tb else 0,)), ((), ()))

    def body(a_ref, b_ref, o_ref):
        part = lax.dot_general(a_ref[...], b_ref[...], dims, preferred_element_type=F32)
        first = pl.program_id(2) == 0
        if split > 1:
            for s in range(split):
                _acc(o_ref.at[s], part[:, s * Ns:(s + 1) * Ns], first)
        elif nk == 1:
            o_ref[...] = part
        else:
            _acc(o_ref, part, first)

    if split > 1:
        out_spec, out_shape = pl.BlockSpec((split, tm, Ns), lambda i, j, k: (0, i, 0)), (split, M, Ns)
    else:
        out_spec, out_shape = pl.BlockSpec((tm, tn), lambda i, j, k: (i, j)), (M, N)
    return pl.pallas_call(
        body, grid=grid, in_specs=[a_spec, b_spec], out_specs=out_spec, out_shape=jax.ShapeDtypeStruct(out_shape, F32),
        compiler_params=_cparams(("parallel", "parallel", "arbitrary")), name=name)(a, b)


def mm_gated(a, b, res, mod, gate_row, lay, *, name, n_tiles, norm=None):
    K, N = b.shape
    M = n_tiles * T

    def body(*refs):
        a_ref, b_ref, r_ref, g_ref = refs[:4]
        y = lax.dot_general(a_ref[...], b_ref[...], (((1,), (0,)), ((), ())), preferred_element_type=F32)
        new = r_ref[...] + g_ref[gate_row:gate_row + 1, :] * y
        if norm is None:
            y_ref, o_ref = refs[4:]
        else:
            gn_ref, mn_ref, y_ref, o_ref, xn_ref = refs[4:]
            k0 = norm[2]
            xn_ref[...] = normmod(new, gn_ref[...], mn_ref[k0:k0 + 1, :], mn_ref[k0 + 1:k0 + 2, :]).astype(BF16)
        y_ref[...] = y.astype(BF16)
        o_ref[...] = new

    rows = pl.BlockSpec((T, N), lambda i: (i, 0))
    extra = [] if norm is None else [_full((1, N)), _modspec(lay, 6, N)]
    return pl.pallas_call(
        body, grid=(n_tiles,),
        in_specs=[pl.BlockSpec((T, K), lambda i: (i, 0)), _full((K, N)), rows, _modspec(lay, 6, N)] + extra,
        out_specs=[rows, rows] + ([] if norm is None else [rows]),
        out_shape=[jax.ShapeDtypeStruct((M, N), BF16), jax.ShapeDtypeStruct((M, N), F32)]
        + ([] if norm is None else [jax.ShapeDtypeStruct((M, N), BF16)]),
        compiler_params=_cparams(("parallel",)), name=name)(a, b, res, mod, *([] if norm is None else norm[:2]))


class Layout:
    def __init__(self, B, SEQ, CTX, D):
        assert CTX == T and SEQ % T == 0 and SEQ >= SWA_SPAN
        self.B, self.SEQ, self.CTX, self.D = B, SEQ, CTX, D
        self.tps = SEQ // T
        self.nxt = B * self.tps
        self.nt = self.nxt + B
        self.NX, self.R = B * SEQ, B * SEQ + B * CTX
        self.nq = self.tps + 1

    def mod_idx(self, i):
        return jnp.where(i < self.nxt, i // self.tps, self.B)

    def rope_idx(self, i):
        return jnp.where(i < self.nxt, i % self.tps, self.tps)

    def first_of_mod(self, i):
        return jnp.logical_or(jnp.logical_and(i < self.nxt, i % self.tps == 0), i == self.nxt)

    def qrow(self, b, qi):
        return jnp.where(qi < self.tps, b * self.tps + qi, self.nxt + b)


def _row(w, col=0):
    return pl.BlockSpec((T, w), lambda i: (i, col))


def _full(shape):
    nd = len(shape)
    return pl.BlockSpec(shape, lambda i: (0,) * nd)


def _modspec(lay, rows, D):
    return pl.BlockSpec((None, rows, D), lambda i: (lay.mod_idx(i), 0, 0))


def _ropespec(lay, w):
    return pl.BlockSpec((T, w), lambda i: (lay.rope_idx(i), 0))


def _xh_specs(lay, w):
    nxt = lay.nxt
    return [pl.BlockSpec((T, w), lambda i: (jnp.minimum(i, nxt - 1), 0)), pl.BlockSpec((T, w), lambda i: (jnp.maximum(i - nxt, 0), 0))]


def _xh_pick(lay, x_ref, h_ref):
    return jnp.where(pl.program_id(0) < lay.nxt, x_ref[...], h_ref[...])


def norm_mod_fwd(S, g, mod, k0, lay, n_tiles, name):
    D = S.shape[1]

    def body(s_ref, g_ref, mod_ref, o_ref):
        o_ref[...] = normmod(s_ref[...], g_ref[...], mod_ref[k0:k0 + 1, :], mod_ref[k0 + 1:k0 + 2, :]).astype(BF16)

    return pl.pallas_call(
        body, grid=(n_tiles,), in_specs=[_row(D), _full((1, D)), _modspec(lay, 6, D)], out_specs=_row(D),
        out_shape=jax.ShapeDtypeStruct((n_tiles * T, D), BF16), compiler_params=_cparams(("parallel",)), name=name)(S, g, mod)


def norm_mod_bwd(S, g, mod, k0, dxn, ds_in, lay, n_tiles, name, gate=None):
    D = S.shape[1]

    def body(*refs):
        s_ref, g_ref, mod_ref, dxn_ref, dsin_ref = refs[:5]
        i = pl.program_id(0)
        _, vjp = jax.vjp(normmod, s_ref[...], g_ref[...], mod_ref[k0:k0 + 1, :], mod_ref[k0 + 1:k0 + 2, :])
        dx, dg, dsh, dsc = vjp(dxn_ref[...])
        ds = dsin_ref[...] + dx
        if gate is None:
            ds_ref, dss_ref, dg_ref = refs[5:]
        else:
            y_ref, gmod_ref, ds_ref, dss_ref, dg_ref, dy_ref, dgate_ref = refs[5:]
            row = gate[2]
            dy_ref[...] = (gmod_ref[row:row + 1, :] * ds).astype(BF16)
            _acc(dgate_ref, jnp.sum(ds * y_ref[...], axis=0, keepdims=True), lay.first_of_mod(i))
        ds_ref[...] = ds
        _acc(dg_ref, dg, i == 0)
        _acc(dss_ref, jnp.concatenate([dsh, dsc], axis=0), lay.first_of_mod(i))

    R_ = n_tiles * T
    gated = gate is not None
    return pl.pallas_call(
        body, grid=(n_tiles,),
        in_specs=[_row(D), _full((1, D)), _modspec(lay, 6, D), _row(D), _row(D)] + ([_row(D), _modspec(lay, 6, D)] if gated else []),
        out_specs=[_row(D), _modspec(lay, 2, D), _full((1, D))] + ([_row(D), _modspec(lay, 1, D)] if gated else []),
        out_shape=[jax.ShapeDtypeStruct((R_, D), F32), jax.ShapeDtypeStruct((lay.B + 1, 2, D), F32),
                   jax.ShapeDtypeStruct((1, D), F32)]
        + ([jax.ShapeDtypeStruct((R_, D), BF16), jax.ShapeDtypeStruct((lay.B + 1, 1, D), F32)] if gated else []),
        compiler_params=_cparams(("arbitrary",)), name=name)(S, g, mod, dxn, ds_in, *(gate[:2] if gated else []))


def gate_bwd(dS, y, mod, gate_row, lay, n_tiles, name):
    D = dS.shape[1]

    def body(ds_ref, y_ref, mod_ref, dy_ref, dgate_ref):
        i = pl.program_id(0)
        ds = ds_ref[...]
        dy_ref[...] = (mod_ref[gate_row:gate_row + 1, :] * ds).astype(BF16)
        _acc(dgate_ref, jnp.sum(ds * y_ref[...], axis=0, keepdims=True), lay.first_of_mod(i))

    return pl.pallas_call(
        body, grid=(n_tiles,), in_specs=[_row(D), _row(D), _modspec(lay, 6, D)],
        out_specs=[_row(D), _modspec(lay, 1, D)],
        out_shape=[jax.ShapeDtypeStruct((n_tiles * T, D), BF16), jax.ShapeDtypeStruct((lay.B + 1, 1, D), F32)],
        compiler_params=_cparams(("arbitrary",)), name=name)(dS, y, mod)


def _swiglu(a, b):
    return a * jax.nn.sigmoid(a) * b


def _ffn_tiles(M, F):
    tn = _pick(F, 1408, 128)
    return _pick(M, (3 << 18) // tn, 128), tn


def _col_chunks(n, width=512):
    return [slice(c, min(c + width, n)) for c in range(0, n, width)]


def ffn_in_act(x, ws, l, name):
    M, D = x.shape
    Cs = ws.shape[3]
    tm = _pick(M, (3 << 18) // Cs, 128)

    def body(x_ref, wa_ref, wb_ref, act_ref, a_ref, b_ref):
        dims = (((1,), (0,)), ((), ()))
        for cols in _col_chunks(Cs):
            a = lax.dot_general(x_ref[...], wa_ref[:, cols], dims, preferred_element_type=F32)
            b = lax.dot_general(x_ref[...], wb_ref[:, cols], dims, preferred_element_type=F32)
            act_ref[:, cols] = _swiglu(a, b).astype(BF16)
            a_ref[:, cols] = a.astype(BF16)
            b_ref[:, cols] = b.astype(BF16)

    out = pl.BlockSpec((tm, Cs), lambda j, i: (i, j))
    return pl.pallas_call(
        body, grid=(2, M // tm),
        in_specs=[pl.BlockSpec((tm, D), lambda j, i: (i, 0)), pl.BlockSpec((None, None, D, Cs), lambda j, i: (j, l, 0, 0)),
                  pl.BlockSpec((None, None, D, Cs), lambda j, i: (2 + j, l, 0, 0))],
        out_specs=[out, out, out], out_shape=[jax.ShapeDtypeStruct((M, 2 * Cs), BF16)] * 3,
        compiler_params=_cparams(("parallel", "parallel")), name=name)(x, ws, ws)


def ffn_out_dx_act(df, wt, a, b, name):
    M, D = df.shape
    F = wt.shape[1]
    tm, tn = _ffn_tiles(M, F)

    def body(df_ref, w_ref, a_ref, b_ref, da_ref, db_ref):
        for cols in _col_chunks(tn):
            dact = lax.dot_general(df_ref[...], w_ref[:, cols], (((1,), (0,)), ((), ())), preferred_element_type=F32)
            _, vjp = jax.vjp(_swiglu, a_ref[:, cols].astype(F32), b_ref[:, cols].astype(F32))
            da, db = vjp(dact)
            da_ref[:, cols] = da.astype(BF16)
            db_ref[:, cols] = db.astype(BF16)

    blk = pl.BlockSpec((tm, tn), lambda j, i: (i, j))
    return pl.pallas_call(
        body, grid=(F // tn, M // tm),
        in_specs=[pl.BlockSpec((tm, D), lambda j, i: (i, 0)), pl.BlockSpec((D, tn), lambda j, i: (0, j)), blk, blk],
        out_specs=[blk, blk], out_shape=[jax.ShapeDtypeStruct((M, F), BF16)] * 2,
        compiler_params=_cparams(("parallel", "parallel")), name=name)(df, wt, a, b)


def ffn_in_dx(da, db, ws, l, name):
    M, F = da.shape
    D, Cs = ws.shape[2], ws.shape[3]
    tm, tn = _pick(M, 512, 128), _pick(D, 1024, 128)

    def body(da_ref, db_ref, w0_ref, w1_ref, w2_ref, w3_ref, o_ref):
        dims = (((1,), (1,)), ((), ()))
        dot = lambda g_ref, s, w_ref: lax.dot_general(g_ref[:, s * Cs:(s + 1) * Cs], w_ref[...], dims, preferred_element_type=F32)
        o_ref[...] = (dot(da_ref, 0, w0_ref) + dot(da_ref, 1, w1_ref)) + (dot(db_ref, 0, w2_ref) + dot(db_ref, 1, w3_ref))

    shard = lambda s: pl.BlockSpec((None, None, tn, Cs), lambda i, j: (s, l, j, 0))
    return pl.pallas_call(
        body, grid=(M // tm, D // tn),
        in_specs=[pl.BlockSpec((tm, F), lambda i, j: (i, 0)), pl.BlockSpec((tm, F), lambda i, j: (i, 0))]
        + [shard(s) for s in range(4)],
        out_specs=pl.BlockSpec((tm, tn), lambda i, j: (i, j)), out_shape=jax.ShapeDtypeStruct((M, D), F32),
        compiler_params=_cparams(("parallel", "parallel")), name=name)(da, db, ws, ws, ws, ws)


def loss_head(S, g, target, lay, name):
    D = S.shape[1]
    nxt = lay.nxt

    def tile_loss(x, gg, t):
        err = rms(x, gg) - t
        return 0.5 * jnp.sum(jnp.mean(err * err, axis=-1))

    def body(s_ref, g_ref, t_ref, loss_ref, ds_ref, dg_ref):
        i = pl.program_id(0)

        @pl.when(i < nxt)
        def _():
            val, vjp = jax.vjp(tile_loss, s_ref[...], g_ref[...], t_ref[...])
            dx, dg, _ = vjp(jnp.ones((), F32))
            ds_ref[...] = dx
            _acc(dg_ref, dg, i == 0)
            _acc(loss_ref, jnp.full((8, 128), val, F32), i == 0)

        @pl.when(i >= nxt)
        def _():
            ds_ref[...] = jnp.zeros((T, D), F32)

    return pl.pallas_call(
        body, grid=(lay.nt,),
        in_specs=[_row(D), _full((1, D)), pl.BlockSpec((T, D), lambda i: (jnp.minimum(i, nxt - 1), 0))],
        out_specs=[_full((8, 128)), _row(D), _full((1, D))],
        out_shape=[jax.ShapeDtypeStruct((8, 128), F32), jax.ShapeDtypeStruct((lay.R, D), F32),
                   jax.ShapeDtypeStruct((1, D), F32)],
        compiler_params=_cparams(("arbitrary",)), name=name)(S, g, target)


def _ab_prep(zkv, zq, zpe, c256, s256, c128, s128, gkv, gq, wk, wv, wqn, wqp):
    kvn = rms(zkv, gkv)
    kn, v = bdot(kvn, wk), bdot(kvn, wv)
    qn = rms(zq, gq)
    qnope, qpe = bdot(qn, wqn) * _MLA_SCALE, rope(bdot(qn, wqp), c256, s256) * _MLA_SCALE
    kpe = rope(zpe, c128, s128)
    fold = _fold_matrix()
    qparts, kparts = [], []
    for h in range(MLA_HEADS):
        qparts += [qnope[:, 128 * h:128 * (h + 1)], bdot(qpe * _head_mask(qpe.shape, h), fold)]
        kparts += [kn[:, 128 * h:128 * (h + 1)], kpe]
    return jnp.concatenate(qparts, axis=1), jnp.concatenate(kparts, axis=1), v


def _ab_prep_specs(lay):
    return [_row(256, 0), _row(256, 1), _row(128, 12), _ropespec(lay, 256), _ropespec(lay, 256), _ropespec(lay, 128),
            _ropespec(lay, 128), _full((1, 256)), _full((1, 256)), _full((256, 512)), _full((256, 512)),
            _full((256, 512)), _full((256, 256))]


def ab_prep_fwd(z, tabc, tabs, gkv, gq, wk, wv, wqn, wqp, lay, name):
    def body(*refs):
        ins, (q_ref, k_ref, v_ref) = refs[:13], refs[13:]
        q, k, v = _ab_prep(*[r[...].astype(F32) for r in ins])
        q_ref[...] = q.astype(BF16)
        k_ref[...] = k.astype(BF16)
        v_ref[...] = v.astype(BF16)

    R = lay.R
    return pl.pallas_call(
        body, grid=(lay.nt,), in_specs=_ab_prep_specs(lay), out_specs=[_row(1024), _row(1024), _row(512)],
        out_shape=[jax.ShapeDtypeStruct((R, 1024), BF16), jax.ShapeDtypeStruct((R, 1024), BF16),
                   jax.ShapeDtypeStruct((R, 512), BF16)],
        compiler_params=_cparams(("parallel",)), name=name)(z, z, z, tabc, tabs, tabc, tabs, gkv, gq, wk, wv, wqn, wqp)


_MLA_SCALE = (MLA_NOPE + MLA_ROPE) ** -0.5


def _mla_x(q, kx, vx, kh, vh):
    sx, sh = bdot_nt(q, kx), bdot_nt(q, kh)
    m = lax.stop_gradient(jnp.maximum(jnp.max(sx, axis=-1, keepdims=True), jnp.max(sh, axis=-1, keepdims=True)))
    ex, eh = jnp.exp(sx - m), jnp.exp(sh - m)
    inv = 1.0 / (jnp.sum(ex, axis=-1, keepdims=True) + jnp.sum(eh, axis=-1, keepdims=True))
    return (bdot(ex, vx) + bdot(eh, vh)) * inv


def _mla_h(q, kh, vh):
    sh = bdot_nt(q, kh)
    eh = jnp.exp(sh - lax.stop_gradient(jnp.max(sh, axis=-1, keepdims=True)))
    return bdot(eh, vh) * (1.0 / jnp.sum(eh, axis=-1, keepdims=True))


def _mla_specs(lay):
    nxt, SEQ = lay.nxt, lay.SEQ
    return [pl.BlockSpec((T, 256), lambda b, h, qi: (lay.qrow(b, qi), h)),
            pl.BlockSpec((SEQ, 256), lambda b, h, qi: (b, h)), pl.BlockSpec((SEQ, 128), lambda b, h, qi: (b, h)),
            pl.BlockSpec((T, 256), lambda b, h, qi: (nxt + b, h)), pl.BlockSpec((T, 128), lambda b, h, qi: (nxt + b, h))]


def mla_fwd(q, k, v, lay, name):
    tps = lay.tps

    def body(q_ref, kx_ref, vx_ref, kh_ref, vh_ref, o_ref):
        qi = pl.program_id(2)
        f = lambda r: r[...]

        @pl.when(qi < tps)
        def _():
            o_ref[...] = _mla_x(f(q_ref), f(kx_ref), f(vx_ref), f(kh_ref), f(vh_ref)).astype(BF16)

        @pl.when(qi == tps)
        def _():
            o_ref[...] = _mla_h(f(q_ref), f(kh_ref), f(vh_ref)).astype(BF16)

    return pl.pallas_call(
        body, grid=(lay.B, MLA_HEADS, lay.nq), in_specs=_mla_specs(lay),
        out_specs=pl.BlockSpec((T, 128), lambda b, h, qi: (lay.qrow(b, qi), h)),
        out_shape=jax.ShapeDtypeStruct((lay.R, 512), BF16),
        compiler_params=_cparams(("parallel", "parallel", "arbitrary")), name=name)(q, k, v, k, v)


def mla_bwd(q, k, v, dmerged, lay, name):
    tps, SEQ, B = lay.tps, lay.SEQ, lay.B

    def body(q_ref, kx_ref, vx_ref, kh_ref, vh_ref, do_ref, dq_ref, dkx_ref, dkh_ref, dvx_ref, dvh_ref):
        qi = pl.program_id(2)
        f = lambda r: r[...].astype(F32)

        @pl.when(qi < tps)
        def _():
            _, vjp = jax.vjp(_mla_x, f(q_ref), f(kx_ref), f(vx_ref), f(kh_ref), f(vh_ref))
            dq, dkx, dvx, dkh, dvh = vjp(do_ref[...])
            dq_ref[...] = dq
            _acc(dkx_ref, dkx, qi == 0)
            _acc(dvx_ref, dvx, qi == 0)
            _acc(dkh_ref, dkh, qi == 0)
            _acc(dvh_ref, dvh, qi == 0)

        @pl.when(qi == tps)
        def _():
            _, vjp = jax.vjp(_mla_h, f(q_ref), f(kh_ref), f(vh_ref))
            dq, dkh, dvh = vjp(do_ref[...])
            dq_ref[...] = dq
            dkh_ref[...] += dkh
            dvh_ref[...] += dvh

    return pl.pallas_call(
        body, grid=(B, MLA_HEADS, lay.nq),
        in_specs=_mla_specs(lay) + [pl.BlockSpec((T, 128), lambda b, h, qi: (lay.qrow(b, qi), h))],
        out_specs=[pl.BlockSpec((T, 256), lambda b, h, qi: (lay.qrow(b, qi), h)),
                   pl.BlockSpec((SEQ, 256), lambda b, h, qi: (b, h)), pl.BlockSpec((T, 256), lambda b, h, qi: (b, h)),
                   pl.BlockSpec((SEQ, 128), lambda b, h, qi: (b, h)), pl.BlockSpec((T, 128), lambda b, h, qi: (b, h))],
        out_shape=[jax.ShapeDtypeStruct((lay.R, 1024), F32), jax.ShapeDtypeStruct((lay.NX, 1024), F32),
                   jax.ShapeDtypeStruct((B * T, 1024), F32), jax.ShapeDtypeStruct((lay.NX, 512), F32),
                   jax.ShapeDtypeStruct((B * T, 512), F32)],
        compiler_params=_cparams(("parallel", "parallel", "arbitrary")), name=name)(q, k, v, k, v, dmerged)


def _cmlp_piece(zu, zv, g, ws, bs):
    u, v = jax.nn.gelu(zu), jax.nn.gelu(zv)
    v = v * lax.rsqrt(jnp.mean(v * v, axis=-1, keepdims=True) + EPS) * g
    return u * (bdot(ws, v) + bs)


def _pieces():
    return [(c, g) for c in range(T // CMLP_CHUNK) for g in range(CMLP_GROUPS)]


def cmlp_merge_fwd(z, o, gvn, ws, bs, lay, name):
    def body(zu_ref, zv_ref, o_ref, g_ref, ws_ref, bs_ref, m_ref):
        m_ref[:, 0:512] = o_ref[...]
        for c, g in _pieces():
            rows, cols = slice(128 * c, 128 * (c + 1)), slice(128 * g, 128 * (g + 1))
            piece = _cmlp_piece(zu_ref[rows, cols], zv_ref[rows, cols], g_ref[:, cols], ws_ref[g], bs_ref[g])
            m_ref[rows, 512 + 128 * g:512 + 128 * (g + 1)] = piece.astype(BF16)

    return pl.pallas_call(
        body, grid=(lay.nt,),
        in_specs=[_row(512, 1), _row(512, 2), _row(512), _full((1, 512)), _full((4, 128, 128)), _full((4, 128, 1))],
        out_specs=_row(1024), out_shape=jax.ShapeDtypeStruct((lay.R, 1024), BF16),
        compiler_params=_cparams(("parallel",)), name=name)(z, z, o, gvn, ws, bs)


def ab_rows_bwd(z, tabc, tabs, dq, dkx, dkh, dvx, dvh, dmerged, gkv, gq, wk, wv, wqn, wqp, gvn, ws, bs, lay, name):
    def body(*refs):
        prep_in = refs[:3] + refs[5:9] + refs[11:17]
        zu_ref, zv_ref = refs[3:5]
        dq_ref, dcm_ref = refs[9:11]
        gvn_ref, ws_ref, bs_ref = refs[17:20]
        dkx_ref, dkh_ref, dvx_ref, dvh_ref = refs[20:24]
        dz_ref, dgkv_ref, dgq_ref, dwk_ref, dwv_ref, dwqn_ref, dwqp_ref, dgvn_ref, dws_ref, dbs_ref = refs[24:]
        first = pl.program_id(0) == 0
        _, vjp = jax.vjp(_ab_prep, *[r[...].astype(F32) for r in prep_in])
        d = vjp((dq_ref[...], _xh_pick(lay, dkx_ref, dkh_ref), _xh_pick(lay, dvx_ref, dvh_ref)))
        dz_ref[:, 0:256] = d[0].astype(BF16)
        dz_ref[:, 256:512] = d[1].astype(BF16)
        dz_ref[:, 1536:1664] = d[2].astype(BF16)
        for ref, val in zip((dgkv_ref, dgq_ref, dwk_ref, dwv_ref, dwqn_ref, dwqp_ref), d[7:]):
            _acc(ref, val, first)
        dws = [0.0] * CMLP_GROUPS
        dbs = [0.0] * CMLP_GROUPS
        dgv = [0.0] * CMLP_GROUPS
        for c, g in _pieces():
            rows, cols = slice(128 * c, 128 * (c + 1)), slice(128 * g, 128 * (g + 1))
            _, vjp = jax.vjp(_cmlp_piece, zu_ref[rows, cols], zv_ref[rows, cols], gvn_ref[:, cols], ws_ref[g], bs_ref[g])
            dzu, dzv, dg_, dws_, dbs_ = vjp(dcm_ref[rows, cols])
            dz_ref[rows, 512 + 128 * g:512 + 128 * (g + 1)] = dzu.astype(BF16)
            dz_ref[rows, 1024 + 128 * g:1024 + 128 * (g + 1)] = dzv.astype(BF16)
            dws[g], dbs[g], dgv[g] = dws[g] + dws_, dbs[g] + dbs_, dgv[g] + dg_
        _acc(dgvn_ref, jnp.concatenate(dgv, axis=1), first)
        _acc(dws_ref, jnp.stack(dws), first)
        _acc(dbs_ref, jnp.stack(dbs), first)

    acc_shapes = [(1, 256), (1, 256), (256, 512), (256, 512), (256, 512), (256, 256), (1, 512), (4, 128, 128), (4, 128, 1)]
    return pl.pallas_call(
        body, grid=(lay.nt,),
        in_specs=_ab_prep_specs(lay)[:3] + [_row(512, 1), _row(512, 2)] + _ab_prep_specs(lay)[3:7]
        + [_row(1024), _row(512, 1)] + _ab_prep_specs(lay)[7:]
        + [_full((1, 512)), _full((4, 128, 128)), _full((4, 128, 1))] + _xh_specs(lay, 1024) + _xh_specs(lay, 512),
        out_specs=[_row(AB_IN_P)] + [_full(s) for s in acc_shapes],
        out_shape=[jax.ShapeDtypeStruct((lay.R, AB_IN_P), BF16)] + [jax.ShapeDtypeStruct(s, F32) for s in acc_shapes],
        compiler_params=_cparams(("arbitrary",)), name=name)(
            z, z, z, z, z, tabc, tabs, tabc, tabs, dq, dmerged, gkv, gq, wk, wv, wqn, wqp, gvn, ws, bs, dkx, dkh, dvx, dvh)


def _cd_prep(zrk, zrq, zsk, zsq0, zsq1, zsv, c256, s256, c128, s128):
    rk = rope(zrk * (RET_QK ** -0.5), c256, s256)
    rq = rope(zrq, c256, s256)
    sk = rope(zsk, c128, s128)
    sq0, sq1 = rope(zsq0 * _SWA_SCALE, c256, s256), rope(zsq1 * _SWA_SCALE, c256, s256)
    e0, e1 = _expand_matrix(0), _expand_matrix(1)
    return rq, rk, sq0, sq1, bdot(sk, e0), bdot(sk, e1), bdot(zsv, e0), bdot(zsv, e1)


def _cd_prep_specs(lay):
    return [_row(256, 0), _row(256, 4), _row(128, 6), _row(256, 7), _row(256, 8), _row(128, 7),
            _ropespec(lay, 256), _ropespec(lay, 256), _ropespec(lay, 128), _ropespec(lay, 128)]


def cd_prep_fwd(z, tabc, tabs, lay, name):
    def body(*refs):
        ins, (rq_ref, rk_ref, sq_ref, ke_ref, ve_ref) = refs[:10], refs[10:]
        rq, rk, sq0, sq1, k0, k1, v0, v1 = _cd_prep(*[r[...] for r in ins])
        rq_ref[...] = rq.astype(BF16)
        rk_ref[...] = rk.astype(BF16)
        for ref, (a, b) in ((sq_ref, (sq0, sq1)), (ke_ref, (k0, k1)), (ve_ref, (v0, v1))):
            ref[:, 0:256] = a.astype(BF16)
            ref[:, 256:512] = b.astype(BF16)

    R = lay.R
    return pl.pallas_call(
        body, grid=(lay.nt,), in_specs=_cd_prep_specs(lay),
        out_specs=[_row(256), _row(256), _row(512), _row(512), _row(512)],
        out_shape=[jax.ShapeDtypeStruct((R, w), BF16) for w in (256, 256, 512, 512, 512)],
        compiler_params=_cparams(("parallel",)), name=name)(z, z, z, z, z, z, tabc, tabs, tabc, tabs)


def _ret_sample(h, qs, ks, vs, df, db):
    lgf, lgb = log_sigmoid(df), log_sigmoid(db)
    idx = lax.broadcasted_iota(jnp.int32, (T, 1), 0).astype(F32)
    diff = idx - lax.broadcasted_iota(jnp.int32, (1, T), 1).astype(F32)
    intra = (jnp.where(diff >= 0, jnp.exp(lgf * jnp.maximum(diff, 0.0)), 0.0)
             + jnp.where(diff <= 0, jnp.exp(lgb * jnp.maximum(-diff, 0.0)), 0.0))
    qdf, kdf, cdf = jnp.exp(lgf * (idx + 1.0)), jnp.exp(lgf * (T - 1.0 - idx)), jnp.exp(lgf * T)
    qdb, kdb, cdb = jnp.exp(lgb * (T - idx)), jnp.exp(lgb * idx), jnp.exp(lgb * T)
    mask = _head_mask(qs[0].shape, h)
    qs = [q * mask for q in qs]
    ys = [bdot(bdot_nt(q, k) * intra, v) for q, k, v in zip(qs, ks, vs)]
    n = len(qs)
    state = bdot_tn(ks[0] * kdf, vs[0])
    for i in range(1, n):
        ys[i] = ys[i] + bdot(qs[i] * qdf, state)
        if i + 1 < n:
            state = state * cdf + bdot_tn(ks[i] * kdf, vs[i])
    state = bdot_tn(ks[0] * kdb, vs[0])
    for i in range(n - 1, 0, -1):
        ys[i] = ys[i] + bdot(qs[i] * qdb, state)
        if i > 1:
            state = state * cdb + bdot_tn(ks[i] * kdb, vs[i])
    return ys


def _ret_specs(lay):
    nxt, SEQ = lay.nxt, lay.SEQ
    xs = lambda w, col: pl.BlockSpec((SEQ, w), lambda b, h: (b, col(h)))
    hs = lambda w, col: pl.BlockSpec((T, w), lambda b, h: (nxt + b, col(h)))
    zero, head = (lambda h: 0), (lambda h: 2 + h)
    dec = pl.BlockSpec((None, 8, 128), lambda b, h: (h, 0, 0))
    return [xs(256, zero), hs(256, zero), xs(256, zero), hs(256, zero), xs(128, head), hs(128, head), dec, dec]


def _ret_tiles(x_ref, h_ref, tps, cast=None):
    tiles = [h_ref[...]] + [x_ref[i * T:(i + 1) * T, :] for i in range(tps)]
    return [t.astype(cast) for t in tiles] if cast is not None else tiles


def ret_fwd(rq, rk, z, decf, decb, lay, name):
    tps, SEQ = lay.tps, lay.SEQ

    def body(qx_ref, qh_ref, kx_ref, kh_ref, vx_ref, vh_ref, df_ref, db_ref, yx_ref, yh_ref):
        ys = _ret_sample(pl.program_id(1), _ret_tiles(qx_ref, qh_ref, tps), _ret_tiles(kx_ref, kh_ref, tps),
                         _ret_tiles(vx_ref, vh_ref, tps), df_ref[0:1, 0:1], db_ref[0:1, 0:1])
        yh_ref[...] = ys[0]
        for i in range(tps):
            yx_ref[i * T:(i + 1) * T, :] = ys[i + 1]

    return pl.pallas_call(
        body, grid=(lay.B, RET_HEADS), in_specs=_ret_specs(lay),
        out_specs=[pl.BlockSpec((SEQ, 128), lambda b, h: (b, h)), pl.BlockSpec((T, 128), lambda b, h: (b, h))],
        out_shape=[jax.ShapeDtypeStruct((lay.NX, 512), F32), jax.ShapeDtypeStruct((lay.B * T, 512), F32)],
        compiler_params=_cparams(("parallel", "arbitrary")), name=name)(rq, rq, rk, rk, z, z, decf, decb)


def ret_bwd(rq, rk, z, decf, decb, dy, lay, name):
    tps, SEQ, B = lay.tps, lay.SEQ, lay.B
    nxt = lay.nxt

    def body(qx_ref, qh_ref, kx_ref, kh_ref, vx_ref, vh_ref, df_ref, db_ref, dyx_ref, dyh_ref,
             dqx_ref, dqh_ref, dkx_ref, dkh_ref, dvx_ref, dvh_ref, ddf_ref, ddb_ref):
        h = pl.program_id(1)
        _, vjp = jax.vjp(functools.partial(_ret_sample, h), _ret_tiles(qx_ref, qh_ref, tps, F32),
                         _ret_tiles(kx_ref, kh_ref, tps, F32), _ret_tiles(vx_ref, vh_ref, tps), df_ref[0:1, 0:1],
                         db_ref[0:1, 0:1])
        dqs, dks, dvs, ddf, ddb = vjp(_ret_tiles(dyx_ref, dyh_ref, tps))
        first = h == 0
        _acc(dqh_ref, dqs[0], first)
        _acc(dkh_ref, dks[0], first)
        dvh_ref[...] = dvs[0]
        for i in range(tps):
            rows = slice(i * T, (i + 1) * T)
            _acc(dqx_ref.at[rows], dqs[i + 1], first)
            _acc(dkx_ref.at[rows], dks[i + 1], first)
            dvx_ref[rows, :] = dvs[i + 1]
        @pl.when(jnp.logical_and(pl.program_id(0) == 0, first))
        def _():
            ddf_ref[...] = jnp.zeros(ddf_ref.shape, F32)
            ddb_ref[...] = jnp.zeros(ddb_ref.shape, F32)

        ddf_ref[h] += jnp.broadcast_to(ddf, (8, 128))
        ddb_ref[h] += jnp.broadcast_to(ddb, (8, 128))

    acc_x, acc_h = pl.BlockSpec((SEQ, 256), lambda b, h: (b, 0)), pl.BlockSpec((T, 256), lambda b, h: (b, 0))
    head_x, head_h = pl.BlockSpec((SEQ, 128), lambda b, h: (b, h)), pl.BlockSpec((T, 128), lambda b, h: (b, h))
    dec = pl.BlockSpec((RET_HEADS, 8, 128), lambda b, h: (0, 0, 0))
    return pl.pallas_call(
        body, grid=(B, RET_HEADS),
        in_specs=_ret_specs(lay) + [head_x, pl.BlockSpec((T, 128), lambda b, h: (nxt + b, h))],
        out_specs=[acc_x, acc_h, acc_x, acc_h, head_x, head_h, dec, dec],
        out_shape=[jax.ShapeDtypeStruct((lay.NX, 256), F32), jax.ShapeDtypeStruct((B * T, 256), F32),
                   jax.ShapeDtypeStruct((lay.NX, 256), F32), jax.ShapeDtypeStruct((B * T, 256), F32),
                   jax.ShapeDtypeStruct((lay.NX, 512), F32), jax.ShapeDtypeStruct((B * T, 512), F32),
                   jax.ShapeDtypeStruct((RET_HEADS, 8, 128), F32), jax.ShapeDtypeStruct((RET_HEADS, 8, 128), F32)],
        compiler_params=_cparams(("arbitrary", "arbitrary")), name=name)(rq, rq, rk, rk, z, z, decf, decb, dy, dy)


_SWA_SCALE = SWA_HEAD_DIM ** -0.5


def _swa_head(qh, sw, kh, vw, vh, sink):
    sh = bdot_nt(qh, kh)
    m = jnp.maximum(jnp.max(sh, axis=-1, keepdims=True), sink)
    if sw is not None:
        m = jnp.maximum(m, jnp.max(sw, axis=-1, keepdims=True))
    m = lax.stop_gradient(m)
    eh, es = jnp.exp(sh - m), jnp.exp(sink - m)
    tot = jnp.sum(eh, axis=-1, keepdims=True) + es
    if sw is None:
        return bdot(eh, vh) * (1.0 / tot)
    ew = jnp.exp(sw - m)
    return (bdot(ew, vw) + bdot(eh, vh)) * (1.0 / (tot + jnp.sum(ew, axis=-1, keepdims=True)))


def _swa_x(t0, kpos0, sq, kw, vw, kh, vh, *sinks):
    t = t0 + lax.broadcasted_iota(jnp.int32, (T, 1), 0)
    pos = kpos0 + lax.broadcasted_iota(jnp.int32, (1, SWA_SPAN), 1)
    band = jnp.abs(t - pos) <= SWA_WINDOW
    out = 0.0
    for i in range(SWA_GROUPS):
        mi = _head_mask(sq.shape, i)
        qh = sq * mi
        sw = jnp.where(band, bdot_nt(qh, kw), NEG_INF)
        out = out + _swa_head(qh, sw, kh, vw, vh, sinks[i]) * mi
    return out


def _swa_h(sq, kh, vh, *sinks):
    out = 0.0
    for i in range(SWA_GROUPS):
        mi = _head_mask(sq.shape, i)
        out = out + _swa_head(sq * mi, None, kh, None, vh, sinks[i]) * mi
    return out


def _swa_specs(lay):
    nxt, SEQ = lay.nxt, lay.SEQ
    return [pl.BlockSpec((T, 256), lambda g, b, qi: (lay.qrow(b, qi), g)),
            pl.BlockSpec((SEQ, 256), lambda g, b, qi: (b, g)), pl.BlockSpec((SEQ, 256), lambda g, b, qi: (b, g)),
            pl.BlockSpec((T, 256), lambda g, b, qi: (nxt + b, g)), pl.BlockSpec((T, 256), lambda g, b, qi: (nxt + b, g)),
            pl.BlockSpec((None, 4, 8, 128), lambda g, b, qi: (g, 0, 0, 0))]


def _swa_start(qi, SEQ):
    return pl.multiple_of(jnp.clip(qi * T - SWA_WINDOW, 0, SEQ - SWA_SPAN), SWA_WINDOW)


def swa_fwd(sq, kexp, vexp, sink, lay, name):
    tps, SEQ = lay.tps, lay.SEQ

    def body(sq_ref, kx_ref, vx_ref, kh_ref, vh_ref, sink_ref, o_ref):
        qi = pl.program_id(2)
        f = lambda r: r[...]
        sinks = [sink_ref[i][0:1, 0:1] for i in range(SWA_GROUPS)]

        @pl.when(qi < tps)
        def _():
            k0 = _swa_start(qi, SEQ)
            kw, vw = kx_ref[pl.ds(k0, SWA_SPAN), :], vx_ref[pl.ds(k0, SWA_SPAN), :]
            o_ref[...] = _swa_x(qi * T, k0, f(sq_ref), kw, vw, f(kh_ref), f(vh_ref), *sinks).astype(BF16)

        @pl.when(qi == tps)
        def _():
            o_ref[...] = _swa_h(f(sq_ref), f(kh_ref), f(vh_ref), *sinks).astype(BF16)

    return pl.pallas_call(
        body, grid=(SWA_KV_HEADS, lay.B, lay.nq), in_specs=_swa_specs(lay),
        out_specs=pl.BlockSpec((T, 256), lambda g, b, qi: (lay.qrow(b, qi), g)),
        out_shape=jax.ShapeDtypeStruct((lay.R, 512), BF16),
        compiler_params=_cparams(("parallel", "parallel", "arbitrary")), name=name)(sq, kexp, vexp, kexp, vexp, sink)


def swa_bwd(sq, kexp, vexp, sink, dmerged, lay, name):
    tps, SEQ, B = lay.tps, lay.SEQ, lay.B

    def body(sq_ref, kx_ref, vx_ref, kh_ref, vh_ref, sink_ref, do_ref, dsq_ref, dkx_ref, dkh_ref, dvx_ref, dvh_ref, dsink_ref):
        b, qi = pl.program_id(1), pl.program_id(2)
        f = lambda r: r[...].astype(F32)
        sinks = [sink_ref[i][0:1, 0:1] for i in range(SWA_GROUPS)]
        very_first = jnp.logical_and(b == 0, qi == 0)

        def acc_sink(ds):
            for i in range(SWA_GROUPS):
                _acc(dsink_ref.at[i], jnp.broadcast_to(ds[i], (8, 128)), very_first)

        @pl.when(qi == 0)
        def _():
            for ref in (dkx_ref, dkh_ref, dvx_ref, dvh_ref):
                ref[...] = jnp.zeros(ref.shape, F32)

        @pl.when(qi < tps)
        def _():
            k0 = _swa_start(qi, SEQ)
            win = pl.ds(k0, SWA_SPAN)
            kw, vw = kx_ref[win, :].astype(F32), vx_ref[win, :].astype(F32)
            _, vjp = jax.vjp(functools.partial(_swa_x, qi * T, k0), f(sq_ref), kw, vw, f(kh_ref), f(vh_ref), *sinks)
            d = vjp(do_ref[...])
            dsq_ref[...] = d[0]
            dkx_ref[win, :] += d[1]
            dvx_ref[win, :] += d[2]
            dkh_ref[...] += d[3]
            dvh_ref[...] += d[4]
            acc_sink(d[5:9])

        @pl.when(qi == tps)
        def _():
            _, vjp = jax.vjp(_swa_h, f(sq_ref), f(kh_ref), f(vh_ref), *sinks)
            d = vjp(do_ref[...])
            dsq_ref[...] = d[0]
            dkh_ref[...] += d[1]
            dvh_ref[...] += d[2]
            acc_sink(d[3:7])

    xs = pl.BlockSpec((SEQ, 256), lambda g, b, qi: (b, g))
    hs = pl.BlockSpec((T, 256), lambda g, b, qi: (b, g))
    return pl.pallas_call(
        body, grid=(SWA_KV_HEADS, B, lay.nq),
        in_specs=_swa_specs(lay) + [pl.BlockSpec((T, 256), lambda g, b, qi: (lay.qrow(b, qi), 2 + g))],
        out_specs=[pl.BlockSpec((T, 256), lambda g, b, qi: (lay.qrow(b, qi), g)), xs, hs, xs, hs,
                   pl.BlockSpec((None, 4, 8, 128), lambda g, b, qi: (g, 0, 0, 0))],
        out_shape=[jax.ShapeDtypeStruct((lay.R, 512), F32), jax.ShapeDtypeStruct((lay.NX, 512), F32),
                   jax.ShapeDtypeStruct((B * T, 512), F32), jax.ShapeDtypeStruct((lay.NX, 512), F32),
                   jax.ShapeDtypeStruct((B * T, 512), F32), jax.ShapeDtypeStruct((SWA_KV_HEADS, 4, 8, 128), F32)],
        compiler_params=_cparams(("arbitrary", "arbitrary", "arbitrary")), name=name)(
            sq, kexp, vexp, kexp, vexp, sink, dmerged)


def _cd_merge_piece(y, rg, g):
    return (y * lax.rsqrt(jnp.mean(y * y, axis=-1, keepdims=True) + EPS) * g) * (rg * jax.nn.sigmoid(rg))


def cd_merge_fwd(y, z, o, gn, lay, name):
    def body(yx_ref, yh_ref, rga_ref, rgb_ref, o_ref, g_ref, m_ref):
        y = _xh_pick(lay, yx_ref, yh_ref)
        for h in range(RET_HEADS):
            cols = slice(128 * h, 128 * (h + 1))
            rg_ref, rcols = (rga_ref, cols) if h < 2 else (rgb_ref, slice(128 * (h - 2), 128 * (h - 1)))
            m_ref[:, cols] = _cd_merge_piece(y[:, cols], rg_ref[:, rcols], g_ref[:, cols]).astype(BF16)
        m_ref[:, 512:1024] = o_ref[...]

    return pl.pallas_call(
        body, grid=(lay.nt,), in_specs=_xh_specs(lay, 512) + [_row(256, 5), _row(256, 6), _row(512), _full((1, 512))],
        out_specs=_row(1024), out_shape=jax.ShapeDtypeStruct((lay.R, 1024), BF16),
        compiler_params=_cparams(("parallel",)), name=name)(*y, z, z, o, gn)


def cd_merge_bwd(y, z, gn, dmerged, lay, name):
    def body(yx_ref, yh_ref, rga_ref, rgb_ref, g_ref, dm_ref, dy_ref, drg_ref, dg_ref):
        first = pl.program_id(0) == 0
        y = _xh_pick(lay, yx_ref, yh_ref)
        dgs = []
        for h in range(RET_HEADS):
            cols = slice(128 * h, 128 * (h + 1))
            rg_ref, rcols = (rga_ref, cols) if h < 2 else (rgb_ref, slice(128 * (h - 2), 128 * (h - 1)))
            _, vjp = jax.vjp(_cd_merge_piece, y[:, cols], rg_ref[:, rcols], g_ref[:, cols])
            dy, drg, dg = vjp(dm_ref[:, cols])
            dy_ref[:, cols] = dy
            drg_ref[:, cols] = drg
            dgs.append(dg)
        _acc(dg_ref, jnp.concatenate(dgs, axis=1), first)

    return pl.pallas_call(
        body, grid=(lay.nt,), in_specs=_xh_specs(lay, 512) + [_row(256, 5), _row(256, 6), _full((1, 512)), _row(512, 0)],
        out_specs=[_row(512), _row(512), _full((1, 512))],
        out_shape=[jax.ShapeDtypeStruct((lay.R, 512), F32), jax.ShapeDtypeStruct((lay.R, 512), F32),
                   jax.ShapeDtypeStruct((1, 512), F32)],
        compiler_params=_cparams(("arbitrary",)), name=name)(*y, z, z, gn, dmerged)


def cd_rows_bwd(z, tabc, tabs, dsq, drg, drq, drk, dke, dve, drv, lay, name):
    def body(*refs):
        ins = refs[:10]
        dsq_ref, drg_ref = refs[10:12]
        drq, drk, dke, dve, drv = (_xh_pick(lay, refs[12 + 2 * n], refs[13 + 2 * n]) for n in range(5))
        dz_ref = refs[22]
        _, vjp = jax.vjp(_cd_prep, *[r[...] for r in ins])
        cts = (drq, drk, dsq_ref[:, 0:256], dsq_ref[:, 256:512], dke[:, 0:256], dke[:, 256:512],
               dve[:, 0:256], dve[:, 256:512])
        dzrk, dzrq, dzsk, dzsq0, dzsq1, dzsv = vjp(cts)[:6]
        dz_ref[:, 0:256] = dzrk.astype(BF16)
        dz_ref[:, 256:768] = drv.astype(BF16)
        dz_ref[:, 768:896] = dzsk.astype(BF16)
        dz_ref[:, 896:1024] = dzsv.astype(BF16)
        dz_ref[:, 1024:1280] = dzrq.astype(BF16)
        dz_ref[:, 1280:1792] = drg_ref[...].astype(BF16)
        dz_ref[:, 1792:2048] = dzsq0.astype(BF16)
        dz_ref[:, 2048:2304] = dzsq1.astype(BF16)

    return pl.pallas_call(
        body, grid=(lay.nt,),
        in_specs=_cd_prep_specs(lay) + [_row(512), _row(512)] + _xh_specs(lay, 256) + _xh_specs(lay, 256)
        + _xh_specs(lay, 512) + _xh_specs(lay, 512) + _xh_specs(lay, 512),
        out_specs=_row(2304), out_shape=jax.ShapeDtypeStruct((lay.R, 2304), BF16),
        compiler_params=_cparams(("parallel",)), name=name)(
            z, z, z, z, z, z, tabc, tabs, tabc, tabs, dsq, drg, *drq, *drk, *dke, *dve, *drv)


def _pos():
    return lax.axis_index("x"), lax.axis_index("y"), lax.axis_index("c")


def _flip(v, bit):
    return 1 - v if bit else v


def _comm_call(name, body, ins, out_shapes, n_remote, n_local, aliases=None):
    return pl.pallas_call(
        body, in_specs=[ANY] * len(ins), out_specs=[ANY] * len(out_shapes), out_shape=out_shapes,
        scratch_shapes=[pltpu.SemaphoreType.DMA((n_remote,)), pltpu.SemaphoreType.DMA((n_remote,)),
                        pltpu.SemaphoreType.DMA((n_local,))],
        input_output_aliases=aliases or {}, name=name)(*ins)


def gather8(arr, name):
    def body(a_ref, o_ref, ssem, rsem, lsem):
        x, y, c = _pos()
        me = 4 * x + 2 * y + c
        loc = pltpu.make_async_copy(a_ref, o_ref.at[me], lsem.at[0])
        loc.start()
        cps = []
        for m in range(1, 8):
            peer = (_flip(x, m & 4), _flip(y, m & 2), _flip(c, m & 1))
            cps.append(pltpu.make_async_remote_copy(a_ref, o_ref.at[me], ssem.at[m - 1], rsem.at[m - 1],
                                                    device_id=peer, device_id_type=MESH))
            cps[-1].start()
        for cp in cps:
            cp.wait()
        loc.wait()

    return _comm_call(name, body, [arr], [jax.ShapeDtypeStruct((8,) + arr.shape, arr.dtype)], 7, 1)[0]


def gather_chips(arr, name):
    def body(a_ref, o_ref, ssem, rsem, lsem):
        x, y, c = _pos()
        k = 2 * x + y
        loc = pltpu.make_async_copy(a_ref, o_ref.at[k], lsem.at[0])
        loc.start()
        cps = []
        for m in range(1, 4):
            peer = (_flip(x, m & 2), _flip(y, m & 1), c)
            cps.append(pltpu.make_async_remote_copy(a_ref, o_ref.at[k], ssem.at[m - 1], rsem.at[m - 1],
                                                    device_id=peer, device_id_type=MESH))
            cps[-1].start()
        for cp in cps:
            cp.wait()
        loc.wait()

    return _comm_call(name, body, [arr], [jax.ShapeDtypeStruct((4,) + arr.shape, arr.dtype)], 3, 1)[0]


def gather_weights(arrs, name):
    n = len(arrs)

    def body(*refs):
        a_refs, o_refs, (isend, irecv, _) = refs[:n], refs[n:2 * n], refs[2 * n:]
        x, y, c = _pos()
        k = 2 * x + y
        sib = (x, y, 1 - c)
        chips = [(m, (_flip(x, m & 2), _flip(y, m & 1)), 2 * _flip(x, m & 2) + _flip(y, m & 1)) for m in range(1, 4)]
        waits = []
        for w, (a, o) in enumerate(zip(a_refs, o_refs)):
            H = a.shape[0] // 2
            own = pl.ds(c * H, H)
            first = [pltpu.make_async_remote_copy(a.at[own], o.at[k, own], isend.at[7 * w + m - 1], irecv.at[7 * w + m - 1],
                                                  device_id=(*chip, c), device_id_type=MESH) for m, chip, _ in chips]
            first.append(pltpu.make_async_remote_copy(a, o.at[k], isend.at[7 * w + 6], irecv.at[7 * w + 6], device_id=sib,
                                                      device_id_type=MESH))
            for cp in first:
                cp.start()
            waits.append((first, H, own, a, o, w))
        for first, H, own, a, o, w in waits:
            first[3].wait_recv()
            passed = []
            for m, chip, kk in chips:
                pltpu.make_async_remote_copy(a.at[own], o.at[kk, own], isend.at[7 * w + m - 1], irecv.at[7 * w + m - 1],
                                             device_id=(*chip, c), device_id_type=MESH).wait_recv()
                fw = pltpu.make_async_remote_copy(o.at[kk, own], o.at[kk, own], isend.at[7 * w + 2 + m], irecv.at[7 * w + 2 + m],
                                                  device_id=sib, device_id_type=MESH)
                fw.start()
                passed.append(fw)
            for fw in passed:
                fw.wait_recv()
            for cp in first + passed:
                cp.wait_send()

    outs = [jax.ShapeDtypeStruct((4,) + a.shape, a.dtype) for a in arrs]
    return _comm_call(name, body, list(arrs), outs, 7 * n, 1)


def swap_other_half(arrs, name):
    n = len(arrs)

    def body(*refs):
        a_refs, o_refs, (ssem, rsem, _) = refs[:n], refs[n:2 * n], refs[2 * n:]
        x, y, c = _pos()
        cps = []
        for w, (a, o) in enumerate(zip(a_refs, o_refs)):
            H = a.shape[1] // 2
            cps.append(pltpu.make_async_remote_copy(a.at[:, pl.ds((1 - c) * H, H)], o, ssem.at[w], rsem.at[w],
                                                    device_id=(x, y, 1 - c), device_id_type=MESH))
            cps[-1].start()
        for cp in cps:
            cp.wait()

    outs = [jax.ShapeDtypeStruct((4, a.shape[1] // 2) + a.shape[2:], a.dtype) for a in arrs]
    return _comm_call(name, body, list(arrs), outs, n, 1)


def exchange_chips(arrs, name):
    n = len(arrs)

    def body(*refs):
        a_refs, o_refs, (ssem, rsem, lsem) = refs[:n], refs[n:2 * n], refs[2 * n:]
        x, y, c = _pos()
        k = 2 * x + y
        cps = []
        for w, (a, o) in enumerate(zip(a_refs, o_refs)):
            cps.append(pltpu.make_async_copy(a.at[k], o.at[k], lsem.at[w]))
            cps[-1].start()
            for m in range(1, 4):
                px, py = _flip(x, m & 2), _flip(y, m & 1)
                cps.append(pltpu.make_async_remote_copy(a.at[2 * px + py], o.at[k], ssem.at[3 * w + m - 1], rsem.at[3 * w + m - 1],
                                                        device_id=(px, py, c), device_id_type=MESH))
                cps[-1].start()
        for cp in cps:
            cp.wait()

    return _comm_call(name, body, list(arrs), [jax.ShapeDtypeStruct(a.shape, a.dtype) for a in arrs], 3 * n, n)


def share_halves(arrs, name):
    n = len(arrs)

    def body(*refs):
        o_refs, (ssem, rsem, _) = refs[n:2 * n], refs[2 * n:]
        x, y, c = _pos()
        cps = []
        for w, o in enumerate(o_refs):
            H = o.shape[0] // 2
            mine = o.at[pl.ds(c * H, H)]
            cps.append(pltpu.make_async_remote_copy(mine, mine, ssem.at[w], rsem.at[w], device_id=(x, y, 1 - c),
                                                    device_id_type=MESH))
            cps[-1].start()
        for cp in cps:
            cp.wait()

    outs = [jax.ShapeDtypeStruct(a.shape, a.dtype) for a in arrs]
    return _comm_call(name, body, list(arrs), outs, n, 1, aliases={i: i for i in range(n)})


def _adamw(w, g, m, v):
    m = ADAM_B1 * m + (1.0 - ADAM_B1) * g
    v = ADAM_B2 * v + (1.0 - ADAM_B2) * (g * g)
    m_hat = m / (1.0 - ADAM_B1 ** ADAM_STEP)
    v_hat = v / (1.0 - ADAM_B2 ** ADAM_STEP)
    return -ADAM_LR * (m_hat / (jnp.sqrt(v_hat) + ADAM_EPS) + ADAM_WD * w), m, v


def _rows_tile(R, C):
    return _pick(R, max(8, (2 << 20) // (4 * C) // 8 * 8), 8)


def chip_partial(gs, buf, cidx, name):
    _, L, R, C = gs.shape
    H, tr = L // 2, _rows_tile(R, C)

    def body(c_ref, g_ref, b_ref, o_ref):
        o_ref[...] = (g_ref[...] + b_ref[...]).astype(BF16)

    return pl.pallas_call(
        body, grid_spec=pltpu.PrefetchScalarGridSpec(
            num_scalar_prefetch=1, grid=(4, H, R // tr),
            in_specs=[pl.BlockSpec((None, None, tr, C), lambda s, l, r, c: (s, c[0] * H + l, r, 0)),
                      pl.BlockSpec((None, None, tr, C), lambda s, l, r, c: (s, l, r, 0))],
            out_specs=pl.BlockSpec((None, None, tr, C), lambda s, l, r, c: (s, l, r, 0))),
        out_shape=jax.ShapeDtypeStruct((4, H, R, C), BF16),
        compiler_params=_cparams(("parallel", "parallel", "parallel")), name=name)(cidx, gs, buf)


def sum_chips(parts, cidx, name):
    _, H, R, C = parts.shape
    tr = _rows_tile(R, C)

    def body(c_ref, p_ref, g_out):
        g = p_ref[0].astype(F32)
        for s in range(1, 4):
            g = g + p_ref[s].astype(F32)
        g_out[...] = g

    return pl.pallas_call(
        body, grid_spec=pltpu.PrefetchScalarGridSpec(
            num_scalar_prefetch=1, grid=(H, R // tr),
            in_specs=[pl.BlockSpec((4, None, tr, C), lambda l, r, c: (0, l, r, 0))],
            out_specs=pl.BlockSpec((None, tr, C), lambda l, r, c: (c[0] * H + l, r, 0))),
        out_shape=jax.ShapeDtypeStruct((2 * H, R, C), F32),
        compiler_params=_cparams(("parallel", "parallel")), name=name)(cidx, parts)


def sum8(arr, name):
    n = arr.shape[1]
    tr = _pick(n, 512, 8)

    def body(a_ref, o_ref):
        s = a_ref[0]
        for j in range(1, 8):
            s = s + a_ref[j]
        o_ref[...] = s

    return pl.pallas_call(
        body, grid=(n // tr,), in_specs=[pl.BlockSpec((8, tr, 128), lambda i: (0, i, 0))],
        out_specs=pl.BlockSpec((tr, 128), lambda i: (i, 0)), out_shape=jax.ShapeDtypeStruct((n, 128), F32),
        compiler_params=_cparams(("parallel",)), name=name)(arr)


def adam_rows(w, g, m, v, name, emit_g=False):
    n, C = w.shape
    tr = _rows_tile(n, C)

    def body(w_ref, g_ref, m_ref, v_ref, *outs):
        d_out, m_out, v_out = outs[-3:]
        d_out[...], m_out[...], v_out[...] = _adamw(w_ref[...], g_ref[...], m_ref[...], v_ref[...])
        if emit_g:
            outs[0][...] = g_ref[...]

    spec = pl.BlockSpec((tr, C), lambda i: (i, 0))
    n_out = 4 if emit_g else 3
    return pl.pallas_call(
        body, grid=(n // tr,), in_specs=[spec] * 4, out_specs=[spec] * n_out,
        out_shape=[jax.ShapeDtypeStruct((n, C), F32)] * n_out, compiler_params=_cparams(("parallel",)), name=name)(w, g, m, v)


def _silu(c):
    return c * jax.nn.sigmoid(c)


def ada_fwd(c_all, w, b, name):
    NC, D = c_all.shape
    L, _, Wc = w.shape
    tn = _pick(Wc, 512, 128)

    def body(c_ref, w_ref, b_ref, o_ref):
        o_ref[...] = bdot(_silu(c_ref[...]), w_ref[...]) + b_ref[...]

    return pl.pallas_call(
        body, grid=(L, Wc // tn),
        in_specs=[pl.BlockSpec((NC, D), lambda l, j: (0, 0)), pl.BlockSpec((None, D, tn), lambda l, j: (l, 0, j)),
                  pl.BlockSpec((None, 1, tn), lambda l, j: (l, 0, j))],
        out_specs=pl.BlockSpec((None, NC, tn), lambda l, j: (l, 0, j)), out_shape=jax.ShapeDtypeStruct((L, NC, Wc), F32),
        compiler_params=_cparams(("parallel", "parallel")), name=name)(c_all, w, b)


def ada_bwd(c_all, w, dmod, name):
    NC, D = c_all.shape
    L, _, Wc = w.shape
    tn = _pick(Wc, 512, 128)

    def body(c_ref, w_ref, d_ref, dw_ref, db_ref, dc_ref):
        first = jnp.logical_and(pl.program_id(0) == 0, pl.program_id(1) == 0)
        f = lambda cs, ww: bdot(cs, ww)
        _, vjp = jax.vjp(f, _silu(c_ref[...]), w_ref[...])
        dcs, dw = vjp(d_ref[...])
        dw_ref[...] = dw
        db_ref[...] = jnp.sum(d_ref[...], axis=0, keepdims=True)
        _acc(dc_ref, dcs, first)

    return pl.pallas_call(
        body, grid=(L, Wc // tn),
        in_specs=[pl.BlockSpec((NC, D), lambda l, j: (0, 0)), pl.BlockSpec((None, D, tn), lambda l, j: (l, 0, j)),
                  pl.BlockSpec((None, NC, tn), lambda l, j: (l, 0, j))],
        out_specs=[pl.BlockSpec((None, D, tn), lambda l, j: (l, 0, j)), pl.BlockSpec((None, 1, tn), lambda l, j: (l, 0, j)),
                   pl.BlockSpec((NC, D), lambda l, j: (0, 0))],
        out_shape=[jax.ShapeDtypeStruct((L, D, Wc), F32), jax.ShapeDtypeStruct((L, 1, Wc), F32),
                   jax.ShapeDtypeStruct((NC, D), F32)],
        compiler_params=_cparams(("arbitrary", "arbitrary")), name=name)(c_all, w, dmod)


def cctx_grad(dcs_twice, c_ctx, name):
    def body(d_ref, c_ref, o_ref):
        _, vjp = jax.vjp(_silu, c_ref[...])
        o_ref[...] = vjp(0.5 * d_ref[...])[0]

    D = c_ctx.shape[1]
    return pl.pallas_call(body, out_shape=jax.ShapeDtypeStruct((8, D), F32), name=name)(dcs_twice, c_ctx)


def _rope_tables(SEQ):
    t = jnp.arange(SEQ)
    row, col = (t // GRID_W).astype(F32), (t % GRID_W).astype(F32)
    n_freq = 16
    freqs = ROPE_THETA ** (-jnp.arange(n_freq, dtype=F32) / n_freq)
    ang = jnp.concatenate([row[:, None] * freqs, col[:, None] * freqs], axis=-1)
    cos, sin = jnp.cos(ang), jnp.sin(ang)
    c = jnp.tile(jnp.concatenate([cos, cos], axis=1), (1, 4))
    s = jnp.tile(jnp.concatenate([-sin, sin], axis=1), (1, 4))
    return (jnp.concatenate([c, jnp.ones((T, 256), F32)], axis=0), jnp.concatenate([s, jnp.zeros((T, 256), F32)], axis=0))


def _unshard_cols(g):
    return jnp.transpose(g, (1, 2, 0, 3)).reshape(g.shape[1], g.shape[2], 4 * g.shape[3])


def _unshard_rows(g):
    return jnp.transpose(g, (1, 0, 2, 3)).reshape(g.shape[1], 4 * g.shape[2], g.shape[3])


def _shard_cols(w, n=4):
    L, R, C = w.shape
    return jnp.transpose(w.reshape(L, R, n, C // n), (2, 0, 1, 3))


def _shard_rows(w):
    L, R, C = w.shape
    return jnp.transpose(w.reshape(L, 4, R // 4, C), (1, 0, 2, 3))


def _ab_in_permute(w):
    L, D, _ = w.shape
    return jnp.concatenate([w[..., 0:256], w[..., 320:1600], w[..., 256:320], jnp.zeros((L, D, 64), w.dtype)], axis=-1)


def _ab_in_unpermute(g):
    return jnp.concatenate([g[..., 0:256], g[..., 1536:1600], g[..., 256:1536]], axis=-1)


def _split_heads(w, a):
    L, K, N = w.shape
    w4 = w.reshape(L, K, 4, N // 4)
    return w4[..., :a].reshape(L, K, 4 * a), w4[..., a:].reshape(L, K, N - 4 * a)


def _join_heads(p, q):
    L, K = p.shape[:2]
    return jnp.concatenate([p.reshape(L, K, 4, -1), q.reshape(L, K, 4, -1)], axis=-1).reshape(L, K, -1)


def _pack(arrs):
    parts = []
    for a in arrs:
        f = a.reshape(-1).astype(F32)
        parts.append(jnp.pad(f, (0, (-f.shape[0]) % 1024)))
    return jnp.concatenate(parts).reshape(-1, 128)


def _unpack(buf, like):
    out, r0 = [], 0
    for a in like:
        n = math.prod(a.shape)
        rows = (n + (-n) % 1024) // 128
        out.append(buf[r0:r0 + rows].reshape(-1)[:n].reshape(a.shape))
        r0 += rows
    return out


_SMALL = ("c_ctx", "ada_b", "norm_mix", "norm_ffn", "norm_final", "mla_q_norm", "mla_kv_norm", "cmlp_v_norm", "cmlp_ws",
          "cmlp_bs", "ret_decay_fwd", "ret_decay_bwd", "ret_norm", "swa_sink")
_BIG = ("ffn_in", "ffn_out", "ab_in", "ab_out", "mla_wq_b", "mla_wkv_b", "cd_in", "cd_out")
_WEIGHTS = ("c_ctx", "ada_w", "ada_b", "norm_mix", "norm_ffn", "norm_final", "ffn_in", "ffn_out", "ab_in", "ab_out",
            "mla_q_norm", "mla_kv_norm", "mla_wq_b", "mla_wkv_b", "cmlp_v_norm", "cmlp_ws", "cmlp_bs", "cd_in", "cd_out",
            "ret_decay_fwd", "ret_decay_bwd", "ret_norm", "swa_sink")


def kernel(x, c, ctx, c_ctx, ada_w, ada_b, norm_mix, norm_ffn, norm_final, ffn_in, ffn_out, ab_in, ab_out, mla_q_norm, mla_kv_norm, mla_wq_b, mla_wkv_b, cmlp_v_norm, cmlp_ws, cmlp_bs, cd_in, cd_out, ret_decay_fwd, ret_decay_bwd, ret_norm, swa_sink, loss_target, m_c_ctx, m_ada_w, m_ada_b, m_norm_mix, m_norm_ffn, m_norm_final, m_ffn_in, m_ffn_out, m_ab_in, m_ab_out, m_mla_q_norm, m_mla_kv_norm, m_mla_wq_b, m_mla_wkv_b, m_cmlp_v_norm, m_cmlp_ws, m_cmlp_bs, m_cd_in, m_cd_out, m_ret_decay_fwd, m_ret_decay_bwd, m_ret_norm, m_swa_sink, v_c_ctx, v_ada_w, v_ada_b, v_norm_mix, v_norm_ffn, v_norm_final, v_ffn_in, v_ffn_out, v_ab_in, v_ab_out, v_mla_q_norm, v_mla_kv_norm, v_mla_wq_b, v_mla_wkv_b, v_cmlp_v_norm, v_cmlp_ws, v_cmlp_bs, v_cd_in, v_cd_out, v_ret_decay_fwd, v_ret_decay_bwd, v_ret_norm, v_swa_sink):
    W = dict(c_ctx=c_ctx, ada_w=ada_w, ada_b=ada_b, norm_mix=norm_mix, norm_ffn=norm_ffn, norm_final=norm_final, ffn_in=ffn_in, ffn_out=ffn_out, ab_in=ab_in, ab_out=ab_out, mla_q_norm=mla_q_norm, mla_kv_norm=mla_kv_norm, mla_wq_b=mla_wq_b, mla_wkv_b=mla_wkv_b, cmlp_v_norm=cmlp_v_norm, cmlp_ws=cmlp_ws, cmlp_bs=cmlp_bs, cd_in=cd_in, cd_out=cd_out, ret_decay_fwd=ret_decay_fwd, ret_decay_bwd=ret_decay_bwd, ret_norm=ret_norm, swa_sink=swa_sink)
    M1 = dict(c_ctx=m_c_ctx, ada_w=m_ada_w, ada_b=m_ada_b, norm_mix=m_norm_mix, norm_ffn=m_norm_ffn, norm_final=m_norm_final, ffn_in=m_ffn_in, ffn_out=m_ffn_out, ab_in=m_ab_in, ab_out=m_ab_out, mla_q_norm=m_mla_q_norm, mla_kv_norm=m_mla_kv_norm, mla_wq_b=m_mla_wq_b, mla_wkv_b=m_mla_wkv_b, cmlp_v_norm=m_cmlp_v_norm, cmlp_ws=m_cmlp_ws, cmlp_bs=m_cmlp_bs, cd_in=m_cd_in, cd_out=m_cd_out, ret_decay_fwd=m_ret_decay_fwd, ret_decay_bwd=m_ret_decay_bwd, ret_norm=m_ret_norm, swa_sink=m_swa_sink)
    M2 = dict(c_ctx=v_c_ctx, ada_w=v_ada_w, ada_b=v_ada_b, norm_mix=v_norm_mix, norm_ffn=v_norm_ffn, norm_final=v_norm_final, ffn_in=v_ffn_in, ffn_out=v_ffn_out, ab_in=v_ab_in, ab_out=v_ab_out, mla_q_norm=v_mla_q_norm, mla_kv_norm=v_mla_kv_norm, mla_wq_b=v_mla_wq_b, mla_wkv_b=v_mla_wkv_b, cmlp_v_norm=v_cmlp_v_norm, cmlp_ws=v_cmlp_ws, cmlp_bs=v_cmlp_bs, cd_in=v_cd_in, cd_out=v_cd_out, ret_decay_fwd=v_ret_decay_fwd, ret_decay_bwd=v_ret_decay_bwd, ret_norm=v_ret_norm, swa_sink=v_swa_sink)

    B, SEQ, D = x.shape
    CTX = ctx.shape[1]
    lay = Layout(B, SEQ, CTX, D)
    nt, NX = lay.nt, lay.NX
    ix, iy, ic = lax.axis_index("x"), lax.axis_index("y"), lax.axis_index("c")
    chip, me = 2 * ix + iy, 4 * ix + 2 * iy + ic
    cidx = jnp.reshape(ic, (1,)).astype(jnp.int32)
    Wc = ada_w.shape[2]
    n_even, n_odd = ab_in.shape[0], cd_in.shape[0]

    rn_row = jnp.pad(ret_norm.reshape(1, -1), ((0, 0), (0, D - ret_norm.size)))
    pack0 = jnp.concatenate([c, rn_row, jnp.zeros((8 - (B + 1) % 8, D), F32)], axis=0) if (B + 1) % 8 else jnp.concatenate([c, rn_row], axis=0)
    g0 = gather8(pack0, "gather_cond")
    NC = -(-(8 * B + 1) // 16) * 16
    c_all = jnp.concatenate([g0[:, :B].reshape(8 * B, D), c_ctx[None], jnp.zeros((NC - 8 * B - 1, D), F32)], axis=0)
    rn_sh = ret_norm.shape[1]
    ret_norm_full = jnp.transpose(g0[0::2, B, :ret_norm.size].reshape(4, n_odd, rn_sh), (1, 0, 2)).reshape(n_odd, 4 * rn_sh)

    gw = gather_weights([W[n].astype(BF16) for n in _BIG], "gather_weights")
    w_ab_in, w_wq, w_wkv, w_cd_in = (_unshard_cols(gw[i]) for i in (2, 4, 5, 6))
    w_ffn_out, w_ab_out, w_cd_out = (_unshard_rows(gw[i]) for i in (1, 3, 7))
    w_ab_in = _ab_in_permute(w_ab_in)
    w_ffn_out_t = jnp.transpose(w_ffn_out, (0, 2, 1))
    w_qn, w_qp = _split_heads(w_wq, MLA_NOPE)
    w_k, w_v = _split_heads(w_wkv, MLA_NOPE)

    ab_sh = lax.dynamic_slice_in_dim(ada_b, chip * Wc, Wc, axis=1)[:, None, :]
    mod_sh = ada_fwd(c_all, ada_w, ab_sh, "ada_fwd")
    mod_all = _unshard_cols(gather_chips(mod_sh, "gather_mod"))
    mod_mine = jnp.concatenate([lax.dynamic_slice_in_dim(mod_all, me * B, B, axis=1), mod_all[:, 8 * B:8 * B + 1]], axis=1)
    mod = mod_mine.reshape(DEPTH, B + 1, 6, D)

    tabc, tabs = _rope_tables(SEQ)
    bc8 = lambda a: jnp.broadcast_to(a.reshape(a.shape + (1, 1)), a.shape + (8, 128))
    row = lambda a: a.reshape(1, -1)

    S = jnp.concatenate([x.reshape(NX, D), ctx.reshape(B * CTX, D)], axis=0)
    saved = []
    xn = norm_mod_fwd(S, row(norm_mix[0]), mod[0], 0, lay, nt, "norm_mix_fwd0")
    for l in range(DEPTH):
        j, even = l // 2, l % 2 == 0
        if even:
            z = mm(xn, w_ab_in[j], name=f"ab_in{l}")
            q, k, v = ab_prep_fwd(z, tabc, tabs, row(mla_kv_norm[j]), row(mla_q_norm[j]), w_k[j], w_v[j], w_qn[j], w_qp[j],
                                  lay, f"ab_prep{l}")
            o = mla_fwd(q, k, v, lay, f"mla{l}")
            merged = cmlp_merge_fwd(z, o, row(cmlp_v_norm[j]), cmlp_ws[j], cmlp_bs[j][:, :, None], lay, f"cmlp{l}")
            w_out, mix = w_ab_out[j], (q, k, v)
        else:
            z = mm(xn, w_cd_in[j], name=f"cd_in{l}")
            rq, rk, sq, ke, ve = cd_prep_fwd(z, tabc, tabs, lay, f"cd_prep{l}")
            decf, decb, sink = bc8(ret_decay_fwd[j]), bc8(ret_decay_bwd[j]), bc8(swa_sink[j].reshape(SWA_KV_HEADS, SWA_GROUPS))
            yret = ret_fwd(rq, rk, z, decf, decb, lay, f"ret{l}")
            osw = swa_fwd(sq, ke, ve, sink, lay, f"swa{l}")
            merged = cd_merge_fwd(yret, z, osw, row(ret_norm_full[j]), lay, f"cd_merge{l}")
            w_out, mix = w_cd_out[j], (rq, rk, sq, ke, ve, decf, decb, sink, yret)
        y, S_mid, xn2 = mm_gated(merged, w_out, S, mod[l], 2, lay, name=f"mix_out{l}", n_tiles=nt,
                                 norm=(row(norm_ffn[l]), mod[l], 3))
        act, fa, fb = ffn_in_act(xn2, gw[0], l, f"ffn_in{l}")
        nxt_norm = (row(norm_mix[l + 1]), mod[l + 1], 0) if l + 1 < DEPTH else None
        f, S_new, *xn_next = mm_gated(act, w_ffn_out[l], S_mid, mod[l], 5, lay, name=f"ffn_out{l}", n_tiles=nt, norm=nxt_norm)
        saved.append((S, xn, z, mix, merged, w_out, y, S_mid, xn2, (fa, fb), act, f))
        S, xn = S_new, (xn_next[0] if xn_next else None)

    loss_blk, dS, d_norm_final = loss_head(S, row(norm_final), loss_target.reshape(NX, D), lay, "loss_head")
    loss = lax.psum(loss_blk[0, 0], ("x", "y", "c"))

    G = {n: [None] * W[n].shape[0] for n in ("norm_mix", "norm_ffn", "mla_q_norm", "mla_kv_norm", "cmlp_v_norm", "cmlp_ws",
                                             "cmlp_bs", "ret_decay_fwd", "ret_decay_bwd", "ret_norm", "swa_sink")}
    GB = {n: [None] * cnt for n, cnt in (("ffn_in_a", DEPTH), ("ffn_in_b", DEPTH), ("ffn_out", DEPTH), ("ab_in", n_even), ("ab_out", n_even),
                                         ("wqn", n_even), ("wqp", n_even), ("wk", n_even), ("wv", n_even), ("cd_in", n_odd),
                                         ("cd_out", n_odd))}
    dmod = [None] * DEPTH
    df, dgate2 = gate_bwd(dS, saved[-1][-1], mod[DEPTH - 1], 5, lay, nt, f"ffn_gate_bwd{DEPTH - 1}")
    for l in reversed(range(DEPTH)):
        j, even = l // 2, l % 2 == 0
        S_in, xn, z, mix, merged, w_out, y, S_mid, xn2, (fa, fb), act, f = saved[l]
        da, db = ffn_out_dx_act(df, w_ffn_out_t[l], fa, fb, f"ffn_out_dx{l}")
        GB["ffn_out"][l] = mm(act, df, ta=True, name=f"ffn_out_dw{l}")
        GB["ffn_in_a"][l] = mm(xn2, da, ta=True, name=f"ffn_in_dwa{l}", split=2)
        GB["ffn_in_b"][l] = mm(xn2, db, ta=True, name=f"ffn_in_dwb{l}", split=2)
        dxn2 = ffn_in_dx(da, db, gw[0], l, f"ffn_in_dx{l}")
        dS_mid, dss2, dg, dy, dgate1 = norm_mod_bwd(S_mid, row(norm_ffn[l]), mod[l], 3, dxn2, dS, lay, nt, f"norm_ffn_bwd{l}",
                                                    gate=(y, mod[l], 2))
        G["norm_ffn"][l] = dg
        dmerged = mm(dy, w_out, tb=True, name=f"mix_out_dx{l}")
        d_w_out = mm(merged, dy, ta=True, name=f"mix_out_dw{l}")
        if even:
            q, k, v = mix
            dq, dkx, dkh, dvx, dvh = mla_bwd(q, k, v, dmerged, lay, f"mla_bwd{l}")
            (dz, dgkv, dgq, dwk, dwv, dwqn, dwqp, dgvn, dws, dbs) = ab_rows_bwd(
                z, tabc, tabs, dq, dkx, dkh, dvx, dvh, dmerged, row(mla_kv_norm[j]), row(mla_q_norm[j]), w_k[j], w_v[j],
                w_qn[j], w_qp[j], row(cmlp_v_norm[j]), cmlp_ws[j], cmlp_bs[j][:, :, None], lay, f"ab_rows_bwd{l}")
            G["mla_kv_norm"][j], G["mla_q_norm"][j], G["cmlp_v_norm"][j] = dgkv, dgq, dgvn
            G["cmlp_ws"][j], G["cmlp_bs"][j] = dws, dbs
            GB["wk"][j], GB["wv"][j], GB["wqn"][j], GB["wqp"][j], GB["ab_out"][j] = dwk, dwv, dwqn, dwqp, d_w_out
            w_in = w_ab_in[j]
        else:
            rq, rk, sq, ke, ve, decf, decb, sink, yret = mix
            dyret, drg, dgn = cd_merge_bwd(yret, z, row(ret_norm_full[j]), dmerged, lay, f"cd_merge_bwd{l}")
            dqx, dqh, dkx, dkh, dvx, dvh, ddf, ddb = ret_bwd(rq, rk, z, decf, decb, dyret, lay, f"ret_bwd{l}")
            dsq, dkex, dkeh, dvex, dveh, dsink = swa_bwd(sq, ke, ve, sink, dmerged, lay, f"swa_bwd{l}")
            dz = cd_rows_bwd(z, tabc, tabs, dsq, drg, (dqx, dqh), (dkx, dkh), (dkex, dkeh), (dvex, dveh), (dvx, dvh), lay,
                             f"cd_rows_bwd{l}")
            G["ret_norm"][j], G["ret_decay_fwd"][j], G["ret_decay_bwd"][j] = dgn, ddf[:, 0, 0], ddb[:, 0, 0]
            G["swa_sink"][j] = dsink[:, :, 0, 0].reshape(-1)
            GB["cd_out"][j] = d_w_out
            w_in = w_cd_in[j]
        GB["ab_in" if even else "cd_in"][j] = mm(xn, dz, ta=True, name=f"mix_in_dw{l}")
        dxn = mm(dz, w_in, tb=True, name=f"mix_in_dx{l}")
        dmod_l = lambda dss1: jnp.concatenate([dss1, dgate1, dss2, dgate2], axis=1)
        if l > 0:
            dS, dss1, dg, df, dgate2_prev = norm_mod_bwd(S_in, row(norm_mix[l]), mod[l], 0, dxn, dS_mid, lay, nt,
                                                         f"norm_mix_bwd{l}", gate=(saved[l - 1][-1], mod[l - 1], 5))
            dmod[l], dgate2 = dmod_l(dss1), dgate2_prev
        else:
            dS, dss1, dg = norm_mod_bwd(S_in, row(norm_mix[l]), mod[l], 0, dxn, dS_mid, lay, nt, f"norm_mix_bwd{l}")
            dmod[l] = dmod_l(dss1)
        G["norm_mix"][l] = dg
    grad_x = dS[:NX].reshape(B, SEQ, D)

    st = lambda n: jnp.stack([g.reshape((4 * rn_sh,) if n == "ret_norm" else W[n].shape[1:]) for g in G[n]])
    small_parts = {n: st(n) for n in G}
    small_parts["norm_final"] = d_norm_final.reshape(-1)
    dmod_local = jnp.stack(dmod).reshape(DEPTH, B + 1, 6 * D)
    names1 = ["norm_mix", "norm_ffn", "norm_final", "mla_q_norm", "mla_kv_norm", "cmlp_v_norm", "cmlp_ws", "cmlp_bs",
              "ret_decay_fwd", "ret_decay_bwd", "ret_norm", "swa_sink"]
    like1 = [dmod_local] + [small_parts[n] for n in names1]
    g1 = gather8(_pack(like1), "gather_small_grads")
    tot1 = _unpack(sum8(g1, "sum_small_grads"), like1)
    sg = dict(zip(names1, tot1[1:]))
    n_dm = math.prod(dmod_local.shape)
    dm_each = g1[:, :-(-n_dm // 128)].reshape(8, -1)[:, :n_dm].reshape(8, DEPTH, B + 1, 6 * D)
    dmod_all = jnp.concatenate([jnp.transpose(dm_each[:, :, :B], (1, 0, 2, 3)).reshape(DEPTH, 8 * B, 6 * D),
                                tot1[0][:, B:B + 1], jnp.zeros((DEPTH, NC - 8 * B - 1, 6 * D), F32)], axis=1)
    dmod_sh = lax.dynamic_slice_in_dim(dmod_all, chip * Wc, Wc, axis=2)
    g_ada_w, g_ada_b_sh, dcs = ada_bwd(c_all, ada_w, dmod_sh, "ada_bwd")
    like2 = [dcs[8 * B], g_ada_b_sh]
    g2 = gather8(_pack(like2), "gather_ada_grads")
    tot2 = _unpack(sum8(g2, "sum_ada_grads"), like2)
    bc = lambda a: jnp.broadcast_to(a.reshape(1, D), (8, D))
    sg["c_ctx"] = cctx_grad(bc(tot2[0]), bc(c_ctx), "c_ctx_grad")[0]
    off = D + (-D) % 1024
    gab = g2[0::2, off // 128:(off + DEPTH * Wc) // 128].reshape(4, DEPTH, Wc)
    sg["ada_b"] = jnp.transpose(gab, (1, 0, 2)).reshape(DEPTH, 4 * Wc)
    sg["ret_norm"] = lax.dynamic_slice_in_dim(sg["ret_norm"], chip * rn_sh, rn_sh, axis=1)

    stack = lambda n: jnp.stack(GB[n])
    gs = {"ffn_in": jnp.concatenate([jnp.stack(GB[n], axis=1) for n in ("ffn_in_a", "ffn_in_b")], axis=0),
          "ffn_out": _shard_rows(stack("ffn_out")),
          "ab_in": _shard_cols(_ab_in_unpermute(stack("ab_in"))), "ab_out": _shard_rows(stack("ab_out")),
          "mla_wq_b": _shard_cols(_join_heads(stack("wqn"), stack("wqp"))),
          "mla_wkv_b": _shard_cols(_join_heads(stack("wk"), stack("wv"))),
          "cd_in": _shard_cols(stack("cd_in")), "cd_out": _shard_rows(stack("cd_out"))}
    bufs = swap_other_half([gs[n] for n in _BIG], "swap_core_halves")
    parts = [chip_partial(gs[n], b, cidx, f"chip_partial_{n}") for n, b in zip(_BIG, bufs)]
    arrived = exchange_chips(parts, "exchange_chips")
    grads = share_halves([sum_chips(p, cidx, f"sum_chips_{n}") for n, p in zip(_BIG, arrived)], "share_core_halves")
    flat2 = lambda a: a.reshape(-1, a.shape[-1])
    out = {}
    for n, g in zip(_BIG, grads):
        res = adam_rows(flat2(W[n]), flat2(g), flat2(M1[n]), flat2(M2[n]), f"adam_{n}", emit_g=True)
        out[n] = tuple(r.reshape(W[n].shape) for r in res)

    like_s = [W[n] for n in _SMALL]
    dsm, msm, vsm = adam_rows(_pack(like_s), _pack([sg[n].reshape(W[n].shape) for n in _SMALL]), _pack([M1[n] for n in _SMALL]),
                                _pack([M2[n] for n in _SMALL]), "adam_small")
    for n, d_, m_, v_ in zip(_SMALL, _unpack(dsm, like_s), _unpack(msm, like_s), _unpack(vsm, like_s)):
        out[n] = (sg[n].reshape(W[n].shape), d_, m_, v_)
    d_, m_, v_ = adam_rows(flat2(ada_w), flat2(g_ada_w), flat2(m_ada_w), flat2(v_ada_w), "adam_ada_w")
    out["ada_w"] = (g_ada_w, d_.reshape(ada_w.shape), m_.reshape(ada_w.shape), v_.reshape(ada_w.shape))

    return (loss, grad_x, *[out[n][0] for n in _WEIGHTS], *[out[n][1] for n in _WEIGHTS], *[out[n][2] for n in _WEIGHTS],
            *[out[n][3] for n in _WEIGHTS])
```

```python
import functools
import math

import jax
import jax.numpy as jnp
import numpy as np
from jax import lax
from jax.experimental import pallas as pl
from jax.experimental.pallas import tpu as pltpu

F32 = jnp.float32
BF16 = jnp.bfloat16
EPS = 1e-6
NEG_INF = -1e30
GRID_W = 64
ROPE_THETA = 10000.0
DEPTH = 4
MLA_HEADS, MLA_Q_LORA, MLA_KV_LORA, MLA_NOPE, MLA_ROPE, MLA_V = 4, 256, 256, 128, 64, 128
CMLP_GROUPS, CMLP_CHUNK = 4, 128
CMLP_WIDTH = 512
RET_HEADS, RET_QK, RET_V = 4, 64, 128
SWA_Q_HEADS, SWA_KV_HEADS, SWA_HEAD_DIM, SWA_WINDOW = 8, 2, 64, 128
SWA_GROUPS = SWA_Q_HEADS // SWA_KV_HEADS
AB_IN_P = 1664
ADAM_LR, ADAM_B1, ADAM_B2, ADAM_EPS, ADAM_WD, ADAM_STEP = 0.001, 0.9, 0.999, 1e-08, 0.01, 10

T = 256
SWA_SPAN = T + 2 * SWA_WINDOW
VMEM_LIMIT = 48 * 1024 * 1024
MESH = pl.DeviceIdType.MESH
ANY = pl.BlockSpec(memory_space=pl.ANY)


def _cparams(sem):
    return pltpu.CompilerParams(dimension_semantics=sem, vmem_limit_bytes=VMEM_LIMIT)


@functools.cache
def _bdot_fn(ca, cb):
    fa, fb = 1 - ca, 1 - cb

    def dg(p, q, cp, cq):
        return lax.dot_general(p.astype(BF16), q.astype(BF16), (((cp,), (cq,)), ((), ())), preferred_element_type=F32)

    @jax.custom_vjp
    def bd(a, b):
        return dg(a, b, ca, cb)

    def fwd(a, b):
        return dg(a, b, ca, cb), (a, b)

    def bwd(res, g):
        a, b = res
        da = dg(g, b, 1, fb) if ca == 1 else dg(b, g, fb, 1)
        db = dg(a, g, fa, 0) if cb == 0 else dg(g, a, 0, fa)
        return da, db

    bd.defvjp(fwd, bwd)
    return bd


def bdot(a, b):
    return _bdot_fn(1, 0)(a, b)


def bdot_nt(a, b):
    return _bdot_fn(1, 1)(a, b)


def bdot_tn(a, b):
    return _bdot_fn(0, 0)(a, b)


def _swap32(x):
    w = x.shape[-1]
    lane = lax.broadcasted_iota(jnp.int32, x.shape, 1)
    return jnp.where((lane & 32) == 0, pltpu.roll(x, w - 32, 1), pltpu.roll(x, 32, 1))


@jax.custom_vjp
def rope(x, c, s):
    return x * c + _swap32(x) * s


def _rope_fwd(x, c, s):
    return rope(x, c, s), (c, s)


def _rope_bwd(res, g):
    c, s = res
    return g * c + _swap32(g * s), jnp.zeros_like(c), jnp.zeros_like(s)


rope.defvjp(_rope_fwd, _rope_bwd)


def rms(x, g):
    return x * lax.rsqrt(jnp.mean(x * x, axis=-1, keepdims=True) + EPS) * g


def normmod(x, g, sh, sc):
    return rms(x, g) * (1.0 + sc) + sh


def log_sigmoid(x):
    return jnp.minimum(x, 0.0) - jnp.log(1.0 + jnp.exp(-jnp.abs(x)))


def _head_mask(shape, h):
    lane = lax.broadcasted_iota(jnp.int32, shape, 1)
    return ((lane >> 6) == h).astype(F32)


def _fold_matrix():
    i = lax.broadcasted_iota(jnp.int32, (256, 128), 0)
    j = lax.broadcasted_iota(jnp.int32, (256, 128), 1)
    return ((i & 63) == j).astype(F32)


def _expand_matrix(g):
    i = lax.broadcasted_iota(jnp.int32, (128, 256), 0)
    j = lax.broadcasted_iota(jnp.int32, (128, 256), 1)
    return (i == (j & 63) + 64 * g).astype(F32)


def _acc(ref, val, first):
    @pl.when(first)
    def _():
        ref[...] = jnp.zeros(ref.shape, ref.dtype)

    ref[...] += val


def _pick(n, cap, mult):
    best = None
    for d in range(mult, min(n, cap) + 1, mult):
        if n % d == 0:
            best = d
    return best if best is not None else n


def mm(a, b, *, ta=False, tb=False, name, split=1):
    M, K = (a.shape[1], a.shape[0]) if ta else a.shape
    N = b.shape[0] if tb else b.shape[1]
    Ns = N // split
    assert split == 1 or (ta and Ns % 128 == 0)
    if ta:
        tn = N
        tm = _pick(M, min(1536, (3 << 20) // tn), 128)
    else:
        tn = _pick(N, 768, 128)
        if tn < 256 and N <= 2304:
            tn = N
        tm = _pick(M, min(1536, (1 << 20) // tn), 128)
    tk = _pick(K, 2048 if not ta else 1024, 128) if K > 2816 or ta else K
    nk = K // tk
    grid = (M // tm, N // tn, nk)
    a_spec = pl.BlockSpec((tk, tm), lambda i, j, k: (k, i)) if ta else pl.BlockSpec((tm, tk), lambda i, j, k: (i, k))
    b_spec = pl.BlockSpec((tn, tk), lambda i, j, k: (j, k)) if tb else pl.BlockSpec((tk, tn), lambda i, j, k: (k, j))
    dims = (((0 if ta else 1,), (1 if tb else 0,)), ((), ()))

    def body(a_ref, b_ref, o_ref):
        part = lax.dot_general(a_ref[...], b_ref[...], dims, preferred_element_type=F32)
        first = pl.program_id(2) == 0
        if split > 1:
            for s in range(split):
                _acc(o_ref.at[s], part[:, s * Ns:(s + 1) * Ns], first)
        elif nk == 1:
            o_ref[...] = part
        else:
            _acc(o_ref, part, first)

    if split > 1:
        out_spec, out_shape = pl.BlockSpec((split, tm, Ns), lambda i, j, k: (0, i, 0)), (split, M, Ns)
    else:
        out_spec, out_shape = pl.BlockSpec((tm, tn), lambda i, j, k: (i, j)), (M, N)
    return pl.pallas_call(
        body, grid=grid, in_specs=[a_spec, b_spec], out_specs=out_spec, out_shape=jax.ShapeDtypeStruct(out_shape, F32),
        compiler_params=_cparams(("parallel", "parallel", "arbitrary")), name=name)(a, b)


def mm_gated(a, b, res, mod, gate_row, lay, *, name, n_tiles, norm=None):
    K, N = b.shape
    M = n_tiles * T

    def body(*refs):
        a_ref, b_ref, r_ref, g_ref = refs[:4]
        y = lax.dot_general(a_ref[...], b_ref[...], (((1,), (0,)), ((), ())), preferred_element_type=F32)
        new = r_ref[...] + g_ref[gate_row:gate_row + 1, :] * y
        if norm is None:
            y_ref, o_ref = refs[4:]
        else:
            gn_ref, mn_ref, y_ref, o_ref, xn_ref = refs[4:]
            k0 = norm[2]
            xn_ref[...] = normmod(new, gn_ref[...], mn_ref[k0:k0 + 1, :], mn_ref[k0 + 1:k0 + 2, :]).astype(BF16)
        y_ref[...] = y.astype(BF16)
        o_ref[...] = new

    rows = pl.BlockSpec((T, N), lambda i: (i, 0))
    extra = [] if norm is None else [_full((1, N)), _modspec(lay, 6, N)]
    return pl.pallas_call(
        body, grid=(n_tiles,),
        in_specs=[pl.BlockSpec((T, K), lambda i: (i, 0)), _full((K, N)), rows, _modspec(lay, 6, N)] + extra,
        out_specs=[rows, rows] + ([] if norm is None else [rows]),
        out_shape=[jax.ShapeDtypeStruct((M, N), BF16), jax.ShapeDtypeStruct((M, N), F32)]
        + ([] if norm is None else [jax.ShapeDtypeStruct((M, N), BF16)]),
        compiler_params=_cparams(("parallel",)), name=name)(a, b, res, mod, *([] if norm is None else norm[:2]))


class Layout:
    def __init__(self, B, SEQ, CTX, D):
        assert CTX == T and SEQ % T == 0 and SEQ >= SWA_SPAN
        self.B, self.SEQ, self.CTX, self.D = B, SEQ, CTX, D
        self.tps = SEQ // T
        self.nxt = B * self.tps
        self.nt = self.nxt + B
        self.NX, self.R = B * SEQ, B * SEQ + B * CTX
        self.nq = self.tps + 1

    def mod_idx(self, i):
        return jnp.where(i < self.nxt, i // self.tps, self.B)

    def rope_idx(self, i):
        return jnp.where(i < self.nxt, i % self.tps, self.tps)

    def first_of_mod(self, i):
        return jnp.logical_or(jnp.logical_and(i < self.nxt, i % self.tps == 0), i == self.nxt)

    def qrow(self, b, qi):
        return jnp.where(qi < self.tps, b * self.tps + qi, self.nxt + b)


def _row(w, col=0):
    return pl.BlockSpec((T, w), lambda i: (i, col))


def _full(shape):
    nd = len(shape)
    return pl.BlockSpec(shape, lambda i: (0,) * nd)


def _modspec(lay, rows, D):
    return pl.BlockSpec((None, rows, D), lambda i: (lay.mod_idx(i), 0, 0))


def _ropespec(lay, w):
    return pl.BlockSpec((T, w), lambda i: (lay.rope_idx(i), 0))


def _xh_specs(lay, w):
    nxt = lay.nxt
    return [pl.BlockSpec((T, w), lambda i: (jnp.minimum(i, nxt - 1), 0)), pl.BlockSpec((T, w), lambda i: (jnp.maximum(i - nxt, 0), 0))]


def _xh_pick(lay, x_ref, h_ref):
    return jnp.where(pl.program_id(0) < lay.nxt, x_ref[...], h_ref[...])


def norm_mod_fwd(S, g, mod, k0, lay, n_tiles, name):
    D = S.shape[1]

    def body(s_ref, g_ref, mod_ref, o_ref):
        o_ref[...] = normmod(s_ref[...], g_ref[...], mod_ref[k0:k0 + 1, :], mod_ref[k0 + 1:k0 + 2, :]).astype(BF16)

    return pl.pallas_call(
        body, grid=(n_tiles,), in_specs=[_row(D), _full((1, D)), _modspec(lay, 6, D)], out_specs=_row(D),
        out_shape=jax.ShapeDtypeStruct((n_tiles * T, D), BF16), compiler_params=_cparams(("parallel",)), name=name)(S, g, mod)


def norm_mod_bwd(S, g, mod, k0, dxn, ds_in, lay, n_tiles, name, gate=None):
    D = S.shape[1]

    def body(*refs):
        s_ref, g_ref, mod_ref, dxn_ref, dsin_ref = refs[:5]
        i = pl.program_id(0)
        _, vjp = jax.vjp(normmod, s_ref[...], g_ref[...], mod_ref[k0:k0 + 1, :], mod_ref[k0 + 1:k0 + 2, :])
        dx, dg, dsh, dsc = vjp(dxn_ref[...])
        ds = dsin_ref[...] + dx
        if gate is None:
            ds_ref, dss_ref, dg_ref = refs[5:]
        else:
            y_ref, gmod_ref, ds_ref, dss_ref, dg_ref, dy_ref, dgate_ref = refs[5:]
            row = gate[2]
            dy_ref[...] = (gmod_ref[row:row + 1, :] * ds).astype(BF16)
            _acc(dgate_ref, jnp.sum(ds * y_ref[...], axis=0, keepdims=True), lay.first_of_mod(i))
        ds_ref[...] = ds
        _acc(dg_ref, dg, i == 0)
        _acc(dss_ref, jnp.concatenate([dsh, dsc], axis=0), lay.first_of_mod(i))

    R_ = n_tiles * T
    gated = gate is not None
    return pl.pallas_call(
        body, grid=(n_tiles,),
        in_specs=[_row(D), _full((1, D)), _modspec(lay, 6, D), _row(D), _row(D)] + ([_row(D), _modspec(lay, 6, D)] if gated else []),
        out_specs=[_row(D), _modspec(lay, 2, D), _full((1, D))] + ([_row(D), _modspec(lay, 1, D)] if gated else []),
        out_shape=[jax.ShapeDtypeStruct((R_, D), F32), jax.ShapeDtypeStruct((lay.B + 1, 2, D), F32),
                   jax.ShapeDtypeStruct((1, D), F32)]
        + ([jax.ShapeDtypeStruct((R_, D), BF16), jax.ShapeDtypeStruct((lay.B + 1, 1, D), F32)] if gated else []),
        compiler_params=_cparams(("arbitrary",)), name=name)(S, g, mod, dxn, ds_in, *(gate[:2] if gated else []))


def gate_bwd(dS, y, mod, gate_row, lay, n_tiles, name):
    D = dS.shape[1]

    def body(ds_ref, y_ref, mod_ref, dy_ref, dgate_ref):
        i = pl.program_id(0)
        ds = ds_ref[...]
        dy_ref[...] = (mod_ref[gate_row:gate_row + 1, :] * ds).astype(BF16)
        _acc(dgate_ref, jnp.sum(ds * y_ref[...], axis=0, keepdims=True), lay.first_of_mod(i))

    return pl.pallas_call(
        body, grid=(n_tiles,), in_specs=[_row(D), _row(D), _modspec(lay, 6, D)],
        out_specs=[_row(D), _modspec(lay, 1, D)],
        out_shape=[jax.ShapeDtypeStruct((n_tiles * T, D), BF16), jax.ShapeDtypeStruct((lay.B + 1, 1, D), F32)],
        compiler_params=_cparams(("arbitrary",)), name=name)(dS, y, mod)


def _swiglu(a, b):
    return a * jax.nn.sigmoid(a) * b


def _ffn_tiles(M, F):
    tn = _pick(F, 1408, 128)
    return _pick(M, (3 << 18) // tn, 128), tn


def ffn_in_act(x, ws, l, name):
    M, D = x.shape
    Cs = ws.shape[3]
    tm = _pick(M, (3 << 18) // Cs, 128)

    def body(x_ref, wa_ref, wb_ref, act_ref, a_ref, b_ref):
        dims = (((1,), (0,)), ((), ()))
        a = lax.dot_general(x_ref[...], wa_ref[...], dims, preferred_element_type=F32)
        b = lax.dot_general(x_ref[...], wb_ref[...], dims, preferred_element_type=F32)
        act_ref[...] = _swiglu(a, b).astype(BF16)
        a_ref[...] = a.astype(BF16)
        b_ref[...] = b.astype(BF16)

    out = pl.BlockSpec((tm, Cs), lambda j, i: (i, j))
    return pl.pallas_call(
        body, grid=(2, M // tm),
        in_specs=[pl.BlockSpec((tm, D), lambda j, i: (i, 0)), pl.BlockSpec((None, None, D, Cs), lambda j, i: (j, l, 0, 0)),
                  pl.BlockSpec((None, None, D, Cs), lambda j, i: (2 + j, l, 0, 0))],
        out_specs=[out, out, out], out_shape=[jax.ShapeDtypeStruct((M, 2 * Cs), BF16)] * 3,
        compiler_params=_cparams(("parallel", "parallel")), name=name)(x, ws, ws)


def ffn_out_dx_act(df, wt, a, b, name):
    M, D = df.shape
    F = wt.shape[1]
    tm, tn = _ffn_tiles(M, F)

    def body(df_ref, w_ref, a_ref, b_ref, da_ref, db_ref):
        dact = lax.dot_general(df_ref[...], w_ref[...], (((1,), (0,)), ((), ())), preferred_element_type=F32)
        _, vjp = jax.vjp(_swiglu, a_ref[...].astype(F32), b_ref[...].astype(F32))
        da, db = vjp(dact)
        da_ref[...] = da.astype(BF16)
        db_ref[...] = db.astype(BF16)

    blk = pl.BlockSpec((tm, tn), lambda j, i: (i, j))
    return pl.pallas_call(
        body, grid=(F // tn, M // tm),
        in_specs=[pl.BlockSpec((tm, D), lambda j, i: (i, 0)), pl.BlockSpec((D, tn), lambda j, i: (0, j)), blk, blk],
        out_specs=[blk, blk], out_shape=[jax.ShapeDtypeStruct((M, F), BF16)] * 2,
        compiler_params=_cparams(("parallel", "parallel")), name=name)(df, wt, a, b)


def ffn_in_dx(da, db, ws, l, name):
    M, F = da.shape
    D, Cs = ws.shape[2], ws.shape[3]
    tm, tn = _pick(M, 512, 128), _pick(D, 1024, 128)

    def body(da_ref, db_ref, w0_ref, w1_ref, w2_ref, w3_ref, o_ref):
        dims = (((1,), (1,)), ((), ()))
        dot = lambda g_ref, s, w_ref: lax.dot_general(g_ref[:, s * Cs:(s + 1) * Cs], w_ref[...], dims, preferred_element_type=F32)
        o_ref[...] = (dot(da_ref, 0, w0_ref) + dot(da_ref, 1, w1_ref)) + (dot(db_ref, 0, w2_ref) + dot(db_ref, 1, w3_ref))

    shard = lambda s: pl.BlockSpec((None, None, tn, Cs), lambda i, j: (s, l, j, 0))
    return pl.pallas_call(
        body, grid=(M // tm, D // tn),
        in_specs=[pl.BlockSpec((tm, F), lambda i, j: (i, 0)), pl.BlockSpec((tm, F), lambda i, j: (i, 0))]
        + [shard(s) for s in range(4)],
        out_specs=pl.BlockSpec((tm, tn), lambda i, j: (i, j)), out_shape=jax.ShapeDtypeStruct((M, D), F32),
        compiler_params=_cparams(("parallel", "parallel")), name=name)(da, db, ws, ws, ws, ws)


def loss_head(S, g, target, lay, name):
    D = S.shape[1]
    nxt = lay.nxt

    def tile_loss(x, gg, t):
        err = rms(x, gg) - t
        return 0.5 * jnp.sum(jnp.mean(err * err, axis=-1))

    def body(s_ref, g_ref, t_ref, loss_ref, ds_ref, dg_ref):
        i = pl.program_id(0)

        @pl.when(i < nxt)
        def _():
            val, vjp = jax.vjp(tile_loss, s_ref[...], g_ref[...], t_ref[...])
            dx, dg, _ = vjp(jnp.ones((), F32))
            ds_ref[...] = dx
            _acc(dg_ref, dg, i == 0)
            _acc(loss_ref, jnp.full((8, 128), val, F32), i == 0)

        @pl.when(i >= nxt)
        def _():
            ds_ref[...] = jnp.zeros((T, D), F32)

    return pl.pallas_call(
        body, grid=(lay.nt,),
        in_specs=[_row(D), _full((1, D)), pl.BlockSpec((T, D), lambda i: (jnp.minimum(i, nxt - 1), 0))],
        out_specs=[_full((8, 128)), _row(D), _full((1, D))],
        out_shape=[jax.ShapeDtypeStruct((8, 128), F32), jax.ShapeDtypeStruct((lay.R, D), F32),
                   jax.ShapeDtypeStruct((1, D), F32)],
        compiler_params=_cparams(("arbitrary",)), name=name)(S, g, target)


def _ab_prep(zkv, zq, zpe, c256, s256, c128, s128, gkv, gq, wk, wv, wqn, wqp):
    kvn = rms(zkv, gkv)
    kn, v = bdot(kvn, wk), bdot(kvn, wv)
    qn = rms(zq, gq)
    qnope, qpe = bdot(qn, wqn) * _MLA_SCALE, rope(bdot(qn, wqp), c256, s256) * _MLA_SCALE
    kpe = rope(zpe, c128, s128)
    fold = _fold_matrix()
    qparts, kparts = [], []
    for h in range(MLA_HEADS):
        qparts += [qnope[:, 128 * h:128 * (h + 1)], bdot(qpe * _head_mask(qpe.shape, h), fold)]
        kparts += [kn[:, 128 * h:128 * (h + 1)], kpe]
    return jnp.concatenate(qparts, axis=1), jnp.concatenate(kparts, axis=1), v


def _ab_prep_specs(lay):
    return [_row(256, 0), _row(256, 1), _row(128, 12), _ropespec(lay, 256), _ropespec(lay, 256), _ropespec(lay, 128),
            _ropespec(lay, 128), _full((1, 256)), _full((1, 256)), _full((256, 512)), _full((256, 512)),
            _full((256, 512)), _full((256, 256))]


def ab_prep_fwd(z, tabc, tabs, gkv, gq, wk, wv, wqn, wqp, lay, name):
    def body(*refs):
        ins, (q_ref, k_ref, v_ref) = refs[:13], refs[13:]
        q, k, v = _ab_prep(*[r[...].astype(F32) for r in ins])
        q_ref[...] = q.astype(BF16)
        k_ref[...] = k.astype(BF16)
        v_ref[...] = v.astype(BF16)

    R = lay.R
    return pl.pallas_call(
        body, grid=(lay.nt,), in_specs=_ab_prep_specs(lay), out_specs=[_row(1024), _row(1024), _row(512)],
        out_shape=[jax.ShapeDtypeStruct((R, 1024), BF16), jax.ShapeDtypeStruct((R, 1024), BF16),
                   jax.ShapeDtypeStruct((R, 512), BF16)],
        compiler_params=_cparams(("parallel",)), name=name)(z, z, z, tabc, tabs, tabc, tabs, gkv, gq, wk, wv, wqn, wqp)


_MLA_SCALE = (MLA_NOPE + MLA_ROPE) ** -0.5


def _mla_x(q, kx, vx, kh, vh):
    sx, sh = bdot_nt(q, kx), bdot_nt(q, kh)
    m = lax.stop_gradient(jnp.maximum(jnp.max(sx, axis=-1, keepdims=True), jnp.max(sh, axis=-1, keepdims=True)))
    ex, eh = jnp.exp(sx - m), jnp.exp(sh - m)
    inv = 1.0 / (jnp.sum(ex, axis=-1, keepdims=True) + jnp.sum(eh, axis=-1, keepdims=True))
    return (bdot(ex, vx) + bdot(eh, vh)) * inv


def _mla_h(q, kh, vh):
    sh = bdot_nt(q, kh)
    eh = jnp.exp(sh - lax.stop_gradient(jnp.max(sh, axis=-1, keepdims=True)))
    return bdot(eh, vh) * (1.0 / jnp.sum(eh, axis=-1, keepdims=True))


def _mla_specs(lay):
    nxt, SEQ = lay.nxt, lay.SEQ
    return [pl.BlockSpec((T, 256), lambda b, h, qi: (lay.qrow(b, qi), h)),
            pl.BlockSpec((SEQ, 256), lambda b, h, qi: (b, h)), pl.BlockSpec((SEQ, 128), lambda b, h, qi: (b, h)),
            pl.BlockSpec((T, 256), lambda b, h, qi: (nxt + b, h)), pl.BlockSpec((T, 128), lambda b, h, qi: (nxt + b, h))]


def mla_fwd(q, k, v, lay, name):
    tps = lay.tps

    def body(q_ref, kx_ref, vx_ref, kh_ref, vh_ref, o_ref):
        qi = pl.program_id(2)
        f = lambda r: r[...]

        @pl.when(qi < tps)
        def _():
            o_ref[...] = _mla_x(f(q_ref), f(kx_ref), f(vx_ref), f(kh_ref), f(vh_ref)).astype(BF16)

        @pl.when(qi == tps)
        def _():
            o_ref[...] = _mla_h(f(q_ref), f(kh_ref), f(vh_ref)).astype(BF16)

    return pl.pallas_call(
        body, grid=(lay.B, MLA_HEADS, lay.nq), in_specs=_mla_specs(lay),
        out_specs=pl.BlockSpec((T, 128), lambda b, h, qi: (lay.qrow(b, qi), h)),
        out_shape=jax.ShapeDtypeStruct((lay.R, 512), BF16),
        compiler_params=_cparams(("parallel", "parallel", "arbitrary")), name=name)(q, k, v, k, v)


def mla_bwd(q, k, v, dmerged, lay, name):
    tps, SEQ, B = lay.tps, lay.SEQ, lay.B

    def body(q_ref, kx_ref, vx_ref, kh_ref, vh_ref, do_ref, dq_ref, dkx_ref, dkh_ref, dvx_ref, dvh_ref):
        qi = pl.program_id(2)
        f = lambda r: r[...].astype(F32)

        @pl.when(qi < tps)
        def _():
            _, vjp = jax.vjp(_mla_x, f(q_ref), f(kx_ref), f(vx_ref), f(kh_ref), f(vh_ref))
            dq, dkx, dvx, dkh, dvh = vjp(do_ref[...])
            dq_ref[...] = dq
            _acc(dkx_ref, dkx, qi == 0)
            _acc(dvx_ref, dvx, qi == 0)
            _acc(dkh_ref, dkh, qi == 0)
            _acc(dvh_ref, dvh, qi == 0)

        @pl.when(qi == tps)
        def _():
            _, vjp = jax.vjp(_mla_h, f(q_ref), f(kh_ref), f(vh_ref))
            dq, dkh, dvh = vjp(do_ref[...])
            dq_ref[...] = dq
            dkh_ref[...] += dkh
            dvh_ref[...] += dvh

    return pl.pallas_call(
        body, grid=(B, MLA_HEADS, lay.nq),
        in_specs=_mla_specs(lay) + [pl.BlockSpec((T, 128), lambda b, h, qi: (lay.qrow(b, qi), h))],
        out_specs=[pl.BlockSpec((T, 256), lambda b, h, qi: (lay.qrow(b, qi), h)),
                   pl.BlockSpec((SEQ, 256), lambda b, h, qi: (b, h)), pl.BlockSpec((T, 256), lambda b, h, qi: (b, h)),
                   pl.BlockSpec((SEQ, 128), lambda b, h, qi: (b, h)), pl.BlockSpec((T, 128), lambda b, h, qi: (b, h))],
        out_shape=[jax.ShapeDtypeStruct((lay.R, 1024), F32), jax.ShapeDtypeStruct((lay.NX, 1024), F32),
                   jax.ShapeDtypeStruct((B * T, 1024), F32), jax.ShapeDtypeStruct((lay.NX, 512), F32),
                   jax.ShapeDtypeStruct((B * T, 512), F32)],
        compiler_params=_cparams(("parallel", "parallel", "arbitrary")), name=name)(q, k, v, k, v, dmerged)


def _cmlp_piece(zu, zv, g, ws, bs):
    u, v = jax.nn.gelu(zu), jax.nn.gelu(zv)
    v = v * lax.rsqrt(jnp.mean(v * v, axis=-1, keepdims=True) + EPS) * g
    return u * (bdot(ws, v) + bs)


def _pieces():
    return [(c, g) for c in range(T // CMLP_CHUNK) for g in range(CMLP_GROUPS)]


def cmlp_merge_fwd(z, o, gvn, ws, bs, lay, name):
    def body(zu_ref, zv_ref, o_ref, g_ref, ws_ref, bs_ref, m_ref):
        m_ref[:, 0:512] = o_ref[...]
        for c, g in _pieces():
            rows, cols = slice(128 * c, 128 * (c + 1)), slice(128 * g, 128 * (g + 1))
            piece = _cmlp_piece(zu_ref[rows, cols], zv_ref[rows, cols], g_ref[:, cols], ws_ref[g], bs_ref[g])
            m_ref[rows, 512 + 128 * g:512 + 128 * (g + 1)] = piece.astype(BF16)

    return pl.pallas_call(
        body, grid=(lay.nt,),
        in_specs=[_row(512, 1), _row(512, 2), _row(512), _full((1, 512)), _full((4, 128, 128)), _full((4, 128, 1))],
        out_specs=_row(1024), out_shape=jax.ShapeDtypeStruct((lay.R, 1024), BF16),
        compiler_params=_cparams(("parallel",)), name=name)(z, z, o, gvn, ws, bs)


def ab_rows_bwd(z, tabc, tabs, dq, dkx, dkh, dvx, dvh, dmerged, gkv, gq, wk, wv, wqn, wqp, gvn, ws, bs, lay, name):
    def body(*refs):
        prep_in = refs[:3] + refs[5:9] + refs[11:17]
        zu_ref, zv_ref = refs[3:5]
        dq_ref, dcm_ref = refs[9:11]
        gvn_ref, ws_ref, bs_ref = refs[17:20]
        dkx_ref, dkh_ref, dvx_ref, dvh_ref = refs[20:24]
        dz_ref, dgkv_ref, dgq_ref, dwk_ref, dwv_ref, dwqn_ref, dwqp_ref, dgvn_ref, dws_ref, dbs_ref = refs[24:]
        first = pl.program_id(0) == 0
        _, vjp = jax.vjp(_ab_prep, *[r[...].astype(F32) for r in prep_in])
        d = vjp((dq_ref[...], _xh_pick(lay, dkx_ref, dkh_ref), _xh_pick(lay, dvx_ref, dvh_ref)))
        dz_ref[:, 0:256] = d[0].astype(BF16)
        dz_ref[:, 256:512] = d[1].astype(BF16)
        dz_ref[:, 1536:1664] = d[2].astype(BF16)
        for ref, val in zip((dgkv_ref, dgq_ref, dwk_ref, dwv_ref, dwqn_ref, dwqp_ref), d[7:]):
            _acc(ref, val, first)
        dws = [0.0] * CMLP_GROUPS
        dbs = [0.0] * CMLP_GROUPS
        dgv = [0.0] * CMLP_GROUPS
        for c, g in _pieces():
            rows, cols = slice(128 * c, 128 * (c + 1)), slice(128 * g, 128 * (g + 1))
            _, vjp = jax.vjp(_cmlp_piece, zu_ref[rows, cols], zv_ref[rows, cols], gvn_ref[:, cols], ws_ref[g], bs_ref[g])
            dzu, dzv, dg_, dws_, dbs_ = vjp(dcm_ref[rows, cols])
            dz_ref[rows, 512 + 128 * g:512 + 128 * (g + 1)] = dzu.astype(BF16)
            dz_ref[rows, 1024 + 128 * g:1024 + 128 * (g + 1)] = dzv.astype(BF16)
            dws[g], dbs[g], dgv[g] = dws[g] + dws_, dbs[g] + dbs_, dgv[g] + dg_
        _acc(dgvn_ref, jnp.concatenate(dgv, axis=1), first)
        _acc(dws_ref, jnp.stack(dws), first)
        _acc(dbs_ref, jnp.stack(dbs), first)

    acc_shapes = [(1, 256), (1, 256), (256, 512), (256, 512), (256, 512), (256, 256), (1, 512), (4, 128, 128), (4, 128, 1)]
    return pl.pallas_call(
        body, grid=(lay.nt,),
        in_specs=_ab_prep_specs(lay)[:3] + [_row(512, 1), _row(512, 2)] + _ab_prep_specs(lay)[3:7]
        + [_row(1024), _row(512, 1)] + _ab_prep_specs(lay)[7:]
        + [_full((1, 512)), _full((4, 128, 128)), _full((4, 128, 1))] + _xh_specs(lay, 1024) + _xh_specs(lay, 512),
        out_specs=[_row(AB_IN_P)] + [_full(s) for s in acc_shapes],
        out_shape=[jax.ShapeDtypeStruct((lay.R, AB_IN_P), BF16)] + [jax.ShapeDtypeStruct(s, F32) for s in acc_shapes],
        compiler_params=_cparams(("arbitrary",)), name=name)(
            z, z, z, z, z, tabc, tabs, tabc, tabs, dq, dmerged, gkv, gq, wk, wv, wqn, wqp, gvn, ws, bs, dkx, dkh, dvx, dvh)


def _cd_prep(zrk, zrq, zsk, zsq0, zsq1, zsv, c256, s256, c128, s128):
    rk = rope(zrk * (RET_QK ** -0.5), c256, s256)
    rq = rope(zrq, c256, s256)
    sk = rope(zsk, c128, s128)
    sq0, sq1 = rope(zsq0 * _SWA_SCALE, c256, s256), rope(zsq1 * _SWA_SCALE, c256, s256)
    e0, e1 = _expand_matrix(0), _expand_matrix(1)
    return rq, rk, sq0, sq1, bdot(sk, e0), bdot(sk, e1), bdot(zsv, e0), bdot(zsv, e1)


def _cd_prep_specs(lay):
    return [_row(256, 0), _row(256, 4), _row(128, 6), _row(256, 7), _row(256, 8), _row(128, 7),
            _ropespec(lay, 256), _ropespec(lay, 256), _ropespec(lay, 128), _ropespec(lay, 128)]


def cd_prep_fwd(z, tabc, tabs, lay, name):
    def body(*refs):
        ins, (rq_ref, rk_ref, sq_ref, ke_ref, ve_ref) = refs[:10], refs[10:]
        rq, rk, sq0, sq1, k0, k1, v0, v1 = _cd_prep(*[r[...] for r in ins])
        rq_ref[...] = rq.astype(BF16)
        rk_ref[...] = rk.astype(BF16)
        for ref, (a, b) in ((sq_ref, (sq0, sq1)), (ke_ref, (k0, k1)), (ve_ref, (v0, v1))):
            ref[:, 0:256] = a.astype(BF16)
            ref[:, 256:512] = b.astype(BF16)

    R = lay.R
    return pl.pallas_call(
        body, grid=(lay.nt,), in_specs=_cd_prep_specs(lay),
        out_specs=[_row(256), _row(256), _row(512), _row(512), _row(512)],
        out_shape=[jax.ShapeDtypeStruct((R, w), BF16) for w in (256, 256, 512, 512, 512)],
        compiler_params=_cparams(("parallel",)), name=name)(z, z, z, z, z, z, tabc, tabs, tabc, tabs)


def _ret_sample(h, qs, ks, vs, df, db):
    lgf, lgb = log_sigmoid(df), log_sigmoid(db)
    idx = lax.broadcasted_iota(jnp.int32, (T, 1), 0).astype(F32)
    diff = idx - lax.broadcasted_iota(jnp.int32, (1, T), 1).astype(F32)
    intra = (jnp.where(diff >= 0, jnp.exp(lgf * jnp.maximum(diff, 0.0)), 0.0)
             + jnp.where(diff <= 0, jnp.exp(lgb * jnp.maximum(-diff, 0.0)), 0.0))
    qdf, kdf, cdf = jnp.exp(lgf * (idx + 1.0)), jnp.exp(lgf * (T - 1.0 - idx)), jnp.exp(lgf * T)
    qdb, kdb, cdb = jnp.exp(lgb * (T - idx)), jnp.exp(lgb * idx), jnp.exp(lgb * T)
    mask = _head_mask(qs[0].shape, h)
    qs = [q * mask for q in qs]
    ys = [bdot(bdot_nt(q, k) * intra, v) for q, k, v in zip(qs, ks, vs)]
    n = len(qs)
    state = bdot_tn(ks[0] * kdf, vs[0])
    for i in range(1, n):
        ys[i] = ys[i] + bdot(qs[i] * qdf, state)
        if i + 1 < n:
            state = state * cdf + bdot_tn(ks[i] * kdf, vs[i])
    state = bdot_tn(ks[0] * kdb, vs[0])
    for i in range(n - 1, 0, -1):
        ys[i] = ys[i] + bdot(qs[i] * qdb, state)
        if i > 1:
            state = state * cdb + bdot_tn(ks[i] * kdb, vs[i])
    return ys


def _ret_specs(lay):
    nxt, SEQ = lay.nxt, lay.SEQ
    xs = lambda w, col: pl.BlockSpec((SEQ, w), lambda b, h: (b, col(h)))
    hs = lambda w, col: pl.BlockSpec((T, w), lambda b, h: (nxt + b, col(h)))
    zero, head = (lambda h: 0), (lambda h: 2 + h)
    dec = pl.BlockSpec((None, 8, 128), lambda b, h: (h, 0, 0))
    return [xs(256, zero), hs(256, zero), xs(256, zero), hs(256, zero), xs(128, head), hs(128, head), dec, dec]


def _ret_tiles(x_ref, h_ref, tps, cast=None):
    tiles = [h_ref[...]] + [x_ref[i * T:(i + 1) * T, :] for i in range(tps)]
    return [t.astype(cast) for t in tiles] if cast is not None else tiles


def ret_fwd(rq, rk, z, decf, decb, lay, name):
    tps, SEQ = lay.tps, lay.SEQ

    def body(qx_ref, qh_ref, kx_ref, kh_ref, vx_ref, vh_ref, df_ref, db_ref, yx_ref, yh_ref):
        ys = _ret_sample(pl.program_id(1), _ret_tiles(qx_ref, qh_ref, tps), _ret_tiles(kx_ref, kh_ref, tps),
                         _ret_tiles(vx_ref, vh_ref, tps), df_ref[0:1, 0:1], db_ref[0:1, 0:1])
        yh_ref[...] = ys[0]
        for i in range(tps):
            yx_ref[i * T:(i + 1) * T, :] = ys[i + 1]

    return pl.pallas_call(
        body, grid=(lay.B, RET_HEADS), in_specs=_ret_specs(lay),
        out_specs=[pl.BlockSpec((SEQ, 128), lambda b, h: (b, h)), pl.BlockSpec((T, 128), lambda b, h: (b, h))],
        out_shape=[jax.ShapeDtypeStruct((lay.NX, 512), F32), jax.ShapeDtypeStruct((lay.B * T, 512), F32)],
        compiler_params=_cparams(("parallel", "arbitrary")), name=name)(rq, rq, rk, rk, z, z, decf, decb)


def ret_bwd(rq, rk, z, decf, decb, dy, lay, name):
    tps, SEQ, B = lay.tps, lay.SEQ, lay.B
    nxt = lay.nxt

    def body(qx_ref, qh_ref, kx_ref, kh_ref, vx_ref, vh_ref, df_ref, db_ref, dyx_ref, dyh_ref,
             dqx_ref, dqh_ref, dkx_ref, dkh_ref, dvx_ref, dvh_ref, ddf_ref, ddb_ref):
        h = pl.program_id(1)
        _, vjp = jax.vjp(functools.partial(_ret_sample, h), _ret_tiles(qx_ref, qh_ref, tps, F32),
                         _ret_tiles(kx_ref, kh_ref, tps, F32), _ret_tiles(vx_ref, vh_ref, tps), df_ref[0:1, 0:1],
                         db_ref[0:1, 0:1])
        dqs, dks, dvs, ddf, ddb = vjp(_ret_tiles(dyx_ref, dyh_ref, tps))
        first = h == 0
        _acc(dqh_ref, dqs[0], first)
        _acc(dkh_ref, dks[0], first)
        dvh_ref[...] = dvs[0]
        for i in range(tps):
            rows = slice(i * T, (i + 1) * T)
            _acc(dqx_ref.at[rows], dqs[i + 1], first)
            _acc(dkx_ref.at[rows], dks[i + 1], first)
            dvx_ref[rows, :] = dvs[i + 1]
        @pl.when(jnp.logical_and(pl.program_id(0) == 0, first))
        def _():
            ddf_ref[...] = jnp.zeros(ddf_ref.shape, F32)
            ddb_ref[...] = jnp.zeros(ddb_ref.shape, F32)

        ddf_ref[h] += jnp.broadcast_to(ddf, (8, 128))
        ddb_ref[h] += jnp.broadcast_to(ddb, (8, 128))

    acc_x, acc_h = pl.BlockSpec((SEQ, 256), lambda b, h: (b, 0)), pl.BlockSpec((T, 256), lambda b, h: (b, 0))
    head_x, head_h = pl.BlockSpec((SEQ, 128), lambda b, h: (b, h)), pl.BlockSpec((T, 128), lambda b, h: (b, h))
    dec = pl.BlockSpec((RET_HEADS, 8, 128), lambda b, h: (0, 0, 0))
    return pl.pallas_call(
        body, grid=(B, RET_HEADS),
        in_specs=_ret_specs(lay) + [head_x, pl.BlockSpec((T, 128), lambda b, h: (nxt + b, h))],
        out_specs=[acc_x, acc_h, acc_x, acc_h, head_x, head_h, dec, dec],
        out_shape=[jax.ShapeDtypeStruct((lay.NX, 256), F32), jax.ShapeDtypeStruct((B * T, 256), F32),
                   jax.ShapeDtypeStruct((lay.NX, 256), F32), jax.ShapeDtypeStruct((B * T, 256), F32),
                   jax.ShapeDtypeStruct((lay.NX, 512), F32), jax.ShapeDtypeStruct((B * T, 512), F32),
                   jax.ShapeDtypeStruct((RET_HEADS, 8, 128), F32), jax.ShapeDtypeStruct((RET_HEADS, 8, 128), F32)],
        compiler_params=_cparams(("arbitrary", "arbitrary")), name=name)(rq, rq, rk, rk, z, z, decf, decb, dy, dy)


_SWA_SCALE = SWA_HEAD_DIM ** -0.5


def _swa_head(qh, sw, kh, vw, vh, sink):
    sh = bdot_nt(qh, kh)
    m = jnp.maximum(jnp.max(sh, axis=-1, keepdims=True), sink)
    if sw is not None:
        m = jnp.maximum(m, jnp.max(sw, axis=-1, keepdims=True))
    m = lax.stop_gradient(m)
    eh, es = jnp.exp(sh - m), jnp.exp(sink - m)
    tot = jnp.sum(eh, axis=-1, keepdims=True) + es
    if sw is None:
        return bdot(eh, vh) * (1.0 / tot)
    ew = jnp.exp(sw - m)
    return (bdot(ew, vw) + bdot(eh, vh)) * (1.0 / (tot + jnp.sum(ew, axis=-1, keepdims=True)))


def _swa_x(t0, kpos0, sq, kw, vw, kh, vh, *sinks):
    t = t0 + lax.broadcasted_iota(jnp.int32, (T, 1), 0)
    pos = kpos0 + lax.broadcasted_iota(jnp.int32, (1, SWA_SPAN), 1)
    band = jnp.abs(t - pos) <= SWA_WINDOW
    out = 0.0
    for i in range(SWA_GROUPS):
        mi = _head_mask(sq.shape, i)
        qh = sq * mi
        sw = jnp.where(band, bdot_nt(qh, kw), NEG_INF)
        out = out + _swa_head(qh, sw, kh, vw, vh, sinks[i]) * mi
    return out


def _swa_h(sq, kh, vh, *sinks):
    out = 0.0
    for i in range(SWA_GROUPS):
        mi = _head_mask(sq.shape, i)
        out = out + _swa_head(sq * mi, None, kh, None, vh, sinks[i]) * mi
    return out


def _swa_specs(lay):
    nxt, SEQ = lay.nxt, lay.SEQ
    return [pl.BlockSpec((T, 256), lambda g, b, qi: (lay.qrow(b, qi), g)),
            pl.BlockSpec((SEQ, 256), lambda g, b, qi: (b, g)), pl.BlockSpec((SEQ, 256), lambda g, b, qi: (b, g)),
            pl.BlockSpec((T, 256), lambda g, b, qi: (nxt + b, g)), pl.BlockSpec((T, 256), lambda g, b, qi: (nxt + b, g)),
            pl.BlockSpec((None, 4, 8, 128), lambda g, b, qi: (g, 0, 0, 0))]


def _swa_start(qi, SEQ):
    return pl.multiple_of(jnp.clip(qi * T - SWA_WINDOW, 0, SEQ - SWA_SPAN), SWA_WINDOW)


def swa_fwd(sq, kexp, vexp, sink, lay, name):
    tps, SEQ = lay.tps, lay.SEQ

    def body(sq_ref, kx_ref, vx_ref, kh_ref, vh_ref, sink_ref, o_ref):
        qi = pl.program_id(2)
        f = lambda r: r[...]
        sinks = [sink_ref[i][0:1, 0:1] for i in range(SWA_GROUPS)]

        @pl.when(qi < tps)
        def _():
            k0 = _swa_start(qi, SEQ)
            kw, vw = kx_ref[pl.ds(k0, SWA_SPAN), :], vx_ref[pl.ds(k0, SWA_SPAN), :]
            o_ref[...] = _swa_x(qi * T, k0, f(sq_ref), kw, vw, f(kh_ref), f(vh_ref), *sinks).astype(BF16)

        @pl.when(qi == tps)
        def _():
            o_ref[...] = _swa_h(f(sq_ref), f(kh_ref), f(vh_ref), *sinks).astype(BF16)

    return pl.pallas_call(
        body, grid=(SWA_KV_HEADS, lay.B, lay.nq), in_specs=_swa_specs(lay),
        out_specs=pl.BlockSpec((T, 256), lambda g, b, qi: (lay.qrow(b, qi), g)),
        out_shape=jax.ShapeDtypeStruct((lay.R, 512), BF16),
        compiler_params=_cparams(("parallel", "parallel", "arbitrary")), name=name)(sq, kexp, vexp, kexp, vexp, sink)


def swa_bwd(sq, kexp, vexp, sink, dmerged, lay, name):
    tps, SEQ, B = lay.tps, lay.SEQ, lay.B

    def body(sq_ref, kx_ref, vx_ref, kh_ref, vh_ref, sink_ref, do_ref, dsq_ref, dkx_ref, dkh_ref, dvx_ref, dvh_ref, dsink_ref):
        b, qi = pl.program_id(1), pl.program_id(2)
        f = lambda r: r[...].astype(F32)
        sinks = [sink_ref[i][0:1, 0:1] for i in range(SWA_GROUPS)]
        very_first = jnp.logical_and(b == 0, qi == 0)

        def acc_sink(ds):
            for i in range(SWA_GROUPS):
                _acc(dsink_ref.at[i], jnp.broadcast_to(ds[i], (8, 128)), very_first)

        @pl.when(qi == 0)
        def _():
            for ref in (dkx_ref, dkh_ref, dvx_ref, dvh_ref):
                ref[...] = jnp.zeros(ref.shape, F32)

        @pl.when(qi < tps)
        def _():
            k0 = _swa_start(qi, SEQ)
            win = pl.ds(k0, SWA_SPAN)
            kw, vw = kx_ref[win, :].astype(F32), vx_ref[win, :].astype(F32)
            _, vjp = jax.vjp(functools.partial(_swa_x, qi * T, k0), f(sq_ref), kw, vw, f(kh_ref), f(vh_ref), *sinks)
            d = vjp(do_ref[...])
            dsq_ref[...] = d[0]
            dkx_ref[win, :] += d[1]
            dvx_ref[win, :] += d[2]
            dkh_ref[...] += d[3]
            dvh_ref[...] += d[4]
            acc_sink(d[5:9])

        @pl.when(qi == tps)
        def _():
            _, vjp = jax.vjp(_swa_h, f(sq_ref), f(kh_ref), f(vh_ref), *sinks)
            d = vjp(do_ref[...])
            dsq_ref[...] = d[0]
            dkh_ref[...] += d[1]
            dvh_ref[...] += d[2]
            acc_sink(d[3:7])

    xs = pl.BlockSpec((SEQ, 256), lambda g, b, qi: (b, g))
    hs = pl.BlockSpec((T, 256), lambda g, b, qi: (b, g))
    return pl.pallas_call(
        body, grid=(SWA_KV_HEADS, B, lay.nq),
        in_specs=_swa_specs(lay) + [pl.BlockSpec((T, 256), lambda g, b, qi: (lay.qrow(b, qi), 2 + g))],
        out_specs=[pl.BlockSpec((T, 256), lambda g, b, qi: (lay.qrow(b, qi), g)), xs, hs, xs, hs,
                   pl.BlockSpec((None, 4, 8, 128), lambda g, b, qi: (g, 0, 0, 0))],
        out_shape=[jax.ShapeDtypeStruct((lay.R, 512), F32), jax.ShapeDtypeStruct((lay.NX, 512), F32),
                   jax.ShapeDtypeStruct((B * T, 512), F32), jax.ShapeDtypeStruct((lay.NX, 512), F32),
                   jax.ShapeDtypeStruct((B * T, 512), F32), jax.ShapeDtypeStruct((SWA_KV_HEADS, 4, 8, 128), F32)],
        compiler_params=_cparams(("arbitrary", "arbitrary", "arbitrary")), name=name)(
            sq, kexp, vexp, kexp, vexp, sink, dmerged)


def _cd_merge_piece(y, rg, g):
    return (y * lax.rsqrt(jnp.mean(y * y, axis=-1, keepdims=True) + EPS) * g) * (rg * jax.nn.sigmoid(rg))


def cd_merge_fwd(y, z, o, gn, lay, name):
    def body(yx_ref, yh_ref, rga_ref, rgb_ref, o_ref, g_ref, m_ref):
        y = _xh_pick(lay, yx_ref, yh_ref)
        for h in range(RET_HEADS):
            cols = slice(128 * h, 128 * (h + 1))
            rg_ref, rcols = (rga_ref, cols) if h < 2 else (rgb_ref, slice(128 * (h - 2), 128 * (h - 1)))
            m_ref[:, cols] = _cd_merge_piece(y[:, cols], rg_ref[:, rcols], g_ref[:, cols]).astype(BF16)
        m_ref[:, 512:1024] = o_ref[...]

    return pl.pallas_call(
        body, grid=(lay.nt,), in_specs=_xh_specs(lay, 512) + [_row(256, 5), _row(256, 6), _row(512), _full((1, 512))],
        out_specs=_row(1024), out_shape=jax.ShapeDtypeStruct((lay.R, 1024), BF16),
        compiler_params=_cparams(("parallel",)), name=name)(*y, z, z, o, gn)


def cd_merge_bwd(y, z, gn, dmerged, lay, name):
    def body(yx_ref, yh_ref, rga_ref, rgb_ref, g_ref, dm_ref, dy_ref, drg_ref, dg_ref):
        first = pl.program_id(0) == 0
        y = _xh_pick(lay, yx_ref, yh_ref)
        dgs = []
        for h in range(RET_HEADS):
            cols = slice(128 * h, 128 * (h + 1))
            rg_ref, rcols = (rga_ref, cols) if h < 2 else (rgb_ref, slice(128 * (h - 2), 128 * (h - 1)))
            _, vjp = jax.vjp(_cd_merge_piece, y[:, cols], rg_ref[:, rcols], g_ref[:, cols])
            dy, drg, dg = vjp(dm_ref[:, cols])
            dy_ref[:, cols] = dy
            drg_ref[:, cols] = drg
            dgs.append(dg)
        _acc(dg_ref, jnp.concatenate(dgs, axis=1), first)

    return pl.pallas_call(
        body, grid=(lay.nt,), in_specs=_xh_specs(lay, 512) + [_row(256, 5), _row(256, 6), _full((1, 512)), _row(512, 0)],
        out_specs=[_row(512), _row(512), _full((1, 512))],
        out_shape=[jax.ShapeDtypeStruct((lay.R, 512), F32), jax.ShapeDtypeStruct((lay.R, 512), F32),
                   jax.ShapeDtypeStruct((1, 512), F32)],
        compiler_params=_cparams(("arbitrary",)), name=name)(*y, z, z, gn, dmerged)


def cd_rows_bwd(z, tabc, tabs, dsq, drg, drq, drk, dke, dve, drv, lay, name):
    def body(*refs):
        ins = refs[:10]
        dsq_ref, drg_ref = refs[10:12]
        drq, drk, dke, dve, drv = (_xh_pick(lay, refs[12 + 2 * n], refs[13 + 2 * n]) for n in range(5))
        dz_ref = refs[22]
        _, vjp = jax.vjp(_cd_prep, *[r[...] for r in ins])
        cts = (drq, drk, dsq_ref[:, 0:256], dsq_ref[:, 256:512], dke[:, 0:256], dke[:, 256:512],
               dve[:, 0:256], dve[:, 256:512])
        dzrk, dzrq, dzsk, dzsq0, dzsq1, dzsv = vjp(cts)[:6]
        dz_ref[:, 0:256] = dzrk.astype(BF16)
        dz_ref[:, 256:768] = drv.astype(BF16)
        dz_ref[:, 768:896] = dzsk.astype(BF16)
        dz_ref[:, 896:1024] = dzsv.astype(BF16)
        dz_ref[:, 1024:1280] = dzrq.astype(BF16)
        dz_ref[:, 1280:1792] = drg_ref[...].astype(BF16)
        dz_ref[:, 1792:2048] = dzsq0.astype(BF16)
        dz_ref[:, 2048:2304] = dzsq1.astype(BF16)

    return pl.pallas_call(
        body, grid=(lay.nt,),
        in_specs=_cd_prep_specs(lay) + [_row(512), _row(512)] + _xh_specs(lay, 256) + _xh_specs(lay, 256)
        + _xh_specs(lay, 512) + _xh_specs(lay, 512) + _xh_specs(lay, 512),
        out_specs=_row(2304), out_shape=jax.ShapeDtypeStruct((lay.R, 2304), BF16),
        compiler_params=_cparams(("parallel",)), name=name)(
            z, z, z, z, z, z, tabc, tabs, tabc, tabs, dsq, drg, *drq, *drk, *dke, *dve, *drv)


def _pos():
    return lax.axis_index("x"), lax.axis_index("y"), lax.axis_index("c")


def _flip(v, bit):
    return 1 - v if bit else v


def _comm_call(name, body, ins, out_shapes, n_remote, n_local, aliases=None):
    return pl.pallas_call(
        body, in_specs=[ANY] * len(ins), out_specs=[ANY] * len(out_shapes), out_shape=out_shapes,
        scratch_shapes=[pltpu.SemaphoreType.DMA((n_remote,)), pltpu.SemaphoreType.DMA((n_remote,)),
                        pltpu.SemaphoreType.DMA((n_local,))],
        input_output_aliases=aliases or {}, name=name)(*ins)


def gather8(arr, name):
    def body(a_ref, o_ref, ssem, rsem, lsem):
        x, y, c = _pos()
        me = 4 * x + 2 * y + c
        loc = pltpu.make_async_copy(a_ref, o_ref.at[me], lsem.at[0])
        loc.start()
        cps = []
        for m in range(1, 8):
            peer = (_flip(x, m & 4), _flip(y, m & 2), _flip(c, m & 1))
            cps.append(pltpu.make_async_remote_copy(a_ref, o_ref.at[me], ssem.at[m - 1], rsem.at[m - 1],
                                                    device_id=peer, device_id_type=MESH))
            cps[-1].start()
        for cp in cps:
            cp.wait()
        loc.wait()

    return _comm_call(name, body, [arr], [jax.ShapeDtypeStruct((8,) + arr.shape, arr.dtype)], 7, 1)[0]


def gather_chips(arr, name):
    def body(a_ref, o_ref, ssem, rsem, lsem):
        x, y, c = _pos()
        k = 2 * x + y
        loc = pltpu.make_async_copy(a_ref, o_ref.at[k], lsem.at[0])
        loc.start()
        cps = []
        for m in range(1, 4):
            peer = (_flip(x, m & 2), _flip(y, m & 1), c)
            cps.append(pltpu.make_async_remote_copy(a_ref, o_ref.at[k], ssem.at[m - 1], rsem.at[m - 1],
                                                    device_id=peer, device_id_type=MESH))
            cps[-1].start()
        for cp in cps:
            cp.wait()
        loc.wait()

    return _comm_call(name, body, [arr], [jax.ShapeDtypeStruct((4,) + arr.shape, arr.dtype)], 3, 1)[0]


def gather_weights(arrs, name):
    n = len(arrs)

    def body(*refs):
        a_refs, o_refs, (isend, irecv, _) = refs[:n], refs[n:2 * n], refs[2 * n:]
        x, y, c = _pos()
        k = 2 * x + y
        sib = (x, y, 1 - c)
        chips = [(m, (_flip(x, m & 2), _flip(y, m & 1)), 2 * _flip(x, m & 2) + _flip(y, m & 1)) for m in range(1, 4)]
        waits = []
        for w, (a, o) in enumerate(zip(a_refs, o_refs)):
            H = a.shape[0] // 2
            own = pl.ds(c * H, H)
            first = [pltpu.make_async_remote_copy(a.at[own], o.at[k, own], isend.at[7 * w + m - 1], irecv.at[7 * w + m - 1],
                                                  device_id=(*chip, c), device_id_type=MESH) for m, chip, _ in chips]
            first.append(pltpu.make_async_remote_copy(a, o.at[k], isend.at[7 * w + 6], irecv.at[7 * w + 6], device_id=sib,
                                                      device_id_type=MESH))
            for cp in first:
                cp.start()
            waits.append((first, H, own, a, o, w))
        for first, H, own, a, o, w in waits:
            first[3].wait_recv()
            passed = []
            for m, chip, kk in chips:
                pltpu.make_async_remote_copy(a.at[own], o.at[kk, own], isend.at[7 * w + m - 1], irecv.at[7 * w + m - 1],
                                             device_id=(*chip, c), device_id_type=MESH).wait_recv()
                fw = pltpu.make_async_remote_copy(o.at[kk, own], o.at[kk, own], isend.at[7 * w + 2 + m], irecv.at[7 * w + 2 + m],
                                                  device_id=sib, device_id_type=MESH)
                fw.start()
                passed.append(fw)
            for fw in passed:
                fw.wait_recv()
            for cp in first + passed:
                cp.wait_send()

    outs = [jax.ShapeDtypeStruct((4,) + a.shape, a.dtype) for a in arrs]
    return _comm_call(name, body, list(arrs), outs, 7 * n, 1)


def swap_other_half(arrs, name):
    n = len(arrs)

    def body(*refs):
        a_refs, o_refs, (ssem, rsem, _) = refs[:n], refs[n:2 * n], refs[2 * n:]
        x, y, c = _pos()
        cps = []
        for w, (a, o) in enumerate(zip(a_refs, o_refs)):
            H = a.shape[1] // 2
            cps.append(pltpu.make_async_remote_copy(a.at[:, pl.ds((1 - c) * H, H)], o, ssem.at[w], rsem.at[w],
                                                    device_id=(x, y, 1 - c), device_id_type=MESH))
            cps[-1].start()
        for cp in cps:
            cp.wait()

    outs = [jax.ShapeDtypeStruct((4, a.shape[1] // 2) + a.shape[2:], a.dtype) for a in arrs]
    return _comm_call(name, body, list(arrs), outs, n, 1)


def exchange_chips(arrs, name):
    n = len(arrs)

    def body(*refs):
        a_refs, o_refs, (ssem, rsem, lsem) = refs[:n], refs[n:2 * n], refs[2 * n:]
        x, y, c = _pos()
        k = 2 * x + y
        cps = []
        for w, (a, o) in enumerate(zip(a_refs, o_refs)):
            cps.append(pltpu.make_async_copy(a.at[k], o.at[k], lsem.at[w]))
            cps[-1].start()
            for m in range(1, 4):
                px, py = _flip(x, m & 2), _flip(y, m & 1)
                cps.append(pltpu.make_async_remote_copy(a.at[2 * px + py], o.at[k], ssem.at[3 * w + m - 1], rsem.at[3 * w + m - 1],
                                                        device_id=(px, py, c), device_id_type=MESH))
                cps[-1].start()
        for cp in cps:
            cp.wait()

    return _comm_call(name, body, list(arrs), [jax.ShapeDtypeStruct(a.shape, a.dtype) for a in arrs], 3 * n, n)


def share_halves(arrs, name):
    n = len(arrs)

    def body(*refs):
        o_refs, (ssem, rsem, _) = refs[n:2 * n], refs[2 * n:]
        x, y, c = _pos()
        cps = []
        for w, o in enumerate(o_refs):
            H = o.shape[0] // 2
            mine = o.at[pl.ds(c * H, H)]
            cps.append(pltpu.make_async_remote_copy(mine, mine, ssem.at[w], rsem.at[w], device_id=(x, y, 1 - c),
                                                    device_id_type=MESH))
            cps[-1].start()
        for cp in cps:
            cp.wait()

    outs = [jax.ShapeDtypeStruct(a.shape, a.dtype) for a in arrs]
    return _comm_call(name, body, list(arrs), outs, n, 1, aliases={i: i for i in range(n)})


def _adamw(w, g, m, v):
    m = ADAM_B1 * m + (1.0 - ADAM_B1) * g
    v = ADAM_B2 * v + (1.0 - ADAM_B2) * (g * g)
    m_hat = m / (1.0 - ADAM_B1 ** ADAM_STEP)
    v_hat = v / (1.0 - ADAM_B2 ** ADAM_STEP)
    return -ADAM_LR * (m_hat / (jnp.sqrt(v_hat) + ADAM_EPS) + ADAM_WD * w), m, v


def _rows_tile(R, C):
    return _pick(R, max(8, (2 << 20) // (4 * C) // 8 * 8), 8)


def chip_partial(gs, buf, cidx, name):
    _, L, R, C = gs.shape
    H, tr = L // 2, _rows_tile(R, C)

    def body(c_ref, g_ref, b_ref, o_ref):
        o_ref[...] = (g_ref[...] + b_ref[...]).astype(BF16)

    return pl.pallas_call(
        body, grid_spec=pltpu.PrefetchScalarGridSpec(
            num_scalar_prefetch=1, grid=(4, H, R // tr),
            in_specs=[pl.BlockSpec((None, None, tr, C), lambda s, l, r, c: (s, c[0] * H + l, r, 0)),
                      pl.BlockSpec((None, None, tr, C), lambda s, l, r, c: (s, l, r, 0))],
            out_specs=pl.BlockSpec((None, None, tr, C), lambda s, l, r, c: (s, l, r, 0))),
        out_shape=jax.ShapeDtypeStruct((4, H, R, C), BF16),
        compiler_params=_cparams(("parallel", "parallel", "parallel")), name=name)(cidx, gs, buf)


def sum_chips(parts, cidx, name):
    _, H, R, C = parts.shape
    tr = _rows_tile(R, C)

    def body(c_ref, p_ref, g_out):
        g = p_ref[0].astype(F32)
        for s in range(1, 4):
            g = g + p_ref[s].astype(F32)
        g_out[...] = g

    return pl.pallas_call(
        body, grid_spec=pltpu.PrefetchScalarGridSpec(
            num_scalar_prefetch=1, grid=(H, R // tr),
            in_specs=[pl.BlockSpec((4, None, tr, C), lambda l, r, c: (0, l, r, 0))],
            out_specs=pl.BlockSpec((None, tr, C), lambda l, r, c: (c[0] * H + l, r, 0))),
        out_shape=jax.ShapeDtypeStruct((2 * H, R, C), F32),
        compiler_params=_cparams(("parallel", "parallel")), name=name)(cidx, parts)


def sum8(arr, name):
    n = arr.shape[1]
    tr = _pick(n, 512, 8)

    def body(a_ref, o_ref):
        s = a_ref[0]
        for j in range(1, 8):
            s = s + a_ref[j]
        o_ref[...] = s

    return pl.pallas_call(
        body, grid=(n // tr,), in_specs=[pl.BlockSpec((8, tr, 128), lambda i: (0, i, 0))],
        out_specs=pl.BlockSpec((tr, 128), lambda i: (i, 0)), out_shape=jax.ShapeDtypeStruct((n, 128), F32),
        compiler_params=_cparams(("parallel",)), name=name)(arr)


def adam_rows(w, g, m, v, name, emit_g=False):
    n, C = w.shape
    tr = _rows_tile(n, C)

    def body(w_ref, g_ref, m_ref, v_ref, *outs):
        d_out, m_out, v_out = outs[-3:]
        d_out[...], m_out[...], v_out[...] = _adamw(w_ref[...], g_ref[...], m_ref[...], v_ref[...])
        if emit_g:
            outs[0][...] = g_ref[...]

    spec = pl.BlockSpec((tr, C), lambda i: (i, 0))
    n_out = 4 if emit_g else 3
    return pl.pallas_call(
        body, grid=(n // tr,), in_specs=[spec] * 4, out_specs=[spec] * n_out,
        out_shape=[jax.ShapeDtypeStruct((n, C), F32)] * n_out, compiler_params=_cparams(("parallel",)), name=name)(w, g, m, v)


def _silu(c):
    return c * jax.nn.sigmoid(c)


def ada_fwd(c_all, w, b, name):
    NC, D = c_all.shape
    L, _, Wc = w.shape
    tn = _pick(Wc, 512, 128)

    def body(c_ref, w_ref, b_ref, o_ref):
        o_ref[...] = bdot(_silu(c_ref[...]), w_ref[...]) + b_ref[...]

    return pl.pallas_call(
        body, grid=(L, Wc // tn),
        in_specs=[pl.BlockSpec((NC, D), lambda l, j: (0, 0)), pl.BlockSpec((None, D, tn), lambda l, j: (l, 0, j)),
                  pl.BlockSpec((None, 1, tn), lambda l, j: (l, 0, j))],
        out_specs=pl.BlockSpec((None, NC, tn), lambda l, j: (l, 0, j)), out_shape=jax.ShapeDtypeStruct((L, NC, Wc), F32),
        compiler_params=_cparams(("parallel", "parallel")), name=name)(c_all, w, b)


def ada_bwd(c_all, w, dmod, name):
    NC, D = c_all.shape
    L, _, Wc = w.shape
    tn = _pick(Wc, 512, 128)

    def body(c_ref, w_ref, d_ref, dw_ref, db_ref, dc_ref):
        first = jnp.logical_and(pl.program_id(0) == 0, pl.program_id(1) == 0)
        f = lambda cs, ww: bdot(cs, ww)
        _, vjp = jax.vjp(f, _silu(c_ref[...]), w_ref[...])
        dcs, dw = vjp(d_ref[...])
        dw_ref[...] = dw
        db_ref[...] = jnp.sum(d_ref[...], axis=0, keepdims=True)
        _acc(dc_ref, dcs, first)

    return pl.pallas_call(
        body, grid=(L, Wc // tn),
        in_specs=[pl.BlockSpec((NC, D), lambda l, j: (0, 0)), pl.BlockSpec((None, D, tn), lambda l, j: (l, 0, j)),
                  pl.BlockSpec((None, NC, tn), lambda l, j: (l, 0, j))],
        out_specs=[pl.BlockSpec((None, D, tn), lambda l, j: (l, 0, j)), pl.BlockSpec((None, 1, tn), lambda l, j: (l, 0, j)),
                   pl.BlockSpec((NC, D), lambda l, j: (0, 0))],
        out_shape=[jax.ShapeDtypeStruct((L, D, Wc), F32), jax.ShapeDtypeStruct((L, 1, Wc), F32),
                   jax.ShapeDtypeStruct((NC, D), F32)],
        compiler_params=_cparams(("arbitrary", "arbitrary")), name=name)(c_all, w, dmod)


def cctx_grad(dcs_twice, c_ctx, name):
    def body(d_ref, c_ref, o_ref):
        _, vjp = jax.vjp(_silu, c_ref[...])
        o_ref[...] = vjp(0.5 * d_ref[...])[0]

    D = c_ctx.shape[1]
    return pl.pallas_call(body, out_shape=jax.ShapeDtypeStruct((8, D), F32), name=name)(dcs_twice, c_ctx)


def _rope_tables(SEQ):
    t = jnp.arange(SEQ)
    row, col = (t // GRID_W).astype(F32), (t % GRID_W).astype(F32)
    n_freq = 16
    freqs = ROPE_THETA ** (-jnp.arange(n_freq, dtype=F32) / n_freq)
    ang = jnp.concatenate([row[:, None] * freqs, col[:, None] * freqs], axis=-1)
    cos, sin = jnp.cos(ang), jnp.sin(ang)
    c = jnp.tile(jnp.concatenate([cos, cos], axis=1), (1, 4))
    s = jnp.tile(jnp.concatenate([-sin, sin], axis=1), (1, 4))
    return (jnp.concatenate([c, jnp.ones((T, 256), F32)], axis=0), jnp.concatenate([s, jnp.zeros((T, 256), F32)], axis=0))


def _unshard_cols(g):
    return jnp.transpose(g, (1, 2, 0, 3)).reshape(g.shape[1], g.shape[2], 4 * g.shape[3])


def _unshard_rows(g):
    return jnp.transpose(g, (1, 0, 2, 3)).reshape(g.shape[1], 4 * g.shape[2], g.shape[3])


def _shard_cols(w, n=4):
    L, R, C = w.shape
    return jnp.transpose(w.reshape(L, R, n, C // n), (2, 0, 1, 3))


def _shard_rows(w):
    L, R, C = w.shape
    return jnp.transpose(w.reshape(L, 4, R // 4, C), (1, 0, 2, 3))


def _ab_in_permute(w):
    L, D, _ = w.shape
    return jnp.concatenate([w[..., 0:256], w[..., 320:1600], w[..., 256:320], jnp.zeros((L, D, 64), w.dtype)], axis=-1)


def _ab_in_unpermute(g):
    return jnp.concatenate([g[..., 0:256], g[..., 1536:1600], g[..., 256:1536]], axis=-1)


def _split_heads(w, a):
    L, K, N = w.shape
    w4 = w.reshape(L, K, 4, N // 4)
    return w4[..., :a].reshape(L, K, 4 * a), w4[..., a:].reshape(L, K, N - 4 * a)


def _join_heads(p, q):
    L, K = p.shape[:2]
    return jnp.concatenate([p.reshape(L, K, 4, -1), q.reshape(L, K, 4, -1)], axis=-1).reshape(L, K, -1)


def _pack(arrs):
    parts = []
    for a in arrs:
        f = a.reshape(-1).astype(F32)
        parts.append(jnp.pad(f, (0, (-f.shape[0]) % 1024)))
    return jnp.concatenate(parts).reshape(-1, 128)


def _unpack(buf, like):
    out, r0 = [], 0
    for a in like:
        n = math.prod(a.shape)
        rows = (n + (-n) % 1024) // 128
        out.append(buf[r0:r0 + rows].reshape(-1)[:n].reshape(a.shape))
        r0 += rows
    return out


_SMALL = ("c_ctx", "ada_b", "norm_mix", "norm_ffn", "norm_final", "mla_q_norm", "mla_kv_norm", "cmlp_v_norm", "cmlp_ws",
          "cmlp_bs", "ret_decay_fwd", "ret_decay_bwd", "ret_norm", "swa_sink")
_BIG = ("ffn_in", "ffn_out", "ab_in", "ab_out", "mla_wq_b", "mla_wkv_b", "cd_in", "cd_out")
_WEIGHTS = ("c_ctx", "ada_w", "ada_b", "norm_mix", "norm_ffn", "norm_final", "ffn_in", "ffn_out", "ab_in", "ab_out",
            "mla_q_norm", "mla_kv_norm", "mla_wq_b", "mla_wkv_b", "cmlp_v_norm", "cmlp_ws", "cmlp_bs", "cd_in", "cd_out",
            "ret_decay_fwd", "ret_decay_bwd", "ret_norm", "swa_sink")


def kernel(x, c, ctx, c_ctx, ada_w, ada_b, norm_mix, norm_ffn, norm_final, ffn_in, ffn_out, ab_in, ab_out, mla_q_norm, mla_kv_norm, mla_wq_b, mla_wkv_b, cmlp_v_norm, cmlp_ws, cmlp_bs, cd_in, cd_out, ret_decay_fwd, ret_decay_bwd, ret_norm, swa_sink, loss_target, m_c_ctx, m_ada_w, m_ada_b, m_norm_mix, m_norm_ffn, m_norm_final, m_ffn_in, m_ffn_out, m_ab_in, m_ab_out, m_mla_q_norm, m_mla_kv_norm, m_mla_wq_b, m_mla_wkv_b, m_cmlp_v_norm, m_cmlp_ws, m_cmlp_bs, m_cd_in, m_cd_out, m_ret_decay_fwd, m_ret_decay_bwd, m_ret_norm, m_swa_sink, v_c_ctx, v_ada_w, v_ada_b, v_norm_mix, v_norm_ffn, v_norm_final, v_ffn_in, v_ffn_out, v_ab_in, v_ab_out, v_mla_q_norm, v_mla_kv_norm, v_mla_wq_b, v_mla_wkv_b, v_cmlp_v_norm, v_cmlp_ws, v_cmlp_bs, v_cd_in, v_cd_out, v_ret_decay_fwd, v_ret_decay_bwd, v_ret_norm, v_swa_sink):
    W = dict(c_ctx=c_ctx, ada_w=ada_w, ada_b=ada_b, norm_mix=norm_mix, norm_ffn=norm_ffn, norm_final=norm_final, ffn_in=ffn_in, ffn_out=ffn_out, ab_in=ab_in, ab_out=ab_out, mla_q_norm=mla_q_norm, mla_kv_norm=mla_kv_norm, mla_wq_b=mla_wq_b, mla_wkv_b=mla_wkv_b, cmlp_v_norm=cmlp_v_norm, cmlp_ws=cmlp_ws, cmlp_bs=cmlp_bs, cd_in=cd_in, cd_out=cd_out, ret_decay_fwd=ret_decay_fwd, ret_decay_bwd=ret_decay_bwd, ret_norm=ret_norm, swa_sink=swa_sink)
    M1 = dict(c_ctx=m_c_ctx, ada_w=m_ada_w, ada_b=m_ada_b, norm_mix=m_norm_mix, norm_ffn=m_norm_ffn, norm_final=m_norm_final, ffn_in=m_ffn_in, ffn_out=m_ffn_out, ab_in=m_ab_in, ab_out=m_ab_out, mla_q_norm=m_mla_q_norm, mla_kv_norm=m_mla_kv_norm, mla_wq_b=m_mla_wq_b, mla_wkv_b=m_mla_wkv_b, cmlp_v_norm=m_cmlp_v_norm, cmlp_ws=m_cmlp_ws, cmlp_bs=m_cmlp_bs, cd_in=m_cd_in, cd_out=m_cd_out, ret_decay_fwd=m_ret_decay_fwd, ret_decay_bwd=m_ret_decay_bwd, ret_norm=m_ret_norm, swa_sink=m_swa_sink)
    M2 = dict(c_ctx=v_c_ctx, ada_w=v_ada_w, ada_b=v_ada_b, norm_mix=v_norm_mix, norm_ffn=v_norm_ffn, norm_final=v_norm_final, ffn_in=v_ffn_in, ffn_out=v_ffn_out, ab_in=v_ab_in, ab_out=v_ab_out, mla_q_norm=v_mla_q_norm, mla_kv_norm=v_mla_kv_norm, mla_wq_b=v_mla_wq_b, mla_wkv_b=v_mla_wkv_b, cmlp_v_norm=v_cmlp_v_norm, cmlp_ws=v_cmlp_ws, cmlp_bs=v_cmlp_bs, cd_in=v_cd_in, cd_out=v_cd_out, ret_decay_fwd=v_ret_decay_fwd, ret_decay_bwd=v_ret_decay_bwd, ret_norm=v_ret_norm, swa_sink=v_swa_sink)

    B, SEQ, D = x.shape
    CTX = ctx.shape[1]
    lay = Layout(B, SEQ, CTX, D)
    nt, NX = lay.nt, lay.NX
    ix, iy, ic = lax.axis_index("x"), lax.axis_index("y"), lax.axis_index("c")
    chip, me = 2 * ix + iy, 4 * ix + 2 * iy + ic
    cidx = jnp.reshape(ic, (1,)).astype(jnp.int32)
    Wc = ada_w.shape[2]
    n_even, n_odd = ab_in.shape[0], cd_in.shape[0]

    rn_row = jnp.pad(ret_norm.reshape(1, -1), ((0, 0), (0, D - ret_norm.size)))
    pack0 = jnp.concatenate([c, rn_row, jnp.zeros((8 - (B + 1) % 8, D), F32)], axis=0) if (B + 1) % 8 else jnp.concatenate([c, rn_row], axis=0)
    g0 = gather8(pack0, "gather_cond")
    NC = -(-(8 * B + 1) // 16) * 16
    c_all = jnp.concatenate([g0[:, :B].reshape(8 * B, D), c_ctx[None], jnp.zeros((NC - 8 * B - 1, D), F32)], axis=0)
    rn_sh = ret_norm.shape[1]
    ret_norm_full = jnp.transpose(g0[0::2, B, :ret_norm.size].reshape(4, n_odd, rn_sh), (1, 0, 2)).reshape(n_odd, 4 * rn_sh)

    gw = gather_weights([W[n].astype(BF16) for n in _BIG], "gather_weights")
    w_ab_in, w_wq, w_wkv, w_cd_in = (_unshard_cols(gw[i]) for i in (2, 4, 5, 6))
    w_ffn_out, w_ab_out, w_cd_out = (_unshard_rows(gw[i]) for i in (1, 3, 7))
    w_ab_in = _ab_in_permute(w_ab_in)
    w_ffn_out_t = jnp.transpose(w_ffn_out, (0, 2, 1))
    w_qn, w_qp = _split_heads(w_wq, MLA_NOPE)
    w_k, w_v = _split_heads(w_wkv, MLA_NOPE)

    ab_sh = lax.dynamic_slice_in_dim(ada_b, chip * Wc, Wc, axis=1)[:, None, :]
    mod_sh = ada_fwd(c_all, ada_w, ab_sh, "ada_fwd")
    mod_all = _unshard_cols(gather_chips(mod_sh, "gather_mod"))
    mod_mine = jnp.concatenate([lax.dynamic_slice_in_dim(mod_all, me * B, B, axis=1), mod_all[:, 8 * B:8 * B + 1]], axis=1)
    mod = mod_mine.reshape(DEPTH, B + 1, 6, D)

    tabc, tabs = _rope_tables(SEQ)
    bc8 = lambda a: jnp.broadcast_to(a.reshape(a.shape + (1, 1)), a.shape + (8, 128))
    row = lambda a: a.reshape(1, -1)

    S = jnp.concatenate([x.reshape(NX, D), ctx.reshape(B * CTX, D)], axis=0)
    saved = []
    xn = norm_mod_fwd(S, row(norm_mix[0]), mod[0], 0, lay, nt, "norm_mix_fwd0")
    for l in range(DEPTH):
        j, even = l // 2, l % 2 == 0
        if even:
            z = mm(xn, w_ab_in[j], name=f"ab_in{l}")
            q, k, v = ab_prep_fwd(z, tabc, tabs, row(mla_kv_norm[j]), row(mla_q_norm[j]), w_k[j], w_v[j], w_qn[j], w_qp[j],
                                  lay, f"ab_prep{l}")
            o = mla_fwd(q, k, v, lay, f"mla{l}")
            merged = cmlp_merge_fwd(z, o, row(cmlp_v_norm[j]), cmlp_ws[j], cmlp_bs[j][:, :, None], lay, f"cmlp{l}")
            w_out, mix = w_ab_out[j], (q, k, v)
        else:
            z = mm(xn, w_cd_in[j], name=f"cd_in{l}")
            rq, rk, sq, ke, ve = cd_prep_fwd(z, tabc, tabs, lay, f"cd_prep{l}")
            decf, decb, sink = bc8(ret_decay_fwd[j]), bc8(ret_decay_bwd[j]), bc8(swa_sink[j].reshape(SWA_KV_HEADS, SWA_GROUPS))
            yret = ret_fwd(rq, rk, z, decf, decb, lay, f"ret{l}")
            osw = swa_fwd(sq, ke, ve, sink, lay, f"swa{l}")
            merged = cd_merge_fwd(yret, z, osw, row(ret_norm_full[j]), lay, f"cd_merge{l}")
            w_out, mix = w_cd_out[j], (rq, rk, sq, ke, ve, decf, decb, sink, yret)
        y, S_mid, xn2 = mm_gated(merged, w_out, S, mod[l], 2, lay, name=f"mix_out{l}", n_tiles=nt,
                                 norm=(row(norm_ffn[l]), mod[l], 3))
        act, fa, fb = ffn_in_act(xn2, gw[0], l, f"ffn_in{l}")
        nxt_norm = (row(norm_mix[l + 1]), mod[l + 1], 0) if l + 1 < DEPTH else None
        f, S_new, *xn_next = mm_gated(act, w_ffn_out[l], S_mid, mod[l], 5, lay, name=f"ffn_out{l}", n_tiles=nt, norm=nxt_norm)
        saved.append((S, xn, z, mix, merged, w_out, y, S_mid, xn2, (fa, fb), act, f))
        S, xn = S_new, (xn_next[0] if xn_next else None)

    loss_blk, dS, d_norm_final = loss_head(S, row(norm_final), loss_target.reshape(NX, D), lay, "loss_head")
    loss = lax.psum(loss_blk[0, 0], ("x", "y", "c"))

    G = {n: [None] * W[n].shape[0] for n in ("norm_mix", "norm_ffn", "mla_q_norm", "mla_kv_norm", "cmlp_v_norm", "cmlp_ws",
                                             "cmlp_bs", "ret_decay_fwd", "ret_decay_bwd", "ret_norm", "swa_sink")}
    GB = {n: [None] * cnt for n, cnt in (("ffn_in_a", DEPTH), ("ffn_in_b", DEPTH), ("ffn_out", DEPTH), ("ab_in", n_even), ("ab_out", n_even),
                                         ("wqn", n_even), ("wqp", n_even), ("wk", n_even), ("wv", n_even), ("cd_in", n_odd),
                                         ("cd_out", n_odd))}
    dmod = [None] * DEPTH
    df, dgate2 = gate_bwd(dS, saved[-1][-1], mod[DEPTH - 1], 5, lay, nt, f"ffn_gate_bwd{DEPTH - 1}")
    for l in reversed(range(DEPTH)):
        j, even = l // 2, l % 2 == 0
        S_in, xn, z, mix, merged, w_out, y, S_mid, xn2, (fa, fb), act, f = saved[l]
        da, db = ffn_out_dx_act(df, w_ffn_out_t[l], fa, fb, f"ffn_out_dx{l}")
        GB["ffn_out"][l] = mm(act, df, ta=True, name=f"ffn_out_dw{l}")
        GB["ffn_in_a"][l] = mm(xn2, da, ta=True, name=f"ffn_in_dwa{l}", split=2)
        GB["ffn_in_b"][l] = mm(xn2, db, ta=True, name=f"ffn_in_dwb{l}", split=2)
        dxn2 = ffn_in_dx(da, db, gw[0], l, f"ffn_in_dx{l}")
        dS_mid, dss2, dg, dy, dgate1 = norm_mod_bwd(S_mid, row(norm_ffn[l]), mod[l], 3, dxn2, dS, lay, nt, f"norm_ffn_bwd{l}",
                                                    gate=(y, mod[l], 2))
        G["norm_ffn"][l] = dg
        dmerged = mm(dy, w_out, tb=True, name=f"mix_out_dx{l}")
        d_w_out = mm(merged, dy, ta=True, name=f"mix_out_dw{l}")
        if even:
            q, k, v = mix
            dq, dkx, dkh, dvx, dvh = mla_bwd(q, k, v, dmerged, lay, f"mla_bwd{l}")
            (dz, dgkv, dgq, dwk, dwv, dwqn, dwqp, dgvn, dws, dbs) = ab_rows_bwd(
                z, tabc, tabs, dq, dkx, dkh, dvx, dvh, dmerged, row(mla_kv_norm[j]), row(mla_q_norm[j]), w_k[j], w_v[j],
                w_qn[j], w_qp[j], row(cmlp_v_norm[j]), cmlp_ws[j], cmlp_bs[j][:, :, None], lay, f"ab_rows_bwd{l}")
            G["mla_kv_norm"][j], G["mla_q_norm"][j], G["cmlp_v_norm"][j] = dgkv, dgq, dgvn
            G["cmlp_ws"][j], G["cmlp_bs"][j] = dws, dbs
            GB["wk"][j], GB["wv"][j], GB["wqn"][j], GB["wqp"][j], GB["ab_out"][j] = dwk, dwv, dwqn, dwqp, d_w_out
            w_in = w_ab_in[j]
        else:
            rq, rk, sq, ke, ve, decf, decb, sink, yret = mix
            dyret, drg, dgn = cd_merge_bwd(yret, z, row(ret_norm_full[j]), dmerged, lay, f"cd_merge_bwd{l}")
            dqx, dqh, dkx, dkh, dvx, dvh, ddf, ddb = ret_bwd(rq, rk, z, decf, decb, dyret, lay, f"ret_bwd{l}")
            dsq, dkex, dkeh, dvex, dveh, dsink = swa_bwd(sq, ke, ve, sink, dmerged, lay, f"swa_bwd{l}")
            dz = cd_rows_bwd(z, tabc, tabs, dsq, drg, (dqx, dqh), (dkx, dkh), (dkex, dkeh), (dvex, dveh), (dvx, dvh), lay,
                             f"cd_rows_bwd{l}")
            G["ret_norm"][j], G["ret_decay_fwd"][j], G["ret_decay_bwd"][j] = dgn, ddf[:, 0, 0], ddb[:, 0, 0]
            G["swa_sink"][j] = dsink[:, :, 0, 0].reshape(-1)
            GB["cd_out"][j] = d_w_out
            w_in = w_cd_in[j]
        GB["ab_in" if even else "cd_in"][j] = mm(xn, dz, ta=True, name=f"mix_in_dw{l}")
        dxn = mm(dz, w_in, tb=True, name=f"mix_in_dx{l}")
        dmod_l = lambda dss1: jnp.concatenate([dss1, dgate1, dss2, dgate2], axis=1)
        if l > 0:
            dS, dss1, dg, df, dgate2_prev = norm_mod_bwd(S_in, row(norm_mix[l]), mod[l], 0, dxn, dS_mid, lay, nt,
                                                         f"norm_mix_bwd{l}", gate=(saved[l - 1][-1], mod[l - 1], 5))
            dmod[l], dgate2 = dmod_l(dss1), dgate2_prev
        else:
            dS, dss1, dg = norm_mod_bwd(S_in, row(norm_mix[l]), mod[l], 0, dxn, dS_mid, lay, nt, f"norm_mix_bwd{l}")
            dmod[l] = dmod_l(dss1)
        G["norm_mix"][l] = dg
    grad_x = dS[:NX].reshape(B, SEQ, D)

    st = lambda n: jnp.stack([g.reshape((4 * rn_sh,) if n == "ret_norm" else W[n].shape[1:]) for g in G[n]])
    small_parts = {n: st(n) for n in G}
    small_parts["norm_final"] = d_norm_final.reshape(-1)
    dmod_local = jnp.stack(dmod).reshape(DEPTH, B + 1, 6 * D)
    names1 = ["norm_mix", "norm_ffn", "norm_final", "mla_q_norm", "mla_kv_norm", "cmlp_v_norm", "cmlp_ws", "cmlp_bs",
              "ret_decay_fwd", "ret_decay_bwd", "ret_norm", "swa_sink"]
    like1 = [dmod_local] + [small_parts[n] for n in names1]
    g1 = gather8(_pack(like1), "gather_small_grads")
    tot1 = _unpack(sum8(g1, "sum_small_grads"), like1)
    sg = dict(zip(names1, tot1[1:]))
    n_dm = math.prod(dmod_local.shape)
    dm_each = g1[:, :-(-n_dm // 128)].reshape(8, -1)[:, :n_dm].reshape(8, DEPTH, B + 1, 6 * D)
    dmod_all = jnp.concatenate([jnp.transpose(dm_each[:, :, :B], (1, 0, 2, 3)).reshape(DEPTH, 8 * B, 6 * D),
                                tot1[0][:, B:B + 1], jnp.zeros((DEPTH, NC - 8 * B - 1, 6 * D), F32)], axis=1)
    dmod_sh = lax.dynamic_slice_in_dim(dmod_all, chip * Wc, Wc, axis=2)
    g_ada_w, g_ada_b_sh, dcs = ada_bwd(c_all, ada_w, dmod_sh, "ada_bwd")
    like2 = [dcs[8 * B], g_ada_b_sh]
    g2 = gather8(_pack(like2), "gather_ada_grads")
    tot2 = _unpack(sum8(g2, "sum_ada_grads"), like2)
    bc = lambda a: jnp.broadcast_to(a.reshape(1, D), (8, D))
    sg["c_ctx"] = cctx_grad(bc(tot2[0]), bc(c_ctx), "c_ctx_grad")[0]
    off = D + (-D) % 1024
    gab = g2[0::2, off // 128:(off + DEPTH * Wc) // 128].reshape(4, DEPTH, Wc)
    sg["ada_b"] = jnp.transpose(gab, (1, 0, 2)).reshape(DEPTH, 4 * Wc)
    sg["ret_norm"] = lax.dynamic_slice_in_dim(sg["ret_norm"], chip * rn_sh, rn_sh, axis=1)

    stack = lambda n: jnp.stack(GB[n])
    gs = {"ffn_in": jnp.concatenate([jnp.stack(GB[n], axis=1) for n in ("ffn_in_a", "ffn_in_b")], axis=0),
          "ffn_out": _shard_rows(stack("ffn_out")),
          "ab_in": _shard_cols(_ab_in_unpermute(stack("ab_in"))), "ab_out": _shard_rows(stack("ab_out")),
          "mla_wq_b": _shard_cols(_join_heads(stack("wqn"), stack("wqp"))),
          "mla_wkv_b": _shard_cols(_join_heads(stack("wk"), stack("wv"))),
          "cd_in": _shard_cols(stack("cd_in")), "cd_out": _shard_rows(stack("cd_out"))}
    bufs = swap_other_half([gs[n] for n in _BIG], "swap_core_halves")
    parts = [chip_partial(gs[n], b, cidx, f"chip_partial_{n}") for n, b in zip(_BIG, bufs)]
    arrived = exchange_chips(parts, "exchange_chips")
    grads = share_halves([sum_chips(p, cidx, f"sum_chips_{n}") for n, p in zip(_BIG, arrived)], "share_core_halves")
    flat2 = lambda a: a.reshape(-1, a.shape[-1])
    out = {}
    for n, g in zip(_BIG, grads):
        res = adam_rows(flat2(W[n]), flat2(g), flat2(M1[n]), flat2(M2[n]), f"adam_{n}", emit_g=True)
        out[n] = tuple(r.reshape(W[n].shape) for r in res)

    like_s = [W[n] for n in _SMALL]
    dsm, msm, vsm = adam_rows(_pack(like_s), _pack([sg[n].reshape(W[n].shape) for n in _SMALL]), _pack([M1[n] for n in _SMALL]),
                                _pack([M2[n] for n in _SMALL]), "adam_small")
    for n, d_, m_, v_ in zip(_SMALL, _unpack(dsm, like_s), _unpack(msm, like_s), _unpack(vsm, like_s)):
        out[n] = (sg[n].reshape(W[n].shape), d_, m_, v_)
    d_, m_, v_ = adam_rows(flat2(ada_w), flat2(g_ada_w), flat2(m_ada_w), flat2(v_ada_w), "adam_ada_w")
    out["ada_w"] = (g_ada_w, d_.reshape(ada_w.shape), m_.reshape(ada_w.shape), v_.reshape(ada_w.shape))

    return (loss, grad_x, *[out[n][0] for n in _WEIGHTS], *[out[n][1] for n in _WEIGHTS], *[out[n][2] for n in _WEIGHTS],
            *[out[n][3] for n in _WEIGHTS])
```
